```python
import math
import jax
import jax.numpy as jnp
from jax import lax
import numpy as np

D_MODEL = 1024
BATCH = 8
SEQ = 8192
DEPTH = 2

N_MIXERS = 4
GROUP_WIDTH = D_MODEL // N_MIXERS
HEAD_DIM = 64
GROUP_HEADS = GROUP_WIDTH // HEAD_DIM
D_FF = 2816
SHORT_CONV = 4
CONF_KERNEL = 31
CONF_GROUPS = 4
GDN_CHUNK = 64
Q_BLOCK = 128
N_MEM = 256
MEM_HEADS = 4
MEM_HEAD_DIM = D_MODEL // MEM_HEADS
DN_ALPHA = float((2 * DEPTH) ** 0.25)
DN_INIT = float((8 * DEPTH) ** -0.25)
LN_EPS = 1e-5
RMS_EPS = 1e-6
L2_EPS = 1e-6
NEG_BIG = -1e30
IN_SPLITS = (3 * GROUP_WIDTH,
             GROUP_WIDTH,
             GROUP_HEADS,
             GROUP_HEADS,
             3 * GROUP_WIDTH,
             GROUP_HEADS,
             2 * GROUP_WIDTH,
             3 * GROUP_WIDTH)
IN_WIDTH = sum(IN_SPLITS)
IN_OFFSETS = tuple(int(o) for o in np.cumsum(IN_SPLITS)[:-1])

kernel_name = 'hybrid_headgroup_gdn_fox_conv_stickbreak'


def layer_norm(x, g, b):
    xf = x.astype(jnp.float32)
    mu = jnp.mean(xf, axis=-1, keepdims=True)
    var = jnp.mean(jnp.square(xf - mu), axis=-1, keepdims=True)
    y = (xf - mu) * lax.rsqrt(var + LN_EPS) * g.astype(jnp.float32) + b.astype(jnp.float32)
    return y.astype(x.dtype)


def swiglu(x, w_gate, w_up, w_down):
    return (jax.nn.silu(x @ w_gate) * (x @ w_up)) @ w_down


def causal_depthwise_conv(x, w):
    width, ch = w.shape
    xp = jnp.pad(x, ((0, 0), (width - 1, 0), (0, 0)))
    return lax.conv_general_dilated(xp, w[:, None, :].astype(x.dtype), window_strides=(1,), padding='VALID',
                                    dimension_numbers=('NWC', 'WIO', 'NWC'), feature_group_count=ch)


def to_heads(t):
    b, s, _ = t.shape
    return t.reshape(b, s, GROUP_HEADS, HEAD_DIM).transpose(0, 2, 1, 3)


def from_heads(t):
    b, h, s, d = t.shape
    return t.transpose(0, 2, 1, 3).reshape(b, s, h * d)


def l2_normalize(t):
    tf = t.astype(jnp.float32)
    return tf * lax.rsqrt(jnp.sum(tf * tf, axis=-1, keepdims=True) + L2_EPS)


def query_blocks(t):
    b, h, s = t.shape[:3]
    return jnp.moveaxis(t.reshape(b, h, s // Q_BLOCK, Q_BLOCK, *t.shape[3:]), 2, 0)


def merge_blocks(t):
    nb, b, h, qb, d = t.shape
    return jnp.moveaxis(t, 0, 2).reshape(b, h, nb * qb, d)


def gated_delta_rule_chunked(q, k, v, g, beta):
    f32 = jnp.float32
    b, h, s, dk = q.shape
    dv = v.shape[-1]
    c = GDN_CHUNK
    n = s // c
    q = q.astype(f32).reshape(b, h, n, c, dk) * (dk ** -0.5)
    k = k.astype(f32).reshape(b, h, n, c, dk)
    v = v.astype(f32).reshape(b, h, n, c, dv)
    beta = beta.astype(f32).reshape(b, h, n, c)
    g = jnp.cumsum(g.astype(f32).reshape(b, h, n, c), axis=-1)
    k_beta = k * beta[..., None]
    v_beta = v * beta[..., None]
    lower_incl = jnp.tril(jnp.ones((c, c), dtype=bool))
    strict_lower = jnp.tril(jnp.ones((c, c), dtype=bool), -1)
    decay = jnp.exp(jnp.where(lower_incl, g[..., :, None] - g[..., None, :], -jnp.inf))
    lkk = jnp.where(strict_lower, jnp.einsum('bhncd,bhnsd->bhncs', k_beta, k) * decay, 0.0)
    eye = jnp.eye(c, dtype=f32)
    t_inv = lax.linalg.triangular_solve(eye + lkk, jnp.broadcast_to(eye, lkk.shape), left_side=True, lower=True)
    u = jnp.einsum('bhncs,bhnsv->bhncv', t_inv, v_beta)
    w = jnp.einsum('bhncs,bhnsd->bhncd', t_inv, k_beta * jnp.exp(g)[..., None])
    a_qk = jnp.einsum('bhncd,bhnsd->bhncs', q, k) * decay
    g_last = g[..., -1]
    q_dec = q * jnp.exp(g)[..., None]
    k_dec = k * jnp.exp(g_last[..., None] - g)[..., None]

    def chunk_step(state, inp):
        q_c, k_c, u_c, w_c, a_c, gl_c = inp
        v_new = u_c - jnp.einsum('bhcd,bhdv->bhcv', w_c, state)
        o_c = jnp.einsum('bhcd,bhdv->bhcv', q_c, state) + jnp.einsum('bhcs,bhsv->bhcv', a_c, v_new)
        state = state * jnp.exp(gl_c)[..., None, None] + jnp.einsum('bhcd,bhcv->bhdv', k_c, v_new)
        return state, o_c

    xs = tuple(jnp.moveaxis(t, 2, 0) for t in (q_dec, k_dec, u, w, a_qk, g_last))
    state0 = jnp.zeros((b, h, dk, dv), f32)
    _, o = lax.scan(chunk_step, state0, xs)
    return jnp.moveaxis(o, 0, 2).reshape(b, h, s, dv)


def forgetting_attention(q, k, v, log_f):
    s_len, d = q.shape[2], q.shape[3]
    scale = d ** -0.5
    cum = jnp.cumsum(log_f.astype(jnp.float32), axis=-1)
    key_pos = jnp.arange(s_len)
    starts = jnp.arange(s_len // Q_BLOCK) * Q_BLOCK

    def block(args):
        q_blk, cum_blk, start = args
        logits = jnp.einsum('bhqd,bhkd->bhqk', q_blk, k).astype(jnp.float32) * scale
        logits = logits + cum_blk[..., :, None] - cum[..., None, :]
        causal = key_pos[None, :] <= (start + jnp.arange(Q_BLOCK))[:, None]
        probs = jax.nn.softmax(jnp.where(causal, logits, NEG_BIG), axis=-1)
        return jnp.einsum('bhqk,bhkd->bhqd', probs.astype(v.dtype), v)

    return merge_blocks(lax.map(block, (query_blocks(q), query_blocks(cum), starts)))


def stick_breaking_attention(q, k, v):
    s_len, d = q.shape[2], q.shape[3]
    scale = d ** -0.5
    key_pos = jnp.arange(s_len)
    starts = jnp.arange(s_len // Q_BLOCK) * Q_BLOCK

    def block(args):
        q_blk, start = args
        z = jnp.einsum('bhqd,bhkd->bhqk', q_blk, k).astype(jnp.float32) * scale
        strict = key_pos[None, :] < (start + jnp.arange(Q_BLOCK))[:, None]
        log_keep = jnp.where(strict, jax.nn.log_sigmoid(-z), 0.0)
        log_rest = lax.cumsum(log_keep, axis=3, reverse=True) - log_keep
        weights = jnp.where(strict, jnp.exp(jax.nn.log_sigmoid(z) + log_rest), 0.0)
        return jnp.einsum('bhqk,bhkd->bhqd', weights.astype(v.dtype), v)

    return merge_blocks(lax.map(block, (query_blocks(q), starts)))


def channel_group_norm(h, g, b):
    bsz, s, ch = h.shape
    hf = h.astype(jnp.float32).reshape(bsz, s, CONF_GROUPS, ch // CONF_GROUPS)
    mu = jnp.mean(hf, axis=-1, keepdims=True)
    var = jnp.mean(jnp.square(hf - mu), axis=-1, keepdims=True)
    hn = ((hf - mu) * lax.rsqrt(var + LN_EPS)).reshape(bsz, s, ch)
    return (hn * g.astype(jnp.float32) + b.astype(jnp.float32)).astype(h.dtype)


def parallel_head_group_mixers(x, w_in, gdn_conv_w, gdn_a_log, gdn_dt_bias, gdn_norm_g, fox_b_f,
                               conf_dw_w, conf_dw_b, conf_norm_g, conf_norm_b, w_out):
    f32 = jnp.float32
    proj = x @ w_in
    gdn_qkv, gdn_z, gdn_a, gdn_b, fox_qkv, fox_f, conf_glu, sb_qkv = jnp.split(proj, IN_OFFSETS, axis=-1)

    qkv = jax.nn.silu(causal_depthwise_conv(gdn_qkv, gdn_conv_w))
    q_a, k_a, v_a = (to_heads(t) for t in jnp.split(qkv, 3, axis=-1))
    beta = jax.nn.sigmoid(gdn_b.astype(f32)).transpose(0, 2, 1)
    log_decay = (-jnp.exp(gdn_a_log.astype(f32)) *
                 jax.nn.softplus(gdn_a.astype(f32) + gdn_dt_bias.astype(f32))).transpose(0, 2, 1)
    o_a = gated_delta_rule_chunked(l2_normalize(q_a), l2_normalize(k_a), v_a, log_decay, beta)
    o_a = o_a * lax.rsqrt(jnp.mean(o_a * o_a, axis=-1, keepdims=True) + RMS_EPS) * gdn_norm_g.astype(f32)
    y_a = (from_heads(o_a) * jax.nn.silu(gdn_z.astype(f32))).astype(x.dtype)

    q_b, k_b, v_b = (to_heads(t) for t in jnp.split(fox_qkv, 3, axis=-1))
    log_f = jax.nn.log_sigmoid(fox_f.astype(f32) + fox_b_f.astype(f32)).transpose(0, 2, 1)
    y_b = from_heads(forgetting_attention(q_b, k_b, v_b, log_f)).astype(x.dtype)

    val, gate = jnp.split(conf_glu, 2, axis=-1)
    c = causal_depthwise_conv(val * jax.nn.sigmoid(gate), conf_dw_w) + conf_dw_b
    y_c = jax.nn.silu(channel_group_norm(c, conf_norm_g, conf_norm_b)).astype(x.dtype)

    q_d, k_d, v_d = (to_heads(t) for t in jnp.split(sb_qkv, 3, axis=-1))
    y_d = from_heads(stick_breaking_attention(q_d, k_d, v_d)).astype(x.dtype)

    return jnp.concatenate([y_a, y_b, y_c, y_d], axis=-1) @ w_out


def memory_cross_attention(x, mem, w_q, w_kv, w_o):
    b, s, _ = x.shape
    m = mem.shape[1]
    q = (x @ w_q).reshape(b, s, MEM_HEADS, MEM_HEAD_DIM)
    k, v = jnp.split(mem @ w_kv, 2, axis=-1)
    k = k.reshape(b, m, MEM_HEADS, MEM_HEAD_DIM)
    v = v.reshape(b, m, MEM_HEADS, MEM_HEAD_DIM)
    scores = jnp.einsum('bthd,bmhd->bhtm', q, k).astype(jnp.float32) * (MEM_HEAD_DIM ** -0.5)
    probs = jax.nn.softmax(scores, axis=-1).astype(v.dtype)
    out = jnp.einsum('bhtm,bmhd->bthd', probs, v).reshape(b, s, D_MODEL)
    return out @ w_o


def _fwd_setup_inputs(seed: int = 0) -> dict:
    key = jax.random.key(seed)
    keys = iter(jax.random.split(key, 40))
    f32 = jnp.float32
    L = DEPTH

    def normal(shape, scale):
        return jax.random.normal(next(keys), shape, f32) * scale

    def gain(shape):
        return 1.0 + normal(shape, 0.02)

    dt = jnp.exp(jax.random.uniform(next(keys), (L, GROUP_HEADS), f32, math.log(1e-3), math.log(1e-1)))
    gdn_dt_bias = dt + jnp.log(-jnp.expm1(-dt))
    gdn_a_log = jnp.log(jax.random.uniform(next(keys), (L, GROUP_HEADS), f32, 1.0, 16.0))
    d_in = D_MODEL ** -0.5
    return {
        'x': normal((BATCH, SEQ, D_MODEL), 1.0),
        'mem': normal((BATCH, N_MEM, D_MODEL), 1.0),
        'ffn1_w_gate': normal((L, D_MODEL, D_FF), d_in),
        'ffn1_w_up': normal((L, D_MODEL, D_FF), d_in),
        'ffn1_w_down': normal((L, D_FF, D_MODEL), DN_INIT * D_FF ** -0.5),
        'ln_ffn1_g': gain((L, D_MODEL)),
        'ln_ffn1_b': normal((L, D_MODEL), 0.02),
        'w_in': normal((L, D_MODEL, IN_WIDTH), d_in),
        'gdn_conv_w': normal((L, SHORT_CONV, 3 * GROUP_WIDTH), SHORT_CONV ** -0.5),
        'gdn_a_log': gdn_a_log,
        'gdn_dt_bias': gdn_dt_bias,
        'gdn_norm_g': gain((L, HEAD_DIM)),
        'fox_b_f': 3.0 + normal((L, GROUP_HEADS), 0.1),
        'conf_dw_w': normal((L, CONF_KERNEL, GROUP_WIDTH), CONF_KERNEL ** -0.5),
        'conf_dw_b': normal((L, GROUP_WIDTH), 0.02),
        'conf_norm_g': gain((L, GROUP_WIDTH)),
        'conf_norm_b': normal((L, GROUP_WIDTH), 0.02),
        'w_out': normal((L, D_MODEL, D_MODEL), DN_INIT * d_in),
        'ln_mix_g': gain((L, D_MODEL)),
        'ln_mix_b': normal((L, D_MODEL), 0.02),
        'mem_w_q': normal((L, D_MODEL, D_MODEL), d_in),
        'mem_w_kv': normal((L, D_MODEL, 2 * D_MODEL), d_in),
        'mem_w_o': normal((L, D_MODEL, D_MODEL), DN_INIT * d_in),
        'ln_mem_g': gain((L, D_MODEL)),
        'ln_mem_b': normal((L, D_MODEL), 0.02),
        'ffn2_w_gate': normal((L, D_MODEL, D_FF), d_in),
        'ffn2_w_up': normal((L, D_MODEL, D_FF), d_in),
        'ffn2_w_down': normal((L, D_FF, D_MODEL), DN_INIT * D_FF ** -0.5),
        'ln_ffn2_g': gain((L, D_MODEL)),
        'ln_ffn2_b': normal((L, D_MODEL), 0.02),
    }


def _fwd_reference(x, mem, ffn1_w_gate, ffn1_w_up, ffn1_w_down, ln_ffn1_g, ln_ffn1_b,
              w_in, gdn_conv_w, gdn_a_log, gdn_dt_bias, gdn_norm_g, fox_b_f,
              conf_dw_w, conf_dw_b, conf_norm_g, conf_norm_b, w_out, ln_mix_g, ln_mix_b,
              mem_w_q, mem_w_kv, mem_w_o, ln_mem_g, ln_mem_b,
              ffn2_w_gate, ffn2_w_up, ffn2_w_down, ln_ffn2_g, ln_ffn2_b):
    for i in range(DEPTH):
        x = layer_norm(DN_ALPHA * x + 0.5 * swiglu(x, ffn1_w_gate[i], ffn1_w_up[i], ffn1_w_down[i]),
                       ln_ffn1_g[i], ln_ffn1_b[i])
        mix = parallel_head_group_mixers(x, w_in[i], gdn_conv_w[i], gdn_a_log[i], gdn_dt_bias[i], gdn_norm_g[i],
                                         fox_b_f[i], conf_dw_w[i], conf_dw_b[i], conf_norm_g[i], conf_norm_b[i],
                                         w_out[i])
        x = layer_norm(DN_ALPHA * x + mix, ln_mix_g[i], ln_mix_b[i])
        x = layer_norm(DN_ALPHA * x + memory_cross_attention(x, mem, mem_w_q[i], mem_w_kv[i], mem_w_o[i]),
                       ln_mem_g[i], ln_mem_b[i])
        x = layer_norm(DN_ALPHA * x + 0.5 * swiglu(x, ffn2_w_gate[i], ffn2_w_up[i], ffn2_w_down[i]),
                       ln_ffn2_g[i], ln_ffn2_b[i])
    return x


import jax as _jax
import jax.numpy as _jnp

TWIN_FORMAT = 'train_step'
FWD_PARAMS = ['x', 'mem', 'ffn1_w_gate', 'ffn1_w_up', 'ffn1_w_down', 'ln_ffn1_g', 'ln_ffn1_b', 'w_in', 'gdn_conv_w', 'gdn_a_log', 'gdn_dt_bias', 'gdn_norm_g', 'fox_b_f', 'conf_dw_w', 'conf_dw_b', 'conf_norm_g', 'conf_norm_b', 'w_out', 'ln_mix_g', 'ln_mix_b', 'mem_w_q', 'mem_w_kv', 'mem_w_o', 'ln_mem_g', 'ln_mem_b', 'ffn2_w_gate', 'ffn2_w_up', 'ffn2_w_down', 'ln_ffn2_g', 'ln_ffn2_b']
TWIN_WEIGHTS = ['ffn1_w_gate', 'ffn1_w_up', 'ffn1_w_down', 'ln_ffn1_g', 'ln_ffn1_b', 'w_in', 'gdn_conv_w', 'gdn_a_log', 'gdn_dt_bias', 'gdn_norm_g', 'fox_b_f', 'conf_dw_w', 'conf_dw_b', 'conf_norm_g', 'conf_norm_b', 'w_out', 'ln_mix_g', 'ln_mix_b', 'mem_w_q', 'mem_w_kv', 'mem_w_o', 'ln_mem_g', 'ln_mem_b', 'ffn2_w_gate', 'ffn2_w_up', 'ffn2_w_down', 'ln_ffn2_g', 'ln_ffn2_b']
TWIN_DIFF_INPUT = 'x'
TWIN_INPUTS = ['x', 'mem', 'ffn1_w_gate', 'ffn1_w_up', 'ffn1_w_down', 'ln_ffn1_g', 'ln_ffn1_b', 'w_in', 'gdn_conv_w', 'gdn_a_log', 'gdn_dt_bias', 'gdn_norm_g', 'fox_b_f', 'conf_dw_w', 'conf_dw_b', 'conf_norm_g', 'conf_norm_b', 'w_out', 'ln_mix_g', 'ln_mix_b', 'mem_w_q', 'mem_w_kv', 'mem_w_o', 'ln_mem_g', 'ln_mem_b', 'ffn2_w_gate', 'ffn2_w_up', 'ffn2_w_down', 'ln_ffn2_g', 'ln_ffn2_b', 'loss_target', 'm_ffn1_w_gate', 'm_ffn1_w_up', 'm_ffn1_w_down', 'm_ln_ffn1_g', 'm_ln_ffn1_b', 'm_w_in', 'm_gdn_conv_w', 'm_gdn_a_log', 'm_gdn_dt_bias', 'm_gdn_norm_g', 'm_fox_b_f', 'm_conf_dw_w', 'm_conf_dw_b', 'm_conf_norm_g', 'm_conf_norm_b', 'm_w_out', 'm_ln_mix_g', 'm_ln_mix_b', 'm_mem_w_q', 'm_mem_w_kv', 'm_mem_w_o', 'm_ln_mem_g', 'm_ln_mem_b', 'm_ffn2_w_gate', 'm_ffn2_w_up', 'm_ffn2_w_down', 'm_ln_ffn2_g', 'm_ln_ffn2_b', 'v_ffn1_w_gate', 'v_ffn1_w_up', 'v_ffn1_w_down', 'v_ln_ffn1_g', 'v_ln_ffn1_b', 'v_w_in', 'v_gdn_conv_w', 'v_gdn_a_log', 'v_gdn_dt_bias', 'v_gdn_norm_g', 'v_fox_b_f', 'v_conf_dw_w', 'v_conf_dw_b', 'v_conf_norm_g', 'v_conf_norm_b', 'v_w_out', 'v_ln_mix_g', 'v_ln_mix_b', 'v_mem_w_q', 'v_mem_w_kv', 'v_mem_w_o', 'v_ln_mem_g', 'v_ln_mem_b', 'v_ffn2_w_gate', 'v_ffn2_w_up', 'v_ffn2_w_down', 'v_ln_ffn2_g', 'v_ln_ffn2_b']
TWIN_OUTPUTS = ['loss', 'grad_x', 'grad_ffn1_w_gate', 'grad_ffn1_w_up', 'grad_ffn1_w_down', 'grad_ln_ffn1_g', 'grad_ln_ffn1_b', 'grad_w_in', 'grad_gdn_conv_w', 'grad_gdn_a_log', 'grad_gdn_dt_bias', 'grad_gdn_norm_g', 'grad_fox_b_f', 'grad_conf_dw_w', 'grad_conf_dw_b', 'grad_conf_norm_g', 'grad_conf_norm_b', 'grad_w_out', 'grad_ln_mix_g', 'grad_ln_mix_b', 'grad_mem_w_q', 'grad_mem_w_kv', 'grad_mem_w_o', 'grad_ln_mem_g', 'grad_ln_mem_b', 'grad_ffn2_w_gate', 'grad_ffn2_w_up', 'grad_ffn2_w_down', 'grad_ln_ffn2_g', 'grad_ln_ffn2_b', 'delta_ffn1_w_gate', 'delta_ffn1_w_up', 'delta_ffn1_w_down', 'delta_ln_ffn1_g', 'delta_ln_ffn1_b', 'delta_w_in', 'delta_gdn_conv_w', 'delta_gdn_a_log', 'delta_gdn_dt_bias', 'delta_gdn_norm_g', 'delta_fox_b_f', 'delta_conf_dw_w', 'delta_conf_dw_b', 'delta_conf_norm_g', 'delta_conf_norm_b', 'delta_w_out', 'delta_ln_mix_g', 'delta_ln_mix_b', 'delta_mem_w_q', 'delta_mem_w_kv', 'delta_mem_w_o', 'delta_ln_mem_g', 'delta_ln_mem_b', 'delta_ffn2_w_gate', 'delta_ffn2_w_up', 'delta_ffn2_w_down', 'delta_ln_ffn2_g', 'delta_ln_ffn2_b', 'new_m_ffn1_w_gate', 'new_m_ffn1_w_up', 'new_m_ffn1_w_down', 'new_m_ln_ffn1_g', 'new_m_ln_ffn1_b', 'new_m_w_in', 'new_m_gdn_conv_w', 'new_m_gdn_a_log', 'new_m_gdn_dt_bias', 'new_m_gdn_norm_g', 'new_m_fox_b_f', 'new_m_conf_dw_w', 'new_m_conf_dw_b', 'new_m_conf_norm_g', 'new_m_conf_norm_b', 'new_m_w_out', 'new_m_ln_mix_g', 'new_m_ln_mix_b', 'new_m_mem_w_q', 'new_m_mem_w_kv', 'new_m_mem_w_o', 'new_m_ln_mem_g', 'new_m_ln_mem_b', 'new_m_ffn2_w_gate', 'new_m_ffn2_w_up', 'new_m_ffn2_w_down', 'new_m_ln_ffn2_g', 'new_m_ln_ffn2_b', 'new_v_ffn1_w_gate', 'new_v_ffn1_w_up', 'new_v_ffn1_w_down', 'new_v_ln_ffn1_g', 'new_v_ln_ffn1_b', 'new_v_w_in', 'new_v_gdn_conv_w', 'new_v_gdn_a_log', 'new_v_gdn_dt_bias', 'new_v_gdn_norm_g', 'new_v_fox_b_f', 'new_v_conf_dw_w', 'new_v_conf_dw_b', 'new_v_conf_norm_g', 'new_v_conf_norm_b', 'new_v_w_out', 'new_v_ln_mix_g', 'new_v_ln_mix_b', 'new_v_mem_w_q', 'new_v_mem_w_kv', 'new_v_mem_w_o', 'new_v_ln_mem_g', 'new_v_ln_mem_b', 'new_v_ffn2_w_gate', 'new_v_ffn2_w_up', 'new_v_ffn2_w_down', 'new_v_ln_ffn2_g', 'new_v_ln_ffn2_b']
TWIN_LEAF_KINDS = {'loss': 'loss', 'grad_x': 'grad_x', 'grad_ffn1_w_gate': 'grad_w', 'grad_ffn1_w_up': 'grad_w', 'grad_ffn1_w_down': 'grad_w', 'grad_ln_ffn1_g': 'grad_w', 'grad_ln_ffn1_b': 'grad_w', 'grad_w_in': 'grad_w', 'grad_gdn_conv_w': 'grad_w', 'grad_gdn_a_log': 'grad_w', 'grad_gdn_dt_bias': 'grad_w', 'grad_gdn_norm_g': 'grad_w', 'grad_fox_b_f': 'grad_w', 'grad_conf_dw_w': 'grad_w', 'grad_conf_dw_b': 'grad_w', 'grad_conf_norm_g': 'grad_w', 'grad_conf_norm_b': 'grad_w', 'grad_w_out': 'grad_w', 'grad_ln_mix_g': 'grad_w', 'grad_ln_mix_b': 'grad_w', 'grad_mem_w_q': 'grad_w', 'grad_mem_w_kv': 'grad_w', 'grad_mem_w_o': 'grad_w', 'grad_ln_mem_g': 'grad_w', 'grad_ln_mem_b': 'grad_w', 'grad_ffn2_w_gate': 'grad_w', 'grad_ffn2_w_up': 'grad_w', 'grad_ffn2_w_down': 'grad_w', 'grad_ln_ffn2_g': 'grad_w', 'grad_ln_ffn2_b': 'grad_w', 'delta_ffn1_w_gate': 'delta_w', 'delta_ffn1_w_up': 'delta_w', 'delta_ffn1_w_down': 'delta_w', 'delta_ln_ffn1_g': 'delta_w', 'delta_ln_ffn1_b': 'delta_w', 'delta_w_in': 'delta_w', 'delta_gdn_conv_w': 'delta_w', 'delta_gdn_a_log': 'delta_w', 'delta_gdn_dt_bias': 'delta_w', 'delta_gdn_norm_g': 'delta_w', 'delta_fox_b_f': 'delta_w', 'delta_conf_dw_w': 'delta_w', 'delta_conf_dw_b': 'delta_w', 'delta_conf_norm_g': 'delta_w', 'delta_conf_norm_b': 'delta_w', 'delta_w_out': 'delta_w', 'delta_ln_mix_g': 'delta_w', 'delta_ln_mix_b': 'delta_w', 'delta_mem_w_q': 'delta_w', 'delta_mem_w_kv': 'delta_w', 'delta_mem_w_o': 'delta_w', 'delta_ln_mem_g': 'delta_w', 'delta_ln_mem_b': 'delta_w', 'delta_ffn2_w_gate': 'delta_w', 'delta_ffn2_w_up': 'delta_w', 'delta_ffn2_w_down': 'delta_w', 'delta_ln_ffn2_g': 'delta_w', 'delta_ln_ffn2_b': 'delta_w', 'new_m_ffn1_w_gate': 'new_m', 'new_m_ffn1_w_up': 'new_m', 'new_m_ffn1_w_down': 'new_m', 'new_m_ln_ffn1_g': 'new_m', 'new_m_ln_ffn1_b': 'new_m', 'new_m_w_in': 'new_m', 'new_m_gdn_conv_w': 'new_m', 'new_m_gdn_a_log': 'new_m', 'new_m_gdn_dt_bias': 'new_m', 'new_m_gdn_norm_g': 'new_m', 'new_m_fox_b_f': 'new_m', 'new_m_conf_dw_w': 'new_m', 'new_m_conf_dw_b': 'new_m', 'new_m_conf_norm_g': 'new_m', 'new_m_conf_norm_b': 'new_m', 'new_m_w_out': 'new_m', 'new_m_ln_mix_g': 'new_m', 'new_m_ln_mix_b': 'new_m', 'new_m_mem_w_q': 'new_m', 'new_m_mem_w_kv': 'new_m', 'new_m_mem_w_o': 'new_m', 'new_m_ln_mem_g': 'new_m', 'new_m_ln_mem_b': 'new_m', 'new_m_ffn2_w_gate': 'new_m', 'new_m_ffn2_w_up': 'new_m', 'new_m_ffn2_w_down': 'new_m', 'new_m_ln_ffn2_g': 'new_m', 'new_m_ln_ffn2_b': 'new_m', 'new_v_ffn1_w_gate': 'new_v', 'new_v_ffn1_w_up': 'new_v', 'new_v_ffn1_w_down': 'new_v', 'new_v_ln_ffn1_g': 'new_v', 'new_v_ln_ffn1_b': 'new_v', 'new_v_w_in': 'new_v', 'new_v_gdn_conv_w': 'new_v', 'new_v_gdn_a_log': 'new_v', 'new_v_gdn_dt_bias': 'new_v', 'new_v_gdn_norm_g': 'new_v', 'new_v_fox_b_f': 'new_v', 'new_v_conf_dw_w': 'new_v', 'new_v_conf_dw_b': 'new_v', 'new_v_conf_norm_g': 'new_v', 'new_v_conf_norm_b': 'new_v', 'new_v_w_out': 'new_v', 'new_v_ln_mix_g': 'new_v', 'new_v_ln_mix_b': 'new_v', 'new_v_mem_w_q': 'new_v', 'new_v_mem_w_kv': 'new_v', 'new_v_mem_w_o': 'new_v', 'new_v_ln_mem_g': 'new_v', 'new_v_ln_mem_b': 'new_v', 'new_v_ffn2_w_gate': 'new_v', 'new_v_ffn2_w_up': 'new_v', 'new_v_ffn2_w_down': 'new_v', 'new_v_ln_ffn2_g': 'new_v', 'new_v_ln_ffn2_b': 'new_v'}


def _forward(args):
    return _fwd_reference(*[args[k] for k in FWD_PARAMS])


def _output_shape():
    def fwd():
        inp = _fwd_setup_inputs(0)
        return _fwd_reference(*[inp[k] for k in FWD_PARAMS])
    out = _jax.eval_shape(fwd)
    return out.shape, out.dtype

N_MICROBATCH = 1
ADAM_LR = 0.001
ADAM_B1 = 0.9
ADAM_B2 = 0.999
ADAM_EPS = 1e-08
ADAM_WD = 0.01
ADAM_STEP = 10
PER_EXAMPLE_BATCH_AXIS = {'x': 0, 'mem': 0, 'loss_target': 0}
SHARED_INPUTS = []
_WEIGHT_DTYPES = {'ffn1_w_gate': _jnp.float32, 'ffn1_w_up': _jnp.float32, 'ffn1_w_down': _jnp.float32, 'ln_ffn1_g': _jnp.float32, 'ln_ffn1_b': _jnp.float32, 'w_in': _jnp.float32, 'gdn_conv_w': _jnp.float32, 'gdn_a_log': _jnp.float32, 'gdn_dt_bias': _jnp.float32, 'gdn_norm_g': _jnp.float32, 'fox_b_f': _jnp.float32, 'conf_dw_w': _jnp.float32, 'conf_dw_b': _jnp.float32, 'conf_norm_g': _jnp.float32, 'conf_norm_b': _jnp.float32, 'w_out': _jnp.float32, 'ln_mix_g': _jnp.float32, 'ln_mix_b': _jnp.float32, 'mem_w_q': _jnp.float32, 'mem_w_kv': _jnp.float32, 'mem_w_o': _jnp.float32, 'ln_mem_g': _jnp.float32, 'ln_mem_b': _jnp.float32, 'ffn2_w_gate': _jnp.float32, 'ffn2_w_up': _jnp.float32, 'ffn2_w_down': _jnp.float32, 'ln_ffn2_g': _jnp.float32, 'ln_ffn2_b': _jnp.float32}
MOMENT_SCALE = {'ffn1_w_gate': 1.699479e-02, 'ffn1_w_up': 1.647085e-02, 'ffn1_w_down': 5.469157e-02, 'ln_ffn1_g': 2.165135e+00, 'ln_ffn1_b': 1.093919e+00, 'w_in': 4.061518e-02, 'gdn_conv_w': 5.021420e-02, 'gdn_a_log': 3.322214e-01, 'gdn_dt_bias': 3.222232e-01, 'gdn_norm_g': 1.220585e-01, 'fox_b_f': 1.022854e-01, 'conf_dw_w': 5.656039e-02, 'conf_dw_b': 2.897449e-01, 'conf_norm_g': 1.261159e-01, 'conf_norm_b': 1.866181e-01, 'w_out': 1.152994e-01, 'ln_mix_g': 2.278640e+00, 'ln_mix_b': 1.053757e+00, 'mem_w_q': 9.471621e-03, 'mem_w_kv': 1.021843e-02, 'mem_w_o': 2.180317e-02, 'ln_mem_g': 2.275327e+00, 'ln_mem_b': 1.053608e+00, 'ffn2_w_gate': 1.658826e-02, 'ffn2_w_up': 1.615276e-02, 'ffn2_w_down': 5.351084e-02, 'ln_ffn2_g': 4.542703e+01, 'ln_ffn2_b': 3.719046e+00}


def _to_microbatches(a, axis):
    t = _jnp.moveaxis(a, axis, 0)
    t = t.reshape((N_MICROBATCH, t.shape[0] // N_MICROBATCH) + t.shape[1:])
    return _jnp.moveaxis(t, 1, axis + 1)


def setup_inputs(seed: int = 0) -> dict:
    inp = _fwd_setup_inputs(seed)
    key = _jax.random.fold_in(_jax.random.key(seed), 7919)
    shape, _ = _output_shape()
    out = dict(inp)
    out["loss_target"] = _jax.random.normal(_jax.random.fold_in(key, 0), shape, _jnp.float32)
    for i, name in enumerate(TWIN_WEIGHTS):
        w = inp[name].astype(_jnp.float32)
        if MOMENT_SCALE is None:
            s = _jnp.sqrt(_jnp.mean(_jnp.square(w)) + 1e-30)
        else:
            s = MOMENT_SCALE[name]
        km, kv = _jax.random.split(_jax.random.fold_in(key, i + 1))
        out[name] = w
        out["m_" + name] = s * _jax.random.normal(km, w.shape, _jnp.float32)
        out["v_" + name] = (s * s) * _jax.random.uniform(kv, w.shape, _jnp.float32, 0.5, 1.5)
    if N_MICROBATCH > 1:
        for name, axis in PER_EXAMPLE_BATCH_AXIS.items():
            out[name] = _to_microbatches(out[name], axis)
    return {'x': out['x'], 'mem': out['mem'], 'ffn1_w_gate': out['ffn1_w_gate'], 'ffn1_w_up': out['ffn1_w_up'], 'ffn1_w_down': out['ffn1_w_down'], 'ln_ffn1_g': out['ln_ffn1_g'], 'ln_ffn1_b': out['ln_ffn1_b'], 'w_in': out['w_in'], 'gdn_conv_w': out['gdn_conv_w'], 'gdn_a_log': out['gdn_a_log'], 'gdn_dt_bias': out['gdn_dt_bias'], 'gdn_norm_g': out['gdn_norm_g'], 'fox_b_f': out['fox_b_f'], 'conf_dw_w': out['conf_dw_w'], 'conf_dw_b': out['conf_dw_b'], 'conf_norm_g': out['conf_norm_g'], 'conf_norm_b': out['conf_norm_b'], 'w_out': out['w_out'], 'ln_mix_g': out['ln_mix_g'], 'ln_mix_b': out['ln_mix_b'], 'mem_w_q': out['mem_w_q'], 'mem_w_kv': out['mem_w_kv'], 'mem_w_o': out['mem_w_o'], 'ln_mem_g': out['ln_mem_g'], 'ln_mem_b': out['ln_mem_b'], 'ffn2_w_gate': out['ffn2_w_gate'], 'ffn2_w_up': out['ffn2_w_up'], 'ffn2_w_down': out['ffn2_w_down'], 'ln_ffn2_g': out['ln_ffn2_g'], 'ln_ffn2_b': out['ln_ffn2_b'], 'loss_target': out['loss_target'], 'm_ffn1_w_gate': out['m_ffn1_w_gate'], 'm_ffn1_w_up': out['m_ffn1_w_up'], 'm_ffn1_w_down': out['m_ffn1_w_down'], 'm_ln_ffn1_g': out['m_ln_ffn1_g'], 'm_ln_ffn1_b': out['m_ln_ffn1_b'], 'm_w_in': out['m_w_in'], 'm_gdn_conv_w': out['m_gdn_conv_w'], 'm_gdn_a_log': out['m_gdn_a_log'], 'm_gdn_dt_bias': out['m_gdn_dt_bias'], 'm_gdn_norm_g': out['m_gdn_norm_g'], 'm_fox_b_f': out['m_fox_b_f'], 'm_conf_dw_w': out['m_conf_dw_w'], 'm_conf_dw_b': out['m_conf_dw_b'], 'm_conf_norm_g': out['m_conf_norm_g'], 'm_conf_norm_b': out['m_conf_norm_b'], 'm_w_out': out['m_w_out'], 'm_ln_mix_g': out['m_ln_mix_g'], 'm_ln_mix_b': out['m_ln_mix_b'], 'm_mem_w_q': out['m_mem_w_q'], 'm_mem_w_kv': out['m_mem_w_kv'], 'm_mem_w_o': out['m_mem_w_o'], 'm_ln_mem_g': out['m_ln_mem_g'], 'm_ln_mem_b': out['m_ln_mem_b'], 'm_ffn2_w_gate': out['m_ffn2_w_gate'], 'm_ffn2_w_up': out['m_ffn2_w_up'], 'm_ffn2_w_down': out['m_ffn2_w_down'], 'm_ln_ffn2_g': out['m_ln_ffn2_g'], 'm_ln_ffn2_b': out['m_ln_ffn2_b'], 'v_ffn1_w_gate': out['v_ffn1_w_gate'], 'v_ffn1_w_up': out['v_ffn1_w_up'], 'v_ffn1_w_down': out['v_ffn1_w_down'], 'v_ln_ffn1_g': out['v_ln_ffn1_g'], 'v_ln_ffn1_b': out['v_ln_ffn1_b'], 'v_w_in': out['v_w_in'], 'v_gdn_conv_w': out['v_gdn_conv_w'], 'v_gdn_a_log': out['v_gdn_a_log'], 'v_gdn_dt_bias': out['v_gdn_dt_bias'], 'v_gdn_norm_g': out['v_gdn_norm_g'], 'v_fox_b_f': out['v_fox_b_f'], 'v_conf_dw_w': out['v_conf_dw_w'], 'v_conf_dw_b': out['v_conf_dw_b'], 'v_conf_norm_g': out['v_conf_norm_g'], 'v_conf_norm_b': out['v_conf_norm_b'], 'v_w_out': out['v_w_out'], 'v_ln_mix_g': out['v_ln_mix_g'], 'v_ln_mix_b': out['v_ln_mix_b'], 'v_mem_w_q': out['v_mem_w_q'], 'v_mem_w_kv': out['v_mem_w_kv'], 'v_mem_w_o': out['v_mem_w_o'], 'v_ln_mem_g': out['v_ln_mem_g'], 'v_ln_mem_b': out['v_ln_mem_b'], 'v_ffn2_w_gate': out['v_ffn2_w_gate'], 'v_ffn2_w_up': out['v_ffn2_w_up'], 'v_ffn2_w_down': out['v_ffn2_w_down'], 'v_ln_ffn2_g': out['v_ln_ffn2_g'], 'v_ln_ffn2_b': out['v_ln_ffn2_b']}


def _loss(weights, diff, rest, loss_target):
    with _jax.named_scope("forward"):
        args = {**rest, TWIN_DIFF_INPUT: diff, **{k: w.astype(_WEIGHT_DTYPES[k]) for k, w in weights.items()}}
        y = _forward(args)
    with _jax.named_scope("loss_head"):
        err = _jnp.square(y.astype(_jnp.float32) - loss_target)
        return 0.5 * _jnp.sum(_jnp.mean(err, axis=-1)) if err.ndim else 0.5 * err


def _adamw(w, g, m, v):
    m = ADAM_B1 * m + (1.0 - ADAM_B1) * g
    v = ADAM_B2 * v + (1.0 - ADAM_B2) * _jnp.square(g)
    m_hat = m / (1.0 - ADAM_B1 ** ADAM_STEP)
    v_hat = v / (1.0 - ADAM_B2 ** ADAM_STEP)
    delta = -ADAM_LR * (m_hat / (_jnp.sqrt(v_hat) + ADAM_EPS) + ADAM_WD * w)
    return delta, m, v


def reference(x, mem, ffn1_w_gate, ffn1_w_up, ffn1_w_down, ln_ffn1_g, ln_ffn1_b, w_in, gdn_conv_w, gdn_a_log, gdn_dt_bias, gdn_norm_g, fox_b_f, conf_dw_w, conf_dw_b, conf_norm_g, conf_norm_b, w_out, ln_mix_g, ln_mix_b, mem_w_q, mem_w_kv, mem_w_o, ln_mem_g, ln_mem_b, ffn2_w_gate, ffn2_w_up, ffn2_w_down, ln_ffn2_g, ln_ffn2_b, loss_target, m_ffn1_w_gate, m_ffn1_w_up, m_ffn1_w_down, m_ln_ffn1_g, m_ln_ffn1_b, m_w_in, m_gdn_conv_w, m_gdn_a_log, m_gdn_dt_bias, m_gdn_norm_g, m_fox_b_f, m_conf_dw_w, m_conf_dw_b, m_conf_norm_g, m_conf_norm_b, m_w_out, m_ln_mix_g, m_ln_mix_b, m_mem_w_q, m_mem_w_kv, m_mem_w_o, m_ln_mem_g, m_ln_mem_b, m_ffn2_w_gate, m_ffn2_w_up, m_ffn2_w_down, m_ln_ffn2_g, m_ln_ffn2_b, v_ffn1_w_gate, v_ffn1_w_up, v_ffn1_w_down, v_ln_ffn1_g, v_ln_ffn1_b, v_w_in, v_gdn_conv_w, v_gdn_a_log, v_gdn_dt_bias, v_gdn_norm_g, v_fox_b_f, v_conf_dw_w, v_conf_dw_b, v_conf_norm_g, v_conf_norm_b, v_w_out, v_ln_mix_g, v_ln_mix_b, v_mem_w_q, v_mem_w_kv, v_mem_w_o, v_ln_mem_g, v_ln_mem_b, v_ffn2_w_gate, v_ffn2_w_up, v_ffn2_w_down, v_ln_ffn2_g, v_ln_ffn2_b):
    given = dict(x=x, mem=mem, ffn1_w_gate=ffn1_w_gate, ffn1_w_up=ffn1_w_up, ffn1_w_down=ffn1_w_down, ln_ffn1_g=ln_ffn1_g, ln_ffn1_b=ln_ffn1_b, w_in=w_in, gdn_conv_w=gdn_conv_w, gdn_a_log=gdn_a_log, gdn_dt_bias=gdn_dt_bias, gdn_norm_g=gdn_norm_g, fox_b_f=fox_b_f, conf_dw_w=conf_dw_w, conf_dw_b=conf_dw_b, conf_norm_g=conf_norm_g, conf_norm_b=conf_norm_b, w_out=w_out, ln_mix_g=ln_mix_g, ln_mix_b=ln_mix_b, mem_w_q=mem_w_q, mem_w_kv=mem_w_kv, mem_w_o=mem_w_o, ln_mem_g=ln_mem_g, ln_mem_b=ln_mem_b, ffn2_w_gate=ffn2_w_gate, ffn2_w_up=ffn2_w_up, ffn2_w_down=ffn2_w_down, ln_ffn2_g=ln_ffn2_g, ln_ffn2_b=ln_ffn2_b, loss_target=loss_target, m_ffn1_w_gate=m_ffn1_w_gate, m_ffn1_w_up=m_ffn1_w_up, m_ffn1_w_down=m_ffn1_w_down, m_ln_ffn1_g=m_ln_ffn1_g, m_ln_ffn1_b=m_ln_ffn1_b, m_w_in=m_w_in, m_gdn_conv_w=m_gdn_conv_w, m_gdn_a_log=m_gdn_a_log, m_gdn_dt_bias=m_gdn_dt_bias, m_gdn_norm_g=m_gdn_norm_g, m_fox_b_f=m_fox_b_f, m_conf_dw_w=m_conf_dw_w, m_conf_dw_b=m_conf_dw_b, m_conf_norm_g=m_conf_norm_g, m_conf_norm_b=m_conf_norm_b, m_w_out=m_w_out, m_ln_mix_g=m_ln_mix_g, m_ln_mix_b=m_ln_mix_b, m_mem_w_q=m_mem_w_q, m_mem_w_kv=m_mem_w_kv, m_mem_w_o=m_mem_w_o, m_ln_mem_g=m_ln_mem_g, m_ln_mem_b=m_ln_mem_b, m_ffn2_w_gate=m_ffn2_w_gate, m_ffn2_w_up=m_ffn2_w_up, m_ffn2_w_down=m_ffn2_w_down, m_ln_ffn2_g=m_ln_ffn2_g, m_ln_ffn2_b=m_ln_ffn2_b, v_ffn1_w_gate=v_ffn1_w_gate, v_ffn1_w_up=v_ffn1_w_up, v_ffn1_w_down=v_ffn1_w_down, v_ln_ffn1_g=v_ln_ffn1_g, v_ln_ffn1_b=v_ln_ffn1_b, v_w_in=v_w_in, v_gdn_conv_w=v_gdn_conv_w, v_gdn_a_log=v_gdn_a_log, v_gdn_dt_bias=v_gdn_dt_bias, v_gdn_norm_g=v_gdn_norm_g, v_fox_b_f=v_fox_b_f, v_conf_dw_w=v_conf_dw_w, v_conf_dw_b=v_conf_dw_b, v_conf_norm_g=v_conf_norm_g, v_conf_norm_b=v_conf_norm_b, v_w_out=v_w_out, v_ln_mix_g=v_ln_mix_g, v_ln_mix_b=v_ln_mix_b, v_mem_w_q=v_mem_w_q, v_mem_w_kv=v_mem_w_kv, v_mem_w_o=v_mem_w_o, v_ln_mem_g=v_ln_mem_g, v_ln_mem_b=v_ln_mem_b, v_ffn2_w_gate=v_ffn2_w_gate, v_ffn2_w_up=v_ffn2_w_up, v_ffn2_w_down=v_ffn2_w_down, v_ln_ffn2_g=v_ln_ffn2_g, v_ln_ffn2_b=v_ln_ffn2_b)
    weights = {n: given[n] for n in TWIN_WEIGHTS}
    shared = {n: given[n] for n in SHARED_INPUTS}
    per_example = {n: given[n] for n in ['x', 'mem']}
    grad_fn = _jax.value_and_grad(_loss, argnums=(0, 1))

    def one_microbatch(ex, loss_target):
        ex = dict(ex)
        diff = ex.pop(TWIN_DIFF_INPUT)
        return grad_fn(weights, diff, {**shared, **ex}, loss_target)

    if N_MICROBATCH == 1:
        loss, (grad_w, grad_x) = one_microbatch(per_example, given["loss_target"])
    else:
        def body(carry, xs):
            loss_sum, grad_sum = carry
            l_k, (gw_k, gx_k) = one_microbatch(xs[0], xs[1])
            with _jax.named_scope("update"):
                return (loss_sum + l_k, _jax.tree.map(_jnp.add, grad_sum, gw_k)), gx_k

        init = (_jnp.zeros((), _jnp.float32), _jax.tree.map(_jnp.zeros_like, weights))
        (loss, grad_w), grad_x = _jax.lax.scan(body, init, (per_example, given["loss_target"]))
    with _jax.named_scope("update"):
        delta_w, new_m, new_v = {}, {}, {}
        for n in TWIN_WEIGHTS:
            delta_w[n], new_m[n], new_v[n] = _adamw(weights[n], grad_w[n], given["m_" + n], given["v_" + n])
    return (loss, grad_x, *[grad_w[n] for n in TWIN_WEIGHTS], *[delta_w[n] for n in TWIN_WEIGHTS],
            *[new_m[n] for n in TWIN_WEIGHTS], *[new_v[n] for n in TWIN_WEIGHTS])
```

```python
import functools
import math

import jax
import jax.numpy as jnp
import numpy as np
from jax import lax
from jax.experimental import pallas as pl
from jax.experimental.pallas import tpu as pltpu

F32 = jnp.float32
BF16 = jnp.bfloat16
MXU_DT = jnp.bfloat16
HI = lax.Precision.HIGHEST

N_DEV = 8
VMEM_LIMIT_BYTES = 56 * 1024 * 1024
LANES = 128

D_MODEL = 1024
DEPTH = 2
GROUP_WIDTH = 256
HEAD_DIM = 64
GROUP_HEADS = 4
D_FF = 2816
SHORT_CONV = 4
CONF_KERNEL = 31
CONF_GROUPS = 4
GDN_CHUNK = 64
N_MEM = 256
MEM_HEADS = 4
MEM_HEAD_DIM = 256
DN_ALPHA = float((2 * DEPTH) ** 0.25)
LN_EPS = 1e-5
RMS_EPS = 1e-6
L2_EPS = 1e-6
NEG_BIG = -1e30
IN_SPLITS = (768, 256, 4, 4, 768, 4, 512, 768)
IN_WIDTH = sum(IN_SPLITS)
P_GDN, P_Z, P_FOX, P_CONF, P_SB, P_SMALL = 0, 768, 1024, 1792, 2304, 3072
P_WIDTH = 3200

ADAM_LR = 0.001
ADAM_B1 = 0.9
ADAM_B2 = 0.999
ADAM_EPS = 1e-08
ADAM_WD = 0.01
ADAM_STEP = 10


def _cparams(sem):
    return pltpu.CompilerParams(dimension_semantics=sem, vmem_limit_bytes=VMEM_LIMIT_BYTES)


def _tile(n, pref, align=LANES):
    if n <= pref:
        return n
    t = (pref // align) * align
    while t >= align:
        if n % t == 0:
            return t
        t -= align
    return n


def mm(a, b, *, mode="nn", add=None, alpha=1.0, beta=1.0, out_dtype=F32, name,
       tm=1024, tn=512, tk=1024):
    if mode == "nn":
        (m, k), (k2, n) = a.shape, b.shape
    elif mode == "nt":
        (m, k), (n, k2) = a.shape, b.shape
    else:
        (k, m), (k2, n) = a.shape, b.shape
    assert k == k2, (a.shape, b.shape, mode)
    tm = _tile(m, tm, 8 if mode != "tn" else LANES)
    tn = _tile(n, tn)
    tk = _tile(k, tk, LANES if mode != "tn" else 8)
    nk = k // tk
    if mode == "nn":
        a_spec = pl.BlockSpec((tm, tk), lambda i, j, kk: (i, kk))
        b_spec = pl.BlockSpec((tk, tn), lambda i, j, kk: (kk, j))
        dims = (((1,), (0,)), ((), ()))
    elif mode == "nt":
        a_spec = pl.BlockSpec((tm, tk), lambda i, j, kk: (i, kk))
        b_spec = pl.BlockSpec((tn, tk), lambda i, j, kk: (j, kk))
        dims = (((1,), (1,)), ((), ()))
    else:
        a_spec = pl.BlockSpec((tk, tm), lambda i, j, kk: (kk, i))
        b_spec = pl.BlockSpec((tk, tn), lambda i, j, kk: (kk, j))
        dims = (((0,), (0,)), ((), ()))
    o_spec = pl.BlockSpec((tm, tn), lambda i, j, kk: (i, j))
    has_add = add is not None

    def body(*refs):
        if has_add:
            a_ref, b_ref, add_ref, o_ref, acc_ref = refs
        else:
            a_ref, b_ref, o_ref, acc_ref = refs
        kk = pl.program_id(2)

        @pl.when(kk == 0)
        def _():
            acc_ref[...] = jnp.zeros_like(acc_ref)

        acc_ref[...] += lax.dot_general(a_ref[...].astype(MXU_DT), b_ref[...].astype(MXU_DT), dims,
                                        preferred_element_type=F32)

        @pl.when(kk == nk - 1)
        def _():
            r = acc_ref[...]
            if alpha != 1.0:
                r = r * alpha
            if has_add:
                r = r + beta * add_ref[...].astype(F32)
            o_ref[...] = r.astype(out_dtype)

    in_specs = [a_spec, b_spec] + ([o_spec] if has_add else [])
    args = (a, b) + ((add,) if has_add else ())
    return pl.pallas_call(
        body, name=name, grid=(m // tm, n // tn, nk),
        in_specs=in_specs, out_specs=o_spec,
        out_shape=jax.ShapeDtypeStruct((m, n), out_dtype),
        scratch_shapes=[pltpu.VMEM((tm, tn), F32)],
        compiler_params=_cparams(("parallel", "parallel", "arbitrary")),
    )(*args)


def ln_res_fwd(x, y, g, b, s, *, name):
    t, d = x.shape
    tm = _tile(t, 512, 8)

    def body(x_ref, y_ref, g_ref, b_ref, o_ref, xh_ref, rs_ref):
        z = DN_ALPHA * x_ref[...] + s * y_ref[...]
        mu = jnp.mean(z, axis=-1, keepdims=True)
        zc = z - mu
        var = jnp.mean(zc * zc, axis=-1, keepdims=True)
        rstd = lax.rsqrt(var + LN_EPS)
        xh = zc * rstd
        xh_ref[...] = xh
        rs_ref[...] = rstd
        o_ref[...] = xh * g_ref[...] + b_ref[...]

    row = pl.BlockSpec((tm, d), lambda i: (i, 0))
    vec = pl.BlockSpec((1, d), lambda i: (0, 0))
    return pl.pallas_call(
        body, name=name, grid=(t // tm,),
        in_specs=[row, row, vec, vec],
        out_specs=[row, row, pl.BlockSpec((tm, 1), lambda i: (i, 0))],
        out_shape=[jax.ShapeDtypeStruct((t, d), F32), jax.ShapeDtypeStruct((t, d), F32),
                   jax.ShapeDtypeStruct((t, 1), F32)],
        compiler_params=_cparams(("parallel",)),
    )(x, y, g, b)


def ln_res_bwd(dout, xhat, rstd, g, *, name):
    t, d = dout.shape
    tm = _tile(t, 512, 8)

    def body(do_ref, xh_ref, rs_ref, g_ref, dz_ref, dg_ref, db_ref):
        i = pl.program_id(0)

        @pl.when(i == 0)
        def _():
            dg_ref[...] = jnp.zeros_like(dg_ref)
            db_ref[...] = jnp.zeros_like(db_ref)

        do = do_ref[...]
        xh = xh_ref[...]
        dxh = do * g_ref[...]
        m1 = jnp.mean(dxh, axis=-1, keepdims=True)
        m2 = jnp.mean(dxh * xh, axis=-1, keepdims=True)
        dz_ref[...] = rs_ref[...] * (dxh - m1 - xh * m2)
        dg_ref[...] += jnp.sum(do * xh, axis=0, keepdims=True)
        db_ref[...] += jnp.sum(do, axis=0, keepdims=True)

    row = pl.BlockSpec((tm, d), lambda i: (i, 0))
    vec = pl.BlockSpec((1, d), lambda i: (0, 0))
    return pl.pallas_call(
        body, name=name, grid=(t // tm,),
        in_specs=[row, row, pl.BlockSpec((tm, 1), lambda i: (i, 0)), vec],
        out_specs=[row, vec, vec],
        out_shape=[jax.ShapeDtypeStruct((t, d), F32), jax.ShapeDtypeStruct((1, d), F32),
                   jax.ShapeDtypeStruct((1, d), F32)],
        compiler_params=_cparams(("arbitrary",)),
    )(dout, xhat, rstd, g)


def _sigmoid(x):
    return 1.0 / (1.0 + jnp.exp(-x))


def act_fwd(gu, *, name):
    t, f2 = gu.shape
    f = f2 // 2
    tm = _tile(t, 256, 8)

    def body(gu_ref, h_ref):
        g = gu_ref[:, :f]
        h_ref[...] = (g * _sigmoid(g) * gu_ref[:, f:]).astype(h_ref.dtype)

    return pl.pallas_call(
        body, name=name, grid=(t // tm,),
        in_specs=[pl.BlockSpec((tm, f2), lambda i: (i, 0))],
        out_specs=pl.BlockSpec((tm, f), lambda i: (i, 0)),
        out_shape=jax.ShapeDtypeStruct((t, f), MXU_DT),
        compiler_params=_cparams(("parallel",)),
    )(gu)


def act_bwd(gu, dh, *, name):
    t, f2 = gu.shape
    f = f2 // 2
    tm = _tile(t, 256, 8)

    def body(gu_ref, dh_ref, o_ref):
        g = gu_ref[:, :f]
        u = gu_ref[:, f:]
        dh = dh_ref[...]
        sg = _sigmoid(g)
        o_ref[:, f:] = (dh * g * sg).astype(o_ref.dtype)
        o_ref[:, :f] = (dh * u * sg * (1.0 + g * (1.0 - sg))).astype(o_ref.dtype)

    return pl.pallas_call(
        body, name=name, grid=(t // tm,),
        in_specs=[pl.BlockSpec((tm, f2), lambda i: (i, 0)), pl.BlockSpec((tm, f), lambda i: (i, 0))],
        out_specs=pl.BlockSpec((tm, f2), lambda i: (i, 0)),
        out_shape=jax.ShapeDtypeStruct((t, f2), MXU_DT),
        compiler_params=_cparams(("parallel",)),
    )(gu, dh)


def loss_head(y, target, *, name):
    t, d = y.shape
    tm = _tile(t, 512, 8)

    def body(y_ref, t_ref, dy_ref, l_ref):
        i = pl.program_id(0)

        @pl.when(i == 0)
        def _():
            l_ref[...] = jnp.zeros_like(l_ref)

        err = y_ref[...] - t_ref[...]
        dy_ref[...] = err * (1.0 / d)
        part = jnp.sum(jnp.sum(err * err, axis=-1, keepdims=True), axis=0, keepdims=True)
        l_ref[...] += jnp.broadcast_to(part * (0.5 / d), l_ref.shape)

    row = pl.BlockSpec((tm, d), lambda i: (i, 0))
    return pl.pallas_call(
        body, name=name, grid=(t // tm,),
        in_specs=[row, row],
        out_specs=[row, pl.BlockSpec((1, LANES), lambda i: (0, 0))],
        out_shape=[jax.ShapeDtypeStruct((t, d), F32), jax.ShapeDtypeStruct((1, LANES), F32)],
        compiler_params=_cparams(("arbitrary",)),
    )(y, target)


def _dot(a, b):
    return lax.dot_general(a, b, (((1,), (0,)), ((), ())), preferred_element_type=F32)


def _dot_nt(a, b):
    return lax.dot_general(a, b, (((1,), (1,)), ((), ())), preferred_element_type=F32)


def _dot_tn(a, b):
    return lax.dot_general(a, b, (((0,), (0,)), ((), ())), preferred_element_type=F32)


def _dot_hi(a, b):
    return lax.dot_general(a, b, (((1,), (0,)), ((), ())), preferred_element_type=F32, precision=HI)


def _dot_nt_hi(a, b):
    return lax.dot_general(a, b, (((1,), (1,)), ((), ())), preferred_element_type=F32, precision=HI)


def _split_dot(x, u):
    hi = x.astype(MXU_DT)
    lo = (x - hi.astype(F32)).astype(MXU_DT)
    return _dot(hi, u) + _dot(lo, u)


def _mem_probs(q_ref, kv_ref, h):
    lo = h * MEM_HEAD_DIM
    qh = q_ref[:, lo:lo + MEM_HEAD_DIM].astype(MXU_DT)
    kh = kv_ref[:, lo:lo + MEM_HEAD_DIM].astype(MXU_DT)
    s = _dot_nt(qh, kh) * (MEM_HEAD_DIM ** -0.5)
    s = s - jnp.max(s, axis=-1, keepdims=True)
    p = jnp.exp(s)
    return p / jnp.sum(p, axis=-1, keepdims=True), qh, kh


def memattn_fwd(q, kv, *, name):
    t, d = q.shape
    tm = _tile(t, 512, 8)

    def body(q_ref, kv_ref, o_ref):
        for h in range(MEM_HEADS):
            lo = h * MEM_HEAD_DIM
            p, _, _ = _mem_probs(q_ref, kv_ref, h)
            vh = kv_ref[:, d + lo:d + lo + MEM_HEAD_DIM].astype(MXU_DT)
            o_ref[:, lo:lo + MEM_HEAD_DIM] = _dot(p.astype(MXU_DT), vh)

    return pl.pallas_call(
        body, name=name, grid=(t // tm,),
        in_specs=[pl.BlockSpec((tm, d), lambda i: (i, 0)), pl.BlockSpec(kv.shape, lambda i: (0, 0))],
        out_specs=pl.BlockSpec((tm, d), lambda i: (i, 0)),
        out_shape=jax.ShapeDtypeStruct((t, d), F32),
        compiler_params=_cparams(("parallel",)),
    )(q, kv)


def memattn_bwd(q, kv, datt, *, name):
    t, d = q.shape
    tm = _tile(t, 512, 8)
    scale = MEM_HEAD_DIM ** -0.5

    def body(q_ref, kv_ref, da_ref, dq_ref, dkv_ref):
        @pl.when(pl.program_id(0) == 0)
        def _():
            dkv_ref[...] = jnp.zeros_like(dkv_ref)

        for h in range(MEM_HEADS):
            lo = h * MEM_HEAD_DIM
            p, qh, kh = _mem_probs(q_ref, kv_ref, h)
            vh = kv_ref[:, d + lo:d + lo + MEM_HEAD_DIM].astype(MXU_DT)
            da = da_ref[:, lo:lo + MEM_HEAD_DIM].astype(MXU_DT)
            dp = _dot_nt(da, vh)
            ds = p * (dp - jnp.sum(dp * p, axis=-1, keepdims=True))
            dsb = ds.astype(MXU_DT)
            dq_ref[:, lo:lo + MEM_HEAD_DIM] = _dot(dsb, kh) * scale
            dkv_ref[:, lo:lo + MEM_HEAD_DIM] += _dot_tn(dsb, qh) * scale
            dkv_ref[:, d + lo:d + lo + MEM_HEAD_DIM] += _dot_tn(p.astype(MXU_DT), da)

    row = pl.BlockSpec((tm, d), lambda i: (i, 0))
    full = pl.BlockSpec(kv.shape, lambda i: (0, 0))
    return pl.pallas_call(
        body, name=name, grid=(t // tm,),
        in_specs=[row, full, row],
        out_specs=[row, full],
        out_shape=[jax.ShapeDtypeStruct((t, d), F32), jax.ShapeDtypeStruct(kv.shape, F32)],
        compiler_params=_cparams(("arbitrary",)),
    )(q, kv, datt)


def _halo(k):
    return 8 * ((k - 1 + 7) // 8)


def dwconv_fwd(u, w, bias, *, col0=0, width=None, name):
    t = u.shape[0]
    kk, c = w.shape
    width = c if width is None else width
    assert width == c and col0 % c == 0
    cb = col0 // c
    hb = _halo(kk)
    tm = _tile(t, 512, hb)
    r = tm // hb
    has_bias = bias is not None

    def body(*refs):
        if has_bias:
            prev_ref, cur_ref, w_ref, b_ref, o_ref, scr = refs
        else:
            prev_ref, cur_ref, w_ref, o_ref, scr = refs
        i = pl.program_id(0)
        scr[0:hb, :] = jnp.where(i == 0, 0.0, prev_ref[...])
        scr[hb:hb + tm, :] = cur_ref[...]
        acc = jnp.zeros((tm, c), F32)
        for k in range(kk):
            acc = acc + w_ref[k:k + 1, :] * scr[pl.ds(hb - (kk - 1) + k, tm), :]
        if has_bias:
            acc = acc + b_ref[...]
        o_ref[...] = acc

    in_specs = [pl.BlockSpec((hb, c), lambda i: (jnp.maximum(i * r - 1, 0), cb)),
                pl.BlockSpec((tm, c), lambda i: (i, cb)),
                pl.BlockSpec((kk, c), lambda i: (0, 0))]
    args = [u, u, w]
    if has_bias:
        in_specs.append(pl.BlockSpec((1, c), lambda i: (0, 0)))
        args.append(bias)
    return pl.pallas_call(
        body, name=name, grid=(t // tm,),
        in_specs=in_specs,
        out_specs=pl.BlockSpec((tm, c), lambda i: (i, 0)),
        out_shape=jax.ShapeDtypeStruct((t, c), F32),
        scratch_shapes=[pltpu.VMEM((hb + tm, c), F32)],
        compiler_params=_cparams(("parallel",)),
    )(*args)


def dwconv_bwd(dc, u, w, *, col0=0, name):
    t, c = dc.shape
    kk = w.shape[0]
    assert col0 % c == 0
    cb = col0 // c
    hb = _halo(kk)
    tm = _tile(t, 512, hb)
    r = tm // hb
    n = t // tm

    def body(dcur_ref, dnext_ref, uprev_ref, ucur_ref, w_ref, du_ref, dw_ref, db_ref, sd, su):
        i = pl.program_id(0)

        @pl.when(i == 0)
        def _():
            dw_ref[...] = jnp.zeros_like(dw_ref)
            db_ref[...] = jnp.zeros_like(db_ref)

        dcur = dcur_ref[...]
        sd[0:tm, :] = dcur
        sd[tm:tm + hb, :] = jnp.where(i == n - 1, 0.0, dnext_ref[...])
        su[0:hb, :] = jnp.where(i == 0, 0.0, uprev_ref[...])
        su[hb:hb + tm, :] = ucur_ref[...]
        acc = jnp.zeros((tm, c), F32)
        for k in range(kk):
            acc = acc + w_ref[k:k + 1, :] * sd[pl.ds(kk - 1 - k, tm), :]
            dw_ref[k:k + 1, :] += jnp.sum(dcur * su[pl.ds(hb - (kk - 1) + k, tm), :], axis=0, keepdims=True)
        du_ref[...] = acc
        db_ref[...] += jnp.sum(dcur, axis=0, keepdims=True)

    return pl.pallas_call(
        body, name=name, grid=(n,),
        in_specs=[pl.BlockSpec((tm, c), lambda i: (i, 0)),
                  pl.BlockSpec((hb, c), lambda i: (jnp.minimum((i + 1) * r, n * r - 1), 0)),
                  pl.BlockSpec((hb, c), lambda i: (jnp.maximum(i * r - 1, 0), cb)),
                  pl.BlockSpec((tm, c), lambda i: (i, cb)),
                  pl.BlockSpec((kk, c), lambda i: (0, 0))],
        out_specs=[pl.BlockSpec((tm, c), lambda i: (i, 0)),
                   pl.BlockSpec((kk, c), lambda i: (0, 0)),
                   pl.BlockSpec((1, c), lambda i: (0, 0))],
        out_shape=[jax.ShapeDtypeStruct((t, c), F32), jax.ShapeDtypeStruct((kk, c), F32),
                   jax.ShapeDtypeStruct((1, c), F32)],
        scratch_shapes=[pltpu.VMEM((tm + hb, c), F32), pltpu.VMEM((hb + tm, c), F32)],
        compiler_params=_cparams(("arbitrary",)),
    )(dc, dc, u, u, w)


def glu_fwd(proj, *, name):
    t = proj.shape[0]
    c = GROUP_WIDTH
    tm = _tile(t, 1024, 8)
    vb, gb = P_CONF // c, P_CONF // c + 1

    def body(v_ref, g_ref, o_ref):
        o_ref[...] = v_ref[...] * _sigmoid(g_ref[...])

    return pl.pallas_call(
        body, name=name, grid=(t // tm,),
        in_specs=[pl.BlockSpec((tm, c), lambda i: (i, vb)), pl.BlockSpec((tm, c), lambda i: (i, gb))],
        out_specs=pl.BlockSpec((tm, c), lambda i: (i, 0)),
        out_shape=jax.ShapeDtypeStruct((t, c), F32),
        compiler_params=_cparams(("parallel",)),
    )(proj, proj)


def glu_bwd(proj, du, *, name):
    t = proj.shape[0]
    c = GROUP_WIDTH
    tm = _tile(t, 1024, 8)
    vb, gb = P_CONF // c, P_CONF // c + 1

    def body(v_ref, g_ref, du_ref, o_ref):
        sg = _sigmoid(g_ref[...])
        du = du_ref[...]
        o_ref[:, :c] = du * sg
        o_ref[:, c:] = du * v_ref[...] * sg * (1.0 - sg)

    return pl.pallas_call(
        body, name=name, grid=(t // tm,),
        in_specs=[pl.BlockSpec((tm, c), lambda i: (i, vb)), pl.BlockSpec((tm, c), lambda i: (i, gb)),
                  pl.BlockSpec((tm, c), lambda i: (i, 0))],
        out_specs=pl.BlockSpec((tm, 2 * c), lambda i: (i, 0)),
        out_shape=jax.ShapeDtypeStruct((t, 2 * c), F32),
        compiler_params=_cparams(("parallel",)),
    )(proj, proj, du)


def _group_mean_matrix(c, groups):
    gsz = c // groups
    ri = lax.broadcasted_iota(jnp.int32, (c, c), 0) // gsz
    ci = lax.broadcasted_iota(jnp.int32, (c, c), 1) // gsz
    return jnp.where(ri == ci, 1.0 / gsz, 0.0).astype(F32)


def gn_silu_fwd(cx, gamma, beta, *, name):
    t, c = cx.shape
    tm = _tile(t, 1024, 8)

    def body(c_ref, g_ref, b_ref, o_ref):
        gm = _group_mean_matrix(c, CONF_GROUPS)
        x = c_ref[...]
        mu = _dot_hi(x, gm)
        xc = x - mu
        var = _dot_hi(xc * xc, gm)
        a = xc * lax.rsqrt(var + LN_EPS) * g_ref[...] + b_ref[...]
        o_ref[...] = a * _sigmoid(a)

    row = pl.BlockSpec((tm, c), lambda i: (i, 0))
    vec = pl.BlockSpec((1, c), lambda i: (0, 0))
    return pl.pallas_call(
        body, name=name, grid=(t // tm,),
        in_specs=[row, vec, vec], out_specs=row,
        out_shape=jax.ShapeDtypeStruct((t, c), F32),
        compiler_params=_cparams(("parallel",)),
    )(cx, gamma, beta)


def gn_silu_bwd(cx, gamma, beta, dy, *, name):
    t, c = cx.shape
    tm = _tile(t, 1024, 8)

    def body(c_ref, g_ref, b_ref, dy_ref, dc_ref, dg_ref, db_ref):
        @pl.when(pl.program_id(0) == 0)
        def _():
            dg_ref[...] = jnp.zeros_like(dg_ref)
            db_ref[...] = jnp.zeros_like(db_ref)

        gm = _group_mean_matrix(c, CONF_GROUPS)
        x = c_ref[...]
        mu = _dot_hi(x, gm)
        xc = x - mu
        var = _dot_hi(xc * xc, gm)
        rstd = lax.rsqrt(var + LN_EPS)
        nrm = xc * rstd
        a = nrm * g_ref[...] + b_ref[...]
        sa = _sigmoid(a)
        da = dy_ref[...] * sa * (1.0 + a * (1.0 - sa))
        dg_ref[...] += jnp.sum(da * nrm, axis=0, keepdims=True)
        db_ref[...] += jnp.sum(da, axis=0, keepdims=True)
        dn = da * g_ref[...]
        dc_ref[...] = rstd * (dn - _dot_hi(dn, gm) - nrm * _dot_hi(dn * nrm, gm))

    row = pl.BlockSpec((tm, c), lambda i: (i, 0))
    vec = pl.BlockSpec((1, c), lambda i: (0, 0))
    return pl.pallas_call(
        body, name=name, grid=(t // tm,),
        in_specs=[row, vec, vec, row], out_specs=[row, vec, vec],
        out_shape=[jax.ShapeDtypeStruct((t, c), F32), jax.ShapeDtypeStruct((1, c), F32),
                   jax.ShapeDtypeStruct((1, c), F32)],
        compiler_params=_cparams(("arbitrary",)),
    )(cx, gamma, beta, dy)


FOX_COL = 8
SMALL_BLK = P_SMALL // LANES


def _log_sigmoid(x):
    return jnp.minimum(x, 0.0) - jnp.log(1.0 + jnp.exp(-jnp.abs(x)))


def _fox_cols(shape):
    col = lax.broadcasted_iota(jnp.int32, shape, 1)
    return (col >= FOX_COL) & (col < FOX_COL + GROUP_HEADS)


def fox_gate_fwd(proj, bvec, *, name):
    t = proj.shape[0]
    tm = _tile(t, 256, 8)

    def body(s_ref, b_ref, o_ref, carry):
        @pl.when(pl.program_id(0) == 0)
        def _():
            carry[...] = jnp.zeros_like(carry)

        lf = jnp.where(_fox_cols((tm, LANES)), _log_sigmoid(s_ref[...] + b_ref[...]), 0.0)
        ri = lax.broadcasted_iota(jnp.int32, (tm, tm), 0)
        ci = lax.broadcasted_iota(jnp.int32, (tm, tm), 1)
        cum = _dot_hi(jnp.where(ri >= ci, 1.0, 0.0).astype(F32), lf) + carry[...]
        o_ref[...] = cum
        carry[...] = cum[tm - 1:tm, :]

    return pl.pallas_call(
        body, name=name, grid=(t // tm,),
        in_specs=[pl.BlockSpec((tm, LANES), lambda i: (i, SMALL_BLK)), pl.BlockSpec((1, LANES), lambda i: (0, 0))],
        out_specs=pl.BlockSpec((tm, LANES), lambda i: (i, 0)),
        out_shape=jax.ShapeDtypeStruct((t, LANES), F32),
        scratch_shapes=[pltpu.VMEM((1, LANES), F32)],
        compiler_params=_cparams(("arbitrary",)),
    )(proj, bvec)


def fox_gate_bwd(dcum, proj, bvec, *, name):
    t = proj.shape[0]
    tm = _tile(t, 256, 8)
    n = t // tm

    def body(d_ref, s_ref, b_ref, o_ref, db_ref, carry):
        @pl.when(pl.program_id(0) == 0)
        def _():
            carry[...] = jnp.zeros_like(carry)
            db_ref[...] = jnp.zeros_like(db_ref)

        ri = lax.broadcasted_iota(jnp.int32, (tm, tm), 0)
        ci = lax.broadcasted_iota(jnp.int32, (tm, tm), 1)
        dlf = _dot_hi(jnp.where(ri <= ci, 1.0, 0.0).astype(F32), d_ref[...]) + carry[...]
        carry[...] = dlf[0:1, :]
        x = s_ref[...] + b_ref[...]
        dx = jnp.where(_fox_cols((tm, LANES)), dlf * (1.0 - _sigmoid(x)), 0.0)
        o_ref[...] = dx
        db_ref[...] += jnp.sum(dx, axis=0, keepdims=True)

    return pl.pallas_call(
        body, name=name, grid=(n,),
        in_specs=[pl.BlockSpec((tm, LANES), lambda i: (n - 1 - i, 0)),
                  pl.BlockSpec((tm, LANES), lambda i: (n - 1 - i, SMALL_BLK)),
                  pl.BlockSpec((1, LANES), lambda i: (0, 0))],
        out_specs=[pl.BlockSpec((tm, LANES), lambda i: (n - 1 - i, 0)), pl.BlockSpec((1, LANES), lambda i: (0, 0))],
        out_shape=[jax.ShapeDtypeStruct((t, LANES), F32), jax.ShapeDtypeStruct((1, LANES), F32)],
        scratch_shapes=[pltpu.VMEM((1, LANES), F32)],
        compiler_params=_cparams(("arbitrary",)),
    )(dcum, proj, bvec)


def _head_masks(c):
    lane_head = lax.broadcasted_iota(jnp.int32, (1, c), 1) // HEAD_DIM
    return [lane_head == h for h in range(GROUP_HEADS)]


def _attn_tiles(t, tq, tk):
    tq = _tile(t, tq, 8)
    tk = _tile(t, tk, LANES)
    return tq, tk, t // tq, t // tk


def fox_fwd(proj, cum, cum_t, *, name, tq=512, tk=512):
    t = proj.shape[0]
    c = GROUP_WIDTH
    tq, tk, nq, nk = _attn_tiles(t, tq, tk)
    qb = P_FOX // c
    scale = HEAD_DIM ** -0.5

    def last_j(i):
        return ((i + 1) * tq - 1) // tk

    def body(q_ref, k_ref, v_ref, cc_ref, cr_ref, o_ref, lse_ref, m_scr, l_scr, acc_scr):
        i = pl.program_id(0)
        j = pl.program_id(1)
        masks = _head_masks(c)

        @pl.when(j == 0)
        def _():
            m_scr[...] = jnp.full_like(m_scr, NEG_BIG)
            l_scr[...] = jnp.zeros_like(l_scr)
            acc_scr[...] = jnp.zeros_like(acc_scr)

        @pl.when(j <= last_j(i))
        def _():
            q = q_ref[...]
            kb = k_ref[...].astype(MXU_DT)
            vb = v_ref[...].astype(MXU_DT)
            row = i * tq + lax.broadcasted_iota(jnp.int32, (tq, tk), 0)
            col = j * tk + lax.broadcasted_iota(jnp.int32, (tq, tk), 1)
            causal = col <= row
            acc = acc_scr[...]
            for h in range(GROUP_HEADS):
                qh = jnp.where(masks[h], q, 0.0).astype(MXU_DT)
                s = _dot_nt(qh, kb) * scale + (cc_ref[:, FOX_COL + h:FOX_COL + h + 1] - cr_ref[h:h + 1, :])
                s = jnp.where(causal, s, NEG_BIG)
                m_old = m_scr[h]
                m_new = jnp.maximum(m_old, jnp.max(s, axis=-1, keepdims=True))
                p = jnp.exp(s - m_new)
                alpha = jnp.exp(m_old - m_new)
                l_scr[h] = alpha * l_scr[h] + jnp.sum(p, axis=-1, keepdims=True)
                m_scr[h] = m_new
                acc = jnp.where(masks[h], alpha * acc + _dot(p.astype(MXU_DT), vb), acc)
            acc_scr[...] = acc

        @pl.when(j == last_j(i))
        def _():
            acc = acc_scr[...]
            o = jnp.zeros_like(acc)
            lse = jnp.zeros((tq, LANES), F32)
            lane = lax.broadcasted_iota(jnp.int32, (1, LANES), 1)
            for h in range(GROUP_HEADS):
                o = jnp.where(masks[h], acc / l_scr[h], o)
                lse = jnp.where(lane == h, m_scr[h] + jnp.log(l_scr[h]), lse)
            o_ref[...] = o
            lse_ref[...] = lse

    def kvmap(blk):
        return lambda i, j: (jnp.minimum(j, last_j(i)), blk)

    return pl.pallas_call(
        body, name=name, grid=(nq, nk),
        in_specs=[pl.BlockSpec((tq, c), lambda i, j: (i, qb)),
                  pl.BlockSpec((tk, c), kvmap(qb + 1)),
                  pl.BlockSpec((tk, c), kvmap(qb + 2)),
                  pl.BlockSpec((tq, LANES), lambda i, j: (i, 0)),
                  pl.BlockSpec((8, tk), lambda i, j: (0, jnp.minimum(j, last_j(i))))],
        out_specs=[pl.BlockSpec((tq, c), lambda i, j: (i, 0)), pl.BlockSpec((tq, LANES), lambda i, j: (i, 0))],
        out_shape=[jax.ShapeDtypeStruct((t, c), F32), jax.ShapeDtypeStruct((t, LANES), F32)],
        scratch_shapes=[pltpu.VMEM((GROUP_HEADS, tq, 1), F32), pltpu.VMEM((GROUP_HEADS, tq, 1), F32),
                        pltpu.VMEM((tq, c), F32)],
        compiler_params=_cparams(("parallel", "arbitrary")),
    )(proj, proj, proj, cum, cum_t)


def fox_bwd(proj, cum, cum_t, o, lse, do, *, name, tq=512, tk=512):
    t = proj.shape[0]
    c = GROUP_WIDTH
    tq, tk, nq, nk = _attn_tiles(t, tq, tk)
    qb = P_FOX // c
    scale = HEAD_DIM ** -0.5

    def last_j(i):
        return ((i + 1) * tq - 1) // tk

    def body(q_ref, k_ref, v_ref, cc_ref, cr_ref, o_ref, lse_ref, do_ref,
             dq_ref, dk_ref, dv_ref, dcc_ref, dcr_ref, dq_scr, rs_scr):
        i = pl.program_id(0)
        j = pl.program_id(1)
        masks = _head_masks(c)

        @pl.when((i == 0) & (j == 0))
        def _():
            dk_ref[...] = jnp.zeros_like(dk_ref)
            dv_ref[...] = jnp.zeros_like(dv_ref)
            dcr_ref[...] = jnp.zeros_like(dcr_ref)

        @pl.when(j == 0)
        def _():
            dq_scr[...] = jnp.zeros_like(dq_scr)
            rs_scr[...] = jnp.zeros_like(rs_scr)

        @pl.when(j <= last_j(i))
        def _():
            q = q_ref[...]
            qf = q.astype(MXU_DT)
            kb = k_ref[...].astype(MXU_DT)
            vb = v_ref[...].astype(MXU_DT)
            do = do_ref[...]
            dob = do.astype(MXU_DT)
            doo = do * o_ref[...]
            row = i * tq + lax.broadcasted_iota(jnp.int32, (tq, tk), 0)
            col = j * tk + lax.broadcasted_iota(jnp.int32, (tq, tk), 1)
            causal = col <= row
            dq = dq_scr[...]
            dk_upd = jnp.zeros((tk, c), F32)
            dv_upd = jnp.zeros((tk, c), F32)
            for h in range(GROUP_HEADS):
                qh = jnp.where(masks[h], q, 0.0).astype(MXU_DT)
                s = _dot_nt(qh, kb) * scale + (cc_ref[:, FOX_COL + h:FOX_COL + h + 1] - cr_ref[h:h + 1, :])
                p = jnp.where(causal, jnp.exp(s - lse_ref[:, h:h + 1]), 0.0)
                delta = jnp.sum(jnp.where(masks[h], doo, 0.0), axis=-1, keepdims=True)
                doh = jnp.where(masks[h], do, 0.0).astype(MXU_DT)
                ds = p * (_dot_nt(doh, vb) - delta)
                dsb = ds.astype(MXU_DT)
                dq = jnp.where(masks[h], dq + _dot(dsb, kb) * scale, dq)
                dk_upd = jnp.where(masks[h], _dot_tn(dsb, qf) * scale, dk_upd)
                dv_upd = jnp.where(masks[h], _dot_tn(p.astype(MXU_DT), dob), dv_upd)
                dcr_ref[j, h:h + 1, :] += -jnp.sum(ds, axis=0, keepdims=True)
                rs_scr[h] += jnp.sum(ds, axis=-1, keepdims=True)
            dq_scr[...] = dq
            rows = pl.ds(pl.multiple_of(j * tk, tk), tk)
            dk_ref[rows, :] += dk_upd
            dv_ref[rows, :] += dv_upd

        @pl.when(j == last_j(i))
        def _():
            dq_ref[...] = dq_scr[...]
            lane = lax.broadcasted_iota(jnp.int32, (1, LANES), 1)
            dcc = jnp.zeros((tq, LANES), F32)
            for h in range(GROUP_HEADS):
                dcc = jnp.where(lane == FOX_COL + h, rs_scr[h], dcc)
            dcc_ref[...] = dcc

    def kvmap(blk):
        return lambda i, j: (jnp.minimum(j, last_j(i)), blk)

    qrow = lambda i, j: (i, 0)
    whole = lambda i, j: (0, 0)
    return pl.pallas_call(
        body, name=name, grid=(nq, nk),
        in_specs=[pl.BlockSpec((tq, c), lambda i, j: (i, qb)),
                  pl.BlockSpec((tk, c), kvmap(qb + 1)),
                  pl.BlockSpec((tk, c), kvmap(qb + 2)),
                  pl.BlockSpec((tq, LANES), qrow),
                  pl.BlockSpec((8, tk), lambda i, j: (0, jnp.minimum(j, last_j(i)))),
                  pl.BlockSpec((tq, c), qrow),
                  pl.BlockSpec((tq, LANES), qrow),
                  pl.BlockSpec((tq, c), qrow)],
        out_specs=[pl.BlockSpec((tq, c), qrow),
                   pl.BlockSpec((t, c), whole),
                   pl.BlockSpec((t, c), whole),
                   pl.BlockSpec((tq, LANES), qrow),
                   pl.BlockSpec((nk, 8, tk), lambda i, j: (0, 0, 0))],
        out_shape=[jax.ShapeDtypeStruct((t, c), F32), jax.ShapeDtypeStruct((t, c), F32),
                   jax.ShapeDtypeStruct((t, c), F32), jax.ShapeDtypeStruct((t, LANES), F32),
                   jax.ShapeDtypeStruct((nk, 8, tk), F32)],
        scratch_shapes=[pltpu.VMEM((tq, c), F32), pltpu.VMEM((GROUP_HEADS, tq, 1), F32)],
        compiler_params=_cparams(("arbitrary", "arbitrary")),
    )(proj, proj, proj, cum, cum_t, o, lse, do)


def _sb_logs(z, strict):
    tt = jnp.log(1.0 + jnp.exp(-jnp.abs(z)))
    log_keep = jnp.where(strict, -(jnp.maximum(z, 0.0) + tt), 0.0)
    log_beta = jnp.minimum(z, 0.0) - tt
    return log_keep, log_beta


def _tri(n, upper):
    a = lax.broadcasted_iota(jnp.int32, (n, n), 0)
    b = lax.broadcasted_iota(jnp.int32, (n, n), 1)
    return jnp.where((a < b) if upper else (a > b), 1.0, 0.0).astype(MXU_DT)


def sb_fwd(proj, *, name, tq=512, tk=256):
    t = proj.shape[0]
    c = GROUP_WIDTH
    tq, tk, nq, nk = _attn_tiles(t, tq, tk)
    qb = P_SB // c
    scale = HEAD_DIM ** -0.5

    def last_j(i):
        return ((i + 1) * tq - 1) // tk

    def body(q_ref, k_ref, v_ref, o_ref, rs_ref, r_scr, acc_scr):
        i = pl.program_id(0)
        jj = pl.program_id(1)
        masks = _head_masks(c)

        @pl.when(jj == 0)
        def _():
            r_scr[...] = jnp.zeros_like(r_scr)
            acc_scr[...] = jnp.zeros_like(acc_scr)

        @pl.when(jj <= last_j(i))
        def _():
            j = last_j(i) - jj
            q = q_ref[...]
            kb = k_ref[...].astype(MXU_DT)
            vb = v_ref[...].astype(MXU_DT)
            row = i * tq + lax.broadcasted_iota(jnp.int32, (tq, tk), 0)
            col = j * tk + lax.broadcasted_iota(jnp.int32, (tq, tk), 1)
            strict = col < row
            later = _tri(tk, upper=False)
            lane = lax.broadcasted_iota(jnp.int32, (1, LANES), 1)
            acc = acc_scr[...]
            rs = jnp.zeros((tq, LANES), F32)
            for h in range(GROUP_HEADS):
                qh = jnp.where(masks[h], q, 0.0).astype(MXU_DT)
                z = _dot_nt(qh, kb) * scale
                log_keep, log_beta = _sb_logs(z, strict)
                r_old = r_scr[h]
                rest = r_old + _split_dot(log_keep, later)
                w = jnp.where(strict, jnp.exp(log_beta + rest), 0.0)
                acc = jnp.where(masks[h], acc + _dot(w.astype(MXU_DT), vb), acc)
                rs = jnp.where(lane == h, r_old, rs)
                r_scr[h] = r_old + jnp.sum(log_keep, axis=-1, keepdims=True)
            acc_scr[...] = acc
            rs_ref[0] = rs

        @pl.when(jj == last_j(i))
        def _():
            o_ref[...] = acc_scr[...]

    def kvmap(blk):
        return lambda i, jj: (jnp.maximum(last_j(i) - jj, 0), blk)

    return pl.pallas_call(
        body, name=name, grid=(nq, nk),
        in_specs=[pl.BlockSpec((tq, c), lambda i, jj: (i, qb)),
                  pl.BlockSpec((tk, c), kvmap(qb + 1)),
                  pl.BlockSpec((tk, c), kvmap(qb + 2))],
        out_specs=[pl.BlockSpec((tq, c), lambda i, jj: (i, 0)),
                   pl.BlockSpec((1, tq, LANES), lambda i, jj: (jnp.maximum(last_j(i) - jj, 0), i, 0))],
        out_shape=[jax.ShapeDtypeStruct((t, c), F32), jax.ShapeDtypeStruct((nk, t, LANES), F32)],
        scratch_shapes=[pltpu.VMEM((GROUP_HEADS, tq, 1), F32), pltpu.VMEM((tq, c), F32)],
        compiler_params=_cparams(("parallel", "arbitrary")),
    )(proj, proj, proj)


def sb_bwd(proj, rsave, do, *, name, tq=512, tk=256):
    t = proj.shape[0]
    c = GROUP_WIDTH
    tq, tk, nq, nk = _attn_tiles(t, tq, tk)
    qb = P_SB // c
    scale = HEAD_DIM ** -0.5

    def last_j(i):
        return ((i + 1) * tq - 1) // tk

    def body(q_ref, k_ref, v_ref, rs_ref, do_ref, dq_ref, dk_ref, dv_ref, e_scr, dq_scr):
        i = pl.program_id(0)
        j = pl.program_id(1)
        masks = _head_masks(c)

        @pl.when((i == 0) & (j == 0))
        def _():
            dk_ref[...] = jnp.zeros_like(dk_ref)
            dv_ref[...] = jnp.zeros_like(dv_ref)

        @pl.when(j == 0)
        def _():
            e_scr[...] = jnp.zeros_like(e_scr)
            dq_scr[...] = jnp.zeros_like(dq_scr)

        @pl.when(j <= last_j(i))
        def _():
            q = q_ref[...]
            qf = q.astype(MXU_DT)
            kb = k_ref[...].astype(MXU_DT)
            vb = v_ref[...].astype(MXU_DT)
            do = do_ref[...]
            dob = do.astype(MXU_DT)
            row = i * tq + lax.broadcasted_iota(jnp.int32, (tq, tk), 0)
            col = j * tk + lax.broadcasted_iota(jnp.int32, (tq, tk), 1)
            strict = col < row
            later = _tri(tk, upper=False)
            earlier = _tri(tk, upper=True)
            rs = rs_ref[0]
            dq = dq_scr[...]
            dk_upd = jnp.zeros((tk, c), F32)
            dv_upd = jnp.zeros((tk, c), F32)
            for h in range(GROUP_HEADS):
                qh = jnp.where(masks[h], q, 0.0).astype(MXU_DT)
                z = _dot_nt(qh, kb) * scale
                log_keep, log_beta = _sb_logs(z, strict)
                rest = rs[:, h:h + 1] + _split_dot(log_keep, later)
                w = jnp.where(strict, jnp.exp(log_beta + rest), 0.0)
                doh = jnp.where(masks[h], do, 0.0).astype(MXU_DT)
                e = w * _dot_nt(doh, vb)
                e_old = e_scr[h]
                dkeep = e_old + _split_dot(e, earlier)
                dz = jnp.where(strict, e * jnp.exp(log_keep) - dkeep * jnp.exp(log_beta), 0.0)
                dzb = dz.astype(MXU_DT)
                dq = jnp.where(masks[h], dq + _dot(dzb, kb) * scale, dq)
                dk_upd = jnp.where(masks[h], _dot_tn(dzb, qf) * scale, dk_upd)
                dv_upd = jnp.where(masks[h], _dot_tn(w.astype(MXU_DT), dob), dv_upd)
                e_scr[h] = e_old + jnp.sum(e, axis=-1, keepdims=True)
            dq_scr[...] = dq
            rows = pl.ds(pl.multiple_of(j * tk, tk), tk)
            dk_ref[rows, :] += dk_upd
            dv_ref[rows, :] += dv_upd

        @pl.when(j == last_j(i))
        def _():
            dq_ref[...] = dq_scr[...]

    def kvmap(blk):
        return lambda i, j: (jnp.minimum(j, last_j(i)), blk)

    qrow = lambda i, j: (i, 0)
    whole = lambda i, j: (0, 0)
    return pl.pallas_call(
        body, name=name, grid=(nq, nk),
        in_specs=[pl.BlockSpec((tq, c), lambda i, j: (i, qb)),
                  pl.BlockSpec((tk, c), kvmap(qb + 1)),
                  pl.BlockSpec((tk, c), kvmap(qb + 2)),
                  pl.BlockSpec((1, tq, LANES), lambda i, j: (jnp.minimum(j, last_j(i)), i, 0)),
                  pl.BlockSpec((tq, c), qrow)],
        out_specs=[pl.BlockSpec((tq, c), qrow), pl.BlockSpec((t, c), whole), pl.BlockSpec((t, c), whole)],
        out_shape=[jax.ShapeDtypeStruct((t, c), F32)] * 3,
        scratch_shapes=[pltpu.VMEM((GROUP_HEADS, tq, 1), F32), pltpu.VMEM((tq, c), F32)],
        compiler_params=_cparams(("arbitrary", "arbitrary")),
    )(proj, proj, proj, rsave, do)


A_COL, B_COL = 0, 4
Z_BLK = P_Z // GROUP_WIDTH


def _dot_tn_hi(a, b):
    return lax.dot_general(a, b, (((0,), (0,)), ((), ())), preferred_element_type=F32, precision=HI)


def _silu(x):
    return x * _sigmoid(x)


def _dsilu(x):
    s = _sigmoid(x)
    return s * (1.0 + x * (1.0 - s))


def _head_sum(x, masks):
    out = jnp.zeros_like(x)
    for m in masks:
        out = jnp.where(m, jnp.sum(jnp.where(m, x, 0.0), axis=-1, keepdims=True), out)
    return out


def _expand(cols, col0, masks):
    out = jnp.zeros((cols.shape[0], GROUP_WIDTH), F32)
    for h, m in enumerate(masks):
        out = jnp.where(m, cols[:, col0 + h:col0 + h + 1], out)
    return out


def _reduce(x, col0, masks):
    lane = lax.broadcasted_iota(jnp.int32, (1, LANES), 1)
    out = jnp.zeros((x.shape[0], LANES), F32)
    for h, m in enumerate(masks):
        out = jnp.where(lane == col0 + h, jnp.sum(jnp.where(m, x, 0.0), axis=-1, keepdims=True), out)
    return out


def _block_ones():
    ri = lax.broadcasted_iota(jnp.int32, (GROUP_WIDTH, GROUP_WIDTH), 0) // HEAD_DIM
    ci = lax.broadcasted_iota(jnp.int32, (GROUP_WIDTH, GROUP_WIDTH), 1) // HEAD_DIM
    return jnp.where(ri == ci, 1.0, 0.0).astype(F32)


def _blk(x, hs):
    return jnp.concatenate([x] * GROUP_HEADS, axis=0) * hs


def _unblk(m, hs):
    mm = m * hs
    c = GDN_CHUNK
    return mm[0:c] + mm[c:2 * c] + mm[2 * c:3 * c] + mm[3 * c:4 * c]


def _row_mask4():
    ri = lax.broadcasted_iota(jnp.int32, (GROUP_WIDTH, LANES), 0) // HEAD_DIM
    ci = lax.broadcasted_iota(jnp.int32, (GROUP_WIDTH, LANES), 1)
    return jnp.where(ri + A_COL == ci, 1.0, 0.0).astype(F32)


def _gdn_chunk(xc, small, avec, dtvec, state, masks, hs):
    c = GDN_CHUNK
    w = GROUP_WIDTH
    b16 = lambda v: v.astype(MXU_DT)
    f = {}
    xq, xk, xv = xc[:, :w], xc[:, w:2 * w], xc[:, 2 * w:]
    qs, ks, v = _silu(xq), _silu(xk), _silu(xv)
    rq = lax.rsqrt(_head_sum(qs * qs, masks) + L2_EPS)
    rk = lax.rsqrt(_head_sum(ks * ks, masks) + L2_EPS)
    qn = qs * rq
    k = ks * rk
    q = qn * (HEAD_DIM ** -0.5)
    xg = small + dtvec
    sp = jnp.maximum(xg, 0.0) + jnp.log(1.0 + jnp.exp(-jnp.abs(xg)))
    g128 = -avec * sp
    beta128 = _sigmoid(small)
    ri = lax.broadcasted_iota(jnp.int32, (c, c), 0)
    ci = lax.broadcasted_iota(jnp.int32, (c, c), 1)
    tril = jnp.where(ri >= ci, 1.0, 0.0).astype(F32)
    gam128 = _dot_hi(tril, g128)
    gam = _expand(gam128, A_COL, masks)
    bfull = _expand(beta128, B_COL, masks)
    mask4 = _row_mask4()
    ones = jnp.ones((c, LANES), F32)
    gam_row = _dot_nt_hi(ones, jnp.concatenate([gam128] * GROUP_HEADS, axis=0) * mask4)
    li = lax.broadcasted_iota(jnp.int32, (c, w), 0)
    lj = lax.broadcasted_iota(jnp.int32, (c, w), 1) % HEAD_DIM
    incl = li >= lj
    strict = li > lj
    dmat = jnp.exp(jnp.where(incl, gam - gam_row, NEG_BIG))
    egam = jnp.exp(gam)
    glast = gam[c - 1:c, :]
    ekd = jnp.exp(glast - gam)
    kb = k * bfull
    vb = v * bfull
    kbg = kb * egam
    qd = q * egam
    kd = k * ekd
    kblk = b16(_blk(k, hs))
    araw = _dot_nt(b16(kb), kblk)
    a = jnp.where(strict, araw * dmat, 0.0)
    tm = jnp.where(li == lj, 1.0, 0.0) - a
    p = a
    for _ in range(5):
        p = _dot_hi(p, _blk(p, hs))
        tm = tm + _dot_hi(tm, _blk(p, hs))
    tm16 = b16(tm)
    u = _dot(tm16, b16(_blk(vb, hs)))
    wm = _dot(tm16, b16(_blk(kbg, hs)))
    qk = _dot_nt(b16(q), kblk)
    aqk = jnp.where(incl, qk * dmat, 0.0)
    s16 = b16(state)
    vn = u - _dot(b16(wm), s16)
    o = _dot(b16(qd), s16) + _dot(b16(aqk), b16(_blk(vn, hs)))
    s_new = state * jnp.exp(glast) + hs * _dot_tn(b16(kd), b16(vn))
    f.update(xq=xq, xk=xk, xv=xv, v=v, rq=rq, rk=rk, qn=qn, k=k, q=q, xg=xg, g128=g128, beta128=beta128,
             tril=tril, gam=gam, bfull=bfull, mask4=mask4, ones=ones, incl=incl, strict=strict, li=li,
             dmat=dmat, egam=egam, glast=glast, ekd=ekd, kb=kb, vb=vb, kbg=kbg, qd=qd, kd=kd, kblk=kblk,
             araw=araw, tm=tm, tm16=tm16, wm=wm, qk=qk, aqk=aqk, s16=s16, vn=vn, o=o, s_new=s_new)
    return f


def _decay_rate(a_log):
    lane = lax.broadcasted_iota(jnp.int32, a_log.shape, 1)
    return jnp.where((lane >= A_COL) & (lane < A_COL + GROUP_HEADS), jnp.exp(a_log), 0.0)


def _gdn_post(o, z, ng, masks):
    r = lax.rsqrt(_head_sum(o * o, masks) * (1.0 / HEAD_DIM) + RMS_EPS)
    on = o * r
    return on, r, on * ng * _silu(z)


def gdn_fwd(cqkv, proj, avec, dtvec, ng, *, name):
    t = cqkv.shape[0]
    c = GDN_CHUNK
    w = GROUP_WIDTH
    n = t // c

    def body(x_ref, z_ref, sm_ref, a_ref, dt_ref, ng_ref, y_ref, st_ref, s_scr):
        @pl.when(pl.program_id(0) == 0)
        def _():
            s_scr[...] = jnp.zeros_like(s_scr)

        masks = _head_masks(w)
        hs = _block_ones()
        state = s_scr[...]
        st_ref[0] = state
        f = _gdn_chunk(x_ref[...], sm_ref[...], _decay_rate(a_ref[...]), dt_ref[...], state, masks, hs)
        _, _, y = _gdn_post(f["o"], z_ref[...], ng_ref[...], masks)
        y_ref[...] = y
        s_scr[...] = f["s_new"]

    vec = pl.BlockSpec((1, LANES), lambda i: (0, 0))
    return pl.pallas_call(
        body, name=name, grid=(n,),
        in_specs=[pl.BlockSpec((c, 3 * w), lambda i: (i, 0)),
                  pl.BlockSpec((c, w), lambda i: (i, Z_BLK)),
                  pl.BlockSpec((c, LANES), lambda i: (i, SMALL_BLK)),
                  vec, vec, pl.BlockSpec((1, w), lambda i: (0, 0))],
        out_specs=[pl.BlockSpec((c, w), lambda i: (i, 0)), pl.BlockSpec((1, w, w), lambda i: (i, 0, 0))],
        out_shape=[jax.ShapeDtypeStruct((t, w), F32), jax.ShapeDtypeStruct((n, w, w), F32)],
        scratch_shapes=[pltpu.VMEM((w, w), F32)],
        compiler_params=_cparams(("arbitrary",)),
    )(cqkv, proj, proj, avec, dtvec, ng)


def gdn_bwd(cqkv, proj, avec, dtvec, ng, states, dy, *, name):
    t = cqkv.shape[0]
    c = GDN_CHUNK
    w = GROUP_WIDTH
    n = t // c
    b16 = lambda v: v.astype(MXU_DT)

    def body(x_ref, z_ref, sm_ref, a_ref, dt_ref, ng_ref, st_ref, dy_ref,
             dx_ref, dz_ref, dsm_ref, dng_ref, dal_ref, ddt_ref, ds_scr):
        @pl.when(pl.program_id(0) == 0)
        def _():
            ds_scr[...] = jnp.zeros_like(ds_scr)
            dng_ref[...] = jnp.zeros_like(dng_ref)
            dal_ref[...] = jnp.zeros_like(dal_ref)
            ddt_ref[...] = jnp.zeros_like(ddt_ref)

        masks = _head_masks(w)
        hs = _block_ones()
        state = st_ref[0]
        avec_v = _decay_rate(a_ref[...])
        f = _gdn_chunk(x_ref[...], sm_ref[...], avec_v, dt_ref[...], state, masks, hs)
        z = z_ref[...]
        ng_v = ng_ref[...]
        dy_v = dy_ref[...]
        on, r, _ = _gdn_post(f["o"], z, ng_v, masks)
        sz = _silu(z)
        dz_ref[...] = dy_v * on * ng_v * _dsilu(z)
        d_on = dy_v * ng_v * sz
        dng_ref[...] += jnp.sum(dy_v * on * sz, axis=0, keepdims=True)
        do = r * (d_on - on * _head_sum(d_on * on, masks) * (1.0 / HEAD_DIM))
        do16 = b16(do)
        dsn = ds_scr[...]
        dsn16 = b16(dsn)
        s16, vn, kd, qd, wm = f["s16"], f["vn"], f["kd"], f["qd"], f["wm"]
        k, q, kblk, tm, tm16 = f["k"], f["q"], f["kblk"], f["tm"], f["tm16"]
        dmat, egam, glast, gam = f["dmat"], f["egam"], f["glast"], f["gam"]
        incl, strict, li = f["incl"], f["strict"], f["li"]
        vn16 = b16(vn)
        dvn = _unblk(_dot_tn(b16(f["aqk"]), do16), hs) + _dot(b16(kd), dsn16)
        daqk = jnp.where(incl, _dot_nt(do16, b16(_blk(vn, hs))), 0.0)
        dqd = _dot_nt(do16, s16)
        dvn16 = b16(dvn)
        ds_scr[...] = hs * (_dot_tn(b16(qd), do16) - _dot_tn(b16(wm), dvn16)) + dsn * jnp.exp(glast)
        dkd = _dot_nt(vn16, dsn16)
        dglast = jnp.sum(dsn * state, axis=0, keepdims=True) * jnp.exp(glast)
        du16 = dvn16
        dw16 = b16(-_dot_nt(dvn16, s16))
        dqk16 = b16(daqk * dmat)
        ddm = daqk * f["qk"]
        dq = _dot(dqk16, kblk)
        dk = _unblk(_dot_tn(dqk16, b16(q)), hs)
        dtm = _dot_nt(du16, b16(_blk(f["vb"], hs))) + _dot_nt(dw16, b16(_blk(f["kbg"], hs)))
        dvb = _unblk(_dot_tn(tm16, du16), hs)
        dkbg = _unblk(_dot_tn(tm16, dw16), hs)
        xx = _unblk(_dot_tn_hi(tm, dtm), hs)
        da = jnp.where(strict, -_dot_nt_hi(xx, _blk(tm, hs)), 0.0)
        daraw16 = b16(da * dmat)
        ddm = ddm + da * f["araw"]
        dkb = _dot(daraw16, kblk)
        dk = dk + _unblk(_dot_tn(daraw16, b16(f["kb"])), hs)
        tcol = ddm * dmat
        dgam = tcol
        dgam128_row = _dot_tn_hi(-tcol, f["ones"]) * f["mask4"]
        dgam128_row = (dgam128_row[0:c] + dgam128_row[c:2 * c] + dgam128_row[2 * c:3 * c] + dgam128_row[3 * c:4 * c])
        dk = dk + dkd * f["ekd"]
        tt = dkd * kd
        dgam = dgam - tt
        dglast = dglast + jnp.sum(tt, axis=0, keepdims=True)
        dq = dq + dqd * egam
        dgam = dgam + dqd * qd
        dkb = dkb + dkbg * egam
        dgam = dgam + dkbg * f["kbg"]
        dk = dk + dkb * f["bfull"]
        dbf = dkb * k + dvb * f["v"]
        dv = dvb * f["bfull"]
        dgam = dgam + jnp.where(li == c - 1, dglast, 0.0)
        beta128 = f["beta128"]
        db128 = _reduce(dbf, B_COL, masks) * beta128 * (1.0 - beta128)
        dgam128 = _reduce(dgam, A_COL, masks) + dgam128_row
        dg128 = _dot_tn_hi(f["tril"], dgam128)
        dxg = dg128 * (-avec_v * _sigmoid(f["xg"]))
        lane = lax.broadcasted_iota(jnp.int32, (1, LANES), 1)
        dsm_ref[...] = jnp.where(lane < B_COL, dxg, db128)
        ddt_ref[...] += jnp.sum(dxg, axis=0, keepdims=True)
        dal_ref[...] += jnp.sum(dg128 * f["g128"], axis=0, keepdims=True)
        dqn = dq * (HEAD_DIM ** -0.5)
        dqs = f["rq"] * (dqn - f["qn"] * _head_sum(dqn * f["qn"], masks))
        dks = f["rk"] * (dk - k * _head_sum(dk * k, masks))
        dx_ref[:, :w] = dqs * _dsilu(f["xq"])
        dx_ref[:, w:2 * w] = dks * _dsilu(f["xk"])
        dx_ref[:, 2 * w:] = dv * _dsilu(f["xv"])

    vec = pl.BlockSpec((1, LANES), lambda i: (0, 0))
    rev = lambda blk: (lambda i: (n - 1 - i, blk))
    return pl.pallas_call(
        body, name=name, grid=(n,),
        in_specs=[pl.BlockSpec((c, 3 * w), rev(0)),
                  pl.BlockSpec((c, w), rev(Z_BLK)),
                  pl.BlockSpec((c, LANES), rev(SMALL_BLK)),
                  vec, vec, pl.BlockSpec((1, w), lambda i: (0, 0)),
                  pl.BlockSpec((1, w, w), lambda i: (n - 1 - i, 0, 0)),
                  pl.BlockSpec((c, w), rev(0))],
        out_specs=[pl.BlockSpec((c, 3 * w), rev(0)), pl.BlockSpec((c, w), rev(0)), pl.BlockSpec((c, LANES), rev(0)),
                   pl.BlockSpec((1, w), lambda i: (0, 0)), vec, vec],
        out_shape=[jax.ShapeDtypeStruct((t, 3 * w), F32), jax.ShapeDtypeStruct((t, w), F32),
                   jax.ShapeDtypeStruct((t, LANES), F32), jax.ShapeDtypeStruct((1, w), F32),
                   jax.ShapeDtypeStruct((1, LANES), F32), jax.ShapeDtypeStruct((1, LANES), F32)],
        scratch_shapes=[pltpu.VMEM((w, w), F32)],
        compiler_params=_cparams(("arbitrary",)),
    )(cqkv, proj, proj, avec, dtvec, ng, states, dy)


def adamw(w, m, v, gslots, *, row0=0, name):
    r, c = w.shape
    s = gslots.shape[0]
    tr = _tile(r, 64, 8)
    assert row0 % tr == 0 and gslots.shape[2] == c
    rb = row0 // tr
    c1 = 1.0 - ADAM_B1 ** ADAM_STEP
    c2 = 1.0 - ADAM_B2 ** ADAM_STEP

    def body(w_ref, m_ref, v_ref, gs_ref, g_ref, d_ref, mo_ref, vo_ref):
        g = gs_ref[0]
        for k in range(1, s):
            g = g + gs_ref[k]
        m_new = ADAM_B1 * m_ref[...] + (1.0 - ADAM_B1) * g
        v_new = ADAM_B2 * v_ref[...] + (1.0 - ADAM_B2) * (g * g)
        m_hat = m_new / c1
        v_hat = v_new / c2
        g_ref[...] = g
        mo_ref[...] = m_new
        vo_ref[...] = v_new
        d_ref[...] = -ADAM_LR * (m_hat / (jnp.sqrt(v_hat) + ADAM_EPS) + ADAM_WD * w_ref[...])

    row = pl.BlockSpec((tr, c), lambda i: (i, 0))
    return pl.pallas_call(
        body, name=name, grid=(r // tr,),
        in_specs=[row, row, row, pl.BlockSpec((s, tr, c), lambda i: (0, rb + i, 0))],
        out_specs=[row] * 4,
        out_shape=[jax.ShapeDtypeStruct((r, c), F32)] * 4,
        compiler_params=_cparams(("parallel",)),
    )(w, m, v, gslots)


def slot_sum(slots, *, name):
    s, r, c = slots.shape

    def body(s_ref, o_ref):
        acc = s_ref[0]
        for k in range(1, s):
            acc = acc + s_ref[k]
        o_ref[...] = acc

    return pl.pallas_call(
        body, name=name, grid=(1,),
        in_specs=[pl.BlockSpec((s, r, c), lambda i: (0, 0, 0))],
        out_specs=pl.BlockSpec((r, c), lambda i: (0, 0)),
        out_shape=jax.ShapeDtypeStruct((r, c), F32),
        compiler_params=_cparams(("arbitrary",)),
    )(slots)


def exchange(srcs, *, broadcast, name):
    n = len(srcs)
    out_shapes = [jax.ShapeDtypeStruct(((N_DEV,) + s.shape) if broadcast else s.shape, s.dtype) for s in srcs]

    def body(*refs):
        src_refs, out_refs = refs[:n], refs[n:2 * n]
        send_sem, recv_sem, loc_sem = refs[2 * n:]
        x, y, c = lax.axis_index("x"), lax.axis_index("y"), lax.axis_index("c")
        me = 4 * x + 2 * y + c
        local = []
        for a in range(n):
            src = src_refs[a] if broadcast else src_refs[a].at[me]
            cp = pltpu.make_async_copy(src, out_refs[a].at[me], loc_sem.at[a])
            cp.start()
            local.append(cp)
        for d in range(1, N_DEV):
            px, py, pc = x ^ ((d >> 2) & 1), y ^ ((d >> 1) & 1), c ^ (d & 1)
            peer = 4 * px + 2 * py + pc
            for a in range(n):
                src = src_refs[a] if broadcast else src_refs[a].at[peer]
                pltpu.make_async_remote_copy(
                    src_ref=src, dst_ref=out_refs[a].at[me],
                    send_sem=send_sem.at[a], recv_sem=recv_sem.at[a],
                    device_id=(px, py, pc), device_id_type=pl.DeviceIdType.MESH).start()
        for a in range(n):
            seven = out_refs[a].at[pl.ds(0, N_DEV - 1)]
            pltpu.make_async_remote_copy(
                src_ref=seven, dst_ref=seven, send_sem=send_sem.at[a], recv_sem=recv_sem.at[a],
                device_id=(x, y, c), device_id_type=pl.DeviceIdType.MESH).wait()
            local[a].wait()

    anyspec = pl.BlockSpec(memory_space=pl.ANY)
    return pl.pallas_call(
        body, name=name,
        in_specs=[anyspec] * n, out_specs=[anyspec] * n, out_shape=out_shapes,
        scratch_shapes=[pltpu.SemaphoreType.DMA((n,)), pltpu.SemaphoreType.DMA((n,)), pltpu.SemaphoreType.DMA((n,))],
        compiler_params=pltpu.CompilerParams(has_side_effects=True),
    )(*srcs)


def _pack(arrs):
    flat = []
    for a in arrs:
        f = a.reshape(-1).astype(F32)
        flat.append(jnp.pad(f, (0, (-f.shape[0]) % LANES)))
    buf = jnp.concatenate(flat)
    buf = jnp.pad(buf, (0, (-buf.shape[0]) % (8 * LANES)))
    return buf.reshape(-1, LANES)


def _unpack(buf, shapes):
    flat = buf.reshape(-1)
    out, off = [], 0
    for s in shapes:
        sz = int(np.prod(s))
        out.append(flat[off:off + sz].reshape(s))
        off += sz + (-sz) % LANES
    return out


def _win_to_aligned(w):
    o = np.cumsum((0,) + IN_SPLITS)
    seg = lambda i: w[..., o[i]:o[i + 1]]
    pad = jnp.zeros(w.shape[:-1] + (P_WIDTH - IN_WIDTH,), w.dtype)
    return jnp.concatenate([seg(0), seg(1), seg(4), seg(6), seg(7), seg(2), seg(3), seg(5), pad], axis=-1)


def _win_from_aligned(w):
    o = np.cumsum((0,) + IN_SPLITS)
    s = P_SMALL
    return jnp.concatenate([w[..., P_GDN:P_GDN + 768], w[..., P_Z:P_Z + 256], w[..., s:s + 4], w[..., s + 4:s + 8],
                            w[..., P_FOX:P_FOX + 768], w[..., s + 8:s + 12], w[..., P_CONF:P_CONF + 512],
                            w[..., P_SB:P_SB + 768]], axis=-1)


def _row128(vals, col0):
    return jnp.pad(vals.astype(F32)[None, :], ((0, 0), (col0, LANES - col0 - GROUP_HEADS)))


def _ffn_fwd(x, wgu, wd, g, b, tag):
    gu = mm(x, wgu, name=f"{tag}_gu", tm=1024, tn=512, tk=1024)
    h = act_fwd(gu, name=f"{tag}_act")
    y = mm(h, wd, name=f"{tag}_down", tm=512, tn=512, tk=D_FF)
    out, xh, rs = ln_res_fwd(x, y, g, b, 0.5, name=f"{tag}_ln")
    return out, (x, gu, h, xh, rs)


def _ffn_bwd(dout, saved, wgu, wd, g, tag):
    x, gu, h, xh, rs = saved
    dz, dg, db = ln_res_bwd(dout, xh, rs, g, name=f"{tag}_ln_bwd")
    dh = mm(dz, wd, mode="nt", alpha=0.5, name=f"{tag}_dh", tm=512, tn=D_FF // 2, tk=1024)
    dgu = act_bwd(gu, dh, name=f"{tag}_act_bwd")
    dwd = mm(h, dz, mode="tn", alpha=0.5, name=f"{tag}_dwd", tm=D_FF // 2, tn=1024, tk=512)
    dwgu = mm(x, dgu, mode="tn", name=f"{tag}_dwgu", tm=1024, tn=D_FF // 2, tk=512)
    dx = mm(dgu, wgu, mode="nt", add=dz, beta=DN_ALPHA, name=f"{tag}_dx", tm=512, tn=1024, tk=D_FF // 2)
    return dx, dwgu, dwd, dg, db


def _layer_fwd(x, mem, w, tag):
    sv = {}
    x1, sv["ffn1"] = _ffn_fwd(x, w["gu1"], w["d1"], w["ln_ffn1_g"], w["ln_ffn1_b"], f"{tag}_ffn1")
    proj = mm(x1, w["win"], name=f"{tag}_inproj", tm=1024, tn=640, tk=1024)
    cqkv = dwconv_fwd(proj, w["gdn_conv_w"], None, col0=P_GDN, name=f"{tag}_gdn_conv")
    ya, states = gdn_fwd(cqkv, proj, w["alog"], w["dtb"], w["ng"], name=f"{tag}_gdn")
    cum = fox_gate_fwd(proj, w["bf"], name=f"{tag}_fox_gate")
    cum_t = jnp.pad(cum[:, FOX_COL:FOX_COL + GROUP_HEADS].T, ((0, 8 - GROUP_HEADS), (0, 0)))
    yb, lse = fox_fwd(proj, cum, cum_t, name=f"{tag}_fox")
    u = glu_fwd(proj, name=f"{tag}_glu")
    cc = dwconv_fwd(u, w["conf_dw_w"], w["conf_dw_b"], name=f"{tag}_conf_conv")
    yc = gn_silu_fwd(cc, w["conf_norm_g"], w["conf_norm_b"], name=f"{tag}_conf_norm")
    yd, rsave = sb_fwd(proj, name=f"{tag}_sb")
    ycat = jnp.concatenate([ya, yb, yc, yd], axis=1)
    mix = mm(ycat, w["wout"], name=f"{tag}_outproj")
    x2, xh2, rs2 = ln_res_fwd(x1, mix, w["ln_mix_g"], w["ln_mix_b"], 1.0, name=f"{tag}_ln_mix")
    sv["mix"] = (x1, proj, cqkv, states, cum, cum_t, yb, lse, u, cc, rsave, ycat, xh2, rs2)
    q = mm(x2, w["wq"], name=f"{tag}_memq")
    kv = mm(mem, w["wkv"], name=f"{tag}_memkv", tm=N_MEM)
    att = memattn_fwd(q, kv, name=f"{tag}_memattn")
    mo = mm(att, w["wo"], name=f"{tag}_memo")
    x3, xh3, rs3 = ln_res_fwd(x2, mo, w["ln_mem_g"], w["ln_mem_b"], 1.0, name=f"{tag}_ln_mem")
    sv["mem"] = (x2, q, kv, att, xh3, rs3)
    x4, sv["ffn2"] = _ffn_fwd(x3, w["gu2"], w["d2"], w["ln_ffn2_g"], w["ln_ffn2_b"], f"{tag}_ffn2")
    return x4, sv


def _layer_bwd(dx4, mem, sv, w, tag):
    t = dx4.shape[0]
    gr = {}
    dx3, gr["gu2"], gr["d2"], gr["ln_ffn2_g"], gr["ln_ffn2_b"] = _ffn_bwd(
        dx4, sv["ffn2"], w["gu2"], w["d2"], w["ln_ffn2_g"], f"{tag}_ffn2")
    x2, q, kv, att, xh3, rs3 = sv["mem"]
    dz, gr["ln_mem_g"], gr["ln_mem_b"] = ln_res_bwd(dx3, xh3, rs3, w["ln_mem_g"], name=f"{tag}_ln_mem_bwd")
    datt = mm(dz, w["wo"], mode="nt", name=f"{tag}_datt")
    gr["wo"] = mm(att, dz, mode="tn", name=f"{tag}_dwo", tk=512)
    dq, dkv = memattn_bwd(q, kv, datt, name=f"{tag}_memattn_bwd")
    gr["wq"] = mm(x2, dq, mode="tn", name=f"{tag}_dwq", tk=512)
    gr["wkv"] = mm(mem, dkv, mode="tn", name=f"{tag}_dwkv", tk=N_MEM)
    dx2 = mm(dq, w["wq"], mode="nt", add=dz, beta=DN_ALPHA, name=f"{tag}_dx2")
    x1, proj, cqkv, states, cum, cum_t, yb, lse, u, cc, rsave, ycat, xh2, rs2 = sv["mix"]
    dz, gr["ln_mix_g"], gr["ln_mix_b"] = ln_res_bwd(dx2, xh2, rs2, w["ln_mix_g"], name=f"{tag}_ln_mix_bwd")
    dycat = mm(dz, w["wout"], mode="nt", name=f"{tag}_dycat")
    gr["wout"] = mm(ycat, dz, mode="tn", name=f"{tag}_dwout", tk=512)
    gw = GROUP_WIDTH
    dya, dyb, dyc, dyd = (dycat[:, i * gw:(i + 1) * gw] for i in range(4))
    dq_d, dk_d, dv_d = sb_bwd(proj, rsave, dyd, name=f"{tag}_sb_bwd")
    dcc, gr["conf_norm_g"], gr["conf_norm_b"] = gn_silu_bwd(cc, w["conf_norm_g"], w["conf_norm_b"], dyc,
                                                            name=f"{tag}_conf_norm_bwd")
    du, gr["conf_dw_w"], gr["conf_dw_b"] = dwconv_bwd(dcc, u, w["conf_dw_w"], name=f"{tag}_conf_conv_bwd")
    dglu = glu_bwd(proj, du, name=f"{tag}_glu_bwd")
    dq_b, dk_b, dv_b, dcc, dcr = fox_bwd(proj, cum, cum_t, yb, lse, dyb, name=f"{tag}_fox_bwd")
    dcum = dcc + jnp.pad(dcr[:, :GROUP_HEADS, :].transpose(0, 2, 1).reshape(t, GROUP_HEADS),
                   ((0, 0), (FOX_COL, LANES - FOX_COL - GROUP_HEADS)))
    dsm_f, dbf = fox_gate_bwd(dcum, proj, w["bf"], name=f"{tag}_fox_gate_bwd")
    gr["fox_b_f"] = dbf[0, FOX_COL:FOX_COL + GROUP_HEADS]
    dcq, dz_a, dsm_a, dng, dal, ddt = gdn_bwd(cqkv, proj, w["alog"], w["dtb"], w["ng"], states, dya,
                                              name=f"{tag}_gdn_bwd")
    gr["gdn_norm_g"] = dng.reshape(GROUP_HEADS, HEAD_DIM).sum(0)
    gr["gdn_a_log"] = dal[0, A_COL:A_COL + GROUP_HEADS]
    gr["gdn_dt_bias"] = ddt[0, A_COL:A_COL + GROUP_HEADS]
    dgq, gr["gdn_conv_w"], _ = dwconv_bwd(dcq, proj, w["gdn_conv_w"], col0=P_GDN, name=f"{tag}_gdn_conv_bwd")
    dproj = jnp.concatenate([dgq, dz_a, dq_b, dk_b, dv_b, dglu, dq_d, dk_d, dv_d, dsm_a + dsm_f], axis=1)
    gr["win"] = mm(x1, dproj, mode="tn", name=f"{tag}_dwin", tm=1024, tn=640, tk=512)
    dx1 = mm(dproj, w["win"], mode="nt", add=dz, beta=DN_ALPHA, name=f"{tag}_dx1", tm=512, tn=1024, tk=640)
    dx0, gr["gu1"], gr["d1"], gr["ln_ffn1_g"], gr["ln_ffn1_b"] = _ffn_bwd(
        dx1, sv["ffn1"], w["gu1"], w["d1"], w["ln_ffn1_g"], f"{tag}_ffn1")
    return dx0, gr


SMALL_REPLICATED = ("ln_ffn1_g", "ln_ffn1_b", "gdn_a_log", "gdn_dt_bias", "gdn_norm_g", "fox_b_f", "conf_dw_b",
                    "conf_norm_g", "conf_norm_b", "ln_mix_g", "ln_mix_b", "ln_mem_g", "ln_mem_b", "ln_ffn2_g",
                    "ln_ffn2_b")
SMALL_SHARDED = ("gdn_conv_w", "conf_dw_w")
BIG = ("ffn1_w_gate", "ffn1_w_up", "ffn1_w_down", "w_in", "w_out", "mem_w_q", "mem_w_kv", "mem_w_o",
       "ffn2_w_gate", "ffn2_w_up", "ffn2_w_down")
WEIGHT_ORDER = ("ffn1_w_gate", "ffn1_w_up", "ffn1_w_down", "ln_ffn1_g", "ln_ffn1_b", "w_in", "gdn_conv_w", "gdn_a_log",
                "gdn_dt_bias", "gdn_norm_g", "fox_b_f", "conf_dw_w", "conf_dw_b", "conf_norm_g", "conf_norm_b", "w_out",
                "ln_mix_g", "ln_mix_b", "mem_w_q", "mem_w_kv", "mem_w_o", "ln_mem_g", "ln_mem_b", "ffn2_w_gate",
                "ffn2_w_up", "ffn2_w_down", "ln_ffn2_g", "ln_ffn2_b")


def _step(x, mem, loss_target, wts, ms, vs):
    me = 4 * lax.axis_index("x") + 2 * lax.axis_index("y") + lax.axis_index("c")
    x = x[0]
    mem = mem[0]
    target = loss_target[0]
    dff_s = D_FF // N_DEV

    gu_sh = jnp.stack([wts[k] for k in ("ffn1_w_gate", "ffn1_w_up", "ffn2_w_gate", "ffn2_w_up")]).astype(MXU_DT)
    dn_sh = jnp.stack([wts["ffn1_w_down"], wts["ffn2_w_down"]]).astype(MXU_DT)
    in_sh = _win_to_aligned(wts["w_in"]).astype(MXU_DT)
    sq_sh = jnp.stack([wts["w_out"], wts["mem_w_q"], wts["mem_w_o"]]).astype(MXU_DT)
    kv_sh = wts["mem_w_kv"].astype(MXU_DT)
    sm_sh = _pack([wts["gdn_conv_w"], wts["conf_dw_w"]])
    gu_g, dn_g, in_g, sq_g, kv_g, sm_g = exchange([gu_sh, dn_sh, in_sh, sq_sh, kv_sh, sm_sh], broadcast=True,
                                                  name="gather_weights")
    gu_full = gu_g.transpose(1, 2, 3, 0, 4).reshape(4, DEPTH, D_MODEL, D_FF)
    dn_full = dn_g.transpose(1, 2, 0, 3, 4).reshape(2, DEPTH, D_FF, D_MODEL)
    in_full = in_g.transpose(1, 0, 2, 3).reshape(DEPTH, D_MODEL, P_WIDTH)
    sq_full = sq_g.transpose(1, 2, 0, 3, 4).reshape(3, DEPTH, D_MODEL, D_MODEL)
    kv_full = kv_g.transpose(1, 2, 0, 3).reshape(DEPTH, D_MODEL, 2 * D_MODEL)
    conv_shapes = [wts["gdn_conv_w"].shape, wts["conf_dw_w"].shape]
    parts = [_unpack(sm_g[j], conv_shapes) for j in range(N_DEV)]
    gconv_full = jnp.concatenate([p[0] for p in parts], axis=-1)
    cconv_full = jnp.concatenate([p[1] for p in parts], axis=-1)

    def layer_weights(l):
        row = lambda k: wts[k][l][None, :]
        w = dict(gu1=jnp.concatenate([gu_full[0, l], gu_full[1, l]], axis=1), d1=dn_full[0, l],
                 gu2=jnp.concatenate([gu_full[2, l], gu_full[3, l]], axis=1), d2=dn_full[1, l],
                 win=in_full[l], wout=sq_full[0, l], wq=sq_full[1, l], wo=sq_full[2, l], wkv=kv_full[l],
                 gdn_conv_w=gconv_full[l], conf_dw_w=cconv_full[l],
                 alog=_row128(wts["gdn_a_log"][l], A_COL), dtb=_row128(wts["gdn_dt_bias"][l], A_COL),
                 bf=_row128(wts["fox_b_f"][l], FOX_COL), ng=jnp.tile(wts["gdn_norm_g"][l], GROUP_HEADS)[None, :])
        for k in ("ln_ffn1_g", "ln_ffn1_b", "conf_dw_b", "conf_norm_g", "conf_norm_b", "ln_mix_g", "ln_mix_b",
                  "ln_mem_g", "ln_mem_b", "ln_ffn2_g", "ln_ffn2_b"):
            w[k] = row(k)
        return w

    lw = [layer_weights(l) for l in range(DEPTH)]
    saved = []
    h = x
    for l in range(DEPTH):
        h, sv = _layer_fwd(h, mem, lw[l], f"l{l}")
        saved.append(sv)
    dh, lpart = loss_head(h, target, name="loss_head")
    grads = [None] * DEPTH
    for l in reversed(range(DEPTH)):
        dh, grads[l] = _layer_bwd(dh, mem, saved[l], lw[l], f"l{l}")
    grad_x = dh[None]

    def gl(k):
        return jnp.stack([grads[l][k] for l in range(DEPTH)])

    def cols_to_chunks(a):
        return a.reshape(DEPTH, a.shape[1], N_DEV, -1).transpose(2, 0, 1, 3)

    def rows_to_chunks(a):
        return a.reshape(DEPTH, N_DEV, -1, a.shape[2]).transpose(1, 0, 2, 3)

    gu1, gu2 = gl("gu1"), gl("gu2")
    gu_c = jnp.stack([cols_to_chunks(gu1[..., :D_FF]), cols_to_chunks(gu1[..., D_FF:]),
                      cols_to_chunks(gu2[..., :D_FF]), cols_to_chunks(gu2[..., D_FF:])], axis=1)
    dn_c = jnp.stack([rows_to_chunks(gl("d1")), rows_to_chunks(gl("d2"))], axis=1)
    in_c = rows_to_chunks(gl("win"))
    sq_c = jnp.stack([rows_to_chunks(gl(k)) for k in ("wout", "wq", "wo")], axis=1)
    kv_c = cols_to_chunks(gl("wkv"))
    gu_r, dn_r, in_r, sq_r, kv_r = exchange([gu_c, dn_c, in_c, sq_c, kv_c], broadcast=False, name="scatter_grads")

    small_names = SMALL_REPLICATED + SMALL_SHARDED
    small_grads = [gl(k) for k in small_names] + [lpart[0, :1]]
    sm_all, = exchange([_pack(small_grads)], broadcast=True, name="gather_small_grads")
    sm_sum = slot_sum(sm_all, name="sum_small_grads")
    sm_g = _unpack(sm_sum, [g.shape for g in small_grads])
    loss = sm_g[-1][0]
    small_g = dict(zip(small_names, sm_g[:-1]))
    for k in SMALL_SHARDED:
        width = wts[k].shape[-1]
        small_g[k] = lax.dynamic_slice_in_dim(small_g[k], me * width, width, axis=2)

    out_g, out_d, out_m, out_v = {}, {}, {}, {}

    def update(k, gslots, row0):
        shp = wts[k].shape
        two = lambda a: a.reshape(-1, shp[-1])
        g, d, mn, vn = adamw(two(wts[k]), two(ms[k]), two(vs[k]), gslots, row0=row0, name=f"adamw_{k}")
        out_g[k], out_d[k], out_m[k], out_v[k] = (a.reshape(shp) for a in (g, d, mn, vn))

    gu_r2 = gu_r.reshape(N_DEV, 4 * DEPTH * D_MODEL, dff_s)
    for i, k in enumerate(("ffn1_w_gate", "ffn1_w_up", "ffn2_w_gate", "ffn2_w_up")):
        update(k, gu_r2, i * DEPTH * D_MODEL)
    dn_r2 = dn_r.reshape(N_DEV, 2 * DEPTH * dff_s, D_MODEL)
    for i, k in enumerate(("ffn1_w_down", "ffn2_w_down")):
        update(k, dn_r2, i * DEPTH * dff_s)
    rows_s = D_MODEL // N_DEV
    in_r2 = _win_from_aligned(in_r).reshape(N_DEV, DEPTH * rows_s, IN_WIDTH)
    update("w_in", in_r2, 0)
    sq_r2 = sq_r.reshape(N_DEV, 3 * DEPTH * rows_s, D_MODEL)
    for i, k in enumerate(("w_out", "mem_w_q", "mem_w_o")):
        update(k, sq_r2, i * DEPTH * rows_s)
    update("mem_w_kv", kv_r.reshape(N_DEV, DEPTH * D_MODEL, 2 * D_MODEL // N_DEV), 0)

    sw = _pack([wts[k] for k in small_names])
    smm = _pack([ms[k] for k in small_names])
    smv = _pack([vs[k] for k in small_names])
    sg = _pack([small_g[k] for k in small_names])
    res = adamw(sw, smm, smv, sg[None], name="adamw_small")
    shapes = [wts[k].shape for k in small_names]
    for dst, buf in zip((out_g, out_d, out_m, out_v), res):
        for k, a in zip(small_names, _unpack(buf, shapes)):
            dst[k] = a

    return (loss, grad_x, *[out_g[k] for k in WEIGHT_ORDER], *[out_d[k] for k in WEIGHT_ORDER],
            *[out_m[k] for k in WEIGHT_ORDER], *[out_v[k] for k in WEIGHT_ORDER])


def kernel(x, mem, ffn1_w_gate, ffn1_w_up, ffn1_w_down, ln_ffn1_g, ln_ffn1_b, w_in, gdn_conv_w, gdn_a_log, gdn_dt_bias, gdn_norm_g, fox_b_f, conf_dw_w, conf_dw_b, conf_norm_g, conf_norm_b, w_out, ln_mix_g, ln_mix_b, mem_w_q, mem_w_kv, mem_w_o, ln_mem_g, ln_mem_b, ffn2_w_gate, ffn2_w_up, ffn2_w_down, ln_ffn2_g, ln_ffn2_b, loss_target, m_ffn1_w_gate, m_ffn1_w_up, m_ffn1_w_down, m_ln_ffn1_g, m_ln_ffn1_b, m_w_in, m_gdn_conv_w, m_gdn_a_log, m_gdn_dt_bias, m_gdn_norm_g, m_fox_b_f, m_conf_dw_w, m_conf_dw_b, m_conf_norm_g, m_conf_norm_b, m_w_out, m_ln_mix_g, m_ln_mix_b, m_mem_w_q, m_mem_w_kv, m_mem_w_o, m_ln_mem_g, m_ln_mem_b, m_ffn2_w_gate, m_ffn2_w_up, m_ffn2_w_down, m_ln_ffn2_g, m_ln_ffn2_b, v_ffn1_w_gate, v_ffn1_w_up, v_ffn1_w_down, v_ln_ffn1_g, v_ln_ffn1_b, v_w_in, v_gdn_conv_w, v_gdn_a_log, v_gdn_dt_bias, v_gdn_norm_g, v_fox_b_f, v_conf_dw_w, v_conf_dw_b, v_conf_norm_g, v_conf_norm_b, v_w_out, v_ln_mix_g, v_ln_mix_b, v_mem_w_q, v_mem_w_kv, v_mem_w_o, v_ln_mem_g, v_ln_mem_b, v_ffn2_w_gate, v_ffn2_w_up, v_ffn2_w_down, v_ln_ffn2_g, v_ln_ffn2_b):
    args = locals()
    wts = {k: args[k] for k in WEIGHT_ORDER}
    ms = {k: args["m_" + k] for k in WEIGHT_ORDER}
    vs = {k: args["v_" + k] for k in WEIGHT_ORDER}
    return _step(x, mem, loss_target, wts, ms, vs)
```

```python
import functools
import math

import jax
import jax.numpy as jnp
import numpy as np
from jax import lax
from jax.experimental import pallas as pl
from jax.experimental.pallas import tpu as pltpu

F32 = jnp.float32
BF16 = jnp.bfloat16
MXU_DT = jnp.bfloat16
HI = lax.Precision.HIGHEST

N_DEV = 8
VMEM_LIMIT_BYTES = 56 * 1024 * 1024
LANES = 128

D_MODEL = 1024
DEPTH = 2
GROUP_WIDTH = 256
HEAD_DIM = 64
GROUP_HEADS = 4
D_FF = 2816
SHORT_CONV = 4
CONF_KERNEL = 31
CONF_GROUPS = 4
GDN_CHUNK = 64
N_MEM = 256
MEM_HEADS = 4
MEM_HEAD_DIM = 256
DN_ALPHA = float((2 * DEPTH) ** 0.25)
LN_EPS = 1e-5
RMS_EPS = 1e-6
L2_EPS = 1e-6
NEG_BIG = -1e30
IN_SPLITS = (768, 256, 4, 4, 768, 4, 512, 768)
IN_WIDTH = sum(IN_SPLITS)
P_GDN, P_Z, P_FOX, P_CONF, P_SB, P_SMALL = 0, 768, 1024, 1792, 2304, 3072
P_WIDTH = 3200

ADAM_LR = 0.001
ADAM_B1 = 0.9
ADAM_B2 = 0.999
ADAM_EPS = 1e-08
ADAM_WD = 0.01
ADAM_STEP = 10


def _cparams(sem):
    return pltpu.CompilerParams(dimension_semantics=sem, vmem_limit_bytes=VMEM_LIMIT_BYTES)


def _tile(n, pref, align=LANES):
    if n <= pref:
        return n
    t = (pref // align) * align
    while t >= align:
        if n % t == 0:
            return t
        t -= align
    return n


def mm(a, b, *, mode="nn", add=None, alpha=1.0, beta=1.0, out_dtype=F32, name,
       tm=1024, tn=512, tk=1024):
    if mode == "nn":
        (m, k), (k2, n) = a.shape, b.shape
    elif mode == "nt":
        (m, k), (n, k2) = a.shape, b.shape
    else:
        (k, m), (k2, n) = a.shape, b.shape
    assert k == k2, (a.shape, b.shape, mode)
    tm = _tile(m, tm, 8 if mode != "tn" else LANES)
    tn = _tile(n, tn)
    tk = _tile(k, tk, LANES if mode != "tn" else 8)
    nk = k // tk
    if mode == "nn":
        a_spec = pl.BlockSpec((tm, tk), lambda i, j, kk: (i, kk))
        b_spec = pl.BlockSpec((tk, tn), lambda i, j, kk: (kk, j))
        dims = (((1,), (0,)), ((), ()))
    elif mode == "nt":
        a_spec = pl.BlockSpec((tm, tk), lambda i, j, kk: (i, kk))
        b_spec = pl.BlockSpec((tn, tk), lambda i, j, kk: (j, kk))
        dims = (((1,), (1,)), ((), ()))
    else:
        a_spec = pl.BlockSpec((tk, tm), lambda i, j, kk: (kk, i))
        b_spec = pl.BlockSpec((tk, tn), lambda i, j, kk: (kk, j))
        dims = (((0,), (0,)), ((), ()))
    o_spec = pl.BlockSpec((tm, tn), lambda i, j, kk: (i, j))
    has_add = add is not None

    def body(*refs):
        if has_add:
            a_ref, b_ref, add_ref, o_ref, acc_ref = refs
        else:
            a_ref, b_ref, o_ref, acc_ref = refs
        kk = pl.program_id(2)

        @pl.when(kk == 0)
        def _():
            acc_ref[...] = jnp.zeros_like(acc_ref)

        acc_ref[...] += lax.dot_general(a_ref[...].astype(MXU_DT), b_ref[...].astype(MXU_DT), dims,
                                        preferred_element_type=F32)

        @pl.when(kk == nk - 1)
        def _():
            r = acc_ref[...]
            if alpha != 1.0:
                r = r * alpha
            if has_add:
                r = r + beta * add_ref[...].astype(F32)
            o_ref[...] = r.astype(out_dtype)

    in_specs = [a_spec, b_spec] + ([o_spec] if has_add else [])
    args = (a, b) + ((add,) if has_add else ())
    return pl.pallas_call(
        body, name=name, grid=(m // tm, n // tn, nk),
        in_specs=in_specs, out_specs=o_spec,
        out_shape=jax.ShapeDtypeStruct((m, n), out_dtype),
        scratch_shapes=[pltpu.VMEM((tm, tn), F32)],
        compiler_params=_cparams(("parallel", "parallel", "arbitrary")),
    )(*args)


def ln_res_fwd(x, y, g, b, s, *, name):
    t, d = x.shape
    tm = _tile(t, 512, 8)

    def body(x_ref, y_ref, g_ref, b_ref, o_ref, xh_ref, rs_ref):
        z = DN_ALPHA * x_ref[...] + s * y_ref[...]
        mu = jnp.mean(z, axis=-1, keepdims=True)
        zc = z - mu
        var = jnp.mean(zc * zc, axis=-1, keepdims=True)
        rstd = lax.rsqrt(var + LN_EPS)
        xh = zc * rstd
        xh_ref[...] = xh
        rs_ref[...] = rstd
        o_ref[...] = xh * g_ref[...] + b_ref[...]

    row = pl.BlockSpec((tm, d), lambda i: (i, 0))
    vec = pl.BlockSpec((1, d), lambda i: (0, 0))
    return pl.pallas_call(
        body, name=name, grid=(t // tm,),
        in_specs=[row, row, vec, vec],
        out_specs=[row, row, pl.BlockSpec((tm, 1), lambda i: (i, 0))],
        out_shape=[jax.ShapeDtypeStruct((t, d), F32), jax.ShapeDtypeStruct((t, d), F32),
                   jax.ShapeDtypeStruct((t, 1), F32)],
        compiler_params=_cparams(("parallel",)),
    )(x, y, g, b)


def ln_res_bwd(dout, xhat, rstd, g, *, name):
    t, d = dout.shape
    tm = _tile(t, 512, 8)

    def body(do_ref, xh_ref, rs_ref, g_ref, dz_ref, dg_ref, db_ref):
        i = pl.program_id(0)

        @pl.when(i == 0)
        def _():
            dg_ref[...] = jnp.zeros_like(dg_ref)
            db_ref[...] = jnp.zeros_like(db_ref)

        do = do_ref[...]
        xh = xh_ref[...]
        dxh = do * g_ref[...]
        m1 = jnp.mean(dxh, axis=-1, keepdims=True)
        m2 = jnp.mean(dxh * xh, axis=-1, keepdims=True)
        dz_ref[...] = rs_ref[...] * (dxh - m1 - xh * m2)
        dg_ref[...] += jnp.sum(do * xh, axis=0, keepdims=True)
        db_ref[...] += jnp.sum(do, axis=0, keepdims=True)

    row = pl.BlockSpec((tm, d), lambda i: (i, 0))
    vec = pl.BlockSpec((1, d), lambda i: (0, 0))
    return pl.pallas_call(
        body, name=name, grid=(t // tm,),
        in_specs=[row, row, pl.BlockSpec((tm, 1), lambda i: (i, 0)), vec],
        out_specs=[row, vec, vec],
        out_shape=[jax.ShapeDtypeStruct((t, d), F32), jax.ShapeDtypeStruct((1, d), F32),
                   jax.ShapeDtypeStruct((1, d), F32)],
        compiler_params=_cparams(("arbitrary",)),
    )(dout, xhat, rstd, g)


def _sigmoid(x):
    return 1.0 / (1.0 + jnp.exp(-x))


def act_fwd(gu, *, name):
    t, f2 = gu.shape
    f = f2 // 2
    tm = _tile(t, 256, 8)

    def body(gu_ref, h_ref):
        g = gu_ref[:, :f]
        h_ref[...] = (g * _sigmoid(g) * gu_ref[:, f:]).astype(h_ref.dtype)

    return pl.pallas_call(
        body, name=name, grid=(t // tm,),
        in_specs=[pl.BlockSpec((tm, f2), lambda i: (i, 0))],
        out_specs=pl.BlockSpec((tm, f), lambda i: (i, 0)),
        out_shape=jax.ShapeDtypeStruct((t, f), MXU_DT),
        compiler_params=_cparams(("parallel",)),
    )(gu)


def act_bwd(gu, dh, *, name):
    t, f2 = gu.shape
    f = f2 // 2
    tm = _tile(t, 256, 8)

    def body(gu_ref, dh_ref, o_ref):
        g = gu_ref[:, :f]
        u = gu_ref[:, f:]
        dh = dh_ref[...]
        sg = _sigmoid(g)
        o_ref[:, f:] = (dh * g * sg).astype(o_ref.dtype)
        o_ref[:, :f] = (dh * u * sg * (1.0 + g * (1.0 - sg))).astype(o_ref.dtype)

    return pl.pallas_call(
        body, name=name, grid=(t // tm,),
        in_specs=[pl.BlockSpec((tm, f2), lambda i: (i, 0)), pl.BlockSpec((tm, f), lambda i: (i, 0))],
        out_specs=pl.BlockSpec((tm, f2), lambda i: (i, 0)),
        out_shape=jax.ShapeDtypeStruct((t, f2), MXU_DT),
        compiler_params=_cparams(("parallel",)),
    )(gu, dh)


def loss_head(y, target, *, name):
    t, d = y.shape
    tm = _tile(t, 512, 8)

    def body(y_ref, t_ref, dy_ref, l_ref):
        i = pl.program_id(0)

        @pl.when(i == 0)
        def _():
            l_ref[...] = jnp.zeros_like(l_ref)

        err = y_ref[...] - t_ref[...]
        dy_ref[...] = err * (1.0 / d)
        part = jnp.sum(jnp.sum(err * err, axis=-1, keepdims=True), axis=0, keepdims=True)
        l_ref[...] += jnp.broadcast_to(part * (0.5 / d), l_ref.shape)

    row = pl.BlockSpec((tm, d), lambda i: (i, 0))
    return pl.pallas_call(
        body, name=name, grid=(t // tm,),
        in_specs=[row, row],
        out_specs=[row, pl.BlockSpec((1, LANES), lambda i: (0, 0))],
        out_shape=[jax.ShapeDtypeStruct((t, d), F32), jax.ShapeDtypeStruct((1, LANES), F32)],
        compiler_params=_cparams(("arbitrary",)),
    )(y, target)


def _dot(a, b):
    return lax.dot_general(a, b, (((1,), (0,)), ((), ())), preferred_element_type=F32)


def _dot_nt(a, b):
    return lax.dot_general(a, b, (((1,), (1,)), ((), ())), preferred_element_type=F32)


def _dot_tn(a, b):
    return lax.dot_general(a, b, (((0,), (0,)), ((), ())), preferred_element_type=F32)


def _dot_hi(a, b):
    return lax.dot_general(a, b, (((1,), (0,)), ((), ())), preferred_element_type=F32, precision=HI)


def _dot_nt_hi(a, b):
    return lax.dot_general(a, b, (((1,), (1,)), ((), ())), preferred_element_type=F32, precision=HI)


def _split_dot(x, u):
    hi = x.astype(MXU_DT)
    lo = (x - hi.astype(F32)).astype(MXU_DT)
    return _dot(hi, u) + _dot(lo, u)


def _mem_probs(q_ref, kv_ref, h):
    lo = h * MEM_HEAD_DIM
    qh = q_ref[:, lo:lo + MEM_HEAD_DIM].astype(MXU_DT)
    kh = kv_ref[:, lo:lo + MEM_HEAD_DIM].astype(MXU_DT)
    s = _dot_nt(qh, kh) * (MEM_HEAD_DIM ** -0.5)
    s = s - jnp.max(s, axis=-1, keepdims=True)
    p = jnp.exp(s)
    return p / jnp.sum(p, axis=-1, keepdims=True), qh, kh


def memattn_fwd(q, kv, *, name):
    t, d = q.shape
    tm = _tile(t, 512, 8)

    def body(q_ref, kv_ref, o_ref):
        for h in range(MEM_HEADS):
            lo = h * MEM_HEAD_DIM
            p, _, _ = _mem_probs(q_ref, kv_ref, h)
            vh = kv_ref[:, d + lo:d + lo + MEM_HEAD_DIM].astype(MXU_DT)
            o_ref[:, lo:lo + MEM_HEAD_DIM] = _dot(p.astype(MXU_DT), vh)

    return pl.pallas_call(
        body, name=name, grid=(t // tm,),
        in_specs=[pl.BlockSpec((tm, d), lambda i: (i, 0)), pl.BlockSpec(kv.shape, lambda i: (0, 0))],
        out_specs=pl.BlockSpec((tm, d), lambda i: (i, 0)),
        out_shape=jax.ShapeDtypeStruct((t, d), F32),
        compiler_params=_cparams(("parallel",)),
    )(q, kv)


def memattn_bwd(q, kv, datt, *, name):
    t, d = q.shape
    tm = _tile(t, 512, 8)
    scale = MEM_HEAD_DIM ** -0.5

    def body(q_ref, kv_ref, da_ref, dq_ref, dkv_ref):
        @pl.when(pl.program_id(0) == 0)
        def _():
            dkv_ref[...] = jnp.zeros_like(dkv_ref)

        for h in range(MEM_HEADS):
            lo = h * MEM_HEAD_DIM
            p, qh, kh = _mem_probs(q_ref, kv_ref, h)
            vh = kv_ref[:, d + lo:d + lo + MEM_HEAD_DIM].astype(MXU_DT)
            da = da_ref[:, lo:lo + MEM_HEAD_DIM].astype(MXU_DT)
            dp = _dot_nt(da, vh)
            ds = p * (dp - jnp.sum(dp * p, axis=-1, keepdims=True))
            dsb = ds.astype(MXU_DT)
            dq_ref[:, lo:lo + MEM_HEAD_DIM] = _dot(dsb, kh) * scale
            dkv_ref[:, lo:lo + MEM_HEAD_DIM] += _dot_tn(dsb, qh) * scale
            dkv_ref[:, d + lo:d + lo + MEM_HEAD_DIM] += _dot_tn(p.astype(MXU_DT), da)

    row = pl.BlockSpec((tm, d), lambda i: (i, 0))
    full = pl.BlockSpec(kv.shape, lambda i: (0, 0))
    return pl.pallas_call(
        body, name=name, grid=(t // tm,),
        in_specs=[row, full, row],
        out_specs=[row, full],
        out_shape=[jax.ShapeDtypeStruct((t, d), F32), jax.ShapeDtypeStruct(kv.shape, F32)],
        compiler_params=_cparams(("arbitrary",)),
    )(q, kv, datt)


def _halo(k):
    return 8 * ((k - 1 + 7) // 8)


def dwconv_fwd(u, w, bias, *, col0=0, width=None, name):
    t = u.shape[0]
    kk, c = w.shape
    width = c if width is None else width
    assert width == c and col0 % c == 0
    cb = col0 // c
    hb = _halo(kk)
    tm = _tile(t, 512, hb)
    r = tm // hb
    has_bias = bias is not None

    def body(*refs):
        if has_bias:
            prev_ref, cur_ref, w_ref, b_ref, o_ref, scr = refs
        else:
            prev_ref, cur_ref, w_ref, o_ref, scr = refs
        i = pl.program_id(0)
        scr[0:hb, :] = jnp.where(i == 0, 0.0, prev_ref[...])
        scr[hb:hb + tm, :] = cur_ref[...]
        acc = jnp.zeros((tm, c), F32)
        for k in range(kk):
            acc = acc + w_ref[k:k + 1, :] * scr[pl.ds(hb - (kk - 1) + k, tm), :]
        if has_bias:
            acc = acc + b_ref[...]
        o_ref[...] = acc

    in_specs = [pl.BlockSpec((hb, c), lambda i: (jnp.maximum(i * r - 1, 0), cb)),
                pl.BlockSpec((tm, c), lambda i: (i, cb)),
                pl.BlockSpec((kk, c), lambda i: (0, 0))]
    args = [u, u, w]
    if has_bias:
        in_specs.append(pl.BlockSpec((1, c), lambda i: (0, 0)))
        args.append(bias)
    return pl.pallas_call(
        body, name=name, grid=(t // tm,),
        in_specs=in_specs,
        out_specs=pl.BlockSpec((tm, c), lambda i: (i, 0)),
        out_shape=jax.ShapeDtypeStruct((t, c), F32),
        scratch_shapes=[pltpu.VMEM((hb + tm, c), F32)],
        compiler_params=_cparams(("parallel",)),
    )(*args)


def dwconv_bwd(dc, u, w, *, col0=0, name):
    t, c = dc.shape
    kk = w.shape[0]
    assert col0 % c == 0
    cb = col0 // c
    hb = _halo(kk)
    tm = _tile(t, 512, hb)
    r = tm // hb
    n = t // tm

    def body(dcur_ref, dnext_ref, uprev_ref, ucur_ref, w_ref, du_ref, dw_ref, db_ref, sd, su):
        i = pl.program_id(0)

        @pl.when(i == 0)
        def _():
            dw_ref[...] = jnp.zeros_like(dw_ref)
            db_ref[...] = jnp.zeros_like(db_ref)

        dcur = dcur_ref[...]
        sd[0:tm, :] = dcur
        sd[tm:tm + hb, :] = jnp.where(i == n - 1, 0.0, dnext_ref[...])
        su[0:hb, :] = jnp.where(i == 0, 0.0, uprev_ref[...])
        su[hb:hb + tm, :] = ucur_ref[...]
        acc = jnp.zeros((tm, c), F32)
        for k in range(kk):
            acc = acc + w_ref[k:k + 1, :] * sd[pl.ds(kk - 1 - k, tm), :]
            dw_ref[k:k + 1, :] += jnp.sum(dcur * su[pl.ds(hb - (kk - 1) + k, tm), :], axis=0, keepdims=True)
        du_ref[...] = acc
        db_ref[...] += jnp.sum(dcur, axis=0, keepdims=True)

    return pl.pallas_call(
        body, name=name, grid=(n,),
        in_specs=[pl.BlockSpec((tm, c), lambda i: (i, 0)),
                  pl.BlockSpec((hb, c), lambda i: (jnp.minimum((i + 1) * r, n * r - 1), 0)),
                  pl.BlockSpec((hb, c), lambda i: (jnp.maximum(i * r - 1, 0), cb)),
                  pl.BlockSpec((tm, c), lambda i: (i, cb)),
                  pl.BlockSpec((kk, c), lambda i: (0, 0))],
        out_specs=[pl.BlockSpec((tm, c), lambda i: (i, 0)),
                   pl.BlockSpec((kk, c), lambda i: (0, 0)),
                   pl.BlockSpec((1, c), lambda i: (0, 0))],
        out_shape=[jax.ShapeDtypeStruct((t, c), F32), jax.ShapeDtypeStruct((kk, c), F32),
                   jax.ShapeDtypeStruct((1, c), F32)],
        scratch_shapes=[pltpu.VMEM((tm + hb, c), F32), pltpu.VMEM((hb + tm, c), F32)],
        compiler_params=_cparams(("arbitrary",)),
    )(dc, dc, u, u, w)


def glu_fwd(proj, *, name):
    t = proj.shape[0]
    c = GROUP_WIDTH
    tm = _tile(t, 1024, 8)
    vb, gb = P_CONF // c, P_CONF // c + 1

    def body(v_ref, g_ref, o_ref):
        o_ref[...] = v_ref[...] * _sigmoid(g_ref[...])

    return pl.pallas_call(
        body, name=name, grid=(t // tm,),
        in_specs=[pl.BlockSpec((tm, c), lambda i: (i, vb)), pl.BlockSpec((tm, c), lambda i: (i, gb))],
        out_specs=pl.BlockSpec((tm, c), lambda i: (i, 0)),
        out_shape=jax.ShapeDtypeStruct((t, c), F32),
        compiler_params=_cparams(("parallel",)),
    )(proj, proj)


def glu_bwd(proj, du, *, name):
    t = proj.shape[0]
    c = GROUP_WIDTH
    tm = _tile(t, 1024, 8)
    vb, gb = P_CONF // c, P_CONF // c + 1

    def body(v_ref, g_ref, du_ref, o_ref):
        sg = _sigmoid(g_ref[...])
        du = du_ref[...]
        o_ref[:, :c] = du * sg
        o_ref[:, c:] = du * v_ref[...] * sg * (1.0 - sg)

    return pl.pallas_call(
        body, name=name, grid=(t // tm,),
        in_specs=[pl.BlockSpec((tm, c), lambda i: (i, vb)), pl.BlockSpec((tm, c), lambda i: (i, gb)),
                  pl.BlockSpec((tm, c), lambda i: (i, 0))],
        out_specs=pl.BlockSpec((tm, 2 * c), lambda i: (i, 0)),
        out_shape=jax.ShapeDtypeStruct((t, 2 * c), F32),
        compiler_params=_cparams(("parallel",)),
    )(proj, proj, du)


def _group_mean_matrix(c, groups):
    gsz = c // groups
    ri = lax.broadcasted_iota(jnp.int32, (c, c), 0) // gsz
    ci = lax.broadcasted_iota(jnp.int32, (c, c), 1) // gsz
    return jnp.where(ri == ci, 1.0 / gsz, 0.0).astype(F32)


def gn_silu_fwd(cx, gamma, beta, *, name):
    t, c = cx.shape
    tm = _tile(t, 1024, 8)

    def body(c_ref, g_ref, b_ref, o_ref):
        gm = _group_mean_matrix(c, CONF_GROUPS)
        x = c_ref[...]
        mu = _dot_hi(x, gm)
        xc = x - mu
        var = _dot_hi(xc * xc, gm)
        a = xc * lax.rsqrt(var + LN_EPS) * g_ref[...] + b_ref[...]
        o_ref[...] = a * _sigmoid(a)

    row = pl.BlockSpec((tm, c), lambda i: (i, 0))
    vec = pl.BlockSpec((1, c), lambda i: (0, 0))
    return pl.pallas_call(
        body, name=name, grid=(t // tm,),
        in_specs=[row, vec, vec], out_specs=row,
        out_shape=jax.ShapeDtypeStruct((t, c), F32),
        compiler_params=_cparams(("parallel",)),
    )(cx, gamma, beta)


def gn_silu_bwd(cx, gamma, beta, dy, *, name):
    t, c = cx.shape
    tm = _tile(t, 1024, 8)

    def body(c_ref, g_ref, b_ref, dy_ref, dc_ref, dg_ref, db_ref):
        @pl.when(pl.program_id(0) == 0)
        def _():
            dg_ref[...] = jnp.zeros_like(dg_ref)
            db_ref[...] = jnp.zeros_like(db_ref)

        gm = _group_mean_matrix(c, CONF_GROUPS)
        x = c_ref[...]
        mu = _dot_hi(x, gm)
        xc = x - mu
        var = _dot_hi(xc * xc, gm)
        rstd = lax.rsqrt(var + LN_EPS)
        nrm = xc * rstd
        a = nrm * g_ref[...] + b_ref[...]
        sa = _sigmoid(a)
        da = dy_ref[...] * sa * (1.0 + a * (1.0 - sa))
        dg_ref[...] += jnp.sum(da * nrm, axis=0, keepdims=True)
        db_ref[...] += jnp.sum(da, axis=0, keepdims=True)
        dn = da * g_ref[...]
        dc_ref[...] = rstd * (dn - _dot_hi(dn, gm) - nrm * _dot_hi(dn * nrm, gm))

    row = pl.BlockSpec((tm, c), lambda i: (i, 0))
    vec = pl.BlockSpec((1, c), lambda i: (0, 0))
    return pl.pallas_call(
        body, name=name, grid=(t // tm,),
        in_specs=[row, vec, vec, row], out_specs=[row, vec, vec],
        out_shape=[jax.ShapeDtypeStruct((t, c), F32), jax.ShapeDtypeStruct((1, c), F32),
                   jax.ShapeDtypeStruct((1, c), F32)],
        compiler_params=_cparams(("arbitrary",)),
    )(cx, gamma, beta, dy)


FOX_COL = 8
SMALL_BLK = P_SMALL // LANES


def _log_sigmoid(x):
    return jnp.minimum(x, 0.0) - jnp.log(1.0 + jnp.exp(-jnp.abs(x)))


def _fox_cols(shape):
    col = lax.broadcasted_iota(jnp.int32, shape, 1)
    return (col >= FOX_COL) & (col < FOX_COL + GROUP_HEADS)


def fox_gate_fwd(proj, bvec, *, name):
    t = proj.shape[0]
    tm = _tile(t, 256, 8)

    def body(s_ref, b_ref, o_ref, carry):
        @pl.when(pl.program_id(0) == 0)
        def _():
            carry[...] = jnp.zeros_like(carry)

        lf = jnp.where(_fox_cols((tm, LANES)), _log_sigmoid(s_ref[...] + b_ref[...]), 0.0)
        ri = lax.broadcasted_iota(jnp.int32, (tm, tm), 0)
        ci = lax.broadcasted_iota(jnp.int32, (tm, tm), 1)
        cum = _dot_hi(jnp.where(ri >= ci, 1.0, 0.0).astype(F32), lf) + carry[...]
        o_ref[...] = cum
        carry[...] = cum[tm - 1:tm, :]

    return pl.pallas_call(
        body, name=name, grid=(t // tm,),
        in_specs=[pl.BlockSpec((tm, LANES), lambda i: (i, SMALL_BLK)), pl.BlockSpec((1, LANES), lambda i: (0, 0))],
        out_specs=pl.BlockSpec((tm, LANES), lambda i: (i, 0)),
        out_shape=jax.ShapeDtypeStruct((t, LANES), F32),
        scratch_shapes=[pltpu.VMEM((1, LANES), F32)],
        compiler_params=_cparams(("arbitrary",)),
    )(proj, bvec)


def fox_gate_bwd(dcum, proj, bvec, *, name):
    t = proj.shape[0]
    tm = _tile(t, 256, 8)
    n = t // tm

    def body(d_ref, s_ref, b_ref, o_ref, db_ref, carry):
        @pl.when(pl.program_id(0) == 0)
        def _():
            carry[...] = jnp.zeros_like(carry)
            db_ref[...] = jnp.zeros_like(db_ref)

        ri = lax.broadcasted_iota(jnp.int32, (tm, tm), 0)
        ci = lax.broadcasted_iota(jnp.int32, (tm, tm), 1)
        dlf = _dot_hi(jnp.where(ri <= ci, 1.0, 0.0).astype(F32), d_ref[...]) + carry[...]
        carry[...] = dlf[0:1, :]
        x = s_ref[...] + b_ref[...]
        dx = jnp.where(_fox_cols((tm, LANES)), dlf * (1.0 - _sigmoid(x)), 0.0)
        o_ref[...] = dx
        db_ref[...] += jnp.sum(dx, axis=0, keepdims=True)

    return pl.pallas_call(
        body, name=name, grid=(n,),
        in_specs=[pl.BlockSpec((tm, LANES), lambda i: (n - 1 - i, 0)),
                  pl.BlockSpec((tm, LANES), lambda i: (n - 1 - i, SMALL_BLK)),
                  pl.BlockSpec((1, LANES), lambda i: (0, 0))],
        out_specs=[pl.BlockSpec((tm, LANES), lambda i: (n - 1 - i, 0)), pl.BlockSpec((1, LANES), lambda i: (0, 0))],
        out_shape=[jax.ShapeDtypeStruct((t, LANES), F32), jax.ShapeDtypeStruct((1, LANES), F32)],
        scratch_shapes=[pltpu.VMEM((1, LANES), F32)],
        compiler_params=_cparams(("arbitrary",)),
    )(dcum, proj, bvec)


def _head_masks(c):
    lane_head = lax.broadcasted_iota(jnp.int32, (1, c), 1) // HEAD_DIM
    return [lane_head == h for h in range(GROUP_HEADS)]


def _attn_tiles(t, tq, tk):
    tq = _tile(t, tq, 8)
    tk = _tile(t, tk, LANES)
    return tq, tk, t // tq, t // tk


def _grid_ends(*sizes):
    first = lambda: functools.reduce(lambda a, b: a & b, [pl.program_id(d) == 0 for d in range(len(sizes))])
    last = lambda: functools.reduce(lambda a, b: a & b, [pl.program_id(d) == s - 1 for d, s in enumerate(sizes)])
    return first, last


def fox_fwd(proj, cum, cum_t, *, name, tq=512, tk=512, comm=None):
    t = proj.shape[0]
    c = GROUP_WIDTH
    tq, tk, nq, nk = _attn_tiles(t, tq, tk)
    qb = P_FOX // c
    scale = HEAD_DIM ** -0.5

    def last_j(i):
        return ((i + 1) * tq - 1) // tk

    def body(q_ref, k_ref, v_ref, cc_ref, cr_ref, o_ref, lse_ref, m_scr, l_scr, acc_scr):
        i = pl.program_id(0)
        j = pl.program_id(1)
        masks = _head_masks(c)

        @pl.when(j == 0)
        def _():
            m_scr[...] = jnp.full_like(m_scr, NEG_BIG)
            l_scr[...] = jnp.zeros_like(l_scr)
            acc_scr[...] = jnp.zeros_like(acc_scr)

        @pl.when(j <= last_j(i))
        def _():
            q = q_ref[...]
            kb = k_ref[...].astype(MXU_DT)
            vb = v_ref[...].astype(MXU_DT)
            row = i * tq + lax.broadcasted_iota(jnp.int32, (tq, tk), 0)
            col = j * tk + lax.broadcasted_iota(jnp.int32, (tq, tk), 1)
            causal = col <= row
            acc = acc_scr[...]
            for h in range(GROUP_HEADS):
                qh = jnp.where(masks[h], q, 0.0).astype(MXU_DT)
                s = _dot_nt(qh, kb) * scale + (cc_ref[:, FOX_COL + h:FOX_COL + h + 1] - cr_ref[h:h + 1, :])
                s = jnp.where(causal, s, NEG_BIG)
                m_old = m_scr[h]
                m_new = jnp.maximum(m_old, jnp.max(s, axis=-1, keepdims=True))
                p = jnp.exp(s - m_new)
                alpha = jnp.exp(m_old - m_new)
                l_scr[h] = alpha * l_scr[h] + jnp.sum(p, axis=-1, keepdims=True)
                m_scr[h] = m_new
                acc = jnp.where(masks[h], alpha * acc + _dot(p.astype(MXU_DT), vb), acc)
            acc_scr[...] = acc

        @pl.when(j == last_j(i))
        def _():
            acc = acc_scr[...]
            o = jnp.zeros_like(acc)
            lse = jnp.zeros((tq, LANES), F32)
            lane = lax.broadcasted_iota(jnp.int32, (1, LANES), 1)
            for h in range(GROUP_HEADS):
                o = jnp.where(masks[h], acc / l_scr[h], o)
                lse = jnp.where(lane == h, m_scr[h] + jnp.log(l_scr[h]), lse)
            o_ref[...] = o
            lse_ref[...] = lse

    def kvmap(blk):
        return lambda i, j: (jnp.minimum(j, last_j(i)), blk)

    call = dict(
        name=name, grid=(nq, nk),
        in_specs=[pl.BlockSpec((tq, c), lambda i, j: (i, qb)),
                  pl.BlockSpec((tk, c), kvmap(qb + 1)),
                  pl.BlockSpec((tk, c), kvmap(qb + 2)),
                  pl.BlockSpec((tq, LANES), lambda i, j: (i, 0)),
                  pl.BlockSpec((8, tk), lambda i, j: (0, jnp.minimum(j, last_j(i))))],
        out_specs=[pl.BlockSpec((tq, c), lambda i, j: (i, 0)), pl.BlockSpec((tq, LANES), lambda i, j: (i, 0))],
        out_shape=[jax.ShapeDtypeStruct((t, c), F32), jax.ShapeDtypeStruct((t, LANES), F32)],
        scratch_shapes=[pltpu.VMEM((GROUP_HEADS, tq, 1), F32), pltpu.VMEM((GROUP_HEADS, tq, 1), F32),
                        pltpu.VMEM((tq, c), F32)],
        compiler_params=_cparams(("arbitrary", "arbitrary")),
    )
    outs, got = carry_comm(call, body, (proj, proj, proj, cum, cum_t), comm, 2, *_grid_ends(nq, nk))
    return (*outs, got)


def fox_bwd(proj, cum, cum_t, o, lse, do, *, name, tq=512, tk=512, comm=None):
    t = proj.shape[0]
    c = GROUP_WIDTH
    tq, tk, nq, nk = _attn_tiles(t, tq, tk)
    qb = P_FOX // c
    scale = HEAD_DIM ** -0.5

    def last_j(i):
        return ((i + 1) * tq - 1) // tk

    def body(q_ref, k_ref, v_ref, cc_ref, cr_ref, o_ref, lse_ref, do_ref,
             dq_ref, dk_ref, dv_ref, dcc_ref, dcr_ref, dq_scr, rs_scr):
        i = pl.program_id(0)
        j = pl.program_id(1)
        masks = _head_masks(c)

        @pl.when((i == 0) & (j == 0))
        def _():
            dk_ref[...] = jnp.zeros_like(dk_ref)
            dv_ref[...] = jnp.zeros_like(dv_ref)
            dcr_ref[...] = jnp.zeros_like(dcr_ref)

        @pl.when(j == 0)
        def _():
            dq_scr[...] = jnp.zeros_like(dq_scr)
            rs_scr[...] = jnp.zeros_like(rs_scr)

        @pl.when(j <= last_j(i))
        def _():
            q = q_ref[...]
            qf = q.astype(MXU_DT)
            kb = k_ref[...].astype(MXU_DT)
            vb = v_ref[...].astype(MXU_DT)
            do = do_ref[...]
            dob = do.astype(MXU_DT)
            doo = do * o_ref[...]
            row = i * tq + lax.broadcasted_iota(jnp.int32, (tq, tk), 0)
            col = j * tk + lax.broadcasted_iota(jnp.int32, (tq, tk), 1)
            causal = col <= row
            dq = dq_scr[...]
            dk_upd = jnp.zeros((tk, c), F32)
            dv_upd = jnp.zeros((tk, c), F32)
            for h in range(GROUP_HEADS):
                qh = jnp.where(masks[h], q, 0.0).astype(MXU_DT)
                s = _dot_nt(qh, kb) * scale + (cc_ref[:, FOX_COL + h:FOX_COL + h + 1] - cr_ref[h:h + 1, :])
                p = jnp.where(causal, jnp.exp(s - lse_ref[:, h:h + 1]), 0.0)
                delta = jnp.sum(jnp.where(masks[h], doo, 0.0), axis=-1, keepdims=True)
                doh = jnp.where(masks[h], do, 0.0).astype(MXU_DT)
                ds = p * (_dot_nt(doh, vb) - delta)
                dsb = ds.astype(MXU_DT)
                dq = jnp.where(masks[h], dq + _dot(dsb, kb) * scale, dq)
                dk_upd = jnp.where(masks[h], _dot_tn(dsb, qf) * scale, dk_upd)
                dv_upd = jnp.where(masks[h], _dot_tn(p.astype(MXU_DT), dob), dv_upd)
                dcr_ref[j, h:h + 1, :] += -jnp.sum(ds, axis=0, keepdims=True)
                rs_scr[h] += jnp.sum(ds, axis=-1, keepdims=True)
            dq_scr[...] = dq
            rows = pl.ds(pl.multiple_of(j * tk, tk), tk)
            dk_ref[rows, :] += dk_upd
            dv_ref[rows, :] += dv_upd

        @pl.when(j == last_j(i))
        def _():
            dq_ref[...] = dq_scr[...]
            lane = lax.broadcasted_iota(jnp.int32, (1, LANES), 1)
            dcc = jnp.zeros((tq, LANES), F32)
            for h in range(GROUP_HEADS):
                dcc = jnp.where(lane == FOX_COL + h, rs_scr[h], dcc)
            dcc_ref[...] = dcc

    def kvmap(blk):
        return lambda i, j: (jnp.minimum(j, last_j(i)), blk)

    qrow = lambda i, j: (i, 0)
    whole = lambda i, j: (0, 0)
    call = dict(
        name=name, grid=(nq, nk),
        in_specs=[pl.BlockSpec((tq, c), lambda i, j: (i, qb)),
                  pl.BlockSpec((tk, c), kvmap(qb + 1)),
                  pl.BlockSpec((tk, c), kvmap(qb + 2)),
                  pl.BlockSpec((tq, LANES), qrow),
                  pl.BlockSpec((8, tk), lambda i, j: (0, jnp.minimum(j, last_j(i)))),
                  pl.BlockSpec((tq, c), qrow),
                  pl.BlockSpec((tq, LANES), qrow),
                  pl.BlockSpec((tq, c), qrow)],
        out_specs=[pl.BlockSpec((tq, c), qrow),
                   pl.BlockSpec((t, c), whole),
                   pl.BlockSpec((t, c), whole),
                   pl.BlockSpec((tq, LANES), qrow),
                   pl.BlockSpec((nk, 8, tk), lambda i, j: (0, 0, 0))],
        out_shape=[jax.ShapeDtypeStruct((t, c), F32), jax.ShapeDtypeStruct((t, c), F32),
                   jax.ShapeDtypeStruct((t, c), F32), jax.ShapeDtypeStruct((t, LANES), F32),
                   jax.ShapeDtypeStruct((nk, 8, tk), F32)],
        scratch_shapes=[pltpu.VMEM((tq, c), F32), pltpu.VMEM((GROUP_HEADS, tq, 1), F32)],
        compiler_params=_cparams(("arbitrary", "arbitrary")),
    )
    outs, got = carry_comm(call, body, (proj, proj, proj, cum, cum_t, o, lse, do), comm, 5, *_grid_ends(nq, nk))
    return (*outs, got)


SB_DEAD = -110.0


def _sb_logs(z, strict):
    tt = jnp.log(1.0 + jnp.exp(-jnp.abs(z)))
    log_keep = jnp.where(strict, -(jnp.maximum(z, 0.0) + tt), 0.0)
    log_beta = jnp.minimum(z, 0.0) - tt
    return log_keep, log_beta


def _tri(n, upper):
    a = lax.broadcasted_iota(jnp.int32, (n, n), 0)
    b = lax.broadcasted_iota(jnp.int32, (n, n), 1)
    return jnp.where((a < b) if upper else (a > b), 1.0, 0.0).astype(MXU_DT)


def sb_fwd(proj, *, name, tq=512, tk=256):
    t = proj.shape[0]
    c = GROUP_WIDTH
    tq, tk, nq, nk = _attn_tiles(t, tq, tk)
    qb = P_SB // c
    scale = HEAD_DIM ** -0.5

    def last_j(i):
        return ((i + 1) * tq - 1) // tk

    def body(q_ref, k_ref, v_ref, o_ref, rs_ref, r_scr, acc_scr):
        i = pl.program_id(0)
        jj = pl.program_id(1)
        masks = _head_masks(c)

        @pl.when(jj == 0)
        def _():
            r_scr[...] = jnp.zeros_like(r_scr)
            acc_scr[...] = jnp.zeros_like(acc_scr)

        @pl.when(jj <= last_j(i))
        def _():
            lane = lax.broadcasted_iota(jnp.int32, (1, LANES), 1)
            rs = jnp.zeros((tq, LANES), F32)
            for h in range(GROUP_HEADS):
                rs = jnp.where(lane == h, r_scr[h], rs)
            rs_ref[0] = rs

        @pl.when((jj <= last_j(i)) & (jnp.max(r_scr[...]) > SB_DEAD))
        def _():
            j = last_j(i) - jj
            q = q_ref[...]
            kb = k_ref[...].astype(MXU_DT)
            vb = v_ref[...].astype(MXU_DT)
            row = i * tq + lax.broadcasted_iota(jnp.int32, (tq, tk), 0)
            col = j * tk + lax.broadcasted_iota(jnp.int32, (tq, tk), 1)
            strict = col < row
            later = _tri(tk, upper=False)
            acc = acc_scr[...]
            for h in range(GROUP_HEADS):
                qh = jnp.where(masks[h], q, 0.0).astype(MXU_DT)
                z = _dot_nt(qh, kb) * scale
                log_keep, log_beta = _sb_logs(z, strict)
                r_old = r_scr[h]
                rest = r_old + _split_dot(log_keep, later)
                w = jnp.where(strict, jnp.exp(log_beta + rest), 0.0)
                acc = jnp.where(masks[h], acc + _dot(w.astype(MXU_DT), vb), acc)
                r_scr[h] = r_old + jnp.sum(log_keep, axis=-1, keepdims=True)
            acc_scr[...] = acc

        @pl.when(jj == last_j(i))
        def _():
            o_ref[...] = acc_scr[...]

    def kvmap(blk):
        return lambda i, jj: (jnp.maximum(last_j(i) - jj, 0), blk)

    return pl.pallas_call(
        body, name=name, grid=(nq, nk),
        in_specs=[pl.BlockSpec((tq, c), lambda i, jj: (i, qb)),
                  pl.BlockSpec((tk, c), kvmap(qb + 1)),
                  pl.BlockSpec((tk, c), kvmap(qb + 2))],
        out_specs=[pl.BlockSpec((tq, c), lambda i, jj: (i, 0)),
                   pl.BlockSpec((1, tq, LANES), lambda i, jj: (jnp.maximum(last_j(i) - jj, 0), i, 0))],
        out_shape=[jax.ShapeDtypeStruct((t, c), F32), jax.ShapeDtypeStruct((nk, t, LANES), F32)],
        scratch_shapes=[pltpu.VMEM((GROUP_HEADS, tq, 1), F32), pltpu.VMEM((tq, c), F32)],
        compiler_params=_cparams(("parallel", "arbitrary")),
    )(proj, proj, proj)


def sb_bwd(proj, rsave, do, *, name, tq=512, tk=256):
    t = proj.shape[0]
    c = GROUP_WIDTH
    tq, tk, nq, nk = _attn_tiles(t, tq, tk)
    qb = P_SB // c
    scale = HEAD_DIM ** -0.5

    def last_j(i):
        return ((i + 1) * tq - 1) // tk

    def body(q_ref, k_ref, v_ref, rs_ref, do_ref, dq_ref, dk_ref, dv_ref, e_scr, dq_scr):
        i = pl.program_id(0)
        j = pl.program_id(1)
        masks = _head_masks(c)

        @pl.when((i == 0) & (j == 0))
        def _():
            dk_ref[...] = jnp.zeros_like(dk_ref)
            dv_ref[...] = jnp.zeros_like(dv_ref)

        @pl.when(j == 0)
        def _():
            e_scr[...] = jnp.zeros_like(e_scr)
            dq_scr[...] = jnp.zeros_like(dq_scr)

        lane = lax.broadcasted_iota(jnp.int32, (1, LANES), 1)
        carry_max = jnp.max(jnp.where(lane < GROUP_HEADS, rs_ref[0], SB_DEAD))

        @pl.when((j <= last_j(i)) & (carry_max > SB_DEAD))
        def _():
            q = q_ref[...]
            qf = q.astype(MXU_DT)
            kb = k_ref[...].astype(MXU_DT)
            vb = v_ref[...].astype(MXU_DT)
            do = do_ref[...]
            dob = do.astype(MXU_DT)
            row = i * tq + lax.broadcasted_iota(jnp.int32, (tq, tk), 0)
            col = j * tk + lax.broadcasted_iota(jnp.int32, (tq, tk), 1)
            strict = col < row
            later = _tri(tk, upper=False)
            earlier = _tri(tk, upper=True)
            rs = rs_ref[0]
            dq = dq_scr[...]
            dk_upd = jnp.zeros((tk, c), F32)
            dv_upd = jnp.zeros((tk, c), F32)
            for h in range(GROUP_HEADS):
                qh = jnp.where(masks[h], q, 0.0).astype(MXU_DT)
                z = _dot_nt(qh, kb) * scale
                log_keep, log_beta = _sb_logs(z, strict)
                rest = rs[:, h:h + 1] + _split_dot(log_keep, later)
                w = jnp.where(strict, jnp.exp(log_beta + rest), 0.0)
                doh = jnp.where(masks[h], do, 0.0).astype(MXU_DT)
                e = w * _dot_nt(doh, vb)
                e_old = e_scr[h]
                dkeep = e_old + _split_dot(e, earlier)
                dz = jnp.where(strict, e * jnp.exp(log_keep) - dkeep * jnp.exp(log_beta), 0.0)
                dzb = dz.astype(MXU_DT)
                dq = jnp.where(masks[h], dq + _dot(dzb, kb) * scale, dq)
                dk_upd = jnp.where(masks[h], _dot_tn(dzb, qf) * scale, dk_upd)
                dv_upd = jnp.where(masks[h], _dot_tn(w.astype(MXU_DT), dob), dv_upd)
                e_scr[h] = e_old + jnp.sum(e, axis=-1, keepdims=True)
            dq_scr[...] = dq
            rows = pl.ds(pl.multiple_of(j * tk, tk), tk)
            dk_ref[rows, :] += dk_upd
            dv_ref[rows, :] += dv_upd

        @pl.when(j == last_j(i))
        def _():
            dq_ref[...] = dq_scr[...]

    def kvmap(blk):
        return lambda i, j: (jnp.minimum(j, last_j(i)), blk)

    qrow = lambda i, j: (i, 0)
    whole = lambda i, j: (0, 0)
    return pl.pallas_call(
        body, name=name, grid=(nq, nk),
        in_specs=[pl.BlockSpec((tq, c), lambda i, j: (i, qb)),
                  pl.BlockSpec((tk, c), kvmap(qb + 1)),
                  pl.BlockSpec((tk, c), kvmap(qb + 2)),
                  pl.BlockSpec((1, tq, LANES), lambda i, j: (jnp.minimum(j, last_j(i)), i, 0)),
                  pl.BlockSpec((tq, c), qrow)],
        out_specs=[pl.BlockSpec((tq, c), qrow), pl.BlockSpec((t, c), whole), pl.BlockSpec((t, c), whole)],
        out_shape=[jax.ShapeDtypeStruct((t, c), F32)] * 3,
        scratch_shapes=[pltpu.VMEM((GROUP_HEADS, tq, 1), F32), pltpu.VMEM((tq, c), F32)],
        compiler_params=_cparams(("arbitrary", "arbitrary")),
    )(proj, proj, proj, rsave, do)


A_COL, B_COL = 0, 4
Z_BLK = P_Z // GROUP_WIDTH


def _dot_tn_hi(a, b):
    return lax.dot_general(a, b, (((0,), (0,)), ((), ())), preferred_element_type=F32, precision=HI)


def _silu(x):
    return x * _sigmoid(x)


def _dsilu(x):
    s = _sigmoid(x)
    return s * (1.0 + x * (1.0 - s))


def _head_sum(x, masks):
    out = jnp.zeros_like(x)
    for m in masks:
        out = jnp.where(m, jnp.sum(jnp.where(m, x, 0.0), axis=-1, keepdims=True), out)
    return out


def _expand(cols, col0, masks):
    out = jnp.zeros((cols.shape[0], GROUP_WIDTH), F32)
    for h, m in enumerate(masks):
        out = jnp.where(m, cols[:, col0 + h:col0 + h + 1], out)
    return out


def _reduce(x, col0, masks):
    lane = lax.broadcasted_iota(jnp.int32, (1, LANES), 1)
    out = jnp.zeros((x.shape[0], LANES), F32)
    for h, m in enumerate(masks):
        out = jnp.where(lane == col0 + h, jnp.sum(jnp.where(m, x, 0.0), axis=-1, keepdims=True), out)
    return out


def _block_ones():
    ri = lax.broadcasted_iota(jnp.int32, (GROUP_WIDTH, GROUP_WIDTH), 0) // HEAD_DIM
    ci = lax.broadcasted_iota(jnp.int32, (GROUP_WIDTH, GROUP_WIDTH), 1) // HEAD_DIM
    return jnp.where(ri == ci, 1.0, 0.0).astype(F32)


def _blk(x, hs):
    return jnp.concatenate([x] * GROUP_HEADS, axis=0) * hs


def _unblk(m, hs):
    mm = m * hs
    c = GDN_CHUNK
    return mm[0:c] + mm[c:2 * c] + mm[2 * c:3 * c] + mm[3 * c:4 * c]


def _row_mask4():
    ri = lax.broadcasted_iota(jnp.int32, (GROUP_WIDTH, LANES), 0) // HEAD_DIM
    ci = lax.broadcasted_iota(jnp.int32, (GROUP_WIDTH, LANES), 1)
    return jnp.where(ri + A_COL == ci, 1.0, 0.0).astype(F32)


def _gdn_chunk(xc, small, avec, dtvec, state, masks, hs):
    c = GDN_CHUNK
    w = GROUP_WIDTH
    b16 = lambda v: v.astype(MXU_DT)
    f = {}
    xq, xk, xv = xc[:, :w], xc[:, w:2 * w], xc[:, 2 * w:]
    qs, ks, v = _silu(xq), _silu(xk), _silu(xv)
    rq = lax.rsqrt(_head_sum(qs * qs, masks) + L2_EPS)
    rk = lax.rsqrt(_head_sum(ks * ks, masks) + L2_EPS)
    qn = qs * rq
    k = ks * rk
    q = qn * (HEAD_DIM ** -0.5)
    xg = small + dtvec
    sp = jnp.maximum(xg, 0.0) + jnp.log(1.0 + jnp.exp(-jnp.abs(xg)))
    g128 = -avec * sp
    beta128 = _sigmoid(small)
    ri = lax.broadcasted_iota(jnp.int32, (c, c), 0)
    ci = lax.broadcasted_iota(jnp.int32, (c, c), 1)
    tril = jnp.where(ri >= ci, 1.0, 0.0).astype(F32)
    gam128 = _dot_hi(tril, g128)
    gam = _expand(gam128, A_COL, masks)
    bfull = _expand(beta128, B_COL, masks)
    mask4 = _row_mask4()
    ones = jnp.ones((c, LANES), F32)
    gam_row = _dot_nt_hi(ones, jnp.concatenate([gam128] * GROUP_HEADS, axis=0) * mask4)
    li = lax.broadcasted_iota(jnp.int32, (c, w), 0)
    lj = lax.broadcasted_iota(jnp.int32, (c, w), 1) % HEAD_DIM
    incl = li >= lj
    strict = li > lj
    dmat = jnp.exp(jnp.where(incl, gam - gam_row, NEG_BIG))
    egam = jnp.exp(gam)
    glast = gam[c - 1:c, :]
    ekd = jnp.exp(glast - gam)
    kb = k * bfull
    vb = v * bfull
    kbg = kb * egam
    qd = q * egam
    kd = k * ekd
    kblk = b16(_blk(k, hs))
    araw = _dot_nt(b16(kb), kblk)
    a = jnp.where(strict, araw * dmat, 0.0)
    tm = jnp.where(li == lj, 1.0, 0.0) - a
    p = a
    for _ in range(5):
        p = _dot_hi(p, _blk(p, hs))
        tm = tm + _dot_hi(tm, _blk(p, hs))
    tm16 = b16(tm)
    u = _dot(tm16, b16(_blk(vb, hs)))
    wm = _dot(tm16, b16(_blk(kbg, hs)))
    qk = _dot_nt(b16(q), kblk)
    aqk = jnp.where(incl, qk * dmat, 0.0)
    s16 = b16(state)
    vn = u - _dot(b16(wm), s16)
    o = _dot(b16(qd), s16) + _dot(b16(aqk), b16(_blk(vn, hs)))
    s_new = state * jnp.exp(glast) + hs * _dot_tn(b16(kd), b16(vn))
    f.update(xq=xq, xk=xk, xv=xv, v=v, rq=rq, rk=rk, qn=qn, k=k, q=q, xg=xg, g128=g128, beta128=beta128,
             tril=tril, gam=gam, bfull=bfull, mask4=mask4, ones=ones, incl=incl, strict=strict, li=li,
             dmat=dmat, egam=egam, glast=glast, ekd=ekd, kb=kb, vb=vb, kbg=kbg, qd=qd, kd=kd, kblk=kblk,
             araw=araw, tm=tm, tm16=tm16, wm=wm, qk=qk, aqk=aqk, s16=s16, vn=vn, o=o, s_new=s_new)
    return f


def _decay_rate(a_log):
    lane = lax.broadcasted_iota(jnp.int32, a_log.shape, 1)
    return jnp.where((lane >= A_COL) & (lane < A_COL + GROUP_HEADS), jnp.exp(a_log), 0.0)


def _gdn_post(o, z, ng, masks):
    r = lax.rsqrt(_head_sum(o * o, masks) * (1.0 / HEAD_DIM) + RMS_EPS)
    on = o * r
    return on, r, on * ng * _silu(z)


def gdn_fwd(cqkv, proj, avec, dtvec, ng, *, name, comm=None):
    t = cqkv.shape[0]
    c = GDN_CHUNK
    w = GROUP_WIDTH
    n = t // c

    def body(x_ref, z_ref, sm_ref, a_ref, dt_ref, ng_ref, y_ref, st_ref, s_scr):
        @pl.when(pl.program_id(0) == 0)
        def _():
            s_scr[...] = jnp.zeros_like(s_scr)

        masks = _head_masks(w)
        hs = _block_ones()
        state = s_scr[...]
        st_ref[0] = state
        f = _gdn_chunk(x_ref[...], sm_ref[...], _decay_rate(a_ref[...]), dt_ref[...], state, masks, hs)
        _, _, y = _gdn_post(f["o"], z_ref[...], ng_ref[...], masks)
        y_ref[...] = y
        s_scr[...] = f["s_new"]

    vec = pl.BlockSpec((1, LANES), lambda i: (0, 0))
    call = dict(
        name=name, grid=(n,),
        in_specs=[pl.BlockSpec((c, 3 * w), lambda i: (i, 0)),
                  pl.BlockSpec((c, w), lambda i: (i, Z_BLK)),
                  pl.BlockSpec((c, LANES), lambda i: (i, SMALL_BLK)),
                  vec, vec, pl.BlockSpec((1, w), lambda i: (0, 0))],
        out_specs=[pl.BlockSpec((c, w), lambda i: (i, 0)), pl.BlockSpec((1, w, w), lambda i: (i, 0, 0))],
        out_shape=[jax.ShapeDtypeStruct((t, w), F32), jax.ShapeDtypeStruct((n, w, w), F32)],
        scratch_shapes=[pltpu.VMEM((w, w), F32)],
        compiler_params=_cparams(("arbitrary",)),
    )
    outs, got = carry_comm(call, body, (cqkv, proj, proj, avec, dtvec, ng), comm, 2, *_grid_ends(n))
    return (*outs, got)


def gdn_bwd(cqkv, proj, avec, dtvec, ng, states, dy, *, name, comm=None):
    t = cqkv.shape[0]
    c = GDN_CHUNK
    w = GROUP_WIDTH
    n = t // c
    b16 = lambda v: v.astype(MXU_DT)

    def body(x_ref, z_ref, sm_ref, a_ref, dt_ref, ng_ref, st_ref, dy_ref,
             dx_ref, dz_ref, dsm_ref, dng_ref, dal_ref, ddt_ref, ds_scr):
        @pl.when(pl.program_id(0) == 0)
        def _():
            ds_scr[...] = jnp.zeros_like(ds_scr)
            dng_ref[...] = jnp.zeros_like(dng_ref)
            dal_ref[...] = jnp.zeros_like(dal_ref)
            ddt_ref[...] = jnp.zeros_like(ddt_ref)

        masks = _head_masks(w)
        hs = _block_ones()
        state = st_ref[0]
        avec_v = _decay_rate(a_ref[...])
        f = _gdn_chunk(x_ref[...], sm_ref[...], avec_v, dt_ref[...], state, masks, hs)
        z = z_ref[...]
        ng_v = ng_ref[...]
        dy_v = dy_ref[...]
        on, r, _ = _gdn_post(f["o"], z, ng_v, masks)
        sz = _silu(z)
        dz_ref[...] = dy_v * on * ng_v * _dsilu(z)
        d_on = dy_v * ng_v * sz
        dng_ref[...] += jnp.sum(dy_v * on * sz, axis=0, keepdims=True)
        do = r * (d_on - on * _head_sum(d_on * on, masks) * (1.0 / HEAD_DIM))
        do16 = b16(do)
        dsn = ds_scr[...]
        dsn16 = b16(dsn)
        s16, vn, kd, qd, wm = f["s16"], f["vn"], f["kd"], f["qd"], f["wm"]
        k, q, kblk, tm, tm16 = f["k"], f["q"], f["kblk"], f["tm"], f["tm16"]
        dmat, egam, glast, gam = f["dmat"], f["egam"], f["glast"], f["gam"]
        incl, strict, li = f["incl"], f["strict"], f["li"]
        vn16 = b16(vn)
        dvn = _unblk(_dot_tn(b16(f["aqk"]), do16), hs) + _dot(b16(kd), dsn16)
        daqk = jnp.where(incl, _dot_nt(do16, b16(_blk(vn, hs))), 0.0)
        dqd = _dot_nt(do16, s16)
        dvn16 = b16(dvn)
        ds_scr[...] = hs * (_dot_tn(b16(qd), do16) - _dot_tn(b16(wm), dvn16)) + dsn * jnp.exp(glast)
        dkd = _dot_nt(vn16, dsn16)
        dglast = jnp.sum(dsn * state, axis=0, keepdims=True) * jnp.exp(glast)
        du16 = dvn16
        dw16 = b16(-_dot_nt(dvn16, s16))
        dqk16 = b16(daqk * dmat)
        ddm = daqk * f["qk"]
        dq = _dot(dqk16, kblk)
        dk = _unblk(_dot_tn(dqk16, b16(q)), hs)
        dtm = _dot_nt(du16, b16(_blk(f["vb"], hs))) + _dot_nt(dw16, b16(_blk(f["kbg"], hs)))
        dvb = _unblk(_dot_tn(tm16, du16), hs)
        dkbg = _unblk(_dot_tn(tm16, dw16), hs)
        xx = _unblk(_dot_tn_hi(tm, dtm), hs)
        da = jnp.where(strict, -_dot_nt_hi(xx, _blk(tm, hs)), 0.0)
        daraw16 = b16(da * dmat)
        ddm = ddm + da * f["araw"]
        dkb = _dot(daraw16, kblk)
        dk = dk + _unblk(_dot_tn(daraw16, b16(f["kb"])), hs)
        tcol = ddm * dmat
        dgam = tcol
        dgam128_row = _dot_tn_hi(-tcol, f["ones"]) * f["mask4"]
        dgam128_row = (dgam128_row[0:c] + dgam128_row[c:2 * c] + dgam128_row[2 * c:3 * c] + dgam128_row[3 * c:4 * c])
        dk = dk + dkd * f["ekd"]
        tt = dkd * kd
        dgam = dgam - tt
        dglast = dglast + jnp.sum(tt, axis=0, keepdims=True)
        dq = dq + dqd * egam
        dgam = dgam + dqd * qd
        dkb = dkb + dkbg * egam
        dgam = dgam + dkbg * f["kbg"]
        dk = dk + dkb * f["bfull"]
        dbf = dkb * k + dvb * f["v"]
        dv = dvb * f["bfull"]
        dgam = dgam + jnp.where(li == c - 1, dglast, 0.0)
        beta128 = f["beta128"]
        db128 = _reduce(dbf, B_COL, masks) * beta128 * (1.0 - beta128)
        dgam128 = _reduce(dgam, A_COL, masks) + dgam128_row
        dg128 = _dot_tn_hi(f["tril"], dgam128)
        dxg = dg128 * (-avec_v * _sigmoid(f["xg"]))
        lane = lax.broadcasted_iota(jnp.int32, (1, LANES), 1)
        dsm_ref[...] = jnp.where(lane < B_COL, dxg, db128)
        ddt_ref[...] += jnp.sum(dxg, axis=0, keepdims=True)
        dal_ref[...] += jnp.sum(dg128 * f["g128"], axis=0, keepdims=True)
        dqn = dq * (HEAD_DIM ** -0.5)
        dqs = f["rq"] * (dqn - f["qn"] * _head_sum(dqn * f["qn"], masks))
        dks = f["rk"] * (dk - k * _head_sum(dk * k, masks))
        dx_ref[:, :w] = dqs * _dsilu(f["xq"])
        dx_ref[:, w:2 * w] = dks * _dsilu(f["xk"])
        dx_ref[:, 2 * w:] = dv * _dsilu(f["xv"])

    vec = pl.BlockSpec((1, LANES), lambda i: (0, 0))
    rev = lambda blk: (lambda i: (n - 1 - i, blk))
    call = dict(
        name=name, grid=(n,),
        in_specs=[pl.BlockSpec((c, 3 * w), rev(0)),
                  pl.BlockSpec((c, w), rev(Z_BLK)),
                  pl.BlockSpec((c, LANES), rev(SMALL_BLK)),
                  vec, vec, pl.BlockSpec((1, w), lambda i: (0, 0)),
                  pl.BlockSpec((1, w, w), lambda i: (n - 1 - i, 0, 0)),
                  pl.BlockSpec((c, w), rev(0))],
        out_specs=[pl.BlockSpec((c, 3 * w), rev(0)), pl.BlockSpec((c, w), rev(0)), pl.BlockSpec((c, LANES), rev(0)),
                   pl.BlockSpec((1, w), lambda i: (0, 0)), vec, vec],
        out_shape=[jax.ShapeDtypeStruct((t, 3 * w), F32), jax.ShapeDtypeStruct((t, w), F32),
                   jax.ShapeDtypeStruct((t, LANES), F32), jax.ShapeDtypeStruct((1, w), F32),
                   jax.ShapeDtypeStruct((1, LANES), F32), jax.ShapeDtypeStruct((1, LANES), F32)],
        scratch_shapes=[pltpu.VMEM((w, w), F32)],
        compiler_params=_cparams(("arbitrary",)),
    )
    outs, got = carry_comm(call, body, (cqkv, proj, proj, avec, dtvec, ng, states, dy), comm, 6, *_grid_ends(n))
    return (*outs, got)


def adamw(w, m, v, gslots, *, row0=0, name):
    r, c = w.shape
    s = gslots.shape[0]
    tr = _tile(r, 64, 8)
    assert row0 % tr == 0 and gslots.shape[2] == c
    rb = row0 // tr
    c1 = 1.0 - ADAM_B1 ** ADAM_STEP
    c2 = 1.0 - ADAM_B2 ** ADAM_STEP

    def body(w_ref, m_ref, v_ref, gs_ref, g_ref, d_ref, mo_ref, vo_ref):
        g = gs_ref[0]
        for k in range(1, s):
            g = g + gs_ref[k]
        m_new = ADAM_B1 * m_ref[...] + (1.0 - ADAM_B1) * g
        v_new = ADAM_B2 * v_ref[...] + (1.0 - ADAM_B2) * (g * g)
        m_hat = m_new / c1
        v_hat = v_new / c2
        g_ref[...] = g
        mo_ref[...] = m_new
        vo_ref[...] = v_new
        d_ref[...] = -ADAM_LR * (m_hat / (jnp.sqrt(v_hat) + ADAM_EPS) + ADAM_WD * w_ref[...])

    row = pl.BlockSpec((tr, c), lambda i: (i, 0))
    return pl.pallas_call(
        body, name=name, grid=(r // tr,),
        in_specs=[row, row, row, pl.BlockSpec((s, tr, c), lambda i: (0, rb + i, 0))],
        out_specs=[row] * 4,
        out_shape=[jax.ShapeDtypeStruct((r, c), F32)] * 4,
        compiler_params=_cparams(("parallel",)),
    )(w, m, v, gslots)


def slot_sum(slots, *, name):
    s, r, c = slots.shape

    def body(s_ref, o_ref):
        acc = s_ref[0]
        for k in range(1, s):
            acc = acc + s_ref[k]
        o_ref[...] = acc

    return pl.pallas_call(
        body, name=name, grid=(1,),
        in_specs=[pl.BlockSpec((s, r, c), lambda i: (0, 0, 0))],
        out_specs=pl.BlockSpec((r, c), lambda i: (0, 0)),
        out_shape=jax.ShapeDtypeStruct((r, c), F32),
        compiler_params=_cparams(("arbitrary",)),
    )(slots)


class Comm:
    def __init__(self, srcs, broadcast):
        self.srcs = list(srcs)
        self.broadcast = [broadcast] * len(self.srcs) if isinstance(broadcast, bool) else list(broadcast)
        self.n = len(self.srcs)
        self.out_shapes = [jax.ShapeDtypeStruct(((N_DEV,) + s.shape) if b else s.shape, s.dtype)
                           for s, b in zip(self.srcs, self.broadcast)]
        self.sems = [pltpu.SemaphoreType.DMA((self.n,))] * 3

    def _local(self, src_refs, out_refs, loc_sem, a, me):
        src = src_refs[a] if self.broadcast[a] else src_refs[a].at[me]
        return pltpu.make_async_copy(src, out_refs[a].at[me], loc_sem.at[a])

    def start(self, src_refs, out_refs, send_sem, recv_sem, loc_sem):
        x, y, c = lax.axis_index("x"), lax.axis_index("y"), lax.axis_index("c")
        me = 4 * x + 2 * y + c
        for a in range(self.n):
            self._local(src_refs, out_refs, loc_sem, a, me).start()
        for d in range(1, N_DEV):
            px, py, pc = x ^ ((d >> 2) & 1), y ^ ((d >> 1) & 1), c ^ (d & 1)
            peer = 4 * px + 2 * py + pc
            for a in range(self.n):
                src = src_refs[a] if self.broadcast[a] else src_refs[a].at[peer]
                pltpu.make_async_remote_copy(
                    src_ref=src, dst_ref=out_refs[a].at[me],
                    send_sem=send_sem.at[a], recv_sem=recv_sem.at[a],
                    device_id=(px, py, pc), device_id_type=pl.DeviceIdType.MESH).start()

    def wait(self, src_refs, out_refs, send_sem, recv_sem, loc_sem):
        x, y, c = lax.axis_index("x"), lax.axis_index("y"), lax.axis_index("c")
        me = 4 * x + 2 * y + c
        for a in range(self.n):
            seven = out_refs[a].at[pl.ds(0, N_DEV - 1)]
            pltpu.make_async_remote_copy(
                src_ref=seven, dst_ref=seven, send_sem=send_sem.at[a], recv_sem=recv_sem.at[a],
                device_id=(x, y, c), device_id_type=pl.DeviceIdType.MESH).wait()
            self._local(src_refs, out_refs, loc_sem, a, me).wait()


def exchange(srcs, *, broadcast, name):
    comm = Comm(srcs, broadcast)
    n = comm.n

    def body(*refs):
        src_refs, out_refs, sems = refs[:n], refs[n:2 * n], refs[2 * n:]
        comm.start(src_refs, out_refs, *sems)
        comm.wait(src_refs, out_refs, *sems)

    anyspec = pl.BlockSpec(memory_space=pl.ANY)
    return pl.pallas_call(
        body, name=name,
        in_specs=[anyspec] * n, out_specs=[anyspec] * n, out_shape=comm.out_shapes,
        scratch_shapes=comm.sems,
        compiler_params=pltpu.CompilerParams(has_side_effects=True),
    )(*srcs)


def carry_comm(call_kwargs, body, args, comm, n_out, is_first, is_last):
    if comm is None:
        return pl.pallas_call(body, **call_kwargs)(*args), []
    n_in, nc = len(args), comm.n
    n_scr = len(call_kwargs["scratch_shapes"])
    anyspec = pl.BlockSpec(memory_space=pl.ANY)

    def wrapped(*refs):
        ins, csrc = refs[:n_in], refs[n_in:n_in + nc]
        outs = refs[n_in + nc:n_in + nc + n_out]
        cout = refs[n_in + nc + n_out:n_in + 2 * nc + n_out]
        rest = refs[n_in + 2 * nc + n_out:]
        scr, sems = rest[:n_scr], rest[n_scr:]

        @pl.when(is_first())
        def _():
            comm.start(csrc, cout, *sems)

        body(*ins, *outs, *scr)

        @pl.when(is_last())
        def _():
            comm.wait(csrc, cout, *sems)

    kw = dict(call_kwargs)
    kw["in_specs"] = list(kw["in_specs"]) + [anyspec] * nc
    kw["out_specs"] = list(kw["out_specs"]) + [anyspec] * nc
    kw["out_shape"] = list(kw["out_shape"]) + comm.out_shapes
    kw["scratch_shapes"] = list(kw["scratch_shapes"]) + comm.sems
    cp = kw["compiler_params"]
    kw["compiler_params"] = pltpu.CompilerParams(dimension_semantics=cp.dimension_semantics,
                                                 vmem_limit_bytes=cp.vmem_limit_bytes, has_side_effects=True)
    res = pl.pallas_call(wrapped, **kw)(*args, *comm.srcs)
    return res[:n_out], res[n_out:]


def _pack(arrs):
    flat = []
    for a in arrs:
        f = a.reshape(-1).astype(F32)
        flat.append(jnp.pad(f, (0, (-f.shape[0]) % LANES)))
    buf = jnp.concatenate(flat)
    buf = jnp.pad(buf, (0, (-buf.shape[0]) % (8 * LANES)))
    return buf.reshape(-1, LANES)


def _unpack(buf, shapes):
    flat = buf.reshape(-1)
    out, off = [], 0
    for s in shapes:
        sz = int(np.prod(s))
        out.append(flat[off:off + sz].reshape(s))
        off += sz + (-sz) % LANES
    return out


def _win_to_aligned(w):
    o = np.cumsum((0,) + IN_SPLITS)
    seg = lambda i: w[..., o[i]:o[i + 1]]
    pad = jnp.zeros(w.shape[:-1] + (P_WIDTH - IN_WIDTH,), w.dtype)
    return jnp.concatenate([seg(0), seg(1), seg(4), seg(6), seg(7), seg(2), seg(3), seg(5), pad], axis=-1)


def _win_from_aligned(w):
    o = np.cumsum((0,) + IN_SPLITS)
    s = P_SMALL
    return jnp.concatenate([w[..., P_GDN:P_GDN + 768], w[..., P_Z:P_Z + 256], w[..., s:s + 4], w[..., s + 4:s + 8],
                            w[..., P_FOX:P_FOX + 768], w[..., s + 8:s + 12], w[..., P_CONF:P_CONF + 512],
                            w[..., P_SB:P_SB + 768]], axis=-1)


def _row128(vals, col0):
    return jnp.pad(vals.astype(F32)[None, :], ((0, 0), (col0, LANES - col0 - GROUP_HEADS)))


def _ffn_fwd(x, wgu, wd, g, b, tag):
    gu = mm(x, wgu, name=f"{tag}_gu", tm=1024, tn=512, tk=1024)
    h = act_fwd(gu, name=f"{tag}_act")
    y = mm(h, wd, name=f"{tag}_down", tm=512, tn=512, tk=D_FF)
    out, xh, rs = ln_res_fwd(x, y, g, b, 0.5, name=f"{tag}_ln")
    return out, (x, gu, h, xh, rs)


def _ffn_bwd(dout, saved, wgu, wd, g, tag):
    x, gu, h, xh, rs = saved
    dz, dg, db = ln_res_bwd(dout, xh, rs, g, name=f"{tag}_ln_bwd")
    dh = mm(dz, wd, mode="nt", alpha=0.5, name=f"{tag}_dh", tm=512, tn=D_FF // 2, tk=1024)
    dgu = act_bwd(gu, dh, name=f"{tag}_act_bwd")
    dwd = mm(h, dz, mode="tn", alpha=0.5, name=f"{tag}_dwd", tm=D_FF // 2, tn=1024, tk=512)
    dwgu = mm(x, dgu, mode="tn", name=f"{tag}_dwgu", tm=1024, tn=D_FF // 2, tk=512)
    dx = mm(dgu, wgu, mode="nt", add=dz, beta=DN_ALPHA, name=f"{tag}_dx", tm=512, tn=1024, tk=D_FF // 2)
    return dx, dwgu, dwd, dg, db


def _layer_fwd(x, mem, w, tag, comm_gdn=None, on_gdn=None, comm_fox=None, on_fox=None):
    sv = {}
    x1, sv["ffn1"] = _ffn_fwd(x, w["gu1"], w["d1"], w["ln_ffn1_g"], w["ln_ffn1_b"], f"{tag}_ffn1")
    proj = mm(x1, w["win"], name=f"{tag}_inproj", tm=1024, tn=640, tk=1024)
    cqkv = dwconv_fwd(proj, w["gdn_conv_w"], None, col0=P_GDN, name=f"{tag}_gdn_conv")
    ya, states, got = gdn_fwd(cqkv, proj, w["alog"], w["dtb"], w["ng"], name=f"{tag}_gdn", comm=comm_gdn)
    if on_gdn is not None:
        on_gdn(got)
    cum = fox_gate_fwd(proj, w["bf"], name=f"{tag}_fox_gate")
    cum_t = jnp.pad(cum[:, FOX_COL:FOX_COL + GROUP_HEADS].T, ((0, 8 - GROUP_HEADS), (0, 0)))
    yb, lse, got = fox_fwd(proj, cum, cum_t, name=f"{tag}_fox", comm=comm_fox)
    if on_fox is not None:
        on_fox(got)
    u = glu_fwd(proj, name=f"{tag}_glu")
    cc = dwconv_fwd(u, w["conf_dw_w"], w["conf_dw_b"], name=f"{tag}_conf_conv")
    yc = gn_silu_fwd(cc, w["conf_norm_g"], w["conf_norm_b"], name=f"{tag}_conf_norm")
    yd, rsave = sb_fwd(proj, name=f"{tag}_sb")
    ycat = jnp.concatenate([ya, yb, yc, yd], axis=1)
    mix = mm(ycat, w["wout"], name=f"{tag}_outproj")
    x2, xh2, rs2 = ln_res_fwd(x1, mix, w["ln_mix_g"], w["ln_mix_b"], 1.0, name=f"{tag}_ln_mix")
    sv["mix"] = (x1, proj, cqkv, states, cum, cum_t, yb, lse, u, cc, rsave, ycat, xh2, rs2)
    q = mm(x2, w["wq"], name=f"{tag}_memq")
    kv = mm(mem, w["wkv"], name=f"{tag}_memkv", tm=N_MEM)
    att = memattn_fwd(q, kv, name=f"{tag}_memattn")
    mo = mm(att, w["wo"], name=f"{tag}_memo")
    x3, xh3, rs3 = ln_res_fwd(x2, mo, w["ln_mem_g"], w["ln_mem_b"], 1.0, name=f"{tag}_ln_mem")
    sv["mem"] = (x2, q, kv, att, xh3, rs3)
    x4, sv["ffn2"] = _ffn_fwd(x3, w["gu2"], w["d2"], w["ln_ffn2_g"], w["ln_ffn2_b"], f"{tag}_ffn2")
    return x4, sv


def _layer_bwd(dx4, mem, sv, w, tag, plan):
    t = dx4.shape[0]
    gr = {}
    dx3, gr["gu2"], gr["d2"], gr["ln_ffn2_g"], gr["ln_ffn2_b"] = _ffn_bwd(
        dx4, sv["ffn2"], w["gu2"], w["d2"], w["ln_ffn2_g"], f"{tag}_ffn2")
    x2, q, kv, att, xh3, rs3 = sv["mem"]
    dz, gr["ln_mem_g"], gr["ln_mem_b"] = ln_res_bwd(dx3, xh3, rs3, w["ln_mem_g"], name=f"{tag}_ln_mem_bwd")
    datt = mm(dz, w["wo"], mode="nt", name=f"{tag}_datt")
    gr["wo"] = mm(att, dz, mode="tn", name=f"{tag}_dwo", tk=512)
    dq, dkv = memattn_bwd(q, kv, datt, name=f"{tag}_memattn_bwd")
    gr["wq"] = mm(x2, dq, mode="tn", name=f"{tag}_dwq", tk=512)
    gr["wkv"] = mm(mem, dkv, mode="tn", name=f"{tag}_dwkv", tk=N_MEM)
    dx2 = mm(dq, w["wq"], mode="nt", add=dz, beta=DN_ALPHA, name=f"{tag}_dx2")
    x1, proj, cqkv, states, cum, cum_t, yb, lse, u, cc, rsave, ycat, xh2, rs2 = sv["mix"]
    dz, gr["ln_mix_g"], gr["ln_mix_b"] = ln_res_bwd(dx2, xh2, rs2, w["ln_mix_g"], name=f"{tag}_ln_mix_bwd")
    dycat = mm(dz, w["wout"], mode="nt", name=f"{tag}_dycat")
    gr["wout"] = mm(ycat, dz, mode="tn", name=f"{tag}_dwout", tk=512)
    comm_fox, comm_gdn = plan(gr)
    gw = GROUP_WIDTH
    dya, dyb, dyc, dyd = (dycat[:, i * gw:(i + 1) * gw] for i in range(4))
    dq_d, dk_d, dv_d = sb_bwd(proj, rsave, dyd, name=f"{tag}_sb_bwd")
    dcc, gr["conf_norm_g"], gr["conf_norm_b"] = gn_silu_bwd(cc, w["conf_norm_g"], w["conf_norm_b"], dyc,
                                                            name=f"{tag}_conf_norm_bwd")
    du, gr["conf_dw_w"], gr["conf_dw_b"] = dwconv_bwd(dcc, u, w["conf_dw_w"], name=f"{tag}_conf_conv_bwd")
    dglu = glu_bwd(proj, du, name=f"{tag}_glu_bwd")
    dq_b, dk_b, dv_b, dcc, dcr, got_fox = fox_bwd(proj, cum, cum_t, yb, lse, dyb, name=f"{tag}_fox_bwd", comm=comm_fox)
    dcum = dcc + jnp.pad(dcr[:, :GROUP_HEADS, :].transpose(0, 2, 1).reshape(t, GROUP_HEADS),
                   ((0, 0), (FOX_COL, LANES - FOX_COL - GROUP_HEADS)))
    dsm_f, dbf = fox_gate_bwd(dcum, proj, w["bf"], name=f"{tag}_fox_gate_bwd")
    gr["fox_b_f"] = dbf[0, FOX_COL:FOX_COL + GROUP_HEADS]
    dcq, dz_a, dsm_a, dng, dal, ddt, got_gdn = gdn_bwd(cqkv, proj, w["alog"], w["dtb"], w["ng"], states, dya,
                                                       name=f"{tag}_gdn_bwd", comm=comm_gdn)
    gr["gdn_norm_g"] = dng.reshape(GROUP_HEADS, HEAD_DIM).sum(0)
    gr["gdn_a_log"] = dal[0, A_COL:A_COL + GROUP_HEADS]
    gr["gdn_dt_bias"] = ddt[0, A_COL:A_COL + GROUP_HEADS]
    dgq, gr["gdn_conv_w"], _ = dwconv_bwd(dcq, proj, w["gdn_conv_w"], col0=P_GDN, name=f"{tag}_gdn_conv_bwd")
    dproj = jnp.concatenate([dgq, dz_a, dq_b, dk_b, dv_b, dglu, dq_d, dk_d, dv_d, dsm_a + dsm_f], axis=1)
    gr["win"] = mm(x1, dproj, mode="tn", name=f"{tag}_dwin", tm=1024, tn=640, tk=512)
    dx1 = mm(dproj, w["win"], mode="nt", add=dz, beta=DN_ALPHA, name=f"{tag}_dx1", tm=512, tn=1024, tk=640)
    dx0, gr["gu1"], gr["d1"], gr["ln_ffn1_g"], gr["ln_ffn1_b"] = _ffn_bwd(
        dx1, sv["ffn1"], w["gu1"], w["d1"], w["ln_ffn1_g"], f"{tag}_ffn1")
    return dx0, gr, got_fox, got_gdn


SMALL_REPLICATED = ("ln_ffn1_g", "ln_ffn1_b", "gdn_a_log", "gdn_dt_bias", "gdn_norm_g", "fox_b_f", "conf_dw_b",
                    "conf_norm_g", "conf_norm_b", "ln_mix_g", "ln_mix_b", "ln_mem_g", "ln_mem_b", "ln_ffn2_g",
                    "ln_ffn2_b")
SMALL_SHARDED = ("gdn_conv_w", "conf_dw_w")
BIG = ("ffn1_w_gate", "ffn1_w_up", "ffn1_w_down", "w_in", "w_out", "mem_w_q", "mem_w_kv", "mem_w_o",
       "ffn2_w_gate", "ffn2_w_up", "ffn2_w_down")
WEIGHT_ORDER = ("ffn1_w_gate", "ffn1_w_up", "ffn1_w_down", "ln_ffn1_g", "ln_ffn1_b", "w_in", "gdn_conv_w", "gdn_a_log",
                "gdn_dt_bias", "gdn_norm_g", "fox_b_f", "conf_dw_w", "conf_dw_b", "conf_norm_g", "conf_norm_b", "w_out",
                "ln_mix_g", "ln_mix_b", "mem_w_q", "mem_w_kv", "mem_w_o", "ln_mem_g", "ln_mem_b", "ffn2_w_gate",
                "ffn2_w_up", "ffn2_w_down", "ln_ffn2_g", "ln_ffn2_b")


def _step(x, mem, loss_target, wts, ms, vs):
    me = 4 * lax.axis_index("x") + 2 * lax.axis_index("y") + lax.axis_index("c")
    x = x[0]
    mem = mem[0]
    target = loss_target[0]
    rows_s = D_MODEL // N_DEV
    first, rest = ("gu1", "d1", "win"), ("sq", "kv", "gu2", "d2")

    def shards(l):
        c = lambda k: wts[k][l].astype(MXU_DT)
        return dict(gu1=jnp.stack([c("ffn1_w_gate"), c("ffn1_w_up")]), d1=c("ffn1_w_down"),
                    win=_win_to_aligned(wts["w_in"][l]).astype(MXU_DT),
                    sq=jnp.stack([c("w_out"), c("mem_w_q"), c("mem_w_o")]), kv=c("mem_w_kv"),
                    gu2=jnp.stack([c("ffn2_w_gate"), c("ffn2_w_up")]), d2=c("ffn2_w_down"))

    def to_compute_layout(w, keys, got):
        for k, g in zip(keys, got):
            if k in ("gu1", "gu2"):
                w[k] = g.transpose(2, 1, 0, 3).reshape(D_MODEL, 2 * D_FF)
            elif k in ("d1", "d2"):
                w[k] = g.reshape(D_FF, D_MODEL)
            elif k == "win":
                w[k] = g.reshape(D_MODEL, P_WIDTH)
            elif k == "sq":
                full = g.transpose(1, 0, 2, 3).reshape(3, D_MODEL, D_MODEL)
                w["wout"], w["wq"], w["wo"] = full[0], full[1], full[2]
            else:
                w["wkv"] = g.transpose(1, 0, 2).reshape(D_MODEL, 2 * D_MODEL)

    def chunks(gr, keys):
        out = []
        for k in keys:
            if k in ("gu1", "gu2"):
                out.append(gr[k].reshape(D_MODEL, 2, N_DEV, -1).transpose(2, 1, 0, 3))
            elif k in ("d1", "d2"):
                out.append(gr[k].reshape(N_DEV, -1, D_MODEL))
            elif k == "win":
                out.append(gr[k].reshape(N_DEV, rows_s, P_WIDTH))
            elif k == "sq":
                out.append(jnp.stack([gr[n].reshape(N_DEV, rows_s, D_MODEL) for n in ("wout", "wq", "wo")], axis=1))
            else:
                out.append(gr["wkv"].reshape(D_MODEL, N_DEV, -1).transpose(1, 0, 2))
        return out

    sh = [shards(l) for l in range(DEPTH)]
    sm_sh = _pack([wts["gdn_conv_w"], wts["conf_dw_w"]])
    got = exchange([sh[0][k] for k in first] + [sm_sh], broadcast=True, name="gather_first")
    conv_shapes = [wts["gdn_conv_w"].shape, wts["conf_dw_w"].shape]
    parts = [_unpack(got[-1][j], conv_shapes) for j in range(N_DEV)]
    gconv_full = jnp.concatenate([p[0] for p in parts], axis=-1)
    cconv_full = jnp.concatenate([p[1] for p in parts], axis=-1)

    def small_weights(l):
        w = dict(gdn_conv_w=gconv_full[l], conf_dw_w=cconv_full[l],
                 alog=_row128(wts["gdn_a_log"][l], A_COL), dtb=_row128(wts["gdn_dt_bias"][l], A_COL),
                 bf=_row128(wts["fox_b_f"][l], FOX_COL), ng=jnp.tile(wts["gdn_norm_g"][l], GROUP_HEADS)[None, :])
        for k in ("ln_ffn1_g", "ln_ffn1_b", "conf_dw_b", "conf_norm_g", "conf_norm_b", "ln_mix_g", "ln_mix_b",
                  "ln_mem_g", "ln_mem_b", "ln_ffn2_g", "ln_ffn2_b"):
            w[k] = wts[k][l][None, :]
        return w

    lw = [small_weights(l) for l in range(DEPTH)]
    to_compute_layout(lw[0], first, got[:-1])

    h, sv0 = _layer_fwd(x, mem, lw[0], "l0",
                        comm_gdn=Comm([sh[0][k] for k in rest], True),
                        on_gdn=lambda g: to_compute_layout(lw[0], rest, g),
                        comm_fox=Comm([sh[1][k] for k in first + rest], True),
                        on_fox=lambda g: to_compute_layout(lw[1], first + rest, g))
    h, sv1 = _layer_fwd(h, mem, lw[1], "l1")
    dh, lpart = loss_head(h, target, name="loss_head")

    recv = [{}, {}]
    e_ffn, e_mem = ("gu2", "d2"), ("sq", "kv")
    dh, g1, got_a, got_b = _layer_bwd(dh, mem, sv1, lw[1], "l1",
                                      lambda gr: (Comm(chunks(gr, e_ffn), False), Comm(chunks(gr, e_mem), False)))
    recv[1].update(zip(e_ffn, got_a))
    recv[1].update(zip(e_mem, got_b))
    dh, g0, got_a, got_b = _layer_bwd(dh, mem, sv0, lw[0], "l0",
                                      lambda gr: (Comm(chunks(g1, first), False), Comm(chunks(gr, rest), False)))
    recv[1].update(zip(first, got_a))
    recv[0].update(zip(rest, got_b))
    grad_x = dh[None]
    grads = [g0, g1]

    def gl(k):
        return jnp.stack([grads[l][k] for l in range(DEPTH)])

    small_names = SMALL_REPLICATED + SMALL_SHARDED
    small_grads = [gl(k) for k in small_names] + [lpart[0, :1]]
    got = exchange(chunks(g0, first) + [_pack(small_grads)], broadcast=[False] * len(first) + [True],
                   name="scatter_last")
    recv[0].update(zip(first, got[:-1]))
    sm_sum = slot_sum(got[-1], name="sum_small_grads")
    sm_g = _unpack(sm_sum, [g.shape for g in small_grads])
    loss = sm_g[-1][0]
    small_g = dict(zip(small_names, sm_g[:-1]))
    for k in SMALL_SHARDED:
        width = wts[k].shape[-1]
        small_g[k] = lax.dynamic_slice_in_dim(small_g[k], me * width, width, axis=2)

    out_g, out_d, out_m, out_v = {}, {}, {}, {}

    def update(names, key, fix=lambda a: a):
        res = {k: [] for k in names}
        for l in range(DEPTH):
            slots = fix(recv[l][key])
            slots = slots.reshape(N_DEV, -1, slots.shape[-1])
            for i, k in enumerate(names):
                two = lambda a: a[l].reshape(-1, a.shape[-1])
                res[k].append(adamw(two(wts[k]), two(ms[k]), two(vs[k]), slots, row0=i * two(wts[k]).shape[0],
                                    name=f"adamw_{k}_l{l}"))
        for k in names:
            for dst, per_layer in zip((out_g, out_d, out_m, out_v), zip(*res[k])):
                dst[k] = jnp.stack(per_layer).reshape(wts[k].shape)

    update(("ffn1_w_gate", "ffn1_w_up"), "gu1")
    update(("ffn1_w_down",), "d1")
    update(("w_in",), "win", _win_from_aligned)
    update(("w_out", "mem_w_q", "mem_w_o"), "sq")
    update(("mem_w_kv",), "kv")
    update(("ffn2_w_gate", "ffn2_w_up"), "gu2")
    update(("ffn2_w_down",), "d2")

    sw = _pack([wts[k] for k in small_names])
    smm = _pack([ms[k] for k in small_names])
    smv = _pack([vs[k] for k in small_names])
    sg = _pack([small_g[k] for k in small_names])
    res = adamw(sw, smm, smv, sg[None], name="adamw_small")
    shapes = [wts[k].shape for k in small_names]
    for dst, buf in zip((out_g, out_d, out_m, out_v), res):
        for k, a in zip(small_names, _unpack(buf, shapes)):
            dst[k] = a

    return (loss, grad_x, *[out_g[k] for k in WEIGHT_ORDER], *[out_d[k] for k in WEIGHT_ORDER],
            *[out_m[k] for k in WEIGHT_ORDER], *[out_v[k] for k in WEIGHT_ORDER])


def kernel(x, mem, ffn1_w_gate, ffn1_w_up, ffn1_w_down, ln_ffn1_g, ln_ffn1_b, w_in, gdn_conv_w, gdn_a_log, gdn_dt_bias, gdn_norm_g, fox_b_f, conf_dw_w, conf_dw_b, conf_norm_g, conf_norm_b, w_out, ln_mix_g, ln_mix_b, mem_w_q, mem_w_kv, mem_w_o, ln_mem_g, ln_mem_b, ffn2_w_gate, ffn2_w_up, ffn2_w_down, ln_ffn2_g, ln_ffn2_b, loss_target, m_ffn1_w_gate, m_ffn1_w_up, m_ffn1_w_down, m_ln_ffn1_g, m_ln_ffn1_b, m_w_in, m_gdn_conv_w, m_gdn_a_log, m_gdn_dt_bias, m_gdn_norm_g, m_fox_b_f, m_conf_dw_w, m_conf_dw_b, m_conf_norm_g, m_conf_norm_b, m_w_out, m_ln_mix_g, m_ln_mix_b, m_mem_w_q, m_mem_w_kv, m_mem_w_o, m_ln_mem_g, m_ln_mem_b, m_ffn2_w_gate, m_ffn2_w_up, m_ffn2_w_down, m_ln_ffn2_g, m_ln_ffn2_b, v_ffn1_w_gate, v_ffn1_w_up, v_ffn1_w_down, v_ln_ffn1_g, v_ln_ffn1_b, v_w_in, v_gdn_conv_w, v_gdn_a_log, v_gdn_dt_bias, v_gdn_norm_g, v_fox_b_f, v_conf_dw_w, v_conf_dw_b, v_conf_norm_g, v_conf_norm_b, v_w_out, v_ln_mix_g, v_ln_mix_b, v_mem_w_q, v_mem_w_kv, v_mem_w_o, v_ln_mem_g, v_ln_mem_b, v_ffn2_w_gate, v_ffn2_w_up, v_ffn2_w_down, v_ln_ffn2_g, v_ln_ffn2_b):
    args = locals()
    wts = {k: args[k] for k in WEIGHT_ORDER}
    ms = {k: args["m_" + k] for k in WEIGHT_ORDER}
    vs = {k: args["v_" + k] for k in WEIGHT_ORDER}
    return _step(x, mem, loss_target, wts, ms, vs)
```

```python
import functools
import math

import jax
import jax.numpy as jnp
import numpy as np
from jax import lax
from jax.experimental import pallas as pl
from jax.experimental.pallas import tpu as pltpu

F32 = jnp.float32
BF16 = jnp.bfloat16
MXU_DT = jnp.bfloat16
HI = lax.Precision.HIGHEST

N_DEV = 8
VMEM_LIMIT_BYTES = 56 * 1024 * 1024
LANES = 128

D_MODEL = 1024
DEPTH = 2
GROUP_WIDTH = 256
HEAD_DIM = 64
GROUP_HEADS = 4
D_FF = 2816
SHORT_CONV = 4
CONF_KERNEL = 31
CONF_GROUPS = 4
GDN_CHUNK = 64
N_MEM = 256
MEM_HEADS = 4
MEM_HEAD_DIM = 256
DN_ALPHA = float((2 * DEPTH) ** 0.25)
LN_EPS = 1e-5
RMS_EPS = 1e-6
L2_EPS = 1e-6
NEG_BIG = -1e30
IN_SPLITS = (768, 256, 4, 4, 768, 4, 512, 768)
IN_WIDTH = sum(IN_SPLITS)
P_GDN, P_Z, P_FOX, P_CONF, P_SB, P_SMALL = 0, 768, 1024, 1792, 2304, 3072
P_WIDTH = 3200

ADAM_LR = 0.001
ADAM_B1 = 0.9
ADAM_B2 = 0.999
ADAM_EPS = 1e-08
ADAM_WD = 0.01
ADAM_STEP = 10


def _cparams(sem):
    return pltpu.CompilerParams(dimension_semantics=sem, vmem_limit_bytes=VMEM_LIMIT_BYTES)


def _tile(n, pref, align=LANES):
    if n <= pref:
        return n
    t = (pref // align) * align
    while t >= align:
        if n % t == 0:
            return t
        t -= align
    return n


def mm(a, b, *, mode="nn", add=None, alpha=1.0, beta=1.0, out_dtype=F32, name,
       tm=1024, tn=512, tk=1024, comm=None):
    if mode == "nn":
        (m, k), (k2, n) = a.shape, b.shape
    elif mode == "nt":
        (m, k), (n, k2) = a.shape, b.shape
    else:
        (k, m), (k2, n) = a.shape, b.shape
    assert k == k2, (a.shape, b.shape, mode)
    tm = _tile(m, tm, 8 if mode != "tn" else LANES)
    tn = _tile(n, tn)
    tk = _tile(k, tk, LANES if mode != "tn" else 8)
    nk = k // tk
    if mode == "nn":
        a_spec = pl.BlockSpec((tm, tk), lambda i, j, kk: (i, kk))
        b_spec = pl.BlockSpec((tk, tn), lambda i, j, kk: (kk, j))
        dims = (((1,), (0,)), ((), ()))
    elif mode == "nt":
        a_spec = pl.BlockSpec((tm, tk), lambda i, j, kk: (i, kk))
        b_spec = pl.BlockSpec((tn, tk), lambda i, j, kk: (j, kk))
        dims = (((1,), (1,)), ((), ()))
    else:
        a_spec = pl.BlockSpec((tk, tm), lambda i, j, kk: (kk, i))
        b_spec = pl.BlockSpec((tk, tn), lambda i, j, kk: (kk, j))
        dims = (((0,), (0,)), ((), ()))
    o_spec = pl.BlockSpec((tm, tn), lambda i, j, kk: (i, j))
    has_add = add is not None

    def body(*refs):
        if has_add:
            a_ref, b_ref, add_ref, o_ref, acc_ref = refs
        else:
            a_ref, b_ref, o_ref, acc_ref = refs
        kk = pl.program_id(2)

        @pl.when(kk == 0)
        def _():
            acc_ref[...] = jnp.zeros_like(acc_ref)

        acc_ref[...] += lax.dot_general(a_ref[...].astype(MXU_DT), b_ref[...].astype(MXU_DT), dims,
                                        preferred_element_type=F32)

        @pl.when(kk == nk - 1)
        def _():
            r = acc_ref[...]
            if alpha != 1.0:
                r = r * alpha
            if has_add:
                r = r + beta * add_ref[...].astype(F32)
            o_ref[...] = r.astype(out_dtype)

    in_specs = [a_spec, b_spec] + ([o_spec] if has_add else [])
    args = (a, b) + ((add,) if has_add else ())
    grid = (m // tm, n // tn, nk)
    call = dict(name=name, grid=grid, in_specs=in_specs, out_specs=[o_spec],
                out_shape=[jax.ShapeDtypeStruct((m, n), out_dtype)],
                scratch_shapes=[pltpu.VMEM((tm, tn), F32)],
                compiler_params=_cparams(("parallel", "parallel", "arbitrary")))
    (out,), got = carry_comm(call, body, args, comm, 1, *_grid_ends(*grid))
    return out if comm is None else (out, got)


def ln_res_fwd(x, y, g, b, s, *, name):
    t, d = x.shape
    tm = _tile(t, 512, 8)

    def body(x_ref, y_ref, g_ref, b_ref, o_ref, xh_ref, rs_ref):
        z = DN_ALPHA * x_ref[...] + s * y_ref[...]
        mu = jnp.mean(z, axis=-1, keepdims=True)
        zc = z - mu
        var = jnp.mean(zc * zc, axis=-1, keepdims=True)
        rstd = lax.rsqrt(var + LN_EPS)
        xh = zc * rstd
        xh_ref[...] = xh
        rs_ref[...] = rstd
        o_ref[...] = xh * g_ref[...] + b_ref[...]

    row = pl.BlockSpec((tm, d), lambda i: (i, 0))
    vec = pl.BlockSpec((1, d), lambda i: (0, 0))
    return pl.pallas_call(
        body, name=name, grid=(t // tm,),
        in_specs=[row, row, vec, vec],
        out_specs=[row, row, pl.BlockSpec((tm, 1), lambda i: (i, 0))],
        out_shape=[jax.ShapeDtypeStruct((t, d), F32), jax.ShapeDtypeStruct((t, d), F32),
                   jax.ShapeDtypeStruct((t, 1), F32)],
        compiler_params=_cparams(("parallel",)),
    )(x, y, g, b)


def ln_res_bwd(dout, xhat, rstd, g, *, name):
    t, d = dout.shape
    tm = _tile(t, 512, 8)

    def body(do_ref, xh_ref, rs_ref, g_ref, dz_ref, dg_ref, db_ref):
        i = pl.program_id(0)

        @pl.when(i == 0)
        def _():
            dg_ref[...] = jnp.zeros_like(dg_ref)
            db_ref[...] = jnp.zeros_like(db_ref)

        do = do_ref[...]
        xh = xh_ref[...]
        dxh = do * g_ref[...]
        m1 = jnp.mean(dxh, axis=-1, keepdims=True)
        m2 = jnp.mean(dxh * xh, axis=-1, keepdims=True)
        dz_ref[...] = rs_ref[...] * (dxh - m1 - xh * m2)
        dg_ref[...] += jnp.sum(do * xh, axis=0, keepdims=True)
        db_ref[...] += jnp.sum(do, axis=0, keepdims=True)

    row = pl.BlockSpec((tm, d), lambda i: (i, 0))
    vec = pl.BlockSpec((1, d), lambda i: (0, 0))
    return pl.pallas_call(
        body, name=name, grid=(t // tm,),
        in_specs=[row, row, pl.BlockSpec((tm, 1), lambda i: (i, 0)), vec],
        out_specs=[row, vec, vec],
        out_shape=[jax.ShapeDtypeStruct((t, d), F32), jax.ShapeDtypeStruct((1, d), F32),
                   jax.ShapeDtypeStruct((1, d), F32)],
        compiler_params=_cparams(("arbitrary",)),
    )(dout, xhat, rstd, g)


def _sigmoid(x):
    return 1.0 / (1.0 + jnp.exp(-x))


def act_fwd(gu, *, name):
    t, f2 = gu.shape
    f = f2 // 2
    tm = _tile(t, 256, 8)

    def body(gu_ref, h_ref):
        g = gu_ref[:, :f]
        h_ref[...] = (g * _sigmoid(g) * gu_ref[:, f:]).astype(h_ref.dtype)

    return pl.pallas_call(
        body, name=name, grid=(t // tm,),
        in_specs=[pl.BlockSpec((tm, f2), lambda i: (i, 0))],
        out_specs=pl.BlockSpec((tm, f), lambda i: (i, 0)),
        out_shape=jax.ShapeDtypeStruct((t, f), MXU_DT),
        compiler_params=_cparams(("parallel",)),
    )(gu)


def act_bwd(gu, dh, *, name):
    t, f2 = gu.shape
    f = f2 // 2
    tm = _tile(t, 256, 8)

    def body(gu_ref, dh_ref, o_ref):
        g = gu_ref[:, :f]
        u = gu_ref[:, f:]
        dh = dh_ref[...]
        sg = _sigmoid(g)
        o_ref[:, f:] = (dh * g * sg).astype(o_ref.dtype)
        o_ref[:, :f] = (dh * u * sg * (1.0 + g * (1.0 - sg))).astype(o_ref.dtype)

    return pl.pallas_call(
        body, name=name, grid=(t // tm,),
        in_specs=[pl.BlockSpec((tm, f2), lambda i: (i, 0)), pl.BlockSpec((tm, f), lambda i: (i, 0))],
        out_specs=pl.BlockSpec((tm, f2), lambda i: (i, 0)),
        out_shape=jax.ShapeDtypeStruct((t, f2), MXU_DT),
        compiler_params=_cparams(("parallel",)),
    )(gu, dh)


def loss_head(y, target, *, name):
    t, d = y.shape
    tm = _tile(t, 512, 8)

    def body(y_ref, t_ref, dy_ref, l_ref):
        i = pl.program_id(0)

        @pl.when(i == 0)
        def _():
            l_ref[...] = jnp.zeros_like(l_ref)

        err = y_ref[...] - t_ref[...]
        dy_ref[...] = err * (1.0 / d)
        part = jnp.sum(jnp.sum(err * err, axis=-1, keepdims=True), axis=0, keepdims=True)
        l_ref[...] += jnp.broadcast_to(part * (0.5 / d), l_ref.shape)

    row = pl.BlockSpec((tm, d), lambda i: (i, 0))
    return pl.pallas_call(
        body, name=name, grid=(t // tm,),
        in_specs=[row, row],
        out_specs=[row, pl.BlockSpec((1, LANES), lambda i: (0, 0))],
        out_shape=[jax.ShapeDtypeStruct((t, d), F32), jax.ShapeDtypeStruct((1, LANES), F32)],
        compiler_params=_cparams(("arbitrary",)),
    )(y, target)


def _dot(a, b):
    return lax.dot_general(a, b, (((1,), (0,)), ((), ())), preferred_element_type=F32)


def _dot_nt(a, b):
    return lax.dot_general(a, b, (((1,), (1,)), ((), ())), preferred_element_type=F32)


def _dot_tn(a, b):
    return lax.dot_general(a, b, (((0,), (0,)), ((), ())), preferred_element_type=F32)


def _dot_hi(a, b):
    return lax.dot_general(a, b, (((1,), (0,)), ((), ())), preferred_element_type=F32, precision=HI)


def _dot_nt_hi(a, b):
    return lax.dot_general(a, b, (((1,), (1,)), ((), ())), preferred_element_type=F32, precision=HI)


def _split_dot(x, u):
    hi = x.astype(MXU_DT)
    lo = (x - hi.astype(F32)).astype(MXU_DT)
    return _dot(hi, u) + _dot(lo, u)


def _mem_probs(q_ref, kv_ref, h):
    lo = h * MEM_HEAD_DIM
    qh = q_ref[:, lo:lo + MEM_HEAD_DIM].astype(MXU_DT)
    kh = kv_ref[:, lo:lo + MEM_HEAD_DIM].astype(MXU_DT)
    s = _dot_nt(qh, kh) * (MEM_HEAD_DIM ** -0.5)
    s = s - jnp.max(s, axis=-1, keepdims=True)
    p = jnp.exp(s)
    return p / jnp.sum(p, axis=-1, keepdims=True), qh, kh


def memattn_fwd(q, kv, *, name):
    t, d = q.shape
    tm = _tile(t, 512, 8)

    def body(q_ref, kv_ref, o_ref):
        for h in range(MEM_HEADS):
            lo = h * MEM_HEAD_DIM
            p, _, _ = _mem_probs(q_ref, kv_ref, h)
            vh = kv_ref[:, d + lo:d + lo + MEM_HEAD_DIM].astype(MXU_DT)
            o_ref[:, lo:lo + MEM_HEAD_DIM] = _dot(p.astype(MXU_DT), vh)

    return pl.pallas_call(
        body, name=name, grid=(t // tm,),
        in_specs=[pl.BlockSpec((tm, d), lambda i: (i, 0)), pl.BlockSpec(kv.shape, lambda i: (0, 0))],
        out_specs=pl.BlockSpec((tm, d), lambda i: (i, 0)),
        out_shape=jax.ShapeDtypeStruct((t, d), F32),
        compiler_params=_cparams(("parallel",)),
    )(q, kv)


def memattn_bwd(q, kv, datt, *, name):
    t, d = q.shape
    tm = _tile(t, 512, 8)
    scale = MEM_HEAD_DIM ** -0.5

    def body(q_ref, kv_ref, da_ref, dq_ref, dkv_ref):
        @pl.when(pl.program_id(0) == 0)
        def _():
            dkv_ref[...] = jnp.zeros_like(dkv_ref)

        for h in range(MEM_HEADS):
            lo = h * MEM_HEAD_DIM
            p, qh, kh = _mem_probs(q_ref, kv_ref, h)
            vh = kv_ref[:, d + lo:d + lo + MEM_HEAD_DIM].astype(MXU_DT)
            da = da_ref[:, lo:lo + MEM_HEAD_DIM].astype(MXU_DT)
            dp = _dot_nt(da, vh)
            ds = p * (dp - jnp.sum(dp * p, axis=-1, keepdims=True))
            dsb = ds.astype(MXU_DT)
            dq_ref[:, lo:lo + MEM_HEAD_DIM] = _dot(dsb, kh) * scale
            dkv_ref[:, lo:lo + MEM_HEAD_DIM] += _dot_tn(dsb, qh) * scale
            dkv_ref[:, d + lo:d + lo + MEM_HEAD_DIM] += _dot_tn(p.astype(MXU_DT), da)

    row = pl.BlockSpec((tm, d), lambda i: (i, 0))
    full = pl.BlockSpec(kv.shape, lambda i: (0, 0))
    return pl.pallas_call(
        body, name=name, grid=(t // tm,),
        in_specs=[row, full, row],
        out_specs=[row, full],
        out_shape=[jax.ShapeDtypeStruct((t, d), F32), jax.ShapeDtypeStruct(kv.shape, F32)],
        compiler_params=_cparams(("arbitrary",)),
    )(q, kv, datt)


def _halo(k):
    return 8 * ((k - 1 + 7) // 8)


def dwconv_fwd(u, w, bias, *, col0=0, width=None, name):
    t = u.shape[0]
    kk, c = w.shape
    width = c if width is None else width
    assert width == c and col0 % c == 0
    cb = col0 // c
    hb = _halo(kk)
    tm = _tile(t, 512, hb)
    r = tm // hb
    has_bias = bias is not None

    def body(*refs):
        if has_bias:
            prev_ref, cur_ref, w_ref, b_ref, o_ref, scr = refs
        else:
            prev_ref, cur_ref, w_ref, o_ref, scr = refs
        i = pl.program_id(0)
        scr[0:hb, :] = jnp.where(i == 0, 0.0, prev_ref[...])
        scr[hb:hb + tm, :] = cur_ref[...]
        acc = jnp.zeros((tm, c), F32)
        for k in range(kk):
            acc = acc + w_ref[k:k + 1, :] * scr[pl.ds(hb - (kk - 1) + k, tm), :]
        if has_bias:
            acc = acc + b_ref[...]
        o_ref[...] = acc

    in_specs = [pl.BlockSpec((hb, c), lambda i: (jnp.maximum(i * r - 1, 0), cb)),
                pl.BlockSpec((tm, c), lambda i: (i, cb)),
                pl.BlockSpec((kk, c), lambda i: (0, 0))]
    args = [u, u, w]
    if has_bias:
        in_specs.append(pl.BlockSpec((1, c), lambda i: (0, 0)))
        args.append(bias)
    return pl.pallas_call(
        body, name=name, grid=(t // tm,),
        in_specs=in_specs,
        out_specs=pl.BlockSpec((tm, c), lambda i: (i, 0)),
        out_shape=jax.ShapeDtypeStruct((t, c), F32),
        scratch_shapes=[pltpu.VMEM((hb + tm, c), F32)],
        compiler_params=_cparams(("parallel",)),
    )(*args)


def dwconv_bwd(dc, u, w, *, col0=0, name):
    t, c = dc.shape
    kk = w.shape[0]
    assert col0 % c == 0
    cb = col0 // c
    hb = _halo(kk)
    tm = _tile(t, 512, hb)
    r = tm // hb
    n = t // tm

    def body(dcur_ref, dnext_ref, uprev_ref, ucur_ref, w_ref, du_ref, dw_ref, db_ref, sd, su):
        i = pl.program_id(0)

        @pl.when(i == 0)
        def _():
            dw_ref[...] = jnp.zeros_like(dw_ref)
            db_ref[...] = jnp.zeros_like(db_ref)

        dcur = dcur_ref[...]
        sd[0:tm, :] = dcur
        sd[tm:tm + hb, :] = jnp.where(i == n - 1, 0.0, dnext_ref[...])
        su[0:hb, :] = jnp.where(i == 0, 0.0, uprev_ref[...])
        su[hb:hb + tm, :] = ucur_ref[...]
        acc = jnp.zeros((tm, c), F32)
        for k in range(kk):
            acc = acc + w_ref[k:k + 1, :] * sd[pl.ds(kk - 1 - k, tm), :]
            dw_ref[k:k + 1, :] += jnp.sum(dcur * su[pl.ds(hb - (kk - 1) + k, tm), :], axis=0, keepdims=True)
        du_ref[...] = acc
        db_ref[...] += jnp.sum(dcur, axis=0, keepdims=True)

    return pl.pallas_call(
        body, name=name, grid=(n,),
        in_specs=[pl.BlockSpec((tm, c), lambda i: (i, 0)),
                  pl.BlockSpec((hb, c), lambda i: (jnp.minimum((i + 1) * r, n * r - 1), 0)),
                  pl.BlockSpec((hb, c), lambda i: (jnp.maximum(i * r - 1, 0), cb)),
                  pl.BlockSpec((tm, c), lambda i: (i, cb)),
                  pl.BlockSpec((kk, c), lambda i: (0, 0))],
        out_specs=[pl.BlockSpec((tm, c), lambda i: (i, 0)),
                   pl.BlockSpec((kk, c), lambda i: (0, 0)),
                   pl.BlockSpec((1, c), lambda i: (0, 0))],
        out_shape=[jax.ShapeDtypeStruct((t, c), F32), jax.ShapeDtypeStruct((kk, c), F32),
                   jax.ShapeDtypeStruct((1, c), F32)],
        scratch_shapes=[pltpu.VMEM((tm + hb, c), F32), pltpu.VMEM((hb + tm, c), F32)],
        compiler_params=_cparams(("arbitrary",)),
    )(dc, dc, u, u, w)


def glu_fwd(proj, *, name):
    t = proj.shape[0]
    c = GROUP_WIDTH
    tm = _tile(t, 1024, 8)
    vb, gb = P_CONF // c, P_CONF // c + 1

    def body(v_ref, g_ref, o_ref):
        o_ref[...] = v_ref[...] * _sigmoid(g_ref[...])

    return pl.pallas_call(
        body, name=name, grid=(t // tm,),
        in_specs=[pl.BlockSpec((tm, c), lambda i: (i, vb)), pl.BlockSpec((tm, c), lambda i: (i, gb))],
        out_specs=pl.BlockSpec((tm, c), lambda i: (i, 0)),
        out_shape=jax.ShapeDtypeStruct((t, c), F32),
        compiler_params=_cparams(("parallel",)),
    )(proj, proj)


def glu_bwd(proj, du, *, name):
    t = proj.shape[0]
    c = GROUP_WIDTH
    tm = _tile(t, 1024, 8)
    vb, gb = P_CONF // c, P_CONF // c + 1

    def body(v_ref, g_ref, du_ref, o_ref):
        sg = _sigmoid(g_ref[...])
        du = du_ref[...]
        o_ref[:, :c] = du * sg
        o_ref[:, c:] = du * v_ref[...] * sg * (1.0 - sg)

    return pl.pallas_call(
        body, name=name, grid=(t // tm,),
        in_specs=[pl.BlockSpec((tm, c), lambda i: (i, vb)), pl.BlockSpec((tm, c), lambda i: (i, gb)),
                  pl.BlockSpec((tm, c), lambda i: (i, 0))],
        out_specs=pl.BlockSpec((tm, 2 * c), lambda i: (i, 0)),
        out_shape=jax.ShapeDtypeStruct((t, 2 * c), F32),
        compiler_params=_cparams(("parallel",)),
    )(proj, proj, du)


def _group_mean_matrix(c, groups):
    gsz = c // groups
    ri = lax.broadcasted_iota(jnp.int32, (c, c), 0) // gsz
    ci = lax.broadcasted_iota(jnp.int32, (c, c), 1) // gsz
    return jnp.where(ri == ci, 1.0 / gsz, 0.0).astype(F32)


def gn_silu_fwd(cx, gamma, beta, *, name):
    t, c = cx.shape
    tm = _tile(t, 1024, 8)

    def body(c_ref, g_ref, b_ref, o_ref):
        gm = _group_mean_matrix(c, CONF_GROUPS)
        x = c_ref[...]
        mu = _dot_hi(x, gm)
        xc = x - mu
        var = _dot_hi(xc * xc, gm)
        a = xc * lax.rsqrt(var + LN_EPS) * g_ref[...] + b_ref[...]
        o_ref[...] = a * _sigmoid(a)

    row = pl.BlockSpec((tm, c), lambda i: (i, 0))
    vec = pl.BlockSpec((1, c), lambda i: (0, 0))
    return pl.pallas_call(
        body, name=name, grid=(t // tm,),
        in_specs=[row, vec, vec], out_specs=row,
        out_shape=jax.ShapeDtypeStruct((t, c), F32),
        compiler_params=_cparams(("parallel",)),
    )(cx, gamma, beta)


def gn_silu_bwd(cx, gamma, beta, dy, *, name):
    t, c = cx.shape
    tm = _tile(t, 1024, 8)

    def body(c_ref, g_ref, b_ref, dy_ref, dc_ref, dg_ref, db_ref):
        @pl.when(pl.program_id(0) == 0)
        def _():
            dg_ref[...] = jnp.zeros_like(dg_ref)
            db_ref[...] = jnp.zeros_like(db_ref)

        gm = _group_mean_matrix(c, CONF_GROUPS)
        x = c_ref[...]
        mu = _dot_hi(x, gm)
        xc = x - mu
        var = _dot_hi(xc * xc, gm)
        rstd = lax.rsqrt(var + LN_EPS)
        nrm = xc * rstd
        a = nrm * g_ref[...] + b_ref[...]
        sa = _sigmoid(a)
        da = dy_ref[...] * sa * (1.0 + a * (1.0 - sa))
        dg_ref[...] += jnp.sum(da * nrm, axis=0, keepdims=True)
        db_ref[...] += jnp.sum(da, axis=0, keepdims=True)
        dn = da * g_ref[...]
        dc_ref[...] = rstd * (dn - _dot_hi(dn, gm) - nrm * _dot_hi(dn * nrm, gm))

    row = pl.BlockSpec((tm, c), lambda i: (i, 0))
    vec = pl.BlockSpec((1, c), lambda i: (0, 0))
    return pl.pallas_call(
        body, name=name, grid=(t // tm,),
        in_specs=[row, vec, vec, row], out_specs=[row, vec, vec],
        out_shape=[jax.ShapeDtypeStruct((t, c), F32), jax.ShapeDtypeStruct((1, c), F32),
                   jax.ShapeDtypeStruct((1, c), F32)],
        compiler_params=_cparams(("arbitrary",)),
    )(cx, gamma, beta, dy)


FOX_COL = 8
SMALL_BLK = P_SMALL // LANES


def _log_sigmoid(x):
    return jnp.minimum(x, 0.0) - jnp.log(1.0 + jnp.exp(-jnp.abs(x)))


def _fox_cols(shape):
    col = lax.broadcasted_iota(jnp.int32, shape, 1)
    return (col >= FOX_COL) & (col < FOX_COL + GROUP_HEADS)


def fox_gate_fwd(proj, bvec, *, name):
    t = proj.shape[0]
    tm = _tile(t, 256, 8)

    def body(s_ref, b_ref, o_ref, carry):
        @pl.when(pl.program_id(0) == 0)
        def _():
            carry[...] = jnp.zeros_like(carry)

        lf = jnp.where(_fox_cols((tm, LANES)), _log_sigmoid(s_ref[...] + b_ref[...]), 0.0)
        ri = lax.broadcasted_iota(jnp.int32, (tm, tm), 0)
        ci = lax.broadcasted_iota(jnp.int32, (tm, tm), 1)
        cum = _dot_hi(jnp.where(ri >= ci, 1.0, 0.0).astype(F32), lf) + carry[...]
        o_ref[...] = cum
        carry[...] = cum[tm - 1:tm, :]

    return pl.pallas_call(
        body, name=name, grid=(t // tm,),
        in_specs=[pl.BlockSpec((tm, LANES), lambda i: (i, SMALL_BLK)), pl.BlockSpec((1, LANES), lambda i: (0, 0))],
        out_specs=pl.BlockSpec((tm, LANES), lambda i: (i, 0)),
        out_shape=jax.ShapeDtypeStruct((t, LANES), F32),
        scratch_shapes=[pltpu.VMEM((1, LANES), F32)],
        compiler_params=_cparams(("arbitrary",)),
    )(proj, bvec)


def fox_gate_bwd(dcum, proj, bvec, *, name):
    t = proj.shape[0]
    tm = _tile(t, 256, 8)
    n = t // tm

    def body(d_ref, s_ref, b_ref, o_ref, db_ref, carry):
        @pl.when(pl.program_id(0) == 0)
        def _():
            carry[...] = jnp.zeros_like(carry)
            db_ref[...] = jnp.zeros_like(db_ref)

        ri = lax.broadcasted_iota(jnp.int32, (tm, tm), 0)
        ci = lax.broadcasted_iota(jnp.int32, (tm, tm), 1)
        dlf = _dot_hi(jnp.where(ri <= ci, 1.0, 0.0).astype(F32), d_ref[...]) + carry[...]
        carry[...] = dlf[0:1, :]
        x = s_ref[...] + b_ref[...]
        dx = jnp.where(_fox_cols((tm, LANES)), dlf * (1.0 - _sigmoid(x)), 0.0)
        o_ref[...] = dx
        db_ref[...] += jnp.sum(dx, axis=0, keepdims=True)

    return pl.pallas_call(
        body, name=name, grid=(n,),
        in_specs=[pl.BlockSpec((tm, LANES), lambda i: (n - 1 - i, 0)),
                  pl.BlockSpec((tm, LANES), lambda i: (n - 1 - i, SMALL_BLK)),
                  pl.BlockSpec((1, LANES), lambda i: (0, 0))],
        out_specs=[pl.BlockSpec((tm, LANES), lambda i: (n - 1 - i, 0)), pl.BlockSpec((1, LANES), lambda i: (0, 0))],
        out_shape=[jax.ShapeDtypeStruct((t, LANES), F32), jax.ShapeDtypeStruct((1, LANES), F32)],
        scratch_shapes=[pltpu.VMEM((1, LANES), F32)],
        compiler_params=_cparams(("arbitrary",)),
    )(dcum, proj, bvec)


def _head_masks(c):
    lane_head = lax.broadcasted_iota(jnp.int32, (1, c), 1) // HEAD_DIM
    return [lane_head == h for h in range(GROUP_HEADS)]


def _attn_tiles(t, tq, tk):
    tq = _tile(t, tq, 8)
    tk = _tile(t, tk, LANES)
    return tq, tk, t // tq, t // tk


def _grid_ends(*sizes):
    first = lambda: functools.reduce(lambda a, b: a & b, [pl.program_id(d) == 0 for d in range(len(sizes))])
    last = lambda: functools.reduce(lambda a, b: a & b, [pl.program_id(d) == s - 1 for d, s in enumerate(sizes)])
    return first, last


def fox_fwd(proj, cum, cum_t, *, name, tq=512, tk=512, comm=None):
    t = proj.shape[0]
    c = GROUP_WIDTH
    tq, tk, nq, nk = _attn_tiles(t, tq, tk)
    qb = P_FOX // c
    scale = HEAD_DIM ** -0.5

    def last_j(i):
        return ((i + 1) * tq - 1) // tk

    def body(q_ref, k_ref, v_ref, cc_ref, cr_ref, o_ref, lse_ref, m_scr, l_scr, acc_scr):
        i = pl.program_id(0)
        j = pl.program_id(1)
        masks = _head_masks(c)

        @pl.when(j == 0)
        def _():
            m_scr[...] = jnp.full_like(m_scr, NEG_BIG)
            l_scr[...] = jnp.zeros_like(l_scr)
            acc_scr[...] = jnp.zeros_like(acc_scr)

        @pl.when(j <= last_j(i))
        def _():
            q = q_ref[...]
            kb = k_ref[...].astype(MXU_DT)
            vb = v_ref[...].astype(MXU_DT)
            row = i * tq + lax.broadcasted_iota(jnp.int32, (tq, tk), 0)
            col = j * tk + lax.broadcasted_iota(jnp.int32, (tq, tk), 1)
            causal = col <= row
            acc = acc_scr[...]
            for h in range(GROUP_HEADS):
                qh = jnp.where(masks[h], q, 0.0).astype(MXU_DT)
                s = _dot_nt(qh, kb) * scale + (cc_ref[:, FOX_COL + h:FOX_COL + h + 1] - cr_ref[h:h + 1, :])
                s = jnp.where(causal, s, NEG_BIG)
                m_old = m_scr[h]
                m_new = jnp.maximum(m_old, jnp.max(s, axis=-1, keepdims=True))
                p = jnp.exp(s - m_new)
                alpha = jnp.exp(m_old - m_new)
                l_scr[h] = alpha * l_scr[h] + jnp.sum(p, axis=-1, keepdims=True)
                m_scr[h] = m_new
                acc = jnp.where(masks[h], alpha * acc + _dot(p.astype(MXU_DT), vb), acc)
            acc_scr[...] = acc

        @pl.when(j == last_j(i))
        def _():
            acc = acc_scr[...]
            o = jnp.zeros_like(acc)
            lse = jnp.zeros((tq, LANES), F32)
            lane = lax.broadcasted_iota(jnp.int32, (1, LANES), 1)
            for h in range(GROUP_HEADS):
                o = jnp.where(masks[h], acc / l_scr[h], o)
                lse = jnp.where(lane == h, m_scr[h] + jnp.log(l_scr[h]), lse)
            o_ref[...] = o
            lse_ref[...] = lse

    def kvmap(blk):
        return lambda i, j: (jnp.minimum(j, last_j(i)), blk)

    call = dict(
        name=name, grid=(nq, nk),
        in_specs=[pl.BlockSpec((tq, c), lambda i, j: (i, qb)),
                  pl.BlockSpec((tk, c), kvmap(qb + 1)),
                  pl.BlockSpec((tk, c), kvmap(qb + 2)),
                  pl.BlockSpec((tq, LANES), lambda i, j: (i, 0)),
                  pl.BlockSpec((8, tk), lambda i, j: (0, jnp.minimum(j, last_j(i))))],
        out_specs=[pl.BlockSpec((tq, c), lambda i, j: (i, 0)), pl.BlockSpec((tq, LANES), lambda i, j: (i, 0))],
        out_shape=[jax.ShapeDtypeStruct((t, c), F32), jax.ShapeDtypeStruct((t, LANES), F32)],
        scratch_shapes=[pltpu.VMEM((GROUP_HEADS, tq, 1), F32), pltpu.VMEM((GROUP_HEADS, tq, 1), F32),
                        pltpu.VMEM((tq, c), F32)],
        compiler_params=_cparams(("arbitrary", "arbitrary")),
    )
    outs, got = carry_comm(call, body, (proj, proj, proj, cum, cum_t), comm, 2, *_grid_ends(nq, nk))
    return (*outs, got)


def fox_bwd(proj, cum, cum_t, o, lse, do, *, name, tq=512, tk=512, comm=None):
    t = proj.shape[0]
    c = GROUP_WIDTH
    tq, tk, nq, nk = _attn_tiles(t, tq, tk)
    qb = P_FOX // c
    scale = HEAD_DIM ** -0.5

    def last_j(i):
        return ((i + 1) * tq - 1) // tk

    def body(q_ref, k_ref, v_ref, cc_ref, cr_ref, o_ref, lse_ref, do_ref,
             dq_ref, dk_ref, dv_ref, dcc_ref, dcr_ref, dq_scr, rs_scr):
        i = pl.program_id(0)
        j = pl.program_id(1)
        masks = _head_masks(c)

        @pl.when((i == 0) & (j == 0))
        def _():
            dk_ref[...] = jnp.zeros_like(dk_ref)
            dv_ref[...] = jnp.zeros_like(dv_ref)
            dcr_ref[...] = jnp.zeros_like(dcr_ref)

        @pl.when(j == 0)
        def _():
            dq_scr[...] = jnp.zeros_like(dq_scr)
            rs_scr[...] = jnp.zeros_like(rs_scr)

        @pl.when(j <= last_j(i))
        def _():
            q = q_ref[...]
            qf = q.astype(MXU_DT)
            kb = k_ref[...].astype(MXU_DT)
            vb = v_ref[...].astype(MXU_DT)
            do = do_ref[...]
            dob = do.astype(MXU_DT)
            doo = do * o_ref[...]
            row = i * tq + lax.broadcasted_iota(jnp.int32, (tq, tk), 0)
            col = j * tk + lax.broadcasted_iota(jnp.int32, (tq, tk), 1)
            causal = col <= row
            dq = dq_scr[...]
            dk_upd = jnp.zeros((tk, c), F32)
            dv_upd = jnp.zeros((tk, c), F32)
            for h in range(GROUP_HEADS):
                qh = jnp.where(masks[h], q, 0.0).astype(MXU_DT)
                s = _dot_nt(qh, kb) * scale + (cc_ref[:, FOX_COL + h:FOX_COL + h + 1] - cr_ref[h:h + 1, :])
                p = jnp.where(causal, jnp.exp(s - lse_ref[:, h:h + 1]), 0.0)
                delta = jnp.sum(jnp.where(masks[h], doo, 0.0), axis=-1, keepdims=True)
                doh = jnp.where(masks[h], do, 0.0).astype(MXU_DT)
                ds = p * (_dot_nt(doh, vb) - delta)
                dsb = ds.astype(MXU_DT)
                dq = jnp.where(masks[h], dq + _dot(dsb, kb) * scale, dq)
                dk_upd = jnp.where(masks[h], _dot_tn(dsb, qf) * scale, dk_upd)
                dv_upd = jnp.where(masks[h], _dot_tn(p.astype(MXU_DT), dob), dv_upd)
                dcr_ref[j, h:h + 1, :] += -jnp.sum(ds, axis=0, keepdims=True)
                rs_scr[h] += jnp.sum(ds, axis=-1, keepdims=True)
            dq_scr[...] = dq
            rows = pl.ds(pl.multiple_of(j * tk, tk), tk)
            dk_ref[rows, :] += dk_upd
            dv_ref[rows, :] += dv_upd

        @pl.when(j == last_j(i))
        def _():
            dq_ref[...] = dq_scr[...]
            lane = lax.broadcasted_iota(jnp.int32, (1, LANES), 1)
            dcc = jnp.zeros((tq, LANES), F32)
            for h in range(GROUP_HEADS):
                dcc = jnp.where(lane == FOX_COL + h, rs_scr[h], dcc)
            dcc_ref[...] = dcc

    def kvmap(blk):
        return lambda i, j: (jnp.minimum(j, last_j(i)), blk)

    qrow = lambda i, j: (i, 0)
    whole = lambda i, j: (0, 0)
    call = dict(
        name=name, grid=(nq, nk),
        in_specs=[pl.BlockSpec((tq, c), lambda i, j: (i, qb)),
                  pl.BlockSpec((tk, c), kvmap(qb + 1)),
                  pl.BlockSpec((tk, c), kvmap(qb + 2)),
                  pl.BlockSpec((tq, LANES), qrow),
                  pl.BlockSpec((8, tk), lambda i, j: (0, jnp.minimum(j, last_j(i)))),
                  pl.BlockSpec((tq, c), qrow),
                  pl.BlockSpec((tq, LANES), qrow),
                  pl.BlockSpec((tq, c), qrow)],
        out_specs=[pl.BlockSpec((tq, c), qrow),
                   pl.BlockSpec((t, c), whole),
                   pl.BlockSpec((t, c), whole),
                   pl.BlockSpec((tq, LANES), qrow),
                   pl.BlockSpec((nk, 8, tk), lambda i, j: (0, 0, 0))],
        out_shape=[jax.ShapeDtypeStruct((t, c), F32), jax.ShapeDtypeStruct((t, c), F32),
                   jax.ShapeDtypeStruct((t, c), F32), jax.ShapeDtypeStruct((t, LANES), F32),
                   jax.ShapeDtypeStruct((nk, 8, tk), F32)],
        scratch_shapes=[pltpu.VMEM((tq, c), F32), pltpu.VMEM((GROUP_HEADS, tq, 1), F32)],
        compiler_params=_cparams(("arbitrary", "arbitrary")),
    )
    outs, got = carry_comm(call, body, (proj, proj, proj, cum, cum_t, o, lse, do), comm, 5, *_grid_ends(nq, nk))
    return (*outs, got)


SB_DEAD = -110.0


def _sb_logs(z, strict):
    tt = jnp.log(1.0 + jnp.exp(-jnp.abs(z)))
    log_keep = jnp.where(strict, -(jnp.maximum(z, 0.0) + tt), 0.0)
    log_beta = jnp.minimum(z, 0.0) - tt
    return log_keep, log_beta


def _tri(n, upper):
    a = lax.broadcasted_iota(jnp.int32, (n, n), 0)
    b = lax.broadcasted_iota(jnp.int32, (n, n), 1)
    return jnp.where((a < b) if upper else (a > b), 1.0, 0.0).astype(MXU_DT)


def _sb_carry_lane(jj, h):
    return GROUP_HEADS * jj + h


def sb_fwd(proj, *, name, tq=512, tk=256):
    t = proj.shape[0]
    c = GROUP_WIDTH
    tq, tk, nq, nk = _attn_tiles(t, tq, tk)
    assert nk * GROUP_HEADS <= LANES
    qb = P_SB // c
    scale = HEAD_DIM ** -0.5

    def body(q_ref, k_ref, v_ref, o_ref, rs_ref, r_scr, acc_scr):
        i = pl.program_id(0)
        last = ((i + 1) * tq - 1) // tk
        masks = _head_masks(c)
        q = q_ref[...]
        qh = [jnp.where(masks[h], q, 0.0).astype(MXU_DT) for h in range(GROUP_HEADS)]
        lane = lax.broadcasted_iota(jnp.int32, (1, LANES), 1)
        later = _tri(tk, upper=False)
        r_scr[...] = jnp.zeros_like(r_scr)
        acc_scr[...] = jnp.zeros_like(acc_scr)
        rs_ref[...] = jnp.full((tq, LANES), 2.0 * SB_DEAD, F32)

        def step(state):
            jj, _ = state
            j = last - jj
            rows = pl.ds(pl.multiple_of(j * tk, tk), tk)
            kb = k_ref[rows, :].astype(MXU_DT)
            vb = v_ref[rows, :].astype(MXU_DT)
            row = i * tq + lax.broadcasted_iota(jnp.int32, (tq, tk), 0)
            col = j * tk + lax.broadcasted_iota(jnp.int32, (tq, tk), 1)
            strict = col < row
            acc = acc_scr[...]
            rs = rs_ref[...]
            for h in range(GROUP_HEADS):
                z = _dot_nt(qh[h], kb) * scale
                log_keep, log_beta = _sb_logs(z, strict)
                r_old = r_scr[h]
                rs = jnp.where(lane == _sb_carry_lane(jj, h), r_old, rs)
                rest = r_old + _split_dot(log_keep, later)
                w = jnp.where(strict, jnp.exp(log_beta + rest), 0.0)
                acc = jnp.where(masks[h], acc + _dot(w.astype(MXU_DT), vb), acc)
                r_scr[h] = r_old + jnp.sum(log_keep, axis=-1, keepdims=True)
            acc_scr[...] = acc
            rs_ref[...] = rs
            return jj + 1, jnp.max(r_scr[...])

        lax.while_loop(lambda s: (s[0] <= last) & (s[1] > SB_DEAD), step, (jnp.int32(0), jnp.float32(0.0)))
        o_ref[...] = acc_scr[...]

    resident = lambda blk: pl.BlockSpec((t, c), lambda i: (0, blk), pipeline_mode=pl.Buffered(1))
    return pl.pallas_call(
        body, name=name, grid=(nq,),
        in_specs=[pl.BlockSpec((tq, c), lambda i: (i, qb)), resident(qb + 1), resident(qb + 2)],
        out_specs=[pl.BlockSpec((tq, c), lambda i: (i, 0)), pl.BlockSpec((tq, LANES), lambda i: (i, 0))],
        out_shape=[jax.ShapeDtypeStruct((t, c), F32), jax.ShapeDtypeStruct((t, LANES), F32)],
        scratch_shapes=[pltpu.VMEM((GROUP_HEADS, tq, 1), F32), pltpu.VMEM((tq, c), F32)],
        compiler_params=_cparams(("arbitrary",)),
    )(proj, proj, proj)


def sb_bwd(proj, rsave, do, *, name, tq=512, tk=256):
    t = proj.shape[0]
    c = GROUP_WIDTH
    tq, tk, nq, nk = _attn_tiles(t, tq, tk)
    qb = P_SB // c
    scale = HEAD_DIM ** -0.5

    def body(q_ref, k_ref, v_ref, rs_ref, do_ref, dq_ref, dk_hbm, dv_hbm, e_scr, dq_scr, dk_scr, dv_scr):
        i = pl.program_id(0)
        last = ((i + 1) * tq - 1) // tk
        masks = _head_masks(c)

        @pl.when(i == 0)
        def _():
            dk_scr[...] = jnp.zeros_like(dk_scr)
            dv_scr[...] = jnp.zeros_like(dv_scr)

        e_scr[...] = jnp.zeros_like(e_scr)
        dq_scr[...] = jnp.zeros_like(dq_scr)
        q = q_ref[...]
        qf = q.astype(MXU_DT)
        qh = [jnp.where(masks[h], q, 0.0).astype(MXU_DT) for h in range(GROUP_HEADS)]
        do = do_ref[...]
        dob = do.astype(MXU_DT)
        doh = [jnp.where(masks[h], do, 0.0).astype(MXU_DT) for h in range(GROUP_HEADS)]
        later = _tri(tk, upper=False)
        earlier = _tri(tk, upper=True)
        rs = rs_ref[...]
        lane = lax.broadcasted_iota(jnp.int32, (1, LANES), 1)
        visited = jnp.where(jnp.max(rs, axis=0, keepdims=True) > SB_DEAD, (lane // GROUP_HEADS + 1).astype(F32), 0.0)
        n_visited = jnp.minimum(jnp.max(visited).astype(jnp.int32), last + 1)

        def step(it, carry):
            jj = n_visited - 1 - it
            j = last - jj
            rows = pl.ds(pl.multiple_of(j * tk, tk), tk)
            kb = k_ref[rows, :].astype(MXU_DT)
            vb = v_ref[rows, :].astype(MXU_DT)
            row = i * tq + lax.broadcasted_iota(jnp.int32, (tq, tk), 0)
            col = j * tk + lax.broadcasted_iota(jnp.int32, (tq, tk), 1)
            strict = col < row
            dq = dq_scr[...]
            dk_upd = jnp.zeros((tk, c), F32)
            dv_upd = jnp.zeros((tk, c), F32)
            for h in range(GROUP_HEADS):
                z = _dot_nt(qh[h], kb) * scale
                log_keep, log_beta = _sb_logs(z, strict)
                r_h = jnp.sum(jnp.where(lane == _sb_carry_lane(jj, h), rs, 0.0), axis=-1, keepdims=True)
                rest = r_h + _split_dot(log_keep, later)
                w = jnp.where(strict, jnp.exp(log_beta + rest), 0.0)
                e = w * _dot_nt(doh[h], vb)
                e_old = e_scr[h]
                dkeep = e_old + _split_dot(e, earlier)
                dz = jnp.where(strict, e * jnp.exp(log_keep) - dkeep * jnp.exp(log_beta), 0.0)
                dzb = dz.astype(MXU_DT)
                dq = jnp.where(masks[h], dq + _dot(dzb, kb) * scale, dq)
                dk_upd = jnp.where(masks[h], _dot_tn(dzb, qf) * scale, dk_upd)
                dv_upd = jnp.where(masks[h], _dot_tn(w.astype(MXU_DT), dob), dv_upd)
                e_scr[h] = e_old + jnp.sum(e, axis=-1, keepdims=True)
            dq_scr[...] = dq
            dk_scr[rows, :] += dk_upd
            dv_scr[rows, :] += dv_upd
            return carry

        lax.fori_loop(0, n_visited, step, 0)
        dq_ref[...] = dq_scr[...]

        @pl.when(i == nq - 1)
        def _():
            pltpu.sync_copy(dk_scr, dk_hbm)
            pltpu.sync_copy(dv_scr, dv_hbm)

    qrow = lambda i: (i, 0)
    resident = lambda blk: pl.BlockSpec((t, c), lambda i: (0, blk), pipeline_mode=pl.Buffered(1))
    hbm = pl.BlockSpec(memory_space=pl.ANY)
    return pl.pallas_call(
        body, name=name, grid=(nq,),
        in_specs=[pl.BlockSpec((tq, c), lambda i: (i, qb)), resident(qb + 1), resident(qb + 2),
                  pl.BlockSpec((tq, LANES), qrow), pl.BlockSpec((tq, c), qrow)],
        out_specs=[pl.BlockSpec((tq, c), qrow), hbm, hbm],
        out_shape=[jax.ShapeDtypeStruct((t, c), F32)] * 3,
        scratch_shapes=[pltpu.VMEM((GROUP_HEADS, tq, 1), F32), pltpu.VMEM((tq, c), F32),
                        pltpu.VMEM((t, c), F32), pltpu.VMEM((t, c), F32)],
        compiler_params=_cparams(("arbitrary",)),
    )(proj, proj, proj, rsave, do)


A_COL, B_COL = 0, 4
Z_BLK = P_Z // GROUP_WIDTH


NN = (((1,), (0,)), ((), ()))
NT = (((1,), (1,)), ((), ()))
TN = (((0,), (0,)), ((), ()))


def _terms(x, n):
    out, rem = [], x
    for _ in range(n):
        t = rem.astype(MXU_DT)
        out.append(t)
        rem = rem - t.astype(F32)
    return out


def _dotp(a, b, dims, a_terms=2, b_terms=2):
    at, bt = _terms(a, a_terms), _terms(b, b_terms)
    out = None
    for i, x in enumerate(at):
        for j, y in enumerate(bt):
            if i + j < max(a_terms, b_terms):
                r = lax.dot_general(x, y, dims, preferred_element_type=F32)
                out = r if out is None else out + r
    return out


def _silu(x):
    return x * _sigmoid(x)


def _dsilu(x):
    s = _sigmoid(x)
    return s * (1.0 + x * (1.0 - s))


def _head_sum(x, masks):
    out = jnp.zeros_like(x)
    for m in masks:
        out = jnp.where(m, jnp.sum(jnp.where(m, x, 0.0), axis=-1, keepdims=True), out)
    return out


def _expand(cols, col0, masks):
    out = jnp.zeros((cols.shape[0], GROUP_WIDTH), F32)
    for h, m in enumerate(masks):
        out = jnp.where(m, cols[:, col0 + h:col0 + h + 1], out)
    return out


def _reduce(x, col0, masks):
    lane = lax.broadcasted_iota(jnp.int32, (1, LANES), 1)
    out = jnp.zeros((x.shape[0], LANES), F32)
    for h, m in enumerate(masks):
        out = jnp.where(lane == col0 + h, jnp.sum(jnp.where(m, x, 0.0), axis=-1, keepdims=True), out)
    return out


def _block_ones():
    ri = lax.broadcasted_iota(jnp.int32, (GROUP_WIDTH, GROUP_WIDTH), 0) // HEAD_DIM
    ci = lax.broadcasted_iota(jnp.int32, (GROUP_WIDTH, GROUP_WIDTH), 1) // HEAD_DIM
    return jnp.where(ri == ci, 1.0, 0.0).astype(F32)


def _blk(x, hs):
    return jnp.concatenate([x] * GROUP_HEADS, axis=0) * hs


def _unblk(m, hs):
    mm = m * hs
    c = GDN_CHUNK
    return mm[0:c] + mm[c:2 * c] + mm[2 * c:3 * c] + mm[3 * c:4 * c]


def _row_mask4():
    ri = lax.broadcasted_iota(jnp.int32, (GROUP_WIDTH, LANES), 0) // HEAD_DIM
    ci = lax.broadcasted_iota(jnp.int32, (GROUP_WIDTH, LANES), 1)
    return jnp.where(ri + A_COL == ci, 1.0, 0.0).astype(F32)


def _gdn_chunk(xc, small, avec, dtvec, state, masks, hs):
    c = GDN_CHUNK
    w = GROUP_WIDTH
    b16 = lambda v: v.astype(MXU_DT)
    f = {}
    xq, xk, xv = xc[:, :w], xc[:, w:2 * w], xc[:, 2 * w:]
    qs, ks, v = _silu(xq), _silu(xk), _silu(xv)
    rq = lax.rsqrt(_head_sum(qs * qs, masks) + L2_EPS)
    rk = lax.rsqrt(_head_sum(ks * ks, masks) + L2_EPS)
    qn = qs * rq
    k = ks * rk
    q = qn * (HEAD_DIM ** -0.5)
    xg = small + dtvec
    sp = jnp.maximum(xg, 0.0) + jnp.log(1.0 + jnp.exp(-jnp.abs(xg)))
    g128 = -avec * sp
    beta128 = _sigmoid(small)
    ri = lax.broadcasted_iota(jnp.int32, (c, c), 0)
    ci = lax.broadcasted_iota(jnp.int32, (c, c), 1)
    tril = jnp.where(ri >= ci, 1.0, 0.0).astype(F32)
    gam128 = _dotp(tril, g128, NN, 1, 3)
    gam = _expand(gam128, A_COL, masks)
    bfull = _expand(beta128, B_COL, masks)
    mask4 = _row_mask4()
    ones = jnp.ones((c, LANES), F32)
    gam_row = _dotp(ones, jnp.concatenate([gam128] * GROUP_HEADS, axis=0) * mask4, NT, 1, 3)
    li = lax.broadcasted_iota(jnp.int32, (c, w), 0)
    lj = lax.broadcasted_iota(jnp.int32, (c, w), 1) % HEAD_DIM
    incl = li >= lj
    strict = li > lj
    dmat = jnp.exp(jnp.where(incl, gam - gam_row, NEG_BIG))
    egam = jnp.exp(gam)
    glast = gam[c - 1:c, :]
    ekd = jnp.exp(glast - gam)
    kb = k * bfull
    vb = v * bfull
    kbg = kb * egam
    qd = q * egam
    kd = k * ekd
    kblk = b16(_blk(k, hs))
    araw = _dot_nt(b16(kb), kblk)
    a = jnp.where(strict, araw * dmat, 0.0)
    tm = jnp.where(li == lj, 1.0, 0.0) - a
    p = a
    for _ in range(5):
        p = _dotp(p, _blk(p, hs), NN)
        tm = tm + _dotp(tm, _blk(p, hs), NN)
    tm16 = b16(tm)
    u = _dot(tm16, b16(_blk(vb, hs)))
    wm = _dot(tm16, b16(_blk(kbg, hs)))
    qk = _dot_nt(b16(q), kblk)
    aqk = jnp.where(incl, qk * dmat, 0.0)
    s16 = b16(state)
    vn = u - _dot(b16(wm), s16)
    o = _dot(b16(qd), s16) + _dot(b16(aqk), b16(_blk(vn, hs)))
    s_new = state * jnp.exp(glast) + hs * _dot_tn(b16(kd), b16(vn))
    f.update(xq=xq, xk=xk, xv=xv, v=v, rq=rq, rk=rk, qn=qn, k=k, q=q, xg=xg, g128=g128, beta128=beta128,
             tril=tril, gam=gam, bfull=bfull, mask4=mask4, ones=ones, incl=incl, strict=strict, li=li,
             dmat=dmat, egam=egam, glast=glast, ekd=ekd, kb=kb, vb=vb, kbg=kbg, qd=qd, kd=kd, kblk=kblk,
             araw=araw, tm=tm, tm16=tm16, wm=wm, qk=qk, aqk=aqk, s16=s16, vn=vn, o=o, s_new=s_new)
    return f


def _decay_rate(a_log):
    lane = lax.broadcasted_iota(jnp.int32, a_log.shape, 1)
    return jnp.where((lane >= A_COL) & (lane < A_COL + GROUP_HEADS), jnp.exp(a_log), 0.0)


def _gdn_post(o, z, ng, masks):
    r = lax.rsqrt(_head_sum(o * o, masks) * (1.0 / HEAD_DIM) + RMS_EPS)
    on = o * r
    return on, r, on * ng * _silu(z)


def gdn_fwd(cqkv, proj, avec, dtvec, ng, *, name, comm=None):
    t = cqkv.shape[0]
    c = GDN_CHUNK
    w = GROUP_WIDTH
    n = t // c

    def body(x_ref, z_ref, sm_ref, a_ref, dt_ref, ng_ref, y_ref, st_ref, s_scr):
        @pl.when(pl.program_id(0) == 0)
        def _():
            s_scr[...] = jnp.zeros_like(s_scr)

        masks = _head_masks(w)
        hs = _block_ones()
        state = s_scr[...]
        st_ref[0] = state
        f = _gdn_chunk(x_ref[...], sm_ref[...], _decay_rate(a_ref[...]), dt_ref[...], state, masks, hs)
        _, _, y = _gdn_post(f["o"], z_ref[...], ng_ref[...], masks)
        y_ref[...] = y
        s_scr[...] = f["s_new"]

    vec = pl.BlockSpec((1, LANES), lambda i: (0, 0))
    call = dict(
        name=name, grid=(n,),
        in_specs=[pl.BlockSpec((c, 3 * w), lambda i: (i, 0)),
                  pl.BlockSpec((c, w), lambda i: (i, Z_BLK)),
                  pl.BlockSpec((c, LANES), lambda i: (i, SMALL_BLK)),
                  vec, vec, pl.BlockSpec((1, w), lambda i: (0, 0))],
        out_specs=[pl.BlockSpec((c, w), lambda i: (i, 0)), pl.BlockSpec((1, w, w), lambda i: (i, 0, 0))],
        out_shape=[jax.ShapeDtypeStruct((t, w), F32), jax.ShapeDtypeStruct((n, w, w), F32)],
        scratch_shapes=[pltpu.VMEM((w, w), F32)],
        compiler_params=_cparams(("arbitrary",)),
    )
    outs, got = carry_comm(call, body, (cqkv, proj, proj, avec, dtvec, ng), comm, 2, *_grid_ends(n))
    return (*outs, got)


def gdn_bwd(cqkv, proj, avec, dtvec, ng, states, dy, *, name, comm=None):
    t = cqkv.shape[0]
    c = GDN_CHUNK
    w = GROUP_WIDTH
    n = t // c
    b16 = lambda v: v.astype(MXU_DT)

    def body(x_ref, z_ref, sm_ref, a_ref, dt_ref, ng_ref, st_ref, dy_ref,
             dx_ref, dz_ref, dsm_ref, dng_ref, dal_ref, ddt_ref, ds_scr):
        @pl.when(pl.program_id(0) == 0)
        def _():
            ds_scr[...] = jnp.zeros_like(ds_scr)
            dng_ref[...] = jnp.zeros_like(dng_ref)
            dal_ref[...] = jnp.zeros_like(dal_ref)
            ddt_ref[...] = jnp.zeros_like(ddt_ref)

        masks = _head_masks(w)
        hs = _block_ones()
        state = st_ref[0]
        avec_v = _decay_rate(a_ref[...])
        f = _gdn_chunk(x_ref[...], sm_ref[...], avec_v, dt_ref[...], state, masks, hs)
        z = z_ref[...]
        ng_v = ng_ref[...]
        dy_v = dy_ref[...]
        on, r, _ = _gdn_post(f["o"], z, ng_v, masks)
        sz = _silu(z)
        dz_ref[...] = dy_v * on * ng_v * _dsilu(z)
        d_on = dy_v * ng_v * sz
        dng_ref[...] += jnp.sum(dy_v * on * sz, axis=0, keepdims=True)
        do = r * (d_on - on * _head_sum(d_on * on, masks) * (1.0 / HEAD_DIM))
        do16 = b16(do)
        dsn = ds_scr[...]
        dsn16 = b16(dsn)
        s16, vn, kd, qd, wm = f["s16"], f["vn"], f["kd"], f["qd"], f["wm"]
        k, q, kblk, tm, tm16 = f["k"], f["q"], f["kblk"], f["tm"], f["tm16"]
        dmat, egam, glast, gam = f["dmat"], f["egam"], f["glast"], f["gam"]
        incl, strict, li = f["incl"], f["strict"], f["li"]
        vn16 = b16(vn)
        dvn = _unblk(_dot_tn(b16(f["aqk"]), do16), hs) + _dot(b16(kd), dsn16)
        daqk = jnp.where(incl, _dot_nt(do16, b16(_blk(vn, hs))), 0.0)
        dqd = _dot_nt(do16, s16)
        dvn16 = b16(dvn)
        ds_scr[...] = hs * (_dot_tn(b16(qd), do16) - _dot_tn(b16(wm), dvn16)) + dsn * jnp.exp(glast)
        dkd = _dot_nt(vn16, dsn16)
        dglast = jnp.sum(dsn * state, axis=0, keepdims=True) * jnp.exp(glast)
        du16 = dvn16
        dw16 = b16(-_dot_nt(dvn16, s16))
        dqk16 = b16(daqk * dmat)
        ddm = daqk * f["qk"]
        dq = _dot(dqk16, kblk)
        dk = _unblk(_dot_tn(dqk16, b16(q)), hs)
        dtm = _dot_nt(du16, b16(_blk(f["vb"], hs))) + _dot_nt(dw16, b16(_blk(f["kbg"], hs)))
        dvb = _unblk(_dot_tn(tm16, du16), hs)
        dkbg = _unblk(_dot_tn(tm16, dw16), hs)
        xx = _unblk(_dotp(tm, dtm, TN), hs)
        da = jnp.where(strict, -_dotp(xx, _blk(tm, hs), NT), 0.0)
        daraw16 = b16(da * dmat)
        ddm = ddm + da * f["araw"]
        dkb = _dot(daraw16, kblk)
        dk = dk + _unblk(_dot_tn(daraw16, b16(f["kb"])), hs)
        tcol = ddm * dmat
        dgam = tcol
        dgam128_row = _dotp(-tcol, f["ones"], TN, 2, 1) * f["mask4"]
        dgam128_row = (dgam128_row[0:c] + dgam128_row[c:2 * c] + dgam128_row[2 * c:3 * c] + dgam128_row[3 * c:4 * c])
        dk = dk + dkd * f["ekd"]
        tt = dkd * kd
        dgam = dgam - tt
        dglast = dglast + jnp.sum(tt, axis=0, keepdims=True)
        dq = dq + dqd * egam
        dgam = dgam + dqd * qd
        dkb = dkb + dkbg * egam
        dgam = dgam + dkbg * f["kbg"]
        dk = dk + dkb * f["bfull"]
        dbf = dkb * k + dvb * f["v"]
        dv = dvb * f["bfull"]
        dgam = dgam + jnp.where(li == c - 1, dglast, 0.0)
        beta128 = f["beta128"]
        db128 = _reduce(dbf, B_COL, masks) * beta128 * (1.0 - beta128)
        dgam128 = _reduce(dgam, A_COL, masks) + dgam128_row
        dg128 = _dotp(f["tril"], dgam128, TN, 1, 2)
        dxg = dg128 * (-avec_v * _sigmoid(f["xg"]))
        lane = lax.broadcasted_iota(jnp.int32, (1, LANES), 1)
        dsm_ref[...] = jnp.where(lane < B_COL, dxg, db128)
        ddt_ref[...] += jnp.sum(dxg, axis=0, keepdims=True)
        dal_ref[...] += jnp.sum(dg128 * f["g128"], axis=0, keepdims=True)
        dqn = dq * (HEAD_DIM ** -0.5)
        dqs = f["rq"] * (dqn - f["qn"] * _head_sum(dqn * f["qn"], masks))
        dks = f["rk"] * (dk - k * _head_sum(dk * k, masks))
        dx_ref[:, :w] = dqs * _dsilu(f["xq"])
        dx_ref[:, w:2 * w] = dks * _dsilu(f["xk"])
        dx_ref[:, 2 * w:] = dv * _dsilu(f["xv"])

    vec = pl.BlockSpec((1, LANES), lambda i: (0, 0))
    rev = lambda blk: (lambda i: (n - 1 - i, blk))
    call = dict(
        name=name, grid=(n,),
        in_specs=[pl.BlockSpec((c, 3 * w), rev(0)),
                  pl.BlockSpec((c, w), rev(Z_BLK)),
                  pl.BlockSpec((c, LANES), rev(SMALL_BLK)),
                  vec, vec, pl.BlockSpec((1, w), lambda i: (0, 0)),
                  pl.BlockSpec((1, w, w), lambda i: (n - 1 - i, 0, 0)),
                  pl.BlockSpec((c, w), rev(0))],
        out_specs=[pl.BlockSpec((c, 3 * w), rev(0)), pl.BlockSpec((c, w), rev(0)), pl.BlockSpec((c, LANES), rev(0)),
                   pl.BlockSpec((1, w), lambda i: (0, 0)), vec, vec],
        out_shape=[jax.ShapeDtypeStruct((t, 3 * w), F32), jax.ShapeDtypeStruct((t, w), F32),
                   jax.ShapeDtypeStruct((t, LANES), F32), jax.ShapeDtypeStruct((1, w), F32),
                   jax.ShapeDtypeStruct((1, LANES), F32), jax.ShapeDtypeStruct((1, LANES), F32)],
        scratch_shapes=[pltpu.VMEM((w, w), F32)],
        compiler_params=_cparams(("arbitrary",)),
    )
    outs, got = carry_comm(call, body, (cqkv, proj, proj, avec, dtvec, ng, states, dy), comm, 6, *_grid_ends(n))
    return (*outs, got)


def adamw(w, m, v, gslots, *, row0=0, name):
    r, c = w.shape
    s = gslots.shape[0]
    tr = _tile(r, 64, 8)
    assert row0 % tr == 0 and gslots.shape[2] == c
    rb = row0 // tr
    c1 = 1.0 - ADAM_B1 ** ADAM_STEP
    c2 = 1.0 - ADAM_B2 ** ADAM_STEP

    def body(w_ref, m_ref, v_ref, gs_ref, g_ref, d_ref, mo_ref, vo_ref):
        g = gs_ref[0]
        for k in range(1, s):
            g = g + gs_ref[k]
        m_new = ADAM_B1 * m_ref[...] + (1.0 - ADAM_B1) * g
        v_new = ADAM_B2 * v_ref[...] + (1.0 - ADAM_B2) * (g * g)
        m_hat = m_new / c1
        v_hat = v_new / c2
        g_ref[...] = g
        mo_ref[...] = m_new
        vo_ref[...] = v_new
        d_ref[...] = -ADAM_LR * (m_hat / (jnp.sqrt(v_hat) + ADAM_EPS) + ADAM_WD * w_ref[...])

    row = pl.BlockSpec((tr, c), lambda i: (i, 0))
    return pl.pallas_call(
        body, name=name, grid=(r // tr,),
        in_specs=[row, row, row, pl.BlockSpec((s, tr, c), lambda i: (0, rb + i, 0))],
        out_specs=[row] * 4,
        out_shape=[jax.ShapeDtypeStruct((r, c), F32)] * 4,
        compiler_params=_cparams(("parallel",)),
    )(w, m, v, gslots)


def slot_sum(slots, *, name):
    s, r, c = slots.shape

    def body(s_ref, o_ref):
        acc = s_ref[0]
        for k in range(1, s):
            acc = acc + s_ref[k]
        o_ref[...] = acc

    return pl.pallas_call(
        body, name=name, grid=(1,),
        in_specs=[pl.BlockSpec((s, r, c), lambda i: (0, 0, 0))],
        out_specs=pl.BlockSpec((r, c), lambda i: (0, 0)),
        out_shape=jax.ShapeDtypeStruct((r, c), F32),
        compiler_params=_cparams(("arbitrary",)),
    )(slots)


class Comm:
    def __init__(self, srcs, broadcast):
        self.srcs = list(srcs)
        self.broadcast = [broadcast] * len(self.srcs) if isinstance(broadcast, bool) else list(broadcast)
        self.n = len(self.srcs)
        self.out_shapes = [jax.ShapeDtypeStruct(((N_DEV,) + s.shape) if b else s.shape, s.dtype)
                           for s, b in zip(self.srcs, self.broadcast)]
        self.sems = [pltpu.SemaphoreType.DMA((self.n,))] * 3

    def _local(self, src_refs, out_refs, loc_sem, a, me):
        src = src_refs[a] if self.broadcast[a] else src_refs[a].at[me]
        return pltpu.make_async_copy(src, out_refs[a].at[me], loc_sem.at[a])

    def start(self, src_refs, out_refs, send_sem, recv_sem, loc_sem):
        x, y, c = lax.axis_index("x"), lax.axis_index("y"), lax.axis_index("c")
        me = 4 * x + 2 * y + c
        for a in range(self.n):
            self._local(src_refs, out_refs, loc_sem, a, me).start()
        for d in range(1, N_DEV):
            px, py, pc = x ^ ((d >> 2) & 1), y ^ ((d >> 1) & 1), c ^ (d & 1)
            peer = 4 * px + 2 * py + pc
            for a in range(self.n):
                src = src_refs[a] if self.broadcast[a] else src_refs[a].at[peer]
                pltpu.make_async_remote_copy(
                    src_ref=src, dst_ref=out_refs[a].at[me],
                    send_sem=send_sem.at[a], recv_sem=recv_sem.at[a],
                    device_id=(px, py, pc), device_id_type=pl.DeviceIdType.MESH).start()

    def wait(self, src_refs, out_refs, send_sem, recv_sem, loc_sem):
        x, y, c = lax.axis_index("x"), lax.axis_index("y"), lax.axis_index("c")
        me = 4 * x + 2 * y + c
        for a in range(self.n):
            seven = out_refs[a].at[pl.ds(0, N_DEV - 1)]
            pltpu.make_async_remote_copy(
                src_ref=seven, dst_ref=seven, send_sem=send_sem.at[a], recv_sem=recv_sem.at[a],
                device_id=(x, y, c), device_id_type=pl.DeviceIdType.MESH).wait()
            self._local(src_refs, out_refs, loc_sem, a, me).wait()


def exchange(srcs, *, broadcast, name):
    comm = Comm(srcs, broadcast)
    n = comm.n

    def body(*refs):
        src_refs, out_refs, sems = refs[:n], refs[n:2 * n], refs[2 * n:]
        comm.start(src_refs, out_refs, *sems)
        comm.wait(src_refs, out_refs, *sems)

    anyspec = pl.BlockSpec(memory_space=pl.ANY)
    return pl.pallas_call(
        body, name=name,
        in_specs=[anyspec] * n, out_specs=[anyspec] * n, out_shape=comm.out_shapes,
        scratch_shapes=comm.sems,
        compiler_params=pltpu.CompilerParams(has_side_effects=True),
    )(*srcs)


def carry_comm(call_kwargs, body, args, comm, n_out, is_first, is_last):
    if comm is None:
        return pl.pallas_call(body, **call_kwargs)(*args), []
    n_in, nc = len(args), comm.n
    n_scr = len(call_kwargs["scratch_shapes"])
    anyspec = pl.BlockSpec(memory_space=pl.ANY)

    def wrapped(*refs):
        ins, csrc = refs[:n_in], refs[n_in:n_in + nc]
        outs = refs[n_in + nc:n_in + nc + n_out]
        cout = refs[n_in + nc + n_out:n_in + 2 * nc + n_out]
        rest = refs[n_in + 2 * nc + n_out:]
        scr, sems = rest[:n_scr], rest[n_scr:]

        @pl.when(is_first())
        def _():
            comm.start(csrc, cout, *sems)

        body(*ins, *outs, *scr)

        @pl.when(is_last())
        def _():
            comm.wait(csrc, cout, *sems)

    kw = dict(call_kwargs)
    kw["in_specs"] = list(kw["in_specs"]) + [anyspec] * nc
    kw["out_specs"] = list(kw["out_specs"]) + [anyspec] * nc
    kw["out_shape"] = list(kw["out_shape"]) + comm.out_shapes
    kw["scratch_shapes"] = list(kw["scratch_shapes"]) + comm.sems
    cp = kw["compiler_params"]
    kw["compiler_params"] = pltpu.CompilerParams(dimension_semantics=cp.dimension_semantics,
                                                 vmem_limit_bytes=cp.vmem_limit_bytes, has_side_effects=True)
    res = pl.pallas_call(wrapped, **kw)(*args, *comm.srcs)
    return res[:n_out], res[n_out:]


def _pack(arrs):
    flat = []
    for a in arrs:
        f = a.reshape(-1).astype(F32)
        flat.append(jnp.pad(f, (0, (-f.shape[0]) % LANES)))
    buf = jnp.concatenate(flat)
    buf = jnp.pad(buf, (0, (-buf.shape[0]) % (8 * LANES)))
    return buf.reshape(-1, LANES)


def _unpack(buf, shapes):
    flat = buf.reshape(-1)
    out, off = [], 0
    for s in shapes:
        sz = int(np.prod(s))
        out.append(flat[off:off + sz].reshape(s))
        off += sz + (-sz) % LANES
    return out


def _win_to_aligned(w):
    o = np.cumsum((0,) + IN_SPLITS)
    seg = lambda i: w[..., o[i]:o[i + 1]]
    pad = jnp.zeros(w.shape[:-1] + (P_WIDTH - IN_WIDTH,), w.dtype)
    return jnp.concatenate([seg(0), seg(1), seg(4), seg(6), seg(7), seg(2), seg(3), seg(5), pad], axis=-1)


def _win_from_aligned(w):
    o = np.cumsum((0,) + IN_SPLITS)
    s = P_SMALL
    return jnp.concatenate([w[..., P_GDN:P_GDN + 768], w[..., P_Z:P_Z + 256], w[..., s:s + 4], w[..., s + 4:s + 8],
                            w[..., P_FOX:P_FOX + 768], w[..., s + 8:s + 12], w[..., P_CONF:P_CONF + 512],
                            w[..., P_SB:P_SB + 768]], axis=-1)


def _row128(vals, col0):
    return jnp.pad(vals.astype(F32)[None, :], ((0, 0), (col0, LANES - col0 - GROUP_HEADS)))


def _ffn_fwd(x, w, n, tag, comm=None, on_comm=None):
    gu = mm(x, w[f"gu{n}"], name=f"{tag}_gu", tm=1024, tn=512, tk=1024, comm=comm)
    if comm is not None:
        gu, got = gu
        on_comm(got)
    h = act_fwd(gu, name=f"{tag}_act")
    y = mm(h, w[f"d{n}"], name=f"{tag}_down", tm=512, tn=512, tk=D_FF)
    out, xh, rs = ln_res_fwd(x, y, w[f"ln_ffn{n}_g"], w[f"ln_ffn{n}_b"], 0.5, name=f"{tag}_ln")
    return out, (x, gu, h, xh, rs)


def _ffn_bwd(dout, saved, w, n, tag, comm_dh=None, comm_dwgu=None, comm_dx=None):
    x, gu, h, xh, rs = saved
    wgu, wd = w[f"gu{n}"], w[f"d{n}"]
    got = [[], [], []]
    dz, dg, db = ln_res_bwd(dout, xh, rs, w[f"ln_ffn{n}_g"], name=f"{tag}_ln_bwd")
    dh = mm(dz, wd, mode="nt", alpha=0.5, name=f"{tag}_dh", tm=512, tn=D_FF // 2, tk=1024, comm=comm_dh)
    if comm_dh is not None:
        dh, got[0] = dh
    dgu = act_bwd(gu, dh, name=f"{tag}_act_bwd")
    dwd = mm(h, dz, mode="tn", alpha=0.5, name=f"{tag}_dwd", tm=D_FF // 2, tn=1024, tk=512)
    c = comm_dwgu(dwd) if comm_dwgu is not None else None
    dwgu = mm(x, dgu, mode="tn", name=f"{tag}_dwgu", tm=1024, tn=D_FF // 2, tk=512, comm=c)
    if c is not None:
        dwgu, got[1] = dwgu
    c = comm_dx(dwgu) if comm_dx is not None else None
    dx = mm(dgu, wgu, mode="nt", add=dz, beta=DN_ALPHA, name=f"{tag}_dx", tm=512, tn=1024, tk=D_FF // 2, comm=c)
    if c is not None:
        dx, got[2] = dx
    return dx, dwgu, dwd, dg, db, got


def _layer_fwd(x, mem, w, tag, comm_ffn1=None, on_ffn1=None, comm_gdn=None, on_gdn=None, comm_fox=None, on_fox=None):
    sv = {}
    x1, sv["ffn1"] = _ffn_fwd(x, w, 1, f"{tag}_ffn1", comm=comm_ffn1, on_comm=on_ffn1)
    proj = mm(x1, w["win"], name=f"{tag}_inproj", tm=1024, tn=640, tk=1024)
    cqkv = dwconv_fwd(proj, w["gdn_conv_w"], None, col0=P_GDN, name=f"{tag}_gdn_conv")
    ya, states, got = gdn_fwd(cqkv, proj, w["alog"], w["dtb"], w["ng"], name=f"{tag}_gdn", comm=comm_gdn)
    if on_gdn is not None:
        on_gdn(got)
    cum = fox_gate_fwd(proj, w["bf"], name=f"{tag}_fox_gate")
    cum_t = jnp.pad(cum[:, FOX_COL:FOX_COL + GROUP_HEADS].T, ((0, 8 - GROUP_HEADS), (0, 0)))
    yb, lse, got = fox_fwd(proj, cum, cum_t, name=f"{tag}_fox", comm=comm_fox)
    if on_fox is not None:
        on_fox(got)
    u = glu_fwd(proj, name=f"{tag}_glu")
    cc = dwconv_fwd(u, w["conf_dw_w"], w["conf_dw_b"], name=f"{tag}_conf_conv")
    yc = gn_silu_fwd(cc, w["conf_norm_g"], w["conf_norm_b"], name=f"{tag}_conf_norm")
    yd, rsave = sb_fwd(proj, name=f"{tag}_sb")
    ycat = jnp.concatenate([ya, yb, yc, yd], axis=1)
    mix = mm(ycat, w["wout"], name=f"{tag}_outproj")
    x2, xh2, rs2 = ln_res_fwd(x1, mix, w["ln_mix_g"], w["ln_mix_b"], 1.0, name=f"{tag}_ln_mix")
    sv["mix"] = (x1, proj, cqkv, states, cum, cum_t, yb, lse, u, cc, rsave, ycat, xh2, rs2)
    q = mm(x2, w["wq"], name=f"{tag}_memq")
    kv = mm(mem, w["wkv"], name=f"{tag}_memkv", tm=N_MEM)
    att = memattn_fwd(q, kv, name=f"{tag}_memattn")
    mo = mm(att, w["wo"], name=f"{tag}_memo")
    x3, xh3, rs3 = ln_res_fwd(x2, mo, w["ln_mem_g"], w["ln_mem_b"], 1.0, name=f"{tag}_ln_mem")
    sv["mem"] = (x2, q, kv, att, xh3, rs3)
    x4, sv["ffn2"] = _ffn_fwd(x3, w, 2, f"{tag}_ffn2")
    return x4, sv


def _layer_bwd(dx4, mem, sv, w, tag, plan, tail=None):
    t = dx4.shape[0]
    gr = {}
    dx3, gr["gu2"], gr["d2"], gr["ln_ffn2_g"], gr["ln_ffn2_b"], _ = _ffn_bwd(dx4, sv["ffn2"], w, 2, f"{tag}_ffn2")
    x2, q, kv, att, xh3, rs3 = sv["mem"]
    dz, gr["ln_mem_g"], gr["ln_mem_b"] = ln_res_bwd(dx3, xh3, rs3, w["ln_mem_g"], name=f"{tag}_ln_mem_bwd")
    datt = mm(dz, w["wo"], mode="nt", name=f"{tag}_datt")
    gr["wo"] = mm(att, dz, mode="tn", name=f"{tag}_dwo", tk=512)
    dq, dkv = memattn_bwd(q, kv, datt, name=f"{tag}_memattn_bwd")
    gr["wq"] = mm(x2, dq, mode="tn", name=f"{tag}_dwq", tk=512)
    gr["wkv"] = mm(mem, dkv, mode="tn", name=f"{tag}_dwkv", tk=N_MEM)
    dx2 = mm(dq, w["wq"], mode="nt", add=dz, beta=DN_ALPHA, name=f"{tag}_dx2")
    x1, proj, cqkv, states, cum, cum_t, yb, lse, u, cc, rsave, ycat, xh2, rs2 = sv["mix"]
    dz, gr["ln_mix_g"], gr["ln_mix_b"] = ln_res_bwd(dx2, xh2, rs2, w["ln_mix_g"], name=f"{tag}_ln_mix_bwd")
    dycat = mm(dz, w["wout"], mode="nt", name=f"{tag}_dycat")
    gr["wout"] = mm(ycat, dz, mode="tn", name=f"{tag}_dwout", tk=512)
    comm_fox, comm_gdn = plan(gr)
    gw = GROUP_WIDTH
    dya, dyb, dyc, dyd = (dycat[:, i * gw:(i + 1) * gw] for i in range(4))
    dq_d, dk_d, dv_d = sb_bwd(proj, rsave, dyd, name=f"{tag}_sb_bwd")
    dcc, gr["conf_norm_g"], gr["conf_norm_b"] = gn_silu_bwd(cc, w["conf_norm_g"], w["conf_norm_b"], dyc,
                                                            name=f"{tag}_conf_norm_bwd")
    du, gr["conf_dw_w"], gr["conf_dw_b"] = dwconv_bwd(dcc, u, w["conf_dw_w"], name=f"{tag}_conf_conv_bwd")
    dglu = glu_bwd(proj, du, name=f"{tag}_glu_bwd")
    dq_b, dk_b, dv_b, dcc, dcr, got_fox = fox_bwd(proj, cum, cum_t, yb, lse, dyb, name=f"{tag}_fox_bwd", comm=comm_fox)
    dcum = dcc + jnp.pad(dcr[:, :GROUP_HEADS, :].transpose(0, 2, 1).reshape(t, GROUP_HEADS),
                   ((0, 0), (FOX_COL, LANES - FOX_COL - GROUP_HEADS)))
    dsm_f, dbf = fox_gate_bwd(dcum, proj, w["bf"], name=f"{tag}_fox_gate_bwd")
    gr["fox_b_f"] = dbf[0, FOX_COL:FOX_COL + GROUP_HEADS]
    dcq, dz_a, dsm_a, dng, dal, ddt, got_gdn = gdn_bwd(cqkv, proj, w["alog"], w["dtb"], w["ng"], states, dya,
                                                       name=f"{tag}_gdn_bwd", comm=comm_gdn)
    gr["gdn_norm_g"] = dng.reshape(GROUP_HEADS, HEAD_DIM).sum(0)
    gr["gdn_a_log"] = dal[0, A_COL:A_COL + GROUP_HEADS]
    gr["gdn_dt_bias"] = ddt[0, A_COL:A_COL + GROUP_HEADS]
    dgq, gr["gdn_conv_w"], _ = dwconv_bwd(dcq, proj, w["gdn_conv_w"], col0=P_GDN, name=f"{tag}_gdn_conv_bwd")
    dproj = jnp.concatenate([dgq, dz_a, dq_b, dk_b, dv_b, dglu, dq_d, dk_d, dv_d, dsm_a + dsm_f], axis=1)
    gr["win"] = mm(x1, dproj, mode="tn", name=f"{tag}_dwin", tm=1024, tn=640, tk=512)
    dx1 = mm(dproj, w["win"], mode="nt", add=dz, beta=DN_ALPHA, name=f"{tag}_dx1", tm=512, tn=1024, tk=640)
    tail = {} if tail is None else dict(tail, comm_dh=tail["comm_dh"](gr))
    dx0, gr["gu1"], gr["d1"], gr["ln_ffn1_g"], gr["ln_ffn1_b"], got_tail = _ffn_bwd(
        dx1, sv["ffn1"], w, 1, f"{tag}_ffn1", **tail)
    return dx0, gr, got_fox, got_gdn, got_tail


SMALL_REPLICATED = ("ln_ffn1_g", "ln_ffn1_b", "gdn_a_log", "gdn_dt_bias", "gdn_norm_g", "fox_b_f", "conf_dw_b",
                    "conf_norm_g", "conf_norm_b", "ln_mix_g", "ln_mix_b", "ln_mem_g", "ln_mem_b", "ln_ffn2_g",
                    "ln_ffn2_b")
SMALL_SHARDED = ("gdn_conv_w", "conf_dw_w")
BIG = ("ffn1_w_gate", "ffn1_w_up", "ffn1_w_down", "w_in", "w_out", "mem_w_q", "mem_w_kv", "mem_w_o",
       "ffn2_w_gate", "ffn2_w_up", "ffn2_w_down")
WEIGHT_ORDER = ("ffn1_w_gate", "ffn1_w_up", "ffn1_w_down", "ln_ffn1_g", "ln_ffn1_b", "w_in", "gdn_conv_w", "gdn_a_log",
                "gdn_dt_bias", "gdn_norm_g", "fox_b_f", "conf_dw_w", "conf_dw_b", "conf_norm_g", "conf_norm_b", "w_out",
                "ln_mix_g", "ln_mix_b", "mem_w_q", "mem_w_kv", "mem_w_o", "ln_mem_g", "ln_mem_b", "ffn2_w_gate",
                "ffn2_w_up", "ffn2_w_down", "ln_ffn2_g", "ln_ffn2_b")


def _step(x, mem, loss_target, wts, ms, vs):
    me = 4 * lax.axis_index("x") + 2 * lax.axis_index("y") + lax.axis_index("c")
    x = x[0]
    mem = mem[0]
    target = loss_target[0]
    rows_s = D_MODEL // N_DEV
    first, rest = ("gu1", "d1", "win"), ("sq", "kv", "gu2", "d2")

    def shards(l):
        c = lambda k: wts[k][l].astype(MXU_DT)
        return dict(gu1=jnp.stack([c("ffn1_w_gate"), c("ffn1_w_up")]), d1=c("ffn1_w_down"),
                    win=_win_to_aligned(wts["w_in"][l]).astype(MXU_DT),
                    sq=jnp.stack([c("w_out"), c("mem_w_q"), c("mem_w_o")]), kv=c("mem_w_kv"),
                    gu2=jnp.stack([c("ffn2_w_gate"), c("ffn2_w_up")]), d2=c("ffn2_w_down"))

    def to_compute_layout(w, keys, got):
        for k, g in zip(keys, got):
            if k in ("gu1", "gu2"):
                w[k] = g.transpose(2, 1, 0, 3).reshape(D_MODEL, 2 * D_FF)
            elif k in ("d1", "d2"):
                w[k] = g.reshape(D_FF, D_MODEL)
            elif k == "win":
                w[k] = g.reshape(D_MODEL, P_WIDTH)
            elif k == "sq":
                full = g.transpose(1, 0, 2, 3).reshape(3, D_MODEL, D_MODEL)
                w["wout"], w["wq"], w["wo"] = full[0], full[1], full[2]
            else:
                w["wkv"] = g.transpose(1, 0, 2).reshape(D_MODEL, 2 * D_MODEL)

    def chunks(gr, keys):
        out = []
        for k in keys:
            if k in ("gu1", "gu2"):
                out.append(gr[k].reshape(D_MODEL, 2, N_DEV, -1).transpose(2, 1, 0, 3))
            elif k in ("d1", "d2"):
                out.append(gr[k].reshape(N_DEV, -1, D_MODEL))
            elif k == "win":
                out.append(gr[k].reshape(N_DEV, rows_s, P_WIDTH))
            elif k == "sq":
                out.append(jnp.stack([gr[n].reshape(N_DEV, rows_s, D_MODEL) for n in ("wout", "wq", "wo")], axis=1))
            else:
                out.append(gr["wkv"].reshape(D_MODEL, N_DEV, -1).transpose(1, 0, 2))
        return out

    sh = [shards(l) for l in range(DEPTH)]
    sm_sh = _pack([wts["gdn_conv_w"], wts["conf_dw_w"]])
    got = exchange([sh[0]["gu1"], sm_sh], broadcast=True, name="gather_first")
    conv_shapes = [wts["gdn_conv_w"].shape, wts["conf_dw_w"].shape]
    parts = [_unpack(got[-1][j], conv_shapes) for j in range(N_DEV)]
    gconv_full = jnp.concatenate([p[0] for p in parts], axis=-1)
    cconv_full = jnp.concatenate([p[1] for p in parts], axis=-1)

    def small_weights(l):
        w = dict(gdn_conv_w=gconv_full[l], conf_dw_w=cconv_full[l],
                 alog=_row128(wts["gdn_a_log"][l], A_COL), dtb=_row128(wts["gdn_dt_bias"][l], A_COL),
                 bf=_row128(wts["fox_b_f"][l], FOX_COL), ng=jnp.tile(wts["gdn_norm_g"][l], GROUP_HEADS)[None, :])
        for k in ("ln_ffn1_g", "ln_ffn1_b", "conf_dw_b", "conf_norm_g", "conf_norm_b", "ln_mix_g", "ln_mix_b",
                  "ln_mem_g", "ln_mem_b", "ln_ffn2_g", "ln_ffn2_b"):
            w[k] = wts[k][l][None, :]
        return w

    lw = [small_weights(l) for l in range(DEPTH)]
    to_compute_layout(lw[0], ("gu1",), got[:-1])

    h, sv0 = _layer_fwd(x, mem, lw[0], "l0",
                        comm_ffn1=Comm([sh[0][k] for k in first[1:]], True),
                        on_ffn1=lambda g: to_compute_layout(lw[0], first[1:], g),
                        comm_gdn=Comm([sh[0][k] for k in rest], True),
                        on_gdn=lambda g: to_compute_layout(lw[0], rest, g),
                        comm_fox=Comm([sh[1][k] for k in first + rest], True),
                        on_fox=lambda g: to_compute_layout(lw[1], first + rest, g))
    h, sv1 = _layer_fwd(h, mem, lw[1], "l1")
    dh, lpart = loss_head(h, target, name="loss_head")

    recv = [{}, {}]
    e_ffn, e_mem = ("gu2", "d2"), ("sq", "kv")
    dh, g1, got_a, got_b, _ = _layer_bwd(dh, mem, sv1, lw[1], "l1",
                                      lambda gr: (Comm(chunks(gr, e_ffn), False), Comm(chunks(gr, e_mem), False)))
    recv[1].update(zip(e_ffn, got_a))
    recv[1].update(zip(e_mem, got_b))
    tail = dict(comm_dh=lambda gr: Comm(chunks(gr, ("win",)), False),
                comm_dwgu=lambda dwd: Comm(chunks({"d1": dwd}, ("d1",)), False),
                comm_dx=lambda dwgu: Comm(chunks({"gu1": dwgu}, ("gu1",)), False))
    dh, g0, got_a, got_b, got_t = _layer_bwd(
        dh, mem, sv0, lw[0], "l0", lambda gr: (Comm(chunks(g1, first), False), Comm(chunks(gr, rest), False)), tail)
    recv[1].update(zip(first, got_a))
    recv[0].update(zip(rest, got_b))
    recv[0].update(win=got_t[0][0], d1=got_t[1][0], gu1=got_t[2][0])
    grad_x = dh[None]
    grads = [g0, g1]

    def gl(k):
        return jnp.stack([grads[l][k] for l in range(DEPTH)])

    small_names = SMALL_REPLICATED + SMALL_SHARDED
    small_grads = [gl(k) for k in small_names] + [lpart[0, :1]]
    got = exchange([_pack(small_grads)], broadcast=True, name="gather_small_grads")
    sm_sum = slot_sum(got[0], name="sum_small_grads")
    sm_g = _unpack(sm_sum, [g.shape for g in small_grads])
    loss = sm_g[-1][0]
    small_g = dict(zip(small_names, sm_g[:-1]))
    for k in SMALL_SHARDED:
        width = wts[k].shape[-1]
        small_g[k] = lax.dynamic_slice_in_dim(small_g[k], me * width, width, axis=2)

    out_g, out_d, out_m, out_v = {}, {}, {}, {}

    def update(names, key, fix=lambda a: a):
        res = {k: [] for k in names}
        for l in range(DEPTH):
            slots = fix(recv[l][key])
            slots = slots.reshape(N_DEV, -1, slots.shape[-1])
            for i, k in enumerate(names):
                two = lambda a: a[l].reshape(-1, a.shape[-1])
                res[k].append(adamw(two(wts[k]), two(ms[k]), two(vs[k]), slots, row0=i * two(wts[k]).shape[0],
                                    name=f"adamw_{k}_l{l}"))
        for k in names:
            for dst, per_layer in zip((out_g, out_d, out_m, out_v), zip(*res[k])):
                dst[k] = jnp.stack(per_layer).reshape(wts[k].shape)

    update(("ffn1_w_gate", "ffn1_w_up"), "gu1")
    update(("ffn1_w_down",), "d1")
    update(("w_in",), "win", _win_from_aligned)
    update(("w_out", "mem_w_q", "mem_w_o"), "sq")
    update(("mem_w_kv",), "kv")
    update(("ffn2_w_gate", "ffn2_w_up"), "gu2")
    update(("ffn2_w_down",), "d2")

    sw = _pack([wts[k] for k in small_names])
    smm = _pack([ms[k] for k in small_names])
    smv = _pack([vs[k] for k in small_names])
    sg = _pack([small_g[k] for k in small_names])
    res = adamw(sw, smm, smv, sg[None], name="adamw_small")
    shapes = [wts[k].shape for k in small_names]
    for dst, buf in zip((out_g, out_d, out_m, out_v), res):
        for k, a in zip(small_names, _unpack(buf, shapes)):
            dst[k] = a

    return (loss, grad_x, *[out_g[k] for k in WEIGHT_ORDER], *[out_d[k] for k in WEIGHT_ORDER],
            *[out_m[k] for k in WEIGHT_ORDER], *[out_v[k] for k in WEIGHT_ORDER])


def kernel(x, mem, ffn1_w_gate, ffn1_w_up, ffn1_w_down, ln_ffn1_g, ln_ffn1_b, w_in, gdn_conv_w, gdn_a_log, gdn_dt_bias, gdn_norm_g, fox_b_f, conf_dw_w, conf_dw_b, conf_norm_g, conf_norm_b, w_out, ln_mix_g, ln_mix_b, mem_w_q, mem_w_kv, mem_w_o, ln_mem_g, ln_mem_b, ffn2_w_gate, ffn2_w_up, ffn2_w_down, ln_ffn2_g, ln_ffn2_b, loss_target, m_ffn1_w_gate, m_ffn1_w_up, m_ffn1_w_down, m_ln_ffn1_g, m_ln_ffn1_b, m_w_in, m_gdn_conv_w, m_gdn_a_log, m_gdn_dt_bias, m_gdn_norm_g, m_fox_b_f, m_conf_dw_w, m_conf_dw_b, m_conf_norm_g, m_conf_norm_b, m_w_out, m_ln_mix_g, m_ln_mix_b, m_mem_w_q, m_mem_w_kv, m_mem_w_o, m_ln_mem_g, m_ln_mem_b, m_ffn2_w_gate, m_ffn2_w_up, m_ffn2_w_down, m_ln_ffn2_g, m_ln_ffn2_b, v_ffn1_w_gate, v_ffn1_w_up, v_ffn1_w_down, v_ln_ffn1_g, v_ln_ffn1_b, v_w_in, v_gdn_conv_w, v_gdn_a_log, v_gdn_dt_bias, v_gdn_norm_g, v_fox_b_f, v_conf_dw_w, v_conf_dw_b, v_conf_norm_g, v_conf_norm_b, v_w_out, v_ln_mix_g, v_ln_mix_b, v_mem_w_q, v_mem_w_kv, v_mem_w_o, v_ln_mem_g, v_ln_mem_b, v_ffn2_w_gate, v_ffn2_w_up, v_ffn2_w_down, v_ln_ffn2_g, v_ln_ffn2_b):
    args = locals()
    wts = {k: args[k] for k in WEIGHT_ORDER}
    ms = {k: args["m_" + k] for k in WEIGHT_ORDER}
    vs = {k: args["v_" + k] for k in WEIGHT_ORDER}
    return _step(x, mem, loss_target, wts, ms, vs)
```

```python
import functools
import math

import jax
import jax.numpy as jnp
import numpy as np
from jax import lax
from jax.experimental import pallas as pl
from jax.experimental.pallas import tpu as pltpu

F32 = jnp.float32
BF16 = jnp.bfloat16
MXU_DT = jnp.bfloat16
HI = lax.Precision.HIGHEST

N_DEV = 8
VMEM_LIMIT_BYTES = 56 * 1024 * 1024
LANES = 128

D_MODEL = 1024
DEPTH = 2
GROUP_WIDTH = 256
HEAD_DIM = 64
GROUP_HEADS = 4
D_FF = 2816
SHORT_CONV = 4
CONF_KERNEL = 31
CONF_GROUPS = 4
GDN_CHUNK = 64
N_MEM = 256
MEM_HEADS = 4
MEM_HEAD_DIM = 256
DN_ALPHA = float((2 * DEPTH) ** 0.25)
LN_EPS = 1e-5
RMS_EPS = 1e-6
L2_EPS = 1e-6
NEG_BIG = -1e30
IN_SPLITS = (768, 256, 4, 4, 768, 4, 512, 768)
IN_WIDTH = sum(IN_SPLITS)
P_GDN, P_Z, P_FOX, P_CONF, P_SB, P_SMALL = 0, 768, 1024, 1792, 2304, 3072
P_WIDTH = 3200

ADAM_LR = 0.001
ADAM_B1 = 0.9
ADAM_B2 = 0.999
ADAM_EPS = 1e-08
ADAM_WD = 0.01
ADAM_STEP = 10


def _cparams(sem):
    return pltpu.CompilerParams(dimension_semantics=sem, vmem_limit_bytes=VMEM_LIMIT_BYTES)


def _tile(n, pref, align=LANES):
    if n <= pref:
        return n
    t = (pref // align) * align
    while t >= align:
        if n % t == 0:
            return t
        t -= align
    return n


def mm(a, b, *, mode="nn", add=None, alpha=1.0, beta=1.0, out_dtype=F32, name,
       tm=1024, tn=512, tk=1024, comm=None):
    if mode == "nn":
        (m, k), (k2, n) = a.shape, b.shape
    elif mode == "nt":
        (m, k), (n, k2) = a.shape, b.shape
    else:
        (k, m), (k2, n) = a.shape, b.shape
    assert k == k2, (a.shape, b.shape, mode)
    tm = _tile(m, tm, 8 if mode != "tn" else LANES)
    tn = _tile(n, tn)
    tk = _tile(k, tk, LANES if mode != "tn" else 8)
    nk = k // tk
    if mode == "nn":
        a_spec = pl.BlockSpec((tm, tk), lambda i, j, kk: (i, kk))
        b_spec = pl.BlockSpec((tk, tn), lambda i, j, kk: (kk, j))
        dims = (((1,), (0,)), ((), ()))
    elif mode == "nt":
        a_spec = pl.BlockSpec((tm, tk), lambda i, j, kk: (i, kk))
        b_spec = pl.BlockSpec((tn, tk), lambda i, j, kk: (j, kk))
        dims = (((1,), (1,)), ((), ()))
    else:
        a_spec = pl.BlockSpec((tk, tm), lambda i, j, kk: (kk, i))
        b_spec = pl.BlockSpec((tk, tn), lambda i, j, kk: (kk, j))
        dims = (((0,), (0,)), ((), ()))
    o_spec = pl.BlockSpec((tm, tn), lambda i, j, kk: (i, j))
    has_add = add is not None

    def body(*refs):
        if has_add:
            a_ref, b_ref, add_ref, o_ref, acc_ref = refs
        else:
            a_ref, b_ref, o_ref, acc_ref = refs
        kk = pl.program_id(2)

        @pl.when(kk == 0)
        def _():
            acc_ref[...] = jnp.zeros_like(acc_ref)

        acc_ref[...] += lax.dot_general(a_ref[...].astype(MXU_DT), b_ref[...].astype(MXU_DT), dims,
                                        preferred_element_type=F32)

        @pl.when(kk == nk - 1)
        def _():
            r = acc_ref[...]
            if alpha != 1.0:
                r = r * alpha
            if has_add:
                r = r + beta * add_ref[...].astype(F32)
            o_ref[...] = r.astype(out_dtype)

    in_specs = [a_spec, b_spec] + ([o_spec] if has_add else [])
    args = (a, b) + ((add,) if has_add else ())
    grid = (m // tm, n // tn, nk)
    call = dict(name=name, grid=grid, in_specs=in_specs, out_specs=[o_spec],
                out_shape=[jax.ShapeDtypeStruct((m, n), out_dtype)],
                scratch_shapes=[pltpu.VMEM((tm, tn), F32)],
                compiler_params=_cparams(("parallel", "parallel", "arbitrary")))
    (out,), got = carry_comm(call, body, args, comm, 1, *_grid_ends(*grid))
    return out if comm is None else (out, got)


def ln_res_fwd(x, y, g, b, s, *, name):
    t, d = x.shape
    tm = _tile(t, 512, 8)

    def body(x_ref, y_ref, g_ref, b_ref, o_ref, xh_ref, rs_ref):
        z = DN_ALPHA * x_ref[...] + s * y_ref[...]
        mu = jnp.mean(z, axis=-1, keepdims=True)
        zc = z - mu
        var = jnp.mean(zc * zc, axis=-1, keepdims=True)
        rstd = lax.rsqrt(var + LN_EPS)
        xh = zc * rstd
        xh_ref[...] = xh
        rs_ref[...] = rstd
        o_ref[...] = xh * g_ref[...] + b_ref[...]

    row = pl.BlockSpec((tm, d), lambda i: (i, 0))
    vec = pl.BlockSpec((1, d), lambda i: (0, 0))
    return pl.pallas_call(
        body, name=name, grid=(t // tm,),
        in_specs=[row, row, vec, vec],
        out_specs=[row, row, pl.BlockSpec((tm, 1), lambda i: (i, 0))],
        out_shape=[jax.ShapeDtypeStruct((t, d), F32), jax.ShapeDtypeStruct((t, d), F32),
                   jax.ShapeDtypeStruct((t, 1), F32)],
        compiler_params=_cparams(("parallel",)),
    )(x, y, g, b)


def ln_res_bwd(dout, xhat, rstd, g, *, name):
    t, d = dout.shape
    tm = _tile(t, 512, 8)

    def body(do_ref, xh_ref, rs_ref, g_ref, dz_ref, dg_ref, db_ref):
        i = pl.program_id(0)

        @pl.when(i == 0)
        def _():
            dg_ref[...] = jnp.zeros_like(dg_ref)
            db_ref[...] = jnp.zeros_like(db_ref)

        do = do_ref[...]
        xh = xh_ref[...]
        dxh = do * g_ref[...]
        m1 = jnp.mean(dxh, axis=-1, keepdims=True)
        m2 = jnp.mean(dxh * xh, axis=-1, keepdims=True)
        dz_ref[...] = rs_ref[...] * (dxh - m1 - xh * m2)
        dg_ref[...] += jnp.sum(do * xh, axis=0, keepdims=True)
        db_ref[...] += jnp.sum(do, axis=0, keepdims=True)

    row = pl.BlockSpec((tm, d), lambda i: (i, 0))
    vec = pl.BlockSpec((1, d), lambda i: (0, 0))
    return pl.pallas_call(
        body, name=name, grid=(t // tm,),
        in_specs=[row, row, pl.BlockSpec((tm, 1), lambda i: (i, 0)), vec],
        out_specs=[row, vec, vec],
        out_shape=[jax.ShapeDtypeStruct((t, d), F32), jax.ShapeDtypeStruct((1, d), F32),
                   jax.ShapeDtypeStruct((1, d), F32)],
        compiler_params=_cparams(("arbitrary",)),
    )(dout, xhat, rstd, g)


def _sigmoid(x):
    return 1.0 / (1.0 + jnp.exp(-x))


def act_fwd(gu, *, name):
    t, f2 = gu.shape
    f = f2 // 2
    tm = _tile(t, 256, 8)

    def body(gu_ref, h_ref):
        g = gu_ref[:, :f]
        h_ref[...] = (g * _sigmoid(g) * gu_ref[:, f:]).astype(h_ref.dtype)

    return pl.pallas_call(
        body, name=name, grid=(t // tm,),
        in_specs=[pl.BlockSpec((tm, f2), lambda i: (i, 0))],
        out_specs=pl.BlockSpec((tm, f), lambda i: (i, 0)),
        out_shape=jax.ShapeDtypeStruct((t, f), MXU_DT),
        compiler_params=_cparams(("parallel",)),
    )(gu)


def act_bwd(gu, dh, *, name):
    t, f2 = gu.shape
    f = f2 // 2
    tm = _tile(t, 256, 8)

    def body(gu_ref, dh_ref, o_ref):
        g = gu_ref[:, :f]
        u = gu_ref[:, f:]
        dh = dh_ref[...]
        sg = _sigmoid(g)
        o_ref[:, f:] = (dh * g * sg).astype(o_ref.dtype)
        o_ref[:, :f] = (dh * u * sg * (1.0 + g * (1.0 - sg))).astype(o_ref.dtype)

    return pl.pallas_call(
        body, name=name, grid=(t // tm,),
        in_specs=[pl.BlockSpec((tm, f2), lambda i: (i, 0)), pl.BlockSpec((tm, f), lambda i: (i, 0))],
        out_specs=pl.BlockSpec((tm, f2), lambda i: (i, 0)),
        out_shape=jax.ShapeDtypeStruct((t, f2), MXU_DT),
        compiler_params=_cparams(("parallel",)),
    )(gu, dh)


def loss_head(y, target, *, name):
    t, d = y.shape
    tm = _tile(t, 512, 8)

    def body(y_ref, t_ref, dy_ref, l_ref):
        i = pl.program_id(0)

        @pl.when(i == 0)
        def _():
            l_ref[...] = jnp.zeros_like(l_ref)

        err = y_ref[...] - t_ref[...]
        dy_ref[...] = err * (1.0 / d)
        part = jnp.sum(jnp.sum(err * err, axis=-1, keepdims=True), axis=0, keepdims=True)
        l_ref[...] += jnp.broadcast_to(part * (0.5 / d), l_ref.shape)

    row = pl.BlockSpec((tm, d), lambda i: (i, 0))
    return pl.pallas_call(
        body, name=name, grid=(t // tm,),
        in_specs=[row, row],
        out_specs=[row, pl.BlockSpec((1, LANES), lambda i: (0, 0))],
        out_shape=[jax.ShapeDtypeStruct((t, d), F32), jax.ShapeDtypeStruct((1, LANES), F32)],
        compiler_params=_cparams(("arbitrary",)),
    )(y, target)


def _dot(a, b):
    return lax.dot_general(a, b, (((1,), (0,)), ((), ())), preferred_element_type=F32)


def _dot_nt(a, b):
    return lax.dot_general(a, b, (((1,), (1,)), ((), ())), preferred_element_type=F32)


def _dot_tn(a, b):
    return lax.dot_general(a, b, (((0,), (0,)), ((), ())), preferred_element_type=F32)


def _dot_hi(a, b):
    return lax.dot_general(a, b, (((1,), (0,)), ((), ())), preferred_element_type=F32, precision=HI)


def _dot_nt_hi(a, b):
    return lax.dot_general(a, b, (((1,), (1,)), ((), ())), preferred_element_type=F32, precision=HI)


def _split_dot(x, u):
    hi = x.astype(MXU_DT)
    lo = (x - hi.astype(F32)).astype(MXU_DT)
    return _dot(hi, u) + _dot(lo, u)


def _mem_probs(q_ref, kv_ref, h):
    lo = h * MEM_HEAD_DIM
    qh = q_ref[:, lo:lo + MEM_HEAD_DIM].astype(MXU_DT)
    kh = kv_ref[:, lo:lo + MEM_HEAD_DIM].astype(MXU_DT)
    s = _dot_nt(qh, kh) * (MEM_HEAD_DIM ** -0.5)
    s = s - jnp.max(s, axis=-1, keepdims=True)
    p = jnp.exp(s)
    return p / jnp.sum(p, axis=-1, keepdims=True), qh, kh


def memattn_fwd(q, kv, *, name):
    t, d = q.shape
    tm = _tile(t, 512, 8)

    def body(q_ref, kv_ref, o_ref):
        for h in range(MEM_HEADS):
            lo = h * MEM_HEAD_DIM
            p, _, _ = _mem_probs(q_ref, kv_ref, h)
            vh = kv_ref[:, d + lo:d + lo + MEM_HEAD_DIM].astype(MXU_DT)
            o_ref[:, lo:lo + MEM_HEAD_DIM] = _dot(p.astype(MXU_DT), vh)

    return pl.pallas_call(
        body, name=name, grid=(t // tm,),
        in_specs=[pl.BlockSpec((tm, d), lambda i: (i, 0)), pl.BlockSpec(kv.shape, lambda i: (0, 0))],
        out_specs=pl.BlockSpec((tm, d), lambda i: (i, 0)),
        out_shape=jax.ShapeDtypeStruct((t, d), F32),
        compiler_params=_cparams(("parallel",)),
    )(q, kv)


def memattn_bwd(q, kv, datt, *, name):
    t, d = q.shape
    tm = _tile(t, 512, 8)
    scale = MEM_HEAD_DIM ** -0.5

    def body(q_ref, kv_ref, da_ref, dq_ref, dkv_ref):
        @pl.when(pl.program_id(0) == 0)
        def _():
            dkv_ref[...] = jnp.zeros_like(dkv_ref)

        for h in range(MEM_HEADS):
            lo = h * MEM_HEAD_DIM
            p, qh, kh = _mem_probs(q_ref, kv_ref, h)
            vh = kv_ref[:, d + lo:d + lo + MEM_HEAD_DIM].astype(MXU_DT)
            da = da_ref[:, lo:lo + MEM_HEAD_DIM].astype(MXU_DT)
            dp = _dot_nt(da, vh)
            ds = p * (dp - jnp.sum(dp * p, axis=-1, keepdims=True))
            dsb = ds.astype(MXU_DT)
            dq_ref[:, lo:lo + MEM_HEAD_DIM] = _dot(dsb, kh) * scale
            dkv_ref[:, lo:lo + MEM_HEAD_DIM] += _dot_tn(dsb, qh) * scale
            dkv_ref[:, d + lo:d + lo + MEM_HEAD_DIM] += _dot_tn(p.astype(MXU_DT), da)

    row = pl.BlockSpec((tm, d), lambda i: (i, 0))
    full = pl.BlockSpec(kv.shape, lambda i: (0, 0))
    return pl.pallas_call(
        body, name=name, grid=(t // tm,),
        in_specs=[row, full, row],
        out_specs=[row, full],
        out_shape=[jax.ShapeDtypeStruct((t, d), F32), jax.ShapeDtypeStruct(kv.shape, F32)],
        compiler_params=_cparams(("arbitrary",)),
    )(q, kv, datt)


def _halo(k):
    return 8 * ((k - 1 + 7) // 8)


def dwconv_fwd(u, w, bias, *, col0=0, width=None, name):
    t = u.shape[0]
    kk, c = w.shape
    width = c if width is None else width
    assert width == c and col0 % c == 0
    cb = col0 // c
    hb = _halo(kk)
    tm = _tile(t, 512, hb)
    r = tm // hb
    has_bias = bias is not None

    def body(*refs):
        if has_bias:
            prev_ref, cur_ref, w_ref, b_ref, o_ref, scr = refs
        else:
            prev_ref, cur_ref, w_ref, o_ref, scr = refs
        i = pl.program_id(0)
        scr[0:hb, :] = jnp.where(i == 0, 0.0, prev_ref[...])
        scr[hb:hb + tm, :] = cur_ref[...]
        acc = jnp.zeros((tm, c), F32)
        for k in range(kk):
            acc = acc + w_ref[k:k + 1, :] * scr[pl.ds(hb - (kk - 1) + k, tm), :]
        if has_bias:
            acc = acc + b_ref[...]
        o_ref[...] = acc

    in_specs = [pl.BlockSpec((hb, c), lambda i: (jnp.maximum(i * r - 1, 0), cb)),
                pl.BlockSpec((tm, c), lambda i: (i, cb)),
                pl.BlockSpec((kk, c), lambda i: (0, 0))]
    args = [u, u, w]
    if has_bias:
        in_specs.append(pl.BlockSpec((1, c), lambda i: (0, 0)))
        args.append(bias)
    return pl.pallas_call(
        body, name=name, grid=(t // tm,),
        in_specs=in_specs,
        out_specs=pl.BlockSpec((tm, c), lambda i: (i, 0)),
        out_shape=jax.ShapeDtypeStruct((t, c), F32),
        scratch_shapes=[pltpu.VMEM((hb + tm, c), F32)],
        compiler_params=_cparams(("parallel",)),
    )(*args)


def dwconv_bwd(dc, u, w, *, col0=0, name):
    t, c = dc.shape
    kk = w.shape[0]
    assert col0 % c == 0
    cb = col0 // c
    hb = _halo(kk)
    tm = _tile(t, 512, hb)
    r = tm // hb
    n = t // tm

    def body(dcur_ref, dnext_ref, uprev_ref, ucur_ref, w_ref, du_ref, dw_ref, db_ref, sd, su):
        i = pl.program_id(0)

        @pl.when(i == 0)
        def _():
            dw_ref[...] = jnp.zeros_like(dw_ref)
            db_ref[...] = jnp.zeros_like(db_ref)

        dcur = dcur_ref[...]
        sd[0:tm, :] = dcur
        sd[tm:tm + hb, :] = jnp.where(i == n - 1, 0.0, dnext_ref[...])
        su[0:hb, :] = jnp.where(i == 0, 0.0, uprev_ref[...])
        su[hb:hb + tm, :] = ucur_ref[...]
        acc = jnp.zeros((tm, c), F32)
        for k in range(kk):
            acc = acc + w_ref[k:k + 1, :] * sd[pl.ds(kk - 1 - k, tm), :]
            dw_ref[k:k + 1, :] += jnp.sum(dcur * su[pl.ds(hb - (kk - 1) + k, tm), :], axis=0, keepdims=True)
        du_ref[...] = acc
        db_ref[...] += jnp.sum(dcur, axis=0, keepdims=True)

    return pl.pallas_call(
        body, name=name, grid=(n,),
        in_specs=[pl.BlockSpec((tm, c), lambda i: (i, 0)),
                  pl.BlockSpec((hb, c), lambda i: (jnp.minimum((i + 1) * r, n * r - 1), 0)),
                  pl.BlockSpec((hb, c), lambda i: (jnp.maximum(i * r - 1, 0), cb)),
                  pl.BlockSpec((tm, c), lambda i: (i, cb)),
                  pl.BlockSpec((kk, c), lambda i: (0, 0))],
        out_specs=[pl.BlockSpec((tm, c), lambda i: (i, 0)),
                   pl.BlockSpec((kk, c), lambda i: (0, 0)),
                   pl.BlockSpec((1, c), lambda i: (0, 0))],
        out_shape=[jax.ShapeDtypeStruct((t, c), F32), jax.ShapeDtypeStruct((kk, c), F32),
                   jax.ShapeDtypeStruct((1, c), F32)],
        scratch_shapes=[pltpu.VMEM((tm + hb, c), F32), pltpu.VMEM((hb + tm, c), F32)],
        compiler_params=_cparams(("arbitrary",)),
    )(dc, dc, u, u, w)


def glu_fwd(proj, *, name):
    t = proj.shape[0]
    c = GROUP_WIDTH
    tm = _tile(t, 1024, 8)
    vb, gb = P_CONF // c, P_CONF // c + 1

    def body(v_ref, g_ref, o_ref):
        o_ref[...] = v_ref[...] * _sigmoid(g_ref[...])

    return pl.pallas_call(
        body, name=name, grid=(t // tm,),
        in_specs=[pl.BlockSpec((tm, c), lambda i: (i, vb)), pl.BlockSpec((tm, c), lambda i: (i, gb))],
        out_specs=pl.BlockSpec((tm, c), lambda i: (i, 0)),
        out_shape=jax.ShapeDtypeStruct((t, c), F32),
        compiler_params=_cparams(("parallel",)),
    )(proj, proj)


def glu_bwd(proj, du, *, name):
    t = proj.shape[0]
    c = GROUP_WIDTH
    tm = _tile(t, 1024, 8)
    vb, gb = P_CONF // c, P_CONF // c + 1

    def body(v_ref, g_ref, du_ref, o_ref):
        sg = _sigmoid(g_ref[...])
        du = du_ref[...]
        o_ref[:, :c] = du * sg
        o_ref[:, c:] = du * v_ref[...] * sg * (1.0 - sg)

    return pl.pallas_call(
        body, name=name, grid=(t // tm,),
        in_specs=[pl.BlockSpec((tm, c), lambda i: (i, vb)), pl.BlockSpec((tm, c), lambda i: (i, gb)),
                  pl.BlockSpec((tm, c), lambda i: (i, 0))],
        out_specs=pl.BlockSpec((tm, 2 * c), lambda i: (i, 0)),
        out_shape=jax.ShapeDtypeStruct((t, 2 * c), F32),
        compiler_params=_cparams(("parallel",)),
    )(proj, proj, du)


def _group_mean_matrix(c, groups):
    gsz = c // groups
    ri = lax.broadcasted_iota(jnp.int32, (c, c), 0) // gsz
    ci = lax.broadcasted_iota(jnp.int32, (c, c), 1) // gsz
    return jnp.where(ri == ci, 1.0 / gsz, 0.0).astype(F32)


def gn_silu_fwd(cx, gamma, beta, *, name):
    t, c = cx.shape
    tm = _tile(t, 1024, 8)

    def body(c_ref, g_ref, b_ref, o_ref):
        gm = _group_mean_matrix(c, CONF_GROUPS)
        x = c_ref[...]
        mu = _dot_hi(x, gm)
        xc = x - mu
        var = _dot_hi(xc * xc, gm)
        a = xc * lax.rsqrt(var + LN_EPS) * g_ref[...] + b_ref[...]
        o_ref[...] = a * _sigmoid(a)

    row = pl.BlockSpec((tm, c), lambda i: (i, 0))
    vec = pl.BlockSpec((1, c), lambda i: (0, 0))
    return pl.pallas_call(
        body, name=name, grid=(t // tm,),
        in_specs=[row, vec, vec], out_specs=row,
        out_shape=jax.ShapeDtypeStruct((t, c), F32),
        compiler_params=_cparams(("parallel",)),
    )(cx, gamma, beta)


def gn_silu_bwd(cx, gamma, beta, dy, *, name):
    t, c = cx.shape
    tm = _tile(t, 1024, 8)

    def body(c_ref, g_ref, b_ref, dy_ref, dc_ref, dg_ref, db_ref):
        @pl.when(pl.program_id(0) == 0)
        def _():
            dg_ref[...] = jnp.zeros_like(dg_ref)
            db_ref[...] = jnp.zeros_like(db_ref)

        gm = _group_mean_matrix(c, CONF_GROUPS)
        x = c_ref[...]
        mu = _dot_hi(x, gm)
        xc = x - mu
        var = _dot_hi(xc * xc, gm)
        rstd = lax.rsqrt(var + LN_EPS)
        nrm = xc * rstd
        a = nrm * g_ref[...] + b_ref[...]
        sa = _sigmoid(a)
        da = dy_ref[...] * sa * (1.0 + a * (1.0 - sa))
        dg_ref[...] += jnp.sum(da * nrm, axis=0, keepdims=True)
        db_ref[...] += jnp.sum(da, axis=0, keepdims=True)
        dn = da * g_ref[...]
        dc_ref[...] = rstd * (dn - _dot_hi(dn, gm) - nrm * _dot_hi(dn * nrm, gm))

    row = pl.BlockSpec((tm, c), lambda i: (i, 0))
    vec = pl.BlockSpec((1, c), lambda i: (0, 0))
    return pl.pallas_call(
        body, name=name, grid=(t // tm,),
        in_specs=[row, vec, vec, row], out_specs=[row, vec, vec],
        out_shape=[jax.ShapeDtypeStruct((t, c), F32), jax.ShapeDtypeStruct((1, c), F32),
                   jax.ShapeDtypeStruct((1, c), F32)],
        compiler_params=_cparams(("arbitrary",)),
    )(cx, gamma, beta, dy)


FOX_COL = 8
SMALL_BLK = P_SMALL // LANES


def _log_sigmoid(x):
    return jnp.minimum(x, 0.0) - jnp.log(1.0 + jnp.exp(-jnp.abs(x)))


def _fox_cols(shape):
    col = lax.broadcasted_iota(jnp.int32, shape, 1)
    return (col >= FOX_COL) & (col < FOX_COL + GROUP_HEADS)


def fox_gate_fwd(proj, bvec, *, name):
    t = proj.shape[0]
    tm = _tile(t, 256, 8)

    def body(s_ref, b_ref, o_ref, carry):
        @pl.when(pl.program_id(0) == 0)
        def _():
            carry[...] = jnp.zeros_like(carry)

        lf = jnp.where(_fox_cols((tm, LANES)), _log_sigmoid(s_ref[...] + b_ref[...]), 0.0)
        ri = lax.broadcasted_iota(jnp.int32, (tm, tm), 0)
        ci = lax.broadcasted_iota(jnp.int32, (tm, tm), 1)
        cum = _dot_hi(jnp.where(ri >= ci, 1.0, 0.0).astype(F32), lf) + carry[...]
        o_ref[...] = cum
        carry[...] = cum[tm - 1:tm, :]

    return pl.pallas_call(
        body, name=name, grid=(t // tm,),
        in_specs=[pl.BlockSpec((tm, LANES), lambda i: (i, SMALL_BLK)), pl.BlockSpec((1, LANES), lambda i: (0, 0))],
        out_specs=pl.BlockSpec((tm, LANES), lambda i: (i, 0)),
        out_shape=jax.ShapeDtypeStruct((t, LANES), F32),
        scratch_shapes=[pltpu.VMEM((1, LANES), F32)],
        compiler_params=_cparams(("arbitrary",)),
    )(proj, bvec)


def fox_gate_bwd(dcum, proj, bvec, *, name):
    t = proj.shape[0]
    tm = _tile(t, 256, 8)
    n = t // tm

    def body(d_ref, s_ref, b_ref, o_ref, db_ref, carry):
        @pl.when(pl.program_id(0) == 0)
        def _():
            carry[...] = jnp.zeros_like(carry)
            db_ref[...] = jnp.zeros_like(db_ref)

        ri = lax.broadcasted_iota(jnp.int32, (tm, tm), 0)
        ci = lax.broadcasted_iota(jnp.int32, (tm, tm), 1)
        dlf = _dot_hi(jnp.where(ri <= ci, 1.0, 0.0).astype(F32), d_ref[...]) + carry[...]
        carry[...] = dlf[0:1, :]
        x = s_ref[...] + b_ref[...]
        dx = jnp.where(_fox_cols((tm, LANES)), dlf * (1.0 - _sigmoid(x)), 0.0)
        o_ref[...] = dx
        db_ref[...] += jnp.sum(dx, axis=0, keepdims=True)

    return pl.pallas_call(
        body, name=name, grid=(n,),
        in_specs=[pl.BlockSpec((tm, LANES), lambda i: (n - 1 - i, 0)),
                  pl.BlockSpec((tm, LANES), lambda i: (n - 1 - i, SMALL_BLK)),
                  pl.BlockSpec((1, LANES), lambda i: (0, 0))],
        out_specs=[pl.BlockSpec((tm, LANES), lambda i: (n - 1 - i, 0)), pl.BlockSpec((1, LANES), lambda i: (0, 0))],
        out_shape=[jax.ShapeDtypeStruct((t, LANES), F32), jax.ShapeDtypeStruct((1, LANES), F32)],
        scratch_shapes=[pltpu.VMEM((1, LANES), F32)],
        compiler_params=_cparams(("arbitrary",)),
    )(dcum, proj, bvec)


def _head_masks(c):
    lane_head = lax.broadcasted_iota(jnp.int32, (1, c), 1) // HEAD_DIM
    return [lane_head == h for h in range(GROUP_HEADS)]


def _attn_tiles(t, tq, tk):
    tq = _tile(t, tq, 8)
    tk = _tile(t, tk, LANES)
    return tq, tk, t // tq, t // tk


def _grid_ends(*sizes):
    first = lambda: functools.reduce(lambda a, b: a & b, [pl.program_id(d) == 0 for d in range(len(sizes))])
    last = lambda: functools.reduce(lambda a, b: a & b, [pl.program_id(d) == s - 1 for d, s in enumerate(sizes)])
    return first, last


EXP_DEAD = -110.0


def _key_norm_max(k_ref, nk, tk, masks):
    lane = lax.broadcasted_iota(jnp.int32, (1, LANES), 1)

    def one(jt, km):
        kb = k_ref[pl.ds(pl.multiple_of(jt * tk, tk), tk), :].astype(MXU_DT).astype(F32)
        sq = kb * kb
        for h in range(GROUP_HEADS):
            top = jnp.max(jnp.sum(jnp.where(masks[h], sq, 0.0), axis=-1, keepdims=True))
            km = jnp.where(lane == h, jnp.maximum(km, top), km)
        return km

    return lax.fori_loop(0, nk, one, jnp.zeros((1, LANES), F32))


def _fox_reach(qh, km, cc_ref, scale):
    out = []
    for h in range(GROUP_HEADS):
        qf = qh[h].astype(F32)
        qn = jnp.sqrt(jnp.sum(qf * qf, axis=-1, keepdims=True))
        out.append(scale * 1.001 * qn * jnp.sqrt(km[:, h:h + 1]) + cc_ref[:, FOX_COL + h:FOX_COL + h + 1])
    return out


def _fox_alive(reach, top, cr_ref, j, tk):
    ends = cr_ref[jnp.maximum(j, 0)][:, tk - 1:tk]
    worst = jnp.float32(NEG_BIG)
    for h in range(GROUP_HEADS):
        worst = jnp.maximum(worst, jnp.max(reach[h] - top[h]) - jnp.max(ends[h:h + 1, :]))
    return (worst > EXP_DEAD).astype(jnp.int32)


def fox_fwd(proj, cum, cum_t, *, name, tq=512, tk=512, comm=None):
    t = proj.shape[0]
    c = GROUP_WIDTH
    tq, tk, nq, nk = _attn_tiles(t, tq, tk)
    assert tq == tk
    qb = P_FOX // c
    scale = HEAD_DIM ** -0.5
    cr3 = cum_t.reshape(8, nk, tk).transpose(1, 0, 2)

    def body(q_ref, k_ref, v_ref, cc_ref, cr_ref, o_ref, lse_ref, m_scr, l_scr, acc_scr, km_scr):
        i = pl.program_id(0)
        masks = _head_masks(c)

        @pl.when(i == 0)
        def _():
            km_scr[...] = _key_norm_max(k_ref, nk, tk, masks)

        q = q_ref[...]
        qh = [jnp.where(masks[h], q, 0.0).astype(MXU_DT) for h in range(GROUP_HEADS)]
        reach = _fox_reach(qh, km_scr[...], cc_ref, scale)
        m_scr[...] = jnp.full_like(m_scr, NEG_BIG)
        l_scr[...] = jnp.zeros_like(l_scr)
        acc_scr[...] = jnp.zeros_like(acc_scr)

        def tile(j, diagonal):
            rows = pl.ds(pl.multiple_of(j * tk, tk), tk)
            kb = k_ref[rows, :].astype(MXU_DT)
            vb = v_ref[rows, :].astype(MXU_DT)
            crj = cr_ref[j]
            if diagonal:
                causal = (lax.broadcasted_iota(jnp.int32, (tq, tk), 1) <= lax.broadcasted_iota(jnp.int32, (tq, tk), 0))
            acc = acc_scr[...]
            for h in range(GROUP_HEADS):
                s = _dot_nt(qh[h], kb) * scale + (cc_ref[:, FOX_COL + h:FOX_COL + h + 1] - crj[h:h + 1, :])
                if diagonal:
                    s = jnp.where(causal, s, NEG_BIG)
                m_old = m_scr[h]
                m_new = jnp.maximum(m_old, jnp.max(s, axis=-1, keepdims=True))
                p = jnp.exp(s - m_new)
                alpha = jnp.exp(m_old - m_new)
                l_scr[h] = alpha * l_scr[h] + jnp.sum(p, axis=-1, keepdims=True)
                m_scr[h] = m_new
                acc = jnp.where(masks[h], alpha * acc + _dot(p.astype(MXU_DT), vb), acc)
            acc_scr[...] = acc

        def alive(j):
            return _fox_alive(reach, [m_scr[h] for h in range(GROUP_HEADS)], cr_ref, j, tk)

        def step(state):
            j = i - state[0]
            tile(j, False)
            return state[0] + 1, alive(j - 1)

        tile(i, True)
        lax.while_loop(lambda s: (s[0] <= i) & (s[1] > 0), step, (jnp.int32(1), alive(i - 1)))
        acc = acc_scr[...]
        o = jnp.zeros_like(acc)
        lse = jnp.zeros((tq, LANES), F32)
        lane = lax.broadcasted_iota(jnp.int32, (1, LANES), 1)
        for h in range(GROUP_HEADS):
            o = jnp.where(masks[h], acc / l_scr[h], o)
            lse = jnp.where(lane == h, m_scr[h] + jnp.log(l_scr[h]), lse)
        o_ref[...] = o
        lse_ref[...] = lse

    resident = lambda blk: pl.BlockSpec((t, c), lambda i: (0, blk), pipeline_mode=pl.Buffered(1))
    call = dict(
        name=name, grid=(nq,),
        in_specs=[pl.BlockSpec((tq, c), lambda i: (i, qb)), resident(qb + 1), resident(qb + 2),
                  pl.BlockSpec((tq, LANES), lambda i: (i, 0)),
                  pl.BlockSpec((nk, 8, tk), lambda i: (0, 0, 0), pipeline_mode=pl.Buffered(1))],
        out_specs=[pl.BlockSpec((tq, c), lambda i: (i, 0)), pl.BlockSpec((tq, LANES), lambda i: (i, 0))],
        out_shape=[jax.ShapeDtypeStruct((t, c), F32), jax.ShapeDtypeStruct((t, LANES), F32)],
        scratch_shapes=[pltpu.VMEM((GROUP_HEADS, tq, 1), F32), pltpu.VMEM((GROUP_HEADS, tq, 1), F32),
                        pltpu.VMEM((tq, c), F32), pltpu.VMEM((1, LANES), F32)],
        compiler_params=_cparams(("arbitrary",)),
    )
    outs, got = carry_comm(call, body, (proj, proj, proj, cum, cr3), comm, 2, *_grid_ends(nq))
    return (*outs, got)


def fox_bwd(proj, cum, cum_t, o, lse, do, *, name, tq=512, tk=512, comm=None):
    t = proj.shape[0]
    c = GROUP_WIDTH
    tq, tk, nq, nk = _attn_tiles(t, tq, tk)
    assert tq == tk
    qb = P_FOX // c
    scale = HEAD_DIM ** -0.5
    cr3 = cum_t.reshape(8, nk, tk).transpose(1, 0, 2)

    def body(q_ref, k_ref, v_ref, cc_ref, cr_ref, o_ref, lse_ref, do_ref,
             dq_ref, dk_hbm, dv_hbm, dcc_ref, dcr_ref, dq_scr, rs_scr, dk_scr, dv_scr, km_scr):
        i = pl.program_id(0)
        masks = _head_masks(c)

        @pl.when(i == 0)
        def _():
            dk_scr[...] = jnp.zeros_like(dk_scr)
            dv_scr[...] = jnp.zeros_like(dv_scr)
            dcr_ref[...] = jnp.zeros_like(dcr_ref)
            km_scr[...] = _key_norm_max(k_ref, nk, tk, masks)

        q = q_ref[...]
        qf = q.astype(MXU_DT)
        qh = [jnp.where(masks[h], q, 0.0).astype(MXU_DT) for h in range(GROUP_HEADS)]
        do = do_ref[...]
        dob = do.astype(MXU_DT)
        doh = [jnp.where(masks[h], do, 0.0).astype(MXU_DT) for h in range(GROUP_HEADS)]
        doo = do * o_ref[...]
        delta = [jnp.sum(jnp.where(masks[h], doo, 0.0), axis=-1, keepdims=True) for h in range(GROUP_HEADS)]
        lse_h = [lse_ref[:, h:h + 1] for h in range(GROUP_HEADS)]
        reach = _fox_reach(qh, km_scr[...], cc_ref, scale)
        dq_scr[...] = jnp.zeros_like(dq_scr)
        rs_scr[...] = jnp.zeros_like(rs_scr)

        def tile(j, diagonal):
            rows = pl.ds(pl.multiple_of(j * tk, tk), tk)
            kb = k_ref[rows, :].astype(MXU_DT)
            vb = v_ref[rows, :].astype(MXU_DT)
            crj = cr_ref[j]
            if diagonal:
                causal = (lax.broadcasted_iota(jnp.int32, (tq, tk), 1) <= lax.broadcasted_iota(jnp.int32, (tq, tk), 0))
            dq = dq_scr[...]
            dk_upd = jnp.zeros((tk, c), F32)
            dv_upd = jnp.zeros((tk, c), F32)
            for h in range(GROUP_HEADS):
                s = _dot_nt(qh[h], kb) * scale + (cc_ref[:, FOX_COL + h:FOX_COL + h + 1] - crj[h:h + 1, :])
                p = jnp.exp(s - lse_h[h])
                if diagonal:
                    p = jnp.where(causal, p, 0.0)
                ds = p * (_dot_nt(doh[h], vb) - delta[h])
                dsb = ds.astype(MXU_DT)
                dq = jnp.where(masks[h], dq + _dot(dsb, kb) * scale, dq)
                dk_upd = jnp.where(masks[h], _dot_tn(dsb, qf) * scale, dk_upd)
                dv_upd = jnp.where(masks[h], _dot_tn(p.astype(MXU_DT), dob), dv_upd)
                dcr_ref[j, h:h + 1, :] += -jnp.sum(ds, axis=0, keepdims=True)
                rs_scr[h] += jnp.sum(ds, axis=-1, keepdims=True)
            dq_scr[...] = dq
            dk_scr[rows, :] += dk_upd
            dv_scr[rows, :] += dv_upd

        def alive(j):
            return _fox_alive(reach, lse_h, cr_ref, j, tk)

        def step(state):
            j = i - state[0]
            tile(j, False)
            return state[0] + 1, alive(j - 1)

        tile(i, True)
        lax.while_loop(lambda s: (s[0] <= i) & (s[1] > 0), step, (jnp.int32(1), alive(i - 1)))
        dq_ref[...] = dq_scr[...]
        lane = lax.broadcasted_iota(jnp.int32, (1, LANES), 1)
        dcc = jnp.zeros((tq, LANES), F32)
        for h in range(GROUP_HEADS):
            dcc = jnp.where(lane == FOX_COL + h, rs_scr[h], dcc)
        dcc_ref[...] = dcc

        @pl.when(i == nq - 1)
        def _():
            pltpu.sync_copy(dk_scr, dk_hbm)
            pltpu.sync_copy(dv_scr, dv_hbm)

    qrow = lambda i: (i, 0)
    resident = lambda blk: pl.BlockSpec((t, c), lambda i: (0, blk), pipeline_mode=pl.Buffered(1))
    hbm = pl.BlockSpec(memory_space=pl.ANY)
    call = dict(
        name=name, grid=(nq,),
        in_specs=[pl.BlockSpec((tq, c), lambda i: (i, qb)), resident(qb + 1), resident(qb + 2),
                  pl.BlockSpec((tq, LANES), qrow),
                  pl.BlockSpec((nk, 8, tk), lambda i: (0, 0, 0), pipeline_mode=pl.Buffered(1)),
                  pl.BlockSpec((tq, c), qrow), pl.BlockSpec((tq, LANES), qrow), pl.BlockSpec((tq, c), qrow)],
        out_specs=[pl.BlockSpec((tq, c), qrow), hbm, hbm, pl.BlockSpec((tq, LANES), qrow),
                   pl.BlockSpec((nk, 8, tk), lambda i: (0, 0, 0))],
        out_shape=[jax.ShapeDtypeStruct((t, c), F32), jax.ShapeDtypeStruct((t, c), F32),
                   jax.ShapeDtypeStruct((t, c), F32), jax.ShapeDtypeStruct((t, LANES), F32),
                   jax.ShapeDtypeStruct((nk, 8, tk), F32)],
        scratch_shapes=[pltpu.VMEM((tq, c), F32), pltpu.VMEM((GROUP_HEADS, tq, 1), F32),
                        pltpu.VMEM((t, c), F32), pltpu.VMEM((t, c), F32), pltpu.VMEM((1, LANES), F32)],
        compiler_params=_cparams(("arbitrary",)),
    )
    outs, got = carry_comm(call, body, (proj, proj, proj, cum, cr3, o, lse, do), comm, 5, *_grid_ends(nq))
    return (*outs, got)


SB_DEAD = -110.0


def _sb_logs(z, strict):
    tt = jnp.log(1.0 + jnp.exp(-jnp.abs(z)))
    log_keep = jnp.where(strict, -(jnp.maximum(z, 0.0) + tt), 0.0)
    log_beta = jnp.minimum(z, 0.0) - tt
    return log_keep, log_beta


def _tri(n, upper):
    a = lax.broadcasted_iota(jnp.int32, (n, n), 0)
    b = lax.broadcasted_iota(jnp.int32, (n, n), 1)
    return jnp.where((a < b) if upper else (a > b), 1.0, 0.0).astype(MXU_DT)


def _sb_carry_lane(jj, h):
    return GROUP_HEADS * jj + h


def sb_fwd(proj, *, name, tq=512, tk=256):
    t = proj.shape[0]
    c = GROUP_WIDTH
    tq, tk, nq, nk = _attn_tiles(t, tq, tk)
    assert nk * GROUP_HEADS <= LANES
    qb = P_SB // c
    scale = HEAD_DIM ** -0.5

    def body(q_ref, k_ref, v_ref, o_ref, rs_ref, r_scr, acc_scr):
        i = pl.program_id(0)
        last = ((i + 1) * tq - 1) // tk
        masks = _head_masks(c)
        q = q_ref[...]
        qh = [jnp.where(masks[h], q, 0.0).astype(MXU_DT) for h in range(GROUP_HEADS)]
        lane = lax.broadcasted_iota(jnp.int32, (1, LANES), 1)
        later = _tri(tk, upper=False)
        r_scr[...] = jnp.zeros_like(r_scr)
        acc_scr[...] = jnp.zeros_like(acc_scr)
        rs_ref[...] = jnp.full((tq, LANES), 2.0 * SB_DEAD, F32)

        def step(state):
            jj, _ = state
            j = last - jj
            rows = pl.ds(pl.multiple_of(j * tk, tk), tk)
            kb = k_ref[rows, :].astype(MXU_DT)
            vb = v_ref[rows, :].astype(MXU_DT)
            row = i * tq + lax.broadcasted_iota(jnp.int32, (tq, tk), 0)
            col = j * tk + lax.broadcasted_iota(jnp.int32, (tq, tk), 1)
            strict = col < row
            acc = acc_scr[...]
            rs = rs_ref[...]
            for h in range(GROUP_HEADS):
                z = _dot_nt(qh[h], kb) * scale
                log_keep, log_beta = _sb_logs(z, strict)
                r_old = r_scr[h]
                rs = jnp.where(lane == _sb_carry_lane(jj, h), r_old, rs)
                rest = r_old + _split_dot(log_keep, later)
                w = jnp.where(strict, jnp.exp(log_beta + rest), 0.0)
                acc = jnp.where(masks[h], acc + _dot(w.astype(MXU_DT), vb), acc)
                r_scr[h] = r_old + jnp.sum(log_keep, axis=-1, keepdims=True)
            acc_scr[...] = acc
            rs_ref[...] = rs
            return jj + 1, jnp.max(r_scr[...])

        lax.while_loop(lambda s: (s[0] <= last) & (s[1] > SB_DEAD), step, (jnp.int32(0), jnp.float32(0.0)))
        o_ref[...] = acc_scr[...]

    resident = lambda blk: pl.BlockSpec((t, c), lambda i: (0, blk), pipeline_mode=pl.Buffered(1))
    return pl.pallas_call(
        body, name=name, grid=(nq,),
        in_specs=[pl.BlockSpec((tq, c), lambda i: (i, qb)), resident(qb + 1), resident(qb + 2)],
        out_specs=[pl.BlockSpec((tq, c), lambda i: (i, 0)), pl.BlockSpec((tq, LANES), lambda i: (i, 0))],
        out_shape=[jax.ShapeDtypeStruct((t, c), F32), jax.ShapeDtypeStruct((t, LANES), F32)],
        scratch_shapes=[pltpu.VMEM((GROUP_HEADS, tq, 1), F32), pltpu.VMEM((tq, c), F32)],
        compiler_params=_cparams(("arbitrary",)),
    )(proj, proj, proj)


def sb_bwd(proj, rsave, do, *, name, tq=512, tk=256, comm=None):
    t = proj.shape[0]
    c = GROUP_WIDTH
    tq, tk, nq, nk = _attn_tiles(t, tq, tk)
    qb = P_SB // c
    scale = HEAD_DIM ** -0.5

    def body(q_ref, k_ref, v_ref, rs_ref, do_ref, dq_ref, dk_hbm, dv_hbm, e_scr, dq_scr, dk_scr, dv_scr):
        i = pl.program_id(0)
        last = ((i + 1) * tq - 1) // tk
        masks = _head_masks(c)

        @pl.when(i == 0)
        def _():
            dk_scr[...] = jnp.zeros_like(dk_scr)
            dv_scr[...] = jnp.zeros_like(dv_scr)

        e_scr[...] = jnp.zeros_like(e_scr)
        dq_scr[...] = jnp.zeros_like(dq_scr)
        q = q_ref[...]
        qf = q.astype(MXU_DT)
        qh = [jnp.where(masks[h], q, 0.0).astype(MXU_DT) for h in range(GROUP_HEADS)]
        do = do_ref[...]
        dob = do.astype(MXU_DT)
        doh = [jnp.where(masks[h], do, 0.0).astype(MXU_DT) for h in range(GROUP_HEADS)]
        later = _tri(tk, upper=False)
        earlier = _tri(tk, upper=True)
        rs = rs_ref[...]
        lane = lax.broadcasted_iota(jnp.int32, (1, LANES), 1)
        visited = jnp.where(jnp.max(rs, axis=0, keepdims=True) > SB_DEAD, (lane // GROUP_HEADS + 1).astype(F32), 0.0)
        n_visited = jnp.minimum(jnp.max(visited).astype(jnp.int32), last + 1)

        def step(it, carry):
            jj = n_visited - 1 - it
            j = last - jj
            rows = pl.ds(pl.multiple_of(j * tk, tk), tk)
            kb = k_ref[rows, :].astype(MXU_DT)
            vb = v_ref[rows, :].astype(MXU_DT)
            row = i * tq + lax.broadcasted_iota(jnp.int32, (tq, tk), 0)
            col = j * tk + lax.broadcasted_iota(jnp.int32, (tq, tk), 1)
            strict = col < row
            dq = dq_scr[...]
            dk_upd = jnp.zeros((tk, c), F32)
            dv_upd = jnp.zeros((tk, c), F32)
            for h in range(GROUP_HEADS):
                z = _dot_nt(qh[h], kb) * scale
                log_keep, log_beta = _sb_logs(z, strict)
                r_h = jnp.sum(jnp.where(lane == _sb_carry_lane(jj, h), rs, 0.0), axis=-1, keepdims=True)
                rest = r_h + _split_dot(log_keep, later)
                w = jnp.where(strict, jnp.exp(log_beta + rest), 0.0)
                e = w * _dot_nt(doh[h], vb)
                e_old = e_scr[h]
                dkeep = e_old + _split_dot(e, earlier)
                dz = jnp.where(strict, e * jnp.exp(log_keep) - dkeep * jnp.exp(log_beta), 0.0)
                dzb = dz.astype(MXU_DT)
                dq = jnp.where(masks[h], dq + _dot(dzb, kb) * scale, dq)
                dk_upd = jnp.where(masks[h], _dot_tn(dzb, qf) * scale, dk_upd)
                dv_upd = jnp.where(masks[h], _dot_tn(w.astype(MXU_DT), dob), dv_upd)
                e_scr[h] = e_old + jnp.sum(e, axis=-1, keepdims=True)
            dq_scr[...] = dq
            dk_scr[rows, :] += dk_upd
            dv_scr[rows, :] += dv_upd
            return carry

        lax.fori_loop(0, n_visited, step, 0)
        dq_ref[...] = dq_scr[...]

        @pl.when(i == nq - 1)
        def _():
            pltpu.sync_copy(dk_scr, dk_hbm)
            pltpu.sync_copy(dv_scr, dv_hbm)

    qrow = lambda i: (i, 0)
    resident = lambda blk: pl.BlockSpec((t, c), lambda i: (0, blk), pipeline_mode=pl.Buffered(1))
    hbm = pl.BlockSpec(memory_space=pl.ANY)
    call = dict(
        name=name, grid=(nq,),
        in_specs=[pl.BlockSpec((tq, c), lambda i: (i, qb)), resident(qb + 1), resident(qb + 2),
                  pl.BlockSpec((tq, LANES), qrow), pl.BlockSpec((tq, c), qrow)],
        out_specs=[pl.BlockSpec((tq, c), qrow), hbm, hbm],
        out_shape=[jax.ShapeDtypeStruct((t, c), F32)] * 3,
        scratch_shapes=[pltpu.VMEM((GROUP_HEADS, tq, 1), F32), pltpu.VMEM((tq, c), F32),
                        pltpu.VMEM((t, c), F32), pltpu.VMEM((t, c), F32)],
        compiler_params=_cparams(("arbitrary",)),
    )
    outs, got = carry_comm(call, body, (proj, proj, proj, rsave, do), comm, 3, *_grid_ends(nq))
    return (*outs, got)


A_COL, B_COL = 0, 4
Z_BLK = P_Z // GROUP_WIDTH


NN = (((1,), (0,)), ((), ()))
NT = (((1,), (1,)), ((), ()))
TN = (((0,), (0,)), ((), ()))


def _terms(x, n):
    out, rem = [], x
    for _ in range(n):
        t = rem.astype(MXU_DT)
        out.append(t)
        rem = rem - t.astype(F32)
    return out


def _dotp(a, b, dims, a_terms=2, b_terms=2):
    at, bt = _terms(a, a_terms), _terms(b, b_terms)
    out = None
    for i, x in enumerate(at):
        for j, y in enumerate(bt):
            if i + j < max(a_terms, b_terms):
                r = lax.dot_general(x, y, dims, preferred_element_type=F32)
                out = r if out is None else out + r
    return out


def _silu(x):
    return x * _sigmoid(x)


def _dsilu(x):
    s = _sigmoid(x)
    return s * (1.0 + x * (1.0 - s))


def _head_sum(x, masks):
    out = jnp.zeros_like(x)
    for m in masks:
        out = jnp.where(m, jnp.sum(jnp.where(m, x, 0.0), axis=-1, keepdims=True), out)
    return out


def _expand(cols, col0, masks):
    out = jnp.zeros((cols.shape[0], GROUP_WIDTH), F32)
    for h, m in enumerate(masks):
        out = jnp.where(m, cols[:, col0 + h:col0 + h + 1], out)
    return out


def _reduce(x, col0, masks):
    lane = lax.broadcasted_iota(jnp.int32, (1, LANES), 1)
    out = jnp.zeros((x.shape[0], LANES), F32)
    for h, m in enumerate(masks):
        out = jnp.where(lane == col0 + h, jnp.sum(jnp.where(m, x, 0.0), axis=-1, keepdims=True), out)
    return out


def _block_ones():
    ri = lax.broadcasted_iota(jnp.int32, (GROUP_WIDTH, GROUP_WIDTH), 0) // HEAD_DIM
    ci = lax.broadcasted_iota(jnp.int32, (GROUP_WIDTH, GROUP_WIDTH), 1) // HEAD_DIM
    return jnp.where(ri == ci, 1.0, 0.0).astype(F32)


def _blk(x, hs):
    return jnp.concatenate([x] * GROUP_HEADS, axis=0) * hs


def _unblk(m, hs):
    mm = m * hs
    c = GDN_CHUNK
    return mm[0:c] + mm[c:2 * c] + mm[2 * c:3 * c] + mm[3 * c:4 * c]


def _row_mask4():
    ri = lax.broadcasted_iota(jnp.int32, (GROUP_WIDTH, LANES), 0) // HEAD_DIM
    ci = lax.broadcasted_iota(jnp.int32, (GROUP_WIDTH, LANES), 1)
    return jnp.where(ri + A_COL == ci, 1.0, 0.0).astype(F32)


def _gdn_chunk(xc, small, avec, dtvec, state, masks, hs):
    c = GDN_CHUNK
    w = GROUP_WIDTH
    b16 = lambda v: v.astype(MXU_DT)
    f = {}
    xq, xk, xv = xc[:, :w], xc[:, w:2 * w], xc[:, 2 * w:]
    qs, ks, v = _silu(xq), _silu(xk), _silu(xv)
    rq = lax.rsqrt(_head_sum(qs * qs, masks) + L2_EPS)
    rk = lax.rsqrt(_head_sum(ks * ks, masks) + L2_EPS)
    qn = qs * rq
    k = ks * rk
    q = qn * (HEAD_DIM ** -0.5)
    xg = small + dtvec
    sp = jnp.maximum(xg, 0.0) + jnp.log(1.0 + jnp.exp(-jnp.abs(xg)))
    g128 = -avec * sp
    beta128 = _sigmoid(small)
    ri = lax.broadcasted_iota(jnp.int32, (c, c), 0)
    ci = lax.broadcasted_iota(jnp.int32, (c, c), 1)
    tril = jnp.where(ri >= ci, 1.0, 0.0).astype(F32)
    gam128 = _dotp(tril, g128, NN, 1, 3)
    gam = _expand(gam128, A_COL, masks)
    bfull = _expand(beta128, B_COL, masks)
    mask4 = _row_mask4()
    ones = jnp.ones((c, LANES), F32)
    gam_row = _dotp(ones, jnp.concatenate([gam128] * GROUP_HEADS, axis=0) * mask4, NT, 1, 3)
    li = lax.broadcasted_iota(jnp.int32, (c, w), 0)
    lj = lax.broadcasted_iota(jnp.int32, (c, w), 1) % HEAD_DIM
    incl = li >= lj
    strict = li > lj
    dmat = jnp.exp(jnp.where(incl, gam - gam_row, NEG_BIG))
    egam = jnp.exp(gam)
    glast = gam[c - 1:c, :]
    ekd = jnp.exp(glast - gam)
    kb = k * bfull
    vb = v * bfull
    kbg = kb * egam
    qd = q * egam
    kd = k * ekd
    kblk = b16(_blk(k, hs))
    araw = _dot_nt(b16(kb), kblk)
    a = jnp.where(strict, araw * dmat, 0.0)
    tm = jnp.where(li == lj, 1.0, 0.0) - a
    p = a
    for _ in range(5):
        p = _dotp(p, _blk(p, hs), NN)
        tm = tm + _dotp(tm, _blk(p, hs), NN)
    tm16 = b16(tm)
    u = _dot(tm16, b16(_blk(vb, hs)))
    wm = _dot(tm16, b16(_blk(kbg, hs)))
    qk = _dot_nt(b16(q), kblk)
    aqk = jnp.where(incl, qk * dmat, 0.0)
    s16 = b16(state)
    vn = u - _dot(b16(wm), s16)
    o = _dot(b16(qd), s16) + _dot(b16(aqk), b16(_blk(vn, hs)))
    s_new = state * jnp.exp(glast) + hs * _dot_tn(b16(kd), b16(vn))
    f.update(xq=xq, xk=xk, xv=xv, v=v, rq=rq, rk=rk, qn=qn, k=k, q=q, xg=xg, g128=g128, beta128=beta128,
             tril=tril, gam=gam, bfull=bfull, mask4=mask4, ones=ones, incl=incl, strict=strict, li=li,
             dmat=dmat, egam=egam, glast=glast, ekd=ekd, kb=kb, vb=vb, kbg=kbg, qd=qd, kd=kd, kblk=kblk,
             araw=araw, tm=tm, tm16=tm16, wm=wm, qk=qk, aqk=aqk, s16=s16, vn=vn, o=o, s_new=s_new)
    return f


def _decay_rate(a_log):
    lane = lax.broadcasted_iota(jnp.int32, a_log.shape, 1)
    return jnp.where((lane >= A_COL) & (lane < A_COL + GROUP_HEADS), jnp.exp(a_log), 0.0)


def _gdn_post(o, z, ng, masks):
    r = lax.rsqrt(_head_sum(o * o, masks) * (1.0 / HEAD_DIM) + RMS_EPS)
    on = o * r
    return on, r, on * ng * _silu(z)


def gdn_fwd(cqkv, proj, avec, dtvec, ng, *, name, comm=None):
    t = cqkv.shape[0]
    c = GDN_CHUNK
    w = GROUP_WIDTH
    n = t // c

    def body(x_ref, z_ref, sm_ref, a_ref, dt_ref, ng_ref, y_ref, st_ref, s_scr):
        @pl.when(pl.program_id(0) == 0)
        def _():
            s_scr[...] = jnp.zeros_like(s_scr)

        masks = _head_masks(w)
        hs = _block_ones()
        state = s_scr[...]
        st_ref[0] = state
        f = _gdn_chunk(x_ref[...], sm_ref[...], _decay_rate(a_ref[...]), dt_ref[...], state, masks, hs)
        _, _, y = _gdn_post(f["o"], z_ref[...], ng_ref[...], masks)
        y_ref[...] = y
        s_scr[...] = f["s_new"]

    vec = pl.BlockSpec((1, LANES), lambda i: (0, 0))
    call = dict(
        name=name, grid=(n,),
        in_specs=[pl.BlockSpec((c, 3 * w), lambda i: (i, 0)),
                  pl.BlockSpec((c, w), lambda i: (i, Z_BLK)),
                  pl.BlockSpec((c, LANES), lambda i: (i, SMALL_BLK)),
                  vec, vec, pl.BlockSpec((1, w), lambda i: (0, 0))],
        out_specs=[pl.BlockSpec((c, w), lambda i: (i, 0)), pl.BlockSpec((1, w, w), lambda i: (i, 0, 0))],
        out_shape=[jax.ShapeDtypeStruct((t, w), F32), jax.ShapeDtypeStruct((n, w, w), F32)],
        scratch_shapes=[pltpu.VMEM((w, w), F32)],
        compiler_params=_cparams(("arbitrary",)),
    )
    outs, got = carry_comm(call, body, (cqkv, proj, proj, avec, dtvec, ng), comm, 2, *_grid_ends(n))
    return (*outs, got)


def gdn_bwd(cqkv, proj, avec, dtvec, ng, states, dy, *, name, comm=None):
    t = cqkv.shape[0]
    c = GDN_CHUNK
    w = GROUP_WIDTH
    n = t // c
    b16 = lambda v: v.astype(MXU_DT)

    def body(x_ref, z_ref, sm_ref, a_ref, dt_ref, ng_ref, st_ref, dy_ref,
             dx_ref, dz_ref, dsm_ref, dng_ref, dal_ref, ddt_ref, ds_scr):
        @pl.when(pl.program_id(0) == 0)
        def _():
            ds_scr[...] = jnp.zeros_like(ds_scr)
            dng_ref[...] = jnp.zeros_like(dng_ref)
            dal_ref[...] = jnp.zeros_like(dal_ref)
            ddt_ref[...] = jnp.zeros_like(ddt_ref)

        masks = _head_masks(w)
        hs = _block_ones()
        state = st_ref[0]
        avec_v = _decay_rate(a_ref[...])
        f = _gdn_chunk(x_ref[...], sm_ref[...], avec_v, dt_ref[...], state, masks, hs)
        z = z_ref[...]
        ng_v = ng_ref[...]
        dy_v = dy_ref[...]
        on, r, _ = _gdn_post(f["o"], z, ng_v, masks)
        sz = _silu(z)
        dz_ref[...] = dy_v * on * ng_v * _dsilu(z)
        d_on = dy_v * ng_v * sz
        dng_ref[...] += jnp.sum(dy_v * on * sz, axis=0, keepdims=True)
        do = r * (d_on - on * _head_sum(d_on * on, masks) * (1.0 / HEAD_DIM))
        do16 = b16(do)
        dsn = ds_scr[...]
        dsn16 = b16(dsn)
        s16, vn, kd, qd, wm = f["s16"], f["vn"], f["kd"], f["qd"], f["wm"]
        k, q, kblk, tm, tm16 = f["k"], f["q"], f["kblk"], f["tm"], f["tm16"]
        dmat, egam, glast, gam = f["dmat"], f["egam"], f["glast"], f["gam"]
        incl, strict, li = f["incl"], f["strict"], f["li"]
        vn16 = b16(vn)
        dvn = _unblk(_dot_tn(b16(f["aqk"]), do16), hs) + _dot(b16(kd), dsn16)
        daqk = jnp.where(incl, _dot_nt(do16, b16(_blk(vn, hs))), 0.0)
        dqd = _dot_nt(do16, s16)
        dvn16 = b16(dvn)
        ds_scr[...] = hs * (_dot_tn(b16(qd), do16) - _dot_tn(b16(wm), dvn16)) + dsn * jnp.exp(glast)
        dkd = _dot_nt(vn16, dsn16)
        dglast = jnp.sum(dsn * state, axis=0, keepdims=True) * jnp.exp(glast)
        du16 = dvn16
        dw16 = b16(-_dot_nt(dvn16, s16))
        dqk16 = b16(daqk * dmat)
        ddm = daqk * f["qk"]
        dq = _dot(dqk16, kblk)
        dk = _unblk(_dot_tn(dqk16, b16(q)), hs)
        dtm = _dot_nt(du16, b16(_blk(f["vb"], hs))) + _dot_nt(dw16, b16(_blk(f["kbg"], hs)))
        dvb = _unblk(_dot_tn(tm16, du16), hs)
        dkbg = _unblk(_dot_tn(tm16, dw16), hs)
        xx = _unblk(_dotp(tm, dtm, TN), hs)
        da = jnp.where(strict, -_dotp(xx, _blk(tm, hs), NT), 0.0)
        daraw16 = b16(da * dmat)
        ddm = ddm + da * f["araw"]
        dkb = _dot(daraw16, kblk)
        dk = dk + _unblk(_dot_tn(daraw16, b16(f["kb"])), hs)
        tcol = ddm * dmat
        dgam = tcol
        dgam128_row = _dotp(-tcol, f["ones"], TN, 2, 1) * f["mask4"]
        dgam128_row = (dgam128_row[0:c] + dgam128_row[c:2 * c] + dgam128_row[2 * c:3 * c] + dgam128_row[3 * c:4 * c])
        dk = dk + dkd * f["ekd"]
        tt = dkd * kd
        dgam = dgam - tt
        dglast = dglast + jnp.sum(tt, axis=0, keepdims=True)
        dq = dq + dqd * egam
        dgam = dgam + dqd * qd
        dkb = dkb + dkbg * egam
        dgam = dgam + dkbg * f["kbg"]
        dk = dk + dkb * f["bfull"]
        dbf = dkb * k + dvb * f["v"]
        dv = dvb * f["bfull"]
        dgam = dgam + jnp.where(li == c - 1, dglast, 0.0)
        beta128 = f["beta128"]
        db128 = _reduce(dbf, B_COL, masks) * beta128 * (1.0 - beta128)
        dgam128 = _reduce(dgam, A_COL, masks) + dgam128_row
        dg128 = _dotp(f["tril"], dgam128, TN, 1, 2)
        dxg = dg128 * (-avec_v * _sigmoid(f["xg"]))
        lane = lax.broadcasted_iota(jnp.int32, (1, LANES), 1)
        dsm_ref[...] = jnp.where(lane < B_COL, dxg, db128)
        ddt_ref[...] += jnp.sum(dxg, axis=0, keepdims=True)
        dal_ref[...] += jnp.sum(dg128 * f["g128"], axis=0, keepdims=True)
        dqn = dq * (HEAD_DIM ** -0.5)
        dqs = f["rq"] * (dqn - f["qn"] * _head_sum(dqn * f["qn"], masks))
        dks = f["rk"] * (dk - k * _head_sum(dk * k, masks))
        dx_ref[:, :w] = dqs * _dsilu(f["xq"])
        dx_ref[:, w:2 * w] = dks * _dsilu(f["xk"])
        dx_ref[:, 2 * w:] = dv * _dsilu(f["xv"])

    vec = pl.BlockSpec((1, LANES), lambda i: (0, 0))
    rev = lambda blk: (lambda i: (n - 1 - i, blk))
    call = dict(
        name=name, grid=(n,),
        in_specs=[pl.BlockSpec((c, 3 * w), rev(0)),
                  pl.BlockSpec((c, w), rev(Z_BLK)),
                  pl.BlockSpec((c, LANES), rev(SMALL_BLK)),
                  vec, vec, pl.BlockSpec((1, w), lambda i: (0, 0)),
                  pl.BlockSpec((1, w, w), lambda i: (n - 1 - i, 0, 0)),
                  pl.BlockSpec((c, w), rev(0))],
        out_specs=[pl.BlockSpec((c, 3 * w), rev(0)), pl.BlockSpec((c, w), rev(0)), pl.BlockSpec((c, LANES), rev(0)),
                   pl.BlockSpec((1, w), lambda i: (0, 0)), vec, vec],
        out_shape=[jax.ShapeDtypeStruct((t, 3 * w), F32), jax.ShapeDtypeStruct((t, w), F32),
                   jax.ShapeDtypeStruct((t, LANES), F32), jax.ShapeDtypeStruct((1, w), F32),
                   jax.ShapeDtypeStruct((1, LANES), F32), jax.ShapeDtypeStruct((1, LANES), F32)],
        scratch_shapes=[pltpu.VMEM((w, w), F32)],
        compiler_params=_cparams(("arbitrary",)),
    )
    outs, got = carry_comm(call, body, (cqkv, proj, proj, avec, dtvec, ng, states, dy), comm, 6, *_grid_ends(n))
    return (*outs, got)


def adamw(w, m, v, gslots, *, row0=0, name):
    r, c = w.shape
    s = gslots.shape[0]
    tr = _tile(r, 64, 8)
    assert row0 % tr == 0 and gslots.shape[2] == c
    rb = row0 // tr
    c1 = 1.0 - ADAM_B1 ** ADAM_STEP
    c2 = 1.0 - ADAM_B2 ** ADAM_STEP

    def body(w_ref, m_ref, v_ref, gs_ref, g_ref, d_ref, mo_ref, vo_ref):
        g = gs_ref[0]
        for k in range(1, s):
            g = g + gs_ref[k]
        m_new = ADAM_B1 * m_ref[...] + (1.0 - ADAM_B1) * g
        v_new = ADAM_B2 * v_ref[...] + (1.0 - ADAM_B2) * (g * g)
        m_hat = m_new / c1
        v_hat = v_new / c2
        g_ref[...] = g
        mo_ref[...] = m_new
        vo_ref[...] = v_new
        d_ref[...] = -ADAM_LR * (m_hat / (jnp.sqrt(v_hat) + ADAM_EPS) + ADAM_WD * w_ref[...])

    row = pl.BlockSpec((tr, c), lambda i: (i, 0))
    return pl.pallas_call(
        body, name=name, grid=(r // tr,),
        in_specs=[row, row, row, pl.BlockSpec((s, tr, c), lambda i: (0, rb + i, 0))],
        out_specs=[row] * 4,
        out_shape=[jax.ShapeDtypeStruct((r, c), F32)] * 4,
        compiler_params=_cparams(("parallel",)),
    )(w, m, v, gslots)


def slot_sum(slots, *, name):
    s, r, c = slots.shape

    def body(s_ref, o_ref):
        acc = s_ref[0]
        for k in range(1, s):
            acc = acc + s_ref[k]
        o_ref[...] = acc

    return pl.pallas_call(
        body, name=name, grid=(1,),
        in_specs=[pl.BlockSpec((s, r, c), lambda i: (0, 0, 0))],
        out_specs=pl.BlockSpec((r, c), lambda i: (0, 0)),
        out_shape=jax.ShapeDtypeStruct((r, c), F32),
        compiler_params=_cparams(("arbitrary",)),
    )(slots)


class Comm:
    def __init__(self, srcs, broadcast):
        self.srcs = list(srcs)
        self.broadcast = [broadcast] * len(self.srcs) if isinstance(broadcast, bool) else list(broadcast)
        self.n = len(self.srcs)
        self.out_shapes = [jax.ShapeDtypeStruct(((N_DEV,) + s.shape) if b else s.shape, s.dtype)
                           for s, b in zip(self.srcs, self.broadcast)]
        self.sems = [pltpu.SemaphoreType.DMA((self.n,))] * 3

    def _local(self, src_refs, out_refs, loc_sem, a, me):
        src = src_refs[a] if self.broadcast[a] else src_refs[a].at[me]
        return pltpu.make_async_copy(src, out_refs[a].at[me], loc_sem.at[a])

    def start(self, src_refs, out_refs, send_sem, recv_sem, loc_sem):
        x, y, c = lax.axis_index("x"), lax.axis_index("y"), lax.axis_index("c")
        me = 4 * x + 2 * y + c
        for a in range(self.n):
            self._local(src_refs, out_refs, loc_sem, a, me).start()
        for d in range(1, N_DEV):
            px, py, pc = x ^ ((d >> 2) & 1), y ^ ((d >> 1) & 1), c ^ (d & 1)
            peer = 4 * px + 2 * py + pc
            for a in range(self.n):
                src = src_refs[a] if self.broadcast[a] else src_refs[a].at[peer]
                pltpu.make_async_remote_copy(
                    src_ref=src, dst_ref=out_refs[a].at[me],
                    send_sem=send_sem.at[a], recv_sem=recv_sem.at[a],
                    device_id=(px, py, pc), device_id_type=pl.DeviceIdType.MESH).start()

    def wait(self, src_refs, out_refs, send_sem, recv_sem, loc_sem):
        x, y, c = lax.axis_index("x"), lax.axis_index("y"), lax.axis_index("c")
        me = 4 * x + 2 * y + c
        for a in range(self.n):
            seven = out_refs[a].at[pl.ds(0, N_DEV - 1)]
            pltpu.make_async_remote_copy(
                src_ref=seven, dst_ref=seven, send_sem=send_sem.at[a], recv_sem=recv_sem.at[a],
                device_id=(x, y, c), device_id_type=pl.DeviceIdType.MESH).wait()
            self._local(src_refs, out_refs, loc_sem, a, me).wait()


def exchange(srcs, *, broadcast, name):
    comm = Comm(srcs, broadcast)
    n = comm.n

    def body(*refs):
        src_refs, out_refs, sems = refs[:n], refs[n:2 * n], refs[2 * n:]
        comm.start(src_refs, out_refs, *sems)
        comm.wait(src_refs, out_refs, *sems)

    anyspec = pl.BlockSpec(memory_space=pl.ANY)
    return pl.pallas_call(
        body, name=name,
        in_specs=[anyspec] * n, out_specs=[anyspec] * n, out_shape=comm.out_shapes,
        scratch_shapes=comm.sems,
        compiler_params=pltpu.CompilerParams(has_side_effects=True),
    )(*srcs)


def carry_comm(call_kwargs, body, args, comm, n_out, is_first, is_last):
    if comm is None:
        return pl.pallas_call(body, **call_kwargs)(*args), []
    n_in, nc = len(args), comm.n
    n_scr = len(call_kwargs["scratch_shapes"])
    anyspec = pl.BlockSpec(memory_space=pl.ANY)

    def wrapped(*refs):
        ins, csrc = refs[:n_in], refs[n_in:n_in + nc]
        outs = refs[n_in + nc:n_in + nc + n_out]
        cout = refs[n_in + nc + n_out:n_in + 2 * nc + n_out]
        rest = refs[n_in + 2 * nc + n_out:]
        scr, sems = rest[:n_scr], rest[n_scr:]

        @pl.when(is_first())
        def _():
            comm.start(csrc, cout, *sems)

        body(*ins, *outs, *scr)

        @pl.when(is_last())
        def _():
            comm.wait(csrc, cout, *sems)

    kw = dict(call_kwargs)
    kw["in_specs"] = list(kw["in_specs"]) + [anyspec] * nc
    kw["out_specs"] = list(kw["out_specs"]) + [anyspec] * nc
    kw["out_shape"] = list(kw["out_shape"]) + comm.out_shapes
    kw["scratch_shapes"] = list(kw["scratch_shapes"]) + comm.sems
    cp = kw["compiler_params"]
    kw["compiler_params"] = pltpu.CompilerParams(dimension_semantics=cp.dimension_semantics,
                                                 vmem_limit_bytes=cp.vmem_limit_bytes, has_side_effects=True)
    res = pl.pallas_call(wrapped, **kw)(*args, *comm.srcs)
    return res[:n_out], res[n_out:]


def _pack(arrs):
    flat = []
    for a in arrs:
        f = a.reshape(-1).astype(F32)
        flat.append(jnp.pad(f, (0, (-f.shape[0]) % LANES)))
    buf = jnp.concatenate(flat)
    buf = jnp.pad(buf, (0, (-buf.shape[0]) % (8 * LANES)))
    return buf.reshape(-1, LANES)


def _unpack(buf, shapes):
    flat = buf.reshape(-1)
    out, off = [], 0
    for s in shapes:
        sz = int(np.prod(s))
        out.append(flat[off:off + sz].reshape(s))
        off += sz + (-sz) % LANES
    return out


def _win_to_aligned(w):
    o = np.cumsum((0,) + IN_SPLITS)
    seg = lambda i: w[..., o[i]:o[i + 1]]
    pad = jnp.zeros(w.shape[:-1] + (P_WIDTH - IN_WIDTH,), w.dtype)
    return jnp.concatenate([seg(0), seg(1), seg(4), seg(6), seg(7), seg(2), seg(3), seg(5), pad], axis=-1)


def _win_from_aligned(w):
    o = np.cumsum((0,) + IN_SPLITS)
    s = P_SMALL
    return jnp.concatenate([w[..., P_GDN:P_GDN + 768], w[..., P_Z:P_Z + 256], w[..., s:s + 4], w[..., s + 4:s + 8],
                            w[..., P_FOX:P_FOX + 768], w[..., s + 8:s + 12], w[..., P_CONF:P_CONF + 512],
                            w[..., P_SB:P_SB + 768]], axis=-1)


def _row128(vals, col0):
    return jnp.pad(vals.astype(F32)[None, :], ((0, 0), (col0, LANES - col0 - GROUP_HEADS)))


def _ffn_fwd(x, w, n, tag, comm=None, on_comm=None):
    gu = mm(x, w[f"gu{n}"], name=f"{tag}_gu", tm=1024, tn=512, tk=1024, comm=comm)
    if comm is not None:
        gu, got = gu
        on_comm(got)
    h = act_fwd(gu, name=f"{tag}_act")
    y = mm(h, w[f"d{n}"], name=f"{tag}_down", tm=1024, tn=512, tk=D_FF)
    out, xh, rs = ln_res_fwd(x, y, w[f"ln_ffn{n}_g"], w[f"ln_ffn{n}_b"], 0.5, name=f"{tag}_ln")
    return out, (x, gu, h, xh, rs)


def _ffn_bwd(dout, saved, w, n, tag, comm_dh=None, comm_dwgu=None, comm_dx=None):
    x, gu, h, xh, rs = saved
    wgu, wd = w[f"gu{n}"], w[f"d{n}"]
    got = [[], [], []]
    dz, dg, db = ln_res_bwd(dout, xh, rs, w[f"ln_ffn{n}_g"], name=f"{tag}_ln_bwd")
    dh = mm(dz, wd, mode="nt", alpha=0.5, name=f"{tag}_dh", tm=1024, tn=D_FF // 2, tk=1024, comm=comm_dh)
    if comm_dh is not None:
        dh, got[0] = dh
    dgu = act_bwd(gu, dh, name=f"{tag}_act_bwd")
    dwd = mm(h, dz, mode="tn", alpha=0.5, name=f"{tag}_dwd", tm=D_FF // 2, tn=1024, tk=512)
    c = comm_dwgu(dwd) if comm_dwgu is not None else None
    dwgu = mm(x, dgu, mode="tn", name=f"{tag}_dwgu", tm=1024, tn=D_FF // 2, tk=512, comm=c)
    if c is not None:
        dwgu, got[1] = dwgu
    c = comm_dx(dwgu) if comm_dx is not None else None
    dx = mm(dgu, wgu, mode="nt", add=dz, beta=DN_ALPHA, name=f"{tag}_dx", tm=1024, tn=1024, tk=D_FF // 2, comm=c)
    if c is not None:
        dx, got[2] = dx
    return dx, dwgu, dwd, dg, db, got


def _layer_fwd(x, mem, w, tag, comm_ffn1=None, on_ffn1=None, comm_gdn=None, on_gdn=None, comm_fox=None, on_fox=None):
    sv = {}
    x1, sv["ffn1"] = _ffn_fwd(x, w, 1, f"{tag}_ffn1", comm=comm_ffn1, on_comm=on_ffn1)
    proj = mm(x1, w["win"], name=f"{tag}_inproj", tm=1024, tn=640, tk=1024)
    cqkv = dwconv_fwd(proj, w["gdn_conv_w"], None, col0=P_GDN, name=f"{tag}_gdn_conv")
    ya, states, got = gdn_fwd(cqkv, proj, w["alog"], w["dtb"], w["ng"], name=f"{tag}_gdn", comm=comm_gdn)
    if on_gdn is not None:
        on_gdn(got)
    cum = fox_gate_fwd(proj, w["bf"], name=f"{tag}_fox_gate")
    cum_t = jnp.pad(cum[:, FOX_COL:FOX_COL + GROUP_HEADS].T, ((0, 8 - GROUP_HEADS), (0, 0)))
    yb, lse, got = fox_fwd(proj, cum, cum_t, name=f"{tag}_fox", comm=comm_fox)
    if on_fox is not None:
        on_fox(got)
    u = glu_fwd(proj, name=f"{tag}_glu")
    cc = dwconv_fwd(u, w["conf_dw_w"], w["conf_dw_b"], name=f"{tag}_conf_conv")
    yc = gn_silu_fwd(cc, w["conf_norm_g"], w["conf_norm_b"], name=f"{tag}_conf_norm")
    yd, rsave = sb_fwd(proj, name=f"{tag}_sb")
    ycat = jnp.concatenate([ya, yb, yc, yd], axis=1)
    mix = mm(ycat, w["wout"], name=f"{tag}_outproj")
    x2, xh2, rs2 = ln_res_fwd(x1, mix, w["ln_mix_g"], w["ln_mix_b"], 1.0, name=f"{tag}_ln_mix")
    sv["mix"] = (x1, proj, cqkv, states, cum, cum_t, yb, lse, u, cc, rsave, ycat, xh2, rs2)
    q = mm(x2, w["wq"], name=f"{tag}_memq")
    kv = mm(mem, w["wkv"], name=f"{tag}_memkv", tm=N_MEM)
    att = memattn_fwd(q, kv, name=f"{tag}_memattn")
    mo = mm(att, w["wo"], name=f"{tag}_memo")
    x3, xh3, rs3 = ln_res_fwd(x2, mo, w["ln_mem_g"], w["ln_mem_b"], 1.0, name=f"{tag}_ln_mem")
    sv["mem"] = (x2, q, kv, att, xh3, rs3)
    x4, sv["ffn2"] = _ffn_fwd(x3, w, 2, f"{tag}_ffn2")
    return x4, sv


def _layer_bwd(dx4, mem, sv, w, tag, plan, tail=None):
    t = dx4.shape[0]
    gr = {}
    dx3, gr["gu2"], gr["d2"], gr["ln_ffn2_g"], gr["ln_ffn2_b"], _ = _ffn_bwd(dx4, sv["ffn2"], w, 2, f"{tag}_ffn2")
    x2, q, kv, att, xh3, rs3 = sv["mem"]
    dz, gr["ln_mem_g"], gr["ln_mem_b"] = ln_res_bwd(dx3, xh3, rs3, w["ln_mem_g"], name=f"{tag}_ln_mem_bwd")
    datt = mm(dz, w["wo"], mode="nt", name=f"{tag}_datt")
    gr["wo"] = mm(att, dz, mode="tn", name=f"{tag}_dwo", tk=512)
    dq, dkv = memattn_bwd(q, kv, datt, name=f"{tag}_memattn_bwd")
    gr["wq"] = mm(x2, dq, mode="tn", name=f"{tag}_dwq", tk=512)
    gr["wkv"] = mm(mem, dkv, mode="tn", name=f"{tag}_dwkv", tk=N_MEM)
    dx2 = mm(dq, w["wq"], mode="nt", add=dz, beta=DN_ALPHA, name=f"{tag}_dx2")
    x1, proj, cqkv, states, cum, cum_t, yb, lse, u, cc, rsave, ycat, xh2, rs2 = sv["mix"]
    dz, gr["ln_mix_g"], gr["ln_mix_b"] = ln_res_bwd(dx2, xh2, rs2, w["ln_mix_g"], name=f"{tag}_ln_mix_bwd")
    dycat = mm(dz, w["wout"], mode="nt", name=f"{tag}_dycat")
    gr["wout"] = mm(ycat, dz, mode="tn", name=f"{tag}_dwout", tk=512)
    comm_sb, comm_fox, comm_gdn = plan(gr)
    gw = GROUP_WIDTH
    dya, dyb, dyc, dyd = (dycat[:, i * gw:(i + 1) * gw] for i in range(4))
    dq_d, dk_d, dv_d, got_sb = sb_bwd(proj, rsave, dyd, name=f"{tag}_sb_bwd", comm=comm_sb)
    dcc, gr["conf_norm_g"], gr["conf_norm_b"] = gn_silu_bwd(cc, w["conf_norm_g"], w["conf_norm_b"], dyc,
                                                            name=f"{tag}_conf_norm_bwd")
    du, gr["conf_dw_w"], gr["conf_dw_b"] = dwconv_bwd(dcc, u, w["conf_dw_w"], name=f"{tag}_conf_conv_bwd")
    dglu = glu_bwd(proj, du, name=f"{tag}_glu_bwd")
    dq_b, dk_b, dv_b, dcc, dcr, got_fox = fox_bwd(proj, cum, cum_t, yb, lse, dyb, name=f"{tag}_fox_bwd", comm=comm_fox)
    dcum = dcc + jnp.pad(dcr[:, :GROUP_HEADS, :].transpose(0, 2, 1).reshape(t, GROUP_HEADS),
                   ((0, 0), (FOX_COL, LANES - FOX_COL - GROUP_HEADS)))
    dsm_f, dbf = fox_gate_bwd(dcum, proj, w["bf"], name=f"{tag}_fox_gate_bwd")
    gr["fox_b_f"] = dbf[0, FOX_COL:FOX_COL + GROUP_HEADS]
    dcq, dz_a, dsm_a, dng, dal, ddt, got_gdn = gdn_bwd(cqkv, proj, w["alog"], w["dtb"], w["ng"], states, dya,
                                                       name=f"{tag}_gdn_bwd", comm=comm_gdn)
    gr["gdn_norm_g"] = dng.reshape(GROUP_HEADS, HEAD_DIM).sum(0)
    gr["gdn_a_log"] = dal[0, A_COL:A_COL + GROUP_HEADS]
    gr["gdn_dt_bias"] = ddt[0, A_COL:A_COL + GROUP_HEADS]
    dgq, gr["gdn_conv_w"], _ = dwconv_bwd(dcq, proj, w["gdn_conv_w"], col0=P_GDN, name=f"{tag}_gdn_conv_bwd")
    dproj = jnp.concatenate([dgq, dz_a, dq_b, dk_b, dv_b, dglu, dq_d, dk_d, dv_d, dsm_a + dsm_f], axis=1)
    gr["win"] = mm(x1, dproj, mode="tn", name=f"{tag}_dwin", tm=1024, tn=640, tk=512)
    dx1 = mm(dproj, w["win"], mode="nt", add=dz, beta=DN_ALPHA, name=f"{tag}_dx1", tm=1024, tn=1024, tk=640)
    tail = {} if tail is None else dict(tail, comm_dh=tail["comm_dh"](gr))
    dx0, gr["gu1"], gr["d1"], gr["ln_ffn1_g"], gr["ln_ffn1_b"], got_tail = _ffn_bwd(
        dx1, sv["ffn1"], w, 1, f"{tag}_ffn1", **tail)
    return dx0, gr, (got_sb, got_fox, got_gdn), got_tail


SMALL_REPLICATED = ("ln_ffn1_g", "ln_ffn1_b", "gdn_a_log", "gdn_dt_bias", "gdn_norm_g", "fox_b_f", "conf_dw_b",
                    "conf_norm_g", "conf_norm_b", "ln_mix_g", "ln_mix_b", "ln_mem_g", "ln_mem_b", "ln_ffn2_g",
                    "ln_ffn2_b")
SMALL_SHARDED = ("gdn_conv_w", "conf_dw_w")
BIG = ("ffn1_w_gate", "ffn1_w_up", "ffn1_w_down", "w_in", "w_out", "mem_w_q", "mem_w_kv", "mem_w_o",
       "ffn2_w_gate", "ffn2_w_up", "ffn2_w_down")
WEIGHT_ORDER = ("ffn1_w_gate", "ffn1_w_up", "ffn1_w_down", "ln_ffn1_g", "ln_ffn1_b", "w_in", "gdn_conv_w", "gdn_a_log",
                "gdn_dt_bias", "gdn_norm_g", "fox_b_f", "conf_dw_w", "conf_dw_b", "conf_norm_g", "conf_norm_b", "w_out",
                "ln_mix_g", "ln_mix_b", "mem_w_q", "mem_w_kv", "mem_w_o", "ln_mem_g", "ln_mem_b", "ffn2_w_gate",
                "ffn2_w_up", "ffn2_w_down", "ln_ffn2_g", "ln_ffn2_b")


def _step(x, mem, loss_target, wts, ms, vs):
    me = 4 * lax.axis_index("x") + 2 * lax.axis_index("y") + lax.axis_index("c")
    x = x[0]
    mem = mem[0]
    target = loss_target[0]
    rows_s = D_MODEL // N_DEV
    first, rest = ("gu1", "d1", "win"), ("sq", "kv", "gu2", "d2")

    def shards(l):
        c = lambda k: wts[k][l].astype(MXU_DT)
        return dict(gu1=jnp.stack([c("ffn1_w_gate"), c("ffn1_w_up")]), d1=c("ffn1_w_down"),
                    win=_win_to_aligned(wts["w_in"][l]).astype(MXU_DT),
                    sq=jnp.stack([c("w_out"), c("mem_w_q"), c("mem_w_o")]), kv=c("mem_w_kv"),
                    gu2=jnp.stack([c("ffn2_w_gate"), c("ffn2_w_up")]), d2=c("ffn2_w_down"))

    def to_compute_layout(w, keys, got):
        for k, g in zip(keys, got):
            if k in ("gu1", "gu2"):
                w[k] = g.transpose(2, 1, 0, 3).reshape(D_MODEL, 2 * D_FF)
            elif k in ("d1", "d2"):
                w[k] = g.reshape(D_FF, D_MODEL)
            elif k == "win":
                w[k] = g.reshape(D_MODEL, P_WIDTH)
            elif k == "sq":
                full = g.transpose(1, 0, 2, 3).reshape(3, D_MODEL, D_MODEL)
                w["wout"], w["wq"], w["wo"] = full[0], full[1], full[2]
            else:
                w["wkv"] = g.transpose(1, 0, 2).reshape(D_MODEL, 2 * D_MODEL)

    def chunks(gr, keys):
        out = []
        for k in keys:
            if k in ("gu1", "gu2"):
                out.append(gr[k].reshape(D_MODEL, 2, N_DEV, -1).transpose(2, 1, 0, 3))
            elif k in ("d1", "d2"):
                out.append(gr[k].reshape(N_DEV, -1, D_MODEL))
            elif k == "win":
                out.append(gr[k].reshape(N_DEV, rows_s, P_WIDTH))
            elif k == "sq":
                out.append(jnp.stack([gr[n].reshape(N_DEV, rows_s, D_MODEL) for n in ("wout", "wq", "wo")], axis=1))
            else:
                out.append(gr["wkv"].reshape(D_MODEL, N_DEV, -1).transpose(1, 0, 2))
        return out

    sh = [shards(l) for l in range(DEPTH)]
    sm_sh = _pack([wts["gdn_conv_w"], wts["conf_dw_w"]])
    got = exchange([sh[0]["gu1"], sm_sh], broadcast=True, name="gather_first")
    conv_shapes = [wts["gdn_conv_w"].shape, wts["conf_dw_w"].shape]
    parts = [_unpack(got[-1][j], conv_shapes) for j in range(N_DEV)]
    gconv_full = jnp.concatenate([p[0] for p in parts], axis=-1)
    cconv_full = jnp.concatenate([p[1] for p in parts], axis=-1)

    def small_weights(l):
        w = dict(gdn_conv_w=gconv_full[l], conf_dw_w=cconv_full[l],
                 alog=_row128(wts["gdn_a_log"][l], A_COL), dtb=_row128(wts["gdn_dt_bias"][l], A_COL),
                 bf=_row128(wts["fox_b_f"][l], FOX_COL), ng=jnp.tile(wts["gdn_norm_g"][l], GROUP_HEADS)[None, :])
        for k in ("ln_ffn1_g", "ln_ffn1_b", "conf_dw_b", "conf_norm_g", "conf_norm_b", "ln_mix_g", "ln_mix_b",
                  "ln_mem_g", "ln_mem_b", "ln_ffn2_g", "ln_ffn2_b"):
            w[k] = wts[k][l][None, :]
        return w

    lw = [small_weights(l) for l in range(DEPTH)]
    to_compute_layout(lw[0], ("gu1",), got[:-1])

    h, sv0 = _layer_fwd(x, mem, lw[0], "l0",
                        comm_ffn1=Comm([sh[0][k] for k in first[1:]], True),
                        on_ffn1=lambda g: to_compute_layout(lw[0], first[1:], g),
                        comm_gdn=Comm([sh[0][k] for k in rest], True),
                        on_gdn=lambda g: to_compute_layout(lw[0], rest, g),
                        comm_fox=Comm([sh[1][k] for k in first + rest[:2]], True),
                        on_fox=lambda g: to_compute_layout(lw[1], first + rest[:2], g))
    h, sv1 = _layer_fwd(h, mem, lw[1], "l1",
                        comm_gdn=Comm([sh[1][k] for k in rest[2:]], True),
                        on_gdn=lambda g: to_compute_layout(lw[1], rest[2:], g))
    dh, lpart = loss_head(h, target, name="loss_head")

    recv = [{}, {}]
    e_ffn, e_mem = ("gu2", "d2"), ("sq", "kv")
    dh, g1, got, _ = _layer_bwd(dh, mem, sv1, lw[1], "l1",
                                lambda gr: (None, Comm(chunks(gr, e_ffn), False), Comm(chunks(gr, e_mem), False)))
    recv[1].update(zip(e_ffn, got[1]))
    recv[1].update(zip(e_mem, got[2]))
    tail = dict(comm_dh=lambda gr: Comm(chunks(gr, ("win",)), False),
                comm_dwgu=lambda dwd: Comm(chunks({"d1": dwd}, ("d1",)), False),
                comm_dx=lambda dwgu: Comm(chunks({"gu1": dwgu}, ("gu1",)), False))
    dh, g0, got, got_t = _layer_bwd(
        dh, mem, sv0, lw[0], "l0",
        lambda gr: (Comm(chunks(gr, e_mem), False), Comm(chunks(g1, first[:2]), False),
                    Comm(chunks(g1, first[2:]) + chunks(gr, e_ffn), False)), tail)
    recv[0].update(zip(e_mem, got[0]))
    recv[1].update(zip(first[:2], got[1]))
    recv[1].update(win=got[2][0])
    recv[0].update(zip(e_ffn, got[2][1:]))
    recv[0].update(win=got_t[0][0], d1=got_t[1][0], gu1=got_t[2][0])
    grad_x = dh[None]
    grads = [g0, g1]

    def gl(k):
        return jnp.stack([grads[l][k] for l in range(DEPTH)])

    small_names = SMALL_REPLICATED + SMALL_SHARDED
    small_grads = [gl(k) for k in small_names] + [lpart[0, :1]]
    got = exchange([_pack(small_grads)], broadcast=True, name="gather_small_grads")
    sm_sum = slot_sum(got[0], name="sum_small_grads")
    sm_g = _unpack(sm_sum, [g.shape for g in small_grads])
    loss = sm_g[-1][0]
    small_g = dict(zip(small_names, sm_g[:-1]))
    for k in SMALL_SHARDED:
        width = wts[k].shape[-1]
        small_g[k] = lax.dynamic_slice_in_dim(small_g[k], me * width, width, axis=2)

    out_g, out_d, out_m, out_v = {}, {}, {}, {}

    def update(names, key, fix=lambda a: a):
        res = {k: [] for k in names}
        for l in range(DEPTH):
            slots = fix(recv[l][key])
            slots = slots.reshape(N_DEV, -1, slots.shape[-1])
            for i, k in enumerate(names):
                two = lambda a: a[l].reshape(-1, a.shape[-1])
                res[k].append(adamw(two(wts[k]), two(ms[k]), two(vs[k]), slots, row0=i * two(wts[k]).shape[0],
                                    name=f"adamw_{k}_l{l}"))
        for k in names:
            for dst, per_layer in zip((out_g, out_d, out_m, out_v), zip(*res[k])):
                dst[k] = jnp.stack(per_layer).reshape(wts[k].shape)

    update(("ffn1_w_gate", "ffn1_w_up"), "gu1")
    update(("ffn1_w_down",), "d1")
    update(("w_in",), "win", _win_from_aligned)
    update(("w_out", "mem_w_q", "mem_w_o"), "sq")
    update(("mem_w_kv",), "kv")
    update(("ffn2_w_gate", "ffn2_w_up"), "gu2")
    update(("ffn2_w_down",), "d2")

    sw = _pack([wts[k] for k in small_names])
    smm = _pack([ms[k] for k in small_names])
    smv = _pack([vs[k] for k in small_names])
    sg = _pack([small_g[k] for k in small_names])
    res = adamw(sw, smm, smv, sg[None], name="adamw_small")
    shapes = [wts[k].shape for k in small_names]
    for dst, buf in zip((out_g, out_d, out_m, out_v), res):
        for k, a in zip(small_names, _unpack(buf, shapes)):
            dst[k] = a

    return (loss, grad_x, *[out_g[k] for k in WEIGHT_ORDER], *[out_d[k] for k in WEIGHT_ORDER],
            *[out_m[k] for k in WEIGHT_ORDER], *[out_v[k] for k in WEIGHT_ORDER])


def kernel(x, mem, ffn1_w_gate, ffn1_w_up, ffn1_w_down, ln_ffn1_g, ln_ffn1_b, w_in, gdn_conv_w, gdn_a_log, gdn_dt_bias, gdn_norm_g, fox_b_f, conf_dw_w, conf_dw_b, conf_norm_g, conf_norm_b, w_out, ln_mix_g, ln_mix_b, mem_w_q, mem_w_kv, mem_w_o, ln_mem_g, ln_mem_b, ffn2_w_gate, ffn2_w_up, ffn2_w_down, ln_ffn2_g, ln_ffn2_b, loss_target, m_ffn1_w_gate, m_ffn1_w_up, m_ffn1_w_down, m_ln_ffn1_g, m_ln_ffn1_b, m_w_in, m_gdn_conv_w, m_gdn_a_log, m_gdn_dt_bias, m_gdn_norm_g, m_fox_b_f, m_conf_dw_w, m_conf_dw_b, m_conf_norm_g, m_conf_norm_b, m_w_out, m_ln_mix_g, m_ln_mix_b, m_mem_w_q, m_mem_w_kv, m_mem_w_o, m_ln_mem_g, m_ln_mem_b, m_ffn2_w_gate, m_ffn2_w_up, m_ffn2_w_down, m_ln_ffn2_g, m_ln_ffn2_b, v_ffn1_w_gate, v_ffn1_w_up, v_ffn1_w_down, v_ln_ffn1_g, v_ln_ffn1_b, v_w_in, v_gdn_conv_w, v_gdn_a_log, v_gdn_dt_bias, v_gdn_norm_g, v_fox_b_f, v_conf_dw_w, v_conf_dw_b, v_conf_norm_g, v_conf_norm_b, v_w_out, v_ln_mix_g, v_ln_mix_b, v_mem_w_q, v_mem_w_kv, v_mem_w_o, v_ln_mem_g, v_ln_mem_b, v_ffn2_w_gate, v_ffn2_w_up, v_ffn2_w_down, v_ln_ffn2_g, v_ln_ffn2_b):
    args = locals()
    wts = {k: args[k] for k in WEIGHT_ORDER}
    ms = {k: args["m_" + k] for k in WEIGHT_ORDER}
    vs = {k: args["v_" + k] for k in WEIGHT_ORDER}
    return _step(x, mem, loss_target, wts, ms, vs)
```

```python
import functools
import math

import jax
import jax.numpy as jnp
import numpy as np
from jax import lax
from jax.experimental import pallas as pl
from jax.experimental.pallas import tpu as pltpu

F32 = jnp.float32
BF16 = jnp.bfloat16
MXU_DT = jnp.bfloat16
HI = lax.Precision.HIGHEST

N_DEV = 8
VMEM_LIMIT_BYTES = 56 * 1024 * 1024
LANES = 128

D_MODEL = 1024
DEPTH = 2
GROUP_WIDTH = 256
HEAD_DIM = 64
GROUP_HEADS = 4
D_FF = 2816
SHORT_CONV = 4
CONF_KERNEL = 31
CONF_GROUPS = 4
GDN_CHUNK = 64
N_MEM = 256
MEM_HEADS = 4
MEM_HEAD_DIM = 256
DN_ALPHA = float((2 * DEPTH) ** 0.25)
LN_EPS = 1e-5
RMS_EPS = 1e-6
L2_EPS = 1e-6
NEG_BIG = -1e30
IN_SPLITS = (768, 256, 4, 4, 768, 4, 512, 768)
IN_WIDTH = sum(IN_SPLITS)
P_GDN, P_Z, P_FOX, P_CONF, P_SB, P_SMALL = 0, 768, 1024, 1792, 2304, 3072
P_WIDTH = 3200

ADAM_LR = 0.001
ADAM_B1 = 0.9
ADAM_B2 = 0.999
ADAM_EPS = 1e-08
ADAM_WD = 0.01
ADAM_STEP = 10


def _cparams(sem):
    return pltpu.CompilerParams(dimension_semantics=sem, vmem_limit_bytes=VMEM_LIMIT_BYTES)


def _tile(n, pref, align=LANES):
    if n <= pref:
        return n
    t = (pref // align) * align
    while t >= align:
        if n % t == 0:
            return t
        t -= align
    return n


def mm(a, b, *, mode="nn", add=None, alpha=1.0, beta=1.0, out_dtype=F32, name,
       tm=1024, tn=512, tk=1024, comm=None):
    if mode == "nn":
        (m, k), (k2, n) = a.shape, b.shape
    elif mode == "nt":
        (m, k), (n, k2) = a.shape, b.shape
    else:
        (k, m), (k2, n) = a.shape, b.shape
    assert k == k2, (a.shape, b.shape, mode)
    tm = _tile(m, tm, 8 if mode != "tn" else LANES)
    tn = _tile(n, tn)
    tk = _tile(k, tk, LANES if mode != "tn" else 8)
    nk = k // tk
    if mode == "nn":
        a_spec = pl.BlockSpec((tm, tk), lambda i, j, kk: (i, kk))
        b_spec = pl.BlockSpec((tk, tn), lambda i, j, kk: (kk, j))
        dims = (((1,), (0,)), ((), ()))
    elif mode == "nt":
        a_spec = pl.BlockSpec((tm, tk), lambda i, j, kk: (i, kk))
        b_spec = pl.BlockSpec((tn, tk), lambda i, j, kk: (j, kk))
        dims = (((1,), (1,)), ((), ()))
    else:
        a_spec = pl.BlockSpec((tk, tm), lambda i, j, kk: (kk, i))
        b_spec = pl.BlockSpec((tk, tn), lambda i, j, kk: (kk, j))
        dims = (((0,), (0,)), ((), ()))
    o_spec = pl.BlockSpec((tm, tn), lambda i, j, kk: (i, j))
    has_add = add is not None

    def body(*refs):
        if has_add:
            a_ref, b_ref, add_ref, o_ref, acc_ref = refs
        else:
            a_ref, b_ref, o_ref, acc_ref = refs
        kk = pl.program_id(2)

        @pl.when(kk == 0)
        def _():
            acc_ref[...] = jnp.zeros_like(acc_ref)

        acc_ref[...] += lax.dot_general(a_ref[...].astype(MXU_DT), b_ref[...].astype(MXU_DT), dims,
                                        preferred_element_type=F32)

        @pl.when(kk == nk - 1)
        def _():
            r = acc_ref[...]
            if alpha != 1.0:
                r = r * alpha
            if has_add:
                r = r + beta * add_ref[...].astype(F32)
            o_ref[...] = r.astype(out_dtype)

    in_specs = [a_spec, b_spec] + ([o_spec] if has_add else [])
    args = (a, b) + ((add,) if has_add else ())
    grid = (m // tm, n // tn, nk)
    call = dict(name=name, grid=grid, in_specs=in_specs, out_specs=[o_spec],
                out_shape=[jax.ShapeDtypeStruct((m, n), out_dtype)],
                scratch_shapes=[pltpu.VMEM((tm, tn), F32)],
                compiler_params=_cparams(("parallel", "parallel", "arbitrary")))
    (out,), got = carry_comm(call, body, args, comm, 1, *_grid_ends(*grid))
    return out if comm is None else (out, got)


def ln_res_fwd(x, y, g, b, s, *, name):
    t, d = x.shape
    tm = _tile(t, 512, 8)

    def body(x_ref, y_ref, g_ref, b_ref, o_ref, o16_ref, xh_ref, rs_ref):
        z = DN_ALPHA * x_ref[...] + s * y_ref[...]
        mu = jnp.mean(z, axis=-1, keepdims=True)
        zc = z - mu
        var = jnp.mean(zc * zc, axis=-1, keepdims=True)
        rstd = lax.rsqrt(var + LN_EPS)
        xh = zc * rstd
        xh_ref[...] = xh
        rs_ref[...] = rstd
        out = xh * g_ref[...] + b_ref[...]
        o_ref[...] = out
        o16_ref[...] = out.astype(o16_ref.dtype)

    row = pl.BlockSpec((tm, d), lambda i: (i, 0))
    vec = pl.BlockSpec((1, d), lambda i: (0, 0))
    return pl.pallas_call(
        body, name=name, grid=(t // tm,),
        in_specs=[row, row, vec, vec],
        out_specs=[row, row, row, pl.BlockSpec((tm, 1), lambda i: (i, 0))],
        out_shape=[jax.ShapeDtypeStruct((t, d), F32), jax.ShapeDtypeStruct((t, d), MXU_DT),
                   jax.ShapeDtypeStruct((t, d), F32), jax.ShapeDtypeStruct((t, 1), F32)],
        compiler_params=_cparams(("parallel",)),
    )(x, y, g, b)


def ln_res_bwd(dout, xhat, rstd, g, *, name):
    t, d = dout.shape
    tm = _tile(t, 512, 8)

    def body(do_ref, xh_ref, rs_ref, g_ref, dz_ref, dg_ref, db_ref):
        i = pl.program_id(0)

        @pl.when(i == 0)
        def _():
            dg_ref[...] = jnp.zeros_like(dg_ref)
            db_ref[...] = jnp.zeros_like(db_ref)

        do = do_ref[...]
        xh = xh_ref[...]
        dxh = do * g_ref[...]
        m1 = jnp.mean(dxh, axis=-1, keepdims=True)
        m2 = jnp.mean(dxh * xh, axis=-1, keepdims=True)
        dz_ref[...] = rs_ref[...] * (dxh - m1 - xh * m2)
        dg_ref[...] += jnp.sum(do * xh, axis=0, keepdims=True)
        db_ref[...] += jnp.sum(do, axis=0, keepdims=True)

    row = pl.BlockSpec((tm, d), lambda i: (i, 0))
    vec = pl.BlockSpec((1, d), lambda i: (0, 0))
    return pl.pallas_call(
        body, name=name, grid=(t // tm,),
        in_specs=[row, row, pl.BlockSpec((tm, 1), lambda i: (i, 0)), vec],
        out_specs=[row, vec, vec],
        out_shape=[jax.ShapeDtypeStruct((t, d), F32), jax.ShapeDtypeStruct((1, d), F32),
                   jax.ShapeDtypeStruct((1, d), F32)],
        compiler_params=_cparams(("arbitrary",)),
    )(dout, xhat, rstd, g)


def _sigmoid(x):
    return 1.0 / (1.0 + jnp.exp(-x))


def act_fwd(gu, *, name):
    t, f2 = gu.shape
    f = f2 // 2
    tm = _tile(t, 256, 8)

    def body(gu_ref, h_ref):
        g = gu_ref[:, :f].astype(F32)
        h_ref[...] = (g * _sigmoid(g) * gu_ref[:, f:].astype(F32)).astype(h_ref.dtype)

    return pl.pallas_call(
        body, name=name, grid=(t // tm,),
        in_specs=[pl.BlockSpec((tm, f2), lambda i: (i, 0))],
        out_specs=pl.BlockSpec((tm, f), lambda i: (i, 0)),
        out_shape=jax.ShapeDtypeStruct((t, f), MXU_DT),
        compiler_params=_cparams(("parallel",)),
    )(gu)


def act_bwd(gu, dh, *, name):
    t, f2 = gu.shape
    f = f2 // 2
    tm = _tile(t, 256, 8)

    def body(gu_ref, dh_ref, o_ref):
        g = gu_ref[:, :f].astype(F32)
        u = gu_ref[:, f:].astype(F32)
        dh = dh_ref[...]
        sg = _sigmoid(g)
        o_ref[:, f:] = (dh * g * sg).astype(o_ref.dtype)
        o_ref[:, :f] = (dh * u * sg * (1.0 + g * (1.0 - sg))).astype(o_ref.dtype)

    return pl.pallas_call(
        body, name=name, grid=(t // tm,),
        in_specs=[pl.BlockSpec((tm, f2), lambda i: (i, 0)), pl.BlockSpec((tm, f), lambda i: (i, 0))],
        out_specs=pl.BlockSpec((tm, f2), lambda i: (i, 0)),
        out_shape=jax.ShapeDtypeStruct((t, f2), MXU_DT),
        compiler_params=_cparams(("parallel",)),
    )(gu, dh)


def loss_head(y, target, *, name):
    t, d = y.shape
    tm = _tile(t, 512, 8)

    def body(y_ref, t_ref, dy_ref, l_ref):
        i = pl.program_id(0)

        @pl.when(i == 0)
        def _():
            l_ref[...] = jnp.zeros_like(l_ref)

        err = y_ref[...] - t_ref[...]
        dy_ref[...] = err * (1.0 / d)
        part = jnp.sum(jnp.sum(err * err, axis=-1, keepdims=True), axis=0, keepdims=True)
        l_ref[...] += jnp.broadcast_to(part * (0.5 / d), l_ref.shape)

    row = pl.BlockSpec((tm, d), lambda i: (i, 0))
    return pl.pallas_call(
        body, name=name, grid=(t // tm,),
        in_specs=[row, row],
        out_specs=[row, pl.BlockSpec((1, LANES), lambda i: (0, 0))],
        out_shape=[jax.ShapeDtypeStruct((t, d), F32), jax.ShapeDtypeStruct((1, LANES), F32)],
        compiler_params=_cparams(("arbitrary",)),
    )(y, target)


def _dot(a, b):
    return lax.dot_general(a, b, (((1,), (0,)), ((), ())), preferred_element_type=F32)


def _dot_nt(a, b):
    return lax.dot_general(a, b, (((1,), (1,)), ((), ())), preferred_element_type=F32)


def _dot_tn(a, b):
    return lax.dot_general(a, b, (((0,), (0,)), ((), ())), preferred_element_type=F32)


def _dot_hi(a, b):
    return lax.dot_general(a, b, (((1,), (0,)), ((), ())), preferred_element_type=F32, precision=HI)


def _dot_nt_hi(a, b):
    return lax.dot_general(a, b, (((1,), (1,)), ((), ())), preferred_element_type=F32, precision=HI)


def _split_dot(x, u):
    hi = x.astype(MXU_DT)
    lo = (x - hi.astype(F32)).astype(MXU_DT)
    return _dot(hi, u) + _dot(lo, u)


def _mem_probs(q_ref, kv_ref, h):
    lo = h * MEM_HEAD_DIM
    qh = q_ref[:, lo:lo + MEM_HEAD_DIM].astype(MXU_DT)
    kh = kv_ref[:, lo:lo + MEM_HEAD_DIM].astype(MXU_DT)
    s = _dot_nt(qh, kh) * (MEM_HEAD_DIM ** -0.5)
    s = s - jnp.max(s, axis=-1, keepdims=True)
    p = jnp.exp(s)
    return p / jnp.sum(p, axis=-1, keepdims=True), qh, kh


def memattn_fwd(q, kv, *, name):
    t, d = q.shape
    tm = _tile(t, 512, 8)

    def body(q_ref, kv_ref, o_ref):
        for h in range(MEM_HEADS):
            lo = h * MEM_HEAD_DIM
            p, _, _ = _mem_probs(q_ref, kv_ref, h)
            vh = kv_ref[:, d + lo:d + lo + MEM_HEAD_DIM].astype(MXU_DT)
            o_ref[:, lo:lo + MEM_HEAD_DIM] = _dot(p.astype(MXU_DT), vh).astype(o_ref.dtype)

    return pl.pallas_call(
        body, name=name, grid=(t // tm,),
        in_specs=[pl.BlockSpec((tm, d), lambda i: (i, 0)), pl.BlockSpec(kv.shape, lambda i: (0, 0))],
        out_specs=pl.BlockSpec((tm, d), lambda i: (i, 0)),
        out_shape=jax.ShapeDtypeStruct((t, d), MXU_DT),
        compiler_params=_cparams(("parallel",)),
    )(q, kv)


def memattn_bwd(q, kv, datt, *, name):
    t, d = q.shape
    tm = _tile(t, 512, 8)
    scale = MEM_HEAD_DIM ** -0.5

    def body(q_ref, kv_ref, da_ref, dq_ref, dkv_ref):
        @pl.when(pl.program_id(0) == 0)
        def _():
            dkv_ref[...] = jnp.zeros_like(dkv_ref)

        for h in range(MEM_HEADS):
            lo = h * MEM_HEAD_DIM
            p, qh, kh = _mem_probs(q_ref, kv_ref, h)
            vh = kv_ref[:, d + lo:d + lo + MEM_HEAD_DIM].astype(MXU_DT)
            da = da_ref[:, lo:lo + MEM_HEAD_DIM].astype(MXU_DT)
            dp = _dot_nt(da, vh)
            ds = p * (dp - jnp.sum(dp * p, axis=-1, keepdims=True))
            dsb = ds.astype(MXU_DT)
            dq_ref[:, lo:lo + MEM_HEAD_DIM] = (_dot(dsb, kh) * scale).astype(dq_ref.dtype)
            dkv_ref[:, lo:lo + MEM_HEAD_DIM] += _dot_tn(dsb, qh) * scale
            dkv_ref[:, d + lo:d + lo + MEM_HEAD_DIM] += _dot_tn(p.astype(MXU_DT), da)

    row = pl.BlockSpec((tm, d), lambda i: (i, 0))
    full = pl.BlockSpec(kv.shape, lambda i: (0, 0))
    return pl.pallas_call(
        body, name=name, grid=(t // tm,),
        in_specs=[row, full, row],
        out_specs=[row, full],
        out_shape=[jax.ShapeDtypeStruct((t, d), MXU_DT), jax.ShapeDtypeStruct(kv.shape, F32)],
        compiler_params=_cparams(("arbitrary",)),
    )(q, kv, datt)


def _halo(k):
    return 8 * ((k - 1 + 7) // 8)


def dwconv_fwd(u, w, bias, *, col0=0, width=None, name):
    t = u.shape[0]
    kk, c = w.shape
    width = c if width is None else width
    assert width == c and col0 % c == 0
    cb = col0 // c
    hb = _halo(kk)
    tm = _tile(t, 512, hb)
    r = tm // hb
    has_bias = bias is not None

    def body(*refs):
        if has_bias:
            prev_ref, cur_ref, w_ref, b_ref, o_ref, scr = refs
        else:
            prev_ref, cur_ref, w_ref, o_ref, scr = refs
        i = pl.program_id(0)
        scr[0:hb, :] = jnp.where(i == 0, 0.0, prev_ref[...])
        scr[hb:hb + tm, :] = cur_ref[...]
        acc = jnp.zeros((tm, c), F32)
        for k in range(kk):
            acc = acc + w_ref[k:k + 1, :] * scr[pl.ds(hb - (kk - 1) + k, tm), :]
        if has_bias:
            acc = acc + b_ref[...]
        o_ref[...] = acc

    in_specs = [pl.BlockSpec((hb, c), lambda i: (jnp.maximum(i * r - 1, 0), cb)),
                pl.BlockSpec((tm, c), lambda i: (i, cb)),
                pl.BlockSpec((kk, c), lambda i: (0, 0))]
    args = [u, u, w]
    if has_bias:
        in_specs.append(pl.BlockSpec((1, c), lambda i: (0, 0)))
        args.append(bias)
    return pl.pallas_call(
        body, name=name, grid=(t // tm,),
        in_specs=in_specs,
        out_specs=pl.BlockSpec((tm, c), lambda i: (i, 0)),
        out_shape=jax.ShapeDtypeStruct((t, c), F32),
        scratch_shapes=[pltpu.VMEM((hb + tm, c), F32)],
        compiler_params=_cparams(("parallel",)),
    )(*args)


def dwconv_bwd(dc, u, w, *, col0=0, name):
    t, c = dc.shape
    kk = w.shape[0]
    assert col0 % c == 0
    cb = col0 // c
    hb = _halo(kk)
    tm = _tile(t, 512, hb)
    r = tm // hb
    n = t // tm

    def body(dcur_ref, dnext_ref, uprev_ref, ucur_ref, w_ref, du_ref, dw_ref, db_ref, sd, su):
        i = pl.program_id(0)

        @pl.when(i == 0)
        def _():
            dw_ref[...] = jnp.zeros_like(dw_ref)
            db_ref[...] = jnp.zeros_like(db_ref)

        dcur = dcur_ref[...]
        sd[0:tm, :] = dcur
        sd[tm:tm + hb, :] = jnp.where(i == n - 1, 0.0, dnext_ref[...])
        su[0:hb, :] = jnp.where(i == 0, 0.0, uprev_ref[...])
        su[hb:hb + tm, :] = ucur_ref[...]
        acc = jnp.zeros((tm, c), F32)
        for k in range(kk):
            acc = acc + w_ref[k:k + 1, :] * sd[pl.ds(kk - 1 - k, tm), :]
            dw_ref[k:k + 1, :] += jnp.sum(dcur * su[pl.ds(hb - (kk - 1) + k, tm), :], axis=0, keepdims=True)
        du_ref[...] = acc
        db_ref[...] += jnp.sum(dcur, axis=0, keepdims=True)

    return pl.pallas_call(
        body, name=name, grid=(n,),
        in_specs=[pl.BlockSpec((tm, c), lambda i: (i, 0)),
                  pl.BlockSpec((hb, c), lambda i: (jnp.minimum((i + 1) * r, n * r - 1), 0)),
                  pl.BlockSpec((hb, c), lambda i: (jnp.maximum(i * r - 1, 0), cb)),
                  pl.BlockSpec((tm, c), lambda i: (i, cb)),
                  pl.BlockSpec((kk, c), lambda i: (0, 0))],
        out_specs=[pl.BlockSpec((tm, c), lambda i: (i, 0)),
                   pl.BlockSpec((kk, c), lambda i: (0, 0)),
                   pl.BlockSpec((1, c), lambda i: (0, 0))],
        out_shape=[jax.ShapeDtypeStruct((t, c), F32), jax.ShapeDtypeStruct((kk, c), F32),
                   jax.ShapeDtypeStruct((1, c), F32)],
        scratch_shapes=[pltpu.VMEM((tm + hb, c), F32), pltpu.VMEM((hb + tm, c), F32)],
        compiler_params=_cparams(("arbitrary",)),
    )(dc, dc, u, u, w)


def glu_fwd(proj, *, name):
    t = proj.shape[0]
    c = GROUP_WIDTH
    tm = _tile(t, 1024, 8)
    vb, gb = P_CONF // c, P_CONF // c + 1

    def body(v_ref, g_ref, o_ref):
        o_ref[...] = v_ref[...] * _sigmoid(g_ref[...])

    return pl.pallas_call(
        body, name=name, grid=(t // tm,),
        in_specs=[pl.BlockSpec((tm, c), lambda i: (i, vb)), pl.BlockSpec((tm, c), lambda i: (i, gb))],
        out_specs=pl.BlockSpec((tm, c), lambda i: (i, 0)),
        out_shape=jax.ShapeDtypeStruct((t, c), F32),
        compiler_params=_cparams(("parallel",)),
    )(proj, proj)


def glu_bwd(proj, du, *, name):
    t = proj.shape[0]
    c = GROUP_WIDTH
    tm = _tile(t, 1024, 8)
    vb, gb = P_CONF // c, P_CONF // c + 1

    def body(v_ref, g_ref, du_ref, o_ref):
        sg = _sigmoid(g_ref[...])
        du = du_ref[...]
        o_ref[:, :c] = du * sg
        o_ref[:, c:] = du * v_ref[...] * sg * (1.0 - sg)

    return pl.pallas_call(
        body, name=name, grid=(t // tm,),
        in_specs=[pl.BlockSpec((tm, c), lambda i: (i, vb)), pl.BlockSpec((tm, c), lambda i: (i, gb)),
                  pl.BlockSpec((tm, c), lambda i: (i, 0))],
        out_specs=pl.BlockSpec((tm, 2 * c), lambda i: (i, 0)),
        out_shape=jax.ShapeDtypeStruct((t, 2 * c), F32),
        compiler_params=_cparams(("parallel",)),
    )(proj, proj, du)


def _group_mean_matrix(c, groups):
    gsz = c // groups
    ri = lax.broadcasted_iota(jnp.int32, (c, c), 0) // gsz
    ci = lax.broadcasted_iota(jnp.int32, (c, c), 1) // gsz
    return jnp.where(ri == ci, 1.0 / gsz, 0.0).astype(F32)


def gn_silu_fwd(cx, gamma, beta, *, name):
    t, c = cx.shape
    tm = _tile(t, 1024, 8)

    def body(c_ref, g_ref, b_ref, o_ref):
        gm = _group_mean_matrix(c, CONF_GROUPS)
        x = c_ref[...]
        mu = _dot_hi(x, gm)
        xc = x - mu
        var = _dot_hi(xc * xc, gm)
        a = xc * lax.rsqrt(var + LN_EPS) * g_ref[...] + b_ref[...]
        o_ref[...] = a * _sigmoid(a)

    row = pl.BlockSpec((tm, c), lambda i: (i, 0))
    vec = pl.BlockSpec((1, c), lambda i: (0, 0))
    return pl.pallas_call(
        body, name=name, grid=(t // tm,),
        in_specs=[row, vec, vec], out_specs=row,
        out_shape=jax.ShapeDtypeStruct((t, c), F32),
        compiler_params=_cparams(("parallel",)),
    )(cx, gamma, beta)


def gn_silu_bwd(cx, gamma, beta, dy, *, name):
    t, c = cx.shape
    tm = _tile(t, 1024, 8)

    def body(c_ref, g_ref, b_ref, dy_ref, dc_ref, dg_ref, db_ref):
        @pl.when(pl.program_id(0) == 0)
        def _():
            dg_ref[...] = jnp.zeros_like(dg_ref)
            db_ref[...] = jnp.zeros_like(db_ref)

        gm = _group_mean_matrix(c, CONF_GROUPS)
        x = c_ref[...]
        mu = _dot_hi(x, gm)
        xc = x - mu
        var = _dot_hi(xc * xc, gm)
        rstd = lax.rsqrt(var + LN_EPS)
        nrm = xc * rstd
        a = nrm * g_ref[...] + b_ref[...]
        sa = _sigmoid(a)
        da = dy_ref[...] * sa * (1.0 + a * (1.0 - sa))
        dg_ref[...] += jnp.sum(da * nrm, axis=0, keepdims=True)
        db_ref[...] += jnp.sum(da, axis=0, keepdims=True)
        dn = da * g_ref[...]
        dc_ref[...] = rstd * (dn - _dot_hi(dn, gm) - nrm * _dot_hi(dn * nrm, gm))

    row = pl.BlockSpec((tm, c), lambda i: (i, 0))
    vec = pl.BlockSpec((1, c), lambda i: (0, 0))
    return pl.pallas_call(
        body, name=name, grid=(t // tm,),
        in_specs=[row, vec, vec, row], out_specs=[row, vec, vec],
        out_shape=[jax.ShapeDtypeStruct((t, c), F32), jax.ShapeDtypeStruct((1, c), F32),
                   jax.ShapeDtypeStruct((1, c), F32)],
        compiler_params=_cparams(("arbitrary",)),
    )(cx, gamma, beta, dy)


FOX_COL = 8
SMALL_BLK = P_SMALL // LANES


def _log_sigmoid(x):
    return jnp.minimum(x, 0.0) - jnp.log(1.0 + jnp.exp(-jnp.abs(x)))


def _fox_cols(shape):
    col = lax.broadcasted_iota(jnp.int32, shape, 1)
    return (col >= FOX_COL) & (col < FOX_COL + GROUP_HEADS)


def fox_gate_fwd(proj, bvec, *, name):
    t = proj.shape[0]
    tm = _tile(t, 256, 8)

    def body(s_ref, b_ref, o_ref, carry):
        @pl.when(pl.program_id(0) == 0)
        def _():
            carry[...] = jnp.zeros_like(carry)

        lf = jnp.where(_fox_cols((tm, LANES)), _log_sigmoid(s_ref[...] + b_ref[...]), 0.0)
        ri = lax.broadcasted_iota(jnp.int32, (tm, tm), 0)
        ci = lax.broadcasted_iota(jnp.int32, (tm, tm), 1)
        cum = _dot_hi(jnp.where(ri >= ci, 1.0, 0.0).astype(F32), lf) + carry[...]
        o_ref[...] = cum
        carry[...] = cum[tm - 1:tm, :]

    return pl.pallas_call(
        body, name=name, grid=(t // tm,),
        in_specs=[pl.BlockSpec((tm, LANES), lambda i: (i, SMALL_BLK)), pl.BlockSpec((1, LANES), lambda i: (0, 0))],
        out_specs=pl.BlockSpec((tm, LANES), lambda i: (i, 0)),
        out_shape=jax.ShapeDtypeStruct((t, LANES), F32),
        scratch_shapes=[pltpu.VMEM((1, LANES), F32)],
        compiler_params=_cparams(("arbitrary",)),
    )(proj, bvec)


def fox_gate_bwd(dcum, proj, bvec, *, name):
    t = proj.shape[0]
    tm = _tile(t, 256, 8)
    n = t // tm

    def body(d_ref, s_ref, b_ref, o_ref, db_ref, carry):
        @pl.when(pl.program_id(0) == 0)
        def _():
            carry[...] = jnp.zeros_like(carry)
            db_ref[...] = jnp.zeros_like(db_ref)

        ri = lax.broadcasted_iota(jnp.int32, (tm, tm), 0)
        ci = lax.broadcasted_iota(jnp.int32, (tm, tm), 1)
        dlf = _dot_hi(jnp.where(ri <= ci, 1.0, 0.0).astype(F32), d_ref[...]) + carry[...]
        carry[...] = dlf[0:1, :]
        x = s_ref[...] + b_ref[...]
        dx = jnp.where(_fox_cols((tm, LANES)), dlf * (1.0 - _sigmoid(x)), 0.0)
        o_ref[...] = dx
        db_ref[...] += jnp.sum(dx, axis=0, keepdims=True)

    return pl.pallas_call(
        body, name=name, grid=(n,),
        in_specs=[pl.BlockSpec((tm, LANES), lambda i: (n - 1 - i, 0)),
                  pl.BlockSpec((tm, LANES), lambda i: (n - 1 - i, SMALL_BLK)),
                  pl.BlockSpec((1, LANES), lambda i: (0, 0))],
        out_specs=[pl.BlockSpec((tm, LANES), lambda i: (n - 1 - i, 0)), pl.BlockSpec((1, LANES), lambda i: (0, 0))],
        out_shape=[jax.ShapeDtypeStruct((t, LANES), F32), jax.ShapeDtypeStruct((1, LANES), F32)],
        scratch_shapes=[pltpu.VMEM((1, LANES), F32)],
        compiler_params=_cparams(("arbitrary",)),
    )(dcum, proj, bvec)


def _head_masks(c):
    lane_head = lax.broadcasted_iota(jnp.int32, (1, c), 1) // HEAD_DIM
    return [lane_head == h for h in range(GROUP_HEADS)]


def _attn_tiles(t, tq, tk):
    tq = _tile(t, tq, 8)
    tk = _tile(t, tk, LANES)
    return tq, tk, t // tq, t // tk


def _grid_ends(*sizes):
    first = lambda: functools.reduce(lambda a, b: a & b, [pl.program_id(d) == 0 for d in range(len(sizes))])
    last = lambda: functools.reduce(lambda a, b: a & b, [pl.program_id(d) == s - 1 for d, s in enumerate(sizes)])
    return first, last


EXP_DEAD = -110.0


def _key_norm_max(k_ref, nk, tk, masks):
    lane = lax.broadcasted_iota(jnp.int32, (1, LANES), 1)

    def one(jt, km):
        kb = k_ref[pl.ds(pl.multiple_of(jt * tk, tk), tk), :].astype(MXU_DT).astype(F32)
        sq = kb * kb
        for h in range(GROUP_HEADS):
            top = jnp.max(jnp.sum(jnp.where(masks[h], sq, 0.0), axis=-1, keepdims=True))
            km = jnp.where(lane == h, jnp.maximum(km, top), km)
        return km

    return lax.fori_loop(0, nk, one, jnp.zeros((1, LANES), F32))


def _fox_reach(qh, km, cc_ref, scale):
    out = []
    for h in range(GROUP_HEADS):
        qf = qh[h].astype(F32)
        qn = jnp.sqrt(jnp.sum(qf * qf, axis=-1, keepdims=True))
        out.append(scale * 1.001 * qn * jnp.sqrt(km[:, h:h + 1]) + cc_ref[:, FOX_COL + h:FOX_COL + h + 1])
    return out


def _fox_alive(reach, top, cr_ref, j, tk):
    ends = cr_ref[jnp.maximum(j, 0)][:, tk - 1:tk]
    worst = jnp.float32(NEG_BIG)
    for h in range(GROUP_HEADS):
        worst = jnp.maximum(worst, jnp.max(reach[h] - top[h]) - jnp.max(ends[h:h + 1, :]))
    return (worst > EXP_DEAD).astype(jnp.int32)


def fox_fwd(proj, cum, cum_t, *, name, tq=512, tk=512, comm=None):
    t = proj.shape[0]
    c = GROUP_WIDTH
    tq, tk, nq, nk = _attn_tiles(t, tq, tk)
    assert tq == tk
    qb = P_FOX // c
    scale = HEAD_DIM ** -0.5
    cr3 = cum_t.reshape(8, nk, tk).transpose(1, 0, 2)

    def body(q_ref, k_ref, v_ref, cc_ref, cr_ref, o_ref, lse_ref, m_scr, l_scr, acc_scr, km_scr):
        i = pl.program_id(0)
        masks = _head_masks(c)

        @pl.when(i == 0)
        def _():
            km_scr[...] = _key_norm_max(k_ref, nk, tk, masks)

        q = q_ref[...]
        qh = [jnp.where(masks[h], q, 0.0).astype(MXU_DT) for h in range(GROUP_HEADS)]
        reach = _fox_reach(qh, km_scr[...], cc_ref, scale)
        m_scr[...] = jnp.full_like(m_scr, NEG_BIG)
        l_scr[...] = jnp.zeros_like(l_scr)
        acc_scr[...] = jnp.zeros_like(acc_scr)

        def tile(j, diagonal):
            rows = pl.ds(pl.multiple_of(j * tk, tk), tk)
            kb = k_ref[rows, :].astype(MXU_DT)
            vb = v_ref[rows, :].astype(MXU_DT)
            crj = cr_ref[j]
            if diagonal:
                causal = (lax.broadcasted_iota(jnp.int32, (tq, tk), 1) <= lax.broadcasted_iota(jnp.int32, (tq, tk), 0))
            acc = acc_scr[...]
            for h in range(GROUP_HEADS):
                s = _dot_nt(qh[h], kb) * scale + (cc_ref[:, FOX_COL + h:FOX_COL + h + 1] - crj[h:h + 1, :])
                if diagonal:
                    s = jnp.where(causal, s, NEG_BIG)
                m_old = m_scr[h]
                m_new = jnp.maximum(m_old, jnp.max(s, axis=-1, keepdims=True))
                p = jnp.exp(s - m_new)
                alpha = jnp.exp(m_old - m_new)
                l_scr[h] = alpha * l_scr[h] + jnp.sum(p, axis=-1, keepdims=True)
                m_scr[h] = m_new
                acc = jnp.where(masks[h], alpha * acc + _dot(p.astype(MXU_DT), vb), acc)
            acc_scr[...] = acc

        def alive(j):
            return _fox_alive(reach, [m_scr[h] for h in range(GROUP_HEADS)], cr_ref, j, tk)

        def step(state):
            j = i - state[0]
            tile(j, False)
            return state[0] + 1, alive(j - 1)

        tile(i, True)
        lax.while_loop(lambda s: (s[0] <= i) & (s[1] > 0), step, (jnp.int32(1), alive(i - 1)))
        acc = acc_scr[...]
        o = jnp.zeros_like(acc)
        lse = jnp.zeros((tq, LANES), F32)
        lane = lax.broadcasted_iota(jnp.int32, (1, LANES), 1)
        for h in range(GROUP_HEADS):
            o = jnp.where(masks[h], acc / l_scr[h], o)
            lse = jnp.where(lane == h, m_scr[h] + jnp.log(l_scr[h]), lse)
        o_ref[...] = o
        lse_ref[...] = lse

    resident = lambda blk: pl.BlockSpec((t, c), lambda i: (0, blk), pipeline_mode=pl.Buffered(1))
    call = dict(
        name=name, grid=(nq,),
        in_specs=[pl.BlockSpec((tq, c), lambda i: (i, qb)), resident(qb + 1), resident(qb + 2),
                  pl.BlockSpec((tq, LANES), lambda i: (i, 0)),
                  pl.BlockSpec((nk, 8, tk), lambda i: (0, 0, 0), pipeline_mode=pl.Buffered(1))],
        out_specs=[pl.BlockSpec((tq, c), lambda i: (i, 0)), pl.BlockSpec((tq, LANES), lambda i: (i, 0))],
        out_shape=[jax.ShapeDtypeStruct((t, c), F32), jax.ShapeDtypeStruct((t, LANES), F32)],
        scratch_shapes=[pltpu.VMEM((GROUP_HEADS, tq, 1), F32), pltpu.VMEM((GROUP_HEADS, tq, 1), F32),
                        pltpu.VMEM((tq, c), F32), pltpu.VMEM((1, LANES), F32)],
        compiler_params=_cparams(("arbitrary",)),
    )
    outs, got = carry_comm(call, body, (proj, proj, proj, cum, cr3), comm, 2, *_grid_ends(nq))
    return (*outs, got)


def fox_bwd(proj, cum, cum_t, o, lse, do, *, name, tq=512, tk=512, comm=None):
    t = proj.shape[0]
    c = GROUP_WIDTH
    tq, tk, nq, nk = _attn_tiles(t, tq, tk)
    assert tq == tk
    qb = P_FOX // c
    scale = HEAD_DIM ** -0.5
    cr3 = cum_t.reshape(8, nk, tk).transpose(1, 0, 2)

    def body(q_ref, k_ref, v_ref, cc_ref, cr_ref, o_ref, lse_ref, do_ref,
             dq_ref, dk_hbm, dv_hbm, dcc_ref, dcr_ref, dq_scr, rs_scr, dk_scr, dv_scr, km_scr):
        i = pl.program_id(0)
        masks = _head_masks(c)

        @pl.when(i == 0)
        def _():
            dk_scr[...] = jnp.zeros_like(dk_scr)
            dv_scr[...] = jnp.zeros_like(dv_scr)
            dcr_ref[...] = jnp.zeros_like(dcr_ref)
            km_scr[...] = _key_norm_max(k_ref, nk, tk, masks)

        q = q_ref[...]
        qf = q.astype(MXU_DT)
        qh = [jnp.where(masks[h], q, 0.0).astype(MXU_DT) for h in range(GROUP_HEADS)]
        do = do_ref[...]
        dob = do.astype(MXU_DT)
        doh = [jnp.where(masks[h], do, 0.0).astype(MXU_DT) for h in range(GROUP_HEADS)]
        doo = do * o_ref[...]
        delta = [jnp.sum(jnp.where(masks[h], doo, 0.0), axis=-1, keepdims=True) for h in range(GROUP_HEADS)]
        lse_h = [lse_ref[:, h:h + 1] for h in range(GROUP_HEADS)]
        reach = _fox_reach(qh, km_scr[...], cc_ref, scale)
        dq_scr[...] = jnp.zeros_like(dq_scr)
        rs_scr[...] = jnp.zeros_like(rs_scr)

        def tile(j, diagonal):
            rows = pl.ds(pl.multiple_of(j * tk, tk), tk)
            kb = k_ref[rows, :].astype(MXU_DT)
            vb = v_ref[rows, :].astype(MXU_DT)
            crj = cr_ref[j]
            if diagonal:
                causal = (lax.broadcasted_iota(jnp.int32, (tq, tk), 1) <= lax.broadcasted_iota(jnp.int32, (tq, tk), 0))
            dq = dq_scr[...]
            dk_upd = jnp.zeros((tk, c), F32)
            dv_upd = jnp.zeros((tk, c), F32)
            for h in range(GROUP_HEADS):
                s = _dot_nt(qh[h], kb) * scale + (cc_ref[:, FOX_COL + h:FOX_COL + h + 1] - crj[h:h + 1, :])
                p = jnp.exp(s - lse_h[h])
                if diagonal:
                    p = jnp.where(causal, p, 0.0)
                ds = p * (_dot_nt(doh[h], vb) - delta[h])
                dsb = ds.astype(MXU_DT)
                dq = jnp.where(masks[h], dq + _dot(dsb, kb) * scale, dq)
                dk_upd = jnp.where(masks[h], _dot_tn(dsb, qf) * scale, dk_upd)
                dv_upd = jnp.where(masks[h], _dot_tn(p.astype(MXU_DT), dob), dv_upd)
                dcr_ref[j, h:h + 1, :] += -jnp.sum(ds, axis=0, keepdims=True)
                rs_scr[h] += jnp.sum(ds, axis=-1, keepdims=True)
            dq_scr[...] = dq
            dk_scr[rows, :] += dk_upd
            dv_scr[rows, :] += dv_upd

        def alive(j):
            return _fox_alive(reach, lse_h, cr_ref, j, tk)

        def step(state):
            j = i - state[0]
            tile(j, False)
            return state[0] + 1, alive(j - 1)

        tile(i, True)
        lax.while_loop(lambda s: (s[0] <= i) & (s[1] > 0), step, (jnp.int32(1), alive(i - 1)))
        dq_ref[...] = dq_scr[...]
        lane = lax.broadcasted_iota(jnp.int32, (1, LANES), 1)
        dcc = jnp.zeros((tq, LANES), F32)
        for h in range(GROUP_HEADS):
            dcc = jnp.where(lane == FOX_COL + h, rs_scr[h], dcc)
        dcc_ref[...] = dcc

        @pl.when(i == nq - 1)
        def _():
            pltpu.sync_copy(dk_scr, dk_hbm)
            pltpu.sync_copy(dv_scr, dv_hbm)

    qrow = lambda i: (i, 0)
    resident = lambda blk: pl.BlockSpec((t, c), lambda i: (0, blk), pipeline_mode=pl.Buffered(1))
    hbm = pl.BlockSpec(memory_space=pl.ANY)
    call = dict(
        name=name, grid=(nq,),
        in_specs=[pl.BlockSpec((tq, c), lambda i: (i, qb)), resident(qb + 1), resident(qb + 2),
                  pl.BlockSpec((tq, LANES), qrow),
                  pl.BlockSpec((nk, 8, tk), lambda i: (0, 0, 0), pipeline_mode=pl.Buffered(1)),
                  pl.BlockSpec((tq, c), qrow), pl.BlockSpec((tq, LANES), qrow), pl.BlockSpec((tq, c), qrow)],
        out_specs=[pl.BlockSpec((tq, c), qrow), hbm, hbm, pl.BlockSpec((tq, LANES), qrow),
                   pl.BlockSpec((nk, 8, tk), lambda i: (0, 0, 0))],
        out_shape=[jax.ShapeDtypeStruct((t, c), F32), jax.ShapeDtypeStruct((t, c), F32),
                   jax.ShapeDtypeStruct((t, c), F32), jax.ShapeDtypeStruct((t, LANES), F32),
                   jax.ShapeDtypeStruct((nk, 8, tk), F32)],
        scratch_shapes=[pltpu.VMEM((tq, c), F32), pltpu.VMEM((GROUP_HEADS, tq, 1), F32),
                        pltpu.VMEM((t, c), F32), pltpu.VMEM((t, c), F32), pltpu.VMEM((1, LANES), F32)],
        compiler_params=_cparams(("arbitrary",)),
    )
    outs, got = carry_comm(call, body, (proj, proj, proj, cum, cr3, o, lse, do), comm, 5, *_grid_ends(nq))
    return (*outs, got)


SB_DEAD = -110.0


def _sb_logs(z, strict):
    tt = jnp.log(1.0 + jnp.exp(-jnp.abs(z)))
    log_keep = jnp.where(strict, -(jnp.maximum(z, 0.0) + tt), 0.0)
    log_beta = jnp.minimum(z, 0.0) - tt
    return log_keep, log_beta


def _tri(n, upper):
    a = lax.broadcasted_iota(jnp.int32, (n, n), 0)
    b = lax.broadcasted_iota(jnp.int32, (n, n), 1)
    return jnp.where((a < b) if upper else (a > b), 1.0, 0.0).astype(MXU_DT)


def _sb_carry_lane(jj, h):
    return GROUP_HEADS * jj + h


def sb_fwd(proj, *, name, tq=512, tk=256):
    t = proj.shape[0]
    c = GROUP_WIDTH
    tq, tk, nq, nk = _attn_tiles(t, tq, tk)
    assert nk * GROUP_HEADS <= LANES
    qb = P_SB // c
    scale = HEAD_DIM ** -0.5

    def body(q_ref, k_ref, v_ref, o_ref, rs_ref, r_scr, acc_scr):
        i = pl.program_id(0)
        last = ((i + 1) * tq - 1) // tk
        masks = _head_masks(c)
        q = q_ref[...]
        qh = [jnp.where(masks[h], q, 0.0).astype(MXU_DT) for h in range(GROUP_HEADS)]
        lane = lax.broadcasted_iota(jnp.int32, (1, LANES), 1)
        later = _tri(tk, upper=False)
        r_scr[...] = jnp.zeros_like(r_scr)
        acc_scr[...] = jnp.zeros_like(acc_scr)
        rs_ref[...] = jnp.full((tq, LANES), 2.0 * SB_DEAD, F32)

        def step(state):
            jj, _ = state
            j = last - jj
            rows = pl.ds(pl.multiple_of(j * tk, tk), tk)
            kb = k_ref[rows, :].astype(MXU_DT)
            vb = v_ref[rows, :].astype(MXU_DT)
            row = i * tq + lax.broadcasted_iota(jnp.int32, (tq, tk), 0)
            col = j * tk + lax.broadcasted_iota(jnp.int32, (tq, tk), 1)
            strict = col < row
            acc = acc_scr[...]
            rs = rs_ref[...]
            for h in range(GROUP_HEADS):
                z = _dot_nt(qh[h], kb) * scale
                log_keep, log_beta = _sb_logs(z, strict)
                r_old = r_scr[h]
                rs = jnp.where(lane == _sb_carry_lane(jj, h), r_old, rs)
                rest = r_old + _split_dot(log_keep, later)
                w = jnp.where(strict, jnp.exp(log_beta + rest), 0.0)
                acc = jnp.where(masks[h], acc + _dot(w.astype(MXU_DT), vb), acc)
                r_scr[h] = r_old + jnp.sum(log_keep, axis=-1, keepdims=True)
            acc_scr[...] = acc
            rs_ref[...] = rs
            return jj + 1, jnp.max(r_scr[...])

        lax.while_loop(lambda s: (s[0] <= last) & (s[1] > SB_DEAD), step, (jnp.int32(0), jnp.float32(0.0)))
        o_ref[...] = acc_scr[...]

    resident = lambda blk: pl.BlockSpec((t, c), lambda i: (0, blk), pipeline_mode=pl.Buffered(1))
    return pl.pallas_call(
        body, name=name, grid=(nq,),
        in_specs=[pl.BlockSpec((tq, c), lambda i: (i, qb)), resident(qb + 1), resident(qb + 2)],
        out_specs=[pl.BlockSpec((tq, c), lambda i: (i, 0)), pl.BlockSpec((tq, LANES), lambda i: (i, 0))],
        out_shape=[jax.ShapeDtypeStruct((t, c), F32), jax.ShapeDtypeStruct((t, LANES), F32)],
        scratch_shapes=[pltpu.VMEM((GROUP_HEADS, tq, 1), F32), pltpu.VMEM((tq, c), F32)],
        compiler_params=_cparams(("arbitrary",)),
    )(proj, proj, proj)


def sb_bwd(proj, rsave, do, *, name, tq=512, tk=256, comm=None):
    t = proj.shape[0]
    c = GROUP_WIDTH
    tq, tk, nq, nk = _attn_tiles(t, tq, tk)
    qb = P_SB // c
    scale = HEAD_DIM ** -0.5

    def body(q_ref, k_ref, v_ref, rs_ref, do_ref, dq_ref, dk_hbm, dv_hbm, e_scr, dq_scr, dk_scr, dv_scr):
        i = pl.program_id(0)
        last = ((i + 1) * tq - 1) // tk
        masks = _head_masks(c)

        @pl.when(i == 0)
        def _():
            dk_scr[...] = jnp.zeros_like(dk_scr)
            dv_scr[...] = jnp.zeros_like(dv_scr)

        e_scr[...] = jnp.zeros_like(e_scr)
        dq_scr[...] = jnp.zeros_like(dq_scr)
        q = q_ref[...]
        qf = q.astype(MXU_DT)
        qh = [jnp.where(masks[h], q, 0.0).astype(MXU_DT) for h in range(GROUP_HEADS)]
        do = do_ref[...]
        dob = do.astype(MXU_DT)
        doh = [jnp.where(masks[h], do, 0.0).astype(MXU_DT) for h in range(GROUP_HEADS)]
        later = _tri(tk, upper=False)
        earlier = _tri(tk, upper=True)
        rs = rs_ref[...]
        lane = lax.broadcasted_iota(jnp.int32, (1, LANES), 1)
        visited = jnp.where(jnp.max(rs, axis=0, keepdims=True) > SB_DEAD, (lane // GROUP_HEADS + 1).astype(F32), 0.0)
        n_visited = jnp.minimum(jnp.max(visited).astype(jnp.int32), last + 1)

        def step(it, carry):
            jj = n_visited - 1 - it
            j = last - jj
            rows = pl.ds(pl.multiple_of(j * tk, tk), tk)
            kb = k_ref[rows, :].astype(MXU_DT)
            vb = v_ref[rows, :].astype(MXU_DT)
            row = i * tq + lax.broadcasted_iota(jnp.int32, (tq, tk), 0)
            col = j * tk + lax.broadcasted_iota(jnp.int32, (tq, tk), 1)
            strict = col < row
            dq = dq_scr[...]
            dk_upd = jnp.zeros((tk, c), F32)
            dv_upd = jnp.zeros((tk, c), F32)
            for h in range(GROUP_HEADS):
                z = _dot_nt(qh[h], kb) * scale
                log_keep, log_beta = _sb_logs(z, strict)
                r_h = jnp.sum(jnp.where(lane == _sb_carry_lane(jj, h), rs, 0.0), axis=-1, keepdims=True)
                rest = r_h + _split_dot(log_keep, later)
                w = jnp.where(strict, jnp.exp(log_beta + rest), 0.0)
                e = w * _dot_nt(doh[h], vb)
                e_old = e_scr[h]
                dkeep = e_old + _split_dot(e, earlier)
                dz = jnp.where(strict, e * jnp.exp(log_keep) - dkeep * jnp.exp(log_beta), 0.0)
                dzb = dz.astype(MXU_DT)
                dq = jnp.where(masks[h], dq + _dot(dzb, kb) * scale, dq)
                dk_upd = jnp.where(masks[h], _dot_tn(dzb, qf) * scale, dk_upd)
                dv_upd = jnp.where(masks[h], _dot_tn(w.astype(MXU_DT), dob), dv_upd)
                e_scr[h] = e_old + jnp.sum(e, axis=-1, keepdims=True)
            dq_scr[...] = dq
            dk_scr[rows, :] += dk_upd
            dv_scr[rows, :] += dv_upd
            return carry

        lax.fori_loop(0, n_visited, step, 0)
        dq_ref[...] = dq_scr[...]

        @pl.when(i == nq - 1)
        def _():
            pltpu.sync_copy(dk_scr, dk_hbm)
            pltpu.sync_copy(dv_scr, dv_hbm)

    qrow = lambda i: (i, 0)
    resident = lambda blk: pl.BlockSpec((t, c), lambda i: (0, blk), pipeline_mode=pl.Buffered(1))
    hbm = pl.BlockSpec(memory_space=pl.ANY)
    call = dict(
        name=name, grid=(nq,),
        in_specs=[pl.BlockSpec((tq, c), lambda i: (i, qb)), resident(qb + 1), resident(qb + 2),
                  pl.BlockSpec((tq, LANES), qrow), pl.BlockSpec((tq, c), qrow)],
        out_specs=[pl.BlockSpec((tq, c), qrow), hbm, hbm],
        out_shape=[jax.ShapeDtypeStruct((t, c), F32)] * 3,
        scratch_shapes=[pltpu.VMEM((GROUP_HEADS, tq, 1), F32), pltpu.VMEM((tq, c), F32),
                        pltpu.VMEM((t, c), F32), pltpu.VMEM((t, c), F32)],
        compiler_params=_cparams(("arbitrary",)),
    )
    outs, got = carry_comm(call, body, (proj, proj, proj, rsave, do), comm, 3, *_grid_ends(nq))
    return (*outs, got)


A_COL, B_COL = 0, 4
Z_BLK = P_Z // GROUP_WIDTH


NN = (((1,), (0,)), ((), ()))
NT = (((1,), (1,)), ((), ()))
TN = (((0,), (0,)), ((), ()))


def _terms(x, n):
    out, rem = [], x
    for _ in range(n):
        t = rem.astype(MXU_DT)
        out.append(t)
        rem = rem - t.astype(F32)
    return out


def _dotp(a, b, dims, a_terms=2, b_terms=2):
    at, bt = _terms(a, a_terms), _terms(b, b_terms)
    out = None
    for i, x in enumerate(at):
        for j, y in enumerate(bt):
            if i + j < max(a_terms, b_terms):
                r = lax.dot_general(x, y, dims, preferred_element_type=F32)
                out = r if out is None else out + r
    return out


def _silu(x):
    return x * _sigmoid(x)


def _dsilu(x):
    s = _sigmoid(x)
    return s * (1.0 + x * (1.0 - s))


def _head_sum(x, masks):
    out = jnp.zeros_like(x)
    for m in masks:
        out = jnp.where(m, jnp.sum(jnp.where(m, x, 0.0), axis=-1, keepdims=True), out)
    return out


def _expand(cols, col0, masks):
    out = jnp.zeros((cols.shape[0], GROUP_WIDTH), F32)
    for h, m in enumerate(masks):
        out = jnp.where(m, cols[:, col0 + h:col0 + h + 1], out)
    return out


def _reduce(x, col0, masks):
    lane = lax.broadcasted_iota(jnp.int32, (1, LANES), 1)
    out = jnp.zeros((x.shape[0], LANES), F32)
    for h, m in enumerate(masks):
        out = jnp.where(lane == col0 + h, jnp.sum(jnp.where(m, x, 0.0), axis=-1, keepdims=True), out)
    return out


def _block_ones():
    ri = lax.broadcasted_iota(jnp.int32, (GROUP_WIDTH, GROUP_WIDTH), 0) // HEAD_DIM
    ci = lax.broadcasted_iota(jnp.int32, (GROUP_WIDTH, GROUP_WIDTH), 1) // HEAD_DIM
    return jnp.where(ri == ci, 1.0, 0.0).astype(F32)


def _blk(x, hs):
    return jnp.concatenate([x] * GROUP_HEADS, axis=0) * hs


def _unblk(m, hs):
    mm = m * hs
    c = GDN_CHUNK
    return mm[0:c] + mm[c:2 * c] + mm[2 * c:3 * c] + mm[3 * c:4 * c]


def _row_mask4():
    ri = lax.broadcasted_iota(jnp.int32, (GROUP_WIDTH, LANES), 0) // HEAD_DIM
    ci = lax.broadcasted_iota(jnp.int32, (GROUP_WIDTH, LANES), 1)
    return jnp.where(ri + A_COL == ci, 1.0, 0.0).astype(F32)


def _gdn_chunk(xc, small, avec, dtvec, state, masks, hs):
    c = GDN_CHUNK
    w = GROUP_WIDTH
    b16 = lambda v: v.astype(MXU_DT)
    f = {}
    xq, xk, xv = xc[:, :w], xc[:, w:2 * w], xc[:, 2 * w:]
    qs, ks, v = _silu(xq), _silu(xk), _silu(xv)
    rq = lax.rsqrt(_head_sum(qs * qs, masks) + L2_EPS)
    rk = lax.rsqrt(_head_sum(ks * ks, masks) + L2_EPS)
    qn = qs * rq
    k = ks * rk
    q = qn * (HEAD_DIM ** -0.5)
    xg = small + dtvec
    sp = jnp.maximum(xg, 0.0) + jnp.log(1.0 + jnp.exp(-jnp.abs(xg)))
    g128 = -avec * sp
    beta128 = _sigmoid(small)
    ri = lax.broadcasted_iota(jnp.int32, (c, c), 0)
    ci = lax.broadcasted_iota(jnp.int32, (c, c), 1)
    tril = jnp.where(ri >= ci, 1.0, 0.0).astype(F32)
    gam128 = _dotp(tril, g128, NN, 1, 3)
    gam = _expand(gam128, A_COL, masks)
    bfull = _expand(beta128, B_COL, masks)
    mask4 = _row_mask4()
    ones = jnp.ones((c, LANES), F32)
    gam_row = _dotp(ones, jnp.concatenate([gam128] * GROUP_HEADS, axis=0) * mask4, NT, 1, 3)
    li = lax.broadcasted_iota(jnp.int32, (c, w), 0)
    lj = lax.broadcasted_iota(jnp.int32, (c, w), 1) % HEAD_DIM
    incl = li >= lj
    strict = li > lj
    dmat = jnp.exp(jnp.where(incl, gam - gam_row, NEG_BIG))
    egam = jnp.exp(gam)
    glast = gam[c - 1:c, :]
    ekd = jnp.exp(glast - gam)
    kb = k * bfull
    vb = v * bfull
    kbg = kb * egam
    qd = q * egam
    kd = k * ekd
    kblk = b16(_blk(k, hs))
    araw = _dot_nt(b16(kb), kblk)
    a = jnp.where(strict, araw * dmat, 0.0)
    tm = jnp.where(li == lj, 1.0, 0.0) - a
    p = a
    for _ in range(5):
        p = _dotp(p, _blk(p, hs), NN)
        tm = tm + _dotp(tm, _blk(p, hs), NN)
    tm16 = b16(tm)
    u = _dot(tm16, b16(_blk(vb, hs)))
    wm = _dot(tm16, b16(_blk(kbg, hs)))
    qk = _dot_nt(b16(q), kblk)
    aqk = jnp.where(incl, qk * dmat, 0.0)
    s16 = b16(state)
    vn = u - _dot(b16(wm), s16)
    o = _dot(b16(qd), s16) + _dot(b16(aqk), b16(_blk(vn, hs)))
    s_new = state * jnp.exp(glast) + hs * _dot_tn(b16(kd), b16(vn))
    f.update(xq=xq, xk=xk, xv=xv, v=v, rq=rq, rk=rk, qn=qn, k=k, q=q, xg=xg, g128=g128, beta128=beta128,
             tril=tril, gam=gam, bfull=bfull, mask4=mask4, ones=ones, incl=incl, strict=strict, li=li,
             dmat=dmat, egam=egam, glast=glast, ekd=ekd, kb=kb, vb=vb, kbg=kbg, qd=qd, kd=kd, kblk=kblk,
             araw=araw, tm=tm, tm16=tm16, wm=wm, qk=qk, aqk=aqk, s16=s16, vn=vn, o=o, s_new=s_new)
    return f


def _decay_rate(a_log):
    lane = lax.broadcasted_iota(jnp.int32, a_log.shape, 1)
    return jnp.where((lane >= A_COL) & (lane < A_COL + GROUP_HEADS), jnp.exp(a_log), 0.0)


def _gdn_post(o, z, ng, masks):
    r = lax.rsqrt(_head_sum(o * o, masks) * (1.0 / HEAD_DIM) + RMS_EPS)
    on = o * r
    return on, r, on * ng * _silu(z)


def gdn_fwd(cqkv, proj, avec, dtvec, ng, *, name, comm=None):
    t = cqkv.shape[0]
    c = GDN_CHUNK
    w = GROUP_WIDTH
    n = t // c

    def body(x_ref, z_ref, sm_ref, a_ref, dt_ref, ng_ref, y_ref, st_ref, s_scr):
        @pl.when(pl.program_id(0) == 0)
        def _():
            s_scr[...] = jnp.zeros_like(s_scr)

        masks = _head_masks(w)
        hs = _block_ones()
        state = s_scr[...]
        st_ref[0] = state
        f = _gdn_chunk(x_ref[...], sm_ref[...], _decay_rate(a_ref[...]), dt_ref[...], state, masks, hs)
        _, _, y = _gdn_post(f["o"], z_ref[...], ng_ref[...], masks)
        y_ref[...] = y
        s_scr[...] = f["s_new"]

    vec = pl.BlockSpec((1, LANES), lambda i: (0, 0))
    call = dict(
        name=name, grid=(n,),
        in_specs=[pl.BlockSpec((c, 3 * w), lambda i: (i, 0)),
                  pl.BlockSpec((c, w), lambda i: (i, Z_BLK)),
                  pl.BlockSpec((c, LANES), lambda i: (i, SMALL_BLK)),
                  vec, vec, pl.BlockSpec((1, w), lambda i: (0, 0))],
        out_specs=[pl.BlockSpec((c, w), lambda i: (i, 0)), pl.BlockSpec((1, w, w), lambda i: (i, 0, 0))],
        out_shape=[jax.ShapeDtypeStruct((t, w), F32), jax.ShapeDtypeStruct((n, w, w), F32)],
        scratch_shapes=[pltpu.VMEM((w, w), F32)],
        compiler_params=_cparams(("arbitrary",)),
    )
    outs, got = carry_comm(call, body, (cqkv, proj, proj, avec, dtvec, ng), comm, 2, *_grid_ends(n))
    return (*outs, got)


def gdn_bwd(cqkv, proj, avec, dtvec, ng, states, dy, *, name, comm=None):
    t = cqkv.shape[0]
    c = GDN_CHUNK
    w = GROUP_WIDTH
    n = t // c
    b16 = lambda v: v.astype(MXU_DT)

    def body(x_ref, z_ref, sm_ref, a_ref, dt_ref, ng_ref, st_ref, dy_ref,
             dx_ref, dz_ref, dsm_ref, dng_ref, dal_ref, ddt_ref, ds_scr):
        @pl.when(pl.program_id(0) == 0)
        def _():
            ds_scr[...] = jnp.zeros_like(ds_scr)
            dng_ref[...] = jnp.zeros_like(dng_ref)
            dal_ref[...] = jnp.zeros_like(dal_ref)
            ddt_ref[...] = jnp.zeros_like(ddt_ref)

        masks = _head_masks(w)
        hs = _block_ones()
        state = st_ref[0]
        avec_v = _decay_rate(a_ref[...])
        f = _gdn_chunk(x_ref[...], sm_ref[...], avec_v, dt_ref[...], state, masks, hs)
        z = z_ref[...]
        ng_v = ng_ref[...]
        dy_v = dy_ref[...]
        on, r, _ = _gdn_post(f["o"], z, ng_v, masks)
        sz = _silu(z)
        dz_ref[...] = dy_v * on * ng_v * _dsilu(z)
        d_on = dy_v * ng_v * sz
        dng_ref[...] += jnp.sum(dy_v * on * sz, axis=0, keepdims=True)
        do = r * (d_on - on * _head_sum(d_on * on, masks) * (1.0 / HEAD_DIM))
        do16 = b16(do)
        dsn = ds_scr[...]
        dsn16 = b16(dsn)
        s16, vn, kd, qd, wm = f["s16"], f["vn"], f["kd"], f["qd"], f["wm"]
        k, q, kblk, tm, tm16 = f["k"], f["q"], f["kblk"], f["tm"], f["tm16"]
        dmat, egam, glast, gam = f["dmat"], f["egam"], f["glast"], f["gam"]
        incl, strict, li = f["incl"], f["strict"], f["li"]
        vn16 = b16(vn)
        dvn = _unblk(_dot_tn(b16(f["aqk"]), do16), hs) + _dot(b16(kd), dsn16)
        daqk = jnp.where(incl, _dot_nt(do16, b16(_blk(vn, hs))), 0.0)
        dqd = _dot_nt(do16, s16)
        dvn16 = b16(dvn)
        ds_scr[...] = hs * (_dot_tn(b16(qd), do16) - _dot_tn(b16(wm), dvn16)) + dsn * jnp.exp(glast)
        dkd = _dot_nt(vn16, dsn16)
        dglast = jnp.sum(dsn * state, axis=0, keepdims=True) * jnp.exp(glast)
        du16 = dvn16
        dw16 = b16(-_dot_nt(dvn16, s16))
        dqk16 = b16(daqk * dmat)
        ddm = daqk * f["qk"]
        dq = _dot(dqk16, kblk)
        dk = _unblk(_dot_tn(dqk16, b16(q)), hs)
        dtm = _dot_nt(du16, b16(_blk(f["vb"], hs))) + _dot_nt(dw16, b16(_blk(f["kbg"], hs)))
        dvb = _unblk(_dot_tn(tm16, du16), hs)
        dkbg = _unblk(_dot_tn(tm16, dw16), hs)
        xx = _unblk(_dotp(tm, dtm, TN), hs)
        da = jnp.where(strict, -_dotp(xx, _blk(tm, hs), NT), 0.0)
        daraw16 = b16(da * dmat)
        ddm = ddm + da * f["araw"]
        dkb = _dot(daraw16, kblk)
        dk = dk + _unblk(_dot_tn(daraw16, b16(f["kb"])), hs)
        tcol = ddm * dmat
        dgam = tcol
        dgam128_row = _dotp(-tcol, f["ones"], TN, 2, 1) * f["mask4"]
        dgam128_row = (dgam128_row[0:c] + dgam128_row[c:2 * c] + dgam128_row[2 * c:3 * c] + dgam128_row[3 * c:4 * c])
        dk = dk + dkd * f["ekd"]
        tt = dkd * kd
        dgam = dgam - tt
        dglast = dglast + jnp.sum(tt, axis=0, keepdims=True)
        dq = dq + dqd * egam
        dgam = dgam + dqd * qd
        dkb = dkb + dkbg * egam
        dgam = dgam + dkbg * f["kbg"]
        dk = dk + dkb * f["bfull"]
        dbf = dkb * k + dvb * f["v"]
        dv = dvb * f["bfull"]
        dgam = dgam + jnp.where(li == c - 1, dglast, 0.0)
        beta128 = f["beta128"]
        db128 = _reduce(dbf, B_COL, masks) * beta128 * (1.0 - beta128)
        dgam128 = _reduce(dgam, A_COL, masks) + dgam128_row
        dg128 = _dotp(f["tril"], dgam128, TN, 1, 2)
        dxg = dg128 * (-avec_v * _sigmoid(f["xg"]))
        lane = lax.broadcasted_iota(jnp.int32, (1, LANES), 1)
        dsm_ref[...] = jnp.where(lane < B_COL, dxg, db128)
        ddt_ref[...] += jnp.sum(dxg, axis=0, keepdims=True)
        dal_ref[...] += jnp.sum(dg128 * f["g128"], axis=0, keepdims=True)
        dqn = dq * (HEAD_DIM ** -0.5)
        dqs = f["rq"] * (dqn - f["qn"] * _head_sum(dqn * f["qn"], masks))
        dks = f["rk"] * (dk - k * _head_sum(dk * k, masks))
        dx_ref[:, :w] = dqs * _dsilu(f["xq"])
        dx_ref[:, w:2 * w] = dks * _dsilu(f["xk"])
        dx_ref[:, 2 * w:] = dv * _dsilu(f["xv"])

    vec = pl.BlockSpec((1, LANES), lambda i: (0, 0))
    rev = lambda blk: (lambda i: (n - 1 - i, blk))
    call = dict(
        name=name, grid=(n,),
        in_specs=[pl.BlockSpec((c, 3 * w), rev(0)),
                  pl.BlockSpec((c, w), rev(Z_BLK)),
                  pl.BlockSpec((c, LANES), rev(SMALL_BLK)),
                  vec, vec, pl.BlockSpec((1, w), lambda i: (0, 0)),
                  pl.BlockSpec((1, w, w), lambda i: (n - 1 - i, 0, 0)),
                  pl.BlockSpec((c, w), rev(0))],
        out_specs=[pl.BlockSpec((c, 3 * w), rev(0)), pl.BlockSpec((c, w), rev(0)), pl.BlockSpec((c, LANES), rev(0)),
                   pl.BlockSpec((1, w), lambda i: (0, 0)), vec, vec],
        out_shape=[jax.ShapeDtypeStruct((t, 3 * w), F32), jax.ShapeDtypeStruct((t, w), F32),
                   jax.ShapeDtypeStruct((t, LANES), F32), jax.ShapeDtypeStruct((1, w), F32),
                   jax.ShapeDtypeStruct((1, LANES), F32), jax.ShapeDtypeStruct((1, LANES), F32)],
        scratch_shapes=[pltpu.VMEM((w, w), F32)],
        compiler_params=_cparams(("arbitrary",)),
    )
    outs, got = carry_comm(call, body, (cqkv, proj, proj, avec, dtvec, ng, states, dy), comm, 6, *_grid_ends(n))
    return (*outs, got)


def adamw(w, m, v, gslots, *, row0=0, name):
    r, c = w.shape
    s = gslots.shape[0]
    tr = _tile(r, 64, 8)
    assert row0 % tr == 0 and gslots.shape[2] == c
    rb = row0 // tr
    c1 = 1.0 - ADAM_B1 ** ADAM_STEP
    c2 = 1.0 - ADAM_B2 ** ADAM_STEP

    def body(w_ref, m_ref, v_ref, gs_ref, g_ref, d_ref, mo_ref, vo_ref):
        g = gs_ref[0]
        for k in range(1, s):
            g = g + gs_ref[k]
        m_new = ADAM_B1 * m_ref[...] + (1.0 - ADAM_B1) * g
        v_new = ADAM_B2 * v_ref[...] + (1.0 - ADAM_B2) * (g * g)
        m_hat = m_new / c1
        v_hat = v_new / c2
        g_ref[...] = g
        mo_ref[...] = m_new
        vo_ref[...] = v_new
        d_ref[...] = -ADAM_LR * (m_hat / (jnp.sqrt(v_hat) + ADAM_EPS) + ADAM_WD * w_ref[...])

    row = pl.BlockSpec((tr, c), lambda i: (i, 0))
    return pl.pallas_call(
        body, name=name, grid=(r // tr,),
        in_specs=[row, row, row, pl.BlockSpec((s, tr, c), lambda i: (0, rb + i, 0))],
        out_specs=[row] * 4,
        out_shape=[jax.ShapeDtypeStruct((r, c), F32)] * 4,
        compiler_params=_cparams(("parallel",)),
    )(w, m, v, gslots)


def slot_sum(slots, *, name):
    s, r, c = slots.shape

    def body(s_ref, o_ref):
        acc = s_ref[0]
        for k in range(1, s):
            acc = acc + s_ref[k]
        o_ref[...] = acc

    return pl.pallas_call(
        body, name=name, grid=(1,),
        in_specs=[pl.BlockSpec((s, r, c), lambda i: (0, 0, 0))],
        out_specs=pl.BlockSpec((r, c), lambda i: (0, 0)),
        out_shape=jax.ShapeDtypeStruct((r, c), F32),
        compiler_params=_cparams(("arbitrary",)),
    )(slots)


class Comm:
    def __init__(self, srcs, broadcast):
        self.srcs = list(srcs)
        self.broadcast = [broadcast] * len(self.srcs) if isinstance(broadcast, bool) else list(broadcast)
        self.n = len(self.srcs)
        self.out_shapes = [jax.ShapeDtypeStruct(((N_DEV,) + s.shape) if b else s.shape, s.dtype)
                           for s, b in zip(self.srcs, self.broadcast)]
        self.sems = [pltpu.SemaphoreType.DMA((self.n,))] * 3

    def _local(self, src_refs, out_refs, loc_sem, a, me):
        src = src_refs[a] if self.broadcast[a] else src_refs[a].at[me]
        return pltpu.make_async_copy(src, out_refs[a].at[me], loc_sem.at[a])

    def start(self, src_refs, out_refs, send_sem, recv_sem, loc_sem):
        x, y, c = lax.axis_index("x"), lax.axis_index("y"), lax.axis_index("c")
        me = 4 * x + 2 * y + c
        for a in range(self.n):
            self._local(src_refs, out_refs, loc_sem, a, me).start()
        for d in range(1, N_DEV):
            px, py, pc = x ^ ((d >> 2) & 1), y ^ ((d >> 1) & 1), c ^ (d & 1)
            peer = 4 * px + 2 * py + pc
            for a in range(self.n):
                src = src_refs[a] if self.broadcast[a] else src_refs[a].at[peer]
                pltpu.make_async_remote_copy(
                    src_ref=src, dst_ref=out_refs[a].at[me],
                    send_sem=send_sem.at[a], recv_sem=recv_sem.at[a],
                    device_id=(px, py, pc), device_id_type=pl.DeviceIdType.MESH).start()

    def wait(self, src_refs, out_refs, send_sem, recv_sem, loc_sem):
        x, y, c = lax.axis_index("x"), lax.axis_index("y"), lax.axis_index("c")
        me = 4 * x + 2 * y + c
        for a in range(self.n):
            seven = out_refs[a].at[pl.ds(0, N_DEV - 1)]
            pltpu.make_async_remote_copy(
                src_ref=seven, dst_ref=seven, send_sem=send_sem.at[a], recv_sem=recv_sem.at[a],
                device_id=(x, y, c), device_id_type=pl.DeviceIdType.MESH).wait()
            self._local(src_refs, out_refs, loc_sem, a, me).wait()


def exchange(srcs, *, broadcast, name):
    comm = Comm(srcs, broadcast)
    n = comm.n

    def body(*refs):
        src_refs, out_refs, sems = refs[:n], refs[n:2 * n], refs[2 * n:]
        comm.start(src_refs, out_refs, *sems)
        comm.wait(src_refs, out_refs, *sems)

    anyspec = pl.BlockSpec(memory_space=pl.ANY)
    return pl.pallas_call(
        body, name=name,
        in_specs=[anyspec] * n, out_specs=[anyspec] * n, out_shape=comm.out_shapes,
        scratch_shapes=comm.sems,
        compiler_params=pltpu.CompilerParams(has_side_effects=True),
    )(*srcs)


def carry_comm(call_kwargs, body, args, comm, n_out, is_first, is_last):
    if comm is None:
        return pl.pallas_call(body, **call_kwargs)(*args), []
    n_in, nc = len(args), comm.n
    n_scr = len(call_kwargs["scratch_shapes"])
    anyspec = pl.BlockSpec(memory_space=pl.ANY)

    def wrapped(*refs):
        ins, csrc = refs[:n_in], refs[n_in:n_in + nc]
        outs = refs[n_in + nc:n_in + nc + n_out]
        cout = refs[n_in + nc + n_out:n_in + 2 * nc + n_out]
        rest = refs[n_in + 2 * nc + n_out:]
        scr, sems = rest[:n_scr], rest[n_scr:]

        @pl.when(is_first())
        def _():
            comm.start(csrc, cout, *sems)

        body(*ins, *outs, *scr)

        @pl.when(is_last())
        def _():
            comm.wait(csrc, cout, *sems)

    kw = dict(call_kwargs)
    kw["in_specs"] = list(kw["in_specs"]) + [anyspec] * nc
    kw["out_specs"] = list(kw["out_specs"]) + [anyspec] * nc
    kw["out_shape"] = list(kw["out_shape"]) + comm.out_shapes
    kw["scratch_shapes"] = list(kw["scratch_shapes"]) + comm.sems
    cp = kw["compiler_params"]
    kw["compiler_params"] = pltpu.CompilerParams(dimension_semantics=cp.dimension_semantics,
                                                 vmem_limit_bytes=cp.vmem_limit_bytes, has_side_effects=True)
    res = pl.pallas_call(wrapped, **kw)(*args, *comm.srcs)
    return res[:n_out], res[n_out:]


def _pack(arrs):
    flat = []
    for a in arrs:
        f = a.reshape(-1).astype(F32)
        flat.append(jnp.pad(f, (0, (-f.shape[0]) % LANES)))
    buf = jnp.concatenate(flat)
    buf = jnp.pad(buf, (0, (-buf.shape[0]) % (8 * LANES)))
    return buf.reshape(-1, LANES)


def _unpack(buf, shapes):
    flat = buf.reshape(-1)
    out, off = [], 0
    for s in shapes:
        sz = int(np.prod(s))
        out.append(flat[off:off + sz].reshape(s))
        off += sz + (-sz) % LANES
    return out


def _win_to_aligned(w):
    o = np.cumsum((0,) + IN_SPLITS)
    seg = lambda i: w[..., o[i]:o[i + 1]]
    pad = jnp.zeros(w.shape[:-1] + (P_WIDTH - IN_WIDTH,), w.dtype)
    return jnp.concatenate([seg(0), seg(1), seg(4), seg(6), seg(7), seg(2), seg(3), seg(5), pad], axis=-1)


def _win_from_aligned(w):
    o = np.cumsum((0,) + IN_SPLITS)
    s = P_SMALL
    return jnp.concatenate([w[..., P_GDN:P_GDN + 768], w[..., P_Z:P_Z + 256], w[..., s:s + 4], w[..., s + 4:s + 8],
                            w[..., P_FOX:P_FOX + 768], w[..., s + 8:s + 12], w[..., P_CONF:P_CONF + 512],
                            w[..., P_SB:P_SB + 768]], axis=-1)


def _row128(vals, col0):
    return jnp.pad(vals.astype(F32)[None, :], ((0, 0), (col0, LANES - col0 - GROUP_HEADS)))


def _ffn_fwd(x, x16, w, n, tag, comm=None, on_comm=None):
    gu = mm(x16, w[f"gu{n}"], name=f"{tag}_gu", tm=1024, tn=512, tk=1024, out_dtype=MXU_DT, comm=comm)
    if comm is not None:
        gu, got = gu
        on_comm(got)
    h = act_fwd(gu, name=f"{tag}_act")
    y = mm(h, w[f"d{n}"], name=f"{tag}_down", tm=1024, tn=512, tk=D_FF)
    out, out16, xh, rs = ln_res_fwd(x, y, w[f"ln_ffn{n}_g"], w[f"ln_ffn{n}_b"], 0.5, name=f"{tag}_ln")
    return out, out16, (x16, gu, h, xh, rs)


def _ffn_bwd(dout, saved, w, n, tag, comm_dh=None, comm_dwgu=None, comm_dx=None):
    x, gu, h, xh, rs = saved
    wgu, wd = w[f"gu{n}"], w[f"d{n}"]
    got = [[], [], []]
    dz, dg, db = ln_res_bwd(dout, xh, rs, w[f"ln_ffn{n}_g"], name=f"{tag}_ln_bwd")
    dh = mm(dz, wd, mode="nt", alpha=0.5, name=f"{tag}_dh", tm=1024, tn=D_FF // 2, tk=1024, comm=comm_dh)
    if comm_dh is not None:
        dh, got[0] = dh
    dgu = act_bwd(gu, dh, name=f"{tag}_act_bwd")
    dwd = mm(h, dz, mode="tn", alpha=0.5, name=f"{tag}_dwd", tm=D_FF // 2, tn=1024, tk=512)
    c = comm_dwgu(dwd) if comm_dwgu is not None else None
    dwgu = mm(x, dgu, mode="tn", name=f"{tag}_dwgu", tm=1024, tn=D_FF // 2, tk=512, comm=c)
    if c is not None:
        dwgu, got[1] = dwgu
    c = comm_dx(dwgu) if comm_dx is not None else None
    dx = mm(dgu, wgu, mode="nt", add=dz, beta=DN_ALPHA, name=f"{tag}_dx", tm=1024, tn=1024, tk=D_FF // 2, comm=c)
    if c is not None:
        dx, got[2] = dx
    return dx, dwgu, dwd, dg, db, got


def _layer_fwd(x, x16, mem, w, tag, comm_ffn1=None, on_ffn1=None, comm_gdn=None, on_gdn=None, comm_fox=None,
               on_fox=None):
    sv = {}
    x1, x1h, sv["ffn1"] = _ffn_fwd(x, x16, w, 1, f"{tag}_ffn1", comm=comm_ffn1, on_comm=on_ffn1)
    proj = mm(x1h, w["win"], name=f"{tag}_inproj", tm=1024, tn=640, tk=1024)
    cqkv = dwconv_fwd(proj, w["gdn_conv_w"], None, col0=P_GDN, name=f"{tag}_gdn_conv")
    ya, states, got = gdn_fwd(cqkv, proj, w["alog"], w["dtb"], w["ng"], name=f"{tag}_gdn", comm=comm_gdn)
    if on_gdn is not None:
        on_gdn(got)
    cum = fox_gate_fwd(proj, w["bf"], name=f"{tag}_fox_gate")
    cum_t = jnp.pad(cum[:, FOX_COL:FOX_COL + GROUP_HEADS].T, ((0, 8 - GROUP_HEADS), (0, 0)))
    yb, lse, got = fox_fwd(proj, cum, cum_t, name=f"{tag}_fox", comm=comm_fox)
    if on_fox is not None:
        on_fox(got)
    u = glu_fwd(proj, name=f"{tag}_glu")
    cc = dwconv_fwd(u, w["conf_dw_w"], w["conf_dw_b"], name=f"{tag}_conf_conv")
    yc = gn_silu_fwd(cc, w["conf_norm_g"], w["conf_norm_b"], name=f"{tag}_conf_norm")
    yd, rsave = sb_fwd(proj, name=f"{tag}_sb")
    ycat = jnp.concatenate([ya, yb, yc, yd], axis=1).astype(MXU_DT)
    mix = mm(ycat, w["wout"], name=f"{tag}_outproj")
    x2, x2h, xh2, rs2 = ln_res_fwd(x1, mix, w["ln_mix_g"], w["ln_mix_b"], 1.0, name=f"{tag}_ln_mix")
    sv["mix"] = (x1h, proj, cqkv, states, cum, cum_t, yb, lse, u, cc, rsave, ycat, xh2, rs2)
    q = mm(x2h, w["wq"], name=f"{tag}_memq", out_dtype=MXU_DT)
    kv = mm(mem, w["wkv"], name=f"{tag}_memkv", tm=N_MEM, out_dtype=MXU_DT)
    att = memattn_fwd(q, kv, name=f"{tag}_memattn")
    mo = mm(att, w["wo"], name=f"{tag}_memo")
    x3, x3h, xh3, rs3 = ln_res_fwd(x2, mo, w["ln_mem_g"], w["ln_mem_b"], 1.0, name=f"{tag}_ln_mem")
    sv["mem"] = (x2h, q, kv, att, xh3, rs3)
    x4, x4h, sv["ffn2"] = _ffn_fwd(x3, x3h, w, 2, f"{tag}_ffn2")
    return x4, x4h, sv


def _layer_bwd(dx4, mem, sv, w, tag, plan, tail=None):
    t = dx4.shape[0]
    gr = {}
    dx3, gr["gu2"], gr["d2"], gr["ln_ffn2_g"], gr["ln_ffn2_b"], _ = _ffn_bwd(dx4, sv["ffn2"], w, 2, f"{tag}_ffn2")
    x2, q, kv, att, xh3, rs3 = sv["mem"]
    dz, gr["ln_mem_g"], gr["ln_mem_b"] = ln_res_bwd(dx3, xh3, rs3, w["ln_mem_g"], name=f"{tag}_ln_mem_bwd")
    datt = mm(dz, w["wo"], mode="nt", name=f"{tag}_datt", out_dtype=MXU_DT)
    gr["wo"] = mm(att, dz, mode="tn", name=f"{tag}_dwo", tk=512)
    dq, dkv = memattn_bwd(q, kv, datt, name=f"{tag}_memattn_bwd")
    gr["wq"] = mm(x2, dq, mode="tn", name=f"{tag}_dwq", tk=512)
    gr["wkv"] = mm(mem, dkv, mode="tn", name=f"{tag}_dwkv", tk=N_MEM)
    dx2 = mm(dq, w["wq"], mode="nt", add=dz, beta=DN_ALPHA, name=f"{tag}_dx2")
    x1, proj, cqkv, states, cum, cum_t, yb, lse, u, cc, rsave, ycat, xh2, rs2 = sv["mix"]
    dz, gr["ln_mix_g"], gr["ln_mix_b"] = ln_res_bwd(dx2, xh2, rs2, w["ln_mix_g"], name=f"{tag}_ln_mix_bwd")
    dycat = mm(dz, w["wout"], mode="nt", name=f"{tag}_dycat")
    gr["wout"] = mm(ycat, dz, mode="tn", name=f"{tag}_dwout", tk=512)
    comm_sb, comm_fox, comm_gdn = plan(gr)
    gw = GROUP_WIDTH
    dya, dyb, dyc, dyd = (dycat[:, i * gw:(i + 1) * gw] for i in range(4))
    dq_d, dk_d, dv_d, got_sb = sb_bwd(proj, rsave, dyd, name=f"{tag}_sb_bwd", comm=comm_sb)
    dcc, gr["conf_norm_g"], gr["conf_norm_b"] = gn_silu_bwd(cc, w["conf_norm_g"], w["conf_norm_b"], dyc,
                                                            name=f"{tag}_conf_norm_bwd")
    du, gr["conf_dw_w"], gr["conf_dw_b"] = dwconv_bwd(dcc, u, w["conf_dw_w"], name=f"{tag}_conf_conv_bwd")
    dglu = glu_bwd(proj, du, name=f"{tag}_glu_bwd")
    dq_b, dk_b, dv_b, dcc, dcr, got_fox = fox_bwd(proj, cum, cum_t, yb, lse, dyb, name=f"{tag}_fox_bwd", comm=comm_fox)
    dcum = dcc + jnp.pad(dcr[:, :GROUP_HEADS, :].transpose(0, 2, 1).reshape(t, GROUP_HEADS),
                   ((0, 0), (FOX_COL, LANES - FOX_COL - GROUP_HEADS)))
    dsm_f, dbf = fox_gate_bwd(dcum, proj, w["bf"], name=f"{tag}_fox_gate_bwd")
    gr["fox_b_f"] = dbf[0, FOX_COL:FOX_COL + GROUP_HEADS]
    dcq, dz_a, dsm_a, dng, dal, ddt, got_gdn = gdn_bwd(cqkv, proj, w["alog"], w["dtb"], w["ng"], states, dya,
                                                       name=f"{tag}_gdn_bwd", comm=comm_gdn)
    gr["gdn_norm_g"] = dng.reshape(GROUP_HEADS, HEAD_DIM).sum(0)
    gr["gdn_a_log"] = dal[0, A_COL:A_COL + GROUP_HEADS]
    gr["gdn_dt_bias"] = ddt[0, A_COL:A_COL + GROUP_HEADS]
    dgq, gr["gdn_conv_w"], _ = dwconv_bwd(dcq, proj, w["gdn_conv_w"], col0=P_GDN, name=f"{tag}_gdn_conv_bwd")
    dproj = jnp.concatenate([dgq, dz_a, dq_b, dk_b, dv_b, dglu, dq_d, dk_d, dv_d, dsm_a + dsm_f],
                            axis=1).astype(MXU_DT)
    gr["win"] = mm(x1, dproj, mode="tn", name=f"{tag}_dwin", tm=1024, tn=640, tk=512)
    dx1 = mm(dproj, w["win"], mode="nt", add=dz, beta=DN_ALPHA, name=f"{tag}_dx1", tm=1024, tn=1024, tk=640)
    tail = {} if tail is None else dict(tail, comm_dh=tail["comm_dh"](gr))
    dx0, gr["gu1"], gr["d1"], gr["ln_ffn1_g"], gr["ln_ffn1_b"], got_tail = _ffn_bwd(
        dx1, sv["ffn1"], w, 1, f"{tag}_ffn1", **tail)
    return dx0, gr, (got_sb, got_fox, got_gdn), got_tail


SMALL_REPLICATED = ("ln_ffn1_g", "ln_ffn1_b", "gdn_a_log", "gdn_dt_bias", "gdn_norm_g", "fox_b_f", "conf_dw_b",
                    "conf_norm_g", "conf_norm_b", "ln_mix_g", "ln_mix_b", "ln_mem_g", "ln_mem_b", "ln_ffn2_g",
                    "ln_ffn2_b")
SMALL_SHARDED = ("gdn_conv_w", "conf_dw_w")
BIG = ("ffn1_w_gate", "ffn1_w_up", "ffn1_w_down", "w_in", "w_out", "mem_w_q", "mem_w_kv", "mem_w_o",
       "ffn2_w_gate", "ffn2_w_up", "ffn2_w_down")
WEIGHT_ORDER = ("ffn1_w_gate", "ffn1_w_up", "ffn1_w_down", "ln_ffn1_g", "ln_ffn1_b", "w_in", "gdn_conv_w", "gdn_a_log",
                "gdn_dt_bias", "gdn_norm_g", "fox_b_f", "conf_dw_w", "conf_dw_b", "conf_norm_g", "conf_norm_b", "w_out",
                "ln_mix_g", "ln_mix_b", "mem_w_q", "mem_w_kv", "mem_w_o", "ln_mem_g", "ln_mem_b", "ffn2_w_gate",
                "ffn2_w_up", "ffn2_w_down", "ln_ffn2_g", "ln_ffn2_b")


def _step(x, mem, loss_target, wts, ms, vs):
    me = 4 * lax.axis_index("x") + 2 * lax.axis_index("y") + lax.axis_index("c")
    x = x[0]
    mem = mem[0]
    target = loss_target[0]
    rows_s = D_MODEL // N_DEV
    first, rest = ("gu1", "d1", "win"), ("sq", "kv", "gu2", "d2")

    def shards(l):
        c = lambda k: wts[k][l].astype(MXU_DT)
        return dict(gu1=jnp.stack([c("ffn1_w_gate"), c("ffn1_w_up")]), d1=c("ffn1_w_down"),
                    win=_win_to_aligned(wts["w_in"][l]).astype(MXU_DT),
                    sq=jnp.stack([c("w_out"), c("mem_w_q"), c("mem_w_o")]), kv=c("mem_w_kv"),
                    gu2=jnp.stack([c("ffn2_w_gate"), c("ffn2_w_up")]), d2=c("ffn2_w_down"))

    def to_compute_layout(w, keys, got):
        for k, g in zip(keys, got):
            if k in ("gu1", "gu2"):
                w[k] = g.transpose(2, 1, 0, 3).reshape(D_MODEL, 2 * D_FF)
            elif k in ("d1", "d2"):
                w[k] = g.reshape(D_FF, D_MODEL)
            elif k == "win":
                w[k] = g.reshape(D_MODEL, P_WIDTH)
            elif k == "sq":
                full = g.transpose(1, 0, 2, 3).reshape(3, D_MODEL, D_MODEL)
                w["wout"], w["wq"], w["wo"] = full[0], full[1], full[2]
            else:
                w["wkv"] = g.transpose(1, 0, 2).reshape(D_MODEL, 2 * D_MODEL)

    def chunks(gr, keys):
        out = []
        for k in keys:
            if k in ("gu1", "gu2"):
                out.append(gr[k].reshape(D_MODEL, 2, N_DEV, -1).transpose(2, 1, 0, 3))
            elif k in ("d1", "d2"):
                out.append(gr[k].reshape(N_DEV, -1, D_MODEL))
            elif k == "win":
                out.append(gr[k].reshape(N_DEV, rows_s, P_WIDTH))
            elif k == "sq":
                out.append(jnp.stack([gr[n].reshape(N_DEV, rows_s, D_MODEL) for n in ("wout", "wq", "wo")], axis=1))
            else:
                out.append(gr["wkv"].reshape(D_MODEL, N_DEV, -1).transpose(1, 0, 2))
        return out

    sh = [shards(l) for l in range(DEPTH)]
    sm_sh = _pack([wts["gdn_conv_w"], wts["conf_dw_w"]])
    got = exchange([sh[0]["gu1"], sm_sh], broadcast=True, name="gather_first")
    conv_shapes = [wts["gdn_conv_w"].shape, wts["conf_dw_w"].shape]
    parts = [_unpack(got[-1][j], conv_shapes) for j in range(N_DEV)]
    gconv_full = jnp.concatenate([p[0] for p in parts], axis=-1)
    cconv_full = jnp.concatenate([p[1] for p in parts], axis=-1)

    def small_weights(l):
        w = dict(gdn_conv_w=gconv_full[l], conf_dw_w=cconv_full[l],
                 alog=_row128(wts["gdn_a_log"][l], A_COL), dtb=_row128(wts["gdn_dt_bias"][l], A_COL),
                 bf=_row128(wts["fox_b_f"][l], FOX_COL), ng=jnp.tile(wts["gdn_norm_g"][l], GROUP_HEADS)[None, :])
        for k in ("ln_ffn1_g", "ln_ffn1_b", "conf_dw_b", "conf_norm_g", "conf_norm_b", "ln_mix_g", "ln_mix_b",
                  "ln_mem_g", "ln_mem_b", "ln_ffn2_g", "ln_ffn2_b"):
            w[k] = wts[k][l][None, :]
        return w

    lw = [small_weights(l) for l in range(DEPTH)]
    to_compute_layout(lw[0], ("gu1",), got[:-1])

    h, h16, sv0 = _layer_fwd(x, x.astype(MXU_DT), mem, lw[0], "l0",
                        comm_ffn1=Comm([sh[0][k] for k in first[1:]], True),
                        on_ffn1=lambda g: to_compute_layout(lw[0], first[1:], g),
                        comm_gdn=Comm([sh[0][k] for k in rest], True),
                        on_gdn=lambda g: to_compute_layout(lw[0], rest, g),
                        comm_fox=Comm([sh[1][k] for k in first + rest[:2]], True),
                        on_fox=lambda g: to_compute_layout(lw[1], first + rest[:2], g))
    h, _, sv1 = _layer_fwd(h, h16, mem, lw[1], "l1",
                        comm_gdn=Comm([sh[1][k] for k in rest[2:]], True),
                        on_gdn=lambda g: to_compute_layout(lw[1], rest[2:], g))
    dh, lpart = loss_head(h, target, name="loss_head")

    recv = [{}, {}]
    e_ffn, e_mem = ("gu2", "d2"), ("sq", "kv")
    dh, g1, got, _ = _layer_bwd(dh, mem, sv1, lw[1], "l1",
                                lambda gr: (None, Comm(chunks(gr, e_ffn), False), Comm(chunks(gr, e_mem), False)))
    recv[1].update(zip(e_ffn, got[1]))
    recv[1].update(zip(e_mem, got[2]))
    tail = dict(comm_dh=lambda gr: Comm(chunks(gr, ("win",)), False),
                comm_dwgu=lambda dwd: Comm(chunks({"d1": dwd}, ("d1",)), False),
                comm_dx=lambda dwgu: Comm(chunks({"gu1": dwgu}, ("gu1",)), False))
    dh, g0, got, got_t = _layer_bwd(
        dh, mem, sv0, lw[0], "l0",
        lambda gr: (Comm(chunks(gr, e_mem), False), Comm(chunks(g1, first[:2]), False),
                    Comm(chunks(g1, first[2:]) + chunks(gr, e_ffn), False)), tail)
    recv[0].update(zip(e_mem, got[0]))
    recv[1].update(zip(first[:2], got[1]))
    recv[1].update(win=got[2][0])
    recv[0].update(zip(e_ffn, got[2][1:]))
    recv[0].update(win=got_t[0][0], d1=got_t[1][0], gu1=got_t[2][0])
    grad_x = dh[None]
    grads = [g0, g1]

    def gl(k):
        return jnp.stack([grads[l][k] for l in range(DEPTH)])

    small_names = SMALL_REPLICATED + SMALL_SHARDED
    small_grads = [gl(k) for k in small_names] + [lpart[0, :1]]
    got = exchange([_pack(small_grads)], broadcast=True, name="gather_small_grads")
    sm_sum = slot_sum(got[0], name="sum_small_grads")
    sm_g = _unpack(sm_sum, [g.shape for g in small_grads])
    loss = sm_g[-1][0]
    small_g = dict(zip(small_names, sm_g[:-1]))
    for k in SMALL_SHARDED:
        width = wts[k].shape[-1]
        small_g[k] = lax.dynamic_slice_in_dim(small_g[k], me * width, width, axis=2)

    out_g, out_d, out_m, out_v = {}, {}, {}, {}

    def update(names, key, fix=lambda a: a):
        res = {k: [] for k in names}
        for l in range(DEPTH):
            slots = fix(recv[l][key])
            slots = slots.reshape(N_DEV, -1, slots.shape[-1])
            for i, k in enumerate(names):
                two = lambda a: a[l].reshape(-1, a.shape[-1])
                res[k].append(adamw(two(wts[k]), two(ms[k]), two(vs[k]), slots, row0=i * two(wts[k]).shape[0],
                                    name=f"adamw_{k}_l{l}"))
        for k in names:
            for dst, per_layer in zip((out_g, out_d, out_m, out_v), zip(*res[k])):
                dst[k] = jnp.stack(per_layer).reshape(wts[k].shape)

    update(("ffn1_w_gate", "ffn1_w_up"), "gu1")
    update(("ffn1_w_down",), "d1")
    update(("w_in",), "win", _win_from_aligned)
    update(("w_out", "mem_w_q", "mem_w_o"), "sq")
    update(("mem_w_kv",), "kv")
    update(("ffn2_w_gate", "ffn2_w_up"), "gu2")
    update(("ffn2_w_down",), "d2")

    sw = _pack([wts[k] for k in small_names])
    smm = _pack([ms[k] for k in small_names])
    smv = _pack([vs[k] for k in small_names])
    sg = _pack([small_g[k] for k in small_names])
    res = adamw(sw, smm, smv, sg[None], name="adamw_small")
    shapes = [wts[k].shape for k in small_names]
    for dst, buf in zip((out_g, out_d, out_m, out_v), res):
        for k, a in zip(small_names, _unpack(buf, shapes)):
            dst[k] = a

    return (loss, grad_x, *[out_g[k] for k in WEIGHT_ORDER], *[out_d[k] for k in WEIGHT_ORDER],
            *[out_m[k] for k in WEIGHT_ORDER], *[out_v[k] for k in WEIGHT_ORDER])


def kernel(x, mem, ffn1_w_gate, ffn1_w_up, ffn1_w_down, ln_ffn1_g, ln_ffn1_b, w_in, gdn_conv_w, gdn_a_log, gdn_dt_bias, gdn_norm_g, fox_b_f, conf_dw_w, conf_dw_b, conf_norm_g, conf_norm_b, w_out, ln_mix_g, ln_mix_b, mem_w_q, mem_w_kv, mem_w_o, ln_mem_g, ln_mem_b, ffn2_w_gate, ffn2_w_up, ffn2_w_down, ln_ffn2_g, ln_ffn2_b, loss_target, m_ffn1_w_gate, m_ffn1_w_up, m_ffn1_w_down, m_ln_ffn1_g, m_ln_ffn1_b, m_w_in, m_gdn_conv_w, m_gdn_a_log, m_gdn_dt_bias, m_gdn_norm_g, m_fox_b_f, m_conf_dw_w, m_conf_dw_b, m_conf_norm_g, m_conf_norm_b, m_w_out, m_ln_mix_g, m_ln_mix_b, m_mem_w_q, m_mem_w_kv, m_mem_w_o, m_ln_mem_g, m_ln_mem_b, m_ffn2_w_gate, m_ffn2_w_up, m_ffn2_w_down, m_ln_ffn2_g, m_ln_ffn2_b, v_ffn1_w_gate, v_ffn1_w_up, v_ffn1_w_down, v_ln_ffn1_g, v_ln_ffn1_b, v_w_in, v_gdn_conv_w, v_gdn_a_log, v_gdn_dt_bias, v_gdn_norm_g, v_fox_b_f, v_conf_dw_w, v_conf_dw_b, v_conf_norm_g, v_conf_norm_b, v_w_out, v_ln_mix_g, v_ln_mix_b, v_mem_w_q, v_mem_w_kv, v_mem_w_o, v_ln_mem_g, v_ln_mem_b, v_ffn2_w_gate, v_ffn2_w_up, v_ffn2_w_down, v_ln_ffn2_g, v_ln_ffn2_b):
    args = locals()
    wts = {k: args[k] for k in WEIGHT_ORDER}
    ms = {k: args["m_" + k] for k in WEIGHT_ORDER}
    vs = {k: args["v_" + k] for k in WEIGHT_ORDER}
    return _step(x, mem, loss_target, wts, ms, vs)
```

```python
import functools
import math

import jax
import jax.numpy as jnp
import numpy as np
from jax import lax
from jax.experimental import pallas as pl
from jax.experimental.pallas import tpu as pltpu

F32 = jnp.float32
BF16 = jnp.bfloat16
MXU_DT = jnp.bfloat16
HI = lax.Precision.HIGHEST

N_DEV = 8
VMEM_LIMIT_BYTES = 56 * 1024 * 1024
LANES = 128

D_MODEL = 1024
DEPTH = 2
GROUP_WIDTH = 256
HEAD_DIM = 64
GROUP_HEADS = 4
D_FF = 2816
SHORT_CONV = 4
CONF_KERNEL = 31
CONF_GROUPS = 4
GDN_CHUNK = 64
N_MEM = 256
MEM_HEADS = 4
MEM_HEAD_DIM = 256
DN_ALPHA = float((2 * DEPTH) ** 0.25)
LN_EPS = 1e-5
RMS_EPS = 1e-6
L2_EPS = 1e-6
NEG_BIG = -1e30
IN_SPLITS = (768, 256, 4, 4, 768, 4, 512, 768)
IN_WIDTH = sum(IN_SPLITS)
P_GDN, P_Z, P_FOX, P_CONF, P_SB, P_SMALL = 0, 768, 1024, 1792, 2304, 3072
P_WIDTH = 3200

ADAM_LR = 0.001
ADAM_B1 = 0.9
ADAM_B2 = 0.999
ADAM_EPS = 1e-08
ADAM_WD = 0.01
ADAM_STEP = 10


def _cparams(sem):
    return pltpu.CompilerParams(dimension_semantics=sem, vmem_limit_bytes=VMEM_LIMIT_BYTES)


def _tile(n, pref, align=LANES):
    if n <= pref:
        return n
    t = (pref // align) * align
    while t >= align:
        if n % t == 0:
            return t
        t -= align
    return n


def mm(a, b, *, mode="nn", add=None, alpha=1.0, beta=1.0, out_dtype=F32, name,
       tm=1024, tn=512, tk=1024, comm=None):
    if mode == "nn":
        (m, k), (k2, n) = a.shape, b.shape
    elif mode == "nt":
        (m, k), (n, k2) = a.shape, b.shape
    else:
        (k, m), (k2, n) = a.shape, b.shape
    assert k == k2, (a.shape, b.shape, mode)
    tm = _tile(m, tm, 8 if mode != "tn" else LANES)
    tn = _tile(n, tn)
    tk = _tile(k, tk, LANES if mode != "tn" else 8)
    nk = k // tk
    if mode == "nn":
        a_spec = pl.BlockSpec((tm, tk), lambda i, j, kk: (i, kk))
        b_spec = pl.BlockSpec((tk, tn), lambda i, j, kk: (kk, j))
        dims = (((1,), (0,)), ((), ()))
    elif mode == "nt":
        a_spec = pl.BlockSpec((tm, tk), lambda i, j, kk: (i, kk))
        b_spec = pl.BlockSpec((tn, tk), lambda i, j, kk: (j, kk))
        dims = (((1,), (1,)), ((), ()))
    else:
        a_spec = pl.BlockSpec((tk, tm), lambda i, j, kk: (kk, i))
        b_spec = pl.BlockSpec((tk, tn), lambda i, j, kk: (kk, j))
        dims = (((0,), (0,)), ((), ()))
    o_spec = pl.BlockSpec((tm, tn), lambda i, j, kk: (i, j))
    has_add = add is not None

    def body(*refs):
        if has_add:
            a_ref, b_ref, add_ref, o_ref, acc_ref = refs
        else:
            a_ref, b_ref, o_ref, acc_ref = refs
        kk = pl.program_id(2)

        @pl.when(kk == 0)
        def _():
            acc_ref[...] = jnp.zeros_like(acc_ref)

        acc_ref[...] += lax.dot_general(a_ref[...].astype(MXU_DT), b_ref[...].astype(MXU_DT), dims,
                                        preferred_element_type=F32)

        @pl.when(kk == nk - 1)
        def _():
            r = acc_ref[...]
            if alpha != 1.0:
                r = r * alpha
            if has_add:
                r = r + beta * add_ref[...].astype(F32)
            o_ref[...] = r.astype(out_dtype)

    in_specs = [a_spec, b_spec] + ([o_spec] if has_add else [])
    args = (a, b) + ((add,) if has_add else ())
    grid = (m // tm, n // tn, nk)
    call = dict(name=name, grid=grid, in_specs=in_specs, out_specs=[o_spec],
                out_shape=[jax.ShapeDtypeStruct((m, n), out_dtype)],
                scratch_shapes=[pltpu.VMEM((tm, tn), F32)],
                compiler_params=_cparams(("parallel", "parallel", "arbitrary")))
    (out,), got = carry_comm(call, body, args, comm, 1, *_grid_ends(*grid))
    return out if comm is None else (out, got)


def ln_res_fwd(x, y, g, b, s, *, name):
    t, d = x.shape
    tm = _tile(t, 512, 8)

    def body(x_ref, y_ref, g_ref, b_ref, o_ref, o16_ref, xh_ref, rs_ref):
        z = DN_ALPHA * x_ref[...] + s * y_ref[...]
        mu = jnp.mean(z, axis=-1, keepdims=True)
        zc = z - mu
        var = jnp.mean(zc * zc, axis=-1, keepdims=True)
        rstd = lax.rsqrt(var + LN_EPS)
        xh = zc * rstd
        xh_ref[...] = xh
        rs_ref[...] = rstd
        out = xh * g_ref[...] + b_ref[...]
        o_ref[...] = out
        o16_ref[...] = out.astype(o16_ref.dtype)

    row = pl.BlockSpec((tm, d), lambda i: (i, 0))
    vec = pl.BlockSpec((1, d), lambda i: (0, 0))
    return pl.pallas_call(
        body, name=name, grid=(t // tm,),
        in_specs=[row, row, vec, vec],
        out_specs=[row, row, row, pl.BlockSpec((tm, 1), lambda i: (i, 0))],
        out_shape=[jax.ShapeDtypeStruct((t, d), F32), jax.ShapeDtypeStruct((t, d), MXU_DT),
                   jax.ShapeDtypeStruct((t, d), F32), jax.ShapeDtypeStruct((t, 1), F32)],
        compiler_params=_cparams(("parallel",)),
    )(x, y, g, b)


def ln_res_bwd(dout, xhat, rstd, g, *, name):
    t, d = dout.shape
    tm = _tile(t, 512, 8)

    def body(do_ref, xh_ref, rs_ref, g_ref, dz_ref, dg_ref, db_ref):
        i = pl.program_id(0)

        @pl.when(i == 0)
        def _():
            dg_ref[...] = jnp.zeros_like(dg_ref)
            db_ref[...] = jnp.zeros_like(db_ref)

        do = do_ref[...]
        xh = xh_ref[...]
        dxh = do * g_ref[...]
        m1 = jnp.mean(dxh, axis=-1, keepdims=True)
        m2 = jnp.mean(dxh * xh, axis=-1, keepdims=True)
        dz_ref[...] = rs_ref[...] * (dxh - m1 - xh * m2)
        dg_ref[...] += jnp.sum(do * xh, axis=0, keepdims=True)
        db_ref[...] += jnp.sum(do, axis=0, keepdims=True)

    row = pl.BlockSpec((tm, d), lambda i: (i, 0))
    vec = pl.BlockSpec((1, d), lambda i: (0, 0))
    return pl.pallas_call(
        body, name=name, grid=(t // tm,),
        in_specs=[row, row, pl.BlockSpec((tm, 1), lambda i: (i, 0)), vec],
        out_specs=[row, vec, vec],
        out_shape=[jax.ShapeDtypeStruct((t, d), F32), jax.ShapeDtypeStruct((1, d), F32),
                   jax.ShapeDtypeStruct((1, d), F32)],
        compiler_params=_cparams(("arbitrary",)),
    )(dout, xhat, rstd, g)


def _sigmoid(x):
    return 1.0 / (1.0 + jnp.exp(-x))


def act_fwd(gu, *, name):
    t, f2 = gu.shape
    f = f2 // 2
    tm = _tile(t, 256, 8)

    def body(gu_ref, h_ref):
        g = gu_ref[:, :f].astype(F32)
        h_ref[...] = (g * _sigmoid(g) * gu_ref[:, f:].astype(F32)).astype(h_ref.dtype)

    return pl.pallas_call(
        body, name=name, grid=(t // tm,),
        in_specs=[pl.BlockSpec((tm, f2), lambda i: (i, 0))],
        out_specs=pl.BlockSpec((tm, f), lambda i: (i, 0)),
        out_shape=jax.ShapeDtypeStruct((t, f), MXU_DT),
        compiler_params=_cparams(("parallel",)),
    )(gu)


def act_bwd(gu, dh, *, name):
    t, f2 = gu.shape
    f = f2 // 2
    tm = _tile(t, 256, 8)

    def body(gu_ref, dh_ref, o_ref):
        g = gu_ref[:, :f].astype(F32)
        u = gu_ref[:, f:].astype(F32)
        dh = dh_ref[...]
        sg = _sigmoid(g)
        o_ref[:, f:] = (dh * g * sg).astype(o_ref.dtype)
        o_ref[:, :f] = (dh * u * sg * (1.0 + g * (1.0 - sg))).astype(o_ref.dtype)

    return pl.pallas_call(
        body, name=name, grid=(t // tm,),
        in_specs=[pl.BlockSpec((tm, f2), lambda i: (i, 0)), pl.BlockSpec((tm, f), lambda i: (i, 0))],
        out_specs=pl.BlockSpec((tm, f2), lambda i: (i, 0)),
        out_shape=jax.ShapeDtypeStruct((t, f2), MXU_DT),
        compiler_params=_cparams(("parallel",)),
    )(gu, dh)


def loss_head(y, target, *, name):
    t, d = y.shape
    tm = _tile(t, 512, 8)

    def body(y_ref, t_ref, dy_ref, l_ref):
        i = pl.program_id(0)

        @pl.when(i == 0)
        def _():
            l_ref[...] = jnp.zeros_like(l_ref)

        err = y_ref[...] - t_ref[...]
        dy_ref[...] = err * (1.0 / d)
        part = jnp.sum(jnp.sum(err * err, axis=-1, keepdims=True), axis=0, keepdims=True)
        l_ref[...] += jnp.broadcast_to(part * (0.5 / d), l_ref.shape)

    row = pl.BlockSpec((tm, d), lambda i: (i, 0))
    return pl.pallas_call(
        body, name=name, grid=(t // tm,),
        in_specs=[row, row],
        out_specs=[row, pl.BlockSpec((1, LANES), lambda i: (0, 0))],
        out_shape=[jax.ShapeDtypeStruct((t, d), F32), jax.ShapeDtypeStruct((1, LANES), F32)],
        compiler_params=_cparams(("arbitrary",)),
    )(y, target)


def _dot(a, b):
    return lax.dot_general(a, b, (((1,), (0,)), ((), ())), preferred_element_type=F32)


def _dot_nt(a, b):
    return lax.dot_general(a, b, (((1,), (1,)), ((), ())), preferred_element_type=F32)


def _dot_tn(a, b):
    return lax.dot_general(a, b, (((0,), (0,)), ((), ())), preferred_element_type=F32)


def _dot_hi(a, b):
    return lax.dot_general(a, b, (((1,), (0,)), ((), ())), preferred_element_type=F32, precision=HI)


def _dot_nt_hi(a, b):
    return lax.dot_general(a, b, (((1,), (1,)), ((), ())), preferred_element_type=F32, precision=HI)


def _split_dot(x, u):
    hi = x.astype(MXU_DT)
    lo = (x - hi.astype(F32)).astype(MXU_DT)
    return _dot(hi, u) + _dot(lo, u)


def _mem_probs(q_ref, kv_ref, h):
    lo = h * MEM_HEAD_DIM
    qh = q_ref[:, lo:lo + MEM_HEAD_DIM].astype(MXU_DT)
    kh = kv_ref[:, lo:lo + MEM_HEAD_DIM].astype(MXU_DT)
    s = _dot_nt(qh, kh) * (MEM_HEAD_DIM ** -0.5)
    s = s - jnp.max(s, axis=-1, keepdims=True)
    p = jnp.exp(s)
    return p / jnp.sum(p, axis=-1, keepdims=True), qh, kh


def memattn_fwd(q, kv, *, name):
    t, d = q.shape
    tm = _tile(t, 512, 8)

    def body(q_ref, kv_ref, o_ref):
        for h in range(MEM_HEADS):
            lo = h * MEM_HEAD_DIM
            p, _, _ = _mem_probs(q_ref, kv_ref, h)
            vh = kv_ref[:, d + lo:d + lo + MEM_HEAD_DIM].astype(MXU_DT)
            o_ref[:, lo:lo + MEM_HEAD_DIM] = _dot(p.astype(MXU_DT), vh).astype(o_ref.dtype)

    return pl.pallas_call(
        body, name=name, grid=(t // tm,),
        in_specs=[pl.BlockSpec((tm, d), lambda i: (i, 0)), pl.BlockSpec(kv.shape, lambda i: (0, 0))],
        out_specs=pl.BlockSpec((tm, d), lambda i: (i, 0)),
        out_shape=jax.ShapeDtypeStruct((t, d), MXU_DT),
        compiler_params=_cparams(("parallel",)),
    )(q, kv)


def memattn_bwd(q, kv, datt, *, name):
    t, d = q.shape
    tm = _tile(t, 512, 8)
    scale = MEM_HEAD_DIM ** -0.5

    def body(q_ref, kv_ref, da_ref, dq_ref, dkv_ref):
        @pl.when(pl.program_id(0) == 0)
        def _():
            dkv_ref[...] = jnp.zeros_like(dkv_ref)

        for h in range(MEM_HEADS):
            lo = h * MEM_HEAD_DIM
            p, qh, kh = _mem_probs(q_ref, kv_ref, h)
            vh = kv_ref[:, d + lo:d + lo + MEM_HEAD_DIM].astype(MXU_DT)
            da = da_ref[:, lo:lo + MEM_HEAD_DIM].astype(MXU_DT)
            dp = _dot_nt(da, vh)
            ds = p * (dp - jnp.sum(dp * p, axis=-1, keepdims=True))
            dsb = ds.astype(MXU_DT)
            dq_ref[:, lo:lo + MEM_HEAD_DIM] = (_dot(dsb, kh) * scale).astype(dq_ref.dtype)
            dkv_ref[:, lo:lo + MEM_HEAD_DIM] += _dot_tn(dsb, qh) * scale
            dkv_ref[:, d + lo:d + lo + MEM_HEAD_DIM] += _dot_tn(p.astype(MXU_DT), da)

    row = pl.BlockSpec((tm, d), lambda i: (i, 0))
    full = pl.BlockSpec(kv.shape, lambda i: (0, 0))
    return pl.pallas_call(
        body, name=name, grid=(t // tm,),
        in_specs=[row, full, row],
        out_specs=[row, full],
        out_shape=[jax.ShapeDtypeStruct((t, d), MXU_DT), jax.ShapeDtypeStruct(kv.shape, F32)],
        compiler_params=_cparams(("arbitrary",)),
    )(q, kv, datt)


def _halo(k):
    return 8 * ((k - 1 + 7) // 8)


def dwconv_fwd(u, w, bias, *, col0=0, width=None, name):
    t = u.shape[0]
    kk, c = w.shape
    width = c if width is None else width
    assert width == c and col0 % c == 0
    cb = col0 // c
    hb = _halo(kk)
    tm = _tile(t, 512, hb)
    r = tm // hb
    has_bias = bias is not None

    def body(*refs):
        if has_bias:
            prev_ref, cur_ref, w_ref, b_ref, o_ref, scr = refs
        else:
            prev_ref, cur_ref, w_ref, o_ref, scr = refs
        i = pl.program_id(0)
        scr[0:hb, :] = jnp.where(i == 0, 0.0, prev_ref[...])
        scr[hb:hb + tm, :] = cur_ref[...]
        acc = jnp.zeros((tm, c), F32)
        for k in range(kk):
            acc = acc + w_ref[k:k + 1, :] * scr[pl.ds(hb - (kk - 1) + k, tm), :]
        if has_bias:
            acc = acc + b_ref[...]
        o_ref[...] = acc

    in_specs = [pl.BlockSpec((hb, c), lambda i: (jnp.maximum(i * r - 1, 0), cb)),
                pl.BlockSpec((tm, c), lambda i: (i, cb)),
                pl.BlockSpec((kk, c), lambda i: (0, 0))]
    args = [u, u, w]
    if has_bias:
        in_specs.append(pl.BlockSpec((1, c), lambda i: (0, 0)))
        args.append(bias)
    return pl.pallas_call(
        body, name=name, grid=(t // tm,),
        in_specs=in_specs,
        out_specs=pl.BlockSpec((tm, c), lambda i: (i, 0)),
        out_shape=jax.ShapeDtypeStruct((t, c), F32),
        scratch_shapes=[pltpu.VMEM((hb + tm, c), F32)],
        compiler_params=_cparams(("parallel",)),
    )(*args)


def dwconv_bwd(dc, u, w, *, col0=0, name):
    t, c = dc.shape
    kk = w.shape[0]
    assert col0 % c == 0
    cb = col0 // c
    hb = _halo(kk)
    tm = _tile(t, 512, hb)
    r = tm // hb
    n = t // tm

    def body(dcur_ref, dnext_ref, uprev_ref, ucur_ref, w_ref, du_ref, dw_ref, db_ref, sd, su):
        i = pl.program_id(0)

        @pl.when(i == 0)
        def _():
            dw_ref[...] = jnp.zeros_like(dw_ref)
            db_ref[...] = jnp.zeros_like(db_ref)

        dcur = dcur_ref[...]
        sd[0:tm, :] = dcur
        sd[tm:tm + hb, :] = jnp.where(i == n - 1, 0.0, dnext_ref[...])
        su[0:hb, :] = jnp.where(i == 0, 0.0, uprev_ref[...])
        su[hb:hb + tm, :] = ucur_ref[...]
        acc = jnp.zeros((tm, c), F32)
        for k in range(kk):
            acc = acc + w_ref[k:k + 1, :] * sd[pl.ds(kk - 1 - k, tm), :]
            dw_ref[k:k + 1, :] += jnp.sum(dcur * su[pl.ds(hb - (kk - 1) + k, tm), :], axis=0, keepdims=True)
        du_ref[...] = acc
        db_ref[...] += jnp.sum(dcur, axis=0, keepdims=True)

    return pl.pallas_call(
        body, name=name, grid=(n,),
        in_specs=[pl.BlockSpec((tm, c), lambda i: (i, 0)),
                  pl.BlockSpec((hb, c), lambda i: (jnp.minimum((i + 1) * r, n * r - 1), 0)),
                  pl.BlockSpec((hb, c), lambda i: (jnp.maximum(i * r - 1, 0), cb)),
                  pl.BlockSpec((tm, c), lambda i: (i, cb)),
                  pl.BlockSpec((kk, c), lambda i: (0, 0))],
        out_specs=[pl.BlockSpec((tm, c), lambda i: (i, 0)),
                   pl.BlockSpec((kk, c), lambda i: (0, 0)),
                   pl.BlockSpec((1, c), lambda i: (0, 0))],
        out_shape=[jax.ShapeDtypeStruct((t, c), F32), jax.ShapeDtypeStruct((kk, c), F32),
                   jax.ShapeDtypeStruct((1, c), F32)],
        scratch_shapes=[pltpu.VMEM((tm + hb, c), F32), pltpu.VMEM((hb + tm, c), F32)],
        compiler_params=_cparams(("arbitrary",)),
    )(dc, dc, u, u, w)


def glu_fwd(proj, *, name):
    t = proj.shape[0]
    c = GROUP_WIDTH
    tm = _tile(t, 1024, 8)
    vb, gb = P_CONF // c, P_CONF // c + 1

    def body(v_ref, g_ref, o_ref):
        o_ref[...] = v_ref[...] * _sigmoid(g_ref[...])

    return pl.pallas_call(
        body, name=name, grid=(t // tm,),
        in_specs=[pl.BlockSpec((tm, c), lambda i: (i, vb)), pl.BlockSpec((tm, c), lambda i: (i, gb))],
        out_specs=pl.BlockSpec((tm, c), lambda i: (i, 0)),
        out_shape=jax.ShapeDtypeStruct((t, c), F32),
        compiler_params=_cparams(("parallel",)),
    )(proj, proj)


def glu_bwd(proj, du, *, name):
    t = proj.shape[0]
    c = GROUP_WIDTH
    tm = _tile(t, 1024, 8)
    vb, gb = P_CONF // c, P_CONF // c + 1

    def body(v_ref, g_ref, du_ref, o_ref):
        sg = _sigmoid(g_ref[...])
        du = du_ref[...]
        o_ref[:, :c] = du * sg
        o_ref[:, c:] = du * v_ref[...] * sg * (1.0 - sg)

    return pl.pallas_call(
        body, name=name, grid=(t // tm,),
        in_specs=[pl.BlockSpec((tm, c), lambda i: (i, vb)), pl.BlockSpec((tm, c), lambda i: (i, gb)),
                  pl.BlockSpec((tm, c), lambda i: (i, 0))],
        out_specs=pl.BlockSpec((tm, 2 * c), lambda i: (i, 0)),
        out_shape=jax.ShapeDtypeStruct((t, 2 * c), F32),
        compiler_params=_cparams(("parallel",)),
    )(proj, proj, du)


def _group_mean_matrix(c, groups):
    gsz = c // groups
    ri = lax.broadcasted_iota(jnp.int32, (c, c), 0) // gsz
    ci = lax.broadcasted_iota(jnp.int32, (c, c), 1) // gsz
    return jnp.where(ri == ci, 1.0 / gsz, 0.0).astype(F32)


def gn_silu_fwd(cx, gamma, beta, *, name):
    t, c = cx.shape
    tm = _tile(t, 1024, 8)

    def body(c_ref, g_ref, b_ref, o_ref):
        gm = _group_mean_matrix(c, CONF_GROUPS)
        x = c_ref[...]
        mu = _dot_hi(x, gm)
        xc = x - mu
        var = _dot_hi(xc * xc, gm)
        a = xc * lax.rsqrt(var + LN_EPS) * g_ref[...] + b_ref[...]
        o_ref[...] = a * _sigmoid(a)

    row = pl.BlockSpec((tm, c), lambda i: (i, 0))
    vec = pl.BlockSpec((1, c), lambda i: (0, 0))
    return pl.pallas_call(
        body, name=name, grid=(t // tm,),
        in_specs=[row, vec, vec], out_specs=row,
        out_shape=jax.ShapeDtypeStruct((t, c), F32),
        compiler_params=_cparams(("parallel",)),
    )(cx, gamma, beta)


def gn_silu_bwd(cx, gamma, beta, dy, *, name):
    t, c = cx.shape
    tm = _tile(t, 1024, 8)

    def body(c_ref, g_ref, b_ref, dy_ref, dc_ref, dg_ref, db_ref):
        @pl.when(pl.program_id(0) == 0)
        def _():
            dg_ref[...] = jnp.zeros_like(dg_ref)
            db_ref[...] = jnp.zeros_like(db_ref)

        gm = _group_mean_matrix(c, CONF_GROUPS)
        x = c_ref[...]
        mu = _dot_hi(x, gm)
        xc = x - mu
        var = _dot_hi(xc * xc, gm)
        rstd = lax.rsqrt(var + LN_EPS)
        nrm = xc * rstd
        a = nrm * g_ref[...] + b_ref[...]
        sa = _sigmoid(a)
        da = dy_ref[...] * sa * (1.0 + a * (1.0 - sa))
        dg_ref[...] += jnp.sum(da * nrm, axis=0, keepdims=True)
        db_ref[...] += jnp.sum(da, axis=0, keepdims=True)
        dn = da * g_ref[...]
        dc_ref[...] = rstd * (dn - _dot_hi(dn, gm) - nrm * _dot_hi(dn * nrm, gm))

    row = pl.BlockSpec((tm, c), lambda i: (i, 0))
    vec = pl.BlockSpec((1, c), lambda i: (0, 0))
    return pl.pallas_call(
        body, name=name, grid=(t // tm,),
        in_specs=[row, vec, vec, row], out_specs=[row, vec, vec],
        out_shape=[jax.ShapeDtypeStruct((t, c), F32), jax.ShapeDtypeStruct((1, c), F32),
                   jax.ShapeDtypeStruct((1, c), F32)],
        compiler_params=_cparams(("arbitrary",)),
    )(cx, gamma, beta, dy)


FOX_COL = 8
SMALL_BLK = P_SMALL // LANES


def _log_sigmoid(x):
    return jnp.minimum(x, 0.0) - jnp.log(1.0 + jnp.exp(-jnp.abs(x)))


def _fox_cols(shape):
    col = lax.broadcasted_iota(jnp.int32, shape, 1)
    return (col >= FOX_COL) & (col < FOX_COL + GROUP_HEADS)


def fox_gate_fwd(proj, bvec, *, name):
    t = proj.shape[0]
    tm = _tile(t, 256, 8)

    def body(s_ref, b_ref, o_ref, carry):
        @pl.when(pl.program_id(0) == 0)
        def _():
            carry[...] = jnp.zeros_like(carry)

        lf = jnp.where(_fox_cols((tm, LANES)), _log_sigmoid(s_ref[...] + b_ref[...]), 0.0)
        ri = lax.broadcasted_iota(jnp.int32, (tm, tm), 0)
        ci = lax.broadcasted_iota(jnp.int32, (tm, tm), 1)
        cum = _dot_hi(jnp.where(ri >= ci, 1.0, 0.0).astype(F32), lf) + carry[...]
        o_ref[...] = cum
        carry[...] = cum[tm - 1:tm, :]

    return pl.pallas_call(
        body, name=name, grid=(t // tm,),
        in_specs=[pl.BlockSpec((tm, LANES), lambda i: (i, SMALL_BLK)), pl.BlockSpec((1, LANES), lambda i: (0, 0))],
        out_specs=pl.BlockSpec((tm, LANES), lambda i: (i, 0)),
        out_shape=jax.ShapeDtypeStruct((t, LANES), F32),
        scratch_shapes=[pltpu.VMEM((1, LANES), F32)],
        compiler_params=_cparams(("arbitrary",)),
    )(proj, bvec)


def fox_gate_bwd(dcum, proj, bvec, *, name):
    t = proj.shape[0]
    tm = _tile(t, 256, 8)
    n = t // tm

    def body(d_ref, s_ref, b_ref, o_ref, db_ref, carry):
        @pl.when(pl.program_id(0) == 0)
        def _():
            carry[...] = jnp.zeros_like(carry)
            db_ref[...] = jnp.zeros_like(db_ref)

        ri = lax.broadcasted_iota(jnp.int32, (tm, tm), 0)
        ci = lax.broadcasted_iota(jnp.int32, (tm, tm), 1)
        dlf = _dot_hi(jnp.where(ri <= ci, 1.0, 0.0).astype(F32), d_ref[...]) + carry[...]
        carry[...] = dlf[0:1, :]
        x = s_ref[...] + b_ref[...]
        dx = jnp.where(_fox_cols((tm, LANES)), dlf * (1.0 - _sigmoid(x)), 0.0)
        o_ref[...] = dx
        db_ref[...] += jnp.sum(dx, axis=0, keepdims=True)

    return pl.pallas_call(
        body, name=name, grid=(n,),
        in_specs=[pl.BlockSpec((tm, LANES), lambda i: (n - 1 - i, 0)),
                  pl.BlockSpec((tm, LANES), lambda i: (n - 1 - i, SMALL_BLK)),
                  pl.BlockSpec((1, LANES), lambda i: (0, 0))],
        out_specs=[pl.BlockSpec((tm, LANES), lambda i: (n - 1 - i, 0)), pl.BlockSpec((1, LANES), lambda i: (0, 0))],
        out_shape=[jax.ShapeDtypeStruct((t, LANES), F32), jax.ShapeDtypeStruct((1, LANES), F32)],
        scratch_shapes=[pltpu.VMEM((1, LANES), F32)],
        compiler_params=_cparams(("arbitrary",)),
    )(dcum, proj, bvec)


def _head_masks(c):
    lane_head = lax.broadcasted_iota(jnp.int32, (1, c), 1) // HEAD_DIM
    return [lane_head == h for h in range(GROUP_HEADS)]


def _attn_tiles(t, tq, tk):
    tq = _tile(t, tq, 8)
    tk = _tile(t, tk, LANES)
    return tq, tk, t // tq, t // tk


def _grid_ends(*sizes):
    first = lambda: functools.reduce(lambda a, b: a & b, [pl.program_id(d) == 0 for d in range(len(sizes))])
    last = lambda: functools.reduce(lambda a, b: a & b, [pl.program_id(d) == s - 1 for d, s in enumerate(sizes)])
    return first, last


EXP_DEAD = -110.0


def _key_norm_max(k_ref, nk, tk, masks):
    lane = lax.broadcasted_iota(jnp.int32, (1, LANES), 1)

    def one(jt, km):
        kb = k_ref[pl.ds(pl.multiple_of(jt * tk, tk), tk), :].astype(MXU_DT).astype(F32)
        sq = kb * kb
        for h in range(GROUP_HEADS):
            top = jnp.max(jnp.sum(jnp.where(masks[h], sq, 0.0), axis=-1, keepdims=True))
            km = jnp.where(lane == h, jnp.maximum(km, top), km)
        return km

    return lax.fori_loop(0, nk, one, jnp.zeros((1, LANES), F32))


def _fox_reach(qh, km, cc_ref, scale):
    out = []
    for h in range(GROUP_HEADS):
        qf = qh[h].astype(F32)
        qn = jnp.sqrt(jnp.sum(qf * qf, axis=-1, keepdims=True))
        out.append(scale * 1.001 * qn * jnp.sqrt(km[:, h:h + 1]) + cc_ref[:, FOX_COL + h:FOX_COL + h + 1])
    return out


def _fox_alive(reach, top, cr_ref, j, tk):
    ends = cr_ref[jnp.maximum(j, 0)][:, tk - 1:tk]
    worst = jnp.float32(NEG_BIG)
    for h in range(GROUP_HEADS):
        worst = jnp.maximum(worst, jnp.max(reach[h] - top[h]) - jnp.max(ends[h:h + 1, :]))
    return (worst > EXP_DEAD).astype(jnp.int32)


def fox_fwd(proj, cum, cum_t, *, name, tq=512, tk=512, comm=None):
    t = proj.shape[0]
    c = GROUP_WIDTH
    tq, tk, nq, nk = _attn_tiles(t, tq, tk)
    assert tq == tk
    qb = P_FOX // c
    scale = HEAD_DIM ** -0.5
    cr3 = cum_t.reshape(8, nk, tk).transpose(1, 0, 2)

    def body(q_ref, k_ref, v_ref, cc_ref, cr_ref, o_ref, lse_ref, m_scr, l_scr, acc_scr, km_scr):
        i = pl.program_id(0)
        masks = _head_masks(c)

        @pl.when(i == 0)
        def _():
            km_scr[...] = _key_norm_max(k_ref, nk, tk, masks)

        q = q_ref[...]
        qh = [jnp.where(masks[h], q, 0.0).astype(MXU_DT) for h in range(GROUP_HEADS)]
        reach = _fox_reach(qh, km_scr[...], cc_ref, scale)
        m_scr[...] = jnp.full_like(m_scr, NEG_BIG)
        l_scr[...] = jnp.zeros_like(l_scr)
        acc_scr[...] = jnp.zeros_like(acc_scr)

        def tile(j, diagonal):
            rows = pl.ds(pl.multiple_of(j * tk, tk), tk)
            kb = k_ref[rows, :].astype(MXU_DT)
            vb = v_ref[rows, :].astype(MXU_DT)
            crj = cr_ref[j]
            if diagonal:
                causal = (lax.broadcasted_iota(jnp.int32, (tq, tk), 1) <= lax.broadcasted_iota(jnp.int32, (tq, tk), 0))
            acc = acc_scr[...]
            for h in range(GROUP_HEADS):
                s = _dot_nt(qh[h], kb) * scale + (cc_ref[:, FOX_COL + h:FOX_COL + h + 1] - crj[h:h + 1, :])
                if diagonal:
                    s = jnp.where(causal, s, NEG_BIG)
                m_old = m_scr[h]
                m_new = jnp.maximum(m_old, jnp.max(s, axis=-1, keepdims=True))
                p = jnp.exp(s - m_new)
                alpha = jnp.exp(m_old - m_new)
                l_scr[h] = alpha * l_scr[h] + jnp.sum(p, axis=-1, keepdims=True)
                m_scr[h] = m_new
                acc = jnp.where(masks[h], alpha * acc + _dot(p.astype(MXU_DT), vb), acc)
            acc_scr[...] = acc

        def alive(j):
            return _fox_alive(reach, [m_scr[h] for h in range(GROUP_HEADS)], cr_ref, j, tk)

        def step(state):
            j = i - state[0]
            tile(j, False)
            return state[0] + 1, alive(j - 1)

        tile(i, True)
        lax.while_loop(lambda s: (s[0] <= i) & (s[1] > 0), step, (jnp.int32(1), alive(i - 1)))
        acc = acc_scr[...]
        o = jnp.zeros_like(acc)
        lse = jnp.zeros((tq, LANES), F32)
        lane = lax.broadcasted_iota(jnp.int32, (1, LANES), 1)
        for h in range(GROUP_HEADS):
            o = jnp.where(masks[h], acc / l_scr[h], o)
            lse = jnp.where(lane == h, m_scr[h] + jnp.log(l_scr[h]), lse)
        o_ref[...] = o
        lse_ref[...] = lse

    resident = lambda blk: pl.BlockSpec((t, c), lambda i: (0, blk), pipeline_mode=pl.Buffered(1))
    call = dict(
        name=name, grid=(nq,),
        in_specs=[pl.BlockSpec((tq, c), lambda i: (i, qb)), resident(qb + 1), resident(qb + 2),
                  pl.BlockSpec((tq, LANES), lambda i: (i, 0)),
                  pl.BlockSpec((nk, 8, tk), lambda i: (0, 0, 0), pipeline_mode=pl.Buffered(1))],
        out_specs=[pl.BlockSpec((tq, c), lambda i: (i, 0)), pl.BlockSpec((tq, LANES), lambda i: (i, 0))],
        out_shape=[jax.ShapeDtypeStruct((t, c), F32), jax.ShapeDtypeStruct((t, LANES), F32)],
        scratch_shapes=[pltpu.VMEM((GROUP_HEADS, tq, 1), F32), pltpu.VMEM((GROUP_HEADS, tq, 1), F32),
                        pltpu.VMEM((tq, c), F32), pltpu.VMEM((1, LANES), F32)],
        compiler_params=_cparams(("arbitrary",)),
    )
    outs, got = carry_comm(call, body, (proj, proj, proj, cum, cr3), comm, 2, *_grid_ends(nq))
    return (*outs, got)


def fox_bwd(proj, cum, cum_t, o, lse, do, *, name, tq=512, tk=512, comm=None):
    t = proj.shape[0]
    c = GROUP_WIDTH
    tq, tk, nq, nk = _attn_tiles(t, tq, tk)
    assert tq == tk
    qb = P_FOX // c
    scale = HEAD_DIM ** -0.5
    cr3 = cum_t.reshape(8, nk, tk).transpose(1, 0, 2)

    def body(q_ref, k_ref, v_ref, cc_ref, cr_ref, o_ref, lse_ref, do_ref,
             dq_ref, dk_hbm, dv_hbm, dcc_ref, dcr_ref, dq_scr, rs_scr, dk_scr, dv_scr, km_scr):
        i = pl.program_id(0)
        masks = _head_masks(c)

        @pl.when(i == 0)
        def _():
            dk_scr[...] = jnp.zeros_like(dk_scr)
            dv_scr[...] = jnp.zeros_like(dv_scr)
            dcr_ref[...] = jnp.zeros_like(dcr_ref)
            km_scr[...] = _key_norm_max(k_ref, nk, tk, masks)

        q = q_ref[...]
        qf = q.astype(MXU_DT)
        qh = [jnp.where(masks[h], q, 0.0).astype(MXU_DT) for h in range(GROUP_HEADS)]
        do = do_ref[...]
        dob = do.astype(MXU_DT)
        doh = [jnp.where(masks[h], do, 0.0).astype(MXU_DT) for h in range(GROUP_HEADS)]
        doo = do * o_ref[...]
        delta = [jnp.sum(jnp.where(masks[h], doo, 0.0), axis=-1, keepdims=True) for h in range(GROUP_HEADS)]
        lse_h = [lse_ref[:, h:h + 1] for h in range(GROUP_HEADS)]
        reach = _fox_reach(qh, km_scr[...], cc_ref, scale)
        dq_scr[...] = jnp.zeros_like(dq_scr)
        rs_scr[...] = jnp.zeros_like(rs_scr)

        def tile(j, diagonal):
            rows = pl.ds(pl.multiple_of(j * tk, tk), tk)
            kb = k_ref[rows, :].astype(MXU_DT)
            vb = v_ref[rows, :].astype(MXU_DT)
            crj = cr_ref[j]
            if diagonal:
                causal = (lax.broadcasted_iota(jnp.int32, (tq, tk), 1) <= lax.broadcasted_iota(jnp.int32, (tq, tk), 0))
            dq = dq_scr[...]
            dk_upd = jnp.zeros((tk, c), F32)
            dv_upd = jnp.zeros((tk, c), F32)
            for h in range(GROUP_HEADS):
                s = _dot_nt(qh[h], kb) * scale + (cc_ref[:, FOX_COL + h:FOX_COL + h + 1] - crj[h:h + 1, :])
                p = jnp.exp(s - lse_h[h])
                if diagonal:
                    p = jnp.where(causal, p, 0.0)
                ds = p * (_dot_nt(doh[h], vb) - delta[h])
                dsb = ds.astype(MXU_DT)
                dq = jnp.where(masks[h], dq + _dot(dsb, kb) * scale, dq)
                dk_upd = jnp.where(masks[h], _dot_tn(dsb, qf) * scale, dk_upd)
                dv_upd = jnp.where(masks[h], _dot_tn(p.astype(MXU_DT), dob), dv_upd)
                dcr_ref[j, h:h + 1, :] += -jnp.sum(ds, axis=0, keepdims=True)
                rs_scr[h] += jnp.sum(ds, axis=-1, keepdims=True)
            dq_scr[...] = dq
            dk_scr[rows, :] += dk_upd
            dv_scr[rows, :] += dv_upd

        def alive(j):
            return _fox_alive(reach, lse_h, cr_ref, j, tk)

        def step(state):
            j = i - state[0]
            tile(j, False)
            return state[0] + 1, alive(j - 1)

        tile(i, True)
        lax.while_loop(lambda s: (s[0] <= i) & (s[1] > 0), step, (jnp.int32(1), alive(i - 1)))
        dq_ref[...] = dq_scr[...]
        lane = lax.broadcasted_iota(jnp.int32, (1, LANES), 1)
        dcc = jnp.zeros((tq, LANES), F32)
        for h in range(GROUP_HEADS):
            dcc = jnp.where(lane == FOX_COL + h, rs_scr[h], dcc)
        dcc_ref[...] = dcc

        @pl.when(i == nq - 1)
        def _():
            pltpu.sync_copy(dk_scr, dk_hbm)
            pltpu.sync_copy(dv_scr, dv_hbm)

    qrow = lambda i: (i, 0)
    resident = lambda blk: pl.BlockSpec((t, c), lambda i: (0, blk), pipeline_mode=pl.Buffered(1))
    hbm = pl.BlockSpec(memory_space=pl.ANY)
    call = dict(
        name=name, grid=(nq,),
        in_specs=[pl.BlockSpec((tq, c), lambda i: (i, qb)), resident(qb + 1), resident(qb + 2),
                  pl.BlockSpec((tq, LANES), qrow),
                  pl.BlockSpec((nk, 8, tk), lambda i: (0, 0, 0), pipeline_mode=pl.Buffered(1)),
                  pl.BlockSpec((tq, c), qrow), pl.BlockSpec((tq, LANES), qrow), pl.BlockSpec((tq, c), qrow)],
        out_specs=[pl.BlockSpec((tq, c), qrow), hbm, hbm, pl.BlockSpec((tq, LANES), qrow),
                   pl.BlockSpec((nk, 8, tk), lambda i: (0, 0, 0))],
        out_shape=[jax.ShapeDtypeStruct((t, c), F32), jax.ShapeDtypeStruct((t, c), F32),
                   jax.ShapeDtypeStruct((t, c), F32), jax.ShapeDtypeStruct((t, LANES), F32),
                   jax.ShapeDtypeStruct((nk, 8, tk), F32)],
        scratch_shapes=[pltpu.VMEM((tq, c), F32), pltpu.VMEM((GROUP_HEADS, tq, 1), F32),
                        pltpu.VMEM((t, c), F32), pltpu.VMEM((t, c), F32), pltpu.VMEM((1, LANES), F32)],
        compiler_params=_cparams(("arbitrary",)),
    )
    outs, got = carry_comm(call, body, (proj, proj, proj, cum, cr3, o, lse, do), comm, 5, *_grid_ends(nq))
    return (*outs, got)


SB_DEAD = -110.0


def _sb_logs(z, strict):
    tt = jnp.log(1.0 + jnp.exp(-jnp.abs(z)))
    log_keep = jnp.where(strict, -(jnp.maximum(z, 0.0) + tt), 0.0)
    log_beta = jnp.minimum(z, 0.0) - tt
    return log_keep, log_beta


def _tri(n, upper):
    a = lax.broadcasted_iota(jnp.int32, (n, n), 0)
    b = lax.broadcasted_iota(jnp.int32, (n, n), 1)
    return jnp.where((a < b) if upper else (a > b), 1.0, 0.0).astype(MXU_DT)


def _sb_carry_lane(jj, h):
    return GROUP_HEADS * jj + h


def sb_fwd(proj, *, name, tq=512, tk=256):
    t = proj.shape[0]
    c = GROUP_WIDTH
    tq, tk, nq, nk = _attn_tiles(t, tq, tk)
    assert nk * GROUP_HEADS <= LANES
    qb = P_SB // c
    scale = HEAD_DIM ** -0.5

    def body(q_ref, k_ref, v_ref, o_ref, rs_ref, r_scr, acc_scr):
        i = pl.program_id(0)
        last = ((i + 1) * tq - 1) // tk
        masks = _head_masks(c)
        q = q_ref[...]
        qh = [jnp.where(masks[h], q, 0.0).astype(MXU_DT) for h in range(GROUP_HEADS)]
        lane = lax.broadcasted_iota(jnp.int32, (1, LANES), 1)
        later = _tri(tk, upper=False)
        r_scr[...] = jnp.zeros_like(r_scr)
        acc_scr[...] = jnp.zeros_like(acc_scr)
        rs_ref[...] = jnp.full((tq, LANES), 2.0 * SB_DEAD, F32)

        def step(state):
            jj, _ = state
            j = last - jj
            rows = pl.ds(pl.multiple_of(j * tk, tk), tk)
            kb = k_ref[rows, :].astype(MXU_DT)
            vb = v_ref[rows, :].astype(MXU_DT)
            row = i * tq + lax.broadcasted_iota(jnp.int32, (tq, tk), 0)
            col = j * tk + lax.broadcasted_iota(jnp.int32, (tq, tk), 1)
            strict = col < row
            acc = acc_scr[...]
            rs = rs_ref[...]
            for h in range(GROUP_HEADS):
                z = _dot_nt(qh[h], kb) * scale
                log_keep, log_beta = _sb_logs(z, strict)
                r_old = r_scr[h]
                rs = jnp.where(lane == _sb_carry_lane(jj, h), r_old, rs)
                rest = r_old + _split_dot(log_keep, later)
                w = jnp.where(strict, jnp.exp(log_beta + rest), 0.0)
                acc = jnp.where(masks[h], acc + _dot(w.astype(MXU_DT), vb), acc)
                r_scr[h] = r_old + jnp.sum(log_keep, axis=-1, keepdims=True)
            acc_scr[...] = acc
            rs_ref[...] = rs
            return jj + 1, jnp.max(r_scr[...])

        lax.while_loop(lambda s: (s[0] <= last) & (s[1] > SB_DEAD), step, (jnp.int32(0), jnp.float32(0.0)))
        o_ref[...] = acc_scr[...]

    resident = lambda blk: pl.BlockSpec((t, c), lambda i: (0, blk), pipeline_mode=pl.Buffered(1))
    return pl.pallas_call(
        body, name=name, grid=(nq,),
        in_specs=[pl.BlockSpec((tq, c), lambda i: (i, qb)), resident(qb + 1), resident(qb + 2)],
        out_specs=[pl.BlockSpec((tq, c), lambda i: (i, 0)), pl.BlockSpec((tq, LANES), lambda i: (i, 0))],
        out_shape=[jax.ShapeDtypeStruct((t, c), F32), jax.ShapeDtypeStruct((t, LANES), F32)],
        scratch_shapes=[pltpu.VMEM((GROUP_HEADS, tq, 1), F32), pltpu.VMEM((tq, c), F32)],
        compiler_params=_cparams(("arbitrary",)),
    )(proj, proj, proj)


def sb_bwd(proj, rsave, do, *, name, tq=512, tk=256, comm=None):
    t = proj.shape[0]
    c = GROUP_WIDTH
    tq, tk, nq, nk = _attn_tiles(t, tq, tk)
    qb = P_SB // c
    scale = HEAD_DIM ** -0.5

    def body(q_ref, k_ref, v_ref, rs_ref, do_ref, dq_ref, dk_hbm, dv_hbm, e_scr, dq_scr, dk_scr, dv_scr):
        i = pl.program_id(0)
        last = ((i + 1) * tq - 1) // tk
        masks = _head_masks(c)

        @pl.when(i == 0)
        def _():
            dk_scr[...] = jnp.zeros_like(dk_scr)
            dv_scr[...] = jnp.zeros_like(dv_scr)

        e_scr[...] = jnp.zeros_like(e_scr)
        dq_scr[...] = jnp.zeros_like(dq_scr)
        q = q_ref[...]
        qf = q.astype(MXU_DT)
        qh = [jnp.where(masks[h], q, 0.0).astype(MXU_DT) for h in range(GROUP_HEADS)]
        do = do_ref[...]
        dob = do.astype(MXU_DT)
        doh = [jnp.where(masks[h], do, 0.0).astype(MXU_DT) for h in range(GROUP_HEADS)]
        later = _tri(tk, upper=False)
        earlier = _tri(tk, upper=True)
        rs = rs_ref[...]
        lane = lax.broadcasted_iota(jnp.int32, (1, LANES), 1)
        visited = jnp.where(jnp.max(rs, axis=0, keepdims=True) > SB_DEAD, (lane // GROUP_HEADS + 1).astype(F32), 0.0)
        n_visited = jnp.minimum(jnp.max(visited).astype(jnp.int32), last + 1)

        def step(it, carry):
            jj = n_visited - 1 - it
            j = last - jj
            rows = pl.ds(pl.multiple_of(j * tk, tk), tk)
            kb = k_ref[rows, :].astype(MXU_DT)
            vb = v_ref[rows, :].astype(MXU_DT)
            row = i * tq + lax.broadcasted_iota(jnp.int32, (tq, tk), 0)
            col = j * tk + lax.broadcasted_iota(jnp.int32, (tq, tk), 1)
            strict = col < row
            dq = dq_scr[...]
            dk_upd = jnp.zeros((tk, c), F32)
            dv_upd = jnp.zeros((tk, c), F32)
            for h in range(GROUP_HEADS):
                z = _dot_nt(qh[h], kb) * scale
                log_keep, log_beta = _sb_logs(z, strict)
                r_h = jnp.sum(jnp.where(lane == _sb_carry_lane(jj, h), rs, 0.0), axis=-1, keepdims=True)
                rest = r_h + _split_dot(log_keep, later)
                w = jnp.where(strict, jnp.exp(log_beta + rest), 0.0)
                e = w * _dot_nt(doh[h], vb)
                e_old = e_scr[h]
                dkeep = e_old + _split_dot(e, earlier)
                dz = jnp.where(strict, e * jnp.exp(log_keep) - dkeep * jnp.exp(log_beta), 0.0)
                dzb = dz.astype(MXU_DT)
                dq = jnp.where(masks[h], dq + _dot(dzb, kb) * scale, dq)
                dk_upd = jnp.where(masks[h], _dot_tn(dzb, qf) * scale, dk_upd)
                dv_upd = jnp.where(masks[h], _dot_tn(w.astype(MXU_DT), dob), dv_upd)
                e_scr[h] = e_old + jnp.sum(e, axis=-1, keepdims=True)
            dq_scr[...] = dq
            dk_scr[rows, :] += dk_upd
            dv_scr[rows, :] += dv_upd
            return carry

        lax.fori_loop(0, n_visited, step, 0)
        dq_ref[...] = dq_scr[...]

        @pl.when(i == nq - 1)
        def _():
            pltpu.sync_copy(dk_scr, dk_hbm)
            pltpu.sync_copy(dv_scr, dv_hbm)

    qrow = lambda i: (i, 0)
    resident = lambda blk: pl.BlockSpec((t, c), lambda i: (0, blk), pipeline_mode=pl.Buffered(1))
    hbm = pl.BlockSpec(memory_space=pl.ANY)
    call = dict(
        name=name, grid=(nq,),
        in_specs=[pl.BlockSpec((tq, c), lambda i: (i, qb)), resident(qb + 1), resident(qb + 2),
                  pl.BlockSpec((tq, LANES), qrow), pl.BlockSpec((tq, c), qrow)],
        out_specs=[pl.BlockSpec((tq, c), qrow), hbm, hbm],
        out_shape=[jax.ShapeDtypeStruct((t, c), F32)] * 3,
        scratch_shapes=[pltpu.VMEM((GROUP_HEADS, tq, 1), F32), pltpu.VMEM((tq, c), F32),
                        pltpu.VMEM((t, c), F32), pltpu.VMEM((t, c), F32)],
        compiler_params=_cparams(("arbitrary",)),
    )
    outs, got = carry_comm(call, body, (proj, proj, proj, rsave, do), comm, 3, *_grid_ends(nq))
    return (*outs, got)


A_COL, B_COL = 0, 4
Z_BLK = P_Z // GROUP_WIDTH
GDN_CHUNKS_PER_STEP = 4


NN = (((1,), (0,)), ((), ()))
NT = (((1,), (1,)), ((), ()))
TN = (((0,), (0,)), ((), ()))


def _terms(x, n):
    out, rem = [], x
    for _ in range(n):
        t = rem.astype(MXU_DT)
        out.append(t)
        rem = rem - t.astype(F32)
    return out


def _dotp(a, b, dims, a_terms=2, b_terms=2):
    at, bt = _terms(a, a_terms), _terms(b, b_terms)
    out = None
    for i, x in enumerate(at):
        for j, y in enumerate(bt):
            if i + j < max(a_terms, b_terms):
                r = lax.dot_general(x, y, dims, preferred_element_type=F32)
                out = r if out is None else out + r
    return out


def _silu(x):
    return x * _sigmoid(x)


def _dsilu(x):
    s = _sigmoid(x)
    return s * (1.0 + x * (1.0 - s))


def _head_sum(x, masks):
    out = jnp.zeros_like(x)
    for m in masks:
        out = jnp.where(m, jnp.sum(jnp.where(m, x, 0.0), axis=-1, keepdims=True), out)
    return out


def _expand(cols, col0, masks):
    out = jnp.zeros((cols.shape[0], GROUP_WIDTH), F32)
    for h, m in enumerate(masks):
        out = jnp.where(m, cols[:, col0 + h:col0 + h + 1], out)
    return out


def _reduce(x, col0, masks):
    lane = lax.broadcasted_iota(jnp.int32, (1, LANES), 1)
    out = jnp.zeros((x.shape[0], LANES), F32)
    for h, m in enumerate(masks):
        out = jnp.where(lane == col0 + h, jnp.sum(jnp.where(m, x, 0.0), axis=-1, keepdims=True), out)
    return out


def _block_ones():
    ri = lax.broadcasted_iota(jnp.int32, (GROUP_WIDTH, GROUP_WIDTH), 0) // HEAD_DIM
    ci = lax.broadcasted_iota(jnp.int32, (GROUP_WIDTH, GROUP_WIDTH), 1) // HEAD_DIM
    return jnp.where(ri == ci, 1.0, 0.0).astype(F32)


def _blk(x, hs):
    return jnp.concatenate([x] * GROUP_HEADS, axis=0) * hs


def _unblk(m, hs):
    mm = m * hs
    c = GDN_CHUNK
    return mm[0:c] + mm[c:2 * c] + mm[2 * c:3 * c] + mm[3 * c:4 * c]


def _row_mask4():
    ri = lax.broadcasted_iota(jnp.int32, (GROUP_WIDTH, LANES), 0) // HEAD_DIM
    ci = lax.broadcasted_iota(jnp.int32, (GROUP_WIDTH, LANES), 1)
    return jnp.where(ri + A_COL == ci, 1.0, 0.0).astype(F32)


def _lockstep(gens):
    results = [None] * len(gens)
    live = list(range(len(gens)))
    while live:
        for i in list(live):
            try:
                next(gens[i])
            except StopIteration as stop:
                results[i] = stop.value
                live.remove(i)
    return results


def _gdn_chunk(xc, small, avec, dtvec, state, masks, hs):
    (f,) = _lockstep([_gdn_local(xc, small, avec, dtvec, masks, hs)])
    return _gdn_recur(f, state, hs)


def _gdn_local(xc, small, avec, dtvec, masks, hs):
    c = GDN_CHUNK
    w = GROUP_WIDTH
    b16 = lambda v: v.astype(MXU_DT)
    f = {}
    xq, xk, xv = xc[:, :w], xc[:, w:2 * w], xc[:, 2 * w:]
    qs, ks, v = _silu(xq), _silu(xk), _silu(xv)
    rq = lax.rsqrt(_head_sum(qs * qs, masks) + L2_EPS)
    rk = lax.rsqrt(_head_sum(ks * ks, masks) + L2_EPS)
    qn = qs * rq
    k = ks * rk
    q = qn * (HEAD_DIM ** -0.5)
    xg = small + dtvec
    sp = jnp.maximum(xg, 0.0) + jnp.log(1.0 + jnp.exp(-jnp.abs(xg)))
    g128 = -avec * sp
    beta128 = _sigmoid(small)
    ri = lax.broadcasted_iota(jnp.int32, (c, c), 0)
    ci = lax.broadcasted_iota(jnp.int32, (c, c), 1)
    tril = jnp.where(ri >= ci, 1.0, 0.0).astype(F32)
    gam128 = _dotp(tril, g128, NN, 1, 3)
    yield
    gam = _expand(gam128, A_COL, masks)
    bfull = _expand(beta128, B_COL, masks)
    mask4 = _row_mask4()
    ones = jnp.ones((c, LANES), F32)
    gam_row = _dotp(ones, jnp.concatenate([gam128] * GROUP_HEADS, axis=0) * mask4, NT, 1, 3)
    yield
    li = lax.broadcasted_iota(jnp.int32, (c, w), 0)
    lj = lax.broadcasted_iota(jnp.int32, (c, w), 1) % HEAD_DIM
    incl = li >= lj
    strict = li > lj
    dmat = jnp.exp(jnp.where(incl, gam - gam_row, NEG_BIG))
    egam = jnp.exp(gam)
    glast = gam[c - 1:c, :]
    ekd = jnp.exp(glast - gam)
    kb = k * bfull
    vb = v * bfull
    kbg = kb * egam
    qd = q * egam
    kd = k * ekd
    kblk = b16(_blk(k, hs))
    araw = _dot_nt(b16(kb), kblk)
    qk = _dot_nt(b16(q), kblk)
    yield
    a = jnp.where(strict, araw * dmat, 0.0)
    tm = jnp.where(li == lj, 1.0, 0.0) - a
    p = a
    for _ in range(5):
        p = _dotp(p, _blk(p, hs), NN)
        yield
        tm = tm + _dotp(tm, _blk(p, hs), NN)
        yield
    tm16 = b16(tm)
    u = _dot(tm16, b16(_blk(vb, hs)))
    wm = _dot(tm16, b16(_blk(kbg, hs)))
    aqk = jnp.where(incl, qk * dmat, 0.0)
    f.update(xq=xq, xk=xk, xv=xv, v=v, rq=rq, rk=rk, qn=qn, k=k, q=q, xg=xg, g128=g128, beta128=beta128,
             tril=tril, gam=gam, bfull=bfull, mask4=mask4, ones=ones, incl=incl, strict=strict, li=li,
             dmat=dmat, egam=egam, glast=glast, ekd=ekd, kb=kb, vb=vb, kbg=kbg, qd=qd, kd=kd, kblk=kblk,
             araw=araw, tm=tm, tm16=tm16, wm=wm, qk=qk, aqk=aqk, u=u)
    return f


def _gdn_recur(f, state, hs):
    b16 = lambda v: v.astype(MXU_DT)
    s16 = b16(state)
    vn = f["u"] - _dot(b16(f["wm"]), s16)
    o = _dot(b16(f["qd"]), s16) + _dot(b16(f["aqk"]), b16(_blk(vn, hs)))
    s_new = state * jnp.exp(f["glast"]) + hs * _dot_tn(b16(f["kd"]), b16(vn))
    f.update(s16=s16, vn=vn, o=o, s_new=s_new)
    return f


def _decay_rate(a_log):
    lane = lax.broadcasted_iota(jnp.int32, a_log.shape, 1)
    return jnp.where((lane >= A_COL) & (lane < A_COL + GROUP_HEADS), jnp.exp(a_log), 0.0)


def _gdn_post(o, z, ng, masks):
    r = lax.rsqrt(_head_sum(o * o, masks) * (1.0 / HEAD_DIM) + RMS_EPS)
    on = o * r
    return on, r, on * ng * _silu(z)


def gdn_fwd(cqkv, proj, avec, dtvec, ng, *, name, comm=None):
    t = cqkv.shape[0]
    c = GDN_CHUNK
    w = GROUP_WIDTH
    n = t // c

    def body(x_ref, z_ref, sm_ref, a_ref, dt_ref, ng_ref, y_ref, st_ref, s_scr):
        @pl.when(pl.program_id(0) == 0)
        def _():
            s_scr[...] = jnp.zeros_like(s_scr)

        masks = _head_masks(w)
        hs = _block_ones()
        avec_v = _decay_rate(a_ref[...])
        rows = [pl.ds(k * c, c) for k in range(sub)]
        fs = _lockstep([_gdn_local(x_ref[r, :], sm_ref[r, :], avec_v, dt_ref[...], masks, hs) for r in rows])
        state = s_scr[...]
        for k, r in enumerate(rows):
            st_ref[k] = state
            f = _gdn_recur(fs[k], state, hs)
            _, _, y = _gdn_post(f["o"], z_ref[r, :], ng_ref[...], masks)
            y_ref[r, :] = y
            state = f["s_new"]
        s_scr[...] = state

    sub = GDN_CHUNKS_PER_STEP if n % GDN_CHUNKS_PER_STEP == 0 else 1
    rows, steps = c * sub, n // sub
    vec = pl.BlockSpec((1, LANES), lambda i: (0, 0))
    call = dict(
        name=name, grid=(steps,),
        in_specs=[pl.BlockSpec((rows, 3 * w), lambda i: (i, 0)),
                  pl.BlockSpec((rows, w), lambda i: (i, Z_BLK)),
                  pl.BlockSpec((rows, LANES), lambda i: (i, SMALL_BLK)),
                  vec, vec, pl.BlockSpec((1, w), lambda i: (0, 0))],
        out_specs=[pl.BlockSpec((rows, w), lambda i: (i, 0)), pl.BlockSpec((sub, w, w), lambda i: (i, 0, 0))],
        out_shape=[jax.ShapeDtypeStruct((t, w), F32), jax.ShapeDtypeStruct((n, w, w), F32)],
        scratch_shapes=[pltpu.VMEM((w, w), F32)],
        compiler_params=_cparams(("arbitrary",)),
    )
    outs, got = carry_comm(call, body, (cqkv, proj, proj, avec, dtvec, ng), comm, 2, *_grid_ends(steps))
    return (*outs, got)


def gdn_bwd(cqkv, proj, avec, dtvec, ng, states, dy, *, name, comm=None):
    t = cqkv.shape[0]
    c = GDN_CHUNK
    w = GROUP_WIDTH
    n = t // c
    b16 = lambda v: v.astype(MXU_DT)

    def body(x_ref, z_ref, sm_ref, a_ref, dt_ref, ng_ref, st_ref, dy_ref,
             dx_ref, dz_ref, dsm_ref, dng_ref, dal_ref, ddt_ref, ds_scr):
        @pl.when(pl.program_id(0) == 0)
        def _():
            ds_scr[...] = jnp.zeros_like(ds_scr)
            dng_ref[...] = jnp.zeros_like(dng_ref)
            dal_ref[...] = jnp.zeros_like(dal_ref)
            ddt_ref[...] = jnp.zeros_like(ddt_ref)

        masks = _head_masks(w)
        hs = _block_ones()
        avec_v = _decay_rate(a_ref[...])
        rows = [pl.ds(k * c, c) for k in range(sub)]
        fs = _lockstep([_gdn_local(x_ref[r, :], sm_ref[r, :], avec_v, dt_ref[...], masks, hs) for r in rows])
        fs = [_gdn_recur(f, st_ref[k], hs) for k, f in enumerate(fs)]
        _lockstep([chunk(fs[k], st_ref[k], avec_v, masks, hs, z_ref.at[r, :], ng_ref, dy_ref.at[r, :], dx_ref.at[r, :],
                         dz_ref.at[r, :], dsm_ref.at[r, :], dng_ref, dal_ref, ddt_ref, ds_scr)
                   for k, r in reversed(list(enumerate(rows)))])

    def chunk(f, state, avec_v, masks, hs, z_ref, ng_ref, dy_ref, dx_ref, dz_ref, dsm_ref, dng_ref, dal_ref, ddt_ref,
              ds_scr):
        z = z_ref[...]
        ng_v = ng_ref[...]
        dy_v = dy_ref[...]
        on, r, _ = _gdn_post(f["o"], z, ng_v, masks)
        sz = _silu(z)
        dz_ref[...] = dy_v * on * ng_v * _dsilu(z)
        d_on = dy_v * ng_v * sz
        dng_ref[...] += jnp.sum(dy_v * on * sz, axis=0, keepdims=True)
        do = r * (d_on - on * _head_sum(d_on * on, masks) * (1.0 / HEAD_DIM))
        do16 = b16(do)
        dsn = ds_scr[...]
        dsn16 = b16(dsn)
        s16, vn, kd, qd, wm = f["s16"], f["vn"], f["kd"], f["qd"], f["wm"]
        k, q, kblk, tm, tm16 = f["k"], f["q"], f["kblk"], f["tm"], f["tm16"]
        dmat, egam, glast, gam = f["dmat"], f["egam"], f["glast"], f["gam"]
        incl, strict, li = f["incl"], f["strict"], f["li"]
        vn16 = b16(vn)
        dvn = _unblk(_dot_tn(b16(f["aqk"]), do16), hs) + _dot(b16(kd), dsn16)
        daqk = jnp.where(incl, _dot_nt(do16, b16(_blk(vn, hs))), 0.0)
        dqd = _dot_nt(do16, s16)
        dvn16 = b16(dvn)
        ds_scr[...] = hs * (_dot_tn(b16(qd), do16) - _dot_tn(b16(wm), dvn16)) + dsn * jnp.exp(glast)
        yield
        dkd = _dot_nt(vn16, dsn16)
        dglast = jnp.sum(dsn * state, axis=0, keepdims=True) * jnp.exp(glast)
        du16 = dvn16
        dw16 = b16(-_dot_nt(dvn16, s16))
        yield
        dqk16 = b16(daqk * dmat)
        ddm = daqk * f["qk"]
        dq = _dot(dqk16, kblk)
        dk = _unblk(_dot_tn(dqk16, b16(q)), hs)
        dtm = _dot_nt(du16, b16(_blk(f["vb"], hs))) + _dot_nt(dw16, b16(_blk(f["kbg"], hs)))
        dvb = _unblk(_dot_tn(tm16, du16), hs)
        dkbg = _unblk(_dot_tn(tm16, dw16), hs)
        yield
        xx = _unblk(_dotp(tm, dtm, TN), hs)
        yield
        da = jnp.where(strict, -_dotp(xx, _blk(tm, hs), NT), 0.0)
        yield
        daraw16 = b16(da * dmat)
        ddm = ddm + da * f["araw"]
        dkb = _dot(daraw16, kblk)
        dk = dk + _unblk(_dot_tn(daraw16, b16(f["kb"])), hs)
        yield
        tcol = ddm * dmat
        dgam = tcol
        dgam128_row = _dotp(-tcol, f["ones"], TN, 2, 1) * f["mask4"]
        yield
        dgam128_row = (dgam128_row[0:c] + dgam128_row[c:2 * c] + dgam128_row[2 * c:3 * c] + dgam128_row[3 * c:4 * c])
        dk = dk + dkd * f["ekd"]
        tt = dkd * kd
        dgam = dgam - tt
        dglast = dglast + jnp.sum(tt, axis=0, keepdims=True)
        dq = dq + dqd * egam
        dgam = dgam + dqd * qd
        dkb = dkb + dkbg * egam
        dgam = dgam + dkbg * f["kbg"]
        dk = dk + dkb * f["bfull"]
        dbf = dkb * k + dvb * f["v"]
        dv = dvb * f["bfull"]
        dgam = dgam + jnp.where(li == c - 1, dglast, 0.0)
        beta128 = f["beta128"]
        db128 = _reduce(dbf, B_COL, masks) * beta128 * (1.0 - beta128)
        dgam128 = _reduce(dgam, A_COL, masks) + dgam128_row
        dg128 = _dotp(f["tril"], dgam128, TN, 1, 2)
        yield
        dxg = dg128 * (-avec_v * _sigmoid(f["xg"]))
        lane = lax.broadcasted_iota(jnp.int32, (1, LANES), 1)
        dsm_ref[...] = jnp.where(lane < B_COL, dxg, db128)
        ddt_ref[...] += jnp.sum(dxg, axis=0, keepdims=True)
        dal_ref[...] += jnp.sum(dg128 * f["g128"], axis=0, keepdims=True)
        dqn = dq * (HEAD_DIM ** -0.5)
        dqs = f["rq"] * (dqn - f["qn"] * _head_sum(dqn * f["qn"], masks))
        dks = f["rk"] * (dk - k * _head_sum(dk * k, masks))
        dx_ref[:, :w] = dqs * _dsilu(f["xq"])
        dx_ref[:, w:2 * w] = dks * _dsilu(f["xk"])
        dx_ref[:, 2 * w:] = dv * _dsilu(f["xv"])

    sub = GDN_CHUNKS_PER_STEP if n % GDN_CHUNKS_PER_STEP == 0 else 1
    rows, steps = c * sub, n // sub
    vec = pl.BlockSpec((1, LANES), lambda i: (0, 0))
    rev = lambda blk: (lambda i: (steps - 1 - i, blk))
    call = dict(
        name=name, grid=(steps,),
        in_specs=[pl.BlockSpec((rows, 3 * w), rev(0)),
                  pl.BlockSpec((rows, w), rev(Z_BLK)),
                  pl.BlockSpec((rows, LANES), rev(SMALL_BLK)),
                  vec, vec, pl.BlockSpec((1, w), lambda i: (0, 0)),
                  pl.BlockSpec((sub, w, w), lambda i: (steps - 1 - i, 0, 0)),
                  pl.BlockSpec((rows, w), rev(0))],
        out_specs=[pl.BlockSpec((rows, 3 * w), rev(0)), pl.BlockSpec((rows, w), rev(0)),
                   pl.BlockSpec((rows, LANES), rev(0)),
                   pl.BlockSpec((1, w), lambda i: (0, 0)), vec, vec],
        out_shape=[jax.ShapeDtypeStruct((t, 3 * w), F32), jax.ShapeDtypeStruct((t, w), F32),
                   jax.ShapeDtypeStruct((t, LANES), F32), jax.ShapeDtypeStruct((1, w), F32),
                   jax.ShapeDtypeStruct((1, LANES), F32), jax.ShapeDtypeStruct((1, LANES), F32)],
        scratch_shapes=[pltpu.VMEM((w, w), F32)],
        compiler_params=_cparams(("arbitrary",)),
    )
    outs, got = carry_comm(call, body, (cqkv, proj, proj, avec, dtvec, ng, states, dy), comm, 6, *_grid_ends(steps))
    return (*outs, got)


def adamw(w, m, v, gslots, *, row0=0, name):
    r, c = w.shape
    s = gslots.shape[0]
    tr = _tile(r, 64, 8)
    assert row0 % tr == 0 and gslots.shape[2] == c
    rb = row0 // tr
    c1 = 1.0 - ADAM_B1 ** ADAM_STEP
    c2 = 1.0 - ADAM_B2 ** ADAM_STEP

    def body(w_ref, m_ref, v_ref, gs_ref, g_ref, d_ref, mo_ref, vo_ref):
        g = gs_ref[0].astype(F32)
        for k in range(1, s):
            g = g + gs_ref[k].astype(F32)
        m_new = ADAM_B1 * m_ref[...] + (1.0 - ADAM_B1) * g
        v_new = ADAM_B2 * v_ref[...] + (1.0 - ADAM_B2) * (g * g)
        m_hat = m_new / c1
        v_hat = v_new / c2
        g_ref[...] = g
        mo_ref[...] = m_new
        vo_ref[...] = v_new
        d_ref[...] = -ADAM_LR * (m_hat / (jnp.sqrt(v_hat) + ADAM_EPS) + ADAM_WD * w_ref[...])

    row = pl.BlockSpec((tr, c), lambda i: (i, 0))
    return pl.pallas_call(
        body, name=name, grid=(r // tr,),
        in_specs=[row, row, row, pl.BlockSpec((s, tr, c), lambda i: (0, rb + i, 0))],
        out_specs=[row] * 4,
        out_shape=[jax.ShapeDtypeStruct((r, c), F32)] * 4,
        compiler_params=_cparams(("parallel",)),
    )(w, m, v, gslots)


def slot_sum(slots, *, name):
    s, r, c = slots.shape

    def body(s_ref, o_ref):
        acc = s_ref[0]
        for k in range(1, s):
            acc = acc + s_ref[k]
        o_ref[...] = acc

    return pl.pallas_call(
        body, name=name, grid=(1,),
        in_specs=[pl.BlockSpec((s, r, c), lambda i: (0, 0, 0))],
        out_specs=pl.BlockSpec((r, c), lambda i: (0, 0)),
        out_shape=jax.ShapeDtypeStruct((r, c), F32),
        compiler_params=_cparams(("arbitrary",)),
    )(slots)


class Comm:
    def __init__(self, srcs, broadcast):
        self.srcs = list(srcs)
        self.broadcast = [broadcast] * len(self.srcs) if isinstance(broadcast, bool) else list(broadcast)
        self.n = len(self.srcs)
        self.out_shapes = [jax.ShapeDtypeStruct(((N_DEV,) + s.shape) if b else s.shape, s.dtype)
                           for s, b in zip(self.srcs, self.broadcast)]
        self.sems = [pltpu.SemaphoreType.DMA((self.n,))] * 3

    def _local(self, src_refs, out_refs, loc_sem, a, me):
        src = src_refs[a] if self.broadcast[a] else src_refs[a].at[me]
        return pltpu.make_async_copy(src, out_refs[a].at[me], loc_sem.at[a])

    def start(self, src_refs, out_refs, send_sem, recv_sem, loc_sem):
        x, y, c = lax.axis_index("x"), lax.axis_index("y"), lax.axis_index("c")
        me = 4 * x + 2 * y + c
        for a in range(self.n):
            self._local(src_refs, out_refs, loc_sem, a, me).start()
        for d in range(1, N_DEV):
            px, py, pc = x ^ ((d >> 2) & 1), y ^ ((d >> 1) & 1), c ^ (d & 1)
            peer = 4 * px + 2 * py + pc
            for a in range(self.n):
                src = src_refs[a] if self.broadcast[a] else src_refs[a].at[peer]
                pltpu.make_async_remote_copy(
                    src_ref=src, dst_ref=out_refs[a].at[me],
                    send_sem=send_sem.at[a], recv_sem=recv_sem.at[a],
                    device_id=(px, py, pc), device_id_type=pl.DeviceIdType.MESH).start()

    def wait(self, src_refs, out_refs, send_sem, recv_sem, loc_sem):
        x, y, c = lax.axis_index("x"), lax.axis_index("y"), lax.axis_index("c")
        me = 4 * x + 2 * y + c
        for a in range(self.n):
            seven = out_refs[a].at[pl.ds(0, N_DEV - 1)]
            pltpu.make_async_remote_copy(
                src_ref=seven, dst_ref=seven, send_sem=send_sem.at[a], recv_sem=recv_sem.at[a],
                device_id=(x, y, c), device_id_type=pl.DeviceIdType.MESH).wait()
            self._local(src_refs, out_refs, loc_sem, a, me).wait()


def exchange(srcs, *, broadcast, name):
    comm = Comm(srcs, broadcast)
    n = comm.n

    def body(*refs):
        src_refs, out_refs, sems = refs[:n], refs[n:2 * n], refs[2 * n:]
        comm.start(src_refs, out_refs, *sems)
        comm.wait(src_refs, out_refs, *sems)

    anyspec = pl.BlockSpec(memory_space=pl.ANY)
    return pl.pallas_call(
        body, name=name,
        in_specs=[anyspec] * n, out_specs=[anyspec] * n, out_shape=comm.out_shapes,
        scratch_shapes=comm.sems,
        compiler_params=pltpu.CompilerParams(has_side_effects=True),
    )(*srcs)


def carry_comm(call_kwargs, body, args, comm, n_out, is_first, is_last):
    if comm is None:
        return pl.pallas_call(body, **call_kwargs)(*args), []
    n_in, nc = len(args), comm.n
    n_scr = len(call_kwargs["scratch_shapes"])
    anyspec = pl.BlockSpec(memory_space=pl.ANY)

    def wrapped(*refs):
        ins, csrc = refs[:n_in], refs[n_in:n_in + nc]
        outs = refs[n_in + nc:n_in + nc + n_out]
        cout = refs[n_in + nc + n_out:n_in + 2 * nc + n_out]
        rest = refs[n_in + 2 * nc + n_out:]
        scr, sems = rest[:n_scr], rest[n_scr:]

        @pl.when(is_first())
        def _():
            comm.start(csrc, cout, *sems)

        body(*ins, *outs, *scr)

        @pl.when(is_last())
        def _():
            comm.wait(csrc, cout, *sems)

    kw = dict(call_kwargs)
    kw["in_specs"] = list(kw["in_specs"]) + [anyspec] * nc
    kw["out_specs"] = list(kw["out_specs"]) + [anyspec] * nc
    kw["out_shape"] = list(kw["out_shape"]) + comm.out_shapes
    kw["scratch_shapes"] = list(kw["scratch_shapes"]) + comm.sems
    cp = kw["compiler_params"]
    kw["compiler_params"] = pltpu.CompilerParams(dimension_semantics=cp.dimension_semantics,
                                                 vmem_limit_bytes=cp.vmem_limit_bytes, has_side_effects=True)
    res = pl.pallas_call(wrapped, **kw)(*args, *comm.srcs)
    return res[:n_out], res[n_out:]


def _pack(arrs):
    flat = []
    for a in arrs:
        f = a.reshape(-1).astype(F32)
        flat.append(jnp.pad(f, (0, (-f.shape[0]) % LANES)))
    buf = jnp.concatenate(flat)
    buf = jnp.pad(buf, (0, (-buf.shape[0]) % (8 * LANES)))
    return buf.reshape(-1, LANES)


def _unpack(buf, shapes):
    flat = buf.reshape(-1)
    out, off = [], 0
    for s in shapes:
        sz = int(np.prod(s))
        out.append(flat[off:off + sz].reshape(s))
        off += sz + (-sz) % LANES
    return out


def _win_to_aligned(w):
    o = np.cumsum((0,) + IN_SPLITS)
    seg = lambda i: w[..., o[i]:o[i + 1]]
    pad = jnp.zeros(w.shape[:-1] + (P_WIDTH - IN_WIDTH,), w.dtype)
    return jnp.concatenate([seg(0), seg(1), seg(4), seg(6), seg(7), seg(2), seg(3), seg(5), pad], axis=-1)


def _win_from_aligned(w):
    o = np.cumsum((0,) + IN_SPLITS)
    s = P_SMALL
    return jnp.concatenate([w[..., P_GDN:P_GDN + 768], w[..., P_Z:P_Z + 256], w[..., s:s + 4], w[..., s + 4:s + 8],
                            w[..., P_FOX:P_FOX + 768], w[..., s + 8:s + 12], w[..., P_CONF:P_CONF + 512],
                            w[..., P_SB:P_SB + 768]], axis=-1)


def _row128(vals, col0):
    return jnp.pad(vals.astype(F32)[None, :], ((0, 0), (col0, LANES - col0 - GROUP_HEADS)))


def _ffn_fwd(x, x16, w, n, tag, comm=None, on_comm=None):
    gu = mm(x16, w[f"gu{n}"], name=f"{tag}_gu", tm=1024, tn=512, tk=1024, out_dtype=MXU_DT, comm=comm)
    if comm is not None:
        gu, got = gu
        on_comm(got)
    h = act_fwd(gu, name=f"{tag}_act")
    y = mm(h, w[f"d{n}"], name=f"{tag}_down", tm=1024, tn=512, tk=D_FF)
    out, out16, xh, rs = ln_res_fwd(x, y, w[f"ln_ffn{n}_g"], w[f"ln_ffn{n}_b"], 0.5, name=f"{tag}_ln")
    return out, out16, (x16, gu, h, xh, rs)


def _ffn_bwd(dout, saved, w, n, tag, comm_dh=None, comm_dwgu=None, comm_dx=None):
    x, gu, h, xh, rs = saved
    wgu, wd = w[f"gu{n}"], w[f"d{n}"]
    got = [[], [], []]
    dz, dg, db = ln_res_bwd(dout, xh, rs, w[f"ln_ffn{n}_g"], name=f"{tag}_ln_bwd")
    dh = mm(dz, wd, mode="nt", alpha=0.5, name=f"{tag}_dh", tm=1024, tn=D_FF // 2, tk=1024, comm=comm_dh)
    if comm_dh is not None:
        dh, got[0] = dh
    dgu = act_bwd(gu, dh, name=f"{tag}_act_bwd")
    dwd = mm(h, dz, mode="tn", alpha=0.5, name=f"{tag}_dwd", tm=D_FF // 2, tn=1024, tk=512)
    c = comm_dwgu(dwd) if comm_dwgu is not None else None
    dwgu = mm(x, dgu, mode="tn", name=f"{tag}_dwgu", tm=1024, tn=D_FF // 2, tk=512, comm=c)
    if c is not None:
        dwgu, got[1] = dwgu
    c = comm_dx(dwgu) if comm_dx is not None else None
    dx = mm(dgu, wgu, mode="nt", add=dz, beta=DN_ALPHA, name=f"{tag}_dx", tm=1024, tn=1024, tk=D_FF // 2, comm=c)
    if c is not None:
        dx, got[2] = dx
    return dx, dwgu, dwd, dg, db, got


def _layer_fwd(x, x16, mem, w, tag, comm_ffn1=None, on_ffn1=None, comm_gdn=None, on_gdn=None, comm_fox=None,
               on_fox=None):
    sv = {}
    x1, x1h, sv["ffn1"] = _ffn_fwd(x, x16, w, 1, f"{tag}_ffn1", comm=comm_ffn1, on_comm=on_ffn1)
    proj = mm(x1h, w["win"], name=f"{tag}_inproj", tm=1024, tn=640, tk=1024)
    cqkv = dwconv_fwd(proj, w["gdn_conv_w"], None, col0=P_GDN, name=f"{tag}_gdn_conv")
    ya, states, got = gdn_fwd(cqkv, proj, w["alog"], w["dtb"], w["ng"], name=f"{tag}_gdn", comm=comm_gdn)
    if on_gdn is not None:
        on_gdn(got)
    cum = fox_gate_fwd(proj, w["bf"], name=f"{tag}_fox_gate")
    cum_t = jnp.pad(cum[:, FOX_COL:FOX_COL + GROUP_HEADS].T, ((0, 8 - GROUP_HEADS), (0, 0)))
    yb, lse, got = fox_fwd(proj, cum, cum_t, name=f"{tag}_fox", comm=comm_fox)
    if on_fox is not None:
        on_fox(got)
    u = glu_fwd(proj, name=f"{tag}_glu")
    cc = dwconv_fwd(u, w["conf_dw_w"], w["conf_dw_b"], name=f"{tag}_conf_conv")
    yc = gn_silu_fwd(cc, w["conf_norm_g"], w["conf_norm_b"], name=f"{tag}_conf_norm")
    yd, rsave = sb_fwd(proj, name=f"{tag}_sb")
    ycat = jnp.concatenate([ya, yb, yc, yd], axis=1).astype(MXU_DT)
    mix = mm(ycat, w["wout"], name=f"{tag}_outproj")
    x2, x2h, xh2, rs2 = ln_res_fwd(x1, mix, w["ln_mix_g"], w["ln_mix_b"], 1.0, name=f"{tag}_ln_mix")
    sv["mix"] = (x1h, proj, cqkv, states, cum, cum_t, yb, lse, u, cc, rsave, ycat, xh2, rs2)
    q = mm(x2h, w["wq"], name=f"{tag}_memq", out_dtype=MXU_DT)
    kv = mm(mem, w["wkv"], name=f"{tag}_memkv", tm=N_MEM, out_dtype=MXU_DT)
    att = memattn_fwd(q, kv, name=f"{tag}_memattn")
    mo = mm(att, w["wo"], name=f"{tag}_memo")
    x3, x3h, xh3, rs3 = ln_res_fwd(x2, mo, w["ln_mem_g"], w["ln_mem_b"], 1.0, name=f"{tag}_ln_mem")
    sv["mem"] = (x2h, q, kv, att, xh3, rs3)
    x4, x4h, sv["ffn2"] = _ffn_fwd(x3, x3h, w, 2, f"{tag}_ffn2")
    return x4, x4h, sv


def _layer_bwd(dx4, mem, sv, w, tag, plan, tail=None):
    t = dx4.shape[0]
    gr = {}
    dx3, gr["gu2"], gr["d2"], gr["ln_ffn2_g"], gr["ln_ffn2_b"], _ = _ffn_bwd(dx4, sv["ffn2"], w, 2, f"{tag}_ffn2")
    x2, q, kv, att, xh3, rs3 = sv["mem"]
    dz, gr["ln_mem_g"], gr["ln_mem_b"] = ln_res_bwd(dx3, xh3, rs3, w["ln_mem_g"], name=f"{tag}_ln_mem_bwd")
    datt = mm(dz, w["wo"], mode="nt", name=f"{tag}_datt", out_dtype=MXU_DT)
    gr["wo"] = mm(att, dz, mode="tn", name=f"{tag}_dwo", tk=512)
    dq, dkv = memattn_bwd(q, kv, datt, name=f"{tag}_memattn_bwd")
    gr["wq"] = mm(x2, dq, mode="tn", name=f"{tag}_dwq", tk=512)
    gr["wkv"] = mm(mem, dkv, mode="tn", name=f"{tag}_dwkv", tk=N_MEM)
    dx2 = mm(dq, w["wq"], mode="nt", add=dz, beta=DN_ALPHA, name=f"{tag}_dx2")
    x1, proj, cqkv, states, cum, cum_t, yb, lse, u, cc, rsave, ycat, xh2, rs2 = sv["mix"]
    dz, gr["ln_mix_g"], gr["ln_mix_b"] = ln_res_bwd(dx2, xh2, rs2, w["ln_mix_g"], name=f"{tag}_ln_mix_bwd")
    dycat = mm(dz, w["wout"], mode="nt", name=f"{tag}_dycat")
    gr["wout"] = mm(ycat, dz, mode="tn", name=f"{tag}_dwout", tk=512)
    comm_sb, comm_fox, comm_gdn = plan(gr)
    gw = GROUP_WIDTH
    dya, dyb, dyc, dyd = (dycat[:, i * gw:(i + 1) * gw] for i in range(4))
    dq_d, dk_d, dv_d, got_sb = sb_bwd(proj, rsave, dyd, name=f"{tag}_sb_bwd", comm=comm_sb)
    dcc, gr["conf_norm_g"], gr["conf_norm_b"] = gn_silu_bwd(cc, w["conf_norm_g"], w["conf_norm_b"], dyc,
                                                            name=f"{tag}_conf_norm_bwd")
    du, gr["conf_dw_w"], gr["conf_dw_b"] = dwconv_bwd(dcc, u, w["conf_dw_w"], name=f"{tag}_conf_conv_bwd")
    dglu = glu_bwd(proj, du, name=f"{tag}_glu_bwd")
    dq_b, dk_b, dv_b, dcc, dcr, got_fox = fox_bwd(proj, cum, cum_t, yb, lse, dyb, name=f"{tag}_fox_bwd", comm=comm_fox)
    dcum = dcc + jnp.pad(dcr[:, :GROUP_HEADS, :].transpose(0, 2, 1).reshape(t, GROUP_HEADS),
                   ((0, 0), (FOX_COL, LANES - FOX_COL - GROUP_HEADS)))
    dsm_f, dbf = fox_gate_bwd(dcum, proj, w["bf"], name=f"{tag}_fox_gate_bwd")
    gr["fox_b_f"] = dbf[0, FOX_COL:FOX_COL + GROUP_HEADS]
    dcq, dz_a, dsm_a, dng, dal, ddt, got_gdn = gdn_bwd(cqkv, proj, w["alog"], w["dtb"], w["ng"], states, dya,
                                                       name=f"{tag}_gdn_bwd", comm=comm_gdn)
    gr["gdn_norm_g"] = dng.reshape(GROUP_HEADS, HEAD_DIM).sum(0)
    gr["gdn_a_log"] = dal[0, A_COL:A_COL + GROUP_HEADS]
    gr["gdn_dt_bias"] = ddt[0, A_COL:A_COL + GROUP_HEADS]
    dgq, gr["gdn_conv_w"], _ = dwconv_bwd(dcq, proj, w["gdn_conv_w"], col0=P_GDN, name=f"{tag}_gdn_conv_bwd")
    dproj = jnp.concatenate([dgq, dz_a, dq_b, dk_b, dv_b, dglu, dq_d, dk_d, dv_d, dsm_a + dsm_f],
                            axis=1).astype(MXU_DT)
    gr["win"] = mm(x1, dproj, mode="tn", name=f"{tag}_dwin", tm=1024, tn=640, tk=512)
    dx1 = mm(dproj, w["win"], mode="nt", add=dz, beta=DN_ALPHA, name=f"{tag}_dx1", tm=1024, tn=1024, tk=640)
    tail = {} if tail is None else dict(tail, comm_dh=tail["comm_dh"](gr))
    dx0, gr["gu1"], gr["d1"], gr["ln_ffn1_g"], gr["ln_ffn1_b"], got_tail = _ffn_bwd(
        dx1, sv["ffn1"], w, 1, f"{tag}_ffn1", **tail)
    return dx0, gr, (got_sb, got_fox, got_gdn), got_tail


SMALL_REPLICATED = ("ln_ffn1_g", "ln_ffn1_b", "gdn_a_log", "gdn_dt_bias", "gdn_norm_g", "fox_b_f", "conf_dw_b",
                    "conf_norm_g", "conf_norm_b", "ln_mix_g", "ln_mix_b", "ln_mem_g", "ln_mem_b", "ln_ffn2_g",
                    "ln_ffn2_b")
SMALL_SHARDED = ("gdn_conv_w", "conf_dw_w")
BIG = ("ffn1_w_gate", "ffn1_w_up", "ffn1_w_down", "w_in", "w_out", "mem_w_q", "mem_w_kv", "mem_w_o",
       "ffn2_w_gate", "ffn2_w_up", "ffn2_w_down")
WEIGHT_ORDER = ("ffn1_w_gate", "ffn1_w_up", "ffn1_w_down", "ln_ffn1_g", "ln_ffn1_b", "w_in", "gdn_conv_w", "gdn_a_log",
                "gdn_dt_bias", "gdn_norm_g", "fox_b_f", "conf_dw_w", "conf_dw_b", "conf_norm_g", "conf_norm_b", "w_out",
                "ln_mix_g", "ln_mix_b", "mem_w_q", "mem_w_kv", "mem_w_o", "ln_mem_g", "ln_mem_b", "ffn2_w_gate",
                "ffn2_w_up", "ffn2_w_down", "ln_ffn2_g", "ln_ffn2_b")


def _step(x, mem, loss_target, wts, ms, vs):
    me = 4 * lax.axis_index("x") + 2 * lax.axis_index("y") + lax.axis_index("c")
    x = x[0]
    mem = mem[0]
    target = loss_target[0]
    rows_s = D_MODEL // N_DEV
    first, rest = ("gu1", "d1", "win"), ("sq", "kv", "gu2", "d2")

    def shards(l):
        c = lambda k: wts[k][l].astype(MXU_DT)
        return dict(gu1=jnp.stack([c("ffn1_w_gate"), c("ffn1_w_up")]), d1=c("ffn1_w_down"),
                    win=_win_to_aligned(wts["w_in"][l]).astype(MXU_DT),
                    sq=jnp.stack([c("w_out"), c("mem_w_q"), c("mem_w_o")]), kv=c("mem_w_kv"),
                    gu2=jnp.stack([c("ffn2_w_gate"), c("ffn2_w_up")]), d2=c("ffn2_w_down"))

    def to_compute_layout(w, keys, got):
        for k, g in zip(keys, got):
            if k in ("gu1", "gu2"):
                w[k] = g.transpose(2, 1, 0, 3).reshape(D_MODEL, 2 * D_FF)
            elif k in ("d1", "d2"):
                w[k] = g.reshape(D_FF, D_MODEL)
            elif k == "win":
                w[k] = g.reshape(D_MODEL, P_WIDTH)
            elif k == "sq":
                full = g.transpose(1, 0, 2, 3).reshape(3, D_MODEL, D_MODEL)
                w["wout"], w["wq"], w["wo"] = full[0], full[1], full[2]
            else:
                w["wkv"] = g.transpose(1, 0, 2).reshape(D_MODEL, 2 * D_MODEL)

    def chunks(gr, keys, dtype=F32):
        out = []
        for k in keys:
            if k in ("gu1", "gu2"):
                out.append(gr[k].reshape(D_MODEL, 2, N_DEV, -1).transpose(2, 1, 0, 3))
            elif k in ("d1", "d2"):
                out.append(gr[k].reshape(N_DEV, -1, D_MODEL))
            elif k == "win":
                out.append(gr[k].reshape(N_DEV, rows_s, P_WIDTH))
            elif k == "sq":
                out.append(jnp.stack([gr[n].reshape(N_DEV, rows_s, D_MODEL) for n in ("wout", "wq", "wo")], axis=1))
            else:
                out.append(gr["wkv"].reshape(D_MODEL, N_DEV, -1).transpose(1, 0, 2))
        return [a.astype(dtype) for a in out]

    sh = [shards(l) for l in range(DEPTH)]
    sm_sh = _pack([wts["gdn_conv_w"], wts["conf_dw_w"]])
    got = exchange([sh[0]["gu1"], sm_sh], broadcast=True, name="gather_first")
    conv_shapes = [wts["gdn_conv_w"].shape, wts["conf_dw_w"].shape]
    parts = [_unpack(got[-1][j], conv_shapes) for j in range(N_DEV)]
    gconv_full = jnp.concatenate([p[0] for p in parts], axis=-1)
    cconv_full = jnp.concatenate([p[1] for p in parts], axis=-1)

    def small_weights(l):
        w = dict(gdn_conv_w=gconv_full[l], conf_dw_w=cconv_full[l],
                 alog=_row128(wts["gdn_a_log"][l], A_COL), dtb=_row128(wts["gdn_dt_bias"][l], A_COL),
                 bf=_row128(wts["fox_b_f"][l], FOX_COL), ng=jnp.tile(wts["gdn_norm_g"][l], GROUP_HEADS)[None, :])
        for k in ("ln_ffn1_g", "ln_ffn1_b", "conf_dw_b", "conf_norm_g", "conf_norm_b", "ln_mix_g", "ln_mix_b",
                  "ln_mem_g", "ln_mem_b", "ln_ffn2_g", "ln_ffn2_b"):
            w[k] = wts[k][l][None, :]
        return w

    lw = [small_weights(l) for l in range(DEPTH)]
    to_compute_layout(lw[0], ("gu1",), got[:-1])

    h, h16, sv0 = _layer_fwd(x, x.astype(MXU_DT), mem, lw[0], "l0",
                        comm_ffn1=Comm([sh[0][k] for k in first[1:]], True),
                        on_ffn1=lambda g: to_compute_layout(lw[0], first[1:], g),
                        comm_gdn=Comm([sh[0][k] for k in rest], True),
                        on_gdn=lambda g: to_compute_layout(lw[0], rest, g),
                        comm_fox=Comm([sh[1][k] for k in first + rest[:2]], True),
                        on_fox=lambda g: to_compute_layout(lw[1], first + rest[:2], g))
    h, _, sv1 = _layer_fwd(h, h16, mem, lw[1], "l1",
                        comm_gdn=Comm([sh[1][k] for k in rest[2:]], True),
                        on_gdn=lambda g: to_compute_layout(lw[1], rest[2:], g))
    dh, lpart = loss_head(h, target, name="loss_head")

    recv = [{}, {}]
    e_ffn, e_mem = ("gu2", "d2"), ("sq", "kv")
    dh, g1, got, _ = _layer_bwd(dh, mem, sv1, lw[1], "l1",
                                lambda gr: (None, Comm(chunks(gr, e_ffn), False), Comm(chunks(gr, e_mem), False)))
    recv[1].update(zip(e_ffn, got[1]))
    recv[1].update(zip(e_mem, got[2]))
    tail = dict(comm_dh=lambda gr: Comm(chunks(gr, ("win",), MXU_DT), False),
                comm_dwgu=lambda dwd: Comm(chunks({"d1": dwd}, ("d1",), MXU_DT), False),
                comm_dx=lambda dwgu: Comm(chunks({"gu1": dwgu}, ("gu1",), MXU_DT), False))
    dh, g0, got, got_t = _layer_bwd(
        dh, mem, sv0, lw[0], "l0",
        lambda gr: (Comm(chunks(gr, e_mem), False), Comm(chunks(g1, first[:2]), False),
                    Comm(chunks(g1, first[2:]) + chunks(gr, e_ffn), False)), tail)
    recv[0].update(zip(e_mem, got[0]))
    recv[1].update(zip(first[:2], got[1]))
    recv[1].update(win=got[2][0])
    recv[0].update(zip(e_ffn, got[2][1:]))
    recv[0].update(win=got_t[0][0], d1=got_t[1][0], gu1=got_t[2][0])
    grad_x = dh[None]
    grads = [g0, g1]

    def gl(k):
        return jnp.stack([grads[l][k] for l in range(DEPTH)])

    small_names = SMALL_REPLICATED + SMALL_SHARDED
    small_grads = [gl(k) for k in small_names] + [lpart[0, :1]]
    got = exchange([_pack(small_grads)], broadcast=True, name="gather_small_grads")
    sm_sum = slot_sum(got[0], name="sum_small_grads")
    sm_g = _unpack(sm_sum, [g.shape for g in small_grads])
    loss = sm_g[-1][0]
    small_g = dict(zip(small_names, sm_g[:-1]))
    for k in SMALL_SHARDED:
        width = wts[k].shape[-1]
        small_g[k] = lax.dynamic_slice_in_dim(small_g[k], me * width, width, axis=2)

    out_g, out_d, out_m, out_v = {}, {}, {}, {}

    def update(names, key, fix=lambda a: a):
        res = {k: [] for k in names}
        for l in range(DEPTH):
            slots = fix(recv[l][key])
            slots = slots.reshape(N_DEV, -1, slots.shape[-1])
            for i, k in enumerate(names):
                two = lambda a: a[l].reshape(-1, a.shape[-1])
                res[k].append(adamw(two(wts[k]), two(ms[k]), two(vs[k]), slots, row0=i * two(wts[k]).shape[0],
                                    name=f"adamw_{k}_l{l}"))
        for k in names:
            for dst, per_layer in zip((out_g, out_d, out_m, out_v), zip(*res[k])):
                dst[k] = jnp.stack(per_layer).reshape(wts[k].shape)

    update(("ffn1_w_gate", "ffn1_w_up"), "gu1")
    update(("ffn1_w_down",), "d1")
    update(("w_in",), "win", _win_from_aligned)
    update(("w_out", "mem_w_q", "mem_w_o"), "sq")
    update(("mem_w_kv",), "kv")
    update(("ffn2_w_gate", "ffn2_w_up"), "gu2")
    update(("ffn2_w_down",), "d2")

    sw = _pack([wts[k] for k in small_names])
    smm = _pack([ms[k] for k in small_names])
    smv = _pack([vs[k] for k in small_names])
    sg = _pack([small_g[k] for k in small_names])
    res = adamw(sw, smm, smv, sg[None], name="adamw_small")
    shapes = [wts[k].shape for k in small_names]
    for dst, buf in zip((out_g, out_d, out_m, out_v), res):
        for k, a in zip(small_names, _unpack(buf, shapes)):
            dst[k] = a

    return (loss, grad_x, *[out_g[k] for k in WEIGHT_ORDER], *[out_d[k] for k in WEIGHT_ORDER],
            *[out_m[k] for k in WEIGHT_ORDER], *[out_v[k] for k in WEIGHT_ORDER])


def kernel(x, mem, ffn1_w_gate, ffn1_w_up, ffn1_w_down, ln_ffn1_g, ln_ffn1_b, w_in, gdn_conv_w, gdn_a_log, gdn_dt_bias, gdn_norm_g, fox_b_f, conf_dw_w, conf_dw_b, conf_norm_g, conf_norm_b, w_out, ln_mix_g, ln_mix_b, mem_w_q, mem_w_kv, mem_w_o, ln_mem_g, ln_mem_b, ffn2_w_gate, ffn2_w_up, ffn2_w_down, ln_ffn2_g, ln_ffn2_b, loss_target, m_ffn1_w_gate, m_ffn1_w_up, m_ffn1_w_down, m_ln_ffn1_g, m_ln_ffn1_b, m_w_in, m_gdn_conv_w, m_gdn_a_log, m_gdn_dt_bias, m_gdn_norm_g, m_fox_b_f, m_conf_dw_w, m_conf_dw_b, m_conf_norm_g, m_conf_norm_b, m_w_out, m_ln_mix_g, m_ln_mix_b, m_mem_w_q, m_mem_w_kv, m_mem_w_o, m_ln_mem_g, m_ln_mem_b, m_ffn2_w_gate, m_ffn2_w_up, m_ffn2_w_down, m_ln_ffn2_g, m_ln_ffn2_b, v_ffn1_w_gate, v_ffn1_w_up, v_ffn1_w_down, v_ln_ffn1_g, v_ln_ffn1_b, v_w_in, v_gdn_conv_w, v_gdn_a_log, v_gdn_dt_bias, v_gdn_norm_g, v_fox_b_f, v_conf_dw_w, v_conf_dw_b, v_conf_norm_g, v_conf_norm_b, v_w_out, v_ln_mix_g, v_ln_mix_b, v_mem_w_q, v_mem_w_kv, v_mem_w_o, v_ln_mem_g, v_ln_mem_b, v_ffn2_w_gate, v_ffn2_w_up, v_ffn2_w_down, v_ln_ffn2_g, v_ln_ffn2_b):
    args = locals()
    wts = {k: args[k] for k in WEIGHT_ORDER}
    ms = {k: args["m_" + k] for k in WEIGHT_ORDER}
    vs = {k: args["v_" + k] for k in WEIGHT_ORDER}
    return _step(x, mem, loss_target, wts, ms, vs)
```

```python
import functools
import math

import jax
import jax.numpy as jnp
import numpy as np
from jax import lax
from jax.experimental import pallas as pl
from jax.experimental.pallas import tpu as pltpu

F32 = jnp.float32
BF16 = jnp.bfloat16
MXU_DT = jnp.bfloat16
HI = lax.Precision.HIGHEST

N_DEV = 8
VMEM_LIMIT_BYTES = 56 * 1024 * 1024
LANES = 128

D_MODEL = 1024
DEPTH = 2
GROUP_WIDTH = 256
HEAD_DIM = 64
GROUP_HEADS = 4
D_FF = 2816
SHORT_CONV = 4
CONF_KERNEL = 31
CONF_GROUPS = 4
GDN_CHUNK = 64
N_MEM = 256
MEM_HEADS = 4
MEM_HEAD_DIM = 256
DN_ALPHA = float((2 * DEPTH) ** 0.25)
LN_EPS = 1e-5
RMS_EPS = 1e-6
L2_EPS = 1e-6
NEG_BIG = -1e30
IN_SPLITS = (768, 256, 4, 4, 768, 4, 512, 768)
IN_WIDTH = sum(IN_SPLITS)
P_GDN, P_Z, P_FOX, P_CONF, P_SB, P_SMALL = 0, 768, 1024, 1792, 2304, 3072
P_WIDTH = 3200

ADAM_LR = 0.001
ADAM_B1 = 0.9
ADAM_B2 = 0.999
ADAM_EPS = 1e-08
ADAM_WD = 0.01
ADAM_STEP = 10


def _cparams(sem):
    return pltpu.CompilerParams(dimension_semantics=sem, vmem_limit_bytes=VMEM_LIMIT_BYTES)


def _tile(n, pref, align=LANES):
    if n <= pref:
        return n
    t = (pref // align) * align
    while t >= align:
        if n % t == 0:
            return t
        t -= align
    return n


def mm(a, b, *, mode="nn", add=None, alpha=1.0, beta=1.0, out_dtype=F32, name,
       tm=1024, tn=512, tk=1024, comm=None):
    if mode == "nn":
        (m, k), (k2, n) = a.shape, b.shape
    elif mode == "nt":
        (m, k), (n, k2) = a.shape, b.shape
    else:
        (k, m), (k2, n) = a.shape, b.shape
    assert k == k2, (a.shape, b.shape, mode)
    tm = _tile(m, tm, 8 if mode != "tn" else LANES)
    tn = _tile(n, tn)
    tk = _tile(k, tk, LANES if mode != "tn" else 8)
    nk = k // tk
    if mode == "nn":
        a_spec = pl.BlockSpec((tm, tk), lambda i, j, kk: (i, kk))
        b_spec = pl.BlockSpec((tk, tn), lambda i, j, kk: (kk, j))
        dims = (((1,), (0,)), ((), ()))
    elif mode == "nt":
        a_spec = pl.BlockSpec((tm, tk), lambda i, j, kk: (i, kk))
        b_spec = pl.BlockSpec((tn, tk), lambda i, j, kk: (j, kk))
        dims = (((1,), (1,)), ((), ()))
    else:
        a_spec = pl.BlockSpec((tk, tm), lambda i, j, kk: (kk, i))
        b_spec = pl.BlockSpec((tk, tn), lambda i, j, kk: (kk, j))
        dims = (((0,), (0,)), ((), ()))
    o_spec = pl.BlockSpec((tm, tn), lambda i, j, kk: (i, j))
    has_add = add is not None

    def body(*refs):
        if has_add:
            a_ref, b_ref, add_ref, o_ref, acc_ref = refs
        else:
            a_ref, b_ref, o_ref, acc_ref = refs
        kk = pl.program_id(2)

        @pl.when(kk == 0)
        def _():
            acc_ref[...] = jnp.zeros_like(acc_ref)

        acc_ref[...] += lax.dot_general(a_ref[...].astype(MXU_DT), b_ref[...].astype(MXU_DT), dims,
                                        preferred_element_type=F32)

        @pl.when(kk == nk - 1)
        def _():
            r = acc_ref[...]
            if alpha != 1.0:
                r = r * alpha
            if has_add:
                r = r + beta * add_ref[...].astype(F32)
            o_ref[...] = r.astype(out_dtype)

    in_specs = [a_spec, b_spec] + ([o_spec] if has_add else [])
    args = (a, b) + ((add,) if has_add else ())
    grid = (m // tm, n // tn, nk)
    call = dict(name=name, grid=grid, in_specs=in_specs, out_specs=[o_spec],
                out_shape=[jax.ShapeDtypeStruct((m, n), out_dtype)],
                scratch_shapes=[pltpu.VMEM((tm, tn), F32)],
                compiler_params=_cparams(("parallel", "parallel", "arbitrary")))
    (out,), got = carry_comm(call, body, args, comm, 1, *_grid_ends(*grid))
    return out if comm is None else (out, got)


def ln_res_fwd(x, y, g, b, s, *, name):
    t, d = x.shape
    tm = _tile(t, 512, 8)

    def body(x_ref, y_ref, g_ref, b_ref, o_ref, o16_ref, xh_ref, rs_ref):
        z = DN_ALPHA * x_ref[...] + s * y_ref[...]
        mu = jnp.mean(z, axis=-1, keepdims=True)
        zc = z - mu
        var = jnp.mean(zc * zc, axis=-1, keepdims=True)
        rstd = lax.rsqrt(var + LN_EPS)
        xh = zc * rstd
        xh_ref[...] = xh
        rs_ref[...] = rstd
        out = xh * g_ref[...] + b_ref[...]
        o_ref[...] = out
        o16_ref[...] = out.astype(o16_ref.dtype)

    row = pl.BlockSpec((tm, d), lambda i: (i, 0))
    vec = pl.BlockSpec((1, d), lambda i: (0, 0))
    return pl.pallas_call(
        body, name=name, grid=(t // tm,),
        in_specs=[row, row, vec, vec],
        out_specs=[row, row, row, pl.BlockSpec((tm, 1), lambda i: (i, 0))],
        out_shape=[jax.ShapeDtypeStruct((t, d), F32), jax.ShapeDtypeStruct((t, d), MXU_DT),
                   jax.ShapeDtypeStruct((t, d), F32), jax.ShapeDtypeStruct((t, 1), F32)],
        compiler_params=_cparams(("parallel",)),
    )(x, y, g, b)


def ln_res_bwd(dout, xhat, rstd, g, *, name):
    t, d = dout.shape
    tm = _tile(t, 512, 8)

    def body(do_ref, xh_ref, rs_ref, g_ref, dz_ref, dg_ref, db_ref):
        i = pl.program_id(0)

        @pl.when(i == 0)
        def _():
            dg_ref[...] = jnp.zeros_like(dg_ref)
            db_ref[...] = jnp.zeros_like(db_ref)

        do = do_ref[...]
        xh = xh_ref[...]
        dxh = do * g_ref[...]
        m1 = jnp.mean(dxh, axis=-1, keepdims=True)
        m2 = jnp.mean(dxh * xh, axis=-1, keepdims=True)
        dz_ref[...] = rs_ref[...] * (dxh - m1 - xh * m2)
        dg_ref[...] += jnp.sum(do * xh, axis=0, keepdims=True)
        db_ref[...] += jnp.sum(do, axis=0, keepdims=True)

    row = pl.BlockSpec((tm, d), lambda i: (i, 0))
    vec = pl.BlockSpec((1, d), lambda i: (0, 0))
    return pl.pallas_call(
        body, name=name, grid=(t // tm,),
        in_specs=[row, row, pl.BlockSpec((tm, 1), lambda i: (i, 0)), vec],
        out_specs=[row, vec, vec],
        out_shape=[jax.ShapeDtypeStruct((t, d), F32), jax.ShapeDtypeStruct((1, d), F32),
                   jax.ShapeDtypeStruct((1, d), F32)],
        compiler_params=_cparams(("arbitrary",)),
    )(dout, xhat, rstd, g)


def _sigmoid(x):
    return 1.0 / (1.0 + jnp.exp(-x))


def act_fwd(gu, *, name):
    t, f2 = gu.shape
    f = f2 // 2
    tm = _tile(t, 256, 8)

    def body(gu_ref, h_ref):
        g = gu_ref[:, :f].astype(F32)
        h_ref[...] = (g * _sigmoid(g) * gu_ref[:, f:].astype(F32)).astype(h_ref.dtype)

    return pl.pallas_call(
        body, name=name, grid=(t // tm,),
        in_specs=[pl.BlockSpec((tm, f2), lambda i: (i, 0))],
        out_specs=pl.BlockSpec((tm, f), lambda i: (i, 0)),
        out_shape=jax.ShapeDtypeStruct((t, f), MXU_DT),
        compiler_params=_cparams(("parallel",)),
    )(gu)


def act_bwd(gu, dh, *, name):
    t, f2 = gu.shape
    f = f2 // 2
    tm = _tile(t, 256, 8)

    def body(gu_ref, dh_ref, o_ref):
        g = gu_ref[:, :f].astype(F32)
        u = gu_ref[:, f:].astype(F32)
        dh = dh_ref[...]
        sg = _sigmoid(g)
        o_ref[:, f:] = (dh * g * sg).astype(o_ref.dtype)
        o_ref[:, :f] = (dh * u * sg * (1.0 + g * (1.0 - sg))).astype(o_ref.dtype)

    return pl.pallas_call(
        body, name=name, grid=(t // tm,),
        in_specs=[pl.BlockSpec((tm, f2), lambda i: (i, 0)), pl.BlockSpec((tm, f), lambda i: (i, 0))],
        out_specs=pl.BlockSpec((tm, f2), lambda i: (i, 0)),
        out_shape=jax.ShapeDtypeStruct((t, f2), MXU_DT),
        compiler_params=_cparams(("parallel",)),
    )(gu, dh)


def loss_head(y, target, *, name):
    t, d = y.shape
    tm = _tile(t, 512, 8)

    def body(y_ref, t_ref, dy_ref, l_ref):
        i = pl.program_id(0)

        @pl.when(i == 0)
        def _():
            l_ref[...] = jnp.zeros_like(l_ref)

        err = y_ref[...] - t_ref[...]
        dy_ref[...] = err * (1.0 / d)
        part = jnp.sum(jnp.sum(err * err, axis=-1, keepdims=True), axis=0, keepdims=True)
        l_ref[...] += jnp.broadcast_to(part * (0.5 / d), l_ref.shape)

    row = pl.BlockSpec((tm, d), lambda i: (i, 0))
    return pl.pallas_call(
        body, name=name, grid=(t // tm,),
        in_specs=[row, row],
        out_specs=[row, pl.BlockSpec((1, LANES), lambda i: (0, 0))],
        out_shape=[jax.ShapeDtypeStruct((t, d), F32), jax.ShapeDtypeStruct((1, LANES), F32)],
        compiler_params=_cparams(("arbitrary",)),
    )(y, target)


def _dot(a, b):
    return lax.dot_general(a, b, (((1,), (0,)), ((), ())), preferred_element_type=F32)


def _dot_nt(a, b):
    return lax.dot_general(a, b, (((1,), (1,)), ((), ())), preferred_element_type=F32)


def _dot_tn(a, b):
    return lax.dot_general(a, b, (((0,), (0,)), ((), ())), preferred_element_type=F32)


def _dot_hi(a, b):
    return lax.dot_general(a, b, (((1,), (0,)), ((), ())), preferred_element_type=F32, precision=HI)


def _dot_nt_hi(a, b):
    return lax.dot_general(a, b, (((1,), (1,)), ((), ())), preferred_element_type=F32, precision=HI)


def _split_dot(x, u):
    hi = x.astype(MXU_DT)
    lo = (x - hi.astype(F32)).astype(MXU_DT)
    return _dot(hi, u) + _dot(lo, u)


def _mem_probs(q_ref, kv_ref, h):
    lo = h * MEM_HEAD_DIM
    qh = q_ref[:, lo:lo + MEM_HEAD_DIM].astype(MXU_DT)
    kh = kv_ref[:, lo:lo + MEM_HEAD_DIM].astype(MXU_DT)
    s = _dot_nt(qh, kh) * (MEM_HEAD_DIM ** -0.5)
    s = s - jnp.max(s, axis=-1, keepdims=True)
    p = jnp.exp(s)
    return p / jnp.sum(p, axis=-1, keepdims=True), qh, kh


def memattn_fwd(q, kv, *, name):
    t, d = q.shape
    tm = _tile(t, 512, 8)

    def body(q_ref, kv_ref, o_ref):
        for h in range(MEM_HEADS):
            lo = h * MEM_HEAD_DIM
            p, _, _ = _mem_probs(q_ref, kv_ref, h)
            vh = kv_ref[:, d + lo:d + lo + MEM_HEAD_DIM].astype(MXU_DT)
            o_ref[:, lo:lo + MEM_HEAD_DIM] = _dot(p.astype(MXU_DT), vh).astype(o_ref.dtype)

    return pl.pallas_call(
        body, name=name, grid=(t // tm,),
        in_specs=[pl.BlockSpec((tm, d), lambda i: (i, 0)), pl.BlockSpec(kv.shape, lambda i: (0, 0))],
        out_specs=pl.BlockSpec((tm, d), lambda i: (i, 0)),
        out_shape=jax.ShapeDtypeStruct((t, d), MXU_DT),
        compiler_params=_cparams(("parallel",)),
    )(q, kv)


def memattn_bwd(q, kv, datt, *, name):
    t, d = q.shape
    tm = _tile(t, 512, 8)
    scale = MEM_HEAD_DIM ** -0.5

    def body(q_ref, kv_ref, da_ref, dq_ref, dkv_ref):
        @pl.when(pl.program_id(0) == 0)
        def _():
            dkv_ref[...] = jnp.zeros_like(dkv_ref)

        for h in range(MEM_HEADS):
            lo = h * MEM_HEAD_DIM
            p, qh, kh = _mem_probs(q_ref, kv_ref, h)
            vh = kv_ref[:, d + lo:d + lo + MEM_HEAD_DIM].astype(MXU_DT)
            da = da_ref[:, lo:lo + MEM_HEAD_DIM].astype(MXU_DT)
            dp = _dot_nt(da, vh)
            ds = p * (dp - jnp.sum(dp * p, axis=-1, keepdims=True))
            dsb = ds.astype(MXU_DT)
            dq_ref[:, lo:lo + MEM_HEAD_DIM] = (_dot(dsb, kh) * scale).astype(dq_ref.dtype)
            dkv_ref[:, lo:lo + MEM_HEAD_DIM] += _dot_tn(dsb, qh) * scale
            dkv_ref[:, d + lo:d + lo + MEM_HEAD_DIM] += _dot_tn(p.astype(MXU_DT), da)

    row = pl.BlockSpec((tm, d), lambda i: (i, 0))
    full = pl.BlockSpec(kv.shape, lambda i: (0, 0))
    return pl.pallas_call(
        body, name=name, grid=(t // tm,),
        in_specs=[row, full, row],
        out_specs=[row, full],
        out_shape=[jax.ShapeDtypeStruct((t, d), MXU_DT), jax.ShapeDtypeStruct(kv.shape, F32)],
        compiler_params=_cparams(("arbitrary",)),
    )(q, kv, datt)


def _halo(k):
    return 8 * ((k - 1 + 7) // 8)


def dwconv_fwd(u, w, bias, *, col0=0, width=None, name):
    t = u.shape[0]
    kk, c = w.shape
    width = c if width is None else width
    assert width == c and col0 % c == 0
    cb = col0 // c
    hb = _halo(kk)
    tm = _tile(t, 512, hb)
    r = tm // hb
    has_bias = bias is not None

    def body(*refs):
        if has_bias:
            prev_ref, cur_ref, w_ref, b_ref, o_ref, scr = refs
        else:
            prev_ref, cur_ref, w_ref, o_ref, scr = refs
        i = pl.program_id(0)
        scr[0:hb, :] = jnp.where(i == 0, 0.0, prev_ref[...])
        scr[hb:hb + tm, :] = cur_ref[...]
        acc = jnp.zeros((tm, c), F32)
        for k in range(kk):
            acc = acc + w_ref[k:k + 1, :] * scr[pl.ds(hb - (kk - 1) + k, tm), :]
        if has_bias:
            acc = acc + b_ref[...]
        o_ref[...] = acc

    in_specs = [pl.BlockSpec((hb, c), lambda i: (jnp.maximum(i * r - 1, 0), cb)),
                pl.BlockSpec((tm, c), lambda i: (i, cb)),
                pl.BlockSpec((kk, c), lambda i: (0, 0))]
    args = [u, u, w]
    if has_bias:
        in_specs.append(pl.BlockSpec((1, c), lambda i: (0, 0)))
        args.append(bias)
    return pl.pallas_call(
        body, name=name, grid=(t // tm,),
        in_specs=in_specs,
        out_specs=pl.BlockSpec((tm, c), lambda i: (i, 0)),
        out_shape=jax.ShapeDtypeStruct((t, c), F32),
        scratch_shapes=[pltpu.VMEM((hb + tm, c), F32)],
        compiler_params=_cparams(("parallel",)),
    )(*args)


def dwconv_bwd(dc, u, w, *, col0=0, name):
    t, c = dc.shape
    kk = w.shape[0]
    assert col0 % c == 0
    cb = col0 // c
    hb = _halo(kk)
    tm = _tile(t, 512, hb)
    r = tm // hb
    n = t // tm

    def body(dcur_ref, dnext_ref, uprev_ref, ucur_ref, w_ref, du_ref, dw_ref, db_ref, sd, su):
        i = pl.program_id(0)

        @pl.when(i == 0)
        def _():
            dw_ref[...] = jnp.zeros_like(dw_ref)
            db_ref[...] = jnp.zeros_like(db_ref)

        dcur = dcur_ref[...]
        sd[0:tm, :] = dcur
        sd[tm:tm + hb, :] = jnp.where(i == n - 1, 0.0, dnext_ref[...])
        su[0:hb, :] = jnp.where(i == 0, 0.0, uprev_ref[...])
        su[hb:hb + tm, :] = ucur_ref[...]
        acc = jnp.zeros((tm, c), F32)
        for k in range(kk):
            acc = acc + w_ref[k:k + 1, :] * sd[pl.ds(kk - 1 - k, tm), :]
            dw_ref[k:k + 1, :] += jnp.sum(dcur * su[pl.ds(hb - (kk - 1) + k, tm), :], axis=0, keepdims=True)
        du_ref[...] = acc
        db_ref[...] += jnp.sum(dcur, axis=0, keepdims=True)

    return pl.pallas_call(
        body, name=name, grid=(n,),
        in_specs=[pl.BlockSpec((tm, c), lambda i: (i, 0)),
                  pl.BlockSpec((hb, c), lambda i: (jnp.minimum((i + 1) * r, n * r - 1), 0)),
                  pl.BlockSpec((hb, c), lambda i: (jnp.maximum(i * r - 1, 0), cb)),
                  pl.BlockSpec((tm, c), lambda i: (i, cb)),
                  pl.BlockSpec((kk, c), lambda i: (0, 0))],
        out_specs=[pl.BlockSpec((tm, c), lambda i: (i, 0)),
                   pl.BlockSpec((kk, c), lambda i: (0, 0)),
                   pl.BlockSpec((1, c), lambda i: (0, 0))],
        out_shape=[jax.ShapeDtypeStruct((t, c), F32), jax.ShapeDtypeStruct((kk, c), F32),
                   jax.ShapeDtypeStruct((1, c), F32)],
        scratch_shapes=[pltpu.VMEM((tm + hb, c), F32), pltpu.VMEM((hb + tm, c), F32)],
        compiler_params=_cparams(("arbitrary",)),
    )(dc, dc, u, u, w)


def glu_fwd(proj, *, name):
    t = proj.shape[0]
    c = GROUP_WIDTH
    tm = _tile(t, 1024, 8)
    vb, gb = P_CONF // c, P_CONF // c + 1

    def body(v_ref, g_ref, o_ref):
        o_ref[...] = v_ref[...] * _sigmoid(g_ref[...])

    return pl.pallas_call(
        body, name=name, grid=(t // tm,),
        in_specs=[pl.BlockSpec((tm, c), lambda i: (i, vb)), pl.BlockSpec((tm, c), lambda i: (i, gb))],
        out_specs=pl.BlockSpec((tm, c), lambda i: (i, 0)),
        out_shape=jax.ShapeDtypeStruct((t, c), F32),
        compiler_params=_cparams(("parallel",)),
    )(proj, proj)


def glu_bwd(proj, du, *, name):
    t = proj.shape[0]
    c = GROUP_WIDTH
    tm = _tile(t, 1024, 8)
    vb, gb = P_CONF // c, P_CONF // c + 1

    def body(v_ref, g_ref, du_ref, o_ref):
        sg = _sigmoid(g_ref[...])
        du = du_ref[...]
        o_ref[:, :c] = du * sg
        o_ref[:, c:] = du * v_ref[...] * sg * (1.0 - sg)

    return pl.pallas_call(
        body, name=name, grid=(t // tm,),
        in_specs=[pl.BlockSpec((tm, c), lambda i: (i, vb)), pl.BlockSpec((tm, c), lambda i: (i, gb)),
                  pl.BlockSpec((tm, c), lambda i: (i, 0))],
        out_specs=pl.BlockSpec((tm, 2 * c), lambda i: (i, 0)),
        out_shape=jax.ShapeDtypeStruct((t, 2 * c), F32),
        compiler_params=_cparams(("parallel",)),
    )(proj, proj, du)


def _group_mean_matrix(c, groups):
    gsz = c // groups
    ri = lax.broadcasted_iota(jnp.int32, (c, c), 0) // gsz
    ci = lax.broadcasted_iota(jnp.int32, (c, c), 1) // gsz
    return jnp.where(ri == ci, 1.0 / gsz, 0.0).astype(F32)


def gn_silu_fwd(cx, gamma, beta, *, name):
    t, c = cx.shape
    tm = _tile(t, 1024, 8)

    def body(c_ref, g_ref, b_ref, o_ref):
        gm = _group_mean_matrix(c, CONF_GROUPS)
        x = c_ref[...]
        mu = _dot_hi(x, gm)
        xc = x - mu
        var = _dot_hi(xc * xc, gm)
        a = xc * lax.rsqrt(var + LN_EPS) * g_ref[...] + b_ref[...]
        o_ref[...] = a * _sigmoid(a)

    row = pl.BlockSpec((tm, c), lambda i: (i, 0))
    vec = pl.BlockSpec((1, c), lambda i: (0, 0))
    return pl.pallas_call(
        body, name=name, grid=(t // tm,),
        in_specs=[row, vec, vec], out_specs=row,
        out_shape=jax.ShapeDtypeStruct((t, c), F32),
        compiler_params=_cparams(("parallel",)),
    )(cx, gamma, beta)


def gn_silu_bwd(cx, gamma, beta, dy, *, name):
    t, c = cx.shape
    tm = _tile(t, 1024, 8)

    def body(c_ref, g_ref, b_ref, dy_ref, dc_ref, dg_ref, db_ref):
        @pl.when(pl.program_id(0) == 0)
        def _():
            dg_ref[...] = jnp.zeros_like(dg_ref)
            db_ref[...] = jnp.zeros_like(db_ref)

        gm = _group_mean_matrix(c, CONF_GROUPS)
        x = c_ref[...]
        mu = _dot_hi(x, gm)
        xc = x - mu
        var = _dot_hi(xc * xc, gm)
        rstd = lax.rsqrt(var + LN_EPS)
        nrm = xc * rstd
        a = nrm * g_ref[...] + b_ref[...]
        sa = _sigmoid(a)
        da = dy_ref[...] * sa * (1.0 + a * (1.0 - sa))
        dg_ref[...] += jnp.sum(da * nrm, axis=0, keepdims=True)
        db_ref[...] += jnp.sum(da, axis=0, keepdims=True)
        dn = da * g_ref[...]
        dc_ref[...] = rstd * (dn - _dot_hi(dn, gm) - nrm * _dot_hi(dn * nrm, gm))

    row = pl.BlockSpec((tm, c), lambda i: (i, 0))
    vec = pl.BlockSpec((1, c), lambda i: (0, 0))
    return pl.pallas_call(
        body, name=name, grid=(t // tm,),
        in_specs=[row, vec, vec, row], out_specs=[row, vec, vec],
        out_shape=[jax.ShapeDtypeStruct((t, c), F32), jax.ShapeDtypeStruct((1, c), F32),
                   jax.ShapeDtypeStruct((1, c), F32)],
        compiler_params=_cparams(("arbitrary",)),
    )(cx, gamma, beta, dy)


FOX_COL = 8
SMALL_BLK = P_SMALL // LANES


def _log_sigmoid(x):
    return jnp.minimum(x, 0.0) - jnp.log(1.0 + jnp.exp(-jnp.abs(x)))


def _fox_cols(shape):
    col = lax.broadcasted_iota(jnp.int32, shape, 1)
    return (col >= FOX_COL) & (col < FOX_COL + GROUP_HEADS)


def fox_gate_fwd(proj, bvec, *, name):
    t = proj.shape[0]
    tm = _tile(t, 256, 8)

    def body(s_ref, b_ref, o_ref, carry):
        @pl.when(pl.program_id(0) == 0)
        def _():
            carry[...] = jnp.zeros_like(carry)

        lf = jnp.where(_fox_cols((tm, LANES)), _log_sigmoid(s_ref[...] + b_ref[...]), 0.0)
        ri = lax.broadcasted_iota(jnp.int32, (tm, tm), 0)
        ci = lax.broadcasted_iota(jnp.int32, (tm, tm), 1)
        cum = _dot_hi(jnp.where(ri >= ci, 1.0, 0.0).astype(F32), lf) + carry[...]
        o_ref[...] = cum
        carry[...] = cum[tm - 1:tm, :]

    return pl.pallas_call(
        body, name=name, grid=(t // tm,),
        in_specs=[pl.BlockSpec((tm, LANES), lambda i: (i, SMALL_BLK)), pl.BlockSpec((1, LANES), lambda i: (0, 0))],
        out_specs=pl.BlockSpec((tm, LANES), lambda i: (i, 0)),
        out_shape=jax.ShapeDtypeStruct((t, LANES), F32),
        scratch_shapes=[pltpu.VMEM((1, LANES), F32)],
        compiler_params=_cparams(("arbitrary",)),
    )(proj, bvec)


def fox_gate_bwd(dcum, proj, bvec, *, name):
    t = proj.shape[0]
    tm = _tile(t, 256, 8)
    n = t // tm

    def body(d_ref, s_ref, b_ref, o_ref, db_ref, carry):
        @pl.when(pl.program_id(0) == 0)
        def _():
            carry[...] = jnp.zeros_like(carry)
            db_ref[...] = jnp.zeros_like(db_ref)

        ri = lax.broadcasted_iota(jnp.int32, (tm, tm), 0)
        ci = lax.broadcasted_iota(jnp.int32, (tm, tm), 1)
        dlf = _dot_hi(jnp.where(ri <= ci, 1.0, 0.0).astype(F32), d_ref[...]) + carry[...]
        carry[...] = dlf[0:1, :]
        x = s_ref[...] + b_ref[...]
        dx = jnp.where(_fox_cols((tm, LANES)), dlf * (1.0 - _sigmoid(x)), 0.0)
        o_ref[...] = dx
        db_ref[...] += jnp.sum(dx, axis=0, keepdims=True)

    return pl.pallas_call(
        body, name=name, grid=(n,),
        in_specs=[pl.BlockSpec((tm, LANES), lambda i: (n - 1 - i, 0)),
                  pl.BlockSpec((tm, LANES), lambda i: (n - 1 - i, SMALL_BLK)),
                  pl.BlockSpec((1, LANES), lambda i: (0, 0))],
        out_specs=[pl.BlockSpec((tm, LANES), lambda i: (n - 1 - i, 0)), pl.BlockSpec((1, LANES), lambda i: (0, 0))],
        out_shape=[jax.ShapeDtypeStruct((t, LANES), F32), jax.ShapeDtypeStruct((1, LANES), F32)],
        scratch_shapes=[pltpu.VMEM((1, LANES), F32)],
        compiler_params=_cparams(("arbitrary",)),
    )(dcum, proj, bvec)


def _head_masks(c):
    lane_head = lax.broadcasted_iota(jnp.int32, (1, c), 1) // HEAD_DIM
    return [lane_head == h for h in range(GROUP_HEADS)]


def _attn_tiles(t, tq, tk):
    tq = _tile(t, tq, 8)
    tk = _tile(t, tk, LANES)
    return tq, tk, t // tq, t // tk


def _grid_ends(*sizes):
    first = lambda: functools.reduce(lambda a, b: a & b, [pl.program_id(d) == 0 for d in range(len(sizes))])
    last = lambda: functools.reduce(lambda a, b: a & b, [pl.program_id(d) == s - 1 for d, s in enumerate(sizes)])
    return first, last


EXP_DEAD = -110.0


def _key_norm_max(k_ref, nk, tk, masks):
    lane = lax.broadcasted_iota(jnp.int32, (1, LANES), 1)

    def one(jt, km):
        kb = k_ref[pl.ds(pl.multiple_of(jt * tk, tk), tk), :].astype(MXU_DT).astype(F32)
        sq = kb * kb
        for h in range(GROUP_HEADS):
            top = jnp.max(jnp.sum(jnp.where(masks[h], sq, 0.0), axis=-1, keepdims=True))
            km = jnp.where(lane == h, jnp.maximum(km, top), km)
        return km

    return lax.fori_loop(0, nk, one, jnp.zeros((1, LANES), F32))


def _fox_reach(qh, km, cc_ref):
    out = []
    for h in range(GROUP_HEADS):
        qf = qh[h].astype(F32)
        qn = jnp.sqrt(jnp.sum(qf * qf, axis=-1, keepdims=True))
        out.append(1.001 * qn * jnp.sqrt(km[:, h:h + 1]) + cc_ref[:, FOX_COL + h:FOX_COL + h + 1])
    return out


def _fox_alive(reach, top, cr_ref, j, tk):
    ends = cr_ref[jnp.maximum(j, 0)][:, tk - 1:tk]
    worst = jnp.float32(NEG_BIG)
    for h in range(GROUP_HEADS):
        worst = jnp.maximum(worst, jnp.max(reach[h] - top[h]) - jnp.max(ends[h:h + 1, :]))
    return (worst > EXP_DEAD).astype(jnp.int32)


def fox_fwd(proj, cum, cum_t, *, name, tq=512, tk=512, comm=None):
    t = proj.shape[0]
    c = GROUP_WIDTH
    tq, tk, nq, nk = _attn_tiles(t, tq, tk)
    assert tq == tk
    qb = P_FOX // c
    scale = HEAD_DIM ** -0.5
    cr3 = cum_t.reshape(8, nk, tk).transpose(1, 0, 2)

    def body(q_ref, k_ref, v_ref, cc_ref, cr_ref, o_ref, lse_ref, m_scr, l_scr, acc_scr, km_scr):
        i = pl.program_id(0)
        masks = _head_masks(c)

        @pl.when(i == 0)
        def _():
            km_scr[...] = _key_norm_max(k_ref, nk, tk, masks)

        q = q_ref[...] * scale
        qh = [jnp.where(masks[h], q, 0.0).astype(MXU_DT) for h in range(GROUP_HEADS)]
        cc = [cc_ref[:, FOX_COL + h:FOX_COL + h + 1] for h in range(GROUP_HEADS)]
        reach = _fox_reach(qh, km_scr[...], cc_ref)
        m_scr[...] = jnp.full_like(m_scr, NEG_BIG)
        l_scr[...] = jnp.zeros_like(l_scr)
        acc_scr[...] = jnp.zeros_like(acc_scr)

        def tile(j, diagonal):
            rows = pl.ds(pl.multiple_of(j * tk, tk), tk)
            kb = k_ref[rows, :].astype(MXU_DT)
            vb = v_ref[rows, :].astype(MXU_DT)
            crj = cr_ref[j]
            if diagonal:
                causal = (lax.broadcasted_iota(jnp.int32, (tq, tk), 1) <= lax.broadcasted_iota(jnp.int32, (tq, tk), 0))
            acc = acc_scr[...]
            for h in range(GROUP_HEADS):
                u = _dot_nt(qh[h], kb) - crj[h:h + 1, :]
                if diagonal:
                    u = jnp.where(causal, u, NEG_BIG)
                m_old = m_scr[h]
                m_new = jnp.maximum(m_old, jnp.max(u, axis=-1, keepdims=True) + cc[h])
                p = jnp.exp(u - (m_new - cc[h]))
                alpha = jnp.exp(m_old - m_new)
                l_scr[h] = alpha * l_scr[h] + jnp.sum(p, axis=-1, keepdims=True)
                m_scr[h] = m_new
                acc = jnp.where(masks[h], alpha * acc + _dot(p.astype(MXU_DT), vb), acc)
            acc_scr[...] = acc

        def alive(j):
            return _fox_alive(reach, [m_scr[h] for h in range(GROUP_HEADS)], cr_ref, j, tk)

        def step(state):
            j = i - state[0]
            tile(j, False)
            return state[0] + 1, alive(j - 1)

        tile(i, True)
        lax.while_loop(lambda s: (s[0] <= i) & (s[1] > 0), step, (jnp.int32(1), alive(i - 1)))
        acc = acc_scr[...]
        o = jnp.zeros_like(acc)
        lse = jnp.zeros((tq, LANES), F32)
        lane = lax.broadcasted_iota(jnp.int32, (1, LANES), 1)
        for h in range(GROUP_HEADS):
            o = jnp.where(masks[h], acc / l_scr[h], o)
            lse = jnp.where(lane == h, m_scr[h] + jnp.log(l_scr[h]), lse)
        o_ref[...] = o
        lse_ref[...] = lse

    resident = lambda blk: pl.BlockSpec((t, c), lambda i: (0, blk), pipeline_mode=pl.Buffered(1))
    call = dict(
        name=name, grid=(nq,),
        in_specs=[pl.BlockSpec((tq, c), lambda i: (i, qb)), resident(qb + 1), resident(qb + 2),
                  pl.BlockSpec((tq, LANES), lambda i: (i, 0)),
                  pl.BlockSpec((nk, 8, tk), lambda i: (0, 0, 0), pipeline_mode=pl.Buffered(1))],
        out_specs=[pl.BlockSpec((tq, c), lambda i: (i, 0)), pl.BlockSpec((tq, LANES), lambda i: (i, 0))],
        out_shape=[jax.ShapeDtypeStruct((t, c), F32), jax.ShapeDtypeStruct((t, LANES), F32)],
        scratch_shapes=[pltpu.VMEM((GROUP_HEADS, tq, 1), F32), pltpu.VMEM((GROUP_HEADS, tq, 1), F32),
                        pltpu.VMEM((tq, c), F32), pltpu.VMEM((1, LANES), F32)],
        compiler_params=_cparams(("arbitrary",)),
    )
    outs, got = carry_comm(call, body, (proj, proj, proj, cum, cr3), comm, 2, *_grid_ends(nq))
    return (*outs, got)


def fox_bwd(proj, cum, cum_t, o, lse, do, *, name, tq=512, tk=512, comm=None):
    t = proj.shape[0]
    c = GROUP_WIDTH
    tq, tk, nq, nk = _attn_tiles(t, tq, tk)
    assert tq == tk
    qb = P_FOX // c
    scale = HEAD_DIM ** -0.5
    cr3 = cum_t.reshape(8, nk, tk).transpose(1, 0, 2)

    def body(q_ref, k_ref, v_ref, cc_ref, cr_ref, o_ref, lse_ref, do_ref,
             dq_ref, dk_hbm, dv_hbm, dcc_ref, dcr_ref, dq_scr, rs_scr, dk_scr, dv_scr, km_scr):
        i = pl.program_id(0)
        masks = _head_masks(c)

        @pl.when(i == 0)
        def _():
            dk_scr[...] = jnp.zeros_like(dk_scr)
            dv_scr[...] = jnp.zeros_like(dv_scr)
            dcr_ref[...] = jnp.zeros_like(dcr_ref)
            km_scr[...] = _key_norm_max(k_ref, nk, tk, masks)

        q = q_ref[...]
        qf = q.astype(MXU_DT)
        qh = [jnp.where(masks[h], q * scale, 0.0).astype(MXU_DT) for h in range(GROUP_HEADS)]
        do = do_ref[...]
        dob = do.astype(MXU_DT)
        doh = [jnp.where(masks[h], do, 0.0).astype(MXU_DT) for h in range(GROUP_HEADS)]
        doo = do * o_ref[...]
        delta = [jnp.sum(jnp.where(masks[h], doo, 0.0), axis=-1, keepdims=True) for h in range(GROUP_HEADS)]
        lse_h = [lse_ref[:, h:h + 1] for h in range(GROUP_HEADS)]
        off = [lse_h[h] - cc_ref[:, FOX_COL + h:FOX_COL + h + 1] for h in range(GROUP_HEADS)]
        reach = _fox_reach(qh, km_scr[...], cc_ref)
        dq_scr[...] = jnp.zeros_like(dq_scr)
        rs_scr[...] = jnp.zeros_like(rs_scr)

        def tile(j, diagonal):
            rows = pl.ds(pl.multiple_of(j * tk, tk), tk)
            kb = k_ref[rows, :].astype(MXU_DT)
            vb = v_ref[rows, :].astype(MXU_DT)
            crj = cr_ref[j]
            if diagonal:
                causal = (lax.broadcasted_iota(jnp.int32, (tq, tk), 1) <= lax.broadcasted_iota(jnp.int32, (tq, tk), 0))
            dq = dq_scr[...]
            dk_upd = jnp.zeros((tk, c), F32)
            dv_upd = jnp.zeros((tk, c), F32)
            for h in range(GROUP_HEADS):
                p = jnp.exp(_dot_nt(qh[h], kb) - crj[h:h + 1, :] - off[h])
                if diagonal:
                    p = jnp.where(causal, p, 0.0)
                ds = p * (_dot_nt(doh[h], vb) - delta[h])
                dsb = ds.astype(MXU_DT)
                dq = jnp.where(masks[h], dq + _dot(dsb, kb) * scale, dq)
                dk_upd = jnp.where(masks[h], _dot_tn(dsb, qf) * scale, dk_upd)
                dv_upd = jnp.where(masks[h], _dot_tn(p.astype(MXU_DT), dob), dv_upd)
                dcr_ref[j, h:h + 1, :] += -jnp.sum(ds, axis=0, keepdims=True)
                rs_scr[h] += jnp.sum(ds, axis=-1, keepdims=True)
            dq_scr[...] = dq
            dk_scr[rows, :] += dk_upd
            dv_scr[rows, :] += dv_upd

        def alive(j):
            return _fox_alive(reach, lse_h, cr_ref, j, tk)

        def step(state):
            j = i - state[0]
            tile(j, False)
            return state[0] + 1, alive(j - 1)

        tile(i, True)
        lax.while_loop(lambda s: (s[0] <= i) & (s[1] > 0), step, (jnp.int32(1), alive(i - 1)))
        dq_ref[...] = dq_scr[...]
        lane = lax.broadcasted_iota(jnp.int32, (1, LANES), 1)
        dcc = jnp.zeros((tq, LANES), F32)
        for h in range(GROUP_HEADS):
            dcc = jnp.where(lane == FOX_COL + h, rs_scr[h], dcc)
        dcc_ref[...] = dcc

        @pl.when(i == nq - 1)
        def _():
            pltpu.sync_copy(dk_scr, dk_hbm)
            pltpu.sync_copy(dv_scr, dv_hbm)

    qrow = lambda i: (i, 0)
    resident = lambda blk: pl.BlockSpec((t, c), lambda i: (0, blk), pipeline_mode=pl.Buffered(1))
    hbm = pl.BlockSpec(memory_space=pl.ANY)
    call = dict(
        name=name, grid=(nq,),
        in_specs=[pl.BlockSpec((tq, c), lambda i: (i, qb)), resident(qb + 1), resident(qb + 2),
                  pl.BlockSpec((tq, LANES), qrow),
                  pl.BlockSpec((nk, 8, tk), lambda i: (0, 0, 0), pipeline_mode=pl.Buffered(1)),
                  pl.BlockSpec((tq, c), qrow), pl.BlockSpec((tq, LANES), qrow), pl.BlockSpec((tq, c), qrow)],
        out_specs=[pl.BlockSpec((tq, c), qrow), hbm, hbm, pl.BlockSpec((tq, LANES), qrow),
                   pl.BlockSpec((nk, 8, tk), lambda i: (0, 0, 0))],
        out_shape=[jax.ShapeDtypeStruct((t, c), F32), jax.ShapeDtypeStruct((t, c), F32),
                   jax.ShapeDtypeStruct((t, c), F32), jax.ShapeDtypeStruct((t, LANES), F32),
                   jax.ShapeDtypeStruct((nk, 8, tk), F32)],
        scratch_shapes=[pltpu.VMEM((tq, c), F32), pltpu.VMEM((GROUP_HEADS, tq, 1), F32),
                        pltpu.VMEM((t, c), F32), pltpu.VMEM((t, c), F32), pltpu.VMEM((1, LANES), F32)],
        compiler_params=_cparams(("arbitrary",)),
    )
    outs, got = carry_comm(call, body, (proj, proj, proj, cum, cr3, o, lse, do), comm, 5, *_grid_ends(nq))
    return (*outs, got)


SB_DEAD = -110.0


def _sb_logs(z, strict):
    tt = jnp.log(1.0 + jnp.exp(-jnp.abs(z)))
    log_keep = jnp.where(strict, -(jnp.maximum(z, 0.0) + tt), 0.0)
    log_beta = jnp.minimum(z, 0.0) - tt
    return log_keep, log_beta


def _tri(n, upper):
    a = lax.broadcasted_iota(jnp.int32, (n, n), 0)
    b = lax.broadcasted_iota(jnp.int32, (n, n), 1)
    return jnp.where((a < b) if upper else (a > b), 1.0, 0.0).astype(MXU_DT)


def _sb_carry_lane(jj, h):
    return GROUP_HEADS * jj + h


def sb_fwd(proj, *, name, tq=512, tk=256, comm=None):
    t = proj.shape[0]
    c = GROUP_WIDTH
    tq, tk, nq, nk = _attn_tiles(t, tq, tk)
    assert nk * GROUP_HEADS <= LANES
    qb = P_SB // c
    scale = HEAD_DIM ** -0.5

    def body(q_ref, k_ref, v_ref, o_ref, rs_ref, r_scr, acc_scr):
        i = pl.program_id(0)
        last = ((i + 1) * tq - 1) // tk
        masks = _head_masks(c)
        q = q_ref[...]
        qh = [jnp.where(masks[h], q, 0.0).astype(MXU_DT) for h in range(GROUP_HEADS)]
        lane = lax.broadcasted_iota(jnp.int32, (1, LANES), 1)
        later = _tri(tk, upper=False)
        r_scr[...] = jnp.zeros_like(r_scr)
        acc_scr[...] = jnp.zeros_like(acc_scr)
        rs_ref[...] = jnp.full((tq, LANES), 2.0 * SB_DEAD, F32)

        def step(state):
            jj, _ = state
            j = last - jj
            rows = pl.ds(pl.multiple_of(j * tk, tk), tk)
            kb = k_ref[rows, :].astype(MXU_DT)
            vb = v_ref[rows, :].astype(MXU_DT)
            row = i * tq + lax.broadcasted_iota(jnp.int32, (tq, tk), 0)
            col = j * tk + lax.broadcasted_iota(jnp.int32, (tq, tk), 1)
            strict = col < row
            acc = acc_scr[...]
            rs = rs_ref[...]
            for h in range(GROUP_HEADS):
                z = _dot_nt(qh[h], kb) * scale
                log_keep, log_beta = _sb_logs(z, strict)
                r_old = r_scr[h]
                rs = jnp.where(lane == _sb_carry_lane(jj, h), r_old, rs)
                rest = r_old + _split_dot(log_keep, later)
                w = jnp.where(strict, jnp.exp(log_beta + rest), 0.0)
                acc = jnp.where(masks[h], acc + _dot(w.astype(MXU_DT), vb), acc)
                r_scr[h] = r_old + jnp.sum(log_keep, axis=-1, keepdims=True)
            acc_scr[...] = acc
            rs_ref[...] = rs
            return jj + 1, jnp.max(r_scr[...])

        lax.while_loop(lambda s: (s[0] <= last) & (s[1] > SB_DEAD), step, (jnp.int32(0), jnp.float32(0.0)))
        o_ref[...] = acc_scr[...]

    resident = lambda blk: pl.BlockSpec((t, c), lambda i: (0, blk), pipeline_mode=pl.Buffered(1))
    call = dict(
        name=name, grid=(nq,),
        in_specs=[pl.BlockSpec((tq, c), lambda i: (i, qb)), resident(qb + 1), resident(qb + 2)],
        out_specs=[pl.BlockSpec((tq, c), lambda i: (i, 0)), pl.BlockSpec((tq, LANES), lambda i: (i, 0))],
        out_shape=[jax.ShapeDtypeStruct((t, c), F32), jax.ShapeDtypeStruct((t, LANES), F32)],
        scratch_shapes=[pltpu.VMEM((GROUP_HEADS, tq, 1), F32), pltpu.VMEM((tq, c), F32)],
        compiler_params=_cparams(("arbitrary",)),
    )
    outs, got = carry_comm(call, body, (proj, proj, proj), comm, 2, *_grid_ends(nq))
    return (*outs, got)


def sb_bwd(proj, rsave, do, *, name, tq=512, tk=256, comm=None):
    t = proj.shape[0]
    c = GROUP_WIDTH
    tq, tk, nq, nk = _attn_tiles(t, tq, tk)
    qb = P_SB // c
    scale = HEAD_DIM ** -0.5

    def body(q_ref, k_ref, v_ref, rs_ref, do_ref, dq_ref, dk_hbm, dv_hbm, e_scr, dq_scr, dk_scr, dv_scr):
        i = pl.program_id(0)
        last = ((i + 1) * tq - 1) // tk
        masks = _head_masks(c)

        @pl.when(i == 0)
        def _():
            dk_scr[...] = jnp.zeros_like(dk_scr)
            dv_scr[...] = jnp.zeros_like(dv_scr)

        e_scr[...] = jnp.zeros_like(e_scr)
        dq_scr[...] = jnp.zeros_like(dq_scr)
        q = q_ref[...]
        qf = q.astype(MXU_DT)
        qh = [jnp.where(masks[h], q, 0.0).astype(MXU_DT) for h in range(GROUP_HEADS)]
        do = do_ref[...]
        dob = do.astype(MXU_DT)
        doh = [jnp.where(masks[h], do, 0.0).astype(MXU_DT) for h in range(GROUP_HEADS)]
        later = _tri(tk, upper=False)
        earlier = _tri(tk, upper=True)
        rs = rs_ref[...]
        lane = lax.broadcasted_iota(jnp.int32, (1, LANES), 1)
        visited = jnp.where(jnp.max(rs, axis=0, keepdims=True) > SB_DEAD, (lane // GROUP_HEADS + 1).astype(F32), 0.0)
        n_visited = jnp.minimum(jnp.max(visited).astype(jnp.int32), last + 1)

        def step(it, carry):
            jj = n_visited - 1 - it
            j = last - jj
            rows = pl.ds(pl.multiple_of(j * tk, tk), tk)
            kb = k_ref[rows, :].astype(MXU_DT)
            vb = v_ref[rows, :].astype(MXU_DT)
            row = i * tq + lax.broadcasted_iota(jnp.int32, (tq, tk), 0)
            col = j * tk + lax.broadcasted_iota(jnp.int32, (tq, tk), 1)
            strict = col < row
            dq = dq_scr[...]
            dk_upd = jnp.zeros((tk, c), F32)
            dv_upd = jnp.zeros((tk, c), F32)
            for h in range(GROUP_HEADS):
                z = _dot_nt(qh[h], kb) * scale
                log_keep, log_beta = _sb_logs(z, strict)
                r_h = jnp.sum(jnp.where(lane == _sb_carry_lane(jj, h), rs, 0.0), axis=-1, keepdims=True)
                rest = r_h + _split_dot(log_keep, later)
                w = jnp.where(strict, jnp.exp(log_beta + rest), 0.0)
                e = w * _dot_nt(doh[h], vb)
                e_old = e_scr[h]
                dkeep = e_old + _split_dot(e, earlier)
                dz = jnp.where(strict, e * jnp.exp(log_keep) - dkeep * jnp.exp(log_beta), 0.0)
                dzb = dz.astype(MXU_DT)
                dq = jnp.where(masks[h], dq + _dot(dzb, kb) * scale, dq)
                dk_upd = jnp.where(masks[h], _dot_tn(dzb, qf) * scale, dk_upd)
                dv_upd = jnp.where(masks[h], _dot_tn(w.astype(MXU_DT), dob), dv_upd)
                e_scr[h] = e_old + jnp.sum(e, axis=-1, keepdims=True)
            dq_scr[...] = dq
            dk_scr[rows, :] += dk_upd
            dv_scr[rows, :] += dv_upd
            return carry

        lax.fori_loop(0, n_visited, step, 0)
        dq_ref[...] = dq_scr[...]

        @pl.when(i == nq - 1)
        def _():
            pltpu.sync_copy(dk_scr, dk_hbm)
            pltpu.sync_copy(dv_scr, dv_hbm)

    qrow = lambda i: (i, 0)
    resident = lambda blk: pl.BlockSpec((t, c), lambda i: (0, blk), pipeline_mode=pl.Buffered(1))
    hbm = pl.BlockSpec(memory_space=pl.ANY)
    call = dict(
        name=name, grid=(nq,),
        in_specs=[pl.BlockSpec((tq, c), lambda i: (i, qb)), resident(qb + 1), resident(qb + 2),
                  pl.BlockSpec((tq, LANES), qrow), pl.BlockSpec((tq, c), qrow)],
        out_specs=[pl.BlockSpec((tq, c), qrow), hbm, hbm],
        out_shape=[jax.ShapeDtypeStruct((t, c), F32)] * 3,
        scratch_shapes=[pltpu.VMEM((GROUP_HEADS, tq, 1), F32), pltpu.VMEM((tq, c), F32),
                        pltpu.VMEM((t, c), F32), pltpu.VMEM((t, c), F32)],
        compiler_params=_cparams(("arbitrary",)),
    )
    outs, got = carry_comm(call, body, (proj, proj, proj, rsave, do), comm, 3, *_grid_ends(nq))
    return (*outs, got)


A_COL, B_COL = 0, 4
Z_BLK = P_Z // GROUP_WIDTH
GDN_CHUNKS_PER_STEP = 4


NN = (((1,), (0,)), ((), ()))
NT = (((1,), (1,)), ((), ()))
TN = (((0,), (0,)), ((), ()))


def _terms(x, n):
    out, rem = [], x
    for _ in range(n):
        t = rem.astype(MXU_DT)
        out.append(t)
        rem = rem - t.astype(F32)
    return out


def _dotp(a, b, dims, a_terms=2, b_terms=2):
    at, bt = _terms(a, a_terms), _terms(b, b_terms)
    out = None
    for i, x in enumerate(at):
        for j, y in enumerate(bt):
            if i + j < max(a_terms, b_terms):
                r = lax.dot_general(x, y, dims, preferred_element_type=F32)
                out = r if out is None else out + r
    return out


def _silu(x):
    return x * _sigmoid(x)


def _dsilu(x):
    s = _sigmoid(x)
    return s * (1.0 + x * (1.0 - s))


def _head_sum(x, masks):
    out = jnp.zeros_like(x)
    for m in masks:
        out = jnp.where(m, jnp.sum(jnp.where(m, x, 0.0), axis=-1, keepdims=True), out)
    return out


def _expand(cols, col0, masks):
    out = jnp.zeros((cols.shape[0], GROUP_WIDTH), F32)
    for h, m in enumerate(masks):
        out = jnp.where(m, cols[:, col0 + h:col0 + h + 1], out)
    return out


def _reduce(x, col0, masks):
    lane = lax.broadcasted_iota(jnp.int32, (1, LANES), 1)
    out = jnp.zeros((x.shape[0], LANES), F32)
    for h, m in enumerate(masks):
        out = jnp.where(lane == col0 + h, jnp.sum(jnp.where(m, x, 0.0), axis=-1, keepdims=True), out)
    return out


def _block_ones():
    ri = lax.broadcasted_iota(jnp.int32, (GROUP_WIDTH, GROUP_WIDTH), 0) // HEAD_DIM
    ci = lax.broadcasted_iota(jnp.int32, (GROUP_WIDTH, GROUP_WIDTH), 1) // HEAD_DIM
    return jnp.where(ri == ci, 1.0, 0.0).astype(F32)


def _blk(x, hs):
    return jnp.concatenate([x] * GROUP_HEADS, axis=0) * hs


def _unblk(m, hs):
    mm = m * hs
    c = GDN_CHUNK
    return mm[0:c] + mm[c:2 * c] + mm[2 * c:3 * c] + mm[3 * c:4 * c]


def _row_mask4():
    ri = lax.broadcasted_iota(jnp.int32, (GROUP_WIDTH, LANES), 0) // HEAD_DIM
    ci = lax.broadcasted_iota(jnp.int32, (GROUP_WIDTH, LANES), 1)
    return jnp.where(ri + A_COL == ci, 1.0, 0.0).astype(F32)


def _lockstep(gens):
    results = [None] * len(gens)
    live = list(range(len(gens)))
    while live:
        for i in list(live):
            try:
                next(gens[i])
            except StopIteration as stop:
                results[i] = stop.value
                live.remove(i)
    return results


def _gdn_chunk(xc, small, avec, dtvec, state, masks, hs):
    (f,) = _lockstep([_gdn_local(xc, small, avec, dtvec, masks, hs)])
    return _gdn_recur(f, state, hs)


def _gdn_local(xc, small, avec, dtvec, masks, hs):
    c = GDN_CHUNK
    w = GROUP_WIDTH
    b16 = lambda v: v.astype(MXU_DT)
    f = {}
    xq, xk, xv = xc[:, :w], xc[:, w:2 * w], xc[:, 2 * w:]
    qs, ks, v = _silu(xq), _silu(xk), _silu(xv)
    rq = lax.rsqrt(_head_sum(qs * qs, masks) + L2_EPS)
    rk = lax.rsqrt(_head_sum(ks * ks, masks) + L2_EPS)
    qn = qs * rq
    k = ks * rk
    q = qn * (HEAD_DIM ** -0.5)
    xg = small + dtvec
    sp = jnp.maximum(xg, 0.0) + jnp.log(1.0 + jnp.exp(-jnp.abs(xg)))
    g128 = -avec * sp
    beta128 = _sigmoid(small)
    ri = lax.broadcasted_iota(jnp.int32, (c, c), 0)
    ci = lax.broadcasted_iota(jnp.int32, (c, c), 1)
    tril = jnp.where(ri >= ci, 1.0, 0.0).astype(F32)
    gam128 = _dotp(tril, g128, NN, 1, 3)
    yield
    gam = _expand(gam128, A_COL, masks)
    bfull = _expand(beta128, B_COL, masks)
    mask4 = _row_mask4()
    ones = jnp.ones((c, LANES), F32)
    gam_row = _dotp(ones, jnp.concatenate([gam128] * GROUP_HEADS, axis=0) * mask4, NT, 1, 3)
    yield
    li = lax.broadcasted_iota(jnp.int32, (c, w), 0)
    lj = lax.broadcasted_iota(jnp.int32, (c, w), 1) % HEAD_DIM
    incl = li >= lj
    strict = li > lj
    dmat = jnp.exp(jnp.where(incl, gam - gam_row, NEG_BIG))
    egam = jnp.exp(gam)
    glast = gam[c - 1:c, :]
    ekd = jnp.exp(glast - gam)
    kb = k * bfull
    vb = v * bfull
    kbg = kb * egam
    qd = q * egam
    kd = k * ekd
    kblk = b16(_blk(k, hs))
    araw = _dot_nt(b16(kb), kblk)
    qk = _dot_nt(b16(q), kblk)
    yield
    a = jnp.where(strict, araw * dmat, 0.0)
    tm = jnp.where(li == lj, 1.0, 0.0) - a
    p = a
    for _ in range(5):
        p = _dotp(p, _blk(p, hs), NN)
        yield
        tm = tm + _dotp(tm, _blk(p, hs), NN)
        yield
    tm16 = b16(tm)
    u = _dot(tm16, b16(_blk(vb, hs)))
    wm = _dot(tm16, b16(_blk(kbg, hs)))
    aqk = jnp.where(incl, qk * dmat, 0.0)
    f.update(xq=xq, xk=xk, xv=xv, v=v, rq=rq, rk=rk, qn=qn, k=k, q=q, xg=xg, g128=g128, beta128=beta128,
             tril=tril, gam=gam, bfull=bfull, mask4=mask4, ones=ones, incl=incl, strict=strict, li=li,
             dmat=dmat, egam=egam, glast=glast, ekd=ekd, kb=kb, vb=vb, kbg=kbg, qd=qd, kd=kd, kblk=kblk,
             araw=araw, tm=tm, tm16=tm16, wm=wm, qk=qk, aqk=aqk, u=u)
    return f


def _gdn_recur(f, state, hs):
    b16 = lambda v: v.astype(MXU_DT)
    s16 = b16(state)
    vn = f["u"] - _dot(b16(f["wm"]), s16)
    o = _dot(b16(f["qd"]), s16) + _dot(b16(f["aqk"]), b16(_blk(vn, hs)))
    s_new = state * jnp.exp(f["glast"]) + hs * _dot_tn(b16(f["kd"]), b16(vn))
    f.update(s16=s16, vn=vn, o=o, s_new=s_new)
    return f


def _decay_rate(a_log):
    lane = lax.broadcasted_iota(jnp.int32, a_log.shape, 1)
    return jnp.where((lane >= A_COL) & (lane < A_COL + GROUP_HEADS), jnp.exp(a_log), 0.0)


def _gdn_post(o, z, ng, masks):
    r = lax.rsqrt(_head_sum(o * o, masks) * (1.0 / HEAD_DIM) + RMS_EPS)
    on = o * r
    return on, r, on * ng * _silu(z)


def gdn_fwd(cqkv, proj, avec, dtvec, ng, *, name, comm=None):
    t = cqkv.shape[0]
    c = GDN_CHUNK
    w = GROUP_WIDTH
    n = t // c

    def body(x_ref, z_ref, sm_ref, a_ref, dt_ref, ng_ref, y_ref, st_ref, s_scr):
        @pl.when(pl.program_id(0) == 0)
        def _():
            s_scr[...] = jnp.zeros_like(s_scr)

        masks = _head_masks(w)
        hs = _block_ones()
        avec_v = _decay_rate(a_ref[...])
        rows = [pl.ds(k * c, c) for k in range(sub)]
        fs = _lockstep([_gdn_local(x_ref[r, :], sm_ref[r, :], avec_v, dt_ref[...], masks, hs) for r in rows])
        state = s_scr[...]
        for k, r in enumerate(rows):
            st_ref[k] = state
            f = _gdn_recur(fs[k], state, hs)
            _, _, y = _gdn_post(f["o"], z_ref[r, :], ng_ref[...], masks)
            y_ref[r, :] = y
            state = f["s_new"]
        s_scr[...] = state

    sub = GDN_CHUNKS_PER_STEP if n % GDN_CHUNKS_PER_STEP == 0 else 1
    rows, steps = c * sub, n // sub
    vec = pl.BlockSpec((1, LANES), lambda i: (0, 0))
    call = dict(
        name=name, grid=(steps,),
        in_specs=[pl.BlockSpec((rows, 3 * w), lambda i: (i, 0)),
                  pl.BlockSpec((rows, w), lambda i: (i, Z_BLK)),
                  pl.BlockSpec((rows, LANES), lambda i: (i, SMALL_BLK)),
                  vec, vec, pl.BlockSpec((1, w), lambda i: (0, 0))],
        out_specs=[pl.BlockSpec((rows, w), lambda i: (i, 0)), pl.BlockSpec((sub, w, w), lambda i: (i, 0, 0))],
        out_shape=[jax.ShapeDtypeStruct((t, w), F32), jax.ShapeDtypeStruct((n, w, w), F32)],
        scratch_shapes=[pltpu.VMEM((w, w), F32)],
        compiler_params=_cparams(("arbitrary",)),
    )
    outs, got = carry_comm(call, body, (cqkv, proj, proj, avec, dtvec, ng), comm, 2, *_grid_ends(steps))
    return (*outs, got)


def gdn_bwd(cqkv, proj, avec, dtvec, ng, states, dy, *, name, comm=None):
    t = cqkv.shape[0]
    c = GDN_CHUNK
    w = GROUP_WIDTH
    n = t // c
    b16 = lambda v: v.astype(MXU_DT)

    def body(x_ref, z_ref, sm_ref, a_ref, dt_ref, ng_ref, st_ref, dy_ref,
             dx_ref, dz_ref, dsm_ref, dng_ref, dal_ref, ddt_ref, ds_scr):
        @pl.when(pl.program_id(0) == 0)
        def _():
            ds_scr[...] = jnp.zeros_like(ds_scr)
            dng_ref[...] = jnp.zeros_like(dng_ref)
            dal_ref[...] = jnp.zeros_like(dal_ref)
            ddt_ref[...] = jnp.zeros_like(ddt_ref)

        masks = _head_masks(w)
        hs = _block_ones()
        avec_v = _decay_rate(a_ref[...])
        rows = [pl.ds(k * c, c) for k in range(sub)]
        fs = _lockstep([_gdn_local(x_ref[r, :], sm_ref[r, :], avec_v, dt_ref[...], masks, hs) for r in rows])
        fs = [_gdn_recur(f, st_ref[k], hs) for k, f in enumerate(fs)]
        _lockstep([chunk(fs[k], st_ref[k], avec_v, masks, hs, z_ref.at[r, :], ng_ref, dy_ref.at[r, :], dx_ref.at[r, :],
                         dz_ref.at[r, :], dsm_ref.at[r, :], dng_ref, dal_ref, ddt_ref, ds_scr)
                   for k, r in reversed(list(enumerate(rows)))])

    def chunk(f, state, avec_v, masks, hs, z_ref, ng_ref, dy_ref, dx_ref, dz_ref, dsm_ref, dng_ref, dal_ref, ddt_ref,
              ds_scr):
        z = z_ref[...]
        ng_v = ng_ref[...]
        dy_v = dy_ref[...]
        on, r, _ = _gdn_post(f["o"], z, ng_v, masks)
        sz = _silu(z)
        dz_ref[...] = dy_v * on * ng_v * _dsilu(z)
        d_on = dy_v * ng_v * sz
        dng_ref[...] += jnp.sum(dy_v * on * sz, axis=0, keepdims=True)
        do = r * (d_on - on * _head_sum(d_on * on, masks) * (1.0 / HEAD_DIM))
        do16 = b16(do)
        dsn = ds_scr[...]
        dsn16 = b16(dsn)
        s16, vn, kd, qd, wm = f["s16"], f["vn"], f["kd"], f["qd"], f["wm"]
        k, q, kblk, tm, tm16 = f["k"], f["q"], f["kblk"], f["tm"], f["tm16"]
        dmat, egam, glast, gam = f["dmat"], f["egam"], f["glast"], f["gam"]
        incl, strict, li = f["incl"], f["strict"], f["li"]
        vn16 = b16(vn)
        dvn = _unblk(_dot_tn(b16(f["aqk"]), do16), hs) + _dot(b16(kd), dsn16)
        daqk = jnp.where(incl, _dot_nt(do16, b16(_blk(vn, hs))), 0.0)
        dqd = _dot_nt(do16, s16)
        dvn16 = b16(dvn)
        ds_scr[...] = hs * (_dot_tn(b16(qd), do16) - _dot_tn(b16(wm), dvn16)) + dsn * jnp.exp(glast)
        yield
        dkd = _dot_nt(vn16, dsn16)
        dglast = jnp.sum(dsn * state, axis=0, keepdims=True) * jnp.exp(glast)
        du16 = dvn16
        dw16 = b16(-_dot_nt(dvn16, s16))
        yield
        dqk16 = b16(daqk * dmat)
        ddm = daqk * f["qk"]
        dq = _dot(dqk16, kblk)
        dk = _unblk(_dot_tn(dqk16, b16(q)), hs)
        dtm = _dot_nt(du16, b16(_blk(f["vb"], hs))) + _dot_nt(dw16, b16(_blk(f["kbg"], hs)))
        dvb = _unblk(_dot_tn(tm16, du16), hs)
        dkbg = _unblk(_dot_tn(tm16, dw16), hs)
        yield
        xx = _unblk(_dotp(tm, dtm, TN), hs)
        yield
        da = jnp.where(strict, -_dotp(xx, _blk(tm, hs), NT), 0.0)
        yield
        daraw16 = b16(da * dmat)
        ddm = ddm + da * f["araw"]
        dkb = _dot(daraw16, kblk)
        dk = dk + _unblk(_dot_tn(daraw16, b16(f["kb"])), hs)
        yield
        tcol = ddm * dmat
        dgam = tcol
        dgam128_row = _dotp(-tcol, f["ones"], TN, 2, 1) * f["mask4"]
        yield
        dgam128_row = (dgam128_row[0:c] + dgam128_row[c:2 * c] + dgam128_row[2 * c:3 * c] + dgam128_row[3 * c:4 * c])
        dk = dk + dkd * f["ekd"]
        tt = dkd * kd
        dgam = dgam - tt
        dglast = dglast + jnp.sum(tt, axis=0, keepdims=True)
        dq = dq + dqd * egam
        dgam = dgam + dqd * qd
        dkb = dkb + dkbg * egam
        dgam = dgam + dkbg * f["kbg"]
        dk = dk + dkb * f["bfull"]
        dbf = dkb * k + dvb * f["v"]
        dv = dvb * f["bfull"]
        dgam = dgam + jnp.where(li == c - 1, dglast, 0.0)
        beta128 = f["beta128"]
        db128 = _reduce(dbf, B_COL, masks) * beta128 * (1.0 - beta128)
        dgam128 = _reduce(dgam, A_COL, masks) + dgam128_row
        dg128 = _dotp(f["tril"], dgam128, TN, 1, 2)
        yield
        dxg = dg128 * (-avec_v * _sigmoid(f["xg"]))
        lane = lax.broadcasted_iota(jnp.int32, (1, LANES), 1)
        dsm_ref[...] = jnp.where(lane < B_COL, dxg, db128)
        ddt_ref[...] += jnp.sum(dxg, axis=0, keepdims=True)
        dal_ref[...] += jnp.sum(dg128 * f["g128"], axis=0, keepdims=True)
        dqn = dq * (HEAD_DIM ** -0.5)
        dqs = f["rq"] * (dqn - f["qn"] * _head_sum(dqn * f["qn"], masks))
        dks = f["rk"] * (dk - k * _head_sum(dk * k, masks))
        dx_ref[:, :w] = dqs * _dsilu(f["xq"])
        dx_ref[:, w:2 * w] = dks * _dsilu(f["xk"])
        dx_ref[:, 2 * w:] = dv * _dsilu(f["xv"])

    sub = GDN_CHUNKS_PER_STEP if n % GDN_CHUNKS_PER_STEP == 0 else 1
    rows, steps = c * sub, n // sub
    vec = pl.BlockSpec((1, LANES), lambda i: (0, 0))
    rev = lambda blk: (lambda i: (steps - 1 - i, blk))
    call = dict(
        name=name, grid=(steps,),
        in_specs=[pl.BlockSpec((rows, 3 * w), rev(0)),
                  pl.BlockSpec((rows, w), rev(Z_BLK)),
                  pl.BlockSpec((rows, LANES), rev(SMALL_BLK)),
                  vec, vec, pl.BlockSpec((1, w), lambda i: (0, 0)),
                  pl.BlockSpec((sub, w, w), lambda i: (steps - 1 - i, 0, 0)),
                  pl.BlockSpec((rows, w), rev(0))],
        out_specs=[pl.BlockSpec((rows, 3 * w), rev(0)), pl.BlockSpec((rows, w), rev(0)),
                   pl.BlockSpec((rows, LANES), rev(0)),
                   pl.BlockSpec((1, w), lambda i: (0, 0)), vec, vec],
        out_shape=[jax.ShapeDtypeStruct((t, 3 * w), F32), jax.ShapeDtypeStruct((t, w), F32),
                   jax.ShapeDtypeStruct((t, LANES), F32), jax.ShapeDtypeStruct((1, w), F32),
                   jax.ShapeDtypeStruct((1, LANES), F32), jax.ShapeDtypeStruct((1, LANES), F32)],
        scratch_shapes=[pltpu.VMEM((w, w), F32)],
        compiler_params=_cparams(("arbitrary",)),
    )
    outs, got = carry_comm(call, body, (cqkv, proj, proj, avec, dtvec, ng, states, dy), comm, 6, *_grid_ends(steps))
    return (*outs, got)


def adamw(w, m, v, gslots, *, row0=0, name):
    r, c = w.shape
    s = gslots.shape[0]
    tr = _tile(r, 64, 8)
    assert row0 % tr == 0 and gslots.shape[2] == c
    rb = row0 // tr
    c1 = 1.0 - ADAM_B1 ** ADAM_STEP
    c2 = 1.0 - ADAM_B2 ** ADAM_STEP

    def body(w_ref, m_ref, v_ref, gs_ref, g_ref, d_ref, mo_ref, vo_ref):
        g = gs_ref[0].astype(F32)
        for k in range(1, s):
            g = g + gs_ref[k].astype(F32)
        m_new = ADAM_B1 * m_ref[...] + (1.0 - ADAM_B1) * g
        v_new = ADAM_B2 * v_ref[...] + (1.0 - ADAM_B2) * (g * g)
        m_hat = m_new / c1
        v_hat = v_new / c2
        g_ref[...] = g
        mo_ref[...] = m_new
        vo_ref[...] = v_new
        d_ref[...] = -ADAM_LR * (m_hat / (jnp.sqrt(v_hat) + ADAM_EPS) + ADAM_WD * w_ref[...])

    row = pl.BlockSpec((tr, c), lambda i: (i, 0))
    return pl.pallas_call(
        body, name=name, grid=(r // tr,),
        in_specs=[row, row, row, pl.BlockSpec((s, tr, c), lambda i: (0, rb + i, 0))],
        out_specs=[row] * 4,
        out_shape=[jax.ShapeDtypeStruct((r, c), F32)] * 4,
        compiler_params=_cparams(("parallel",)),
    )(w, m, v, gslots)


def slot_sum(slots, *, name):
    s, r, c = slots.shape

    def body(s_ref, o_ref):
        acc = s_ref[0]
        for k in range(1, s):
            acc = acc + s_ref[k]
        o_ref[...] = acc

    return pl.pallas_call(
        body, name=name, grid=(1,),
        in_specs=[pl.BlockSpec((s, r, c), lambda i: (0, 0, 0))],
        out_specs=pl.BlockSpec((r, c), lambda i: (0, 0)),
        out_shape=jax.ShapeDtypeStruct((r, c), F32),
        compiler_params=_cparams(("arbitrary",)),
    )(slots)


class Comm:
    def __init__(self, srcs, broadcast):
        self.srcs = list(srcs)
        self.broadcast = [broadcast] * len(self.srcs) if isinstance(broadcast, bool) else list(broadcast)
        self.n = len(self.srcs)
        self.out_shapes = [jax.ShapeDtypeStruct(((N_DEV,) + s.shape) if b else s.shape, s.dtype)
                           for s, b in zip(self.srcs, self.broadcast)]
        self.sems = [pltpu.SemaphoreType.DMA((self.n,))] * 3

    def _local(self, src_refs, out_refs, loc_sem, a, me):
        src = src_refs[a] if self.broadcast[a] else src_refs[a].at[me]
        return pltpu.make_async_copy(src, out_refs[a].at[me], loc_sem.at[a])

    def start(self, src_refs, out_refs, send_sem, recv_sem, loc_sem):
        x, y, c = lax.axis_index("x"), lax.axis_index("y"), lax.axis_index("c")
        me = 4 * x + 2 * y + c
        for a in range(self.n):
            self._local(src_refs, out_refs, loc_sem, a, me).start()
        for d in range(1, N_DEV):
            px, py, pc = x ^ ((d >> 2) & 1), y ^ ((d >> 1) & 1), c ^ (d & 1)
            peer = 4 * px + 2 * py + pc
            for a in range(self.n):
                src = src_refs[a] if self.broadcast[a] else src_refs[a].at[peer]
                pltpu.make_async_remote_copy(
                    src_ref=src, dst_ref=out_refs[a].at[me],
                    send_sem=send_sem.at[a], recv_sem=recv_sem.at[a],
                    device_id=(px, py, pc), device_id_type=pl.DeviceIdType.MESH).start()

    def wait(self, src_refs, out_refs, send_sem, recv_sem, loc_sem):
        x, y, c = lax.axis_index("x"), lax.axis_index("y"), lax.axis_index("c")
        me = 4 * x + 2 * y + c
        for a in range(self.n):
            seven = out_refs[a].at[pl.ds(0, N_DEV - 1)]
            pltpu.make_async_remote_copy(
                src_ref=seven, dst_ref=seven, send_sem=send_sem.at[a], recv_sem=recv_sem.at[a],
                device_id=(x, y, c), device_id_type=pl.DeviceIdType.MESH).wait()
            self._local(src_refs, out_refs, loc_sem, a, me).wait()


def exchange(srcs, *, broadcast, name):
    comm = Comm(srcs, broadcast)
    n = comm.n

    def body(*refs):
        src_refs, out_refs, sems = refs[:n], refs[n:2 * n], refs[2 * n:]
        comm.start(src_refs, out_refs, *sems)
        comm.wait(src_refs, out_refs, *sems)

    anyspec = pl.BlockSpec(memory_space=pl.ANY)
    return pl.pallas_call(
        body, name=name,
        in_specs=[anyspec] * n, out_specs=[anyspec] * n, out_shape=comm.out_shapes,
        scratch_shapes=comm.sems,
        compiler_params=pltpu.CompilerParams(has_side_effects=True),
    )(*srcs)


def carry_comm(call_kwargs, body, args, comm, n_out, is_first, is_last):
    if comm is None:
        return pl.pallas_call(body, **call_kwargs)(*args), []
    n_in, nc = len(args), comm.n
    n_scr = len(call_kwargs["scratch_shapes"])
    anyspec = pl.BlockSpec(memory_space=pl.ANY)

    def wrapped(*refs):
        ins, csrc = refs[:n_in], refs[n_in:n_in + nc]
        outs = refs[n_in + nc:n_in + nc + n_out]
        cout = refs[n_in + nc + n_out:n_in + 2 * nc + n_out]
        rest = refs[n_in + 2 * nc + n_out:]
        scr, sems = rest[:n_scr], rest[n_scr:]

        @pl.when(is_first())
        def _():
            comm.start(csrc, cout, *sems)

        body(*ins, *outs, *scr)

        @pl.when(is_last())
        def _():
            comm.wait(csrc, cout, *sems)

    kw = dict(call_kwargs)
    kw["in_specs"] = list(kw["in_specs"]) + [anyspec] * nc
    kw["out_specs"] = list(kw["out_specs"]) + [anyspec] * nc
    kw["out_shape"] = list(kw["out_shape"]) + comm.out_shapes
    kw["scratch_shapes"] = list(kw["scratch_shapes"]) + comm.sems
    cp = kw["compiler_params"]
    kw["compiler_params"] = pltpu.CompilerParams(dimension_semantics=cp.dimension_semantics,
                                                 vmem_limit_bytes=cp.vmem_limit_bytes, has_side_effects=True)
    res = pl.pallas_call(wrapped, **kw)(*args, *comm.srcs)
    return res[:n_out], res[n_out:]


def _pack(arrs):
    flat = []
    for a in arrs:
        f = a.reshape(-1).astype(F32)
        flat.append(jnp.pad(f, (0, (-f.shape[0]) % LANES)))
    buf = jnp.concatenate(flat)
    buf = jnp.pad(buf, (0, (-buf.shape[0]) % (8 * LANES)))
    return buf.reshape(-1, LANES)


def _unpack(buf, shapes):
    flat = buf.reshape(-1)
    out, off = [], 0
    for s in shapes:
        sz = int(np.prod(s))
        out.append(flat[off:off + sz].reshape(s))
        off += sz + (-sz) % LANES
    return out


def _win_to_aligned(w):
    o = np.cumsum((0,) + IN_SPLITS)
    seg = lambda i: w[..., o[i]:o[i + 1]]
    pad = jnp.zeros(w.shape[:-1] + (P_WIDTH - IN_WIDTH,), w.dtype)
    return jnp.concatenate([seg(0), seg(1), seg(4), seg(6), seg(7), seg(2), seg(3), seg(5), pad], axis=-1)


def _win_from_aligned(w):
    o = np.cumsum((0,) + IN_SPLITS)
    s = P_SMALL
    return jnp.concatenate([w[..., P_GDN:P_GDN + 768], w[..., P_Z:P_Z + 256], w[..., s:s + 4], w[..., s + 4:s + 8],
                            w[..., P_FOX:P_FOX + 768], w[..., s + 8:s + 12], w[..., P_CONF:P_CONF + 512],
                            w[..., P_SB:P_SB + 768]], axis=-1)


def _row128(vals, col0):
    return jnp.pad(vals.astype(F32)[None, :], ((0, 0), (col0, LANES - col0 - GROUP_HEADS)))


def _ffn_fwd(x, x16, w, n, tag, comm=None, on_comm=None):
    gu = mm(x16, w[f"gu{n}"], name=f"{tag}_gu", tm=1024, tn=512, tk=1024, out_dtype=MXU_DT, comm=comm)
    if comm is not None:
        gu, got = gu
        on_comm(got)
    h = act_fwd(gu, name=f"{tag}_act")
    y = mm(h, w[f"d{n}"], name=f"{tag}_down", tm=1024, tn=512, tk=D_FF)
    out, out16, xh, rs = ln_res_fwd(x, y, w[f"ln_ffn{n}_g"], w[f"ln_ffn{n}_b"], 0.5, name=f"{tag}_ln")
    return out, out16, (x16, gu, h, xh, rs)


def _ffn_bwd(dout, saved, w, n, tag, comm_dh=None, comm_dwgu=None, comm_dx=None):
    x, gu, h, xh, rs = saved
    wgu, wd = w[f"gu{n}"], w[f"d{n}"]
    got = [[], [], []]
    dz, dg, db = ln_res_bwd(dout, xh, rs, w[f"ln_ffn{n}_g"], name=f"{tag}_ln_bwd")
    dh = mm(dz, wd, mode="nt", alpha=0.5, name=f"{tag}_dh", tm=1024, tn=D_FF // 2, tk=1024, comm=comm_dh)
    if comm_dh is not None:
        dh, got[0] = dh
    dgu = act_bwd(gu, dh, name=f"{tag}_act_bwd")
    dwd = mm(h, dz, mode="tn", alpha=0.5, name=f"{tag}_dwd", tm=D_FF // 2, tn=1024, tk=512)
    c = comm_dwgu(dwd) if comm_dwgu is not None else None
    dwgu = mm(x, dgu, mode="tn", name=f"{tag}_dwgu", tm=1024, tn=D_FF // 2, tk=512, comm=c)
    if c is not None:
        dwgu, got[1] = dwgu
    c = comm_dx(dwgu) if comm_dx is not None else None
    dx = mm(dgu, wgu, mode="nt", add=dz, beta=DN_ALPHA, name=f"{tag}_dx", tm=1024, tn=1024, tk=D_FF // 2, comm=c)
    if c is not None:
        dx, got[2] = dx
    return dx, dwgu, dwd, dg, db, got


def _layer_fwd(x, x16, mem, w, tag, comm_ffn1=None, on_ffn1=None, comm_gdn=None, on_gdn=None, comm_fox=None,
               on_fox=None, comm_sb=None, on_sb=None):
    sv = {}
    x1, x1h, sv["ffn1"] = _ffn_fwd(x, x16, w, 1, f"{tag}_ffn1", comm=comm_ffn1, on_comm=on_ffn1)
    proj = mm(x1h, w["win"], name=f"{tag}_inproj", tm=1024, tn=640, tk=1024)
    cqkv = dwconv_fwd(proj, w["gdn_conv_w"], None, col0=P_GDN, name=f"{tag}_gdn_conv")
    ya, states, got = gdn_fwd(cqkv, proj, w["alog"], w["dtb"], w["ng"], name=f"{tag}_gdn", comm=comm_gdn)
    if on_gdn is not None:
        on_gdn(got)
    cum = fox_gate_fwd(proj, w["bf"], name=f"{tag}_fox_gate")
    cum_t = jnp.pad(cum[:, FOX_COL:FOX_COL + GROUP_HEADS].T, ((0, 8 - GROUP_HEADS), (0, 0)))
    yb, lse, got = fox_fwd(proj, cum, cum_t, name=f"{tag}_fox", comm=comm_fox)
    if on_fox is not None:
        on_fox(got)
    u = glu_fwd(proj, name=f"{tag}_glu")
    cc = dwconv_fwd(u, w["conf_dw_w"], w["conf_dw_b"], name=f"{tag}_conf_conv")
    yc = gn_silu_fwd(cc, w["conf_norm_g"], w["conf_norm_b"], name=f"{tag}_conf_norm")
    yd, rsave, got = sb_fwd(proj, name=f"{tag}_sb", comm=comm_sb)
    if on_sb is not None:
        on_sb(got)
    ycat = jnp.concatenate([ya, yb, yc, yd], axis=1).astype(MXU_DT)
    mix = mm(ycat, w["wout"], name=f"{tag}_outproj")
    x2, x2h, xh2, rs2 = ln_res_fwd(x1, mix, w["ln_mix_g"], w["ln_mix_b"], 1.0, name=f"{tag}_ln_mix")
    sv["mix"] = (x1h, proj, cqkv, states, cum, cum_t, yb, lse, u, cc, rsave, ycat, xh2, rs2)
    q = mm(x2h, w["wq"], name=f"{tag}_memq", out_dtype=MXU_DT)
    kv = mm(mem, w["wkv"], name=f"{tag}_memkv", tm=N_MEM, out_dtype=MXU_DT)
    att = memattn_fwd(q, kv, name=f"{tag}_memattn")
    mo = mm(att, w["wo"], name=f"{tag}_memo")
    x3, x3h, xh3, rs3 = ln_res_fwd(x2, mo, w["ln_mem_g"], w["ln_mem_b"], 1.0, name=f"{tag}_ln_mem")
    sv["mem"] = (x2h, q, kv, att, xh3, rs3)
    x4, x4h, sv["ffn2"] = _ffn_fwd(x3, x3h, w, 2, f"{tag}_ffn2")
    return x4, x4h, sv


def _layer_bwd(dx4, mem, sv, w, tag, plan, tail=None):
    t = dx4.shape[0]
    gr = {}
    dx3, gr["gu2"], gr["d2"], gr["ln_ffn2_g"], gr["ln_ffn2_b"], _ = _ffn_bwd(dx4, sv["ffn2"], w, 2, f"{tag}_ffn2")
    x2, q, kv, att, xh3, rs3 = sv["mem"]
    dz, gr["ln_mem_g"], gr["ln_mem_b"] = ln_res_bwd(dx3, xh3, rs3, w["ln_mem_g"], name=f"{tag}_ln_mem_bwd")
    datt = mm(dz, w["wo"], mode="nt", name=f"{tag}_datt", out_dtype=MXU_DT)
    gr["wo"] = mm(att, dz, mode="tn", name=f"{tag}_dwo", tk=512)
    dq, dkv = memattn_bwd(q, kv, datt, name=f"{tag}_memattn_bwd")
    gr["wq"] = mm(x2, dq, mode="tn", name=f"{tag}_dwq", tk=512)
    gr["wkv"] = mm(mem, dkv, mode="tn", name=f"{tag}_dwkv", tk=N_MEM)
    dx2 = mm(dq, w["wq"], mode="nt", add=dz, beta=DN_ALPHA, name=f"{tag}_dx2")
    x1, proj, cqkv, states, cum, cum_t, yb, lse, u, cc, rsave, ycat, xh2, rs2 = sv["mix"]
    dz, gr["ln_mix_g"], gr["ln_mix_b"] = ln_res_bwd(dx2, xh2, rs2, w["ln_mix_g"], name=f"{tag}_ln_mix_bwd")
    dycat = mm(dz, w["wout"], mode="nt", name=f"{tag}_dycat")
    gr["wout"] = mm(ycat, dz, mode="tn", name=f"{tag}_dwout", tk=512)
    comm_sb, comm_fox, comm_gdn = plan(gr)
    gw = GROUP_WIDTH
    dya, dyb, dyc, dyd = (dycat[:, i * gw:(i + 1) * gw] for i in range(4))
    dq_d, dk_d, dv_d, got_sb = sb_bwd(proj, rsave, dyd, name=f"{tag}_sb_bwd", comm=comm_sb)
    dcc, gr["conf_norm_g"], gr["conf_norm_b"] = gn_silu_bwd(cc, w["conf_norm_g"], w["conf_norm_b"], dyc,
                                                            name=f"{tag}_conf_norm_bwd")
    du, gr["conf_dw_w"], gr["conf_dw_b"] = dwconv_bwd(dcc, u, w["conf_dw_w"], name=f"{tag}_conf_conv_bwd")
    dglu = glu_bwd(proj, du, name=f"{tag}_glu_bwd")
    dq_b, dk_b, dv_b, dcc, dcr, got_fox = fox_bwd(proj, cum, cum_t, yb, lse, dyb, name=f"{tag}_fox_bwd", comm=comm_fox)
    dcum = dcc + jnp.pad(dcr[:, :GROUP_HEADS, :].transpose(0, 2, 1).reshape(t, GROUP_HEADS),
                   ((0, 0), (FOX_COL, LANES - FOX_COL - GROUP_HEADS)))
    dsm_f, dbf = fox_gate_bwd(dcum, proj, w["bf"], name=f"{tag}_fox_gate_bwd")
    gr["fox_b_f"] = dbf[0, FOX_COL:FOX_COL + GROUP_HEADS]
    dcq, dz_a, dsm_a, dng, dal, ddt, got_gdn = gdn_bwd(cqkv, proj, w["alog"], w["dtb"], w["ng"], states, dya,
                                                       name=f"{tag}_gdn_bwd", comm=comm_gdn)
    gr["gdn_norm_g"] = dng.reshape(GROUP_HEADS, HEAD_DIM).sum(0)
    gr["gdn_a_log"] = dal[0, A_COL:A_COL + GROUP_HEADS]
    gr["gdn_dt_bias"] = ddt[0, A_COL:A_COL + GROUP_HEADS]
    dgq, gr["gdn_conv_w"], _ = dwconv_bwd(dcq, proj, w["gdn_conv_w"], col0=P_GDN, name=f"{tag}_gdn_conv_bwd")
    dproj = jnp.concatenate([dgq, dz_a, dq_b, dk_b, dv_b, dglu, dq_d, dk_d, dv_d, dsm_a + dsm_f],
                            axis=1).astype(MXU_DT)
    gr["win"] = mm(x1, dproj, mode="tn", name=f"{tag}_dwin", tm=1024, tn=640, tk=512)
    dx1 = mm(dproj, w["win"], mode="nt", add=dz, beta=DN_ALPHA, name=f"{tag}_dx1", tm=1024, tn=1024, tk=640)
    tail = {} if tail is None else dict(tail, comm_dh=tail["comm_dh"](gr))
    dx0, gr["gu1"], gr["d1"], gr["ln_ffn1_g"], gr["ln_ffn1_b"], got_tail = _ffn_bwd(
        dx1, sv["ffn1"], w, 1, f"{tag}_ffn1", **tail)
    return dx0, gr, (got_sb, got_fox, got_gdn), got_tail


SMALL_REPLICATED = ("ln_ffn1_g", "ln_ffn1_b", "gdn_a_log", "gdn_dt_bias", "gdn_norm_g", "fox_b_f", "conf_dw_b",
                    "conf_norm_g", "conf_norm_b", "ln_mix_g", "ln_mix_b", "ln_mem_g", "ln_mem_b", "ln_ffn2_g",
                    "ln_ffn2_b")
SMALL_SHARDED = ("gdn_conv_w", "conf_dw_w")
BIG = ("ffn1_w_gate", "ffn1_w_up", "ffn1_w_down", "w_in", "w_out", "mem_w_q", "mem_w_kv", "mem_w_o",
       "ffn2_w_gate", "ffn2_w_up", "ffn2_w_down")
WEIGHT_ORDER = ("ffn1_w_gate", "ffn1_w_up", "ffn1_w_down", "ln_ffn1_g", "ln_ffn1_b", "w_in", "gdn_conv_w", "gdn_a_log",
                "gdn_dt_bias", "gdn_norm_g", "fox_b_f", "conf_dw_w", "conf_dw_b", "conf_norm_g", "conf_norm_b", "w_out",
                "ln_mix_g", "ln_mix_b", "mem_w_q", "mem_w_kv", "mem_w_o", "ln_mem_g", "ln_mem_b", "ffn2_w_gate",
                "ffn2_w_up", "ffn2_w_down", "ln_ffn2_g", "ln_ffn2_b")


def _step(x, mem, loss_target, wts, ms, vs):
    me = 4 * lax.axis_index("x") + 2 * lax.axis_index("y") + lax.axis_index("c")
    x = x[0]
    mem = mem[0]
    target = loss_target[0]
    rows_s = D_MODEL // N_DEV
    first, rest = ("gu1", "d1", "win"), ("sq", "kv", "gu2", "d2")

    def shards(l):
        c = lambda k: wts[k][l].astype(MXU_DT)
        return dict(gu1=jnp.stack([c("ffn1_w_gate"), c("ffn1_w_up")]), d1=c("ffn1_w_down"),
                    win=_win_to_aligned(wts["w_in"][l]).astype(MXU_DT),
                    sq=jnp.stack([c("w_out"), c("mem_w_q"), c("mem_w_o")]), kv=c("mem_w_kv"),
                    gu2=jnp.stack([c("ffn2_w_gate"), c("ffn2_w_up")]), d2=c("ffn2_w_down"))

    def to_compute_layout(w, keys, got):
        for k, g in zip(keys, got):
            if k in ("gu1", "gu2"):
                w[k] = g.transpose(2, 1, 0, 3).reshape(D_MODEL, 2 * D_FF)
            elif k in ("d1", "d2"):
                w[k] = g.reshape(D_FF, D_MODEL)
            elif k == "win":
                w[k] = g.reshape(D_MODEL, P_WIDTH)
            elif k == "sq":
                full = g.transpose(1, 0, 2, 3).reshape(3, D_MODEL, D_MODEL)
                w["wout"], w["wq"], w["wo"] = full[0], full[1], full[2]
            else:
                w["wkv"] = g.transpose(1, 0, 2).reshape(D_MODEL, 2 * D_MODEL)

    def chunks(gr, keys, dtype=MXU_DT):
        out = []
        for k in keys:
            if k in ("gu1", "gu2"):
                out.append(gr[k].reshape(D_MODEL, 2, N_DEV, -1).transpose(2, 1, 0, 3))
            elif k in ("d1", "d2"):
                out.append(gr[k].reshape(N_DEV, -1, D_MODEL))
            elif k == "win":
                out.append(gr[k].reshape(N_DEV, rows_s, P_WIDTH))
            elif k == "sq":
                out.append(jnp.stack([gr[n].reshape(N_DEV, rows_s, D_MODEL) for n in ("wout", "wq", "wo")], axis=1))
            else:
                out.append(gr["wkv"].reshape(D_MODEL, N_DEV, -1).transpose(1, 0, 2))
        return [a.astype(dtype) for a in out]

    sh = [shards(l) for l in range(DEPTH)]
    sm_sh = _pack([wts["gdn_conv_w"], wts["conf_dw_w"]])
    got = exchange([sh[0]["gu1"], sm_sh], broadcast=True, name="gather_first")
    conv_shapes = [wts["gdn_conv_w"].shape, wts["conf_dw_w"].shape]
    parts = [_unpack(got[-1][j], conv_shapes) for j in range(N_DEV)]
    gconv_full = jnp.concatenate([p[0] for p in parts], axis=-1)
    cconv_full = jnp.concatenate([p[1] for p in parts], axis=-1)

    def small_weights(l):
        w = dict(gdn_conv_w=gconv_full[l], conf_dw_w=cconv_full[l],
                 alog=_row128(wts["gdn_a_log"][l], A_COL), dtb=_row128(wts["gdn_dt_bias"][l], A_COL),
                 bf=_row128(wts["fox_b_f"][l], FOX_COL), ng=jnp.tile(wts["gdn_norm_g"][l], GROUP_HEADS)[None, :])
        for k in ("ln_ffn1_g", "ln_ffn1_b", "conf_dw_b", "conf_norm_g", "conf_norm_b", "ln_mix_g", "ln_mix_b",
                  "ln_mem_g", "ln_mem_b", "ln_ffn2_g", "ln_ffn2_b"):
            w[k] = wts[k][l][None, :]
        return w

    lw = [small_weights(l) for l in range(DEPTH)]
    to_compute_layout(lw[0], ("gu1",), got[:-1])

    def take(l, keys):
        return lambda g: to_compute_layout(lw[l], keys, g)

    def take_fox0(g):
        to_compute_layout(lw[0], rest[2:], g[:2])
        to_compute_layout(lw[1], first[:2], g[2:])

    h, h16, sv0 = _layer_fwd(
        x, x.astype(MXU_DT), mem, lw[0], "l0",
        comm_ffn1=Comm([sh[0][k] for k in first[1:]], True), on_ffn1=take(0, first[1:]),
        comm_gdn=Comm([sh[0][k] for k in rest[:2]], True), on_gdn=take(0, rest[:2]),
        comm_fox=Comm([sh[0][k] for k in rest[2:]] + [sh[1][k] for k in first[:2]], True), on_fox=take_fox0,
        comm_sb=Comm([sh[1]["win"]], True), on_sb=take(1, ("win",)))
    h, _, sv1 = _layer_fwd(
        h, h16, mem, lw[1], "l1",
        comm_gdn=Comm([sh[1][k] for k in rest[:2]], True), on_gdn=take(1, rest[:2]),
        comm_fox=Comm([sh[1][k] for k in rest[2:]], True), on_fox=take(1, rest[2:]))
    dh, lpart = loss_head(h, target, name="loss_head")

    recv = [{}, {}]
    e_ffn, e_mem = ("gu2", "d2"), ("sq", "kv")
    dh, g1, got, _ = _layer_bwd(dh, mem, sv1, lw[1], "l1",
                                lambda gr: (None, Comm(chunks(gr, e_ffn), False), Comm(chunks(gr, e_mem), False)))
    recv[1].update(zip(e_ffn, got[1]))
    recv[1].update(zip(e_mem, got[2]))
    tail = dict(comm_dh=lambda gr: Comm(chunks(gr, ("win",)), False),
                comm_dwgu=lambda dwd: Comm(chunks({"d1": dwd}, ("d1",)), False),
                comm_dx=lambda dwgu: Comm(chunks({"gu1": dwgu}, ("gu1",)), False))
    dh, g0, got, got_t = _layer_bwd(
        dh, mem, sv0, lw[0], "l0",
        lambda gr: (Comm(chunks(gr, e_mem), False), Comm(chunks(g1, first[:2]), False),
                    Comm(chunks(g1, first[2:]) + chunks(gr, e_ffn), False)), tail)
    recv[0].update(zip(e_mem, got[0]))
    recv[1].update(zip(first[:2], got[1]))
    recv[1].update(win=got[2][0])
    recv[0].update(zip(e_ffn, got[2][1:]))
    recv[0].update(win=got_t[0][0], d1=got_t[1][0], gu1=got_t[2][0])
    grad_x = dh[None]
    grads = [g0, g1]

    def gl(k):
        return jnp.stack([grads[l][k] for l in range(DEPTH)])

    small_names = SMALL_REPLICATED + SMALL_SHARDED
    small_grads = [gl(k) for k in small_names] + [lpart[0, :1]]
    got = exchange([_pack(small_grads)], broadcast=True, name="gather_small_grads")
    sm_sum = slot_sum(got[0], name="sum_small_grads")
    sm_g = _unpack(sm_sum, [g.shape for g in small_grads])
    loss = sm_g[-1][0]
    small_g = dict(zip(small_names, sm_g[:-1]))
    for k in SMALL_SHARDED:
        width = wts[k].shape[-1]
        small_g[k] = lax.dynamic_slice_in_dim(small_g[k], me * width, width, axis=2)

    out_g, out_d, out_m, out_v = {}, {}, {}, {}

    def update(names, key, fix=lambda a: a):
        res = {k: [] for k in names}
        for l in range(DEPTH):
            slots = fix(recv[l][key])
            slots = slots.reshape(N_DEV, -1, slots.shape[-1])
            for i, k in enumerate(names):
                two = lambda a: a[l].reshape(-1, a.shape[-1])
                res[k].append(adamw(two(wts[k]), two(ms[k]), two(vs[k]), slots, row0=i * two(wts[k]).shape[0],
                                    name=f"adamw_{k}_l{l}"))
        for k in names:
            for dst, per_layer in zip((out_g, out_d, out_m, out_v), zip(*res[k])):
                dst[k] = jnp.stack(per_layer).reshape(wts[k].shape)

    update(("ffn1_w_gate", "ffn1_w_up"), "gu1")
    update(("ffn1_w_down",), "d1")
    update(("w_in",), "win", _win_from_aligned)
    update(("w_out", "mem_w_q", "mem_w_o"), "sq")
    update(("mem_w_kv",), "kv")
    update(("ffn2_w_gate", "ffn2_w_up"), "gu2")
    update(("ffn2_w_down",), "d2")

    sw = _pack([wts[k] for k in small_names])
    smm = _pack([ms[k] for k in small_names])
    smv = _pack([vs[k] for k in small_names])
    sg = _pack([small_g[k] for k in small_names])
    res = adamw(sw, smm, smv, sg[None], name="adamw_small")
    shapes = [wts[k].shape for k in small_names]
    for dst, buf in zip((out_g, out_d, out_m, out_v), res):
        for k, a in zip(small_names, _unpack(buf, shapes)):
            dst[k] = a

    return (loss, grad_x, *[out_g[k] for k in WEIGHT_ORDER], *[out_d[k] for k in WEIGHT_ORDER],
            *[out_m[k] for k in WEIGHT_ORDER], *[out_v[k] for k in WEIGHT_ORDER])


def kernel(x, mem, ffn1_w_gate, ffn1_w_up, ffn1_w_down, ln_ffn1_g, ln_ffn1_b, w_in, gdn_conv_w, gdn_a_log, gdn_dt_bias, gdn_norm_g, fox_b_f, conf_dw_w, conf_dw_b, conf_norm_g, conf_norm_b, w_out, ln_mix_g, ln_mix_b, mem_w_q, mem_w_kv, mem_w_o, ln_mem_g, ln_mem_b, ffn2_w_gate, ffn2_w_up, ffn2_w_down, ln_ffn2_g, ln_ffn2_b, loss_target, m_ffn1_w_gate, m_ffn1_w_up, m_ffn1_w_down, m_ln_ffn1_g, m_ln_ffn1_b, m_w_in, m_gdn_conv_w, m_gdn_a_log, m_gdn_dt_bias, m_gdn_norm_g, m_fox_b_f, m_conf_dw_w, m_conf_dw_b, m_conf_norm_g, m_conf_norm_b, m_w_out, m_ln_mix_g, m_ln_mix_b, m_mem_w_q, m_mem_w_kv, m_mem_w_o, m_ln_mem_g, m_ln_mem_b, m_ffn2_w_gate, m_ffn2_w_up, m_ffn2_w_down, m_ln_ffn2_g, m_ln_ffn2_b, v_ffn1_w_gate, v_ffn1_w_up, v_ffn1_w_down, v_ln_ffn1_g, v_ln_ffn1_b, v_w_in, v_gdn_conv_w, v_gdn_a_log, v_gdn_dt_bias, v_gdn_norm_g, v_fox_b_f, v_conf_dw_w, v_conf_dw_b, v_conf_norm_g, v_conf_norm_b, v_w_out, v_ln_mix_g, v_ln_mix_b, v_mem_w_q, v_mem_w_kv, v_mem_w_o, v_ln_mem_g, v_ln_mem_b, v_ffn2_w_gate, v_ffn2_w_up, v_ffn2_w_down, v_ln_ffn2_g, v_ln_ffn2_b):
    args = locals()
    wts = {k: args[k] for k in WEIGHT_ORDER}
    ms = {k: args["m_" + k] for k in WEIGHT_ORDER}
    vs = {k: args["v_" + k] for k in WEIGHT_ORDER}
    return _step(x, mem, loss_target, wts, ms, vs)
```

```python
import functools
import math

import jax
import jax.numpy as jnp
import numpy as np
from jax import lax
from jax.experimental import pallas as pl
from jax.experimental.pallas import tpu as pltpu

F32 = jnp.float32
BF16 = jnp.bfloat16
MXU_DT = jnp.bfloat16
HI = lax.Precision.HIGHEST

N_DEV = 8
VMEM_LIMIT_BYTES = 56 * 1024 * 1024
LANES = 128

D_MODEL = 1024
DEPTH = 2
GROUP_WIDTH = 256
HEAD_DIM = 64
GROUP_HEADS = 4
D_FF = 2816
SHORT_CONV = 4
CONF_KERNEL = 31
CONF_GROUPS = 4
GDN_CHUNK = 64
N_MEM = 256
MEM_HEADS = 4
MEM_HEAD_DIM = 256
DN_ALPHA = float((2 * DEPTH) ** 0.25)
LN_EPS = 1e-5
RMS_EPS = 1e-6
L2_EPS = 1e-6
NEG_BIG = -1e30
IN_SPLITS = (768, 256, 4, 4, 768, 4, 512, 768)
IN_WIDTH = sum(IN_SPLITS)
P_GDN, P_Z, P_FOX, P_CONF, P_SB, P_SMALL = 0, 768, 1024, 1792, 2304, 3072
P_WIDTH = 3200

ADAM_LR = 0.001
ADAM_B1 = 0.9
ADAM_B2 = 0.999
ADAM_EPS = 1e-08
ADAM_WD = 0.01
ADAM_STEP = 10


def _cparams(sem):
    return pltpu.CompilerParams(dimension_semantics=sem, vmem_limit_bytes=VMEM_LIMIT_BYTES)


def _tile(n, pref, align=LANES):
    if n <= pref:
        return n
    t = (pref // align) * align
    while t >= align:
        if n % t == 0:
            return t
        t -= align
    return n


def mm(a, b, *, mode="nn", add=None, alpha=1.0, beta=1.0, out_dtype=F32, name,
       tm=1024, tn=512, tk=1024, comm=None):
    if mode == "nn":
        (m, k), (k2, n) = a.shape, b.shape
    elif mode == "nt":
        (m, k), (n, k2) = a.shape, b.shape
    else:
        (k, m), (k2, n) = a.shape, b.shape
    assert k == k2, (a.shape, b.shape, mode)
    tm = _tile(m, tm, 8 if mode != "tn" else LANES)
    tn = _tile(n, tn)
    tk = _tile(k, tk, LANES if mode != "tn" else 8)
    nk = k // tk
    if mode == "nn":
        a_spec = pl.BlockSpec((tm, tk), lambda i, j, kk: (i, kk))
        b_spec = pl.BlockSpec((tk, tn), lambda i, j, kk: (kk, j))
        dims = (((1,), (0,)), ((), ()))
    elif mode == "nt":
        a_spec = pl.BlockSpec((tm, tk), lambda i, j, kk: (i, kk))
        b_spec = pl.BlockSpec((tn, tk), lambda i, j, kk: (j, kk))
        dims = (((1,), (1,)), ((), ()))
    else:
        a_spec = pl.BlockSpec((tk, tm), lambda i, j, kk: (kk, i))
        b_spec = pl.BlockSpec((tk, tn), lambda i, j, kk: (kk, j))
        dims = (((0,), (0,)), ((), ()))
    o_spec = pl.BlockSpec((tm, tn), lambda i, j, kk: (i, j))
    has_add = add is not None

    def body(*refs):
        if has_add:
            a_ref, b_ref, add_ref, o_ref, acc_ref = refs
        else:
            a_ref, b_ref, o_ref, acc_ref = refs
        kk = pl.program_id(2)

        @pl.when(kk == 0)
        def _():
            acc_ref[...] = jnp.zeros_like(acc_ref)

        acc_ref[...] += lax.dot_general(a_ref[...].astype(MXU_DT), b_ref[...].astype(MXU_DT), dims,
                                        preferred_element_type=F32)

        @pl.when(kk == nk - 1)
        def _():
            r = acc_ref[...]
            if alpha != 1.0:
                r = r * alpha
            if has_add:
                r = r + beta * add_ref[...].astype(F32)
            o_ref[...] = r.astype(out_dtype)

    in_specs = [a_spec, b_spec] + ([o_spec] if has_add else [])
    args = (a, b) + ((add,) if has_add else ())
    grid = (m // tm, n // tn, nk)
    call = dict(name=name, grid=grid, in_specs=in_specs, out_specs=[o_spec],
                out_shape=[jax.ShapeDtypeStruct((m, n), out_dtype)],
                scratch_shapes=[pltpu.VMEM((tm, tn), F32)],
                compiler_params=_cparams(("parallel", "parallel", "arbitrary")))
    (out,), got = carry_comm(call, body, args, comm, 1, *_grid_ends(*grid))
    return out if comm is None else (out, got)


def ln_res_fwd(x, y, g, b, s, *, name):
    t, d = x.shape
    tm = _tile(t, 512, 8)

    def body(x_ref, y_ref, g_ref, b_ref, o_ref, o16_ref, xh_ref, rs_ref):
        z = DN_ALPHA * x_ref[...] + s * y_ref[...]
        mu = jnp.mean(z, axis=-1, keepdims=True)
        zc = z - mu
        var = jnp.mean(zc * zc, axis=-1, keepdims=True)
        rstd = lax.rsqrt(var + LN_EPS)
        xh = zc * rstd
        xh_ref[...] = xh
        rs_ref[...] = jnp.broadcast_to(rstd, rs_ref.shape)
        out = xh * g_ref[...] + b_ref[...]
        o_ref[...] = out
        o16_ref[...] = out.astype(o16_ref.dtype)

    row = pl.BlockSpec((tm, d), lambda i: (i, 0))
    vec = pl.BlockSpec((1, d), lambda i: (0, 0))
    return pl.pallas_call(
        body, name=name, grid=(t // tm,),
        in_specs=[row, row, vec, vec],
        out_specs=[row, row, row, pl.BlockSpec((tm, LANES), lambda i: (i, 0))],
        out_shape=[jax.ShapeDtypeStruct((t, d), F32), jax.ShapeDtypeStruct((t, d), MXU_DT),
                   jax.ShapeDtypeStruct((t, d), F32), jax.ShapeDtypeStruct((t, LANES), F32)],
        compiler_params=_cparams(("parallel",)),
    )(x, y, g, b)


def ln_res_bwd(dout, xhat, rstd, g, *, name):
    t, d = dout.shape
    tm = _tile(t, 512, 8)

    def body(do_ref, xh_ref, rs_ref, g_ref, dz_ref, dg_ref, db_ref):
        i = pl.program_id(0)

        @pl.when(i == 0)
        def _():
            dg_ref[...] = jnp.zeros_like(dg_ref)
            db_ref[...] = jnp.zeros_like(db_ref)

        do = do_ref[...]
        xh = xh_ref[...]
        dxh = do * g_ref[...]
        m1 = jnp.mean(dxh, axis=-1, keepdims=True)
        m2 = jnp.mean(dxh * xh, axis=-1, keepdims=True)
        dz_ref[...] = rs_ref[:, 0:1] * (dxh - m1 - xh * m2)
        dg_ref[...] += jnp.sum(do * xh, axis=0, keepdims=True)
        db_ref[...] += jnp.sum(do, axis=0, keepdims=True)

    row = pl.BlockSpec((tm, d), lambda i: (i, 0))
    vec = pl.BlockSpec((1, d), lambda i: (0, 0))
    return pl.pallas_call(
        body, name=name, grid=(t // tm,),
        in_specs=[row, row, pl.BlockSpec((tm, LANES), lambda i: (i, 0)), vec],
        out_specs=[row, vec, vec],
        out_shape=[jax.ShapeDtypeStruct((t, d), F32), jax.ShapeDtypeStruct((1, d), F32),
                   jax.ShapeDtypeStruct((1, d), F32)],
        compiler_params=_cparams(("arbitrary",)),
    )(dout, xhat, rstd, g)


def _sigmoid(x):
    return 1.0 / (1.0 + jnp.exp(-x))


def act_fwd(gu, *, name):
    t, f2 = gu.shape
    f = f2 // 2
    tm = _tile(t, 256, 8)

    def body(gu_ref, h_ref):
        g = gu_ref[:, :f].astype(F32)
        h_ref[...] = (g * _sigmoid(g) * gu_ref[:, f:].astype(F32)).astype(h_ref.dtype)

    return pl.pallas_call(
        body, name=name, grid=(t // tm,),
        in_specs=[pl.BlockSpec((tm, f2), lambda i: (i, 0))],
        out_specs=pl.BlockSpec((tm, f), lambda i: (i, 0)),
        out_shape=jax.ShapeDtypeStruct((t, f), MXU_DT),
        compiler_params=_cparams(("parallel",)),
    )(gu)


def act_bwd(gu, dh, *, name):
    t, f2 = gu.shape
    f = f2 // 2
    tm = _tile(t, 256, 8)

    def body(gu_ref, dh_ref, o_ref):
        g = gu_ref[:, :f].astype(F32)
        u = gu_ref[:, f:].astype(F32)
        dh = dh_ref[...]
        sg = _sigmoid(g)
        o_ref[:, f:] = (dh * g * sg).astype(o_ref.dtype)
        o_ref[:, :f] = (dh * u * sg * (1.0 + g * (1.0 - sg))).astype(o_ref.dtype)

    return pl.pallas_call(
        body, name=name, grid=(t // tm,),
        in_specs=[pl.BlockSpec((tm, f2), lambda i: (i, 0)), pl.BlockSpec((tm, f), lambda i: (i, 0))],
        out_specs=pl.BlockSpec((tm, f2), lambda i: (i, 0)),
        out_shape=jax.ShapeDtypeStruct((t, f2), MXU_DT),
        compiler_params=_cparams(("parallel",)),
    )(gu, dh)


def loss_head(y, target, *, name):
    t, d = y.shape
    tm = _tile(t, 512, 8)

    def body(y_ref, t_ref, dy_ref, l_ref):
        i = pl.program_id(0)

        @pl.when(i == 0)
        def _():
            l_ref[...] = jnp.zeros_like(l_ref)

        err = y_ref[...] - t_ref[...]
        dy_ref[...] = err * (1.0 / d)
        part = jnp.sum(jnp.sum(err * err, axis=-1, keepdims=True), axis=0, keepdims=True)
        l_ref[...] += jnp.broadcast_to(part * (0.5 / d), l_ref.shape)

    row = pl.BlockSpec((tm, d), lambda i: (i, 0))
    return pl.pallas_call(
        body, name=name, grid=(t // tm,),
        in_specs=[row, row],
        out_specs=[row, pl.BlockSpec((1, LANES), lambda i: (0, 0))],
        out_shape=[jax.ShapeDtypeStruct((t, d), F32), jax.ShapeDtypeStruct((1, LANES), F32)],
        compiler_params=_cparams(("arbitrary",)),
    )(y, target)


def _dot(a, b):
    return lax.dot_general(a, b, (((1,), (0,)), ((), ())), preferred_element_type=F32)


def _dot_nt(a, b):
    return lax.dot_general(a, b, (((1,), (1,)), ((), ())), preferred_element_type=F32)


def _dot_tn(a, b):
    return lax.dot_general(a, b, (((0,), (0,)), ((), ())), preferred_element_type=F32)


def _dot_hi(a, b):
    return lax.dot_general(a, b, (((1,), (0,)), ((), ())), preferred_element_type=F32, precision=HI)


def _dot_nt_hi(a, b):
    return lax.dot_general(a, b, (((1,), (1,)), ((), ())), preferred_element_type=F32, precision=HI)


def _split_dot(x, u):
    hi = x.astype(MXU_DT)
    lo = (x - hi.astype(F32)).astype(MXU_DT)
    return _dot(hi, u) + _dot(lo, u)


def _mem_probs(q_ref, kv_ref, h):
    lo = h * MEM_HEAD_DIM
    qh = q_ref[:, lo:lo + MEM_HEAD_DIM].astype(MXU_DT)
    kh = kv_ref[:, lo:lo + MEM_HEAD_DIM].astype(MXU_DT)
    s = _dot_nt(qh, kh) * (MEM_HEAD_DIM ** -0.5)
    s = s - jnp.max(s, axis=-1, keepdims=True)
    p = jnp.exp(s)
    return p / jnp.sum(p, axis=-1, keepdims=True), qh, kh


def memattn_fwd(q, kv, *, name):
    t, d = q.shape
    tm = _tile(t, 512, 8)

    def body(q_ref, kv_ref, o_ref):
        for h in range(MEM_HEADS):
            lo = h * MEM_HEAD_DIM
            p, _, _ = _mem_probs(q_ref, kv_ref, h)
            vh = kv_ref[:, d + lo:d + lo + MEM_HEAD_DIM].astype(MXU_DT)
            o_ref[:, lo:lo + MEM_HEAD_DIM] = _dot(p.astype(MXU_DT), vh).astype(o_ref.dtype)

    return pl.pallas_call(
        body, name=name, grid=(t // tm,),
        in_specs=[pl.BlockSpec((tm, d), lambda i: (i, 0)), pl.BlockSpec(kv.shape, lambda i: (0, 0))],
        out_specs=pl.BlockSpec((tm, d), lambda i: (i, 0)),
        out_shape=jax.ShapeDtypeStruct((t, d), MXU_DT),
        compiler_params=_cparams(("parallel",)),
    )(q, kv)


def memattn_bwd(q, kv, datt, *, name):
    t, d = q.shape
    tm = _tile(t, 512, 8)
    scale = MEM_HEAD_DIM ** -0.5

    def body(q_ref, kv_ref, da_ref, dq_ref, dkv_ref):
        @pl.when(pl.program_id(0) == 0)
        def _():
            dkv_ref[...] = jnp.zeros_like(dkv_ref)

        for h in range(MEM_HEADS):
            lo = h * MEM_HEAD_DIM
            p, qh, kh = _mem_probs(q_ref, kv_ref, h)
            vh = kv_ref[:, d + lo:d + lo + MEM_HEAD_DIM].astype(MXU_DT)
            da = da_ref[:, lo:lo + MEM_HEAD_DIM].astype(MXU_DT)
            dp = _dot_nt(da, vh)
            ds = p * (dp - jnp.sum(dp * p, axis=-1, keepdims=True))
            dsb = ds.astype(MXU_DT)
            dq_ref[:, lo:lo + MEM_HEAD_DIM] = (_dot(dsb, kh) * scale).astype(dq_ref.dtype)
            dkv_ref[:, lo:lo + MEM_HEAD_DIM] += _dot_tn(dsb, qh) * scale
            dkv_ref[:, d + lo:d + lo + MEM_HEAD_DIM] += _dot_tn(p.astype(MXU_DT), da)

    row = pl.BlockSpec((tm, d), lambda i: (i, 0))
    full = pl.BlockSpec(kv.shape, lambda i: (0, 0))
    return pl.pallas_call(
        body, name=name, grid=(t // tm,),
        in_specs=[row, full, row],
        out_specs=[row, full],
        out_shape=[jax.ShapeDtypeStruct((t, d), MXU_DT), jax.ShapeDtypeStruct(kv.shape, F32)],
        compiler_params=_cparams(("arbitrary",)),
    )(q, kv, datt)


def _halo(k):
    return 8 * ((k - 1 + 7) // 8)


def dwconv_fwd(u, w, bias, *, col0=0, width=None, name):
    t = u.shape[0]
    kk, c = w.shape
    width = c if width is None else width
    assert width == c and col0 % c == 0
    cb = col0 // c
    hb = _halo(kk)
    tm = _tile(t, 512, hb)
    r = tm // hb
    has_bias = bias is not None

    def body(*refs):
        if has_bias:
            prev_ref, cur_ref, w_ref, b_ref, o_ref, scr = refs
        else:
            prev_ref, cur_ref, w_ref, o_ref, scr = refs
        i = pl.program_id(0)
        scr[0:hb, :] = jnp.where(i == 0, 0.0, prev_ref[...])
        scr[hb:hb + tm, :] = cur_ref[...]
        acc = jnp.zeros((tm, c), F32)
        for k in range(kk):
            acc = acc + w_ref[k:k + 1, :] * scr[pl.ds(hb - (kk - 1) + k, tm), :]
        if has_bias:
            acc = acc + b_ref[...]
        o_ref[...] = acc

    in_specs = [pl.BlockSpec((hb, c), lambda i: (jnp.maximum(i * r - 1, 0), cb)),
                pl.BlockSpec((tm, c), lambda i: (i, cb)),
                pl.BlockSpec((kk, c), lambda i: (0, 0))]
    args = [u, u, w]
    if has_bias:
        in_specs.append(pl.BlockSpec((1, c), lambda i: (0, 0)))
        args.append(bias)
    return pl.pallas_call(
        body, name=name, grid=(t // tm,),
        in_specs=in_specs,
        out_specs=pl.BlockSpec((tm, c), lambda i: (i, 0)),
        out_shape=jax.ShapeDtypeStruct((t, c), F32),
        scratch_shapes=[pltpu.VMEM((hb + tm, c), F32)],
        compiler_params=_cparams(("parallel",)),
    )(*args)


def dwconv_bwd(dc, u, w, *, col0=0, name):
    t, c = dc.shape
    kk = w.shape[0]
    assert col0 % c == 0
    cb = col0 // c
    hb = _halo(kk)
    tm = _tile(t, 512, hb)
    r = tm // hb
    n = t // tm

    def body(dcur_ref, dnext_ref, uprev_ref, ucur_ref, w_ref, du_ref, dw_ref, db_ref, sd, su):
        i = pl.program_id(0)

        @pl.when(i == 0)
        def _():
            dw_ref[...] = jnp.zeros_like(dw_ref)
            db_ref[...] = jnp.zeros_like(db_ref)

        dcur = dcur_ref[...]
        sd[0:tm, :] = dcur
        sd[tm:tm + hb, :] = jnp.where(i == n - 1, 0.0, dnext_ref[...])
        su[0:hb, :] = jnp.where(i == 0, 0.0, uprev_ref[...])
        su[hb:hb + tm, :] = ucur_ref[...]
        acc = jnp.zeros((tm, c), F32)
        for k in range(kk):
            acc = acc + w_ref[k:k + 1, :] * sd[pl.ds(kk - 1 - k, tm), :]
            dw_ref[k:k + 1, :] += jnp.sum(dcur * su[pl.ds(hb - (kk - 1) + k, tm), :], axis=0, keepdims=True)
        du_ref[...] = acc
        db_ref[...] += jnp.sum(dcur, axis=0, keepdims=True)

    return pl.pallas_call(
        body, name=name, grid=(n,),
        in_specs=[pl.BlockSpec((tm, c), lambda i: (i, 0)),
                  pl.BlockSpec((hb, c), lambda i: (jnp.minimum((i + 1) * r, n * r - 1), 0)),
                  pl.BlockSpec((hb, c), lambda i: (jnp.maximum(i * r - 1, 0), cb)),
                  pl.BlockSpec((tm, c), lambda i: (i, cb)),
                  pl.BlockSpec((kk, c), lambda i: (0, 0))],
        out_specs=[pl.BlockSpec((tm, c), lambda i: (i, 0)),
                   pl.BlockSpec((kk, c), lambda i: (0, 0)),
                   pl.BlockSpec((1, c), lambda i: (0, 0))],
        out_shape=[jax.ShapeDtypeStruct((t, c), F32), jax.ShapeDtypeStruct((kk, c), F32),
                   jax.ShapeDtypeStruct((1, c), F32)],
        scratch_shapes=[pltpu.VMEM((tm + hb, c), F32), pltpu.VMEM((hb + tm, c), F32)],
        compiler_params=_cparams(("arbitrary",)),
    )(dc, dc, u, u, w)


def glu_fwd(proj, *, name):
    t = proj.shape[0]
    c = GROUP_WIDTH
    tm = _tile(t, 1024, 8)
    vb, gb = P_CONF // c, P_CONF // c + 1

    def body(v_ref, g_ref, o_ref):
        o_ref[...] = v_ref[...] * _sigmoid(g_ref[...])

    return pl.pallas_call(
        body, name=name, grid=(t // tm,),
        in_specs=[pl.BlockSpec((tm, c), lambda i: (i, vb)), pl.BlockSpec((tm, c), lambda i: (i, gb))],
        out_specs=pl.BlockSpec((tm, c), lambda i: (i, 0)),
        out_shape=jax.ShapeDtypeStruct((t, c), F32),
        compiler_params=_cparams(("parallel",)),
    )(proj, proj)


def glu_bwd(proj, du, *, name):
    t = proj.shape[0]
    c = GROUP_WIDTH
    tm = _tile(t, 1024, 8)
    vb, gb = P_CONF // c, P_CONF // c + 1

    def body(v_ref, g_ref, du_ref, o_ref):
        sg = _sigmoid(g_ref[...])
        du = du_ref[...]
        o_ref[:, :c] = du * sg
        o_ref[:, c:] = du * v_ref[...] * sg * (1.0 - sg)

    return pl.pallas_call(
        body, name=name, grid=(t // tm,),
        in_specs=[pl.BlockSpec((tm, c), lambda i: (i, vb)), pl.BlockSpec((tm, c), lambda i: (i, gb)),
                  pl.BlockSpec((tm, c), lambda i: (i, 0))],
        out_specs=pl.BlockSpec((tm, 2 * c), lambda i: (i, 0)),
        out_shape=jax.ShapeDtypeStruct((t, 2 * c), F32),
        compiler_params=_cparams(("parallel",)),
    )(proj, proj, du)


def _group_mean_matrix(c, groups):
    gsz = c // groups
    ri = lax.broadcasted_iota(jnp.int32, (c, c), 0) // gsz
    ci = lax.broadcasted_iota(jnp.int32, (c, c), 1) // gsz
    return jnp.where(ri == ci, 1.0 / gsz, 0.0).astype(F32)


def gn_silu_fwd(cx, gamma, beta, *, name):
    t, c = cx.shape
    tm = _tile(t, 1024, 8)

    def body(c_ref, g_ref, b_ref, o_ref):
        gm = _group_mean_matrix(c, CONF_GROUPS)
        x = c_ref[...]
        mu = _dot_hi(x, gm)
        xc = x - mu
        var = _dot_hi(xc * xc, gm)
        a = xc * lax.rsqrt(var + LN_EPS) * g_ref[...] + b_ref[...]
        o_ref[...] = a * _sigmoid(a)

    row = pl.BlockSpec((tm, c), lambda i: (i, 0))
    vec = pl.BlockSpec((1, c), lambda i: (0, 0))
    return pl.pallas_call(
        body, name=name, grid=(t // tm,),
        in_specs=[row, vec, vec], out_specs=row,
        out_shape=jax.ShapeDtypeStruct((t, c), F32),
        compiler_params=_cparams(("parallel",)),
    )(cx, gamma, beta)


def gn_silu_bwd(cx, gamma, beta, dy, *, name):
    t, c = cx.shape
    tm = _tile(t, 1024, 8)

    def body(c_ref, g_ref, b_ref, dy_ref, dc_ref, dg_ref, db_ref):
        @pl.when(pl.program_id(0) == 0)
        def _():
            dg_ref[...] = jnp.zeros_like(dg_ref)
            db_ref[...] = jnp.zeros_like(db_ref)

        gm = _group_mean_matrix(c, CONF_GROUPS)
        x = c_ref[...]
        mu = _dot_hi(x, gm)
        xc = x - mu
        var = _dot_hi(xc * xc, gm)
        rstd = lax.rsqrt(var + LN_EPS)
        nrm = xc * rstd
        a = nrm * g_ref[...] + b_ref[...]
        sa = _sigmoid(a)
        da = dy_ref[...] * sa * (1.0 + a * (1.0 - sa))
        dg_ref[...] += jnp.sum(da * nrm, axis=0, keepdims=True)
        db_ref[...] += jnp.sum(da, axis=0, keepdims=True)
        dn = da * g_ref[...]
        dc_ref[...] = rstd * (dn - _dot_hi(dn, gm) - nrm * _dot_hi(dn * nrm, gm))

    row = pl.BlockSpec((tm, c), lambda i: (i, 0))
    vec = pl.BlockSpec((1, c), lambda i: (0, 0))
    return pl.pallas_call(
        body, name=name, grid=(t // tm,),
        in_specs=[row, vec, vec, row], out_specs=[row, vec, vec],
        out_shape=[jax.ShapeDtypeStruct((t, c), F32), jax.ShapeDtypeStruct((1, c), F32),
                   jax.ShapeDtypeStruct((1, c), F32)],
        compiler_params=_cparams(("arbitrary",)),
    )(cx, gamma, beta, dy)


FOX_COL = 8
SMALL_BLK = P_SMALL // LANES


def _log_sigmoid(x):
    return jnp.minimum(x, 0.0) - jnp.log(1.0 + jnp.exp(-jnp.abs(x)))


def _fox_cols(shape):
    col = lax.broadcasted_iota(jnp.int32, shape, 1)
    return (col >= FOX_COL) & (col < FOX_COL + GROUP_HEADS)


def fox_gate_fwd(proj, bvec, *, name):
    t = proj.shape[0]
    tm = _tile(t, 256, 8)

    def body(s_ref, b_ref, o_ref, carry):
        @pl.when(pl.program_id(0) == 0)
        def _():
            carry[...] = jnp.zeros_like(carry)

        lf = jnp.where(_fox_cols((tm, LANES)), _log_sigmoid(s_ref[...] + b_ref[...]), 0.0)
        ri = lax.broadcasted_iota(jnp.int32, (tm, tm), 0)
        ci = lax.broadcasted_iota(jnp.int32, (tm, tm), 1)
        cum = _dot_hi(jnp.where(ri >= ci, 1.0, 0.0).astype(F32), lf) + carry[...]
        o_ref[...] = cum
        carry[...] = cum[tm - 1:tm, :]

    return pl.pallas_call(
        body, name=name, grid=(t // tm,),
        in_specs=[pl.BlockSpec((tm, LANES), lambda i: (i, SMALL_BLK)), pl.BlockSpec((1, LANES), lambda i: (0, 0))],
        out_specs=pl.BlockSpec((tm, LANES), lambda i: (i, 0)),
        out_shape=jax.ShapeDtypeStruct((t, LANES), F32),
        scratch_shapes=[pltpu.VMEM((1, LANES), F32)],
        compiler_params=_cparams(("arbitrary",)),
    )(proj, bvec)


def fox_gate_bwd(dcum, proj, bvec, *, name):
    t = proj.shape[0]
    tm = _tile(t, 256, 8)
    n = t // tm

    def body(d_ref, s_ref, b_ref, o_ref, db_ref, carry):
        @pl.when(pl.program_id(0) == 0)
        def _():
            carry[...] = jnp.zeros_like(carry)
            db_ref[...] = jnp.zeros_like(db_ref)

        ri = lax.broadcasted_iota(jnp.int32, (tm, tm), 0)
        ci = lax.broadcasted_iota(jnp.int32, (tm, tm), 1)
        dlf = _dot_hi(jnp.where(ri <= ci, 1.0, 0.0).astype(F32), d_ref[...]) + carry[...]
        carry[...] = dlf[0:1, :]
        x = s_ref[...] + b_ref[...]
        dx = jnp.where(_fox_cols((tm, LANES)), dlf * (1.0 - _sigmoid(x)), 0.0)
        o_ref[...] = dx
        db_ref[...] += jnp.sum(dx, axis=0, keepdims=True)

    return pl.pallas_call(
        body, name=name, grid=(n,),
        in_specs=[pl.BlockSpec((tm, LANES), lambda i: (n - 1 - i, 0)),
                  pl.BlockSpec((tm, LANES), lambda i: (n - 1 - i, SMALL_BLK)),
                  pl.BlockSpec((1, LANES), lambda i: (0, 0))],
        out_specs=[pl.BlockSpec((tm, LANES), lambda i: (n - 1 - i, 0)), pl.BlockSpec((1, LANES), lambda i: (0, 0))],
        out_shape=[jax.ShapeDtypeStruct((t, LANES), F32), jax.ShapeDtypeStruct((1, LANES), F32)],
        scratch_shapes=[pltpu.VMEM((1, LANES), F32)],
        compiler_params=_cparams(("arbitrary",)),
    )(dcum, proj, bvec)


def _head_masks(c):
    lane_head = lax.broadcasted_iota(jnp.int32, (1, c), 1) // HEAD_DIM
    return [lane_head == h for h in range(GROUP_HEADS)]


def _attn_tiles(t, tq, tk):
    tq = _tile(t, tq, 8)
    tk = _tile(t, tk, LANES)
    return tq, tk, t // tq, t // tk


def _grid_ends(*sizes):
    first = lambda: functools.reduce(lambda a, b: a & b, [pl.program_id(d) == 0 for d in range(len(sizes))])
    last = lambda: functools.reduce(lambda a, b: a & b, [pl.program_id(d) == s - 1 for d, s in enumerate(sizes)])
    return first, last


EXP_DEAD = -110.0


def _key_norm_max(k_ref, nk, tk, masks):
    lane = lax.broadcasted_iota(jnp.int32, (1, LANES), 1)

    def one(jt, km):
        kb = k_ref[pl.ds(pl.multiple_of(jt * tk, tk), tk), :].astype(MXU_DT).astype(F32)
        sq = kb * kb
        for h in range(GROUP_HEADS):
            top = jnp.max(jnp.sum(jnp.where(masks[h], sq, 0.0), axis=-1, keepdims=True))
            km = jnp.where(lane == h, jnp.maximum(km, top), km)
        return km

    return lax.fori_loop(0, nk, one, jnp.zeros((1, LANES), F32))


def _fox_reach(qh, km, cc_ref):
    out = []
    for h in range(GROUP_HEADS):
        qf = qh[h].astype(F32)
        qn = jnp.sqrt(jnp.sum(qf * qf, axis=-1, keepdims=True))
        out.append(1.001 * qn * jnp.sqrt(km[:, h:h + 1]) + cc_ref[:, FOX_COL + h:FOX_COL + h + 1])
    return out


def _fox_alive(reach, top, cr_ref, j, tk):
    ends = cr_ref[jnp.maximum(j, 0)][:, tk - 1:tk]
    worst = jnp.float32(NEG_BIG)
    for h in range(GROUP_HEADS):
        worst = jnp.maximum(worst, jnp.max(reach[h] - top[h]) - jnp.max(ends[h:h + 1, :]))
    return (worst > EXP_DEAD).astype(jnp.int32)


def fox_fwd(proj, cum, cum_t, *, name, tq=512, tk=512, comm=None):
    t = proj.shape[0]
    c = GROUP_WIDTH
    tq, tk, nq, nk = _attn_tiles(t, tq, tk)
    assert tq == tk
    qb = P_FOX // c
    scale = HEAD_DIM ** -0.5
    cr3 = cum_t.reshape(8, nk, tk).transpose(1, 0, 2)

    def body(q_ref, k_ref, v_ref, cc_ref, cr_ref, o_ref, lse_ref, m_scr, l_scr, acc_scr, km_scr):
        i = pl.program_id(0)
        masks = _head_masks(c)

        @pl.when(i == 0)
        def _():
            km_scr[...] = _key_norm_max(k_ref, nk, tk, masks)

        q = q_ref[...] * scale
        qh = [jnp.where(masks[h], q, 0.0).astype(MXU_DT) for h in range(GROUP_HEADS)]
        cc = [cc_ref[:, FOX_COL + h:FOX_COL + h + 1] for h in range(GROUP_HEADS)]
        reach = _fox_reach(qh, km_scr[...], cc_ref)
        m_scr[...] = jnp.full_like(m_scr, NEG_BIG)
        l_scr[...] = jnp.zeros_like(l_scr)
        acc_scr[...] = jnp.zeros_like(acc_scr)

        def tile(j, diagonal):
            rows = pl.ds(pl.multiple_of(j * tk, tk), tk)
            kb = k_ref[rows, :].astype(MXU_DT)
            vb = v_ref[rows, :].astype(MXU_DT)
            crj = cr_ref[j]
            if diagonal:
                causal = (lax.broadcasted_iota(jnp.int32, (tq, tk), 1) <= lax.broadcasted_iota(jnp.int32, (tq, tk), 0))
            acc = acc_scr[...]
            for h in range(GROUP_HEADS):
                u = _dot_nt(qh[h], kb) - crj[h:h + 1, :]
                if diagonal:
                    u = jnp.where(causal, u, NEG_BIG)
                m_old = m_scr[h]
                m_new = jnp.maximum(m_old, jnp.max(u, axis=-1, keepdims=True) + cc[h])
                p = jnp.exp(u - (m_new - cc[h]))
                alpha = jnp.exp(m_old - m_new)
                l_scr[h] = alpha * l_scr[h] + jnp.sum(p, axis=-1, keepdims=True)
                m_scr[h] = m_new
                acc = jnp.where(masks[h], alpha * acc + _dot(p.astype(MXU_DT), vb), acc)
            acc_scr[...] = acc

        def alive(j):
            return _fox_alive(reach, [m_scr[h] for h in range(GROUP_HEADS)], cr_ref, j, tk)

        def step(state):
            j = i - state[0]
            tile(j, False)
            return state[0] + 1, alive(j - 1)

        tile(i, True)
        lax.while_loop(lambda s: (s[0] <= i) & (s[1] > 0), step, (jnp.int32(1), alive(i - 1)))
        acc = acc_scr[...]
        o = jnp.zeros_like(acc)
        lse = jnp.zeros((tq, LANES), F32)
        lane = lax.broadcasted_iota(jnp.int32, (1, LANES), 1)
        for h in range(GROUP_HEADS):
            o = jnp.where(masks[h], acc / l_scr[h], o)
            lse = jnp.where(lane == h, m_scr[h] + jnp.log(l_scr[h]), lse)
        o_ref[...] = o
        lse_ref[...] = lse

    resident = lambda blk: pl.BlockSpec((t, c), lambda i: (0, blk), pipeline_mode=pl.Buffered(1))
    call = dict(
        name=name, grid=(nq,),
        in_specs=[pl.BlockSpec((tq, c), lambda i: (i, qb)), resident(qb + 1), resident(qb + 2),
                  pl.BlockSpec((tq, LANES), lambda i: (i, 0)),
                  pl.BlockSpec((nk, 8, tk), lambda i: (0, 0, 0), pipeline_mode=pl.Buffered(1))],
        out_specs=[pl.BlockSpec((tq, c), lambda i: (i, 0)), pl.BlockSpec((tq, LANES), lambda i: (i, 0))],
        out_shape=[jax.ShapeDtypeStruct((t, c), F32), jax.ShapeDtypeStruct((t, LANES), F32)],
        scratch_shapes=[pltpu.VMEM((GROUP_HEADS, tq, 1), F32), pltpu.VMEM((GROUP_HEADS, tq, 1), F32),
                        pltpu.VMEM((tq, c), F32), pltpu.VMEM((1, LANES), F32)],
        compiler_params=_cparams(("arbitrary",)),
    )
    outs, got = carry_comm(call, body, (proj, proj, proj, cum, cr3), comm, 2, *_grid_ends(nq))
    return (*outs, got)


def fox_bwd(proj, cum, cum_t, o, lse, do, *, name, tq=512, tk=512, comm=None):
    t = proj.shape[0]
    c = GROUP_WIDTH
    tq, tk, nq, nk = _attn_tiles(t, tq, tk)
    assert tq == tk
    qb = P_FOX // c
    scale = HEAD_DIM ** -0.5
    cr3 = cum_t.reshape(8, nk, tk).transpose(1, 0, 2)

    def body(q_ref, k_ref, v_ref, cc_ref, cr_ref, o_ref, lse_ref, do_ref,
             dq_ref, dk_hbm, dv_hbm, dcc_ref, dcr_ref, dq_scr, rs_scr, dk_scr, dv_scr, km_scr):
        i = pl.program_id(0)
        masks = _head_masks(c)

        @pl.when(i == 0)
        def _():
            dk_scr[...] = jnp.zeros_like(dk_scr)
            dv_scr[...] = jnp.zeros_like(dv_scr)
            dcr_ref[...] = jnp.zeros_like(dcr_ref)
            km_scr[...] = _key_norm_max(k_ref, nk, tk, masks)

        q = q_ref[...]
        qf = q.astype(MXU_DT)
        qh = [jnp.where(masks[h], q * scale, 0.0).astype(MXU_DT) for h in range(GROUP_HEADS)]
        do = do_ref[...]
        dob = do.astype(MXU_DT)
        doh = [jnp.where(masks[h], do, 0.0).astype(MXU_DT) for h in range(GROUP_HEADS)]
        doo = do * o_ref[...]
        delta = [jnp.sum(jnp.where(masks[h], doo, 0.0), axis=-1, keepdims=True) for h in range(GROUP_HEADS)]
        lse_h = [lse_ref[:, h:h + 1] for h in range(GROUP_HEADS)]
        off = [lse_h[h] - cc_ref[:, FOX_COL + h:FOX_COL + h + 1] for h in range(GROUP_HEADS)]
        reach = _fox_reach(qh, km_scr[...], cc_ref)
        dq_scr[...] = jnp.zeros_like(dq_scr)
        rs_scr[...] = jnp.zeros_like(rs_scr)

        def tile(j, diagonal):
            rows = pl.ds(pl.multiple_of(j * tk, tk), tk)
            kb = k_ref[rows, :].astype(MXU_DT)
            vb = v_ref[rows, :].astype(MXU_DT)
            crj = cr_ref[j]
            if diagonal:
                causal = (lax.broadcasted_iota(jnp.int32, (tq, tk), 1) <= lax.broadcasted_iota(jnp.int32, (tq, tk), 0))
            dq = dq_scr[...]
            dk_upd = jnp.zeros((tk, c), F32)
            dv_upd = jnp.zeros((tk, c), F32)
            for h in range(GROUP_HEADS):
                p = jnp.exp(_dot_nt(qh[h], kb) - crj[h:h + 1, :] - off[h])
                if diagonal:
                    p = jnp.where(causal, p, 0.0)
                ds = p * (_dot_nt(doh[h], vb) - delta[h])
                dsb = ds.astype(MXU_DT)
                dq = jnp.where(masks[h], dq + _dot(dsb, kb) * scale, dq)
                dk_upd = jnp.where(masks[h], _dot_tn(dsb, qf) * scale, dk_upd)
                dv_upd = jnp.where(masks[h], _dot_tn(p.astype(MXU_DT), dob), dv_upd)
                dcr_ref[j, h:h + 1, :] += -jnp.sum(ds, axis=0, keepdims=True)
                rs_scr[h] += jnp.sum(ds, axis=-1, keepdims=True)
            dq_scr[...] = dq
            dk_scr[rows, :] += dk_upd
            dv_scr[rows, :] += dv_upd

        def alive(j):
            return _fox_alive(reach, lse_h, cr_ref, j, tk)

        def step(state):
            j = i - state[0]
            tile(j, False)
            return state[0] + 1, alive(j - 1)

        tile(i, True)
        lax.while_loop(lambda s: (s[0] <= i) & (s[1] > 0), step, (jnp.int32(1), alive(i - 1)))
        dq_ref[...] = dq_scr[...]
        lane = lax.broadcasted_iota(jnp.int32, (1, LANES), 1)
        dcc = jnp.zeros((tq, LANES), F32)
        for h in range(GROUP_HEADS):
            dcc = jnp.where(lane == FOX_COL + h, rs_scr[h], dcc)
        dcc_ref[...] = dcc

        @pl.when(i == nq - 1)
        def _():
            pltpu.sync_copy(dk_scr, dk_hbm)
            pltpu.sync_copy(dv_scr, dv_hbm)

    qrow = lambda i: (i, 0)
    resident = lambda blk: pl.BlockSpec((t, c), lambda i: (0, blk), pipeline_mode=pl.Buffered(1))
    hbm = pl.BlockSpec(memory_space=pl.ANY)
    call = dict(
        name=name, grid=(nq,),
        in_specs=[pl.BlockSpec((tq, c), lambda i: (i, qb)), resident(qb + 1), resident(qb + 2),
                  pl.BlockSpec((tq, LANES), qrow),
                  pl.BlockSpec((nk, 8, tk), lambda i: (0, 0, 0), pipeline_mode=pl.Buffered(1)),
                  pl.BlockSpec((tq, c), qrow), pl.BlockSpec((tq, LANES), qrow), pl.BlockSpec((tq, c), qrow)],
        out_specs=[pl.BlockSpec((tq, c), qrow), hbm, hbm, pl.BlockSpec((tq, LANES), qrow),
                   pl.BlockSpec((nk, 8, tk), lambda i: (0, 0, 0))],
        out_shape=[jax.ShapeDtypeStruct((t, c), F32), jax.ShapeDtypeStruct((t, c), F32),
                   jax.ShapeDtypeStruct((t, c), F32), jax.ShapeDtypeStruct((t, LANES), F32),
                   jax.ShapeDtypeStruct((nk, 8, tk), F32)],
        scratch_shapes=[pltpu.VMEM((tq, c), F32), pltpu.VMEM((GROUP_HEADS, tq, 1), F32),
                        pltpu.VMEM((t, c), F32), pltpu.VMEM((t, c), F32), pltpu.VMEM((1, LANES), F32)],
        compiler_params=_cparams(("arbitrary",)),
    )
    outs, got = carry_comm(call, body, (proj, proj, proj, cum, cr3, o, lse, do), comm, 5, *_grid_ends(nq))
    return (*outs, got)


SB_DEAD = -110.0


def _sb_logs(z, strict):
    tt = jnp.log(1.0 + jnp.exp(-jnp.abs(z)))
    log_keep = jnp.where(strict, -(jnp.maximum(z, 0.0) + tt), 0.0)
    log_beta = jnp.minimum(z, 0.0) - tt
    return log_keep, log_beta


def _tri(n, upper):
    a = lax.broadcasted_iota(jnp.int32, (n, n), 0)
    b = lax.broadcasted_iota(jnp.int32, (n, n), 1)
    return jnp.where((a < b) if upper else (a > b), 1.0, 0.0).astype(MXU_DT)


def _sb_carry_lane(jj, h):
    return GROUP_HEADS * jj + h


def sb_fwd(proj, *, name, tq=512, tk=256, comm=None):
    t = proj.shape[0]
    c = GROUP_WIDTH
    tq, tk, nq, nk = _attn_tiles(t, tq, tk)
    assert nk * GROUP_HEADS <= LANES
    qb = P_SB // c
    scale = HEAD_DIM ** -0.5

    def body(q_ref, k_ref, v_ref, o_ref, rs_ref, r_scr, acc_scr):
        i = pl.program_id(0)
        last = ((i + 1) * tq - 1) // tk
        masks = _head_masks(c)
        q = q_ref[...]
        qh = [jnp.where(masks[h], q, 0.0).astype(MXU_DT) for h in range(GROUP_HEADS)]
        lane = lax.broadcasted_iota(jnp.int32, (1, LANES), 1)
        later = _tri(tk, upper=False)
        r_scr[...] = jnp.zeros_like(r_scr)
        acc_scr[...] = jnp.zeros_like(acc_scr)
        rs_ref[...] = jnp.full((tq, LANES), 2.0 * SB_DEAD, F32)

        def step(state):
            jj, _ = state
            j = last - jj
            rows = pl.ds(pl.multiple_of(j * tk, tk), tk)
            kb = k_ref[rows, :].astype(MXU_DT)
            vb = v_ref[rows, :].astype(MXU_DT)
            row = i * tq + lax.broadcasted_iota(jnp.int32, (tq, tk), 0)
            col = j * tk + lax.broadcasted_iota(jnp.int32, (tq, tk), 1)
            strict = col < row
            acc = acc_scr[...]
            rs = rs_ref[...]
            for h in range(GROUP_HEADS):
                z = _dot_nt(qh[h], kb) * scale
                log_keep, log_beta = _sb_logs(z, strict)
                r_old = r_scr[h]
                rs = jnp.where(lane == _sb_carry_lane(jj, h), r_old, rs)
                rest = r_old + _split_dot(log_keep, later)
                w = jnp.where(strict, jnp.exp(log_beta + rest), 0.0)
                acc = jnp.where(masks[h], acc + _dot(w.astype(MXU_DT), vb), acc)
                r_scr[h] = r_old + jnp.sum(log_keep, axis=-1, keepdims=True)
            acc_scr[...] = acc
            rs_ref[...] = rs
            return jj + 1, jnp.max(r_scr[...])

        lax.while_loop(lambda s: (s[0] <= last) & (s[1] > SB_DEAD), step, (jnp.int32(0), jnp.float32(0.0)))
        o_ref[...] = acc_scr[...]

    resident = lambda blk: pl.BlockSpec((t, c), lambda i: (0, blk), pipeline_mode=pl.Buffered(1))
    call = dict(
        name=name, grid=(nq,),
        in_specs=[pl.BlockSpec((tq, c), lambda i: (i, qb)), resident(qb + 1), resident(qb + 2)],
        out_specs=[pl.BlockSpec((tq, c), lambda i: (i, 0)), pl.BlockSpec((tq, LANES), lambda i: (i, 0))],
        out_shape=[jax.ShapeDtypeStruct((t, c), F32), jax.ShapeDtypeStruct((t, LANES), F32)],
        scratch_shapes=[pltpu.VMEM((GROUP_HEADS, tq, 1), F32), pltpu.VMEM((tq, c), F32)],
        compiler_params=_cparams(("arbitrary",)),
    )
    outs, got = carry_comm(call, body, (proj, proj, proj), comm, 2, *_grid_ends(nq))
    return (*outs, got)


def sb_bwd(proj, rsave, do, *, name, tq=512, tk=256, comm=None):
    t = proj.shape[0]
    c = GROUP_WIDTH
    tq, tk, nq, nk = _attn_tiles(t, tq, tk)
    qb = P_SB // c
    scale = HEAD_DIM ** -0.5

    def body(q_ref, k_ref, v_ref, rs_ref, do_ref, dq_ref, dk_hbm, dv_hbm, e_scr, dq_scr, dk_scr, dv_scr):
        i = pl.program_id(0)
        last = ((i + 1) * tq - 1) // tk
        masks = _head_masks(c)

        @pl.when(i == 0)
        def _():
            dk_scr[...] = jnp.zeros_like(dk_scr)
            dv_scr[...] = jnp.zeros_like(dv_scr)

        e_scr[...] = jnp.zeros_like(e_scr)
        dq_scr[...] = jnp.zeros_like(dq_scr)
        q = q_ref[...]
        qf = q.astype(MXU_DT)
        qh = [jnp.where(masks[h], q, 0.0).astype(MXU_DT) for h in range(GROUP_HEADS)]
        do = do_ref[...]
        dob = do.astype(MXU_DT)
        doh = [jnp.where(masks[h], do, 0.0).astype(MXU_DT) for h in range(GROUP_HEADS)]
        later = _tri(tk, upper=False)
        earlier = _tri(tk, upper=True)
        rs = rs_ref[...]
        lane = lax.broadcasted_iota(jnp.int32, (1, LANES), 1)
        visited = jnp.where(jnp.max(rs, axis=0, keepdims=True) > SB_DEAD, (lane // GROUP_HEADS + 1).astype(F32), 0.0)
        n_visited = jnp.minimum(jnp.max(visited).astype(jnp.int32), last + 1)

        def step(it, carry):
            jj = n_visited - 1 - it
            j = last - jj
            rows = pl.ds(pl.multiple_of(j * tk, tk), tk)
            kb = k_ref[rows, :].astype(MXU_DT)
            vb = v_ref[rows, :].astype(MXU_DT)
            row = i * tq + lax.broadcasted_iota(jnp.int32, (tq, tk), 0)
            col = j * tk + lax.broadcasted_iota(jnp.int32, (tq, tk), 1)
            strict = col < row
            dq = dq_scr[...]
            dk_upd = jnp.zeros((tk, c), F32)
            dv_upd = jnp.zeros((tk, c), F32)
            for h in range(GROUP_HEADS):
                z = _dot_nt(qh[h], kb) * scale
                log_keep, log_beta = _sb_logs(z, strict)
                r_h = jnp.sum(jnp.where(lane == _sb_carry_lane(jj, h), rs, 0.0), axis=-1, keepdims=True)
                rest = r_h + _split_dot(log_keep, later)
                w = jnp.where(strict, jnp.exp(log_beta + rest), 0.0)
                e = w * _dot_nt(doh[h], vb)
                e_old = e_scr[h]
                dkeep = e_old + _split_dot(e, earlier)
                dz = jnp.where(strict, e * jnp.exp(log_keep) - dkeep * jnp.exp(log_beta), 0.0)
                dzb = dz.astype(MXU_DT)
                dq = jnp.where(masks[h], dq + _dot(dzb, kb) * scale, dq)
                dk_upd = jnp.where(masks[h], _dot_tn(dzb, qf) * scale, dk_upd)
                dv_upd = jnp.where(masks[h], _dot_tn(w.astype(MXU_DT), dob), dv_upd)
                e_scr[h] = e_old + jnp.sum(e, axis=-1, keepdims=True)
            dq_scr[...] = dq
            dk_scr[rows, :] += dk_upd
            dv_scr[rows, :] += dv_upd
            return carry

        lax.fori_loop(0, n_visited, step, 0)
        dq_ref[...] = dq_scr[...]

        @pl.when(i == nq - 1)
        def _():
            pltpu.sync_copy(dk_scr, dk_hbm)
            pltpu.sync_copy(dv_scr, dv_hbm)

    qrow = lambda i: (i, 0)
    resident = lambda blk: pl.BlockSpec((t, c), lambda i: (0, blk), pipeline_mode=pl.Buffered(1))
    hbm = pl.BlockSpec(memory_space=pl.ANY)
    call = dict(
        name=name, grid=(nq,),
        in_specs=[pl.BlockSpec((tq, c), lambda i: (i, qb)), resident(qb + 1), resident(qb + 2),
                  pl.BlockSpec((tq, LANES), qrow), pl.BlockSpec((tq, c), qrow)],
        out_specs=[pl.BlockSpec((tq, c), qrow), hbm, hbm],
        out_shape=[jax.ShapeDtypeStruct((t, c), F32)] * 3,
        scratch_shapes=[pltpu.VMEM((GROUP_HEADS, tq, 1), F32), pltpu.VMEM((tq, c), F32),
                        pltpu.VMEM((t, c), F32), pltpu.VMEM((t, c), F32)],
        compiler_params=_cparams(("arbitrary",)),
    )
    outs, got = carry_comm(call, body, (proj, proj, proj, rsave, do), comm, 3, *_grid_ends(nq))
    return (*outs, got)


A_COL, B_COL = 0, 4
Z_BLK = P_Z // GROUP_WIDTH
GDN_CHUNKS_PER_STEP = 4


NN = (((1,), (0,)), ((), ()))
NT = (((1,), (1,)), ((), ()))
TN = (((0,), (0,)), ((), ()))


def _terms(x, n):
    out, rem = [], x
    for _ in range(n):
        t = rem.astype(MXU_DT)
        out.append(t)
        rem = rem - t.astype(F32)
    return out


def _dotp(a, b, dims, a_terms=2, b_terms=2):
    at, bt = _terms(a, a_terms), _terms(b, b_terms)
    out = None
    for i, x in enumerate(at):
        for j, y in enumerate(bt):
            if i + j < max(a_terms, b_terms):
                r = lax.dot_general(x, y, dims, preferred_element_type=F32)
                out = r if out is None else out + r
    return out


def _silu(x):
    return x * _sigmoid(x)


def _dsilu(x):
    s = _sigmoid(x)
    return s * (1.0 + x * (1.0 - s))


def _head_sum(x, masks):
    out = jnp.zeros_like(x)
    for m in masks:
        out = jnp.where(m, jnp.sum(jnp.where(m, x, 0.0), axis=-1, keepdims=True), out)
    return out


def _expand(cols, col0, masks):
    out = jnp.zeros((cols.shape[0], GROUP_WIDTH), F32)
    for h, m in enumerate(masks):
        out = jnp.where(m, cols[:, col0 + h:col0 + h + 1], out)
    return out


def _reduce(x, col0, masks):
    lane = lax.broadcasted_iota(jnp.int32, (1, LANES), 1)
    out = jnp.zeros((x.shape[0], LANES), F32)
    for h, m in enumerate(masks):
        out = jnp.where(lane == col0 + h, jnp.sum(jnp.where(m, x, 0.0), axis=-1, keepdims=True), out)
    return out


def _block_ones():
    ri = lax.broadcasted_iota(jnp.int32, (GROUP_WIDTH, GROUP_WIDTH), 0) // HEAD_DIM
    ci = lax.broadcasted_iota(jnp.int32, (GROUP_WIDTH, GROUP_WIDTH), 1) // HEAD_DIM
    return jnp.where(ri == ci, 1.0, 0.0).astype(F32)


def _blk(x, hs):
    return jnp.concatenate([x] * GROUP_HEADS, axis=0) * hs


def _unblk(m, hs):
    mm = m * hs
    c = GDN_CHUNK
    return mm[0:c] + mm[c:2 * c] + mm[2 * c:3 * c] + mm[3 * c:4 * c]


def _row_mask4():
    ri = lax.broadcasted_iota(jnp.int32, (GROUP_WIDTH, LANES), 0) // HEAD_DIM
    ci = lax.broadcasted_iota(jnp.int32, (GROUP_WIDTH, LANES), 1)
    return jnp.where(ri + A_COL == ci, 1.0, 0.0).astype(F32)


def _lockstep(gens):
    results = [None] * len(gens)
    live = list(range(len(gens)))
    while live:
        for i in list(live):
            try:
                next(gens[i])
            except StopIteration as stop:
                results[i] = stop.value
                live.remove(i)
    return results


def _gdn_chunk(xc, small, avec, dtvec, state, masks, hs):
    (f,) = _lockstep([_gdn_local(xc, small, avec, dtvec, masks, hs)])
    return _gdn_recur(f, state, hs)


def _gdn_local(xc, small, avec, dtvec, masks, hs):
    c = GDN_CHUNK
    w = GROUP_WIDTH
    b16 = lambda v: v.astype(MXU_DT)
    f = {}
    xq, xk, xv = xc[:, :w], xc[:, w:2 * w], xc[:, 2 * w:]
    qs, ks, v = _silu(xq), _silu(xk), _silu(xv)
    rq = lax.rsqrt(_head_sum(qs * qs, masks) + L2_EPS)
    rk = lax.rsqrt(_head_sum(ks * ks, masks) + L2_EPS)
    qn = qs * rq
    k = ks * rk
    q = qn * (HEAD_DIM ** -0.5)
    xg = small + dtvec
    sp = jnp.maximum(xg, 0.0) + jnp.log(1.0 + jnp.exp(-jnp.abs(xg)))
    g128 = -avec * sp
    beta128 = _sigmoid(small)
    ri = lax.broadcasted_iota(jnp.int32, (c, c), 0)
    ci = lax.broadcasted_iota(jnp.int32, (c, c), 1)
    tril = jnp.where(ri >= ci, 1.0, 0.0).astype(F32)
    gam128 = _dotp(tril, g128, NN, 1, 3)
    yield
    gam = _expand(gam128, A_COL, masks)
    bfull = _expand(beta128, B_COL, masks)
    mask4 = _row_mask4()
    ones = jnp.ones((c, LANES), F32)
    gam_row = _dotp(ones, jnp.concatenate([gam128] * GROUP_HEADS, axis=0) * mask4, NT, 1, 3)
    yield
    li = lax.broadcasted_iota(jnp.int32, (c, w), 0)
    lj = lax.broadcasted_iota(jnp.int32, (c, w), 1) % HEAD_DIM
    incl = li >= lj
    strict = li > lj
    dmat = jnp.exp(jnp.where(incl, gam - gam_row, NEG_BIG))
    egam = jnp.exp(gam)
    glast = gam[c - 1:c, :]
    ekd = jnp.exp(glast - gam)
    kb = k * bfull
    vb = v * bfull
    kbg = kb * egam
    qd = q * egam
    kd = k * ekd
    kblk = b16(_blk(k, hs))
    araw = _dot_nt(b16(kb), kblk)
    qk = _dot_nt(b16(q), kblk)
    yield
    a = jnp.where(strict, araw * dmat, 0.0)
    tm = jnp.where(li == lj, 1.0, 0.0) - a
    p = a
    for _ in range(5):
        p = _dotp(p, _blk(p, hs), NN)
        yield
        tm = tm + _dotp(tm, _blk(p, hs), NN)
        yield
    tm16 = b16(tm)
    u = _dot(tm16, b16(_blk(vb, hs)))
    wm = _dot(tm16, b16(_blk(kbg, hs)))
    aqk = jnp.where(incl, qk * dmat, 0.0)
    f.update(xq=xq, xk=xk, xv=xv, v=v, rq=rq, rk=rk, qn=qn, k=k, q=q, xg=xg, g128=g128, beta128=beta128,
             tril=tril, gam=gam, bfull=bfull, mask4=mask4, ones=ones, incl=incl, strict=strict, li=li,
             dmat=dmat, egam=egam, glast=glast, ekd=ekd, kb=kb, vb=vb, kbg=kbg, qd=qd, kd=kd, kblk=kblk,
             araw=araw, tm=tm, tm16=tm16, wm=wm, qk=qk, aqk=aqk, u=u)
    return f


def _gdn_recur(f, state, hs):
    b16 = lambda v: v.astype(MXU_DT)
    s16 = b16(state)
    vn = f["u"] - _dot(b16(f["wm"]), s16)
    o = _dot(b16(f["qd"]), s16) + _dot(b16(f["aqk"]), b16(_blk(vn, hs)))
    s_new = state * jnp.exp(f["glast"]) + hs * _dot_tn(b16(f["kd"]), b16(vn))
    f.update(s16=s16, vn=vn, o=o, s_new=s_new)
    return f


def _decay_rate(a_log):
    lane = lax.broadcasted_iota(jnp.int32, a_log.shape, 1)
    return jnp.where((lane >= A_COL) & (lane < A_COL + GROUP_HEADS), jnp.exp(a_log), 0.0)


def _gdn_post(o, z, ng, masks):
    r = lax.rsqrt(_head_sum(o * o, masks) * (1.0 / HEAD_DIM) + RMS_EPS)
    on = o * r
    return on, r, on * ng * _silu(z)


def gdn_fwd(cqkv, proj, avec, dtvec, ng, *, name, comm=None):
    t = cqkv.shape[0]
    c = GDN_CHUNK
    w = GROUP_WIDTH
    n = t // c

    def body(x_ref, z_ref, sm_ref, a_ref, dt_ref, ng_ref, y_ref, st_ref, s_scr):
        @pl.when(pl.program_id(0) == 0)
        def _():
            s_scr[...] = jnp.zeros_like(s_scr)

        masks = _head_masks(w)
        hs = _block_ones()
        avec_v = _decay_rate(a_ref[...])
        rows = [pl.ds(k * c, c) for k in range(sub)]
        fs = _lockstep([_gdn_local(x_ref[r, :], sm_ref[r, :], avec_v, dt_ref[...], masks, hs) for r in rows])
        state = s_scr[...]
        for k, r in enumerate(rows):
            st_ref[k] = state
            f = _gdn_recur(fs[k], state, hs)
            _, _, y = _gdn_post(f["o"], z_ref[r, :], ng_ref[...], masks)
            y_ref[r, :] = y
            state = f["s_new"]
        s_scr[...] = state

    sub = GDN_CHUNKS_PER_STEP if n % GDN_CHUNKS_PER_STEP == 0 else 1
    rows, steps = c * sub, n // sub
    vec = pl.BlockSpec((1, LANES), lambda i: (0, 0))
    call = dict(
        name=name, grid=(steps,),
        in_specs=[pl.BlockSpec((rows, 3 * w), lambda i: (i, 0)),
                  pl.BlockSpec((rows, w), lambda i: (i, Z_BLK)),
                  pl.BlockSpec((rows, LANES), lambda i: (i, SMALL_BLK)),
                  vec, vec, pl.BlockSpec((1, w), lambda i: (0, 0))],
        out_specs=[pl.BlockSpec((rows, w), lambda i: (i, 0)), pl.BlockSpec((sub, w, w), lambda i: (i, 0, 0))],
        out_shape=[jax.ShapeDtypeStruct((t, w), F32), jax.ShapeDtypeStruct((n, w, w), F32)],
        scratch_shapes=[pltpu.VMEM((w, w), F32)],
        compiler_params=_cparams(("arbitrary",)),
    )
    outs, got = carry_comm(call, body, (cqkv, proj, proj, avec, dtvec, ng), comm, 2, *_grid_ends(steps))
    return (*outs, got)


def gdn_bwd(cqkv, proj, avec, dtvec, ng, states, dy, *, name, comm=None):
    t = cqkv.shape[0]
    c = GDN_CHUNK
    w = GROUP_WIDTH
    n = t // c
    b16 = lambda v: v.astype(MXU_DT)

    def body(x_ref, z_ref, sm_ref, a_ref, dt_ref, ng_ref, st_ref, dy_ref,
             dx_ref, dz_ref, dsm_ref, dng_ref, dal_ref, ddt_ref, ds_scr):
        @pl.when(pl.program_id(0) == 0)
        def _():
            ds_scr[...] = jnp.zeros_like(ds_scr)
            dng_ref[...] = jnp.zeros_like(dng_ref)
            dal_ref[...] = jnp.zeros_like(dal_ref)
            ddt_ref[...] = jnp.zeros_like(ddt_ref)

        masks = _head_masks(w)
        hs = _block_ones()
        avec_v = _decay_rate(a_ref[...])
        rows = [pl.ds(k * c, c) for k in range(sub)]
        fs = _lockstep([_gdn_local(x_ref[r, :], sm_ref[r, :], avec_v, dt_ref[...], masks, hs) for r in rows])
        fs = [_gdn_recur(f, st_ref[k], hs) for k, f in enumerate(fs)]
        _lockstep([chunk(fs[k], st_ref[k], avec_v, masks, hs, z_ref.at[r, :], ng_ref, dy_ref.at[r, :], dx_ref.at[r, :],
                         dz_ref.at[r, :], dsm_ref.at[r, :], dng_ref, dal_ref, ddt_ref, ds_scr)
                   for k, r in reversed(list(enumerate(rows)))])

    def chunk(f, state, avec_v, masks, hs, z_ref, ng_ref, dy_ref, dx_ref, dz_ref, dsm_ref, dng_ref, dal_ref, ddt_ref,
              ds_scr):
        z = z_ref[...]
        ng_v = ng_ref[...]
        dy_v = dy_ref[...]
        on, r, _ = _gdn_post(f["o"], z, ng_v, masks)
        sz = _silu(z)
        dz_ref[...] = dy_v * on * ng_v * _dsilu(z)
        d_on = dy_v * ng_v * sz
        dng_ref[...] += jnp.sum(dy_v * on * sz, axis=0, keepdims=True)
        do = r * (d_on - on * _head_sum(d_on * on, masks) * (1.0 / HEAD_DIM))
        do16 = b16(do)
        dsn = ds_scr[...]
        dsn16 = b16(dsn)
        s16, vn, kd, qd, wm = f["s16"], f["vn"], f["kd"], f["qd"], f["wm"]
        k, q, kblk, tm, tm16 = f["k"], f["q"], f["kblk"], f["tm"], f["tm16"]
        dmat, egam, glast, gam = f["dmat"], f["egam"], f["glast"], f["gam"]
        incl, strict, li = f["incl"], f["strict"], f["li"]
        vn16 = b16(vn)
        dvn = _unblk(_dot_tn(b16(f["aqk"]), do16), hs) + _dot(b16(kd), dsn16)
        daqk = jnp.where(incl, _dot_nt(do16, b16(_blk(vn, hs))), 0.0)
        dqd = _dot_nt(do16, s16)
        dvn16 = b16(dvn)
        ds_scr[...] = hs * (_dot_tn(b16(qd), do16) - _dot_tn(b16(wm), dvn16)) + dsn * jnp.exp(glast)
        yield
        dkd = _dot_nt(vn16, dsn16)
        dglast = jnp.sum(dsn * state, axis=0, keepdims=True) * jnp.exp(glast)
        du16 = dvn16
        dw16 = b16(-_dot_nt(dvn16, s16))
        yield
        dqk16 = b16(daqk * dmat)
        ddm = daqk * f["qk"]
        dq = _dot(dqk16, kblk)
        dk = _unblk(_dot_tn(dqk16, b16(q)), hs)
        dtm = _dot_nt(du16, b16(_blk(f["vb"], hs))) + _dot_nt(dw16, b16(_blk(f["kbg"], hs)))
        dvb = _unblk(_dot_tn(tm16, du16), hs)
        dkbg = _unblk(_dot_tn(tm16, dw16), hs)
        yield
        xx = _unblk(_dotp(tm, dtm, TN), hs)
        yield
        da = jnp.where(strict, -_dotp(xx, _blk(tm, hs), NT), 0.0)
        yield
        daraw16 = b16(da * dmat)
        ddm = ddm + da * f["araw"]
        dkb = _dot(daraw16, kblk)
        dk = dk + _unblk(_dot_tn(daraw16, b16(f["kb"])), hs)
        yield
        tcol = ddm * dmat
        dgam = tcol
        dgam128_row = _dotp(-tcol, f["ones"], TN, 2, 1) * f["mask4"]
        yield
        dgam128_row = (dgam128_row[0:c] + dgam128_row[c:2 * c] + dgam128_row[2 * c:3 * c] + dgam128_row[3 * c:4 * c])
        dk = dk + dkd * f["ekd"]
        tt = dkd * kd
        dgam = dgam - tt
        dglast = dglast + jnp.sum(tt, axis=0, keepdims=True)
        dq = dq + dqd * egam
        dgam = dgam + dqd * qd
        dkb = dkb + dkbg * egam
        dgam = dgam + dkbg * f["kbg"]
        dk = dk + dkb * f["bfull"]
        dbf = dkb * k + dvb * f["v"]
        dv = dvb * f["bfull"]
        dgam = dgam + jnp.where(li == c - 1, dglast, 0.0)
        beta128 = f["beta128"]
        db128 = _reduce(dbf, B_COL, masks) * beta128 * (1.0 - beta128)
        dgam128 = _reduce(dgam, A_COL, masks) + dgam128_row
        dg128 = _dotp(f["tril"], dgam128, TN, 1, 2)
        yield
        dxg = dg128 * (-avec_v * _sigmoid(f["xg"]))
        lane = lax.broadcasted_iota(jnp.int32, (1, LANES), 1)
        dsm_ref[...] = jnp.where(lane < B_COL, dxg, db128)
        ddt_ref[...] += jnp.sum(dxg, axis=0, keepdims=True)
        dal_ref[...] += jnp.sum(dg128 * f["g128"], axis=0, keepdims=True)
        dqn = dq * (HEAD_DIM ** -0.5)
        dqs = f["rq"] * (dqn - f["qn"] * _head_sum(dqn * f["qn"], masks))
        dks = f["rk"] * (dk - k * _head_sum(dk * k, masks))
        dx_ref[:, :w] = dqs * _dsilu(f["xq"])
        dx_ref[:, w:2 * w] = dks * _dsilu(f["xk"])
        dx_ref[:, 2 * w:] = dv * _dsilu(f["xv"])

    sub = GDN_CHUNKS_PER_STEP if n % GDN_CHUNKS_PER_STEP == 0 else 1
    rows, steps = c * sub, n // sub
    vec = pl.BlockSpec((1, LANES), lambda i: (0, 0))
    rev = lambda blk: (lambda i: (steps - 1 - i, blk))
    call = dict(
        name=name, grid=(steps,),
        in_specs=[pl.BlockSpec((rows, 3 * w), rev(0)),
                  pl.BlockSpec((rows, w), rev(Z_BLK)),
                  pl.BlockSpec((rows, LANES), rev(SMALL_BLK)),
                  vec, vec, pl.BlockSpec((1, w), lambda i: (0, 0)),
                  pl.BlockSpec((sub, w, w), lambda i: (steps - 1 - i, 0, 0)),
                  pl.BlockSpec((rows, w), rev(0))],
        out_specs=[pl.BlockSpec((rows, 3 * w), rev(0)), pl.BlockSpec((rows, w), rev(0)),
                   pl.BlockSpec((rows, LANES), rev(0)),
                   pl.BlockSpec((1, w), lambda i: (0, 0)), vec, vec],
        out_shape=[jax.ShapeDtypeStruct((t, 3 * w), F32), jax.ShapeDtypeStruct((t, w), F32),
                   jax.ShapeDtypeStruct((t, LANES), F32), jax.ShapeDtypeStruct((1, w), F32),
                   jax.ShapeDtypeStruct((1, LANES), F32), jax.ShapeDtypeStruct((1, LANES), F32)],
        scratch_shapes=[pltpu.VMEM((w, w), F32)],
        compiler_params=_cparams(("arbitrary",)),
    )
    outs, got = carry_comm(call, body, (cqkv, proj, proj, avec, dtvec, ng, states, dy), comm, 6, *_grid_ends(steps))
    return (*outs, got)


def adamw(w, m, v, gslots, *, row0=0, name):
    r, c = w.shape
    s = gslots.shape[0]
    tr = _tile(r, 64, 8)
    assert row0 % tr == 0 and gslots.shape[2] == c
    rb = row0 // tr
    c1 = 1.0 - ADAM_B1 ** ADAM_STEP
    c2 = 1.0 - ADAM_B2 ** ADAM_STEP

    def body(w_ref, m_ref, v_ref, gs_ref, g_ref, d_ref, mo_ref, vo_ref):
        g = gs_ref[0].astype(F32)
        for k in range(1, s):
            g = g + gs_ref[k].astype(F32)
        m_new = ADAM_B1 * m_ref[...] + (1.0 - ADAM_B1) * g
        v_new = ADAM_B2 * v_ref[...] + (1.0 - ADAM_B2) * (g * g)
        m_hat = m_new / c1
        v_hat = v_new / c2
        g_ref[...] = g
        mo_ref[...] = m_new
        vo_ref[...] = v_new
        d_ref[...] = -ADAM_LR * (m_hat / (jnp.sqrt(v_hat) + ADAM_EPS) + ADAM_WD * w_ref[...])

    row = pl.BlockSpec((tr, c), lambda i: (i, 0))
    return pl.pallas_call(
        body, name=name, grid=(r // tr,),
        in_specs=[row, row, row, pl.BlockSpec((s, tr, c), lambda i: (0, rb + i, 0))],
        out_specs=[row] * 4,
        out_shape=[jax.ShapeDtypeStruct((r, c), F32)] * 4,
        compiler_params=_cparams(("parallel",)),
    )(w, m, v, gslots)


def slot_sum(slots, *, name):
    s, r, c = slots.shape

    def body(s_ref, o_ref):
        acc = s_ref[0]
        for k in range(1, s):
            acc = acc + s_ref[k]
        o_ref[...] = acc

    return pl.pallas_call(
        body, name=name, grid=(1,),
        in_specs=[pl.BlockSpec((s, r, c), lambda i: (0, 0, 0))],
        out_specs=pl.BlockSpec((r, c), lambda i: (0, 0)),
        out_shape=jax.ShapeDtypeStruct((r, c), F32),
        compiler_params=_cparams(("arbitrary",)),
    )(slots)


class Comm:
    def __init__(self, srcs, broadcast):
        self.srcs = list(srcs)
        self.broadcast = [broadcast] * len(self.srcs) if isinstance(broadcast, bool) else list(broadcast)
        self.n = len(self.srcs)
        self.out_shapes = [jax.ShapeDtypeStruct(((N_DEV,) + s.shape) if b else s.shape, s.dtype)
                           for s, b in zip(self.srcs, self.broadcast)]
        self.sems = [pltpu.SemaphoreType.DMA((self.n,))] * 3

    def _local(self, src_refs, out_refs, loc_sem, a, me):
        src = src_refs[a] if self.broadcast[a] else src_refs[a].at[me]
        return pltpu.make_async_copy(src, out_refs[a].at[me], loc_sem.at[a])

    def start(self, src_refs, out_refs, send_sem, recv_sem, loc_sem):
        x, y, c = lax.axis_index("x"), lax.axis_index("y"), lax.axis_index("c")
        me = 4 * x + 2 * y + c
        for a in range(self.n):
            self._local(src_refs, out_refs, loc_sem, a, me).start()
        for d in range(1, N_DEV):
            px, py, pc = x ^ ((d >> 2) & 1), y ^ ((d >> 1) & 1), c ^ (d & 1)
            peer = 4 * px + 2 * py + pc
            for a in range(self.n):
                src = src_refs[a] if self.broadcast[a] else src_refs[a].at[peer]
                pltpu.make_async_remote_copy(
                    src_ref=src, dst_ref=out_refs[a].at[me],
                    send_sem=send_sem.at[a], recv_sem=recv_sem.at[a],
                    device_id=(px, py, pc), device_id_type=pl.DeviceIdType.MESH).start()

    def wait(self, src_refs, out_refs, send_sem, recv_sem, loc_sem):
        x, y, c = lax.axis_index("x"), lax.axis_index("y"), lax.axis_index("c")
        me = 4 * x + 2 * y + c
        for a in range(self.n):
            seven = out_refs[a].at[pl.ds(0, N_DEV - 1)]
            pltpu.make_async_remote_copy(
                src_ref=seven, dst_ref=seven, send_sem=send_sem.at[a], recv_sem=recv_sem.at[a],
                device_id=(x, y, c), device_id_type=pl.DeviceIdType.MESH).wait()
            self._local(src_refs, out_refs, loc_sem, a, me).wait()


def exchange(srcs, *, broadcast, name):
    comm = Comm(srcs, broadcast)
    n = comm.n

    def body(*refs):
        src_refs, out_refs, sems = refs[:n], refs[n:2 * n], refs[2 * n:]
        comm.start(src_refs, out_refs, *sems)
        comm.wait(src_refs, out_refs, *sems)

    anyspec = pl.BlockSpec(memory_space=pl.ANY)
    return pl.pallas_call(
        body, name=name,
        in_specs=[anyspec] * n, out_specs=[anyspec] * n, out_shape=comm.out_shapes,
        scratch_shapes=comm.sems,
        compiler_params=pltpu.CompilerParams(has_side_effects=True),
    )(*srcs)


def carry_comm(call_kwargs, body, args, comm, n_out, is_first, is_last):
    if comm is None:
        return pl.pallas_call(body, **call_kwargs)(*args), []
    n_in, nc = len(args), comm.n
    n_scr = len(call_kwargs["scratch_shapes"])
    anyspec = pl.BlockSpec(memory_space=pl.ANY)

    def wrapped(*refs):
        ins, csrc = refs[:n_in], refs[n_in:n_in + nc]
        outs = refs[n_in + nc:n_in + nc + n_out]
        cout = refs[n_in + nc + n_out:n_in + 2 * nc + n_out]
        rest = refs[n_in + 2 * nc + n_out:]
        scr, sems = rest[:n_scr], rest[n_scr:]

        @pl.when(is_first())
        def _():
            comm.start(csrc, cout, *sems)

        body(*ins, *outs, *scr)

        @pl.when(is_last())
        def _():
            comm.wait(csrc, cout, *sems)

    kw = dict(call_kwargs)
    kw["in_specs"] = list(kw["in_specs"]) + [anyspec] * nc
    kw["out_specs"] = list(kw["out_specs"]) + [anyspec] * nc
    kw["out_shape"] = list(kw["out_shape"]) + comm.out_shapes
    kw["scratch_shapes"] = list(kw["scratch_shapes"]) + comm.sems
    cp = kw["compiler_params"]
    kw["compiler_params"] = pltpu.CompilerParams(dimension_semantics=cp.dimension_semantics,
                                                 vmem_limit_bytes=cp.vmem_limit_bytes, has_side_effects=True)
    res = pl.pallas_call(wrapped, **kw)(*args, *comm.srcs)
    return res[:n_out], res[n_out:]


def _pack(arrs):
    flat = []
    for a in arrs:
        f = a.reshape(-1).astype(F32)
        flat.append(jnp.pad(f, (0, (-f.shape[0]) % LANES)))
    buf = jnp.concatenate(flat)
    buf = jnp.pad(buf, (0, (-buf.shape[0]) % (8 * LANES)))
    return buf.reshape(-1, LANES)


def _unpack(buf, shapes):
    flat = buf.reshape(-1)
    out, off = [], 0
    for s in shapes:
        sz = int(np.prod(s))
        out.append(flat[off:off + sz].reshape(s))
        off += sz + (-sz) % LANES
    return out


def _win_to_aligned(w):
    o = np.cumsum((0,) + IN_SPLITS)
    seg = lambda i: w[..., o[i]:o[i + 1]]
    pad = jnp.zeros(w.shape[:-1] + (P_WIDTH - IN_WIDTH,), w.dtype)
    return jnp.concatenate([seg(0), seg(1), seg(4), seg(6), seg(7), seg(2), seg(3), seg(5), pad], axis=-1)


def _win_from_aligned(w):
    o = np.cumsum((0,) + IN_SPLITS)
    s = P_SMALL
    return jnp.concatenate([w[..., P_GDN:P_GDN + 768], w[..., P_Z:P_Z + 256], w[..., s:s + 4], w[..., s + 4:s + 8],
                            w[..., P_FOX:P_FOX + 768], w[..., s + 8:s + 12], w[..., P_CONF:P_CONF + 512],
                            w[..., P_SB:P_SB + 768]], axis=-1)


def _row128(vals, col0):
    return jnp.pad(vals.astype(F32)[None, :], ((0, 0), (col0, LANES - col0 - GROUP_HEADS)))


def _ffn_fwd(x, x16, w, n, tag, comm=None, on_comm=None):
    gu = mm(x16, w[f"gu{n}"], name=f"{tag}_gu", tm=1024, tn=512, tk=1024, out_dtype=MXU_DT, comm=comm)
    if comm is not None:
        gu, got = gu
        on_comm(got)
    h = act_fwd(gu, name=f"{tag}_act")
    y = mm(h, w[f"d{n}"], name=f"{tag}_down", tm=1024, tn=512, tk=D_FF)
    out, out16, xh, rs = ln_res_fwd(x, y, w[f"ln_ffn{n}_g"], w[f"ln_ffn{n}_b"], 0.5, name=f"{tag}_ln")
    return out, out16, (x16, gu, h, xh, rs)


def _ffn_bwd(dout, saved, w, n, tag, comm_dh=None, comm_dwgu=None, comm_dx=None):
    x, gu, h, xh, rs = saved
    wgu, wd = w[f"gu{n}"], w[f"d{n}"]
    got = [[], [], []]
    dz, dg, db = ln_res_bwd(dout, xh, rs, w[f"ln_ffn{n}_g"], name=f"{tag}_ln_bwd")
    dh = mm(dz, wd, mode="nt", alpha=0.5, name=f"{tag}_dh", tm=1024, tn=D_FF // 2, tk=1024, comm=comm_dh)
    if comm_dh is not None:
        dh, got[0] = dh
    dgu = act_bwd(gu, dh, name=f"{tag}_act_bwd")
    dwd = mm(h, dz, mode="tn", alpha=0.5, name=f"{tag}_dwd", tm=D_FF // 2, tn=1024, tk=512)
    c = comm_dwgu(dwd) if comm_dwgu is not None else None
    dwgu = mm(x, dgu, mode="tn", name=f"{tag}_dwgu", tm=1024, tn=D_FF // 2, tk=512, comm=c)
    if c is not None:
        dwgu, got[1] = dwgu
    c = comm_dx(dwgu) if comm_dx is not None else None
    dx = mm(dgu, wgu, mode="nt", add=dz, beta=DN_ALPHA, name=f"{tag}_dx", tm=1024, tn=1024, tk=D_FF // 2, comm=c)
    if c is not None:
        dx, got[2] = dx
    return dx, dwgu, dwd, dg, db, got


def _layer_fwd(x, x16, mem, w, tag, comm_ffn1=None, on_ffn1=None, comm_gdn=None, on_gdn=None, comm_fox=None,
               on_fox=None, comm_sb=None, on_sb=None):
    sv = {}
    x1, x1h, sv["ffn1"] = _ffn_fwd(x, x16, w, 1, f"{tag}_ffn1", comm=comm_ffn1, on_comm=on_ffn1)
    proj = mm(x1h, w["win"], name=f"{tag}_inproj", tm=1024, tn=640, tk=1024)
    cqkv = dwconv_fwd(proj, w["gdn_conv_w"], None, col0=P_GDN, name=f"{tag}_gdn_conv")
    ya, states, got = gdn_fwd(cqkv, proj, w["alog"], w["dtb"], w["ng"], name=f"{tag}_gdn", comm=comm_gdn)
    if on_gdn is not None:
        on_gdn(got)
    cum = fox_gate_fwd(proj, w["bf"], name=f"{tag}_fox_gate")
    cum_t = jnp.pad(cum[:, FOX_COL:FOX_COL + GROUP_HEADS].T, ((0, 8 - GROUP_HEADS), (0, 0)))
    yb, lse, got = fox_fwd(proj, cum, cum_t, name=f"{tag}_fox", comm=comm_fox)
    if on_fox is not None:
        on_fox(got)
    u = glu_fwd(proj, name=f"{tag}_glu")
    cc = dwconv_fwd(u, w["conf_dw_w"], w["conf_dw_b"], name=f"{tag}_conf_conv")
    yc = gn_silu_fwd(cc, w["conf_norm_g"], w["conf_norm_b"], name=f"{tag}_conf_norm")
    yd, rsave, got = sb_fwd(proj, name=f"{tag}_sb", comm=comm_sb)
    if on_sb is not None:
        on_sb(got)
    ycat = jnp.concatenate([ya, yb, yc, yd], axis=1).astype(MXU_DT)
    mix = mm(ycat, w["wout"], name=f"{tag}_outproj")
    x2, x2h, xh2, rs2 = ln_res_fwd(x1, mix, w["ln_mix_g"], w["ln_mix_b"], 1.0, name=f"{tag}_ln_mix")
    sv["mix"] = (x1h, proj, cqkv, states, cum, cum_t, yb, lse, u, cc, rsave, ycat, xh2, rs2)
    q = mm(x2h, w["wq"], name=f"{tag}_memq", out_dtype=MXU_DT)
    kv = mm(mem, w["wkv"], name=f"{tag}_memkv", tm=N_MEM, out_dtype=MXU_DT)
    att = memattn_fwd(q, kv, name=f"{tag}_memattn")
    mo = mm(att, w["wo"], name=f"{tag}_memo")
    x3, x3h, xh3, rs3 = ln_res_fwd(x2, mo, w["ln_mem_g"], w["ln_mem_b"], 1.0, name=f"{tag}_ln_mem")
    sv["mem"] = (x2h, q, kv, att, xh3, rs3)
    x4, x4h, sv["ffn2"] = _ffn_fwd(x3, x3h, w, 2, f"{tag}_ffn2")
    return x4, x4h, sv


def _layer_bwd(dx4, mem, sv, w, tag, plan, tail=None):
    t = dx4.shape[0]
    gr = {}
    dx3, gr["gu2"], gr["d2"], gr["ln_ffn2_g"], gr["ln_ffn2_b"], _ = _ffn_bwd(dx4, sv["ffn2"], w, 2, f"{tag}_ffn2")
    x2, q, kv, att, xh3, rs3 = sv["mem"]
    dz, gr["ln_mem_g"], gr["ln_mem_b"] = ln_res_bwd(dx3, xh3, rs3, w["ln_mem_g"], name=f"{tag}_ln_mem_bwd")
    datt = mm(dz, w["wo"], mode="nt", name=f"{tag}_datt", out_dtype=MXU_DT)
    gr["wo"] = mm(att, dz, mode="tn", name=f"{tag}_dwo", tk=512)
    dq, dkv = memattn_bwd(q, kv, datt, name=f"{tag}_memattn_bwd")
    gr["wq"] = mm(x2, dq, mode="tn", name=f"{tag}_dwq", tk=512)
    gr["wkv"] = mm(mem, dkv, mode="tn", name=f"{tag}_dwkv", tk=N_MEM)
    dx2 = mm(dq, w["wq"], mode="nt", add=dz, beta=DN_ALPHA, name=f"{tag}_dx2")
    x1, proj, cqkv, states, cum, cum_t, yb, lse, u, cc, rsave, ycat, xh2, rs2 = sv["mix"]
    dz, gr["ln_mix_g"], gr["ln_mix_b"] = ln_res_bwd(dx2, xh2, rs2, w["ln_mix_g"], name=f"{tag}_ln_mix_bwd")
    dycat = mm(dz, w["wout"], mode="nt", name=f"{tag}_dycat")
    gr["wout"] = mm(ycat, dz, mode="tn", name=f"{tag}_dwout", tk=512)
    comm_sb, comm_fox, comm_gdn = plan(gr)
    gw = GROUP_WIDTH
    dya, dyb, dyc, dyd = (dycat[:, i * gw:(i + 1) * gw] for i in range(4))
    dq_d, dk_d, dv_d, got_sb = sb_bwd(proj, rsave, dyd, name=f"{tag}_sb_bwd", comm=comm_sb)
    dcc, gr["conf_norm_g"], gr["conf_norm_b"] = gn_silu_bwd(cc, w["conf_norm_g"], w["conf_norm_b"], dyc,
                                                            name=f"{tag}_conf_norm_bwd")
    du, gr["conf_dw_w"], gr["conf_dw_b"] = dwconv_bwd(dcc, u, w["conf_dw_w"], name=f"{tag}_conf_conv_bwd")
    dglu = glu_bwd(proj, du, name=f"{tag}_glu_bwd")
    dq_b, dk_b, dv_b, dcc, dcr, got_fox = fox_bwd(proj, cum, cum_t, yb, lse, dyb, name=f"{tag}_fox_bwd", comm=comm_fox)
    dcum = dcc + jnp.pad(dcr[:, :GROUP_HEADS, :].transpose(0, 2, 1).reshape(t, GROUP_HEADS),
                   ((0, 0), (FOX_COL, LANES - FOX_COL - GROUP_HEADS)))
    dsm_f, dbf = fox_gate_bwd(dcum, proj, w["bf"], name=f"{tag}_fox_gate_bwd")
    gr["fox_b_f"] = dbf[0, FOX_COL:FOX_COL + GROUP_HEADS]
    dcq, dz_a, dsm_a, dng, dal, ddt, got_gdn = gdn_bwd(cqkv, proj, w["alog"], w["dtb"], w["ng"], states, dya,
                                                       name=f"{tag}_gdn_bwd", comm=comm_gdn)
    gr["gdn_norm_g"] = dng.reshape(GROUP_HEADS, HEAD_DIM).sum(0)
    gr["gdn_a_log"] = dal[0, A_COL:A_COL + GROUP_HEADS]
    gr["gdn_dt_bias"] = ddt[0, A_COL:A_COL + GROUP_HEADS]
    dgq, gr["gdn_conv_w"], _ = dwconv_bwd(dcq, proj, w["gdn_conv_w"], col0=P_GDN, name=f"{tag}_gdn_conv_bwd")
    dproj = jnp.concatenate([dgq, dz_a, dq_b, dk_b, dv_b, dglu, dq_d, dk_d, dv_d, dsm_a + dsm_f],
                            axis=1).astype(MXU_DT)
    gr["win"] = mm(x1, dproj, mode="tn", name=f"{tag}_dwin", tm=1024, tn=640, tk=512)
    dx1 = mm(dproj, w["win"], mode="nt", add=dz, beta=DN_ALPHA, name=f"{tag}_dx1", tm=1024, tn=1024, tk=640)
    tail = {} if tail is None else dict(tail, comm_dh=tail["comm_dh"](gr))
    dx0, gr["gu1"], gr["d1"], gr["ln_ffn1_g"], gr["ln_ffn1_b"], got_tail = _ffn_bwd(
        dx1, sv["ffn1"], w, 1, f"{tag}_ffn1", **tail)
    return dx0, gr, (got_sb, got_fox, got_gdn), got_tail


SMALL_REPLICATED = ("ln_ffn1_g", "ln_ffn1_b", "gdn_a_log", "gdn_dt_bias", "gdn_norm_g", "fox_b_f", "conf_dw_b",
                    "conf_norm_g", "conf_norm_b", "ln_mix_g", "ln_mix_b", "ln_mem_g", "ln_mem_b", "ln_ffn2_g",
                    "ln_ffn2_b")
SMALL_SHARDED = ("gdn_conv_w", "conf_dw_w")
BIG = ("ffn1_w_gate", "ffn1_w_up", "ffn1_w_down", "w_in", "w_out", "mem_w_q", "mem_w_kv", "mem_w_o",
       "ffn2_w_gate", "ffn2_w_up", "ffn2_w_down")
WEIGHT_ORDER = ("ffn1_w_gate", "ffn1_w_up", "ffn1_w_down", "ln_ffn1_g", "ln_ffn1_b", "w_in", "gdn_conv_w", "gdn_a_log",
                "gdn_dt_bias", "gdn_norm_g", "fox_b_f", "conf_dw_w", "conf_dw_b", "conf_norm_g", "conf_norm_b", "w_out",
                "ln_mix_g", "ln_mix_b", "mem_w_q", "mem_w_kv", "mem_w_o", "ln_mem_g", "ln_mem_b", "ffn2_w_gate",
                "ffn2_w_up", "ffn2_w_down", "ln_ffn2_g", "ln_ffn2_b")


def _step(x, mem, loss_target, wts, ms, vs):
    me = 4 * lax.axis_index("x") + 2 * lax.axis_index("y") + lax.axis_index("c")
    x = x[0]
    mem = mem[0]
    target = loss_target[0]
    rows_s = D_MODEL // N_DEV
    first, rest = ("gu1", "d1", "win"), ("sq", "kv", "gu2", "d2")

    def shards(l):
        c = lambda k: wts[k][l].astype(MXU_DT)
        return dict(gu1=jnp.stack([c("ffn1_w_gate"), c("ffn1_w_up")]), d1=c("ffn1_w_down"),
                    win=_win_to_aligned(wts["w_in"][l]).astype(MXU_DT),
                    sq=jnp.stack([c("w_out"), c("mem_w_q"), c("mem_w_o")]), kv=c("mem_w_kv"),
                    gu2=jnp.stack([c("ffn2_w_gate"), c("ffn2_w_up")]), d2=c("ffn2_w_down"))

    def to_compute_layout(w, keys, got):
        for k, g in zip(keys, got):
            if k in ("gu1", "gu2"):
                w[k] = g.transpose(2, 1, 0, 3).reshape(D_MODEL, 2 * D_FF)
            elif k in ("d1", "d2"):
                w[k] = g.reshape(D_FF, D_MODEL)
            elif k == "win":
                w[k] = g.reshape(D_MODEL, P_WIDTH)
            elif k == "sq":
                full = g.transpose(1, 0, 2, 3).reshape(3, D_MODEL, D_MODEL)
                w["wout"], w["wq"], w["wo"] = full[0], full[1], full[2]
            else:
                w["wkv"] = g.transpose(1, 0, 2).reshape(D_MODEL, 2 * D_MODEL)

    def chunks(gr, keys, dtype=MXU_DT):
        out = []
        for k in keys:
            if k in ("gu1", "gu2"):
                out.append(gr[k].reshape(D_MODEL, 2, N_DEV, -1).transpose(2, 1, 0, 3))
            elif k in ("d1", "d2"):
                out.append(gr[k].reshape(N_DEV, -1, D_MODEL))
            elif k == "win":
                out.append(gr[k].reshape(N_DEV, rows_s, P_WIDTH))
            elif k == "sq":
                out.append(jnp.stack([gr[n].reshape(N_DEV, rows_s, D_MODEL) for n in ("wout", "wq", "wo")], axis=1))
            else:
                out.append(gr["wkv"].reshape(D_MODEL, N_DEV, -1).transpose(1, 0, 2))
        return [a.astype(dtype) for a in out]

    sh = [shards(l) for l in range(DEPTH)]
    sm_sh = _pack([wts["gdn_conv_w"], wts["conf_dw_w"]])
    got = exchange([sh[0]["gu1"], sm_sh], broadcast=True, name="gather_first")
    conv_shapes = [wts["gdn_conv_w"].shape, wts["conf_dw_w"].shape]
    parts = [_unpack(got[-1][j], conv_shapes) for j in range(N_DEV)]
    gconv_full = jnp.concatenate([p[0] for p in parts], axis=-1)
    cconv_full = jnp.concatenate([p[1] for p in parts], axis=-1)

    def small_weights(l):
        w = dict(gdn_conv_w=gconv_full[l], conf_dw_w=cconv_full[l],
                 alog=_row128(wts["gdn_a_log"][l], A_COL), dtb=_row128(wts["gdn_dt_bias"][l], A_COL),
                 bf=_row128(wts["fox_b_f"][l], FOX_COL), ng=jnp.tile(wts["gdn_norm_g"][l], GROUP_HEADS)[None, :])
        for k in ("ln_ffn1_g", "ln_ffn1_b", "conf_dw_b", "conf_norm_g", "conf_norm_b", "ln_mix_g", "ln_mix_b",
                  "ln_mem_g", "ln_mem_b", "ln_ffn2_g", "ln_ffn2_b"):
            w[k] = wts[k][l][None, :]
        return w

    lw = [small_weights(l) for l in range(DEPTH)]
    to_compute_layout(lw[0], ("gu1",), got[:-1])

    def take(l, keys):
        return lambda g: to_compute_layout(lw[l], keys, g)

    def take_fox0(g):
        to_compute_layout(lw[0], rest[2:], g[:2])
        to_compute_layout(lw[1], first[:2], g[2:])

    h, h16, sv0 = _layer_fwd(
        x, x.astype(MXU_DT), mem, lw[0], "l0",
        comm_ffn1=Comm([sh[0][k] for k in first[1:]], True), on_ffn1=take(0, first[1:]),
        comm_gdn=Comm([sh[0][k] for k in rest[:2]], True), on_gdn=take(0, rest[:2]),
        comm_fox=Comm([sh[0][k] for k in rest[2:]] + [sh[1][k] for k in first[:2]], True), on_fox=take_fox0,
        comm_sb=Comm([sh[1]["win"]], True), on_sb=take(1, ("win",)))
    h, _, sv1 = _layer_fwd(
        h, h16, mem, lw[1], "l1",
        comm_gdn=Comm([sh[1][k] for k in rest[:2]], True), on_gdn=take(1, rest[:2]),
        comm_fox=Comm([sh[1][k] for k in rest[2:]], True), on_fox=take(1, rest[2:]))
    dh, lpart = loss_head(h, target, name="loss_head")

    recv = [{}, {}]
    e_ffn, e_mem = ("gu2", "d2"), ("sq", "kv")
    dh, g1, got, _ = _layer_bwd(dh, mem, sv1, lw[1], "l1",
                                lambda gr: (None, Comm(chunks(gr, e_ffn), False), Comm(chunks(gr, e_mem), False)))
    recv[1].update(zip(e_ffn, got[1]))
    recv[1].update(zip(e_mem, got[2]))
    tail = dict(comm_dh=lambda gr: Comm(chunks(gr, ("win",)), False),
                comm_dwgu=lambda dwd: Comm(chunks({"d1": dwd}, ("d1",)), False),
                comm_dx=lambda dwgu: Comm(chunks({"gu1": dwgu}, ("gu1",)), False))
    dh, g0, got, got_t = _layer_bwd(
        dh, mem, sv0, lw[0], "l0",
        lambda gr: (Comm(chunks(gr, e_mem), False), Comm(chunks(g1, first[:2]), False),
                    Comm(chunks(g1, first[2:]) + chunks(gr, e_ffn), False)), tail)
    recv[0].update(zip(e_mem, got[0]))
    recv[1].update(zip(first[:2], got[1]))
    recv[1].update(win=got[2][0])
    recv[0].update(zip(e_ffn, got[2][1:]))
    recv[0].update(win=got_t[0][0], d1=got_t[1][0], gu1=got_t[2][0])
    grad_x = dh[None]
    grads = [g0, g1]

    def gl(k):
        return jnp.stack([grads[l][k] for l in range(DEPTH)])

    small_names = SMALL_REPLICATED + SMALL_SHARDED
    small_grads = [gl(k) for k in small_names] + [lpart[0, :1]]
    got = exchange([_pack(small_grads)], broadcast=True, name="gather_small_grads")
    sm_sum = slot_sum(got[0], name="sum_small_grads")
    sm_g = _unpack(sm_sum, [g.shape for g in small_grads])
    loss = sm_g[-1][0]
    small_g = dict(zip(small_names, sm_g[:-1]))
    for k in SMALL_SHARDED:
        width = wts[k].shape[-1]
        small_g[k] = lax.dynamic_slice_in_dim(small_g[k], me * width, width, axis=2)

    out_g, out_d, out_m, out_v = {}, {}, {}, {}

    def update(names, key, fix=lambda a: a):
        res = {k: [] for k in names}
        for l in range(DEPTH):
            slots = fix(recv[l][key])
            slots = slots.reshape(N_DEV, -1, slots.shape[-1])
            for i, k in enumerate(names):
                two = lambda a: a[l].reshape(-1, a.shape[-1])
                res[k].append(adamw(two(wts[k]), two(ms[k]), two(vs[k]), slots, row0=i * two(wts[k]).shape[0],
                                    name=f"adamw_{k}_l{l}"))
        for k in names:
            for dst, per_layer in zip((out_g, out_d, out_m, out_v), zip(*res[k])):
                dst[k] = jnp.stack(per_layer).reshape(wts[k].shape)

    update(("ffn1_w_gate", "ffn1_w_up"), "gu1")
    update(("ffn1_w_down",), "d1")
    update(("w_in",), "win", _win_from_aligned)
    update(("w_out", "mem_w_q", "mem_w_o"), "sq")
    update(("mem_w_kv",), "kv")
    update(("ffn2_w_gate", "ffn2_w_up"), "gu2")
    update(("ffn2_w_down",), "d2")

    sw = _pack([wts[k] for k in small_names])
    smm = _pack([ms[k] for k in small_names])
    smv = _pack([vs[k] for k in small_names])
    sg = _pack([small_g[k] for k in small_names])
    res = adamw(sw, smm, smv, sg[None], name="adamw_small")
    shapes = [wts[k].shape for k in small_names]
    for dst, buf in zip((out_g, out_d, out_m, out_v), res):
        for k, a in zip(small_names, _unpack(buf, shapes)):
            dst[k] = a

    return (loss, grad_x, *[out_g[k] for k in WEIGHT_ORDER], *[out_d[k] for k in WEIGHT_ORDER],
            *[out_m[k] for k in WEIGHT_ORDER], *[out_v[k] for k in WEIGHT_ORDER])


def kernel(x, mem, ffn1_w_gate, ffn1_w_up, ffn1_w_down, ln_ffn1_g, ln_ffn1_b, w_in, gdn_conv_w, gdn_a_log, gdn_dt_bias, gdn_norm_g, fox_b_f, conf_dw_w, conf_dw_b, conf_norm_g, conf_norm_b, w_out, ln_mix_g, ln_mix_b, mem_w_q, mem_w_kv, mem_w_o, ln_mem_g, ln_mem_b, ffn2_w_gate, ffn2_w_up, ffn2_w_down, ln_ffn2_g, ln_ffn2_b, loss_target, m_ffn1_w_gate, m_ffn1_w_up, m_ffn1_w_down, m_ln_ffn1_g, m_ln_ffn1_b, m_w_in, m_gdn_conv_w, m_gdn_a_log, m_gdn_dt_bias, m_gdn_norm_g, m_fox_b_f, m_conf_dw_w, m_conf_dw_b, m_conf_norm_g, m_conf_norm_b, m_w_out, m_ln_mix_g, m_ln_mix_b, m_mem_w_q, m_mem_w_kv, m_mem_w_o, m_ln_mem_g, m_ln_mem_b, m_ffn2_w_gate, m_ffn2_w_up, m_ffn2_w_down, m_ln_ffn2_g, m_ln_ffn2_b, v_ffn1_w_gate, v_ffn1_w_up, v_ffn1_w_down, v_ln_ffn1_g, v_ln_ffn1_b, v_w_in, v_gdn_conv_w, v_gdn_a_log, v_gdn_dt_bias, v_gdn_norm_g, v_fox_b_f, v_conf_dw_w, v_conf_dw_b, v_conf_norm_g, v_conf_norm_b, v_w_out, v_ln_mix_g, v_ln_mix_b, v_mem_w_q, v_mem_w_kv, v_mem_w_o, v_ln_mem_g, v_ln_mem_b, v_ffn2_w_gate, v_ffn2_w_up, v_ffn2_w_down, v_ln_ffn2_g, v_ln_ffn2_b):
    args = locals()
    wts = {k: args[k] for k in WEIGHT_ORDER}
    ms = {k: args["m_" + k] for k in WEIGHT_ORDER}
    vs = {k: args["v_" + k] for k in WEIGHT_ORDER}
    return _step(x, mem, loss_target, wts, ms, vs)
```

```python
import functools
import math

import jax
import jax.numpy as jnp
import numpy as np
from jax import lax
from jax.experimental import pallas as pl
from jax.experimental.pallas import tpu as pltpu

F32 = jnp.float32
BF16 = jnp.bfloat16
MXU_DT = jnp.bfloat16
HI = lax.Precision.HIGHEST

N_DEV = 8
VMEM_LIMIT_BYTES = 56 * 1024 * 1024
LANES = 128

D_MODEL = 1024
DEPTH = 2
GROUP_WIDTH = 256
HEAD_DIM = 64
GROUP_HEADS = 4
D_FF = 2816
SHORT_CONV = 4
CONF_KERNEL = 31
CONF_GROUPS = 4
GDN_CHUNK = 64
N_MEM = 256
MEM_HEADS = 4
MEM_HEAD_DIM = 256
DN_ALPHA = float((2 * DEPTH) ** 0.25)
LN_EPS = 1e-5
RMS_EPS = 1e-6
L2_EPS = 1e-6
NEG_BIG = -1e30
IN_SPLITS = (768, 256, 4, 4, 768, 4, 512, 768)
IN_WIDTH = sum(IN_SPLITS)
P_GDN, P_Z, P_FOX, P_CONF, P_SB, P_SMALL = 0, 768, 1024, 1792, 2304, 3072
P_WIDTH = 3200

ADAM_LR = 0.001
ADAM_B1 = 0.9
ADAM_B2 = 0.999
ADAM_EPS = 1e-08
ADAM_WD = 0.01
ADAM_STEP = 10


def _cparams(sem):
    return pltpu.CompilerParams(dimension_semantics=sem, vmem_limit_bytes=VMEM_LIMIT_BYTES)


def _tile(n, pref, align=LANES):
    if n <= pref:
        return n
    t = (pref // align) * align
    while t >= align:
        if n % t == 0:
            return t
        t -= align
    return n


def mm(a, b, *, mode="nn", add=None, alpha=1.0, beta=1.0, out_dtype=F32, name,
       tm=1024, tn=512, tk=1024, comm=None):
    if mode == "nn":
        (m, k), (k2, n) = a.shape, b.shape
    elif mode == "nt":
        (m, k), (n, k2) = a.shape, b.shape
    else:
        (k, m), (k2, n) = a.shape, b.shape
    assert k == k2, (a.shape, b.shape, mode)
    tm = _tile(m, tm, 8 if mode != "tn" else LANES)
    tn = _tile(n, tn)
    tk = _tile(k, tk, LANES if mode != "tn" else 8)
    nk = k // tk
    if mode == "nn":
        a_spec = pl.BlockSpec((tm, tk), lambda i, j, kk: (i, kk))
        b_spec = pl.BlockSpec((tk, tn), lambda i, j, kk: (kk, j))
        dims = (((1,), (0,)), ((), ()))
    elif mode == "nt":
        a_spec = pl.BlockSpec((tm, tk), lambda i, j, kk: (i, kk))
        b_spec = pl.BlockSpec((tn, tk), lambda i, j, kk: (j, kk))
        dims = (((1,), (1,)), ((), ()))
    else:
        a_spec = pl.BlockSpec((tk, tm), lambda i, j, kk: (kk, i))
        b_spec = pl.BlockSpec((tk, tn), lambda i, j, kk: (kk, j))
        dims = (((0,), (0,)), ((), ()))
    o_spec = pl.BlockSpec((tm, tn), lambda i, j, kk: (i, j))
    has_add = add is not None

    def body(*refs):
        if has_add:
            a_ref, b_ref, add_ref, o_ref, acc_ref = refs
        else:
            a_ref, b_ref, o_ref, acc_ref = refs
        kk = pl.program_id(2)

        @pl.when(kk == 0)
        def _():
            acc_ref[...] = jnp.zeros_like(acc_ref)

        acc_ref[...] += lax.dot_general(a_ref[...].astype(MXU_DT), b_ref[...].astype(MXU_DT), dims,
                                        preferred_element_type=F32)

        @pl.when(kk == nk - 1)
        def _():
            r = acc_ref[...]
            if alpha != 1.0:
                r = r * alpha
            if has_add:
                r = r + beta * add_ref[...].astype(F32)
            o_ref[...] = r.astype(out_dtype)

    in_specs = [a_spec, b_spec] + ([o_spec] if has_add else [])
    args = (a, b) + ((add,) if has_add else ())
    grid = (m // tm, n // tn, nk)
    call = dict(name=name, grid=grid, in_specs=in_specs, out_specs=[o_spec],
                out_shape=[jax.ShapeDtypeStruct((m, n), out_dtype)],
                scratch_shapes=[pltpu.VMEM((tm, tn), F32)],
                compiler_params=_cparams(("parallel", "parallel", "arbitrary")))
    (out,), got = carry_comm(call, body, args, comm, 1, *_grid_ends(*grid))
    return out if comm is None else (out, got)


def ln_res_fwd(x, y, g, b, s, *, name):
    t, d = x.shape
    tm = _tile(t, 512, 8)

    def body(x_ref, y_ref, g_ref, b_ref, o_ref, o16_ref, xh_ref, rs_ref):
        z = DN_ALPHA * x_ref[...] + s * y_ref[...]
        mu = jnp.mean(z, axis=-1, keepdims=True)
        zc = z - mu
        var = jnp.mean(zc * zc, axis=-1, keepdims=True)
        rstd = lax.rsqrt(var + LN_EPS)
        xh = zc * rstd
        xh_ref[...] = xh
        rs_ref[...] = jnp.broadcast_to(rstd, rs_ref.shape)
        out = xh * g_ref[...] + b_ref[...]
        o_ref[...] = out
        o16_ref[...] = out.astype(o16_ref.dtype)

    row = pl.BlockSpec((tm, d), lambda i: (i, 0))
    vec = pl.BlockSpec((1, d), lambda i: (0, 0))
    return pl.pallas_call(
        body, name=name, grid=(t // tm,),
        in_specs=[row, row, vec, vec],
        out_specs=[row, row, row, pl.BlockSpec((tm, LANES), lambda i: (i, 0))],
        out_shape=[jax.ShapeDtypeStruct((t, d), F32), jax.ShapeDtypeStruct((t, d), MXU_DT),
                   jax.ShapeDtypeStruct((t, d), F32), jax.ShapeDtypeStruct((t, LANES), F32)],
        compiler_params=_cparams(("parallel",)),
    )(x, y, g, b)


def ln_res_bwd(dout, xhat, rstd, g, *, name):
    t, d = dout.shape
    tm = _tile(t, 512, 8)

    def body(do_ref, xh_ref, rs_ref, g_ref, dz_ref, dg_ref, db_ref):
        i = pl.program_id(0)

        @pl.when(i == 0)
        def _():
            dg_ref[...] = jnp.zeros_like(dg_ref)
            db_ref[...] = jnp.zeros_like(db_ref)

        do = do_ref[...]
        xh = xh_ref[...]
        dxh = do * g_ref[...]
        m1 = jnp.mean(dxh, axis=-1, keepdims=True)
        m2 = jnp.mean(dxh * xh, axis=-1, keepdims=True)
        dz_ref[...] = rs_ref[:, 0:1] * (dxh - m1 - xh * m2)
        dg_ref[...] += jnp.sum(do * xh, axis=0, keepdims=True)
        db_ref[...] += jnp.sum(do, axis=0, keepdims=True)

    row = pl.BlockSpec((tm, d), lambda i: (i, 0))
    vec = pl.BlockSpec((1, d), lambda i: (0, 0))
    return pl.pallas_call(
        body, name=name, grid=(t // tm,),
        in_specs=[row, row, pl.BlockSpec((tm, LANES), lambda i: (i, 0)), vec],
        out_specs=[row, vec, vec],
        out_shape=[jax.ShapeDtypeStruct((t, d), F32), jax.ShapeDtypeStruct((1, d), F32),
                   jax.ShapeDtypeStruct((1, d), F32)],
        compiler_params=_cparams(("arbitrary",)),
    )(dout, xhat, rstd, g)


def _sigmoid(x):
    return 1.0 / (1.0 + jnp.exp(-x))


def act_fwd(gu, *, name):
    t, f2 = gu.shape
    f = f2 // 2
    tm = _tile(t, 256, 8)

    def body(gu_ref, h_ref):
        g = gu_ref[:, :f].astype(F32)
        h_ref[...] = (g * _sigmoid(g) * gu_ref[:, f:].astype(F32)).astype(h_ref.dtype)

    return pl.pallas_call(
        body, name=name, grid=(t // tm,),
        in_specs=[pl.BlockSpec((tm, f2), lambda i: (i, 0))],
        out_specs=pl.BlockSpec((tm, f), lambda i: (i, 0)),
        out_shape=jax.ShapeDtypeStruct((t, f), MXU_DT),
        compiler_params=_cparams(("parallel",)),
    )(gu)


def act_bwd(gu, dh, *, name):
    t, f2 = gu.shape
    f = f2 // 2
    tm = _tile(t, 256, 8)

    def body(gu_ref, dh_ref, o_ref):
        g = gu_ref[:, :f].astype(F32)
        u = gu_ref[:, f:].astype(F32)
        dh = dh_ref[...]
        sg = _sigmoid(g)
        o_ref[:, f:] = (dh * g * sg).astype(o_ref.dtype)
        o_ref[:, :f] = (dh * u * sg * (1.0 + g * (1.0 - sg))).astype(o_ref.dtype)

    return pl.pallas_call(
        body, name=name, grid=(t // tm,),
        in_specs=[pl.BlockSpec((tm, f2), lambda i: (i, 0)), pl.BlockSpec((tm, f), lambda i: (i, 0))],
        out_specs=pl.BlockSpec((tm, f2), lambda i: (i, 0)),
        out_shape=jax.ShapeDtypeStruct((t, f2), MXU_DT),
        compiler_params=_cparams(("parallel",)),
    )(gu, dh)


def loss_head(y, target, *, name):
    t, d = y.shape
    tm = _tile(t, 512, 8)

    def body(y_ref, t_ref, dy_ref, l_ref):
        i = pl.program_id(0)

        @pl.when(i == 0)
        def _():
            l_ref[...] = jnp.zeros_like(l_ref)

        err = y_ref[...] - t_ref[...]
        dy_ref[...] = err * (1.0 / d)
        part = jnp.sum(jnp.sum(err * err, axis=-1, keepdims=True), axis=0, keepdims=True)
        l_ref[...] += jnp.broadcast_to(part * (0.5 / d), l_ref.shape)

    row = pl.BlockSpec((tm, d), lambda i: (i, 0))
    return pl.pallas_call(
        body, name=name, grid=(t // tm,),
        in_specs=[row, row],
        out_specs=[row, pl.BlockSpec((1, LANES), lambda i: (0, 0))],
        out_shape=[jax.ShapeDtypeStruct((t, d), F32), jax.ShapeDtypeStruct((1, LANES), F32)],
        compiler_params=_cparams(("arbitrary",)),
    )(y, target)


def _dot(a, b):
    return lax.dot_general(a, b, (((1,), (0,)), ((), ())), preferred_element_type=F32)


def _dot_nt(a, b):
    return lax.dot_general(a, b, (((1,), (1,)), ((), ())), preferred_element_type=F32)


def _dot_tn(a, b):
    return lax.dot_general(a, b, (((0,), (0,)), ((), ())), preferred_element_type=F32)


def _dot_hi(a, b):
    return lax.dot_general(a, b, (((1,), (0,)), ((), ())), preferred_element_type=F32, precision=HI)


def _dot_nt_hi(a, b):
    return lax.dot_general(a, b, (((1,), (1,)), ((), ())), preferred_element_type=F32, precision=HI)


def _split_dot(x, u):
    hi = x.astype(MXU_DT)
    lo = (x - hi.astype(F32)).astype(MXU_DT)
    return _dot(hi, u) + _dot(lo, u)


def _mem_probs(q_ref, kv_ref, h):
    lo = h * MEM_HEAD_DIM
    qh = q_ref[:, lo:lo + MEM_HEAD_DIM].astype(MXU_DT)
    kh = kv_ref[:, lo:lo + MEM_HEAD_DIM].astype(MXU_DT)
    s = _dot_nt(qh, kh) * (MEM_HEAD_DIM ** -0.5)
    s = s - jnp.max(s, axis=-1, keepdims=True)
    p = jnp.exp(s)
    return p / jnp.sum(p, axis=-1, keepdims=True), qh, kh


def memattn_fwd(q, kv, *, name):
    t, d = q.shape
    tm = _tile(t, 512, 8)

    def body(q_ref, kv_ref, o_ref):
        for h in range(MEM_HEADS):
            lo = h * MEM_HEAD_DIM
            p, _, _ = _mem_probs(q_ref, kv_ref, h)
            vh = kv_ref[:, d + lo:d + lo + MEM_HEAD_DIM].astype(MXU_DT)
            o_ref[:, lo:lo + MEM_HEAD_DIM] = _dot(p.astype(MXU_DT), vh).astype(o_ref.dtype)

    return pl.pallas_call(
        body, name=name, grid=(t // tm,),
        in_specs=[pl.BlockSpec((tm, d), lambda i: (i, 0)), pl.BlockSpec(kv.shape, lambda i: (0, 0))],
        out_specs=pl.BlockSpec((tm, d), lambda i: (i, 0)),
        out_shape=jax.ShapeDtypeStruct((t, d), MXU_DT),
        compiler_params=_cparams(("parallel",)),
    )(q, kv)


def memattn_bwd(q, kv, datt, *, name):
    t, d = q.shape
    tm = _tile(t, 512, 8)
    scale = MEM_HEAD_DIM ** -0.5

    def body(q_ref, kv_ref, da_ref, dq_ref, dkv_ref):
        @pl.when(pl.program_id(0) == 0)
        def _():
            dkv_ref[...] = jnp.zeros_like(dkv_ref)

        for h in range(MEM_HEADS):
            lo = h * MEM_HEAD_DIM
            p, qh, kh = _mem_probs(q_ref, kv_ref, h)
            vh = kv_ref[:, d + lo:d + lo + MEM_HEAD_DIM].astype(MXU_DT)
            da = da_ref[:, lo:lo + MEM_HEAD_DIM].astype(MXU_DT)
            dp = _dot_nt(da, vh)
            ds = p * (dp - jnp.sum(dp * p, axis=-1, keepdims=True))
            dsb = ds.astype(MXU_DT)
            dq_ref[:, lo:lo + MEM_HEAD_DIM] = (_dot(dsb, kh) * scale).astype(dq_ref.dtype)
            dkv_ref[:, lo:lo + MEM_HEAD_DIM] += _dot_tn(dsb, qh) * scale
            dkv_ref[:, d + lo:d + lo + MEM_HEAD_DIM] += _dot_tn(p.astype(MXU_DT), da)

    row = pl.BlockSpec((tm, d), lambda i: (i, 0))
    full = pl.BlockSpec(kv.shape, lambda i: (0, 0))
    return pl.pallas_call(
        body, name=name, grid=(t // tm,),
        in_specs=[row, full, row],
        out_specs=[row, full],
        out_shape=[jax.ShapeDtypeStruct((t, d), MXU_DT), jax.ShapeDtypeStruct(kv.shape, F32)],
        compiler_params=_cparams(("arbitrary",)),
    )(q, kv, datt)


def _halo(k):
    return 8 * ((k - 1 + 7) // 8)


def dwconv_fwd(u, w, bias, *, col0=0, width=None, name):
    t = u.shape[0]
    kk, c = w.shape
    width = c if width is None else width
    assert width == c and col0 % c == 0
    cb = col0 // c
    hb = _halo(kk)
    tm = _tile(t, 512, hb)
    r = tm // hb
    has_bias = bias is not None

    def body(*refs):
        if has_bias:
            prev_ref, cur_ref, w_ref, b_ref, o_ref, scr = refs
        else:
            prev_ref, cur_ref, w_ref, o_ref, scr = refs
        i = pl.program_id(0)
        scr[0:hb, :] = jnp.where(i == 0, 0.0, prev_ref[...])
        scr[hb:hb + tm, :] = cur_ref[...]
        acc = jnp.zeros((tm, c), F32)
        for k in range(kk):
            acc = acc + w_ref[k:k + 1, :] * scr[pl.ds(hb - (kk - 1) + k, tm), :]
        if has_bias:
            acc = acc + b_ref[...]
        o_ref[...] = acc

    in_specs = [pl.BlockSpec((hb, c), lambda i: (jnp.maximum(i * r - 1, 0), cb)),
                pl.BlockSpec((tm, c), lambda i: (i, cb)),
                pl.BlockSpec((kk, c), lambda i: (0, 0))]
    args = [u, u, w]
    if has_bias:
        in_specs.append(pl.BlockSpec((1, c), lambda i: (0, 0)))
        args.append(bias)
    return pl.pallas_call(
        body, name=name, grid=(t // tm,),
        in_specs=in_specs,
        out_specs=pl.BlockSpec((tm, c), lambda i: (i, 0)),
        out_shape=jax.ShapeDtypeStruct((t, c), F32),
        scratch_shapes=[pltpu.VMEM((hb + tm, c), F32)],
        compiler_params=_cparams(("parallel",)),
    )(*args)


def dwconv_bwd(dc, u, w, *, col0=0, name):
    t, c = dc.shape
    kk = w.shape[0]
    assert col0 % c == 0
    cb = col0 // c
    hb = _halo(kk)
    tm = _tile(t, 512, hb)
    r = tm // hb
    n = t // tm

    def body(dcur_ref, dnext_ref, uprev_ref, ucur_ref, w_ref, du_ref, dw_ref, db_ref, sd, su):
        i = pl.program_id(0)

        @pl.when(i == 0)
        def _():
            dw_ref[...] = jnp.zeros_like(dw_ref)
            db_ref[...] = jnp.zeros_like(db_ref)

        dcur = dcur_ref[...]
        sd[0:tm, :] = dcur
        sd[tm:tm + hb, :] = jnp.where(i == n - 1, 0.0, dnext_ref[...])
        su[0:hb, :] = jnp.where(i == 0, 0.0, uprev_ref[...])
        su[hb:hb + tm, :] = ucur_ref[...]
        acc = jnp.zeros((tm, c), F32)
        for k in range(kk):
            acc = acc + w_ref[k:k + 1, :] * sd[pl.ds(kk - 1 - k, tm), :]
            dw_ref[k:k + 1, :] += jnp.sum(dcur * su[pl.ds(hb - (kk - 1) + k, tm), :], axis=0, keepdims=True)
        du_ref[...] = acc
        db_ref[...] += jnp.sum(dcur, axis=0, keepdims=True)

    return pl.pallas_call(
        body, name=name, grid=(n,),
        in_specs=[pl.BlockSpec((tm, c), lambda i: (i, 0)),
                  pl.BlockSpec((hb, c), lambda i: (jnp.minimum((i + 1) * r, n * r - 1), 0)),
                  pl.BlockSpec((hb, c), lambda i: (jnp.maximum(i * r - 1, 0), cb)),
                  pl.BlockSpec((tm, c), lambda i: (i, cb)),
                  pl.BlockSpec((kk, c), lambda i: (0, 0))],
        out_specs=[pl.BlockSpec((tm, c), lambda i: (i, 0)),
                   pl.BlockSpec((kk, c), lambda i: (0, 0)),
                   pl.BlockSpec((1, c), lambda i: (0, 0))],
        out_shape=[jax.ShapeDtypeStruct((t, c), F32), jax.ShapeDtypeStruct((kk, c), F32),
                   jax.ShapeDtypeStruct((1, c), F32)],
        scratch_shapes=[pltpu.VMEM((tm + hb, c), F32), pltpu.VMEM((hb + tm, c), F32)],
        compiler_params=_cparams(("arbitrary",)),
    )(dc, dc, u, u, w)


def glu_fwd(proj, *, name):
    t = proj.shape[0]
    c = GROUP_WIDTH
    tm = _tile(t, 1024, 8)
    vb, gb = P_CONF // c, P_CONF // c + 1

    def body(v_ref, g_ref, o_ref):
        o_ref[...] = v_ref[...] * _sigmoid(g_ref[...])

    return pl.pallas_call(
        body, name=name, grid=(t // tm,),
        in_specs=[pl.BlockSpec((tm, c), lambda i: (i, vb)), pl.BlockSpec((tm, c), lambda i: (i, gb))],
        out_specs=pl.BlockSpec((tm, c), lambda i: (i, 0)),
        out_shape=jax.ShapeDtypeStruct((t, c), F32),
        compiler_params=_cparams(("parallel",)),
    )(proj, proj)


def glu_bwd(proj, du, *, name):
    t = proj.shape[0]
    c = GROUP_WIDTH
    tm = _tile(t, 1024, 8)
    vb, gb = P_CONF // c, P_CONF // c + 1

    def body(v_ref, g_ref, du_ref, o_ref):
        sg = _sigmoid(g_ref[...])
        du = du_ref[...]
        o_ref[:, :c] = du * sg
        o_ref[:, c:] = du * v_ref[...] * sg * (1.0 - sg)

    return pl.pallas_call(
        body, name=name, grid=(t // tm,),
        in_specs=[pl.BlockSpec((tm, c), lambda i: (i, vb)), pl.BlockSpec((tm, c), lambda i: (i, gb)),
                  pl.BlockSpec((tm, c), lambda i: (i, 0))],
        out_specs=pl.BlockSpec((tm, 2 * c), lambda i: (i, 0)),
        out_shape=jax.ShapeDtypeStruct((t, 2 * c), F32),
        compiler_params=_cparams(("parallel",)),
    )(proj, proj, du)


def _group_mean_matrix(c, groups):
    gsz = c // groups
    ri = lax.broadcasted_iota(jnp.int32, (c, c), 0) // gsz
    ci = lax.broadcasted_iota(jnp.int32, (c, c), 1) // gsz
    return jnp.where(ri == ci, 1.0 / gsz, 0.0).astype(F32)


def gn_silu_fwd(cx, gamma, beta, *, name):
    t, c = cx.shape
    tm = _tile(t, 1024, 8)

    def body(c_ref, g_ref, b_ref, o_ref):
        gm = _group_mean_matrix(c, CONF_GROUPS)
        x = c_ref[...]
        mu = _dot_hi(x, gm)
        xc = x - mu
        var = _dot_hi(xc * xc, gm)
        a = xc * lax.rsqrt(var + LN_EPS) * g_ref[...] + b_ref[...]
        o_ref[...] = a * _sigmoid(a)

    row = pl.BlockSpec((tm, c), lambda i: (i, 0))
    vec = pl.BlockSpec((1, c), lambda i: (0, 0))
    return pl.pallas_call(
        body, name=name, grid=(t // tm,),
        in_specs=[row, vec, vec], out_specs=row,
        out_shape=jax.ShapeDtypeStruct((t, c), F32),
        compiler_params=_cparams(("parallel",)),
    )(cx, gamma, beta)


def gn_silu_bwd(cx, gamma, beta, dy, *, name):
    t, c = cx.shape
    tm = _tile(t, 1024, 8)

    def body(c_ref, g_ref, b_ref, dy_ref, dc_ref, dg_ref, db_ref):
        @pl.when(pl.program_id(0) == 0)
        def _():
            dg_ref[...] = jnp.zeros_like(dg_ref)
            db_ref[...] = jnp.zeros_like(db_ref)

        gm = _group_mean_matrix(c, CONF_GROUPS)
        x = c_ref[...]
        mu = _dot_hi(x, gm)
        xc = x - mu
        var = _dot_hi(xc * xc, gm)
        rstd = lax.rsqrt(var + LN_EPS)
        nrm = xc * rstd
        a = nrm * g_ref[...] + b_ref[...]
        sa = _sigmoid(a)
        da = dy_ref[...] * sa * (1.0 + a * (1.0 - sa))
        dg_ref[...] += jnp.sum(da * nrm, axis=0, keepdims=True)
        db_ref[...] += jnp.sum(da, axis=0, keepdims=True)
        dn = da * g_ref[...]
        dc_ref[...] = rstd * (dn - _dot_hi(dn, gm) - nrm * _dot_hi(dn * nrm, gm))

    row = pl.BlockSpec((tm, c), lambda i: (i, 0))
    vec = pl.BlockSpec((1, c), lambda i: (0, 0))
    return pl.pallas_call(
        body, name=name, grid=(t // tm,),
        in_specs=[row, vec, vec, row], out_specs=[row, vec, vec],
        out_shape=[jax.ShapeDtypeStruct((t, c), F32), jax.ShapeDtypeStruct((1, c), F32),
                   jax.ShapeDtypeStruct((1, c), F32)],
        compiler_params=_cparams(("arbitrary",)),
    )(cx, gamma, beta, dy)


FOX_COL = 8
SMALL_BLK = P_SMALL // LANES


def _log_sigmoid(x):
    return jnp.minimum(x, 0.0) - jnp.log(1.0 + jnp.exp(-jnp.abs(x)))


def _fox_cols(shape):
    col = lax.broadcasted_iota(jnp.int32, shape, 1)
    return (col >= FOX_COL) & (col < FOX_COL + GROUP_HEADS)


def fox_gate_fwd(proj, bvec, *, name):
    t = proj.shape[0]
    tm = _tile(t, 256, 8)

    def body(s_ref, b_ref, o_ref, carry):
        @pl.when(pl.program_id(0) == 0)
        def _():
            carry[...] = jnp.zeros_like(carry)

        lf = jnp.where(_fox_cols((tm, LANES)), _log_sigmoid(s_ref[...] + b_ref[...]), 0.0)
        ri = lax.broadcasted_iota(jnp.int32, (tm, tm), 0)
        ci = lax.broadcasted_iota(jnp.int32, (tm, tm), 1)
        cum = _dot_hi(jnp.where(ri >= ci, 1.0, 0.0).astype(F32), lf) + carry[...]
        o_ref[...] = cum
        carry[...] = cum[tm - 1:tm, :]

    return pl.pallas_call(
        body, name=name, grid=(t // tm,),
        in_specs=[pl.BlockSpec((tm, LANES), lambda i: (i, SMALL_BLK)), pl.BlockSpec((1, LANES), lambda i: (0, 0))],
        out_specs=pl.BlockSpec((tm, LANES), lambda i: (i, 0)),
        out_shape=jax.ShapeDtypeStruct((t, LANES), F32),
        scratch_shapes=[pltpu.VMEM((1, LANES), F32)],
        compiler_params=_cparams(("arbitrary",)),
    )(proj, bvec)


def fox_gate_bwd(dcum, proj, bvec, *, name):
    t = proj.shape[0]
    tm = _tile(t, 256, 8)
    n = t // tm

    def body(d_ref, s_ref, b_ref, o_ref, db_ref, carry):
        @pl.when(pl.program_id(0) == 0)
        def _():
            carry[...] = jnp.zeros_like(carry)
            db_ref[...] = jnp.zeros_like(db_ref)

        ri = lax.broadcasted_iota(jnp.int32, (tm, tm), 0)
        ci = lax.broadcasted_iota(jnp.int32, (tm, tm), 1)
        dlf = _dot_hi(jnp.where(ri <= ci, 1.0, 0.0).astype(F32), d_ref[...]) + carry[...]
        carry[...] = dlf[0:1, :]
        x = s_ref[...] + b_ref[...]
        dx = jnp.where(_fox_cols((tm, LANES)), dlf * (1.0 - _sigmoid(x)), 0.0)
        o_ref[...] = dx
        db_ref[...] += jnp.sum(dx, axis=0, keepdims=True)

    return pl.pallas_call(
        body, name=name, grid=(n,),
        in_specs=[pl.BlockSpec((tm, LANES), lambda i: (n - 1 - i, 0)),
                  pl.BlockSpec((tm, LANES), lambda i: (n - 1 - i, SMALL_BLK)),
                  pl.BlockSpec((1, LANES), lambda i: (0, 0))],
        out_specs=[pl.BlockSpec((tm, LANES), lambda i: (n - 1 - i, 0)), pl.BlockSpec((1, LANES), lambda i: (0, 0))],
        out_shape=[jax.ShapeDtypeStruct((t, LANES), F32), jax.ShapeDtypeStruct((1, LANES), F32)],
        scratch_shapes=[pltpu.VMEM((1, LANES), F32)],
        compiler_params=_cparams(("arbitrary",)),
    )(dcum, proj, bvec)


def _head_masks(c):
    lane_head = lax.broadcasted_iota(jnp.int32, (1, c), 1) // HEAD_DIM
    return [lane_head == h for h in range(GROUP_HEADS)]


def _attn_tiles(t, tq, tk):
    tq = _tile(t, tq, 8)
    tk = _tile(t, tk, LANES)
    return tq, tk, t // tq, t // tk


def _grid_ends(*sizes):
    first = lambda: functools.reduce(lambda a, b: a & b, [pl.program_id(d) == 0 for d in range(len(sizes))])
    last = lambda: functools.reduce(lambda a, b: a & b, [pl.program_id(d) == s - 1 for d, s in enumerate(sizes)])
    return first, last


EXP_DEAD = -110.0


def _key_norm_max(k_ref, nk, tk, masks):
    lane = lax.broadcasted_iota(jnp.int32, (1, LANES), 1)

    def one(jt, km):
        kb = k_ref[pl.ds(pl.multiple_of(jt * tk, tk), tk), :].astype(MXU_DT).astype(F32)
        sq = kb * kb
        for h in range(GROUP_HEADS):
            top = jnp.max(jnp.sum(jnp.where(masks[h], sq, 0.0), axis=-1, keepdims=True))
            km = jnp.where(lane == h, jnp.maximum(km, top), km)
        return km

    return lax.fori_loop(0, nk, one, jnp.zeros((1, LANES), F32))


def _fox_reach(qh, km, cc_ref):
    out = []
    for h in range(GROUP_HEADS):
        qf = qh[h].astype(F32)
        qn = jnp.sqrt(jnp.sum(qf * qf, axis=-1, keepdims=True))
        out.append(1.001 * qn * jnp.sqrt(km[:, h:h + 1]) + cc_ref[:, FOX_COL + h:FOX_COL + h + 1])
    return out


def _fox_alive(reach, top, cr_ref, j, tk):
    ends = cr_ref[jnp.maximum(j, 0)][:, tk - 1:tk]
    worst = jnp.float32(NEG_BIG)
    for h in range(GROUP_HEADS):
        worst = jnp.maximum(worst, jnp.max(reach[h] - top[h]) - jnp.max(ends[h:h + 1, :]))
    return (worst > EXP_DEAD).astype(jnp.int32)


def fox_fwd(proj, cum, cum_t, *, name, tq=512, tk=512, comm=None):
    t = proj.shape[0]
    c = GROUP_WIDTH
    tq, tk, nq, nk = _attn_tiles(t, tq, tk)
    assert tq == tk
    qb = P_FOX // c
    scale = HEAD_DIM ** -0.5
    cr3 = cum_t.reshape(8, nk, tk).transpose(1, 0, 2)

    def body(q_ref, k_ref, v_ref, cc_ref, cr_ref, o_ref, lse_ref, m_scr, l_scr, acc_scr, km_scr):
        i = pl.program_id(0)
        masks = _head_masks(c)

        @pl.when(i == 0)
        def _():
            km_scr[...] = _key_norm_max(k_ref, nk, tk, masks)

        q = q_ref[...] * scale
        qh = [jnp.where(masks[h], q, 0.0).astype(MXU_DT) for h in range(GROUP_HEADS)]
        cc = [cc_ref[:, FOX_COL + h:FOX_COL + h + 1] for h in range(GROUP_HEADS)]
        reach = _fox_reach(qh, km_scr[...], cc_ref)
        m_scr[...] = jnp.full_like(m_scr, NEG_BIG)
        l_scr[...] = jnp.zeros_like(l_scr)
        acc_scr[...] = jnp.zeros_like(acc_scr)

        def tile(j, diagonal):
            rows = pl.ds(pl.multiple_of(j * tk, tk), tk)
            kb = k_ref[rows, :].astype(MXU_DT)
            vb = v_ref[rows, :].astype(MXU_DT)
            crj = cr_ref[j]
            if diagonal:
                causal = (lax.broadcasted_iota(jnp.int32, (tq, tk), 1) <= lax.broadcasted_iota(jnp.int32, (tq, tk), 0))
            acc = acc_scr[...]
            for h in range(GROUP_HEADS):
                u = _dot_nt(qh[h], kb) - crj[h:h + 1, :]
                if diagonal:
                    u = jnp.where(causal, u, NEG_BIG)
                m_old = m_scr[h]
                m_new = jnp.maximum(m_old, jnp.max(u, axis=-1, keepdims=True) + cc[h])
                p = jnp.exp(u - (m_new - cc[h]))
                alpha = jnp.exp(m_old - m_new)
                l_scr[h] = alpha * l_scr[h] + jnp.sum(p, axis=-1, keepdims=True)
                m_scr[h] = m_new
                acc = jnp.where(masks[h], alpha * acc + _dot(p.astype(MXU_DT), vb), acc)
            acc_scr[...] = acc

        def alive(j):
            return _fox_alive(reach, [m_scr[h] for h in range(GROUP_HEADS)], cr_ref, j, tk)

        def step(state):
            j = i - state[0]
            tile(j, False)
            return state[0] + 1, alive(j - 1)

        tile(i, True)
        lax.while_loop(lambda s: (s[0] <= i) & (s[1] > 0), step, (jnp.int32(1), alive(i - 1)))
        acc = acc_scr[...]
        o = jnp.zeros_like(acc)
        lse = jnp.zeros((tq, LANES), F32)
        lane = lax.broadcasted_iota(jnp.int32, (1, LANES), 1)
        for h in range(GROUP_HEADS):
            o = jnp.where(masks[h], acc / l_scr[h], o)
            lse = jnp.where(lane == h, m_scr[h] + jnp.log(l_scr[h]), lse)
        o_ref[...] = o
        lse_ref[...] = lse

    resident = lambda blk: pl.BlockSpec((t, c), lambda i: (0, blk), pipeline_mode=pl.Buffered(1))
    call = dict(
        name=name, grid=(nq,),
        in_specs=[pl.BlockSpec((tq, c), lambda i: (i, qb)), resident(qb + 1), resident(qb + 2),
                  pl.BlockSpec((tq, LANES), lambda i: (i, 0)),
                  pl.BlockSpec((nk, 8, tk), lambda i: (0, 0, 0), pipeline_mode=pl.Buffered(1))],
        out_specs=[pl.BlockSpec((tq, c), lambda i: (i, 0)), pl.BlockSpec((tq, LANES), lambda i: (i, 0))],
        out_shape=[jax.ShapeDtypeStruct((t, c), F32), jax.ShapeDtypeStruct((t, LANES), F32)],
        scratch_shapes=[pltpu.VMEM((GROUP_HEADS, tq, 1), F32), pltpu.VMEM((GROUP_HEADS, tq, 1), F32),
                        pltpu.VMEM((tq, c), F32), pltpu.VMEM((1, LANES), F32)],
        compiler_params=_cparams(("arbitrary",)),
    )
    outs, got = carry_comm(call, body, (proj, proj, proj, cum, cr3), comm, 2, *_grid_ends(nq))
    return (*outs, got)


def fox_bwd(proj, cum, cum_t, o, lse, do, *, name, tq=512, tk=512, comm=None):
    t = proj.shape[0]
    c = GROUP_WIDTH
    tq, tk, nq, nk = _attn_tiles(t, tq, tk)
    assert tq == tk
    qb = P_FOX // c
    scale = HEAD_DIM ** -0.5
    cr3 = cum_t.reshape(8, nk, tk).transpose(1, 0, 2)

    def body(q_ref, k_ref, v_ref, cc_ref, cr_ref, o_ref, lse_ref, do_ref,
             dq_ref, dk_hbm, dv_hbm, dcc_ref, dcr_ref, dq_scr, rs_scr, dk_scr, dv_scr, km_scr):
        i = pl.program_id(0)
        masks = _head_masks(c)

        @pl.when(i == 0)
        def _():
            dk_scr[...] = jnp.zeros_like(dk_scr)
            dv_scr[...] = jnp.zeros_like(dv_scr)
            dcr_ref[...] = jnp.zeros_like(dcr_ref)
            km_scr[...] = _key_norm_max(k_ref, nk, tk, masks)

        q = q_ref[...]
        qf = q.astype(MXU_DT)
        qh = [jnp.where(masks[h], q * scale, 0.0).astype(MXU_DT) for h in range(GROUP_HEADS)]
        do = do_ref[...]
        dob = do.astype(MXU_DT)
        doh = [jnp.where(masks[h], do, 0.0).astype(MXU_DT) for h in range(GROUP_HEADS)]
        doo = do * o_ref[...]
        delta = [jnp.sum(jnp.where(masks[h], doo, 0.0), axis=-1, keepdims=True) for h in range(GROUP_HEADS)]
        lse_h = [lse_ref[:, h:h + 1] for h in range(GROUP_HEADS)]
        off = [lse_h[h] - cc_ref[:, FOX_COL + h:FOX_COL + h + 1] for h in range(GROUP_HEADS)]
        reach = _fox_reach(qh, km_scr[...], cc_ref)
        dq_scr[...] = jnp.zeros_like(dq_scr)
        rs_scr[...] = jnp.zeros_like(rs_scr)

        def tile(j, diagonal):
            rows = pl.ds(pl.multiple_of(j * tk, tk), tk)
            kb = k_ref[rows, :].astype(MXU_DT)
            vb = v_ref[rows, :].astype(MXU_DT)
            crj = cr_ref[j]
            if diagonal:
                causal = (lax.broadcasted_iota(jnp.int32, (tq, tk), 1) <= lax.broadcasted_iota(jnp.int32, (tq, tk), 0))
            dq = dq_scr[...]
            dk_upd = jnp.zeros((tk, c), F32)
            dv_upd = jnp.zeros((tk, c), F32)
            for h in range(GROUP_HEADS):
                p = jnp.exp(_dot_nt(qh[h], kb) - crj[h:h + 1, :] - off[h])
                if diagonal:
                    p = jnp.where(causal, p, 0.0)
                ds = p * (_dot_nt(doh[h], vb) - delta[h])
                dsb = ds.astype(MXU_DT)
                dq = jnp.where(masks[h], dq + _dot(dsb, kb) * scale, dq)
                dk_upd = jnp.where(masks[h], _dot_tn(dsb, qf) * scale, dk_upd)
                dv_upd = jnp.where(masks[h], _dot_tn(p.astype(MXU_DT), dob), dv_upd)
                dcr_ref[j, h:h + 1, :] += -jnp.sum(ds, axis=0, keepdims=True)
                rs_scr[h] += jnp.sum(ds, axis=-1, keepdims=True)
            dq_scr[...] = dq
            dk_scr[rows, :] += dk_upd
            dv_scr[rows, :] += dv_upd

        def alive(j):
            return _fox_alive(reach, lse_h, cr_ref, j, tk)

        def step(state):
            j = i - state[0]
            tile(j, False)
            return state[0] + 1, alive(j - 1)

        tile(i, True)
        lax.while_loop(lambda s: (s[0] <= i) & (s[1] > 0), step, (jnp.int32(1), alive(i - 1)))
        dq_ref[...] = dq_scr[...]
        lane = lax.broadcasted_iota(jnp.int32, (1, LANES), 1)
        dcc = jnp.zeros((tq, LANES), F32)
        for h in range(GROUP_HEADS):
            dcc = jnp.where(lane == FOX_COL + h, rs_scr[h], dcc)
        dcc_ref[...] = dcc

        @pl.when(i == nq - 1)
        def _():
            pltpu.sync_copy(dk_scr, dk_hbm)
            pltpu.sync_copy(dv_scr, dv_hbm)

    qrow = lambda i: (i, 0)
    resident = lambda blk: pl.BlockSpec((t, c), lambda i: (0, blk), pipeline_mode=pl.Buffered(1))
    hbm = pl.BlockSpec(memory_space=pl.ANY)
    call = dict(
        name=name, grid=(nq,),
        in_specs=[pl.BlockSpec((tq, c), lambda i: (i, qb)), resident(qb + 1), resident(qb + 2),
                  pl.BlockSpec((tq, LANES), qrow),
                  pl.BlockSpec((nk, 8, tk), lambda i: (0, 0, 0), pipeline_mode=pl.Buffered(1)),
                  pl.BlockSpec((tq, c), qrow), pl.BlockSpec((tq, LANES), qrow), pl.BlockSpec((tq, c), qrow)],
        out_specs=[pl.BlockSpec((tq, c), qrow), hbm, hbm, pl.BlockSpec((tq, LANES), qrow),
                   pl.BlockSpec((nk, 8, tk), lambda i: (0, 0, 0))],
        out_shape=[jax.ShapeDtypeStruct((t, c), F32), jax.ShapeDtypeStruct((t, c), F32),
                   jax.ShapeDtypeStruct((t, c), F32), jax.ShapeDtypeStruct((t, LANES), F32),
                   jax.ShapeDtypeStruct((nk, 8, tk), F32)],
        scratch_shapes=[pltpu.VMEM((tq, c), F32), pltpu.VMEM((GROUP_HEADS, tq, 1), F32),
                        pltpu.VMEM((t, c), F32), pltpu.VMEM((t, c), F32), pltpu.VMEM((1, LANES), F32)],
        compiler_params=_cparams(("arbitrary",)),
    )
    outs, got = carry_comm(call, body, (proj, proj, proj, cum, cr3, o, lse, do), comm, 5, *_grid_ends(nq))
    return (*outs, got)


SB_DEAD = -110.0


def _sb_logs(z, strict):
    tt = jnp.log(1.0 + jnp.exp(-jnp.abs(z)))
    log_keep = jnp.where(strict, -(jnp.maximum(z, 0.0) + tt), 0.0)
    log_beta = jnp.minimum(z, 0.0) - tt
    return log_keep, log_beta


def _tri(n, upper):
    a = lax.broadcasted_iota(jnp.int32, (n, n), 0)
    b = lax.broadcasted_iota(jnp.int32, (n, n), 1)
    return jnp.where((a < b) if upper else (a > b), 1.0, 0.0).astype(MXU_DT)


def _sb_carry_lane(jj, h):
    return GROUP_HEADS * jj + h


def sb_fwd(proj, *, name, tq=1024, tk=256, comm=None):
    t = proj.shape[0]
    c = GROUP_WIDTH
    tq, tk, nq, nk = _attn_tiles(t, tq, tk)
    assert nk * GROUP_HEADS <= LANES
    qb = P_SB // c
    scale = HEAD_DIM ** -0.5

    def body(q_ref, k_ref, v_ref, o_ref, rs_ref, r_scr, acc_scr):
        i = pl.program_id(0)
        last = ((i + 1) * tq - 1) // tk
        masks = _head_masks(c)
        q = q_ref[...]
        qh = [jnp.where(masks[h], q, 0.0).astype(MXU_DT) for h in range(GROUP_HEADS)]
        lane = lax.broadcasted_iota(jnp.int32, (1, LANES), 1)
        later = _tri(tk, upper=False)
        r_scr[...] = jnp.zeros_like(r_scr)
        acc_scr[...] = jnp.zeros_like(acc_scr)
        rs_ref[...] = jnp.full((tq, LANES), 2.0 * SB_DEAD, F32)

        def step(state):
            jj, _ = state
            j = last - jj
            rows = pl.ds(pl.multiple_of(j * tk, tk), tk)
            kb = k_ref[rows, :].astype(MXU_DT)
            vb = v_ref[rows, :].astype(MXU_DT)
            row = i * tq + lax.broadcasted_iota(jnp.int32, (tq, tk), 0)
            col = j * tk + lax.broadcasted_iota(jnp.int32, (tq, tk), 1)
            strict = col < row
            acc = acc_scr[...]
            rs = rs_ref[...]
            for h in range(GROUP_HEADS):
                z = _dot_nt(qh[h], kb) * scale
                log_keep, log_beta = _sb_logs(z, strict)
                r_old = r_scr[h]
                rs = jnp.where(lane == _sb_carry_lane(jj, h), r_old, rs)
                rest = r_old + _split_dot(log_keep, later)
                w = jnp.where(strict, jnp.exp(log_beta + rest), 0.0)
                acc = jnp.where(masks[h], acc + _dot(w.astype(MXU_DT), vb), acc)
                r_scr[h] = r_old + jnp.sum(log_keep, axis=-1, keepdims=True)
            acc_scr[...] = acc
            rs_ref[...] = rs
            return jj + 1, jnp.max(r_scr[...])

        lax.while_loop(lambda s: (s[0] <= last) & (s[1] > SB_DEAD), step, (jnp.int32(0), jnp.float32(0.0)))
        o_ref[...] = acc_scr[...]

    resident = lambda blk: pl.BlockSpec((t, c), lambda i: (0, blk), pipeline_mode=pl.Buffered(1))
    call = dict(
        name=name, grid=(nq,),
        in_specs=[pl.BlockSpec((tq, c), lambda i: (i, qb)), resident(qb + 1), resident(qb + 2)],
        out_specs=[pl.BlockSpec((tq, c), lambda i: (i, 0)), pl.BlockSpec((tq, LANES), lambda i: (i, 0))],
        out_shape=[jax.ShapeDtypeStruct((t, c), F32), jax.ShapeDtypeStruct((t, LANES), F32)],
        scratch_shapes=[pltpu.VMEM((GROUP_HEADS, tq, 1), F32), pltpu.VMEM((tq, c), F32)],
        compiler_params=_cparams(("arbitrary",)),
    )
    outs, got = carry_comm(call, body, (proj, proj, proj), comm, 2, *_grid_ends(nq))
    return (*outs, got)


def sb_bwd(proj, rsave, do, *, name, tq=1024, tk=256, comm=None):
    t = proj.shape[0]
    c = GROUP_WIDTH
    tq, tk, nq, nk = _attn_tiles(t, tq, tk)
    qb = P_SB // c
    scale = HEAD_DIM ** -0.5

    def body(q_ref, k_ref, v_ref, rs_ref, do_ref, dq_ref, dk_hbm, dv_hbm, e_scr, dq_scr, dk_scr, dv_scr):
        i = pl.program_id(0)
        last = ((i + 1) * tq - 1) // tk
        masks = _head_masks(c)

        @pl.when(i == 0)
        def _():
            dk_scr[...] = jnp.zeros_like(dk_scr)
            dv_scr[...] = jnp.zeros_like(dv_scr)

        e_scr[...] = jnp.zeros_like(e_scr)
        dq_scr[...] = jnp.zeros_like(dq_scr)
        q = q_ref[...]
        qf = q.astype(MXU_DT)
        qh = [jnp.where(masks[h], q, 0.0).astype(MXU_DT) for h in range(GROUP_HEADS)]
        do = do_ref[...]
        dob = do.astype(MXU_DT)
        doh = [jnp.where(masks[h], do, 0.0).astype(MXU_DT) for h in range(GROUP_HEADS)]
        later = _tri(tk, upper=False)
        earlier = _tri(tk, upper=True)
        rs = rs_ref[...]
        lane = lax.broadcasted_iota(jnp.int32, (1, LANES), 1)
        visited = jnp.where(jnp.max(rs, axis=0, keepdims=True) > SB_DEAD, (lane // GROUP_HEADS + 1).astype(F32), 0.0)
        n_visited = jnp.minimum(jnp.max(visited).astype(jnp.int32), last + 1)

        def step(it, carry):
            jj = n_visited - 1 - it
            j = last - jj
            rows = pl.ds(pl.multiple_of(j * tk, tk), tk)
            kb = k_ref[rows, :].astype(MXU_DT)
            vb = v_ref[rows, :].astype(MXU_DT)
            row = i * tq + lax.broadcasted_iota(jnp.int32, (tq, tk), 0)
            col = j * tk + lax.broadcasted_iota(jnp.int32, (tq, tk), 1)
            strict = col < row
            dq = dq_scr[...]
            dk_upd = jnp.zeros((tk, c), F32)
            dv_upd = jnp.zeros((tk, c), F32)
            for h in range(GROUP_HEADS):
                z = _dot_nt(qh[h], kb) * scale
                log_keep, log_beta = _sb_logs(z, strict)
                r_h = jnp.sum(jnp.where(lane == _sb_carry_lane(jj, h), rs, 0.0), axis=-1, keepdims=True)
                rest = r_h + _split_dot(log_keep, later)
                w = jnp.where(strict, jnp.exp(log_beta + rest), 0.0)
                e = w * _dot_nt(doh[h], vb)
                e_old = e_scr[h]
                dkeep = e_old + _split_dot(e, earlier)
                dz = jnp.where(strict, e * jnp.exp(log_keep) - dkeep * jnp.exp(log_beta), 0.0)
                dzb = dz.astype(MXU_DT)
                dq = jnp.where(masks[h], dq + _dot(dzb, kb) * scale, dq)
                dk_upd = jnp.where(masks[h], _dot_tn(dzb, qf) * scale, dk_upd)
                dv_upd = jnp.where(masks[h], _dot_tn(w.astype(MXU_DT), dob), dv_upd)
                e_scr[h] = e_old + jnp.sum(e, axis=-1, keepdims=True)
            dq_scr[...] = dq
            dk_scr[rows, :] += dk_upd
            dv_scr[rows, :] += dv_upd
            return carry

        lax.fori_loop(0, n_visited, step, 0)
        dq_ref[...] = dq_scr[...]

        @pl.when(i == nq - 1)
        def _():
            pltpu.sync_copy(dk_scr, dk_hbm)
            pltpu.sync_copy(dv_scr, dv_hbm)

    qrow = lambda i: (i, 0)
    resident = lambda blk: pl.BlockSpec((t, c), lambda i: (0, blk), pipeline_mode=pl.Buffered(1))
    hbm = pl.BlockSpec(memory_space=pl.ANY)
    call = dict(
        name=name, grid=(nq,),
        in_specs=[pl.BlockSpec((tq, c), lambda i: (i, qb)), resident(qb + 1), resident(qb + 2),
                  pl.BlockSpec((tq, LANES), qrow), pl.BlockSpec((tq, c), qrow)],
        out_specs=[pl.BlockSpec((tq, c), qrow), hbm, hbm],
        out_shape=[jax.ShapeDtypeStruct((t, c), F32)] * 3,
        scratch_shapes=[pltpu.VMEM((GROUP_HEADS, tq, 1), F32), pltpu.VMEM((tq, c), F32),
                        pltpu.VMEM((t, c), F32), pltpu.VMEM((t, c), F32)],
        compiler_params=_cparams(("arbitrary",)),
    )
    outs, got = carry_comm(call, body, (proj, proj, proj, rsave, do), comm, 3, *_grid_ends(nq))
    return (*outs, got)


A_COL, B_COL = 0, 4
Z_BLK = P_Z // GROUP_WIDTH
GDN_CHUNKS_PER_STEP = 4


NN = (((1,), (0,)), ((), ()))
NT = (((1,), (1,)), ((), ()))
TN = (((0,), (0,)), ((), ()))


def _terms(x, n):
    out, rem = [], x
    for _ in range(n):
        t = rem.astype(MXU_DT)
        out.append(t)
        rem = rem - t.astype(F32)
    return out


def _dotp(a, b, dims, a_terms=2, b_terms=2):
    at, bt = _terms(a, a_terms), _terms(b, b_terms)
    out = None
    for i, x in enumerate(at):
        for j, y in enumerate(bt):
            if i + j < max(a_terms, b_terms):
                r = lax.dot_general(x, y, dims, preferred_element_type=F32)
                out = r if out is None else out + r
    return out


def _silu(x):
    return x * _sigmoid(x)


def _dsilu(x):
    s = _sigmoid(x)
    return s * (1.0 + x * (1.0 - s))


def _head_sum(x, masks):
    out = jnp.zeros_like(x)
    for m in masks:
        out = jnp.where(m, jnp.sum(jnp.where(m, x, 0.0), axis=-1, keepdims=True), out)
    return out


def _expand(cols, col0, masks):
    out = jnp.zeros((cols.shape[0], GROUP_WIDTH), F32)
    for h, m in enumerate(masks):
        out = jnp.where(m, cols[:, col0 + h:col0 + h + 1], out)
    return out


def _reduce(x, col0, masks):
    lane = lax.broadcasted_iota(jnp.int32, (1, LANES), 1)
    out = jnp.zeros((x.shape[0], LANES), F32)
    for h, m in enumerate(masks):
        out = jnp.where(lane == col0 + h, jnp.sum(jnp.where(m, x, 0.0), axis=-1, keepdims=True), out)
    return out


def _block_ones():
    ri = lax.broadcasted_iota(jnp.int32, (GROUP_WIDTH, GROUP_WIDTH), 0) // HEAD_DIM
    ci = lax.broadcasted_iota(jnp.int32, (GROUP_WIDTH, GROUP_WIDTH), 1) // HEAD_DIM
    return jnp.where(ri == ci, 1.0, 0.0).astype(F32)


def _blk(x, hs):
    return jnp.concatenate([x] * GROUP_HEADS, axis=0) * hs


def _unblk(m, hs):
    mm = m * hs
    c = GDN_CHUNK
    return mm[0:c] + mm[c:2 * c] + mm[2 * c:3 * c] + mm[3 * c:4 * c]


def _row_mask4():
    ri = lax.broadcasted_iota(jnp.int32, (GROUP_WIDTH, LANES), 0) // HEAD_DIM
    ci = lax.broadcasted_iota(jnp.int32, (GROUP_WIDTH, LANES), 1)
    return jnp.where(ri + A_COL == ci, 1.0, 0.0).astype(F32)


def _lockstep(gens):
    results = [None] * len(gens)
    live = list(range(len(gens)))
    while live:
        for i in list(live):
            try:
                next(gens[i])
            except StopIteration as stop:
                results[i] = stop.value
                live.remove(i)
    return results


def _gdn_chunk(xc, small, avec, dtvec, state, masks, hs):
    (f,) = _lockstep([_gdn_local(xc, small, avec, dtvec, masks, hs)])
    return _gdn_recur(f, state, hs)


def _gdn_local(xc, small, avec, dtvec, masks, hs):
    c = GDN_CHUNK
    w = GROUP_WIDTH
    b16 = lambda v: v.astype(MXU_DT)
    f = {}
    xq, xk, xv = xc[:, :w], xc[:, w:2 * w], xc[:, 2 * w:]
    qs, ks, v = _silu(xq), _silu(xk), _silu(xv)
    rq = lax.rsqrt(_head_sum(qs * qs, masks) + L2_EPS)
    rk = lax.rsqrt(_head_sum(ks * ks, masks) + L2_EPS)
    qn = qs * rq
    k = ks * rk
    q = qn * (HEAD_DIM ** -0.5)
    xg = small + dtvec
    sp = jnp.maximum(xg, 0.0) + jnp.log(1.0 + jnp.exp(-jnp.abs(xg)))
    g128 = -avec * sp
    beta128 = _sigmoid(small)
    ri = lax.broadcasted_iota(jnp.int32, (c, c), 0)
    ci = lax.broadcasted_iota(jnp.int32, (c, c), 1)
    tril = jnp.where(ri >= ci, 1.0, 0.0).astype(F32)
    gam128 = _dotp(tril, g128, NN, 1, 3)
    yield
    gam = _expand(gam128, A_COL, masks)
    bfull = _expand(beta128, B_COL, masks)
    mask4 = _row_mask4()
    ones = jnp.ones((c, LANES), F32)
    gam_row = _dotp(ones, jnp.concatenate([gam128] * GROUP_HEADS, axis=0) * mask4, NT, 1, 3)
    yield
    li = lax.broadcasted_iota(jnp.int32, (c, w), 0)
    lj = lax.broadcasted_iota(jnp.int32, (c, w), 1) % HEAD_DIM
    incl = li >= lj
    strict = li > lj
    dmat = jnp.exp(jnp.where(incl, gam - gam_row, NEG_BIG))
    egam = jnp.exp(gam)
    glast = gam[c - 1:c, :]
    ekd = jnp.exp(glast - gam)
    kb = k * bfull
    vb = v * bfull
    kbg = kb * egam
    qd = q * egam
    kd = k * ekd
    kblk = b16(_blk(k, hs))
    araw = _dot_nt(b16(kb), kblk)
    qk = _dot_nt(b16(q), kblk)
    yield
    a = jnp.where(strict, araw * dmat, 0.0)
    tm = jnp.where(li == lj, 1.0, 0.0) - a
    p = a
    for _ in range(5):
        p = _dotp(p, _blk(p, hs), NN)
        yield
        tm = tm + _dotp(tm, _blk(p, hs), NN)
        yield
    tm16 = b16(tm)
    u = _dot(tm16, b16(_blk(vb, hs)))
    wm = _dot(tm16, b16(_blk(kbg, hs)))
    aqk = jnp.where(incl, qk * dmat, 0.0)
    f.update(xq=xq, xk=xk, xv=xv, v=v, rq=rq, rk=rk, qn=qn, k=k, q=q, xg=xg, g128=g128, beta128=beta128,
             tril=tril, gam=gam, bfull=bfull, mask4=mask4, ones=ones, incl=incl, strict=strict, li=li,
             dmat=dmat, egam=egam, glast=glast, ekd=ekd, kb=kb, vb=vb, kbg=kbg, qd=qd, kd=kd, kblk=kblk,
             araw=araw, tm=tm, tm16=tm16, wm=wm, qk=qk, aqk=aqk, u=u)
    return f


def _gdn_recur(f, state, hs):
    b16 = lambda v: v.astype(MXU_DT)
    s16 = b16(state)
    vn = f["u"] - _dot(b16(f["wm"]), s16)
    o = _dot(b16(f["qd"]), s16) + _dot(b16(f["aqk"]), b16(_blk(vn, hs)))
    s_new = state * jnp.exp(f["glast"]) + hs * _dot_tn(b16(f["kd"]), b16(vn))
    f.update(s16=s16, vn=vn, o=o, s_new=s_new)
    return f


def _decay_rate(a_log):
    lane = lax.broadcasted_iota(jnp.int32, a_log.shape, 1)
    return jnp.where((lane >= A_COL) & (lane < A_COL + GROUP_HEADS), jnp.exp(a_log), 0.0)


def _gdn_post(o, z, ng, masks):
    r = lax.rsqrt(_head_sum(o * o, masks) * (1.0 / HEAD_DIM) + RMS_EPS)
    on = o * r
    return on, r, on * ng * _silu(z)


def gdn_fwd(cqkv, proj, avec, dtvec, ng, *, name, comm=None):
    t = cqkv.shape[0]
    c = GDN_CHUNK
    w = GROUP_WIDTH
    n = t // c

    def body(x_ref, z_ref, sm_ref, a_ref, dt_ref, ng_ref, y_ref, st_ref, s_scr):
        @pl.when(pl.program_id(0) == 0)
        def _():
            s_scr[...] = jnp.zeros_like(s_scr)

        masks = _head_masks(w)
        hs = _block_ones()
        avec_v = _decay_rate(a_ref[...])
        rows = [pl.ds(k * c, c) for k in range(sub)]
        fs = _lockstep([_gdn_local(x_ref[r, :], sm_ref[r, :], avec_v, dt_ref[...], masks, hs) for r in rows])
        state = s_scr[...]
        for k, r in enumerate(rows):
            st_ref[k] = state
            f = _gdn_recur(fs[k], state, hs)
            _, _, y = _gdn_post(f["o"], z_ref[r, :], ng_ref[...], masks)
            y_ref[r, :] = y
            state = f["s_new"]
        s_scr[...] = state

    sub = GDN_CHUNKS_PER_STEP if n % GDN_CHUNKS_PER_STEP == 0 else 1
    rows, steps = c * sub, n // sub
    vec = pl.BlockSpec((1, LANES), lambda i: (0, 0))
    call = dict(
        name=name, grid=(steps,),
        in_specs=[pl.BlockSpec((rows, 3 * w), lambda i: (i, 0)),
                  pl.BlockSpec((rows, w), lambda i: (i, Z_BLK)),
                  pl.BlockSpec((rows, LANES), lambda i: (i, SMALL_BLK)),
                  vec, vec, pl.BlockSpec((1, w), lambda i: (0, 0))],
        out_specs=[pl.BlockSpec((rows, w), lambda i: (i, 0)), pl.BlockSpec((sub, w, w), lambda i: (i, 0, 0))],
        out_shape=[jax.ShapeDtypeStruct((t, w), F32), jax.ShapeDtypeStruct((n, w, w), F32)],
        scratch_shapes=[pltpu.VMEM((w, w), F32)],
        compiler_params=_cparams(("arbitrary",)),
    )
    outs, got = carry_comm(call, body, (cqkv, proj, proj, avec, dtvec, ng), comm, 2, *_grid_ends(steps))
    return (*outs, got)


def gdn_bwd(cqkv, proj, avec, dtvec, ng, states, dy, *, name, comm=None):
    t = cqkv.shape[0]
    c = GDN_CHUNK
    w = GROUP_WIDTH
    n = t // c
    b16 = lambda v: v.astype(MXU_DT)

    def body(x_ref, z_ref, sm_ref, a_ref, dt_ref, ng_ref, st_ref, dy_ref,
             dx_ref, dz_ref, dsm_ref, dng_ref, dal_ref, ddt_ref, ds_scr):
        @pl.when(pl.program_id(0) == 0)
        def _():
            ds_scr[...] = jnp.zeros_like(ds_scr)
            dng_ref[...] = jnp.zeros_like(dng_ref)
            dal_ref[...] = jnp.zeros_like(dal_ref)
            ddt_ref[...] = jnp.zeros_like(ddt_ref)

        masks = _head_masks(w)
        hs = _block_ones()
        avec_v = _decay_rate(a_ref[...])
        rows = [pl.ds(k * c, c) for k in range(sub)]
        fs = _lockstep([_gdn_local(x_ref[r, :], sm_ref[r, :], avec_v, dt_ref[...], masks, hs) for r in rows])
        fs = [_gdn_recur(f, st_ref[k], hs) for k, f in enumerate(fs)]
        _lockstep([chunk(fs[k], st_ref[k], avec_v, masks, hs, z_ref.at[r, :], ng_ref, dy_ref.at[r, :], dx_ref.at[r, :],
                         dz_ref.at[r, :], dsm_ref.at[r, :], dng_ref, dal_ref, ddt_ref, ds_scr)
                   for k, r in reversed(list(enumerate(rows)))])

    def chunk(f, state, avec_v, masks, hs, z_ref, ng_ref, dy_ref, dx_ref, dz_ref, dsm_ref, dng_ref, dal_ref, ddt_ref,
              ds_scr):
        z = z_ref[...]
        ng_v = ng_ref[...]
        dy_v = dy_ref[...]
        on, r, _ = _gdn_post(f["o"], z, ng_v, masks)
        sz = _silu(z)
        dz_ref[...] = dy_v * on * ng_v * _dsilu(z)
        d_on = dy_v * ng_v * sz
        dng_ref[...] += jnp.sum(dy_v * on * sz, axis=0, keepdims=True)
        do = r * (d_on - on * _head_sum(d_on * on, masks) * (1.0 / HEAD_DIM))
        do16 = b16(do)
        dsn = ds_scr[...]
        dsn16 = b16(dsn)
        s16, vn, kd, qd, wm = f["s16"], f["vn"], f["kd"], f["qd"], f["wm"]
        k, q, kblk, tm, tm16 = f["k"], f["q"], f["kblk"], f["tm"], f["tm16"]
        dmat, egam, glast, gam = f["dmat"], f["egam"], f["glast"], f["gam"]
        incl, strict, li = f["incl"], f["strict"], f["li"]
        vn16 = b16(vn)
        dvn = _unblk(_dot_tn(b16(f["aqk"]), do16), hs) + _dot(b16(kd), dsn16)
        daqk = jnp.where(incl, _dot_nt(do16, b16(_blk(vn, hs))), 0.0)
        dqd = _dot_nt(do16, s16)
        dvn16 = b16(dvn)
        ds_scr[...] = hs * (_dot_tn(b16(qd), do16) - _dot_tn(b16(wm), dvn16)) + dsn * jnp.exp(glast)
        yield
        dkd = _dot_nt(vn16, dsn16)
        dglast = jnp.sum(dsn * state, axis=0, keepdims=True) * jnp.exp(glast)
        du16 = dvn16
        dw16 = b16(-_dot_nt(dvn16, s16))
        yield
        dqk16 = b16(daqk * dmat)
        ddm = daqk * f["qk"]
        dq = _dot(dqk16, kblk)
        dk = _unblk(_dot_tn(dqk16, b16(q)), hs)
        dtm = _dot_nt(du16, b16(_blk(f["vb"], hs))) + _dot_nt(dw16, b16(_blk(f["kbg"], hs)))
        dvb = _unblk(_dot_tn(tm16, du16), hs)
        dkbg = _unblk(_dot_tn(tm16, dw16), hs)
        yield
        xx = _unblk(_dotp(tm, dtm, TN), hs)
        yield
        da = jnp.where(strict, -_dotp(xx, _blk(tm, hs), NT), 0.0)
        yield
        daraw16 = b16(da * dmat)
        ddm = ddm + da * f["araw"]
        dkb = _dot(daraw16, kblk)
        dk = dk + _unblk(_dot_tn(daraw16, b16(f["kb"])), hs)
        yield
        tcol = ddm * dmat
        dgam = tcol
        dgam128_row = _dotp(-tcol, f["ones"], TN, 2, 1) * f["mask4"]
        yield
        dgam128_row = (dgam128_row[0:c] + dgam128_row[c:2 * c] + dgam128_row[2 * c:3 * c] + dgam128_row[3 * c:4 * c])
        dk = dk + dkd * f["ekd"]
        tt = dkd * kd
        dgam = dgam - tt
        dglast = dglast + jnp.sum(tt, axis=0, keepdims=True)
        dq = dq + dqd * egam
        dgam = dgam + dqd * qd
        dkb = dkb + dkbg * egam
        dgam = dgam + dkbg * f["kbg"]
        dk = dk + dkb * f["bfull"]
        dbf = dkb * k + dvb * f["v"]
        dv = dvb * f["bfull"]
        dgam = dgam + jnp.where(li == c - 1, dglast, 0.0)
        beta128 = f["beta128"]
        db128 = _reduce(dbf, B_COL, masks) * beta128 * (1.0 - beta128)
        dgam128 = _reduce(dgam, A_COL, masks) + dgam128_row
        dg128 = _dotp(f["tril"], dgam128, TN, 1, 2)
        yield
        dxg = dg128 * (-avec_v * _sigmoid(f["xg"]))
        lane = lax.broadcasted_iota(jnp.int32, (1, LANES), 1)
        dsm_ref[...] = jnp.where(lane < B_COL, dxg, db128)
        ddt_ref[...] += jnp.sum(dxg, axis=0, keepdims=True)
        dal_ref[...] += jnp.sum(dg128 * f["g128"], axis=0, keepdims=True)
        dqn = dq * (HEAD_DIM ** -0.5)
        dqs = f["rq"] * (dqn - f["qn"] * _head_sum(dqn * f["qn"], masks))
        dks = f["rk"] * (dk - k * _head_sum(dk * k, masks))
        dx_ref[:, :w] = dqs * _dsilu(f["xq"])
        dx_ref[:, w:2 * w] = dks * _dsilu(f["xk"])
        dx_ref[:, 2 * w:] = dv * _dsilu(f["xv"])

    sub = GDN_CHUNKS_PER_STEP if n % GDN_CHUNKS_PER_STEP == 0 else 1
    rows, steps = c * sub, n // sub
    vec = pl.BlockSpec((1, LANES), lambda i: (0, 0))
    rev = lambda blk: (lambda i: (steps - 1 - i, blk))
    call = dict(
        name=name, grid=(steps,),
        in_specs=[pl.BlockSpec((rows, 3 * w), rev(0)),
                  pl.BlockSpec((rows, w), rev(Z_BLK)),
                  pl.BlockSpec((rows, LANES), rev(SMALL_BLK)),
                  vec, vec, pl.BlockSpec((1, w), lambda i: (0, 0)),
                  pl.BlockSpec((sub, w, w), lambda i: (steps - 1 - i, 0, 0)),
                  pl.BlockSpec((rows, w), rev(0))],
        out_specs=[pl.BlockSpec((rows, 3 * w), rev(0)), pl.BlockSpec((rows, w), rev(0)),
                   pl.BlockSpec((rows, LANES), rev(0)),
                   pl.BlockSpec((1, w), lambda i: (0, 0)), vec, vec],
        out_shape=[jax.ShapeDtypeStruct((t, 3 * w), F32), jax.ShapeDtypeStruct((t, w), F32),
                   jax.ShapeDtypeStruct((t, LANES), F32), jax.ShapeDtypeStruct((1, w), F32),
                   jax.ShapeDtypeStruct((1, LANES), F32), jax.ShapeDtypeStruct((1, LANES), F32)],
        scratch_shapes=[pltpu.VMEM((w, w), F32)],
        compiler_params=_cparams(("arbitrary",)),
    )
    outs, got = carry_comm(call, body, (cqkv, proj, proj, avec, dtvec, ng, states, dy), comm, 6, *_grid_ends(steps))
    return (*outs, got)


def adamw(w, m, v, gslots, *, row0=0, name):
    r, c = w.shape
    s = gslots.shape[0]
    tr = _tile(r, 64, 8)
    assert row0 % tr == 0 and gslots.shape[2] == c
    rb = row0 // tr
    c1 = 1.0 - ADAM_B1 ** ADAM_STEP
    c2 = 1.0 - ADAM_B2 ** ADAM_STEP

    def body(w_ref, m_ref, v_ref, gs_ref, g_ref, d_ref, mo_ref, vo_ref):
        g = gs_ref[0].astype(F32)
        for k in range(1, s):
            g = g + gs_ref[k].astype(F32)
        m_new = ADAM_B1 * m_ref[...] + (1.0 - ADAM_B1) * g
        v_new = ADAM_B2 * v_ref[...] + (1.0 - ADAM_B2) * (g * g)
        m_hat = m_new / c1
        v_hat = v_new / c2
        g_ref[...] = g
        mo_ref[...] = m_new
        vo_ref[...] = v_new
        d_ref[...] = -ADAM_LR * (m_hat / (jnp.sqrt(v_hat) + ADAM_EPS) + ADAM_WD * w_ref[...])

    row = pl.BlockSpec((tr, c), lambda i: (i, 0))
    return pl.pallas_call(
        body, name=name, grid=(r // tr,),
        in_specs=[row, row, row, pl.BlockSpec((s, tr, c), lambda i: (0, rb + i, 0))],
        out_specs=[row] * 4,
        out_shape=[jax.ShapeDtypeStruct((r, c), F32)] * 4,
        compiler_params=_cparams(("parallel",)),
    )(w, m, v, gslots)


def slot_sum(slots, *, name):
    s, r, c = slots.shape

    def body(s_ref, o_ref):
        acc = s_ref[0]
        for k in range(1, s):
            acc = acc + s_ref[k]
        o_ref[...] = acc

    return pl.pallas_call(
        body, name=name, grid=(1,),
        in_specs=[pl.BlockSpec((s, r, c), lambda i: (0, 0, 0))],
        out_specs=pl.BlockSpec((r, c), lambda i: (0, 0)),
        out_shape=jax.ShapeDtypeStruct((r, c), F32),
        compiler_params=_cparams(("arbitrary",)),
    )(slots)


class Comm:
    def __init__(self, srcs, broadcast):
        self.srcs = list(srcs)
        self.broadcast = [broadcast] * len(self.srcs) if isinstance(broadcast, bool) else list(broadcast)
        self.n = len(self.srcs)
        self.out_shapes = [jax.ShapeDtypeStruct(((N_DEV,) + s.shape) if b else s.shape, s.dtype)
                           for s, b in zip(self.srcs, self.broadcast)]
        self.sems = [pltpu.SemaphoreType.DMA((self.n,))] * 3

    def _local(self, src_refs, out_refs, loc_sem, a, me):
        src = src_refs[a] if self.broadcast[a] else src_refs[a].at[me]
        return pltpu.make_async_copy(src, out_refs[a].at[me], loc_sem.at[a])

    def start(self, src_refs, out_refs, send_sem, recv_sem, loc_sem):
        x, y, c = lax.axis_index("x"), lax.axis_index("y"), lax.axis_index("c")
        me = 4 * x + 2 * y + c
        for a in range(self.n):
            self._local(src_refs, out_refs, loc_sem, a, me).start()
        for d in range(1, N_DEV):
            px, py, pc = x ^ ((d >> 2) & 1), y ^ ((d >> 1) & 1), c ^ (d & 1)
            peer = 4 * px + 2 * py + pc
            for a in range(self.n):
                src = src_refs[a] if self.broadcast[a] else src_refs[a].at[peer]
                pltpu.make_async_remote_copy(
                    src_ref=src, dst_ref=out_refs[a].at[me],
                    send_sem=send_sem.at[a], recv_sem=recv_sem.at[a],
                    device_id=(px, py, pc), device_id_type=pl.DeviceIdType.MESH).start()

    def wait(self, src_refs, out_refs, send_sem, recv_sem, loc_sem):
        x, y, c = lax.axis_index("x"), lax.axis_index("y"), lax.axis_index("c")
        me = 4 * x + 2 * y + c
        for a in range(self.n):
            seven = out_refs[a].at[pl.ds(0, N_DEV - 1)]
            pltpu.make_async_remote_copy(
                src_ref=seven, dst_ref=seven, send_sem=send_sem.at[a], recv_sem=recv_sem.at[a],
                device_id=(x, y, c), device_id_type=pl.DeviceIdType.MESH).wait()
            self._local(src_refs, out_refs, loc_sem, a, me).wait()


def exchange(srcs, *, broadcast, name):
    comm = Comm(srcs, broadcast)
    n = comm.n

    def body(*refs):
        src_refs, out_refs, sems = refs[:n], refs[n:2 * n], refs[2 * n:]
        comm.start(src_refs, out_refs, *sems)
        comm.wait(src_refs, out_refs, *sems)

    anyspec = pl.BlockSpec(memory_space=pl.ANY)
    return pl.pallas_call(
        body, name=name,
        in_specs=[anyspec] * n, out_specs=[anyspec] * n, out_shape=comm.out_shapes,
        scratch_shapes=comm.sems,
        compiler_params=pltpu.CompilerParams(has_side_effects=True),
    )(*srcs)


def carry_comm(call_kwargs, body, args, comm, n_out, is_first, is_last):
    if comm is None:
        return pl.pallas_call(body, **call_kwargs)(*args), []
    n_in, nc = len(args), comm.n
    n_scr = len(call_kwargs["scratch_shapes"])
    anyspec = pl.BlockSpec(memory_space=pl.ANY)

    def wrapped(*refs):
        ins, csrc = refs[:n_in], refs[n_in:n_in + nc]
        outs = refs[n_in + nc:n_in + nc + n_out]
        cout = refs[n_in + nc + n_out:n_in + 2 * nc + n_out]
        rest = refs[n_in + 2 * nc + n_out:]
        scr, sems = rest[:n_scr], rest[n_scr:]

        @pl.when(is_first())
        def _():
            comm.start(csrc, cout, *sems)

        body(*ins, *outs, *scr)

        @pl.when(is_last())
        def _():
            comm.wait(csrc, cout, *sems)

    kw = dict(call_kwargs)
    kw["in_specs"] = list(kw["in_specs"]) + [anyspec] * nc
    kw["out_specs"] = list(kw["out_specs"]) + [anyspec] * nc
    kw["out_shape"] = list(kw["out_shape"]) + comm.out_shapes
    kw["scratch_shapes"] = list(kw["scratch_shapes"]) + comm.sems
    cp = kw["compiler_params"]
    kw["compiler_params"] = pltpu.CompilerParams(dimension_semantics=cp.dimension_semantics,
                                                 vmem_limit_bytes=cp.vmem_limit_bytes, has_side_effects=True)
    res = pl.pallas_call(wrapped, **kw)(*args, *comm.srcs)
    return res[:n_out], res[n_out:]


def _pack(arrs):
    flat = []
    for a in arrs:
        f = a.reshape(-1).astype(F32)
        flat.append(jnp.pad(f, (0, (-f.shape[0]) % LANES)))
    buf = jnp.concatenate(flat)
    buf = jnp.pad(buf, (0, (-buf.shape[0]) % (8 * LANES)))
    return buf.reshape(-1, LANES)


def _unpack(buf, shapes):
    flat = buf.reshape(-1)
    out, off = [], 0
    for s in shapes:
        sz = int(np.prod(s))
        out.append(flat[off:off + sz].reshape(s))
        off += sz + (-sz) % LANES
    return out


def _win_to_aligned(w):
    o = np.cumsum((0,) + IN_SPLITS)
    seg = lambda i: w[..., o[i]:o[i + 1]]
    pad = jnp.zeros(w.shape[:-1] + (P_WIDTH - IN_WIDTH,), w.dtype)
    return jnp.concatenate([seg(0), seg(1), seg(4), seg(6), seg(7), seg(2), seg(3), seg(5), pad], axis=-1)


def _win_from_aligned(w):
    o = np.cumsum((0,) + IN_SPLITS)
    s = P_SMALL
    return jnp.concatenate([w[..., P_GDN:P_GDN + 768], w[..., P_Z:P_Z + 256], w[..., s:s + 4], w[..., s + 4:s + 8],
                            w[..., P_FOX:P_FOX + 768], w[..., s + 8:s + 12], w[..., P_CONF:P_CONF + 512],
                            w[..., P_SB:P_SB + 768]], axis=-1)


def _row128(vals, col0):
    return jnp.pad(vals.astype(F32)[None, :], ((0, 0), (col0, LANES - col0 - GROUP_HEADS)))


def _ffn_fwd(x, x16, w, n, tag, comm=None, on_comm=None):
    gu = mm(x16, w[f"gu{n}"], name=f"{tag}_gu", tm=1024, tn=512, tk=1024, out_dtype=MXU_DT, comm=comm)
    if comm is not None:
        gu, got = gu
        on_comm(got)
    h = act_fwd(gu, name=f"{tag}_act")
    y = mm(h, w[f"d{n}"], name=f"{tag}_down", tm=1024, tn=512, tk=D_FF)
    out, out16, xh, rs = ln_res_fwd(x, y, w[f"ln_ffn{n}_g"], w[f"ln_ffn{n}_b"], 0.5, name=f"{tag}_ln")
    return out, out16, (x16, gu, h, xh, rs)


def _ffn_bwd(dout, saved, w, n, tag, comm_dh=None, comm_dwgu=None, comm_dx=None):
    x, gu, h, xh, rs = saved
    wgu, wd = w[f"gu{n}"], w[f"d{n}"]
    got = [[], [], []]
    dz, dg, db = ln_res_bwd(dout, xh, rs, w[f"ln_ffn{n}_g"], name=f"{tag}_ln_bwd")
    dh = mm(dz, wd, mode="nt", alpha=0.5, name=f"{tag}_dh", tm=1024, tn=D_FF // 2, tk=1024, comm=comm_dh)
    if comm_dh is not None:
        dh, got[0] = dh
    dgu = act_bwd(gu, dh, name=f"{tag}_act_bwd")
    dwd = mm(h, dz, mode="tn", alpha=0.5, name=f"{tag}_dwd", tm=D_FF // 2, tn=1024, tk=512)
    c = comm_dwgu(dwd) if comm_dwgu is not None else None
    dwgu = mm(x, dgu, mode="tn", name=f"{tag}_dwgu", tm=1024, tn=D_FF // 2, tk=512, comm=c)
    if c is not None:
        dwgu, got[1] = dwgu
    c = comm_dx(dwgu) if comm_dx is not None else None
    dx = mm(dgu, wgu, mode="nt", add=dz, beta=DN_ALPHA, name=f"{tag}_dx", tm=1024, tn=1024, tk=D_FF // 2, comm=c)
    if c is not None:
        dx, got[2] = dx
    return dx, dwgu, dwd, dg, db, got


def _layer_fwd(x, x16, mem, w, tag, comm_ffn1=None, on_ffn1=None, comm_gdn=None, on_gdn=None, comm_fox=None,
               on_fox=None, comm_sb=None, on_sb=None):
    sv = {}
    x1, x1h, sv["ffn1"] = _ffn_fwd(x, x16, w, 1, f"{tag}_ffn1", comm=comm_ffn1, on_comm=on_ffn1)
    proj = mm(x1h, w["win"], name=f"{tag}_inproj", tm=1024, tn=640, tk=1024)
    cqkv = dwconv_fwd(proj, w["gdn_conv_w"], None, col0=P_GDN, name=f"{tag}_gdn_conv")
    ya, states, got = gdn_fwd(cqkv, proj, w["alog"], w["dtb"], w["ng"], name=f"{tag}_gdn", comm=comm_gdn)
    if on_gdn is not None:
        on_gdn(got)
    cum = fox_gate_fwd(proj, w["bf"], name=f"{tag}_fox_gate")
    cum_t = jnp.pad(cum[:, FOX_COL:FOX_COL + GROUP_HEADS].T, ((0, 8 - GROUP_HEADS), (0, 0)))
    yb, lse, got = fox_fwd(proj, cum, cum_t, name=f"{tag}_fox", comm=comm_fox)
    if on_fox is not None:
        on_fox(got)
    u = glu_fwd(proj, name=f"{tag}_glu")
    cc = dwconv_fwd(u, w["conf_dw_w"], w["conf_dw_b"], name=f"{tag}_conf_conv")
    yc = gn_silu_fwd(cc, w["conf_norm_g"], w["conf_norm_b"], name=f"{tag}_conf_norm")
    yd, rsave, got = sb_fwd(proj, name=f"{tag}_sb", comm=comm_sb)
    if on_sb is not None:
        on_sb(got)
    ycat = jnp.concatenate([ya, yb, yc, yd], axis=1).astype(MXU_DT)
    mix = mm(ycat, w["wout"], name=f"{tag}_outproj")
    x2, x2h, xh2, rs2 = ln_res_fwd(x1, mix, w["ln_mix_g"], w["ln_mix_b"], 1.0, name=f"{tag}_ln_mix")
    sv["mix"] = (x1h, proj, cqkv, states, cum, cum_t, yb, lse, u, cc, rsave, ycat, xh2, rs2)
    q = mm(x2h, w["wq"], name=f"{tag}_memq", out_dtype=MXU_DT)
    kv = mm(mem, w["wkv"], name=f"{tag}_memkv", tm=N_MEM, out_dtype=MXU_DT)
    att = memattn_fwd(q, kv, name=f"{tag}_memattn")
    mo = mm(att, w["wo"], name=f"{tag}_memo")
    x3, x3h, xh3, rs3 = ln_res_fwd(x2, mo, w["ln_mem_g"], w["ln_mem_b"], 1.0, name=f"{tag}_ln_mem")
    sv["mem"] = (x2h, q, kv, att, xh3, rs3)
    x4, x4h, sv["ffn2"] = _ffn_fwd(x3, x3h, w, 2, f"{tag}_ffn2")
    return x4, x4h, sv


def _layer_bwd(dx4, mem, sv, w, tag, plan, tail=None):
    t = dx4.shape[0]
    gr = {}
    dx3, gr["gu2"], gr["d2"], gr["ln_ffn2_g"], gr["ln_ffn2_b"], _ = _ffn_bwd(dx4, sv["ffn2"], w, 2, f"{tag}_ffn2")
    x2, q, kv, att, xh3, rs3 = sv["mem"]
    dz, gr["ln_mem_g"], gr["ln_mem_b"] = ln_res_bwd(dx3, xh3, rs3, w["ln_mem_g"], name=f"{tag}_ln_mem_bwd")
    datt = mm(dz, w["wo"], mode="nt", name=f"{tag}_datt", out_dtype=MXU_DT)
    gr["wo"] = mm(att, dz, mode="tn", name=f"{tag}_dwo", tk=512)
    dq, dkv = memattn_bwd(q, kv, datt, name=f"{tag}_memattn_bwd")
    gr["wq"] = mm(x2, dq, mode="tn", name=f"{tag}_dwq", tk=512)
    gr["wkv"] = mm(mem, dkv, mode="tn", name=f"{tag}_dwkv", tk=N_MEM)
    dx2 = mm(dq, w["wq"], mode="nt", add=dz, beta=DN_ALPHA, name=f"{tag}_dx2")
    x1, proj, cqkv, states, cum, cum_t, yb, lse, u, cc, rsave, ycat, xh2, rs2 = sv["mix"]
    dz, gr["ln_mix_g"], gr["ln_mix_b"] = ln_res_bwd(dx2, xh2, rs2, w["ln_mix_g"], name=f"{tag}_ln_mix_bwd")
    dycat = mm(dz, w["wout"], mode="nt", name=f"{tag}_dycat")
    gr["wout"] = mm(ycat, dz, mode="tn", name=f"{tag}_dwout", tk=512)
    comm_sb, comm_fox, comm_gdn = plan(gr)
    gw = GROUP_WIDTH
    dya, dyb, dyc, dyd = (dycat[:, i * gw:(i + 1) * gw] for i in range(4))
    dq_d, dk_d, dv_d, got_sb = sb_bwd(proj, rsave, dyd, name=f"{tag}_sb_bwd", comm=comm_sb)
    dcc, gr["conf_norm_g"], gr["conf_norm_b"] = gn_silu_bwd(cc, w["conf_norm_g"], w["conf_norm_b"], dyc,
                                                            name=f"{tag}_conf_norm_bwd")
    du, gr["conf_dw_w"], gr["conf_dw_b"] = dwconv_bwd(dcc, u, w["conf_dw_w"], name=f"{tag}_conf_conv_bwd")
    dglu = glu_bwd(proj, du, name=f"{tag}_glu_bwd")
    dq_b, dk_b, dv_b, dcc, dcr, got_fox = fox_bwd(proj, cum, cum_t, yb, lse, dyb, name=f"{tag}_fox_bwd", comm=comm_fox)
    dcum = dcc + jnp.pad(dcr[:, :GROUP_HEADS, :].transpose(0, 2, 1).reshape(t, GROUP_HEADS),
                   ((0, 0), (FOX_COL, LANES - FOX_COL - GROUP_HEADS)))
    dsm_f, dbf = fox_gate_bwd(dcum, proj, w["bf"], name=f"{tag}_fox_gate_bwd")
    gr["fox_b_f"] = dbf[0, FOX_COL:FOX_COL + GROUP_HEADS]
    dcq, dz_a, dsm_a, dng, dal, ddt, got_gdn = gdn_bwd(cqkv, proj, w["alog"], w["dtb"], w["ng"], states, dya,
                                                       name=f"{tag}_gdn_bwd", comm=comm_gdn)
    gr["gdn_norm_g"] = dng.reshape(GROUP_HEADS, HEAD_DIM).sum(0)
    gr["gdn_a_log"] = dal[0, A_COL:A_COL + GROUP_HEADS]
    gr["gdn_dt_bias"] = ddt[0, A_COL:A_COL + GROUP_HEADS]
    dgq, gr["gdn_conv_w"], _ = dwconv_bwd(dcq, proj, w["gdn_conv_w"], col0=P_GDN, name=f"{tag}_gdn_conv_bwd")
    dproj = jnp.concatenate([dgq, dz_a, dq_b, dk_b, dv_b, dglu, dq_d, dk_d, dv_d, dsm_a + dsm_f],
                            axis=1).astype(MXU_DT)
    gr["win"] = mm(x1, dproj, mode="tn", name=f"{tag}_dwin", tm=1024, tn=640, tk=512)
    dx1 = mm(dproj, w["win"], mode="nt", add=dz, beta=DN_ALPHA, name=f"{tag}_dx1", tm=1024, tn=1024, tk=640)
    tail = {} if tail is None else dict(tail, comm_dh=tail["comm_dh"](gr))
    dx0, gr["gu1"], gr["d1"], gr["ln_ffn1_g"], gr["ln_ffn1_b"], got_tail = _ffn_bwd(
        dx1, sv["ffn1"], w, 1, f"{tag}_ffn1", **tail)
    return dx0, gr, (got_sb, got_fox, got_gdn), got_tail


SMALL_REPLICATED = ("ln_ffn1_g", "ln_ffn1_b", "gdn_a_log", "gdn_dt_bias", "gdn_norm_g", "fox_b_f", "conf_dw_b",
                    "conf_norm_g", "conf_norm_b", "ln_mix_g", "ln_mix_b", "ln_mem_g", "ln_mem_b", "ln_ffn2_g",
                    "ln_ffn2_b")
SMALL_SHARDED = ("gdn_conv_w", "conf_dw_w")
BIG = ("ffn1_w_gate", "ffn1_w_up", "ffn1_w_down", "w_in", "w_out", "mem_w_q", "mem_w_kv", "mem_w_o",
       "ffn2_w_gate", "ffn2_w_up", "ffn2_w_down")
WEIGHT_ORDER = ("ffn1_w_gate", "ffn1_w_up", "ffn1_w_down", "ln_ffn1_g", "ln_ffn1_b", "w_in", "gdn_conv_w", "gdn_a_log",
                "gdn_dt_bias", "gdn_norm_g", "fox_b_f", "conf_dw_w", "conf_dw_b", "conf_norm_g", "conf_norm_b", "w_out",
                "ln_mix_g", "ln_mix_b", "mem_w_q", "mem_w_kv", "mem_w_o", "ln_mem_g", "ln_mem_b", "ffn2_w_gate",
                "ffn2_w_up", "ffn2_w_down", "ln_ffn2_g", "ln_ffn2_b")


def _step(x, mem, loss_target, wts, ms, vs):
    me = 4 * lax.axis_index("x") + 2 * lax.axis_index("y") + lax.axis_index("c")
    x = x[0]
    mem = mem[0]
    target = loss_target[0]
    rows_s = D_MODEL // N_DEV
    first, rest = ("gu1", "d1", "win"), ("sq", "kv", "gu2", "d2")

    def shards(l):
        c = lambda k: wts[k][l].astype(MXU_DT)
        return dict(gu1=jnp.stack([c("ffn1_w_gate"), c("ffn1_w_up")]), d1=c("ffn1_w_down"),
                    win=_win_to_aligned(wts["w_in"][l]).astype(MXU_DT),
                    sq=jnp.stack([c("w_out"), c("mem_w_q"), c("mem_w_o")]), kv=c("mem_w_kv"),
                    gu2=jnp.stack([c("ffn2_w_gate"), c("ffn2_w_up")]), d2=c("ffn2_w_down"))

    def to_compute_layout(w, keys, got):
        for k, g in zip(keys, got):
            if k in ("gu1", "gu2"):
                w[k] = g.transpose(2, 1, 0, 3).reshape(D_MODEL, 2 * D_FF)
            elif k in ("d1", "d2"):
                w[k] = g.reshape(D_FF, D_MODEL)
            elif k == "win":
                w[k] = g.reshape(D_MODEL, P_WIDTH)
            elif k == "sq":
                full = g.transpose(1, 0, 2, 3).reshape(3, D_MODEL, D_MODEL)
                w["wout"], w["wq"], w["wo"] = full[0], full[1], full[2]
            else:
                w["wkv"] = g.transpose(1, 0, 2).reshape(D_MODEL, 2 * D_MODEL)

    def chunks(gr, keys, dtype=MXU_DT):
        out = []
        for k in keys:
            if k in ("gu1", "gu2"):
                out.append(gr[k].reshape(D_MODEL, 2, N_DEV, -1).transpose(2, 1, 0, 3))
            elif k in ("d1", "d2"):
                out.append(gr[k].reshape(N_DEV, -1, D_MODEL))
            elif k == "win":
                out.append(gr[k].reshape(N_DEV, rows_s, P_WIDTH))
            elif k == "sq":
                out.append(jnp.stack([gr[n].reshape(N_DEV, rows_s, D_MODEL) for n in ("wout", "wq", "wo")], axis=1))
            else:
                out.append(gr["wkv"].reshape(D_MODEL, N_DEV, -1).transpose(1, 0, 2))
        return [a.astype(dtype) for a in out]

    sh = [shards(l) for l in range(DEPTH)]
    sm_sh = _pack([wts["gdn_conv_w"], wts["conf_dw_w"]])
    got = exchange([sh[0]["gu1"], sm_sh], broadcast=True, name="gather_first")
    conv_shapes = [wts["gdn_conv_w"].shape, wts["conf_dw_w"].shape]
    parts = [_unpack(got[-1][j], conv_shapes) for j in range(N_DEV)]
    gconv_full = jnp.concatenate([p[0] for p in parts], axis=-1)
    cconv_full = jnp.concatenate([p[1] for p in parts], axis=-1)

    def small_weights(l):
        w = dict(gdn_conv_w=gconv_full[l], conf_dw_w=cconv_full[l],
                 alog=_row128(wts["gdn_a_log"][l], A_COL), dtb=_row128(wts["gdn_dt_bias"][l], A_COL),
                 bf=_row128(wts["fox_b_f"][l], FOX_COL), ng=jnp.tile(wts["gdn_norm_g"][l], GROUP_HEADS)[None, :])
        for k in ("ln_ffn1_g", "ln_ffn1_b", "conf_dw_b", "conf_norm_g", "conf_norm_b", "ln_mix_g", "ln_mix_b",
                  "ln_mem_g", "ln_mem_b", "ln_ffn2_g", "ln_ffn2_b"):
            w[k] = wts[k][l][None, :]
        return w

    lw = [small_weights(l) for l in range(DEPTH)]
    to_compute_layout(lw[0], ("gu1",), got[:-1])

    def take(l, keys):
        return lambda g: to_compute_layout(lw[l], keys, g)

    def take_fox0(g):
        to_compute_layout(lw[0], rest[2:], g[:2])
        to_compute_layout(lw[1], first[:2], g[2:])

    h, h16, sv0 = _layer_fwd(
        x, x.astype(MXU_DT), mem, lw[0], "l0",
        comm_ffn1=Comm([sh[0][k] for k in first[1:]], True), on_ffn1=take(0, first[1:]),
        comm_gdn=Comm([sh[0][k] for k in rest[:2]], True), on_gdn=take(0, rest[:2]),
        comm_fox=Comm([sh[0][k] for k in rest[2:]] + [sh[1][k] for k in first[:2]], True), on_fox=take_fox0,
        comm_sb=Comm([sh[1]["win"]], True), on_sb=take(1, ("win",)))
    h, _, sv1 = _layer_fwd(
        h, h16, mem, lw[1], "l1",
        comm_gdn=Comm([sh[1][k] for k in rest[:2]], True), on_gdn=take(1, rest[:2]),
        comm_fox=Comm([sh[1][k] for k in rest[2:]], True), on_fox=take(1, rest[2:]))
    dh, lpart = loss_head(h, target, name="loss_head")

    recv = [{}, {}]
    e_ffn, e_mem = ("gu2", "d2"), ("sq", "kv")
    dh, g1, got, _ = _layer_bwd(dh, mem, sv1, lw[1], "l1",
                                lambda gr: (None, Comm(chunks(gr, e_ffn), False), Comm(chunks(gr, e_mem), False)))
    recv[1].update(zip(e_ffn, got[1]))
    recv[1].update(zip(e_mem, got[2]))
    tail = dict(comm_dh=lambda gr: Comm(chunks(gr, ("win",)), False),
                comm_dwgu=lambda dwd: Comm(chunks({"d1": dwd}, ("d1",)), False),
                comm_dx=lambda dwgu: Comm(chunks({"gu1": dwgu}, ("gu1",)), False))
    dh, g0, got, got_t = _layer_bwd(
        dh, mem, sv0, lw[0], "l0",
        lambda gr: (Comm(chunks(gr, e_mem), False), Comm(chunks(g1, first[:2]), False),
                    Comm(chunks(g1, first[2:]) + chunks(gr, e_ffn), False)), tail)
    recv[0].update(zip(e_mem, got[0]))
    recv[1].update(zip(first[:2], got[1]))
    recv[1].update(win=got[2][0])
    recv[0].update(zip(e_ffn, got[2][1:]))
    recv[0].update(win=got_t[0][0], d1=got_t[1][0], gu1=got_t[2][0])
    grad_x = dh[None]
    grads = [g0, g1]

    def gl(k):
        return jnp.stack([grads[l][k] for l in range(DEPTH)])

    small_names = SMALL_REPLICATED + SMALL_SHARDED
    small_grads = [gl(k) for k in small_names] + [lpart[0, :1]]
    got = exchange([_pack(small_grads)], broadcast=True, name="gather_small_grads")
    sm_sum = slot_sum(got[0], name="sum_small_grads")
    sm_g = _unpack(sm_sum, [g.shape for g in small_grads])
    loss = sm_g[-1][0]
    small_g = dict(zip(small_names, sm_g[:-1]))
    for k in SMALL_SHARDED:
        width = wts[k].shape[-1]
        small_g[k] = lax.dynamic_slice_in_dim(small_g[k], me * width, width, axis=2)

    out_g, out_d, out_m, out_v = {}, {}, {}, {}

    def update(names, key, fix=lambda a: a):
        res = {k: [] for k in names}
        for l in range(DEPTH):
            slots = fix(recv[l][key])
            slots = slots.reshape(N_DEV, -1, slots.shape[-1])
            for i, k in enumerate(names):
                two = lambda a: a[l].reshape(-1, a.shape[-1])
                res[k].append(adamw(two(wts[k]), two(ms[k]), two(vs[k]), slots, row0=i * two(wts[k]).shape[0],
                                    name=f"adamw_{k}_l{l}"))
        for k in names:
            for dst, per_layer in zip((out_g, out_d, out_m, out_v), zip(*res[k])):
                dst[k] = jnp.stack(per_layer).reshape(wts[k].shape)

    update(("ffn1_w_gate", "ffn1_w_up"), "gu1")
    update(("ffn1_w_down",), "d1")
    update(("w_in",), "win", _win_from_aligned)
    update(("w_out", "mem_w_q", "mem_w_o"), "sq")
    update(("mem_w_kv",), "kv")
    update(("ffn2_w_gate", "ffn2_w_up"), "gu2")
    update(("ffn2_w_down",), "d2")

    sw = _pack([wts[k] for k in small_names])
    smm = _pack([ms[k] for k in small_names])
    smv = _pack([vs[k] for k in small_names])
    sg = _pack([small_g[k] for k in small_names])
    res = adamw(sw, smm, smv, sg[None], name="adamw_small")
    shapes = [wts[k].shape for k in small_names]
    for dst, buf in zip((out_g, out_d, out_m, out_v), res):
        for k, a in zip(small_names, _unpack(buf, shapes)):
            dst[k] = a

    return (loss, grad_x, *[out_g[k] for k in WEIGHT_ORDER], *[out_d[k] for k in WEIGHT_ORDER],
            *[out_m[k] for k in WEIGHT_ORDER], *[out_v[k] for k in WEIGHT_ORDER])


def kernel(x, mem, ffn1_w_gate, ffn1_w_up, ffn1_w_down, ln_ffn1_g, ln_ffn1_b, w_in, gdn_conv_w, gdn_a_log, gdn_dt_bias, gdn_norm_g, fox_b_f, conf_dw_w, conf_dw_b, conf_norm_g, conf_norm_b, w_out, ln_mix_g, ln_mix_b, mem_w_q, mem_w_kv, mem_w_o, ln_mem_g, ln_mem_b, ffn2_w_gate, ffn2_w_up, ffn2_w_down, ln_ffn2_g, ln_ffn2_b, loss_target, m_ffn1_w_gate, m_ffn1_w_up, m_ffn1_w_down, m_ln_ffn1_g, m_ln_ffn1_b, m_w_in, m_gdn_conv_w, m_gdn_a_log, m_gdn_dt_bias, m_gdn_norm_g, m_fox_b_f, m_conf_dw_w, m_conf_dw_b, m_conf_norm_g, m_conf_norm_b, m_w_out, m_ln_mix_g, m_ln_mix_b, m_mem_w_q, m_mem_w_kv, m_mem_w_o, m_ln_mem_g, m_ln_mem_b, m_ffn2_w_gate, m_ffn2_w_up, m_ffn2_w_down, m_ln_ffn2_g, m_ln_ffn2_b, v_ffn1_w_gate, v_ffn1_w_up, v_ffn1_w_down, v_ln_ffn1_g, v_ln_ffn1_b, v_w_in, v_gdn_conv_w, v_gdn_a_log, v_gdn_dt_bias, v_gdn_norm_g, v_fox_b_f, v_conf_dw_w, v_conf_dw_b, v_conf_norm_g, v_conf_norm_b, v_w_out, v_ln_mix_g, v_ln_mix_b, v_mem_w_q, v_mem_w_kv, v_mem_w_o, v_ln_mem_g, v_ln_mem_b, v_ffn2_w_gate, v_ffn2_w_up, v_ffn2_w_down, v_ln_ffn2_g, v_ln_ffn2_b):
    args = locals()
    wts = {k: args[k] for k in WEIGHT_ORDER}
    ms = {k: args["m_" + k] for k in WEIGHT_ORDER}
    vs = {k: args["v_" + k] for k in WEIGHT_ORDER}
    return _step(x, mem, loss_target, wts, ms, vs)
```

```python
import functools
import math

import jax
import jax.numpy as jnp
import numpy as np
from jax import lax
from jax.experimental import pallas as pl
from jax.experimental.pallas import tpu as pltpu

F32 = jnp.float32
BF16 = jnp.bfloat16
MXU_DT = jnp.bfloat16
HI = lax.Precision.HIGHEST

N_DEV = 8
VMEM_LIMIT_BYTES = 56 * 1024 * 1024
LANES = 128

D_MODEL = 1024
DEPTH = 2
GROUP_WIDTH = 256
HEAD_DIM = 64
GROUP_HEADS = 4
D_FF = 2816
SHORT_CONV = 4
CONF_KERNEL = 31
CONF_GROUPS = 4
GDN_CHUNK = 64
N_MEM = 256
MEM_HEADS = 4
MEM_HEAD_DIM = 256
DN_ALPHA = float((2 * DEPTH) ** 0.25)
LN_EPS = 1e-5
RMS_EPS = 1e-6
L2_EPS = 1e-6
NEG_BIG = -1e30
IN_SPLITS = (768, 256, 4, 4, 768, 4, 512, 768)
IN_WIDTH = sum(IN_SPLITS)
P_GDN, P_Z, P_FOX, P_CONF, P_SB, P_SMALL = 0, 768, 1024, 1792, 2304, 3072
P_WIDTH = 3200

ADAM_LR = 0.001
ADAM_B1 = 0.9
ADAM_B2 = 0.999
ADAM_EPS = 1e-08
ADAM_WD = 0.01
ADAM_STEP = 10


def _cparams(sem):
    return pltpu.CompilerParams(dimension_semantics=sem, vmem_limit_bytes=VMEM_LIMIT_BYTES)


def _tile(n, pref, align=LANES):
    if n <= pref:
        return n
    t = (pref // align) * align
    while t >= align:
        if n % t == 0:
            return t
        t -= align
    return n


def mm(a, b, *, mode="nn", add=None, alpha=1.0, beta=1.0, out_dtype=F32, name,
       tm=1024, tn=512, tk=1024, comm=None):
    if mode == "nn":
        (m, k), (k2, n) = a.shape, b.shape
    elif mode == "nt":
        (m, k), (n, k2) = a.shape, b.shape
    else:
        (k, m), (k2, n) = a.shape, b.shape
    assert k == k2, (a.shape, b.shape, mode)
    tm = _tile(m, tm, 8 if mode != "tn" else LANES)
    tn = _tile(n, tn)
    tk = _tile(k, tk, LANES if mode != "tn" else 8)
    nk = k // tk
    if mode == "nn":
        a_spec = pl.BlockSpec((tm, tk), lambda i, j, kk: (i, kk))
        b_spec = pl.BlockSpec((tk, tn), lambda i, j, kk: (kk, j))
        dims = (((1,), (0,)), ((), ()))
    elif mode == "nt":
        a_spec = pl.BlockSpec((tm, tk), lambda i, j, kk: (i, kk))
        b_spec = pl.BlockSpec((tn, tk), lambda i, j, kk: (j, kk))
        dims = (((1,), (1,)), ((), ()))
    else:
        a_spec = pl.BlockSpec((tk, tm), lambda i, j, kk: (kk, i))
        b_spec = pl.BlockSpec((tk, tn), lambda i, j, kk: (kk, j))
        dims = (((0,), (0,)), ((), ()))
    o_spec = pl.BlockSpec((tm, tn), lambda i, j, kk: (i, j))
    has_add = add is not None

    def body(*refs):
        if has_add:
            a_ref, b_ref, add_ref, o_ref, acc_ref = refs
        else:
            a_ref, b_ref, o_ref, acc_ref = refs
        kk = pl.program_id(2)

        @pl.when(kk == 0)
        def _():
            acc_ref[...] = jnp.zeros_like(acc_ref)

        acc_ref[...] += lax.dot_general(a_ref[...].astype(MXU_DT), b_ref[...].astype(MXU_DT), dims,
                                        preferred_element_type=F32)

        @pl.when(kk == nk - 1)
        def _():
            r = acc_ref[...]
            if alpha != 1.0:
                r = r * alpha
            if has_add:
                r = r + beta * add_ref[...].astype(F32)
            o_ref[...] = r.astype(out_dtype)

    in_specs = [a_spec, b_spec] + ([o_spec] if has_add else [])
    args = (a, b) + ((add,) if has_add else ())
    grid = (m // tm, n // tn, nk)
    call = dict(name=name, grid=grid, in_specs=in_specs, out_specs=[o_spec],
                out_shape=[jax.ShapeDtypeStruct((m, n), out_dtype)],
                scratch_shapes=[pltpu.VMEM((tm, tn), F32)],
                compiler_params=_cparams(("parallel", "parallel", "arbitrary")))
    (out,), got = carry_comm(call, body, args, comm, 1, *_grid_ends(*grid))
    return out if comm is None else (out, got)


def ln_res_fwd(x, y, g, b, s, *, name):
    t, d = x.shape
    tm = _tile(t, 512, 8)

    def body(x_ref, y_ref, g_ref, b_ref, o_ref, o16_ref, xh_ref, rs_ref):
        z = DN_ALPHA * x_ref[...] + s * y_ref[...]
        mu = jnp.mean(z, axis=-1, keepdims=True)
        zc = z - mu
        var = jnp.mean(zc * zc, axis=-1, keepdims=True)
        rstd = lax.rsqrt(var + LN_EPS)
        xh = zc * rstd
        xh_ref[...] = xh
        rs_ref[...] = jnp.broadcast_to(rstd, rs_ref.shape)
        out = xh * g_ref[...] + b_ref[...]
        o_ref[...] = out
        o16_ref[...] = out.astype(o16_ref.dtype)

    row = pl.BlockSpec((tm, d), lambda i: (i, 0))
    vec = pl.BlockSpec((1, d), lambda i: (0, 0))
    return pl.pallas_call(
        body, name=name, grid=(t // tm,),
        in_specs=[row, row, vec, vec],
        out_specs=[row, row, row, pl.BlockSpec((tm, LANES), lambda i: (i, 0))],
        out_shape=[jax.ShapeDtypeStruct((t, d), F32), jax.ShapeDtypeStruct((t, d), MXU_DT),
                   jax.ShapeDtypeStruct((t, d), F32), jax.ShapeDtypeStruct((t, LANES), F32)],
        compiler_params=_cparams(("parallel",)),
    )(x, y, g, b)


def ln_res_bwd(dout, xhat, rstd, g, *, name):
    t, d = dout.shape
    tm = _tile(t, 512, 8)

    def body(do_ref, xh_ref, rs_ref, g_ref, dz_ref, dg_ref, db_ref):
        i = pl.program_id(0)

        @pl.when(i == 0)
        def _():
            dg_ref[...] = jnp.zeros_like(dg_ref)
            db_ref[...] = jnp.zeros_like(db_ref)

        do = do_ref[...]
        xh = xh_ref[...]
        dxh = do * g_ref[...]
        m1 = jnp.mean(dxh, axis=-1, keepdims=True)
        m2 = jnp.mean(dxh * xh, axis=-1, keepdims=True)
        dz_ref[...] = rs_ref[:, 0:1] * (dxh - m1 - xh * m2)
        dg_ref[...] += jnp.sum(do * xh, axis=0, keepdims=True)
        db_ref[...] += jnp.sum(do, axis=0, keepdims=True)

    row = pl.BlockSpec((tm, d), lambda i: (i, 0))
    vec = pl.BlockSpec((1, d), lambda i: (0, 0))
    return pl.pallas_call(
        body, name=name, grid=(t // tm,),
        in_specs=[row, row, pl.BlockSpec((tm, LANES), lambda i: (i, 0)), vec],
        out_specs=[row, vec, vec],
        out_shape=[jax.ShapeDtypeStruct((t, d), F32), jax.ShapeDtypeStruct((1, d), F32),
                   jax.ShapeDtypeStruct((1, d), F32)],
        compiler_params=_cparams(("arbitrary",)),
    )(dout, xhat, rstd, g)


def _sigmoid(x):
    return 1.0 / (1.0 + jnp.exp(-x))


def act_fwd(gu, *, name):
    t, f2 = gu.shape
    f = f2 // 2
    tm = _tile(t, 256, 8)

    def body(gu_ref, h_ref):
        g = gu_ref[:, :f].astype(F32)
        h_ref[...] = (g * _sigmoid(g) * gu_ref[:, f:].astype(F32)).astype(h_ref.dtype)

    return pl.pallas_call(
        body, name=name, grid=(t // tm,),
        in_specs=[pl.BlockSpec((tm, f2), lambda i: (i, 0))],
        out_specs=pl.BlockSpec((tm, f), lambda i: (i, 0)),
        out_shape=jax.ShapeDtypeStruct((t, f), MXU_DT),
        compiler_params=_cparams(("parallel",)),
    )(gu)


def act_bwd(gu, dh, *, name):
    t, f2 = gu.shape
    f = f2 // 2
    tm = _tile(t, 256, 8)

    def body(gu_ref, dh_ref, o_ref):
        g = gu_ref[:, :f].astype(F32)
        u = gu_ref[:, f:].astype(F32)
        dh = dh_ref[...]
        sg = _sigmoid(g)
        o_ref[:, f:] = (dh * g * sg).astype(o_ref.dtype)
        o_ref[:, :f] = (dh * u * sg * (1.0 + g * (1.0 - sg))).astype(o_ref.dtype)

    return pl.pallas_call(
        body, name=name, grid=(t // tm,),
        in_specs=[pl.BlockSpec((tm, f2), lambda i: (i, 0)), pl.BlockSpec((tm, f), lambda i: (i, 0))],
        out_specs=pl.BlockSpec((tm, f2), lambda i: (i, 0)),
        out_shape=jax.ShapeDtypeStruct((t, f2), MXU_DT),
        compiler_params=_cparams(("parallel",)),
    )(gu, dh)


def loss_head(y, target, *, name):
    t, d = y.shape
    tm = _tile(t, 512, 8)

    def body(y_ref, t_ref, dy_ref, l_ref):
        i = pl.program_id(0)

        @pl.when(i == 0)
        def _():
            l_ref[...] = jnp.zeros_like(l_ref)

        err = y_ref[...] - t_ref[...]
        dy_ref[...] = err * (1.0 / d)
        part = jnp.sum(jnp.sum(err * err, axis=-1, keepdims=True), axis=0, keepdims=True)
        l_ref[...] += jnp.broadcast_to(part * (0.5 / d), l_ref.shape)

    row = pl.BlockSpec((tm, d), lambda i: (i, 0))
    return pl.pallas_call(
        body, name=name, grid=(t // tm,),
        in_specs=[row, row],
        out_specs=[row, pl.BlockSpec((1, LANES), lambda i: (0, 0))],
        out_shape=[jax.ShapeDtypeStruct((t, d), F32), jax.ShapeDtypeStruct((1, LANES), F32)],
        compiler_params=_cparams(("arbitrary",)),
    )(y, target)


def _dot(a, b):
    return lax.dot_general(a, b, (((1,), (0,)), ((), ())), preferred_element_type=F32)


def _dot_nt(a, b):
    return lax.dot_general(a, b, (((1,), (1,)), ((), ())), preferred_element_type=F32)


def _dot_tn(a, b):
    return lax.dot_general(a, b, (((0,), (0,)), ((), ())), preferred_element_type=F32)


def _dot_hi(a, b):
    return lax.dot_general(a, b, (((1,), (0,)), ((), ())), preferred_element_type=F32, precision=HI)


def _dot_nt_hi(a, b):
    return lax.dot_general(a, b, (((1,), (1,)), ((), ())), preferred_element_type=F32, precision=HI)


def _split_dot(x, u):
    hi = x.astype(MXU_DT)
    lo = (x - hi.astype(F32)).astype(MXU_DT)
    return _dot(hi, u) + _dot(lo, u)


def _mem_probs(q_ref, kv_ref, h):
    lo = h * MEM_HEAD_DIM
    qh = q_ref[:, lo:lo + MEM_HEAD_DIM].astype(MXU_DT)
    kh = kv_ref[:, lo:lo + MEM_HEAD_DIM].astype(MXU_DT)
    s = _dot_nt(qh, kh) * (MEM_HEAD_DIM ** -0.5)
    s = s - jnp.max(s, axis=-1, keepdims=True)
    p = jnp.exp(s)
    return p / jnp.sum(p, axis=-1, keepdims=True), qh, kh


def memattn_fwd(q, kv, *, name):
    t, d = q.shape
    tm = _tile(t, 512, 8)

    def body(q_ref, kv_ref, o_ref):
        for h in range(MEM_HEADS):
            lo = h * MEM_HEAD_DIM
            p, _, _ = _mem_probs(q_ref, kv_ref, h)
            vh = kv_ref[:, d + lo:d + lo + MEM_HEAD_DIM].astype(MXU_DT)
            o_ref[:, lo:lo + MEM_HEAD_DIM] = _dot(p.astype(MXU_DT), vh).astype(o_ref.dtype)

    return pl.pallas_call(
        body, name=name, grid=(t // tm,),
        in_specs=[pl.BlockSpec((tm, d), lambda i: (i, 0)), pl.BlockSpec(kv.shape, lambda i: (0, 0))],
        out_specs=pl.BlockSpec((tm, d), lambda i: (i, 0)),
        out_shape=jax.ShapeDtypeStruct((t, d), MXU_DT),
        compiler_params=_cparams(("parallel",)),
    )(q, kv)


def memattn_bwd(q, kv, datt, *, name):
    t, d = q.shape
    tm = _tile(t, 512, 8)
    scale = MEM_HEAD_DIM ** -0.5

    def body(q_ref, kv_ref, da_ref, dq_ref, dkv_ref):
        @pl.when(pl.program_id(0) == 0)
        def _():
            dkv_ref[...] = jnp.zeros_like(dkv_ref)

        for h in range(MEM_HEADS):
            lo = h * MEM_HEAD_DIM
            p, qh, kh = _mem_probs(q_ref, kv_ref, h)
            vh = kv_ref[:, d + lo:d + lo + MEM_HEAD_DIM].astype(MXU_DT)
            da = da_ref[:, lo:lo + MEM_HEAD_DIM].astype(MXU_DT)
            dp = _dot_nt(da, vh)
            ds = p * (dp - jnp.sum(dp * p, axis=-1, keepdims=True))
            dsb = ds.astype(MXU_DT)
            dq_ref[:, lo:lo + MEM_HEAD_DIM] = (_dot(dsb, kh) * scale).astype(dq_ref.dtype)
            dkv_ref[:, lo:lo + MEM_HEAD_DIM] += _dot_tn(dsb, qh) * scale
            dkv_ref[:, d + lo:d + lo + MEM_HEAD_DIM] += _dot_tn(p.astype(MXU_DT), da)

    row = pl.BlockSpec((tm, d), lambda i: (i, 0))
    full = pl.BlockSpec(kv.shape, lambda i: (0, 0))
    return pl.pallas_call(
        body, name=name, grid=(t // tm,),
        in_specs=[row, full, row],
        out_specs=[row, full],
        out_shape=[jax.ShapeDtypeStruct((t, d), MXU_DT), jax.ShapeDtypeStruct(kv.shape, F32)],
        compiler_params=_cparams(("arbitrary",)),
    )(q, kv, datt)


def _halo(k):
    return 8 * ((k - 1 + 7) // 8)


def dwconv_fwd(u, w, bias, *, col0=0, width=None, name):
    t = u.shape[0]
    kk, c = w.shape
    width = c if width is None else width
    assert width == c and col0 % c == 0
    cb = col0 // c
    hb = _halo(kk)
    tm = _tile(t, 512, hb)
    r = tm // hb
    has_bias = bias is not None

    def body(*refs):
        if has_bias:
            prev_ref, cur_ref, w_ref, b_ref, o_ref, scr = refs
        else:
            prev_ref, cur_ref, w_ref, o_ref, scr = refs
        i = pl.program_id(0)
        scr[0:hb, :] = jnp.where(i == 0, 0.0, prev_ref[...])
        scr[hb:hb + tm, :] = cur_ref[...]
        acc = jnp.zeros((tm, c), F32)
        for k in range(kk):
            acc = acc + w_ref[k:k + 1, :] * scr[pl.ds(hb - (kk - 1) + k, tm), :]
        if has_bias:
            acc = acc + b_ref[...]
        o_ref[...] = acc

    in_specs = [pl.BlockSpec((hb, c), lambda i: (jnp.maximum(i * r - 1, 0), cb)),
                pl.BlockSpec((tm, c), lambda i: (i, cb)),
                pl.BlockSpec((kk, c), lambda i: (0, 0))]
    args = [u, u, w]
    if has_bias:
        in_specs.append(pl.BlockSpec((1, c), lambda i: (0, 0)))
        args.append(bias)
    return pl.pallas_call(
        body, name=name, grid=(t // tm,),
        in_specs=in_specs,
        out_specs=pl.BlockSpec((tm, c), lambda i: (i, 0)),
        out_shape=jax.ShapeDtypeStruct((t, c), F32),
        scratch_shapes=[pltpu.VMEM((hb + tm, c), F32)],
        compiler_params=_cparams(("parallel",)),
    )(*args)


def dwconv_bwd(dc, u, w, *, col0=0, name):
    t, c = dc.shape
    kk = w.shape[0]
    assert col0 % c == 0
    cb = col0 // c
    hb = _halo(kk)
    tm = _tile(t, 512, hb)
    r = tm // hb
    n = t // tm

    def body(dcur_ref, dnext_ref, uprev_ref, ucur_ref, w_ref, du_ref, dw_ref, db_ref, sd, su):
        i = pl.program_id(0)

        @pl.when(i == 0)
        def _():
            dw_ref[...] = jnp.zeros_like(dw_ref)
            db_ref[...] = jnp.zeros_like(db_ref)

        dcur = dcur_ref[...]
        sd[0:tm, :] = dcur
        sd[tm:tm + hb, :] = jnp.where(i == n - 1, 0.0, dnext_ref[...])
        su[0:hb, :] = jnp.where(i == 0, 0.0, uprev_ref[...])
        su[hb:hb + tm, :] = ucur_ref[...]
        acc = jnp.zeros((tm, c), F32)
        for k in range(kk):
            acc = acc + w_ref[k:k + 1, :] * sd[pl.ds(kk - 1 - k, tm), :]
            dw_ref[k:k + 1, :] += jnp.sum(dcur * su[pl.ds(hb - (kk - 1) + k, tm), :], axis=0, keepdims=True)
        du_ref[...] = acc
        db_ref[...] += jnp.sum(dcur, axis=0, keepdims=True)

    return pl.pallas_call(
        body, name=name, grid=(n,),
        in_specs=[pl.BlockSpec((tm, c), lambda i: (i, 0)),
                  pl.BlockSpec((hb, c), lambda i: (jnp.minimum((i + 1) * r, n * r - 1), 0)),
                  pl.BlockSpec((hb, c), lambda i: (jnp.maximum(i * r - 1, 0), cb)),
                  pl.BlockSpec((tm, c), lambda i: (i, cb)),
                  pl.BlockSpec((kk, c), lambda i: (0, 0))],
        out_specs=[pl.BlockSpec((tm, c), lambda i: (i, 0)),
                   pl.BlockSpec((kk, c), lambda i: (0, 0)),
                   pl.BlockSpec((1, c), lambda i: (0, 0))],
        out_shape=[jax.ShapeDtypeStruct((t, c), F32), jax.ShapeDtypeStruct((kk, c), F32),
                   jax.ShapeDtypeStruct((1, c), F32)],
        scratch_shapes=[pltpu.VMEM((tm + hb, c), F32), pltpu.VMEM((hb + tm, c), F32)],
        compiler_params=_cparams(("arbitrary",)),
    )(dc, dc, u, u, w)


def glu_fwd(proj, *, name):
    t = proj.shape[0]
    c = GROUP_WIDTH
    tm = _tile(t, 1024, 8)
    vb, gb = P_CONF // c, P_CONF // c + 1

    def body(v_ref, g_ref, o_ref):
        o_ref[...] = v_ref[...] * _sigmoid(g_ref[...])

    return pl.pallas_call(
        body, name=name, grid=(t // tm,),
        in_specs=[pl.BlockSpec((tm, c), lambda i: (i, vb)), pl.BlockSpec((tm, c), lambda i: (i, gb))],
        out_specs=pl.BlockSpec((tm, c), lambda i: (i, 0)),
        out_shape=jax.ShapeDtypeStruct((t, c), F32),
        compiler_params=_cparams(("parallel",)),
    )(proj, proj)


def glu_bwd(proj, du, *, name):
    t = proj.shape[0]
    c = GROUP_WIDTH
    tm = _tile(t, 1024, 8)
    vb, gb = P_CONF // c, P_CONF // c + 1

    def body(v_ref, g_ref, du_ref, o_ref):
        sg = _sigmoid(g_ref[...])
        du = du_ref[...]
        o_ref[:, :c] = du * sg
        o_ref[:, c:] = du * v_ref[...] * sg * (1.0 - sg)

    return pl.pallas_call(
        body, name=name, grid=(t // tm,),
        in_specs=[pl.BlockSpec((tm, c), lambda i: (i, vb)), pl.BlockSpec((tm, c), lambda i: (i, gb)),
                  pl.BlockSpec((tm, c), lambda i: (i, 0))],
        out_specs=pl.BlockSpec((tm, 2 * c), lambda i: (i, 0)),
        out_shape=jax.ShapeDtypeStruct((t, 2 * c), F32),
        compiler_params=_cparams(("parallel",)),
    )(proj, proj, du)


def _group_mean_matrix(c, groups):
    gsz = c // groups
    ri = lax.broadcasted_iota(jnp.int32, (c, c), 0) // gsz
    ci = lax.broadcasted_iota(jnp.int32, (c, c), 1) // gsz
    return jnp.where(ri == ci, 1.0 / gsz, 0.0).astype(F32)


def gn_silu_fwd(cx, gamma, beta, *, name):
    t, c = cx.shape
    tm = _tile(t, 1024, 8)

    def body(c_ref, g_ref, b_ref, o_ref):
        gm = _group_mean_matrix(c, CONF_GROUPS)
        x = c_ref[...]
        mu = _dot_hi(x, gm)
        xc = x - mu
        var = _dot_hi(xc * xc, gm)
        a = xc * lax.rsqrt(var + LN_EPS) * g_ref[...] + b_ref[...]
        o_ref[...] = a * _sigmoid(a)

    row = pl.BlockSpec((tm, c), lambda i: (i, 0))
    vec = pl.BlockSpec((1, c), lambda i: (0, 0))
    return pl.pallas_call(
        body, name=name, grid=(t // tm,),
        in_specs=[row, vec, vec], out_specs=row,
        out_shape=jax.ShapeDtypeStruct((t, c), F32),
        compiler_params=_cparams(("parallel",)),
    )(cx, gamma, beta)


def gn_silu_bwd(cx, gamma, beta, dy, *, name):
    t, c = cx.shape
    tm = _tile(t, 1024, 8)

    def body(c_ref, g_ref, b_ref, dy_ref, dc_ref, dg_ref, db_ref):
        @pl.when(pl.program_id(0) == 0)
        def _():
            dg_ref[...] = jnp.zeros_like(dg_ref)
            db_ref[...] = jnp.zeros_like(db_ref)

        gm = _group_mean_matrix(c, CONF_GROUPS)
        x = c_ref[...]
        mu = _dot_hi(x, gm)
        xc = x - mu
        var = _dot_hi(xc * xc, gm)
        rstd = lax.rsqrt(var + LN_EPS)
        nrm = xc * rstd
        a = nrm * g_ref[...] + b_ref[...]
        sa = _sigmoid(a)
        da = dy_ref[...] * sa * (1.0 + a * (1.0 - sa))
        dg_ref[...] += jnp.sum(da * nrm, axis=0, keepdims=True)
        db_ref[...] += jnp.sum(da, axis=0, keepdims=True)
        dn = da * g_ref[...]
        dc_ref[...] = rstd * (dn - _dot_hi(dn, gm) - nrm * _dot_hi(dn * nrm, gm))

    row = pl.BlockSpec((tm, c), lambda i: (i, 0))
    vec = pl.BlockSpec((1, c), lambda i: (0, 0))
    return pl.pallas_call(
        body, name=name, grid=(t // tm,),
        in_specs=[row, vec, vec, row], out_specs=[row, vec, vec],
        out_shape=[jax.ShapeDtypeStruct((t, c), F32), jax.ShapeDtypeStruct((1, c), F32),
                   jax.ShapeDtypeStruct((1, c), F32)],
        compiler_params=_cparams(("arbitrary",)),
    )(cx, gamma, beta, dy)


FOX_COL = 8
SMALL_BLK = P_SMALL // LANES


def _log_sigmoid(x):
    return jnp.minimum(x, 0.0) - jnp.log(1.0 + jnp.exp(-jnp.abs(x)))


def _fox_cols(shape):
    col = lax.broadcasted_iota(jnp.int32, shape, 1)
    return (col >= FOX_COL) & (col < FOX_COL + GROUP_HEADS)


def fox_gate_fwd(proj, bvec, *, name):
    t = proj.shape[0]
    tm = _tile(t, 256, 8)

    def body(s_ref, b_ref, o_ref, carry):
        @pl.when(pl.program_id(0) == 0)
        def _():
            carry[...] = jnp.zeros_like(carry)

        lf = jnp.where(_fox_cols((tm, LANES)), _log_sigmoid(s_ref[...] + b_ref[...]), 0.0)
        ri = lax.broadcasted_iota(jnp.int32, (tm, tm), 0)
        ci = lax.broadcasted_iota(jnp.int32, (tm, tm), 1)
        cum = _dot_hi(jnp.where(ri >= ci, 1.0, 0.0).astype(F32), lf) + carry[...]
        o_ref[...] = cum
        carry[...] = cum[tm - 1:tm, :]

    return pl.pallas_call(
        body, name=name, grid=(t // tm,),
        in_specs=[pl.BlockSpec((tm, LANES), lambda i: (i, SMALL_BLK)), pl.BlockSpec((1, LANES), lambda i: (0, 0))],
        out_specs=pl.BlockSpec((tm, LANES), lambda i: (i, 0)),
        out_shape=jax.ShapeDtypeStruct((t, LANES), F32),
        scratch_shapes=[pltpu.VMEM((1, LANES), F32)],
        compiler_params=_cparams(("arbitrary",)),
    )(proj, bvec)


def fox_gate_bwd(dcum, proj, bvec, *, name):
    t = proj.shape[0]
    tm = _tile(t, 256, 8)
    n = t // tm

    def body(d_ref, s_ref, b_ref, o_ref, db_ref, carry):
        @pl.when(pl.program_id(0) == 0)
        def _():
            carry[...] = jnp.zeros_like(carry)
            db_ref[...] = jnp.zeros_like(db_ref)

        ri = lax.broadcasted_iota(jnp.int32, (tm, tm), 0)
        ci = lax.broadcasted_iota(jnp.int32, (tm, tm), 1)
        dlf = _dot_hi(jnp.where(ri <= ci, 1.0, 0.0).astype(F32), d_ref[...]) + carry[...]
        carry[...] = dlf[0:1, :]
        x = s_ref[...] + b_ref[...]
        dx = jnp.where(_fox_cols((tm, LANES)), dlf * (1.0 - _sigmoid(x)), 0.0)
        o_ref[...] = dx
        db_ref[...] += jnp.sum(dx, axis=0, keepdims=True)

    return pl.pallas_call(
        body, name=name, grid=(n,),
        in_specs=[pl.BlockSpec((tm, LANES), lambda i: (n - 1 - i, 0)),
                  pl.BlockSpec((tm, LANES), lambda i: (n - 1 - i, SMALL_BLK)),
                  pl.BlockSpec((1, LANES), lambda i: (0, 0))],
        out_specs=[pl.BlockSpec((tm, LANES), lambda i: (n - 1 - i, 0)), pl.BlockSpec((1, LANES), lambda i: (0, 0))],
        out_shape=[jax.ShapeDtypeStruct((t, LANES), F32), jax.ShapeDtypeStruct((1, LANES), F32)],
        scratch_shapes=[pltpu.VMEM((1, LANES), F32)],
        compiler_params=_cparams(("arbitrary",)),
    )(dcum, proj, bvec)


def _head_masks(c):
    lane_head = lax.broadcasted_iota(jnp.int32, (1, c), 1) // HEAD_DIM
    return [lane_head == h for h in range(GROUP_HEADS)]


def _attn_tiles(t, tq, tk):
    tq = _tile(t, tq, 8)
    tk = _tile(t, tk, LANES)
    return tq, tk, t // tq, t // tk


def _grid_ends(*sizes):
    first = lambda: functools.reduce(lambda a, b: a & b, [pl.program_id(d) == 0 for d in range(len(sizes))])
    last = lambda: functools.reduce(lambda a, b: a & b, [pl.program_id(d) == s - 1 for d, s in enumerate(sizes)])
    return first, last


EXP_DEAD = -110.0


def _key_norm_max(k_ref, nk, tk, masks):
    lane = lax.broadcasted_iota(jnp.int32, (1, LANES), 1)

    def one(jt, km):
        kb = k_ref[pl.ds(pl.multiple_of(jt * tk, tk), tk), :].astype(MXU_DT).astype(F32)
        sq = kb * kb
        for h in range(GROUP_HEADS):
            top = jnp.max(jnp.sum(jnp.where(masks[h], sq, 0.0), axis=-1, keepdims=True))
            km = jnp.where(lane == h, jnp.maximum(km, top), km)
        return km

    return lax.fori_loop(0, nk, one, jnp.zeros((1, LANES), F32))


def _fox_reach(qh, km, cc_ref):
    out = []
    for h in range(GROUP_HEADS):
        qf = qh[h].astype(F32)
        qn = jnp.sqrt(jnp.sum(qf * qf, axis=-1, keepdims=True))
        out.append(1.001 * qn * jnp.sqrt(km[:, h:h + 1]) + cc_ref[:, FOX_COL + h:FOX_COL + h + 1])
    return out


def _fox_alive(reach, top, cr_ref, j, tk):
    ends = cr_ref[jnp.maximum(j, 0)][:, tk - 1:tk]
    worst = jnp.float32(NEG_BIG)
    for h in range(GROUP_HEADS):
        worst = jnp.maximum(worst, jnp.max(reach[h] - top[h]) - jnp.max(ends[h:h + 1, :]))
    return (worst > EXP_DEAD).astype(jnp.int32)


def fox_fwd(proj, cum, cum_t, *, name, tq=512, tk=512, comm=None):
    t = proj.shape[0]
    c = GROUP_WIDTH
    tq, tk, nq, nk = _attn_tiles(t, tq, tk)
    assert tq == tk
    qb = P_FOX // c
    scale = HEAD_DIM ** -0.5
    cr3 = cum_t.reshape(8, nk, tk).transpose(1, 0, 2)

    def body(q_ref, k_ref, v_ref, cc_ref, cr_ref, o_ref, lse_ref, m_scr, l_scr, acc_scr, km_scr):
        i = pl.program_id(0)
        masks = _head_masks(c)

        @pl.when(i == 0)
        def _():
            km_scr[...] = _key_norm_max(k_ref, nk, tk, masks)

        q = q_ref[...] * scale
        qh = [jnp.where(masks[h], q, 0.0).astype(MXU_DT) for h in range(GROUP_HEADS)]
        cc = [cc_ref[:, FOX_COL + h:FOX_COL + h + 1] for h in range(GROUP_HEADS)]
        reach = _fox_reach(qh, km_scr[...], cc_ref)
        m_scr[...] = jnp.full_like(m_scr, NEG_BIG)
        l_scr[...] = jnp.zeros_like(l_scr)
        acc_scr[...] = jnp.zeros_like(acc_scr)

        def tile(j, diagonal):
            rows = pl.ds(pl.multiple_of(j * tk, tk), tk)
            kb = k_ref[rows, :].astype(MXU_DT)
            vb = v_ref[rows, :].astype(MXU_DT)
            crj = cr_ref[j]
            if diagonal:
                causal = (lax.broadcasted_iota(jnp.int32, (tq, tk), 1) <= lax.broadcasted_iota(jnp.int32, (tq, tk), 0))
            acc = acc_scr[...]
            for h in range(GROUP_HEADS):
                u = _dot_nt(qh[h], kb) - crj[h:h + 1, :]
                if diagonal:
                    u = jnp.where(causal, u, NEG_BIG)
                m_old = m_scr[h]
                m_new = jnp.maximum(m_old, jnp.max(u, axis=-1, keepdims=True) + cc[h])
                p = jnp.exp(u - (m_new - cc[h]))
                alpha = jnp.exp(m_old - m_new)
                l_scr[h] = alpha * l_scr[h] + jnp.sum(p, axis=-1, keepdims=True)
                m_scr[h] = m_new
                acc = jnp.where(masks[h], alpha * acc + _dot(p.astype(MXU_DT), vb), acc)
            acc_scr[...] = acc

        def alive(j):
            return _fox_alive(reach, [m_scr[h] for h in range(GROUP_HEADS)], cr_ref, j, tk)

        def step(state):
            j = i - state[0]
            tile(j, False)
            return state[0] + 1, alive(j - 1)

        tile(i, True)
        lax.while_loop(lambda s: (s[0] <= i) & (s[1] > 0), step, (jnp.int32(1), alive(i - 1)))
        acc = acc_scr[...]
        o = jnp.zeros_like(acc)
        lse = jnp.zeros((tq, LANES), F32)
        lane = lax.broadcasted_iota(jnp.int32, (1, LANES), 1)
        for h in range(GROUP_HEADS):
            o = jnp.where(masks[h], acc / l_scr[h], o)
            lse = jnp.where(lane == h, m_scr[h] + jnp.log(l_scr[h]), lse)
        o_ref[...] = o
        lse_ref[...] = lse

    resident = lambda blk: pl.BlockSpec((t, c), lambda i: (0, blk), pipeline_mode=pl.Buffered(1))
    call = dict(
        name=name, grid=(nq,),
        in_specs=[pl.BlockSpec((tq, c), lambda i: (i, qb)), resident(qb + 1), resident(qb + 2),
                  pl.BlockSpec((tq, LANES), lambda i: (i, 0)),
                  pl.BlockSpec((nk, 8, tk), lambda i: (0, 0, 0), pipeline_mode=pl.Buffered(1))],
        out_specs=[pl.BlockSpec((tq, c), lambda i: (i, 0)), pl.BlockSpec((tq, LANES), lambda i: (i, 0))],
        out_shape=[jax.ShapeDtypeStruct((t, c), F32), jax.ShapeDtypeStruct((t, LANES), F32)],
        scratch_shapes=[pltpu.VMEM((GROUP_HEADS, tq, 1), F32), pltpu.VMEM((GROUP_HEADS, tq, 1), F32),
                        pltpu.VMEM((tq, c), F32), pltpu.VMEM((1, LANES), F32)],
        compiler_params=_cparams(("arbitrary",)),
    )
    outs, got = carry_comm(call, body, (proj, proj, proj, cum, cr3), comm, 2, *_grid_ends(nq))
    return (*outs, got)


def fox_bwd(proj, cum, cum_t, o, lse, do, *, name, tq=512, tk=512, comm=None):
    t = proj.shape[0]
    c = GROUP_WIDTH
    tq, tk, nq, nk = _attn_tiles(t, tq, tk)
    assert tq == tk
    qb = P_FOX // c
    scale = HEAD_DIM ** -0.5
    cr3 = cum_t.reshape(8, nk, tk).transpose(1, 0, 2)

    def body(q_ref, k_ref, v_ref, cc_ref, cr_ref, o_ref, lse_ref, do_ref,
             dq_ref, dk_hbm, dv_hbm, dcc_ref, dcr_ref, dq_scr, rs_scr, dk_scr, dv_scr, km_scr):
        i = pl.program_id(0)
        masks = _head_masks(c)

        @pl.when(i == 0)
        def _():
            dk_scr[...] = jnp.zeros_like(dk_scr)
            dv_scr[...] = jnp.zeros_like(dv_scr)
            dcr_ref[...] = jnp.zeros_like(dcr_ref)
            km_scr[...] = _key_norm_max(k_ref, nk, tk, masks)

        q = q_ref[...]
        qf = q.astype(MXU_DT)
        qh = [jnp.where(masks[h], q * scale, 0.0).astype(MXU_DT) for h in range(GROUP_HEADS)]
        do = do_ref[...]
        dob = do.astype(MXU_DT)
        doh = [jnp.where(masks[h], do, 0.0).astype(MXU_DT) for h in range(GROUP_HEADS)]
        doo = do * o_ref[...]
        delta = [jnp.sum(jnp.where(masks[h], doo, 0.0), axis=-1, keepdims=True) for h in range(GROUP_HEADS)]
        lse_h = [lse_ref[:, h:h + 1] for h in range(GROUP_HEADS)]
        off = [lse_h[h] - cc_ref[:, FOX_COL + h:FOX_COL + h + 1] for h in range(GROUP_HEADS)]
        reach = _fox_reach(qh, km_scr[...], cc_ref)
        dq_scr[...] = jnp.zeros_like(dq_scr)
        rs_scr[...] = jnp.zeros_like(rs_scr)

        def tile(j, diagonal):
            rows = pl.ds(pl.multiple_of(j * tk, tk), tk)
            kb = k_ref[rows, :].astype(MXU_DT)
            vb = v_ref[rows, :].astype(MXU_DT)
            crj = cr_ref[j]
            if diagonal:
                causal = (lax.broadcasted_iota(jnp.int32, (tq, tk), 1) <= lax.broadcasted_iota(jnp.int32, (tq, tk), 0))
            dq = dq_scr[...]
            dk_upd = jnp.zeros((tk, c), F32)
            dv_upd = jnp.zeros((tk, c), F32)
            for h in range(GROUP_HEADS):
                p = jnp.exp(_dot_nt(qh[h], kb) - crj[h:h + 1, :] - off[h])
                if diagonal:
                    p = jnp.where(causal, p, 0.0)
                ds = p * (_dot_nt(doh[h], vb) - delta[h])
                dsb = ds.astype(MXU_DT)
                dq = jnp.where(masks[h], dq + _dot(dsb, kb) * scale, dq)
                dk_upd = jnp.where(masks[h], _dot_tn(dsb, qf) * scale, dk_upd)
                dv_upd = jnp.where(masks[h], _dot_tn(p.astype(MXU_DT), dob), dv_upd)
                dcr_ref[j, h:h + 1, :] += -jnp.sum(ds, axis=0, keepdims=True)
                rs_scr[h] += jnp.sum(ds, axis=-1, keepdims=True)
            dq_scr[...] = dq
            dk_scr[rows, :] += dk_upd
            dv_scr[rows, :] += dv_upd

        def alive(j):
            return _fox_alive(reach, lse_h, cr_ref, j, tk)

        def step(state):
            j = i - state[0]
            tile(j, False)
            return state[0] + 1, alive(j - 1)

        tile(i, True)
        lax.while_loop(lambda s: (s[0] <= i) & (s[1] > 0), step, (jnp.int32(1), alive(i - 1)))
        dq_ref[...] = dq_scr[...]
        lane = lax.broadcasted_iota(jnp.int32, (1, LANES), 1)
        dcc = jnp.zeros((tq, LANES), F32)
        for h in range(GROUP_HEADS):
            dcc = jnp.where(lane == FOX_COL + h, rs_scr[h], dcc)
        dcc_ref[...] = dcc

        @pl.when(i == nq - 1)
        def _():
            pltpu.sync_copy(dk_scr, dk_hbm)
            pltpu.sync_copy(dv_scr, dv_hbm)

    qrow = lambda i: (i, 0)
    resident = lambda blk: pl.BlockSpec((t, c), lambda i: (0, blk), pipeline_mode=pl.Buffered(1))
    hbm = pl.BlockSpec(memory_space=pl.ANY)
    call = dict(
        name=name, grid=(nq,),
        in_specs=[pl.BlockSpec((tq, c), lambda i: (i, qb)), resident(qb + 1), resident(qb + 2),
                  pl.BlockSpec((tq, LANES), qrow),
                  pl.BlockSpec((nk, 8, tk), lambda i: (0, 0, 0), pipeline_mode=pl.Buffered(1)),
                  pl.BlockSpec((tq, c), qrow), pl.BlockSpec((tq, LANES), qrow), pl.BlockSpec((tq, c), qrow)],
        out_specs=[pl.BlockSpec((tq, c), qrow), hbm, hbm, pl.BlockSpec((tq, LANES), qrow),
                   pl.BlockSpec((nk, 8, tk), lambda i: (0, 0, 0))],
        out_shape=[jax.ShapeDtypeStruct((t, c), F32), jax.ShapeDtypeStruct((t, c), F32),
                   jax.ShapeDtypeStruct((t, c), F32), jax.ShapeDtypeStruct((t, LANES), F32),
                   jax.ShapeDtypeStruct((nk, 8, tk), F32)],
        scratch_shapes=[pltpu.VMEM((tq, c), F32), pltpu.VMEM((GROUP_HEADS, tq, 1), F32),
                        pltpu.VMEM((t, c), F32), pltpu.VMEM((t, c), F32), pltpu.VMEM((1, LANES), F32)],
        compiler_params=_cparams(("arbitrary",)),
    )
    outs, got = carry_comm(call, body, (proj, proj, proj, cum, cr3, o, lse, do), comm, 5, *_grid_ends(nq))
    return (*outs, got)


SB_DEAD = -110.0


def _sb_logs(z, strict):
    tt = jnp.log(1.0 + jnp.exp(-jnp.abs(z)))
    log_keep = jnp.where(strict, -(jnp.maximum(z, 0.0) + tt), 0.0)
    log_beta = jnp.minimum(z, 0.0) - tt
    return log_keep, log_beta


def _tri(n, upper):
    a = lax.broadcasted_iota(jnp.int32, (n, n), 0)
    b = lax.broadcasted_iota(jnp.int32, (n, n), 1)
    return jnp.where((a < b) if upper else (a > b), 1.0, 0.0).astype(MXU_DT)


def _sb_carry_lane(jj, h):
    return GROUP_HEADS * jj + h


def sb_fwd(proj, *, name, tq=256, tk=256, comm=None):
    t = proj.shape[0]
    c = GROUP_WIDTH
    tq, tk, nq, nk = _attn_tiles(t, tq, tk)
    assert nk * GROUP_HEADS <= LANES
    qb = P_SB // c
    scale = HEAD_DIM ** -0.5

    def body(q_ref, k_ref, v_ref, o_ref, rs_ref, r_scr, acc_scr):
        i = pl.program_id(0)
        last = ((i + 1) * tq - 1) // tk
        masks = _head_masks(c)
        q = q_ref[...]
        qh = [jnp.where(masks[h], q, 0.0).astype(MXU_DT) for h in range(GROUP_HEADS)]
        lane = lax.broadcasted_iota(jnp.int32, (1, LANES), 1)
        later = _tri(tk, upper=False)
        r_scr[...] = jnp.zeros_like(r_scr)
        acc_scr[...] = jnp.zeros_like(acc_scr)
        rs_ref[...] = jnp.full((tq, LANES), 2.0 * SB_DEAD, F32)

        def step(state):
            jj, _ = state
            j = last - jj
            rows = pl.ds(pl.multiple_of(j * tk, tk), tk)
            kb = k_ref[rows, :].astype(MXU_DT)
            vb = v_ref[rows, :].astype(MXU_DT)
            row = i * tq + lax.broadcasted_iota(jnp.int32, (tq, tk), 0)
            col = j * tk + lax.broadcasted_iota(jnp.int32, (tq, tk), 1)
            strict = col < row
            acc = acc_scr[...]
            rs = rs_ref[...]
            for h in range(GROUP_HEADS):
                z = _dot_nt(qh[h], kb) * scale
                log_keep, log_beta = _sb_logs(z, strict)
                r_old = r_scr[h]
                rs = jnp.where(lane == _sb_carry_lane(jj, h), r_old, rs)
                rest = r_old + _split_dot(log_keep, later)
                w = jnp.where(strict, jnp.exp(log_beta + rest), 0.0)
                acc = jnp.where(masks[h], acc + _dot(w.astype(MXU_DT), vb), acc)
                r_scr[h] = r_old + jnp.sum(log_keep, axis=-1, keepdims=True)
            acc_scr[...] = acc
            rs_ref[...] = rs
            return jj + 1, jnp.max(r_scr[...])

        lax.while_loop(lambda s: (s[0] <= last) & (s[1] > SB_DEAD), step, (jnp.int32(0), jnp.float32(0.0)))
        o_ref[...] = acc_scr[...]

    resident = lambda blk: pl.BlockSpec((t, c), lambda i: (0, blk), pipeline_mode=pl.Buffered(1))
    call = dict(
        name=name, grid=(nq,),
        in_specs=[pl.BlockSpec((tq, c), lambda i: (i, qb)), resident(qb + 1), resident(qb + 2)],
        out_specs=[pl.BlockSpec((tq, c), lambda i: (i, 0)), pl.BlockSpec((tq, LANES), lambda i: (i, 0))],
        out_shape=[jax.ShapeDtypeStruct((t, c), F32), jax.ShapeDtypeStruct((t, LANES), F32)],
        scratch_shapes=[pltpu.VMEM((GROUP_HEADS, tq, 1), F32), pltpu.VMEM((tq, c), F32)],
        compiler_params=_cparams(("arbitrary",)),
    )
    outs, got = carry_comm(call, body, (proj, proj, proj), comm, 2, *_grid_ends(nq))
    return (*outs, got)


def sb_bwd(proj, rsave, do, *, name, tq=256, tk=256, comm=None):
    t = proj.shape[0]
    c = GROUP_WIDTH
    tq, tk, nq, nk = _attn_tiles(t, tq, tk)
    qb = P_SB // c
    scale = HEAD_DIM ** -0.5

    def body(q_ref, k_ref, v_ref, rs_ref, do_ref, dq_ref, dk_hbm, dv_hbm, e_scr, dq_scr, dk_scr, dv_scr):
        i = pl.program_id(0)
        last = ((i + 1) * tq - 1) // tk
        masks = _head_masks(c)

        @pl.when(i == 0)
        def _():
            dk_scr[...] = jnp.zeros_like(dk_scr)
            dv_scr[...] = jnp.zeros_like(dv_scr)

        e_scr[...] = jnp.zeros_like(e_scr)
        dq_scr[...] = jnp.zeros_like(dq_scr)
        q = q_ref[...]
        qf = q.astype(MXU_DT)
        qh = [jnp.where(masks[h], q, 0.0).astype(MXU_DT) for h in range(GROUP_HEADS)]
        do = do_ref[...]
        dob = do.astype(MXU_DT)
        doh = [jnp.where(masks[h], do, 0.0).astype(MXU_DT) for h in range(GROUP_HEADS)]
        later = _tri(tk, upper=False)
        earlier = _tri(tk, upper=True)
        rs = rs_ref[...]
        lane = lax.broadcasted_iota(jnp.int32, (1, LANES), 1)
        visited = jnp.where(jnp.max(rs, axis=0, keepdims=True) > SB_DEAD, (lane // GROUP_HEADS + 1).astype(F32), 0.0)
        n_visited = jnp.minimum(jnp.max(visited).astype(jnp.int32), last + 1)

        def step(it, carry):
            jj = n_visited - 1 - it
            j = last - jj
            rows = pl.ds(pl.multiple_of(j * tk, tk), tk)
            kb = k_ref[rows, :].astype(MXU_DT)
            vb = v_ref[rows, :].astype(MXU_DT)
            row = i * tq + lax.broadcasted_iota(jnp.int32, (tq, tk), 0)
            col = j * tk + lax.broadcasted_iota(jnp.int32, (tq, tk), 1)
            strict = col < row
            dq = dq_scr[...]
            dk_upd = jnp.zeros((tk, c), F32)
            dv_upd = jnp.zeros((tk, c), F32)
            for h in range(GROUP_HEADS):
                z = _dot_nt(qh[h], kb) * scale
                log_keep, log_beta = _sb_logs(z, strict)
                r_h = jnp.sum(jnp.where(lane == _sb_carry_lane(jj, h), rs, 0.0), axis=-1, keepdims=True)
                rest = r_h + _split_dot(log_keep, later)
                w = jnp.where(strict, jnp.exp(log_beta + rest), 0.0)
                e = w * _dot_nt(doh[h], vb)
                e_old = e_scr[h]
                dkeep = e_old + _split_dot(e, earlier)
                dz = jnp.where(strict, e * jnp.exp(log_keep) - dkeep * jnp.exp(log_beta), 0.0)
                dzb = dz.astype(MXU_DT)
                dq = jnp.where(masks[h], dq + _dot(dzb, kb) * scale, dq)
                dk_upd = jnp.where(masks[h], _dot_tn(dzb, qf) * scale, dk_upd)
                dv_upd = jnp.where(masks[h], _dot_tn(w.astype(MXU_DT), dob), dv_upd)
                e_scr[h] = e_old + jnp.sum(e, axis=-1, keepdims=True)
            dq_scr[...] = dq
            dk_scr[rows, :] += dk_upd
            dv_scr[rows, :] += dv_upd
            return carry

        lax.fori_loop(0, n_visited, step, 0)
        dq_ref[...] = dq_scr[...]

        @pl.when(i == nq - 1)
        def _():
            pltpu.sync_copy(dk_scr, dk_hbm)
            pltpu.sync_copy(dv_scr, dv_hbm)

    qrow = lambda i: (i, 0)
    resident = lambda blk: pl.BlockSpec((t, c), lambda i: (0, blk), pipeline_mode=pl.Buffered(1))
    hbm = pl.BlockSpec(memory_space=pl.ANY)
    call = dict(
        name=name, grid=(nq,),
        in_specs=[pl.BlockSpec((tq, c), lambda i: (i, qb)), resident(qb + 1), resident(qb + 2),
                  pl.BlockSpec((tq, LANES), qrow), pl.BlockSpec((tq, c), qrow)],
        out_specs=[pl.BlockSpec((tq, c), qrow), hbm, hbm],
        out_shape=[jax.ShapeDtypeStruct((t, c), F32)] * 3,
        scratch_shapes=[pltpu.VMEM((GROUP_HEADS, tq, 1), F32), pltpu.VMEM((tq, c), F32),
                        pltpu.VMEM((t, c), F32), pltpu.VMEM((t, c), F32)],
        compiler_params=_cparams(("arbitrary",)),
    )
    outs, got = carry_comm(call, body, (proj, proj, proj, rsave, do), comm, 3, *_grid_ends(nq))
    return (*outs, got)


A_COL, B_COL = 0, 4
Z_BLK = P_Z // GROUP_WIDTH
GDN_CHUNKS_PER_STEP = 4


NN = (((1,), (0,)), ((), ()))
NT = (((1,), (1,)), ((), ()))
TN = (((0,), (0,)), ((), ()))


def _terms(x, n):
    out, rem = [], x
    for _ in range(n):
        t = rem.astype(MXU_DT)
        out.append(t)
        rem = rem - t.astype(F32)
    return out


def _dotp(a, b, dims, a_terms=2, b_terms=2):
    at, bt = _terms(a, a_terms), _terms(b, b_terms)
    out = None
    for i, x in enumerate(at):
        for j, y in enumerate(bt):
            if i + j < max(a_terms, b_terms):
                r = lax.dot_general(x, y, dims, preferred_element_type=F32)
                out = r if out is None else out + r
    return out


def _silu(x):
    return x * _sigmoid(x)


def _dsilu(x):
    s = _sigmoid(x)
    return s * (1.0 + x * (1.0 - s))


def _head_sum(x, masks):
    out = jnp.zeros_like(x)
    for m in masks:
        out = jnp.where(m, jnp.sum(jnp.where(m, x, 0.0), axis=-1, keepdims=True), out)
    return out


def _expand(cols, col0, masks):
    out = jnp.zeros((cols.shape[0], GROUP_WIDTH), F32)
    for h, m in enumerate(masks):
        out = jnp.where(m, cols[:, col0 + h:col0 + h + 1], out)
    return out


def _reduce(x, col0, masks):
    lane = lax.broadcasted_iota(jnp.int32, (1, LANES), 1)
    out = jnp.zeros((x.shape[0], LANES), F32)
    for h, m in enumerate(masks):
        out = jnp.where(lane == col0 + h, jnp.sum(jnp.where(m, x, 0.0), axis=-1, keepdims=True), out)
    return out


def _block_ones():
    ri = lax.broadcasted_iota(jnp.int32, (GROUP_WIDTH, GROUP_WIDTH), 0) // HEAD_DIM
    ci = lax.broadcasted_iota(jnp.int32, (GROUP_WIDTH, GROUP_WIDTH), 1) // HEAD_DIM
    return jnp.where(ri == ci, 1.0, 0.0).astype(F32)


def _blk(x, hs):
    return jnp.concatenate([x] * GROUP_HEADS, axis=0) * hs


def _unblk(m, hs):
    mm = m * hs
    c = GDN_CHUNK
    return mm[0:c] + mm[c:2 * c] + mm[2 * c:3 * c] + mm[3 * c:4 * c]


def _row_mask4():
    ri = lax.broadcasted_iota(jnp.int32, (GROUP_WIDTH, LANES), 0) // HEAD_DIM
    ci = lax.broadcasted_iota(jnp.int32, (GROUP_WIDTH, LANES), 1)
    return jnp.where(ri + A_COL == ci, 1.0, 0.0).astype(F32)


def _lockstep(gens):
    results = [None] * len(gens)
    live = list(range(len(gens)))
    while live:
        for i in list(live):
            try:
                next(gens[i])
            except StopIteration as stop:
                results[i] = stop.value
                live.remove(i)
    return results


def _gdn_chunk(xc, small, avec, dtvec, state, masks, hs):
    (f,) = _lockstep([_gdn_local(xc, small, avec, dtvec, masks, hs)])
    return _gdn_recur(f, state, hs)


def _gdn_local(xc, small, avec, dtvec, masks, hs):
    c = GDN_CHUNK
    w = GROUP_WIDTH
    b16 = lambda v: v.astype(MXU_DT)
    f = {}
    xq, xk, xv = xc[:, :w], xc[:, w:2 * w], xc[:, 2 * w:]
    qs, ks, v = _silu(xq), _silu(xk), _silu(xv)
    rq = lax.rsqrt(_head_sum(qs * qs, masks) + L2_EPS)
    rk = lax.rsqrt(_head_sum(ks * ks, masks) + L2_EPS)
    qn = qs * rq
    k = ks * rk
    q = qn * (HEAD_DIM ** -0.5)
    xg = small + dtvec
    sp = jnp.maximum(xg, 0.0) + jnp.log(1.0 + jnp.exp(-jnp.abs(xg)))
    g128 = -avec * sp
    beta128 = _sigmoid(small)
    ri = lax.broadcasted_iota(jnp.int32, (c, c), 0)
    ci = lax.broadcasted_iota(jnp.int32, (c, c), 1)
    tril = jnp.where(ri >= ci, 1.0, 0.0).astype(F32)
    gam128 = _dotp(tril, g128, NN, 1, 3)
    yield
    gam = _expand(gam128, A_COL, masks)
    bfull = _expand(beta128, B_COL, masks)
    mask4 = _row_mask4()
    ones = jnp.ones((c, LANES), F32)
    gam_row = _dotp(ones, jnp.concatenate([gam128] * GROUP_HEADS, axis=0) * mask4, NT, 1, 3)
    yield
    li = lax.broadcasted_iota(jnp.int32, (c, w), 0)
    lj = lax.broadcasted_iota(jnp.int32, (c, w), 1) % HEAD_DIM
    incl = li >= lj
    strict = li > lj
    dmat = jnp.exp(jnp.where(incl, gam - gam_row, NEG_BIG))
    egam = jnp.exp(gam)
    glast = gam[c - 1:c, :]
    ekd = jnp.exp(glast - gam)
    kb = k * bfull
    vb = v * bfull
    kbg = kb * egam
    qd = q * egam
    kd = k * ekd
    kblk = b16(_blk(k, hs))
    araw = _dot_nt(b16(kb), kblk)
    qk = _dot_nt(b16(q), kblk)
    yield
    a = jnp.where(strict, araw * dmat, 0.0)
    tm = jnp.where(li == lj, 1.0, 0.0) - a
    p = a
    for _ in range(5):
        p = _dotp(p, _blk(p, hs), NN)
        yield
        tm = tm + _dotp(tm, _blk(p, hs), NN)
        yield
    tm16 = b16(tm)
    u = _dot(tm16, b16(_blk(vb, hs)))
    wm = _dot(tm16, b16(_blk(kbg, hs)))
    aqk = jnp.where(incl, qk * dmat, 0.0)
    f.update(xq=xq, xk=xk, xv=xv, v=v, rq=rq, rk=rk, qn=qn, k=k, q=q, xg=xg, g128=g128, beta128=beta128,
             tril=tril, gam=gam, bfull=bfull, mask4=mask4, ones=ones, incl=incl, strict=strict, li=li,
             dmat=dmat, egam=egam, glast=glast, ekd=ekd, kb=kb, vb=vb, kbg=kbg, qd=qd, kd=kd, kblk=kblk,
             araw=araw, tm=tm, tm16=tm16, wm=wm, qk=qk, aqk=aqk, u=u)
    return f


def _gdn_recur(f, state, hs):
    b16 = lambda v: v.astype(MXU_DT)
    s16 = b16(state)
    vn = f["u"] - _dot(b16(f["wm"]), s16)
    o = _dot(b16(f["qd"]), s16) + _dot(b16(f["aqk"]), b16(_blk(vn, hs)))
    s_new = state * jnp.exp(f["glast"]) + hs * _dot_tn(b16(f["kd"]), b16(vn))
    f.update(s16=s16, vn=vn, o=o, s_new=s_new)
    return f


def _decay_rate(a_log):
    lane = lax.broadcasted_iota(jnp.int32, a_log.shape, 1)
    return jnp.where((lane >= A_COL) & (lane < A_COL + GROUP_HEADS), jnp.exp(a_log), 0.0)


def _gdn_post(o, z, ng, masks):
    r = lax.rsqrt(_head_sum(o * o, masks) * (1.0 / HEAD_DIM) + RMS_EPS)
    on = o * r
    return on, r, on * ng * _silu(z)


def gdn_fwd(cqkv, proj, avec, dtvec, ng, *, name, comm=None):
    t = cqkv.shape[0]
    c = GDN_CHUNK
    w = GROUP_WIDTH
    n = t // c

    def body(x_ref, z_ref, sm_ref, a_ref, dt_ref, ng_ref, y_ref, st_ref, s_scr):
        @pl.when(pl.program_id(0) == 0)
        def _():
            s_scr[...] = jnp.zeros_like(s_scr)

        masks = _head_masks(w)
        hs = _block_ones()
        avec_v = _decay_rate(a_ref[...])
        rows = [pl.ds(k * c, c) for k in range(sub)]
        fs = _lockstep([_gdn_local(x_ref[r, :], sm_ref[r, :], avec_v, dt_ref[...], masks, hs) for r in rows])
        state = s_scr[...]
        for k, r in enumerate(rows):
            st_ref[k] = state
            f = _gdn_recur(fs[k], state, hs)
            _, _, y = _gdn_post(f["o"], z_ref[r, :], ng_ref[...], masks)
            y_ref[r, :] = y
            state = f["s_new"]
        s_scr[...] = state

    sub = GDN_CHUNKS_PER_STEP if n % GDN_CHUNKS_PER_STEP == 0 else 1
    rows, steps = c * sub, n // sub
    vec = pl.BlockSpec((1, LANES), lambda i: (0, 0))
    call = dict(
        name=name, grid=(steps,),
        in_specs=[pl.BlockSpec((rows, 3 * w), lambda i: (i, 0)),
                  pl.BlockSpec((rows, w), lambda i: (i, Z_BLK)),
                  pl.BlockSpec((rows, LANES), lambda i: (i, SMALL_BLK)),
                  vec, vec, pl.BlockSpec((1, w), lambda i: (0, 0))],
        out_specs=[pl.BlockSpec((rows, w), lambda i: (i, 0)), pl.BlockSpec((sub, w, w), lambda i: (i, 0, 0))],
        out_shape=[jax.ShapeDtypeStruct((t, w), F32), jax.ShapeDtypeStruct((n, w, w), F32)],
        scratch_shapes=[pltpu.VMEM((w, w), F32)],
        compiler_params=_cparams(("arbitrary",)),
    )
    outs, got = carry_comm(call, body, (cqkv, proj, proj, avec, dtvec, ng), comm, 2, *_grid_ends(steps))
    return (*outs, got)


def gdn_bwd(cqkv, proj, avec, dtvec, ng, states, dy, *, name, comm=None):
    t = cqkv.shape[0]
    c = GDN_CHUNK
    w = GROUP_WIDTH
    n = t // c
    b16 = lambda v: v.astype(MXU_DT)

    def body(x_ref, z_ref, sm_ref, a_ref, dt_ref, ng_ref, st_ref, dy_ref,
             dx_ref, dz_ref, dsm_ref, dng_ref, dal_ref, ddt_ref, ds_scr):
        @pl.when(pl.program_id(0) == 0)
        def _():
            ds_scr[...] = jnp.zeros_like(ds_scr)
            dng_ref[...] = jnp.zeros_like(dng_ref)
            dal_ref[...] = jnp.zeros_like(dal_ref)
            ddt_ref[...] = jnp.zeros_like(ddt_ref)

        masks = _head_masks(w)
        hs = _block_ones()
        avec_v = _decay_rate(a_ref[...])
        rows = [pl.ds(k * c, c) for k in range(sub)]
        fs = _lockstep([_gdn_local(x_ref[r, :], sm_ref[r, :], avec_v, dt_ref[...], masks, hs) for r in rows])
        fs = [_gdn_recur(f, st_ref[k], hs) for k, f in enumerate(fs)]
        _lockstep([chunk(fs[k], st_ref[k], avec_v, masks, hs, z_ref.at[r, :], ng_ref, dy_ref.at[r, :], dx_ref.at[r, :],
                         dz_ref.at[r, :], dsm_ref.at[r, :], dng_ref, dal_ref, ddt_ref, ds_scr)
                   for k, r in reversed(list(enumerate(rows)))])

    def chunk(f, state, avec_v, masks, hs, z_ref, ng_ref, dy_ref, dx_ref, dz_ref, dsm_ref, dng_ref, dal_ref, ddt_ref,
              ds_scr):
        z = z_ref[...]
        ng_v = ng_ref[...]
        dy_v = dy_ref[...]
        on, r, _ = _gdn_post(f["o"], z, ng_v, masks)
        sz = _silu(z)
        dz_ref[...] = dy_v * on * ng_v * _dsilu(z)
        d_on = dy_v * ng_v * sz
        dng_ref[...] += jnp.sum(dy_v * on * sz, axis=0, keepdims=True)
        do = r * (d_on - on * _head_sum(d_on * on, masks) * (1.0 / HEAD_DIM))
        do16 = b16(do)
        dsn = ds_scr[...]
        dsn16 = b16(dsn)
        s16, vn, kd, qd, wm = f["s16"], f["vn"], f["kd"], f["qd"], f["wm"]
        k, q, kblk, tm, tm16 = f["k"], f["q"], f["kblk"], f["tm"], f["tm16"]
        dmat, egam, glast, gam = f["dmat"], f["egam"], f["glast"], f["gam"]
        incl, strict, li = f["incl"], f["strict"], f["li"]
        vn16 = b16(vn)
        dvn = _unblk(_dot_tn(b16(f["aqk"]), do16), hs) + _dot(b16(kd), dsn16)
        daqk = jnp.where(incl, _dot_nt(do16, b16(_blk(vn, hs))), 0.0)
        dqd = _dot_nt(do16, s16)
        dvn16 = b16(dvn)
        ds_scr[...] = hs * (_dot_tn(b16(qd), do16) - _dot_tn(b16(wm), dvn16)) + dsn * jnp.exp(glast)
        yield
        dkd = _dot_nt(vn16, dsn16)
        dglast = jnp.sum(dsn * state, axis=0, keepdims=True) * jnp.exp(glast)
        du16 = dvn16
        dw16 = b16(-_dot_nt(dvn16, s16))
        yield
        dqk16 = b16(daqk * dmat)
        ddm = daqk * f["qk"]
        dq = _dot(dqk16, kblk)
        dk = _unblk(_dot_tn(dqk16, b16(q)), hs)
        dtm = _dot_nt(du16, b16(_blk(f["vb"], hs))) + _dot_nt(dw16, b16(_blk(f["kbg"], hs)))
        dvb = _unblk(_dot_tn(tm16, du16), hs)
        dkbg = _unblk(_dot_tn(tm16, dw16), hs)
        yield
        xx = _unblk(_dotp(tm, dtm, TN), hs)
        yield
        da = jnp.where(strict, -_dotp(xx, _blk(tm, hs), NT), 0.0)
        yield
        daraw16 = b16(da * dmat)
        ddm = ddm + da * f["araw"]
        dkb = _dot(daraw16, kblk)
        dk = dk + _unblk(_dot_tn(daraw16, b16(f["kb"])), hs)
        yield
        tcol = ddm * dmat
        dgam = tcol
        dgam128_row = _dotp(-tcol, f["ones"], TN, 2, 1) * f["mask4"]
        yield
        dgam128_row = (dgam128_row[0:c] + dgam128_row[c:2 * c] + dgam128_row[2 * c:3 * c] + dgam128_row[3 * c:4 * c])
        dk = dk + dkd * f["ekd"]
        tt = dkd * kd
        dgam = dgam - tt
        dglast = dglast + jnp.sum(tt, axis=0, keepdims=True)
        dq = dq + dqd * egam
        dgam = dgam + dqd * qd
        dkb = dkb + dkbg * egam
        dgam = dgam + dkbg * f["kbg"]
        dk = dk + dkb * f["bfull"]
        dbf = dkb * k + dvb * f["v"]
        dv = dvb * f["bfull"]
        dgam = dgam + jnp.where(li == c - 1, dglast, 0.0)
        beta128 = f["beta128"]
        db128 = _reduce(dbf, B_COL, masks) * beta128 * (1.0 - beta128)
        dgam128 = _reduce(dgam, A_COL, masks) + dgam128_row
        dg128 = _dotp(f["tril"], dgam128, TN, 1, 2)
        yield
        dxg = dg128 * (-avec_v * _sigmoid(f["xg"]))
        lane = lax.broadcasted_iota(jnp.int32, (1, LANES), 1)
        dsm_ref[...] = jnp.where(lane < B_COL, dxg, db128)
        ddt_ref[...] += jnp.sum(dxg, axis=0, keepdims=True)
        dal_ref[...] += jnp.sum(dg128 * f["g128"], axis=0, keepdims=True)
        dqn = dq * (HEAD_DIM ** -0.5)
        dqs = f["rq"] * (dqn - f["qn"] * _head_sum(dqn * f["qn"], masks))
        dks = f["rk"] * (dk - k * _head_sum(dk * k, masks))
        dx_ref[:, :w] = dqs * _dsilu(f["xq"])
        dx_ref[:, w:2 * w] = dks * _dsilu(f["xk"])
        dx_ref[:, 2 * w:] = dv * _dsilu(f["xv"])

    sub = GDN_CHUNKS_PER_STEP if n % GDN_CHUNKS_PER_STEP == 0 else 1
    rows, steps = c * sub, n // sub
    vec = pl.BlockSpec((1, LANES), lambda i: (0, 0))
    rev = lambda blk: (lambda i: (steps - 1 - i, blk))
    call = dict(
        name=name, grid=(steps,),
        in_specs=[pl.BlockSpec((rows, 3 * w), rev(0)),
                  pl.BlockSpec((rows, w), rev(Z_BLK)),
                  pl.BlockSpec((rows, LANES), rev(SMALL_BLK)),
                  vec, vec, pl.BlockSpec((1, w), lambda i: (0, 0)),
                  pl.BlockSpec((sub, w, w), lambda i: (steps - 1 - i, 0, 0)),
                  pl.BlockSpec((rows, w), rev(0))],
        out_specs=[pl.BlockSpec((rows, 3 * w), rev(0)), pl.BlockSpec((rows, w), rev(0)),
                   pl.BlockSpec((rows, LANES), rev(0)),
                   pl.BlockSpec((1, w), lambda i: (0, 0)), vec, vec],
        out_shape=[jax.ShapeDtypeStruct((t, 3 * w), F32), jax.ShapeDtypeStruct((t, w), F32),
                   jax.ShapeDtypeStruct((t, LANES), F32), jax.ShapeDtypeStruct((1, w), F32),
                   jax.ShapeDtypeStruct((1, LANES), F32), jax.ShapeDtypeStruct((1, LANES), F32)],
        scratch_shapes=[pltpu.VMEM((w, w), F32)],
        compiler_params=_cparams(("arbitrary",)),
    )
    outs, got = carry_comm(call, body, (cqkv, proj, proj, avec, dtvec, ng, states, dy), comm, 6, *_grid_ends(steps))
    return (*outs, got)


def adamw(w, m, v, gslots, *, row0=0, name):
    r, c = w.shape
    s = gslots.shape[0]
    tr = _tile(r, 64, 8)
    assert row0 % tr == 0 and gslots.shape[2] == c
    rb = row0 // tr
    c1 = 1.0 - ADAM_B1 ** ADAM_STEP
    c2 = 1.0 - ADAM_B2 ** ADAM_STEP

    def body(w_ref, m_ref, v_ref, gs_ref, g_ref, d_ref, mo_ref, vo_ref):
        g = gs_ref[0].astype(F32)
        for k in range(1, s):
            g = g + gs_ref[k].astype(F32)
        m_new = ADAM_B1 * m_ref[...] + (1.0 - ADAM_B1) * g
        v_new = ADAM_B2 * v_ref[...] + (1.0 - ADAM_B2) * (g * g)
        m_hat = m_new / c1
        v_hat = v_new / c2
        g_ref[...] = g
        mo_ref[...] = m_new
        vo_ref[...] = v_new
        d_ref[...] = -ADAM_LR * (m_hat / (jnp.sqrt(v_hat) + ADAM_EPS) + ADAM_WD * w_ref[...])

    row = pl.BlockSpec((tr, c), lambda i: (i, 0))
    return pl.pallas_call(
        body, name=name, grid=(r // tr,),
        in_specs=[row, row, row, pl.BlockSpec((s, tr, c), lambda i: (0, rb + i, 0))],
        out_specs=[row] * 4,
        out_shape=[jax.ShapeDtypeStruct((r, c), F32)] * 4,
        compiler_params=_cparams(("parallel",)),
    )(w, m, v, gslots)


def slot_sum(slots, *, name):
    s, r, c = slots.shape

    def body(s_ref, o_ref):
        acc = s_ref[0]
        for k in range(1, s):
            acc = acc + s_ref[k]
        o_ref[...] = acc

    return pl.pallas_call(
        body, name=name, grid=(1,),
        in_specs=[pl.BlockSpec((s, r, c), lambda i: (0, 0, 0))],
        out_specs=pl.BlockSpec((r, c), lambda i: (0, 0)),
        out_shape=jax.ShapeDtypeStruct((r, c), F32),
        compiler_params=_cparams(("arbitrary",)),
    )(slots)


class Comm:
    def __init__(self, srcs, broadcast):
        self.srcs = list(srcs)
        self.broadcast = [broadcast] * len(self.srcs) if isinstance(broadcast, bool) else list(broadcast)
        self.n = len(self.srcs)
        self.out_shapes = [jax.ShapeDtypeStruct(((N_DEV,) + s.shape) if b else s.shape, s.dtype)
                           for s, b in zip(self.srcs, self.broadcast)]
        self.sems = [pltpu.SemaphoreType.DMA((self.n,))] * 3

    def _local(self, src_refs, out_refs, loc_sem, a, me):
        src = src_refs[a] if self.broadcast[a] else src_refs[a].at[me]
        return pltpu.make_async_copy(src, out_refs[a].at[me], loc_sem.at[a])

    def start(self, src_refs, out_refs, send_sem, recv_sem, loc_sem):
        x, y, c = lax.axis_index("x"), lax.axis_index("y"), lax.axis_index("c")
        me = 4 * x + 2 * y + c
        for a in range(self.n):
            self._local(src_refs, out_refs, loc_sem, a, me).start()
        for d in range(1, N_DEV):
            px, py, pc = x ^ ((d >> 2) & 1), y ^ ((d >> 1) & 1), c ^ (d & 1)
            peer = 4 * px + 2 * py + pc
            for a in range(self.n):
                src = src_refs[a] if self.broadcast[a] else src_refs[a].at[peer]
                pltpu.make_async_remote_copy(
                    src_ref=src, dst_ref=out_refs[a].at[me],
                    send_sem=send_sem.at[a], recv_sem=recv_sem.at[a],
                    device_id=(px, py, pc), device_id_type=pl.DeviceIdType.MESH).start()

    def wait(self, src_refs, out_refs, send_sem, recv_sem, loc_sem):
        x, y, c = lax.axis_index("x"), lax.axis_index("y"), lax.axis_index("c")
        me = 4 * x + 2 * y + c
        for a in range(self.n):
            seven = out_refs[a].at[pl.ds(0, N_DEV - 1)]
            pltpu.make_async_remote_copy(
                src_ref=seven, dst_ref=seven, send_sem=send_sem.at[a], recv_sem=recv_sem.at[a],
                device_id=(x, y, c), device_id_type=pl.DeviceIdType.MESH).wait()
            self._local(src_refs, out_refs, loc_sem, a, me).wait()


def exchange(srcs, *, broadcast, name):
    comm = Comm(srcs, broadcast)
    n = comm.n

    def body(*refs):
        src_refs, out_refs, sems = refs[:n], refs[n:2 * n], refs[2 * n:]
        comm.start(src_refs, out_refs, *sems)
        comm.wait(src_refs, out_refs, *sems)

    anyspec = pl.BlockSpec(memory_space=pl.ANY)
    return pl.pallas_call(
        body, name=name,
        in_specs=[anyspec] * n, out_specs=[anyspec] * n, out_shape=comm.out_shapes,
        scratch_shapes=comm.sems,
        compiler_params=pltpu.CompilerParams(has_side_effects=True),
    )(*srcs)


def carry_comm(call_kwargs, body, args, comm, n_out, is_first, is_last):
    if comm is None:
        return pl.pallas_call(body, **call_kwargs)(*args), []
    n_in, nc = len(args), comm.n
    n_scr = len(call_kwargs["scratch_shapes"])
    anyspec = pl.BlockSpec(memory_space=pl.ANY)

    def wrapped(*refs):
        ins, csrc = refs[:n_in], refs[n_in:n_in + nc]
        outs = refs[n_in + nc:n_in + nc + n_out]
        cout = refs[n_in + nc + n_out:n_in + 2 * nc + n_out]
        rest = refs[n_in + 2 * nc + n_out:]
        scr, sems = rest[:n_scr], rest[n_scr:]

        @pl.when(is_first())
        def _():
            comm.start(csrc, cout, *sems)

        body(*ins, *outs, *scr)

        @pl.when(is_last())
        def _():
            comm.wait(csrc, cout, *sems)

    kw = dict(call_kwargs)
    kw["in_specs"] = list(kw["in_specs"]) + [anyspec] * nc
    kw["out_specs"] = list(kw["out_specs"]) + [anyspec] * nc
    kw["out_shape"] = list(kw["out_shape"]) + comm.out_shapes
    kw["scratch_shapes"] = list(kw["scratch_shapes"]) + comm.sems
    cp = kw["compiler_params"]
    kw["compiler_params"] = pltpu.CompilerParams(dimension_semantics=cp.dimension_semantics,
                                                 vmem_limit_bytes=cp.vmem_limit_bytes, has_side_effects=True)
    res = pl.pallas_call(wrapped, **kw)(*args, *comm.srcs)
    return res[:n_out], res[n_out:]


def _pack(arrs):
    flat = []
    for a in arrs:
        f = a.reshape(-1).astype(F32)
        flat.append(jnp.pad(f, (0, (-f.shape[0]) % LANES)))
    buf = jnp.concatenate(flat)
    buf = jnp.pad(buf, (0, (-buf.shape[0]) % (8 * LANES)))
    return buf.reshape(-1, LANES)


def _unpack(buf, shapes):
    flat = buf.reshape(-1)
    out, off = [], 0
    for s in shapes:
        sz = int(np.prod(s))
        out.append(flat[off:off + sz].reshape(s))
        off += sz + (-sz) % LANES
    return out


def _win_to_aligned(w):
    o = np.cumsum((0,) + IN_SPLITS)
    seg = lambda i: w[..., o[i]:o[i + 1]]
    pad = jnp.zeros(w.shape[:-1] + (P_WIDTH - IN_WIDTH,), w.dtype)
    return jnp.concatenate([seg(0), seg(1), seg(4), seg(6), seg(7), seg(2), seg(3), seg(5), pad], axis=-1)


def _win_from_aligned(w):
    o = np.cumsum((0,) + IN_SPLITS)
    s = P_SMALL
    return jnp.concatenate([w[..., P_GDN:P_GDN + 768], w[..., P_Z:P_Z + 256], w[..., s:s + 4], w[..., s + 4:s + 8],
                            w[..., P_FOX:P_FOX + 768], w[..., s + 8:s + 12], w[..., P_CONF:P_CONF + 512],
                            w[..., P_SB:P_SB + 768]], axis=-1)


def _row128(vals, col0):
    return jnp.pad(vals.astype(F32)[None, :], ((0, 0), (col0, LANES - col0 - GROUP_HEADS)))


def _ffn_fwd(x, x16, w, n, tag, comm=None, on_comm=None):
    gu = mm(x16, w[f"gu{n}"], name=f"{tag}_gu", tm=1024, tn=512, tk=1024, out_dtype=MXU_DT, comm=comm)
    if comm is not None:
        gu, got = gu
        on_comm(got)
    h = act_fwd(gu, name=f"{tag}_act")
    y = mm(h, w[f"d{n}"], name=f"{tag}_down", tm=1024, tn=512, tk=D_FF)
    out, out16, xh, rs = ln_res_fwd(x, y, w[f"ln_ffn{n}_g"], w[f"ln_ffn{n}_b"], 0.5, name=f"{tag}_ln")
    return out, out16, (x16, gu, h, xh, rs)


def _ffn_bwd(dout, saved, w, n, tag, comm_dh=None, comm_dwgu=None, comm_dx=None):
    x, gu, h, xh, rs = saved
    wgu, wd = w[f"gu{n}"], w[f"d{n}"]
    got = [[], [], []]
    dz, dg, db = ln_res_bwd(dout, xh, rs, w[f"ln_ffn{n}_g"], name=f"{tag}_ln_bwd")
    dh = mm(dz, wd, mode="nt", alpha=0.5, name=f"{tag}_dh", tm=1024, tn=D_FF // 2, tk=1024, comm=comm_dh)
    if comm_dh is not None:
        dh, got[0] = dh
    dgu = act_bwd(gu, dh, name=f"{tag}_act_bwd")
    dwd = mm(h, dz, mode="tn", alpha=0.5, name=f"{tag}_dwd", tm=D_FF // 2, tn=1024, tk=512)
    c = comm_dwgu(dwd) if comm_dwgu is not None else None
    dwgu = mm(x, dgu, mode="tn", name=f"{tag}_dwgu", tm=1024, tn=D_FF // 2, tk=512, comm=c)
    if c is not None:
        dwgu, got[1] = dwgu
    c = comm_dx(dwgu) if comm_dx is not None else None
    dx = mm(dgu, wgu, mode="nt", add=dz, beta=DN_ALPHA, name=f"{tag}_dx", tm=1024, tn=1024, tk=D_FF // 2, comm=c)
    if c is not None:
        dx, got[2] = dx
    return dx, dwgu, dwd, dg, db, got


def _layer_fwd(x, x16, mem, w, tag, comm_ffn1=None, on_ffn1=None, comm_gdn=None, on_gdn=None, comm_fox=None,
               on_fox=None, comm_sb=None, on_sb=None):
    sv = {}
    x1, x1h, sv["ffn1"] = _ffn_fwd(x, x16, w, 1, f"{tag}_ffn1", comm=comm_ffn1, on_comm=on_ffn1)
    proj = mm(x1h, w["win"], name=f"{tag}_inproj", tm=1024, tn=640, tk=1024)
    cqkv = dwconv_fwd(proj, w["gdn_conv_w"], None, col0=P_GDN, name=f"{tag}_gdn_conv")
    ya, states, got = gdn_fwd(cqkv, proj, w["alog"], w["dtb"], w["ng"], name=f"{tag}_gdn", comm=comm_gdn)
    if on_gdn is not None:
        on_gdn(got)
    cum = fox_gate_fwd(proj, w["bf"], name=f"{tag}_fox_gate")
    cum_t = jnp.pad(cum[:, FOX_COL:FOX_COL + GROUP_HEADS].T, ((0, 8 - GROUP_HEADS), (0, 0)))
    yb, lse, got = fox_fwd(proj, cum, cum_t, name=f"{tag}_fox", comm=comm_fox)
    if on_fox is not None:
        on_fox(got)
    u = glu_fwd(proj, name=f"{tag}_glu")
    cc = dwconv_fwd(u, w["conf_dw_w"], w["conf_dw_b"], name=f"{tag}_conf_conv")
    yc = gn_silu_fwd(cc, w["conf_norm_g"], w["conf_norm_b"], name=f"{tag}_conf_norm")
    yd, rsave, got = sb_fwd(proj, name=f"{tag}_sb", comm=comm_sb)
    if on_sb is not None:
        on_sb(got)
    ycat = jnp.concatenate([ya, yb, yc, yd], axis=1).astype(MXU_DT)
    mix = mm(ycat, w["wout"], name=f"{tag}_outproj")
    x2, x2h, xh2, rs2 = ln_res_fwd(x1, mix, w["ln_mix_g"], w["ln_mix_b"], 1.0, name=f"{tag}_ln_mix")
    sv["mix"] = (x1h, proj, cqkv, states, cum, cum_t, yb, lse, u, cc, rsave, ycat, xh2, rs2)
    q = mm(x2h, w["wq"], name=f"{tag}_memq", out_dtype=MXU_DT)
    kv = mm(mem, w["wkv"], name=f"{tag}_memkv", tm=N_MEM, out_dtype=MXU_DT)
    att = memattn_fwd(q, kv, name=f"{tag}_memattn")
    mo = mm(att, w["wo"], name=f"{tag}_memo")
    x3, x3h, xh3, rs3 = ln_res_fwd(x2, mo, w["ln_mem_g"], w["ln_mem_b"], 1.0, name=f"{tag}_ln_mem")
    sv["mem"] = (x2h, q, kv, att, xh3, rs3)
    x4, x4h, sv["ffn2"] = _ffn_fwd(x3, x3h, w, 2, f"{tag}_ffn2")
    return x4, x4h, sv


def _layer_bwd(dx4, mem, sv, w, tag, plan, tail=None):
    t = dx4.shape[0]
    gr = {}
    dx3, gr["gu2"], gr["d2"], gr["ln_ffn2_g"], gr["ln_ffn2_b"], _ = _ffn_bwd(dx4, sv["ffn2"], w, 2, f"{tag}_ffn2")
    x2, q, kv, att, xh3, rs3 = sv["mem"]
    dz, gr["ln_mem_g"], gr["ln_mem_b"] = ln_res_bwd(dx3, xh3, rs3, w["ln_mem_g"], name=f"{tag}_ln_mem_bwd")
    datt = mm(dz, w["wo"], mode="nt", name=f"{tag}_datt", out_dtype=MXU_DT)
    gr["wo"] = mm(att, dz, mode="tn", name=f"{tag}_dwo", tk=512)
    dq, dkv = memattn_bwd(q, kv, datt, name=f"{tag}_memattn_bwd")
    gr["wq"] = mm(x2, dq, mode="tn", name=f"{tag}_dwq", tk=512)
    gr["wkv"] = mm(mem, dkv, mode="tn", name=f"{tag}_dwkv", tk=N_MEM)
    dx2 = mm(dq, w["wq"], mode="nt", add=dz, beta=DN_ALPHA, name=f"{tag}_dx2")
    x1, proj, cqkv, states, cum, cum_t, yb, lse, u, cc, rsave, ycat, xh2, rs2 = sv["mix"]
    dz, gr["ln_mix_g"], gr["ln_mix_b"] = ln_res_bwd(dx2, xh2, rs2, w["ln_mix_g"], name=f"{tag}_ln_mix_bwd")
    dycat = mm(dz, w["wout"], mode="nt", name=f"{tag}_dycat")
    gr["wout"] = mm(ycat, dz, mode="tn", name=f"{tag}_dwout", tk=512)
    comm_sb, comm_fox, comm_gdn = plan(gr)
    gw = GROUP_WIDTH
    dya, dyb, dyc, dyd = (dycat[:, i * gw:(i + 1) * gw] for i in range(4))
    dq_d, dk_d, dv_d, got_sb = sb_bwd(proj, rsave, dyd, name=f"{tag}_sb_bwd", comm=comm_sb)
    dcc, gr["conf_norm_g"], gr["conf_norm_b"] = gn_silu_bwd(cc, w["conf_norm_g"], w["conf_norm_b"], dyc,
                                                            name=f"{tag}_conf_norm_bwd")
    du, gr["conf_dw_w"], gr["conf_dw_b"] = dwconv_bwd(dcc, u, w["conf_dw_w"], name=f"{tag}_conf_conv_bwd")
    dglu = glu_bwd(proj, du, name=f"{tag}_glu_bwd")
    dq_b, dk_b, dv_b, dcc, dcr, got_fox = fox_bwd(proj, cum, cum_t, yb, lse, dyb, name=f"{tag}_fox_bwd", comm=comm_fox)
    dcum = dcc + jnp.pad(dcr[:, :GROUP_HEADS, :].transpose(0, 2, 1).reshape(t, GROUP_HEADS),
                   ((0, 0), (FOX_COL, LANES - FOX_COL - GROUP_HEADS)))
    dsm_f, dbf = fox_gate_bwd(dcum, proj, w["bf"], name=f"{tag}_fox_gate_bwd")
    gr["fox_b_f"] = dbf[0, FOX_COL:FOX_COL + GROUP_HEADS]
    dcq, dz_a, dsm_a, dng, dal, ddt, got_gdn = gdn_bwd(cqkv, proj, w["alog"], w["dtb"], w["ng"], states, dya,
                                                       name=f"{tag}_gdn_bwd", comm=comm_gdn)
    gr["gdn_norm_g"] = dng.reshape(GROUP_HEADS, HEAD_DIM).sum(0)
    gr["gdn_a_log"] = dal[0, A_COL:A_COL + GROUP_HEADS]
    gr["gdn_dt_bias"] = ddt[0, A_COL:A_COL + GROUP_HEADS]
    dgq, gr["gdn_conv_w"], _ = dwconv_bwd(dcq, proj, w["gdn_conv_w"], col0=P_GDN, name=f"{tag}_gdn_conv_bwd")
    dproj = jnp.concatenate([dgq, dz_a, dq_b, dk_b, dv_b, dglu, dq_d, dk_d, dv_d, dsm_a + dsm_f],
                            axis=1).astype(MXU_DT)
    gr["win"] = mm(x1, dproj, mode="tn", name=f"{tag}_dwin", tm=1024, tn=640, tk=512)
    dx1 = mm(dproj, w["win"], mode="nt", add=dz, beta=DN_ALPHA, name=f"{tag}_dx1", tm=1024, tn=1024, tk=640)
    tail = {} if tail is None else dict(tail, comm_dh=tail["comm_dh"](gr))
    dx0, gr["gu1"], gr["d1"], gr["ln_ffn1_g"], gr["ln_ffn1_b"], got_tail = _ffn_bwd(
        dx1, sv["ffn1"], w, 1, f"{tag}_ffn1", **tail)
    return dx0, gr, (got_sb, got_fox, got_gdn), got_tail


SMALL_REPLICATED = ("ln_ffn1_g", "ln_ffn1_b", "gdn_a_log", "gdn_dt_bias", "gdn_norm_g", "fox_b_f", "conf_dw_b",
                    "conf_norm_g", "conf_norm_b", "ln_mix_g", "ln_mix_b", "ln_mem_g", "ln_mem_b", "ln_ffn2_g",
                    "ln_ffn2_b")
SMALL_SHARDED = ("gdn_conv_w", "conf_dw_w")
BIG = ("ffn1_w_gate", "ffn1_w_up", "ffn1_w_down", "w_in", "w_out", "mem_w_q", "mem_w_kv", "mem_w_o",
       "ffn2_w_gate", "ffn2_w_up", "ffn2_w_down")
WEIGHT_ORDER = ("ffn1_w_gate", "ffn1_w_up", "ffn1_w_down", "ln_ffn1_g", "ln_ffn1_b", "w_in", "gdn_conv_w", "gdn_a_log",
                "gdn_dt_bias", "gdn_norm_g", "fox_b_f", "conf_dw_w", "conf_dw_b", "conf_norm_g", "conf_norm_b", "w_out",
                "ln_mix_g", "ln_mix_b", "mem_w_q", "mem_w_kv", "mem_w_o", "ln_mem_g", "ln_mem_b", "ffn2_w_gate",
                "ffn2_w_up", "ffn2_w_down", "ln_ffn2_g", "ln_ffn2_b")


def _step(x, mem, loss_target, wts, ms, vs):
    me = 4 * lax.axis_index("x") + 2 * lax.axis_index("y") + lax.axis_index("c")
    x = x[0]
    mem = mem[0]
    target = loss_target[0]
    rows_s = D_MODEL // N_DEV
    first, rest = ("gu1", "d1", "win"), ("sq", "kv", "gu2", "d2")

    def shards(l):
        c = lambda k: wts[k][l].astype(MXU_DT)
        return dict(gu1=jnp.stack([c("ffn1_w_gate"), c("ffn1_w_up")]), d1=c("ffn1_w_down"),
                    win=_win_to_aligned(wts["w_in"][l]).astype(MXU_DT),
                    sq=jnp.stack([c("w_out"), c("mem_w_q"), c("mem_w_o")]), kv=c("mem_w_kv"),
                    gu2=jnp.stack([c("ffn2_w_gate"), c("ffn2_w_up")]), d2=c("ffn2_w_down"))

    def to_compute_layout(w, keys, got):
        for k, g in zip(keys, got):
            if k in ("gu1", "gu2"):
                w[k] = g.transpose(2, 1, 0, 3).reshape(D_MODEL, 2 * D_FF)
            elif k in ("d1", "d2"):
                w[k] = g.reshape(D_FF, D_MODEL)
            elif k == "win":
                w[k] = g.reshape(D_MODEL, P_WIDTH)
            elif k == "sq":
                full = g.transpose(1, 0, 2, 3).reshape(3, D_MODEL, D_MODEL)
                w["wout"], w["wq"], w["wo"] = full[0], full[1], full[2]
            else:
                w["wkv"] = g.transpose(1, 0, 2).reshape(D_MODEL, 2 * D_MODEL)

    def chunks(gr, keys, dtype=MXU_DT):
        out = []
        for k in keys:
            if k in ("gu1", "gu2"):
                out.append(gr[k].reshape(D_MODEL, 2, N_DEV, -1).transpose(2, 1, 0, 3))
            elif k in ("d1", "d2"):
                out.append(gr[k].reshape(N_DEV, -1, D_MODEL))
            elif k == "win":
                out.append(gr[k].reshape(N_DEV, rows_s, P_WIDTH))
            elif k == "sq":
                out.append(jnp.stack([gr[n].reshape(N_DEV, rows_s, D_MODEL) for n in ("wout", "wq", "wo")], axis=1))
            else:
                out.append(gr["wkv"].reshape(D_MODEL, N_DEV, -1).transpose(1, 0, 2))
        return [a.astype(dtype) for a in out]

    sh = [shards(l) for l in range(DEPTH)]
    sm_sh = _pack([wts["gdn_conv_w"], wts["conf_dw_w"]])
    got = exchange([sh[0]["gu1"], sm_sh], broadcast=True, name="gather_first")
    conv_shapes = [wts["gdn_conv_w"].shape, wts["conf_dw_w"].shape]
    parts = [_unpack(got[-1][j], conv_shapes) for j in range(N_DEV)]
    gconv_full = jnp.concatenate([p[0] for p in parts], axis=-1)
    cconv_full = jnp.concatenate([p[1] for p in parts], axis=-1)

    def small_weights(l):
        w = dict(gdn_conv_w=gconv_full[l], conf_dw_w=cconv_full[l],
                 alog=_row128(wts["gdn_a_log"][l], A_COL), dtb=_row128(wts["gdn_dt_bias"][l], A_COL),
                 bf=_row128(wts["fox_b_f"][l], FOX_COL), ng=jnp.tile(wts["gdn_norm_g"][l], GROUP_HEADS)[None, :])
        for k in ("ln_ffn1_g", "ln_ffn1_b", "conf_dw_b", "conf_norm_g", "conf_norm_b", "ln_mix_g", "ln_mix_b",
                  "ln_mem_g", "ln_mem_b", "ln_ffn2_g", "ln_ffn2_b"):
            w[k] = wts[k][l][None, :]
        return w

    lw = [small_weights(l) for l in range(DEPTH)]
    to_compute_layout(lw[0], ("gu1",), got[:-1])

    def take(l, keys):
        return lambda g: to_compute_layout(lw[l], keys, g)

    def take_fox0(g):
        to_compute_layout(lw[0], rest[2:], g[:2])
        to_compute_layout(lw[1], first[:2], g[2:])

    h, h16, sv0 = _layer_fwd(
        x, x.astype(MXU_DT), mem, lw[0], "l0",
        comm_ffn1=Comm([sh[0][k] for k in first[1:]], True), on_ffn1=take(0, first[1:]),
        comm_gdn=Comm([sh[0][k] for k in rest[:2]], True), on_gdn=take(0, rest[:2]),
        comm_fox=Comm([sh[0][k] for k in rest[2:]] + [sh[1][k] for k in first[:2]], True), on_fox=take_fox0,
        comm_sb=Comm([sh[1]["win"]], True), on_sb=take(1, ("win",)))
    h, _, sv1 = _layer_fwd(
        h, h16, mem, lw[1], "l1",
        comm_gdn=Comm([sh[1][k] for k in rest[:2]], True), on_gdn=take(1, rest[:2]),
        comm_fox=Comm([sh[1][k] for k in rest[2:]], True), on_fox=take(1, rest[2:]))
    dh, lpart = loss_head(h, target, name="loss_head")

    recv = [{}, {}]
    e_ffn, e_mem = ("gu2", "d2"), ("sq", "kv")
    dh, g1, got, _ = _layer_bwd(dh, mem, sv1, lw[1], "l1",
                                lambda gr: (None, Comm(chunks(gr, e_ffn), False), Comm(chunks(gr, e_mem), False)))
    recv[1].update(zip(e_ffn, got[1]))
    recv[1].update(zip(e_mem, got[2]))
    tail = dict(comm_dh=lambda gr: Comm(chunks(gr, ("win",)), False),
                comm_dwgu=lambda dwd: Comm(chunks({"d1": dwd}, ("d1",)), False),
                comm_dx=lambda dwgu: Comm(chunks({"gu1": dwgu}, ("gu1",)), False))
    dh, g0, got, got_t = _layer_bwd(
        dh, mem, sv0, lw[0], "l0",
        lambda gr: (Comm(chunks(gr, e_mem), False), Comm(chunks(g1, first[:2]), False),
                    Comm(chunks(g1, first[2:]) + chunks(gr, e_ffn), False)), tail)
    recv[0].update(zip(e_mem, got[0]))
    recv[1].update(zip(first[:2], got[1]))
    recv[1].update(win=got[2][0])
    recv[0].update(zip(e_ffn, got[2][1:]))
    recv[0].update(win=got_t[0][0], d1=got_t[1][0], gu1=got_t[2][0])
    grad_x = dh[None]
    grads = [g0, g1]

    def gl(k):
        return jnp.stack([grads[l][k] for l in range(DEPTH)])

    small_names = SMALL_REPLICATED + SMALL_SHARDED
    small_grads = [gl(k) for k in small_names] + [lpart[0, :1]]
    got = exchange([_pack(small_grads)], broadcast=True, name="gather_small_grads")
    sm_sum = slot_sum(got[0], name="sum_small_grads")
    sm_g = _unpack(sm_sum, [g.shape for g in small_grads])
    loss = sm_g[-1][0]
    small_g = dict(zip(small_names, sm_g[:-1]))
    for k in SMALL_SHARDED:
        width = wts[k].shape[-1]
        small_g[k] = lax.dynamic_slice_in_dim(small_g[k], me * width, width, axis=2)

    out_g, out_d, out_m, out_v = {}, {}, {}, {}

    def update(names, key, fix=lambda a: a):
        res = {k: [] for k in names}
        for l in range(DEPTH):
            slots = fix(recv[l][key])
            slots = slots.reshape(N_DEV, -1, slots.shape[-1])
            for i, k in enumerate(names):
                two = lambda a: a[l].reshape(-1, a.shape[-1])
                res[k].append(adamw(two(wts[k]), two(ms[k]), two(vs[k]), slots, row0=i * two(wts[k]).shape[0],
                                    name=f"adamw_{k}_l{l}"))
        for k in names:
            for dst, per_layer in zip((out_g, out_d, out_m, out_v), zip(*res[k])):
                dst[k] = jnp.stack(per_layer).reshape(wts[k].shape)

    update(("ffn1_w_gate", "ffn1_w_up"), "gu1")
    update(("ffn1_w_down",), "d1")
    update(("w_in",), "win", _win_from_aligned)
    update(("w_out", "mem_w_q", "mem_w_o"), "sq")
    update(("mem_w_kv",), "kv")
    update(("ffn2_w_gate", "ffn2_w_up"), "gu2")
    update(("ffn2_w_down",), "d2")

    sw = _pack([wts[k] for k in small_names])
    smm = _pack([ms[k] for k in small_names])
    smv = _pack([vs[k] for k in small_names])
    sg = _pack([small_g[k] for k in small_names])
    res = adamw(sw, smm, smv, sg[None], name="adamw_small")
    shapes = [wts[k].shape for k in small_names]
    for dst, buf in zip((out_g, out_d, out_m, out_v), res):
        for k, a in zip(small_names, _unpack(buf, shapes)):
            dst[k] = a

    return (loss, grad_x, *[out_g[k] for k in WEIGHT_ORDER], *[out_d[k] for k in WEIGHT_ORDER],
            *[out_m[k] for k in WEIGHT_ORDER], *[out_v[k] for k in WEIGHT_ORDER])


def kernel(x, mem, ffn1_w_gate, ffn1_w_up, ffn1_w_down, ln_ffn1_g, ln_ffn1_b, w_in, gdn_conv_w, gdn_a_log, gdn_dt_bias, gdn_norm_g, fox_b_f, conf_dw_w, conf_dw_b, conf_norm_g, conf_norm_b, w_out, ln_mix_g, ln_mix_b, mem_w_q, mem_w_kv, mem_w_o, ln_mem_g, ln_mem_b, ffn2_w_gate, ffn2_w_up, ffn2_w_down, ln_ffn2_g, ln_ffn2_b, loss_target, m_ffn1_w_gate, m_ffn1_w_up, m_ffn1_w_down, m_ln_ffn1_g, m_ln_ffn1_b, m_w_in, m_gdn_conv_w, m_gdn_a_log, m_gdn_dt_bias, m_gdn_norm_g, m_fox_b_f, m_conf_dw_w, m_conf_dw_b, m_conf_norm_g, m_conf_norm_b, m_w_out, m_ln_mix_g, m_ln_mix_b, m_mem_w_q, m_mem_w_kv, m_mem_w_o, m_ln_mem_g, m_ln_mem_b, m_ffn2_w_gate, m_ffn2_w_up, m_ffn2_w_down, m_ln_ffn2_g, m_ln_ffn2_b, v_ffn1_w_gate, v_ffn1_w_up, v_ffn1_w_down, v_ln_ffn1_g, v_ln_ffn1_b, v_w_in, v_gdn_conv_w, v_gdn_a_log, v_gdn_dt_bias, v_gdn_norm_g, v_fox_b_f, v_conf_dw_w, v_conf_dw_b, v_conf_norm_g, v_conf_norm_b, v_w_out, v_ln_mix_g, v_ln_mix_b, v_mem_w_q, v_mem_w_kv, v_mem_w_o, v_ln_mem_g, v_ln_mem_b, v_ffn2_w_gate, v_ffn2_w_up, v_ffn2_w_down, v_ln_ffn2_g, v_ln_ffn2_b):
    args = locals()
    wts = {k: args[k] for k in WEIGHT_ORDER}
    ms = {k: args["m_" + k] for k in WEIGHT_ORDER}
    vs = {k: args["v_" + k] for k in WEIGHT_ORDER}
    return _step(x, mem, loss_target, wts, ms, vs)
```

```python
import functools
import math

import jax
import jax.numpy as jnp
import numpy as np
from jax import lax
from jax.experimental import pallas as pl
from jax.experimental.pallas import tpu as pltpu

F32 = jnp.float32
BF16 = jnp.bfloat16
MXU_DT = jnp.bfloat16
HI = lax.Precision.HIGHEST

N_DEV = 8
VMEM_LIMIT_BYTES = 56 * 1024 * 1024
LANES = 128

D_MODEL = 1024
DEPTH = 2
GROUP_WIDTH = 256
HEAD_DIM = 64
GROUP_HEADS = 4
D_FF = 2816
SHORT_CONV = 4
CONF_KERNEL = 31
CONF_GROUPS = 4
GDN_CHUNK = 64
N_MEM = 256
MEM_HEADS = 4
MEM_HEAD_DIM = 256
DN_ALPHA = float((2 * DEPTH) ** 0.25)
LN_EPS = 1e-5
RMS_EPS = 1e-6
L2_EPS = 1e-6
NEG_BIG = -1e30
IN_SPLITS = (768, 256, 4, 4, 768, 4, 512, 768)
IN_WIDTH = sum(IN_SPLITS)
P_GDN, P_Z, P_FOX, P_CONF, P_SB, P_SMALL = 0, 768, 1024, 1792, 2304, 3072
P_WIDTH = 3200

ADAM_LR = 0.001
ADAM_B1 = 0.9
ADAM_B2 = 0.999
ADAM_EPS = 1e-08
ADAM_WD = 0.01
ADAM_STEP = 10


def _cparams(sem):
    return pltpu.CompilerParams(dimension_semantics=sem, vmem_limit_bytes=VMEM_LIMIT_BYTES)


def _tile(n, pref, align=LANES):
    if n <= pref:
        return n
    t = (pref // align) * align
    while t >= align:
        if n % t == 0:
            return t
        t -= align
    return n


def mm(a, b, *, mode="nn", add=None, alpha=1.0, beta=1.0, out_dtype=F32, name,
       tm=1024, tn=512, tk=1024, comm=None):
    if mode == "nn":
        (m, k), (k2, n) = a.shape, b.shape
    elif mode == "nt":
        (m, k), (n, k2) = a.shape, b.shape
    else:
        (k, m), (k2, n) = a.shape, b.shape
    assert k == k2, (a.shape, b.shape, mode)
    tm = _tile(m, tm, 8 if mode != "tn" else LANES)
    tn = _tile(n, tn)
    tk = _tile(k, tk, LANES if mode != "tn" else 8)
    nk = k // tk
    if mode == "nn":
        a_spec = pl.BlockSpec((tm, tk), lambda i, j, kk: (i, kk))
        b_spec = pl.BlockSpec((tk, tn), lambda i, j, kk: (kk, j))
        dims = (((1,), (0,)), ((), ()))
    elif mode == "nt":
        a_spec = pl.BlockSpec((tm, tk), lambda i, j, kk: (i, kk))
        b_spec = pl.BlockSpec((tn, tk), lambda i, j, kk: (j, kk))
        dims = (((1,), (1,)), ((), ()))
    else:
        a_spec = pl.BlockSpec((tk, tm), lambda i, j, kk: (kk, i))
        b_spec = pl.BlockSpec((tk, tn), lambda i, j, kk: (kk, j))
        dims = (((0,), (0,)), ((), ()))
    o_spec = pl.BlockSpec((tm, tn), lambda i, j, kk: (i, j))
    has_add = add is not None

    def body(*refs):
        if has_add:
            a_ref, b_ref, add_ref, o_ref, acc_ref = refs
        else:
            a_ref, b_ref, o_ref, acc_ref = refs
        kk = pl.program_id(2)

        @pl.when(kk == 0)
        def _():
            acc_ref[...] = jnp.zeros_like(acc_ref)

        acc_ref[...] += lax.dot_general(a_ref[...].astype(MXU_DT), b_ref[...].astype(MXU_DT), dims,
                                        preferred_element_type=F32)

        @pl.when(kk == nk - 1)
        def _():
            r = acc_ref[...]
            if alpha != 1.0:
                r = r * alpha
            if has_add:
                r = r + beta * add_ref[...].astype(F32)
            o_ref[...] = r.astype(out_dtype)

    in_specs = [a_spec, b_spec] + ([o_spec] if has_add else [])
    args = (a, b) + ((add,) if has_add else ())
    grid = (m // tm, n // tn, nk)
    call = dict(name=name, grid=grid, in_specs=in_specs, out_specs=[o_spec],
                out_shape=[jax.ShapeDtypeStruct((m, n), out_dtype)],
                scratch_shapes=[pltpu.VMEM((tm, tn), F32)],
                compiler_params=_cparams(("parallel", "parallel", "arbitrary")))
    (out,), got = carry_comm(call, body, args, comm, 1, *_grid_ends(*grid))
    return out if comm is None else (out, got)


def ln_res_fwd(x, y, g, b, s, *, name):
    t, d = x.shape
    tm = _tile(t, 512, 8)

    def body(x_ref, y_ref, g_ref, b_ref, o_ref, o16_ref, xh_ref, rs_ref):
        z = DN_ALPHA * x_ref[...] + s * y_ref[...]
        mu = jnp.mean(z, axis=-1, keepdims=True)
        zc = z - mu
        var = jnp.mean(zc * zc, axis=-1, keepdims=True)
        rstd = lax.rsqrt(var + LN_EPS)
        xh = zc * rstd
        xh_ref[...] = xh
        rs_ref[...] = jnp.broadcast_to(rstd, rs_ref.shape)
        out = xh * g_ref[...] + b_ref[...]
        o_ref[...] = out
        o16_ref[...] = out.astype(o16_ref.dtype)

    row = pl.BlockSpec((tm, d), lambda i: (i, 0))
    vec = pl.BlockSpec((1, d), lambda i: (0, 0))
    return pl.pallas_call(
        body, name=name, grid=(t // tm,),
        in_specs=[row, row, vec, vec],
        out_specs=[row, row, row, pl.BlockSpec((tm, LANES), lambda i: (i, 0))],
        out_shape=[jax.ShapeDtypeStruct((t, d), F32), jax.ShapeDtypeStruct((t, d), MXU_DT),
                   jax.ShapeDtypeStruct((t, d), F32), jax.ShapeDtypeStruct((t, LANES), F32)],
        compiler_params=_cparams(("parallel",)),
    )(x, y, g, b)


def ln_res_bwd(dout, xhat, rstd, g, *, name):
    t, d = dout.shape
    tm = _tile(t, 512, 8)

    def body(do_ref, xh_ref, rs_ref, g_ref, dz_ref, dg_ref, db_ref):
        i = pl.program_id(0)

        @pl.when(i == 0)
        def _():
            dg_ref[...] = jnp.zeros_like(dg_ref)
            db_ref[...] = jnp.zeros_like(db_ref)

        do = do_ref[...]
        xh = xh_ref[...]
        dxh = do * g_ref[...]
        m1 = jnp.mean(dxh, axis=-1, keepdims=True)
        m2 = jnp.mean(dxh * xh, axis=-1, keepdims=True)
        dz_ref[...] = rs_ref[:, 0:1] * (dxh - m1 - xh * m2)
        dg_ref[...] += jnp.sum(do * xh, axis=0, keepdims=True)
        db_ref[...] += jnp.sum(do, axis=0, keepdims=True)

    row = pl.BlockSpec((tm, d), lambda i: (i, 0))
    vec = pl.BlockSpec((1, d), lambda i: (0, 0))
    return pl.pallas_call(
        body, name=name, grid=(t // tm,),
        in_specs=[row, row, pl.BlockSpec((tm, LANES), lambda i: (i, 0)), vec],
        out_specs=[row, vec, vec],
        out_shape=[jax.ShapeDtypeStruct((t, d), F32), jax.ShapeDtypeStruct((1, d), F32),
                   jax.ShapeDtypeStruct((1, d), F32)],
        compiler_params=_cparams(("arbitrary",)),
    )(dout, xhat, rstd, g)


def _sigmoid(x):
    return 1.0 / (1.0 + jnp.exp(-x))


def act_fwd(gu, *, name):
    t, f2 = gu.shape
    f = f2 // 2
    tm = _tile(t, 256, 8)

    def body(gu_ref, h_ref):
        g = gu_ref[:, :f].astype(F32)
        h_ref[...] = (g * _sigmoid(g) * gu_ref[:, f:].astype(F32)).astype(h_ref.dtype)

    return pl.pallas_call(
        body, name=name, grid=(t // tm,),
        in_specs=[pl.BlockSpec((tm, f2), lambda i: (i, 0))],
        out_specs=pl.BlockSpec((tm, f), lambda i: (i, 0)),
        out_shape=jax.ShapeDtypeStruct((t, f), MXU_DT),
        compiler_params=_cparams(("parallel",)),
    )(gu)


def act_bwd(gu, dh, *, name):
    t, f2 = gu.shape
    f = f2 // 2
    tm = _tile(t, 256, 8)

    def body(gu_ref, dh_ref, o_ref):
        g = gu_ref[:, :f].astype(F32)
        u = gu_ref[:, f:].astype(F32)
        dh = dh_ref[...]
        sg = _sigmoid(g)
        o_ref[:, f:] = (dh * g * sg).astype(o_ref.dtype)
        o_ref[:, :f] = (dh * u * sg * (1.0 + g * (1.0 - sg))).astype(o_ref.dtype)

    return pl.pallas_call(
        body, name=name, grid=(t // tm,),
        in_specs=[pl.BlockSpec((tm, f2), lambda i: (i, 0)), pl.BlockSpec((tm, f), lambda i: (i, 0))],
        out_specs=pl.BlockSpec((tm, f2), lambda i: (i, 0)),
        out_shape=jax.ShapeDtypeStruct((t, f2), MXU_DT),
        compiler_params=_cparams(("parallel",)),
    )(gu, dh)


def loss_head(y, target, *, name):
    t, d = y.shape
    tm = _tile(t, 512, 8)

    def body(y_ref, t_ref, dy_ref, l_ref):
        i = pl.program_id(0)

        @pl.when(i == 0)
        def _():
            l_ref[...] = jnp.zeros_like(l_ref)

        err = y_ref[...] - t_ref[...]
        dy_ref[...] = err * (1.0 / d)
        part = jnp.sum(jnp.sum(err * err, axis=-1, keepdims=True), axis=0, keepdims=True)
        l_ref[...] += jnp.broadcast_to(part * (0.5 / d), l_ref.shape)

    row = pl.BlockSpec((tm, d), lambda i: (i, 0))
    return pl.pallas_call(
        body, name=name, grid=(t // tm,),
        in_specs=[row, row],
        out_specs=[row, pl.BlockSpec((1, LANES), lambda i: (0, 0))],
        out_shape=[jax.ShapeDtypeStruct((t, d), F32), jax.ShapeDtypeStruct((1, LANES), F32)],
        compiler_params=_cparams(("arbitrary",)),
    )(y, target)


def _dot(a, b):
    return lax.dot_general(a, b, (((1,), (0,)), ((), ())), preferred_element_type=F32)


def _dot_nt(a, b):
    return lax.dot_general(a, b, (((1,), (1,)), ((), ())), preferred_element_type=F32)


def _dot_tn(a, b):
    return lax.dot_general(a, b, (((0,), (0,)), ((), ())), preferred_element_type=F32)


def _dot_hi(a, b):
    return lax.dot_general(a, b, (((1,), (0,)), ((), ())), preferred_element_type=F32, precision=HI)


def _dot_nt_hi(a, b):
    return lax.dot_general(a, b, (((1,), (1,)), ((), ())), preferred_element_type=F32, precision=HI)


def _split_dot(x, u):
    hi = x.astype(MXU_DT)
    lo = (x - hi.astype(F32)).astype(MXU_DT)
    return _dot(hi, u) + _dot(lo, u)


def _mem_probs(q_ref, kv_ref, h):
    lo = h * MEM_HEAD_DIM
    qh = q_ref[:, lo:lo + MEM_HEAD_DIM].astype(MXU_DT)
    kh = kv_ref[:, lo:lo + MEM_HEAD_DIM].astype(MXU_DT)
    s = _dot_nt(qh, kh) * (MEM_HEAD_DIM ** -0.5)
    s = s - jnp.max(s, axis=-1, keepdims=True)
    p = jnp.exp(s)
    return p / jnp.sum(p, axis=-1, keepdims=True), qh, kh


def memattn_fwd(q, kv, *, name):
    t, d = q.shape
    tm = _tile(t, 512, 8)

    def body(q_ref, kv_ref, o_ref):
        for h in range(MEM_HEADS):
            lo = h * MEM_HEAD_DIM
            p, _, _ = _mem_probs(q_ref, kv_ref, h)
            vh = kv_ref[:, d + lo:d + lo + MEM_HEAD_DIM].astype(MXU_DT)
            o_ref[:, lo:lo + MEM_HEAD_DIM] = _dot(p.astype(MXU_DT), vh).astype(o_ref.dtype)

    return pl.pallas_call(
        body, name=name, grid=(t // tm,),
        in_specs=[pl.BlockSpec((tm, d), lambda i: (i, 0)), pl.BlockSpec(kv.shape, lambda i: (0, 0))],
        out_specs=pl.BlockSpec((tm, d), lambda i: (i, 0)),
        out_shape=jax.ShapeDtypeStruct((t, d), MXU_DT),
        compiler_params=_cparams(("parallel",)),
    )(q, kv)


def memattn_bwd(q, kv, datt, *, name):
    t, d = q.shape
    tm = _tile(t, 512, 8)
    scale = MEM_HEAD_DIM ** -0.5

    def body(q_ref, kv_ref, da_ref, dq_ref, dkv_ref):
        @pl.when(pl.program_id(0) == 0)
        def _():
            dkv_ref[...] = jnp.zeros_like(dkv_ref)

        for h in range(MEM_HEADS):
            lo = h * MEM_HEAD_DIM
            p, qh, kh = _mem_probs(q_ref, kv_ref, h)
            vh = kv_ref[:, d + lo:d + lo + MEM_HEAD_DIM].astype(MXU_DT)
            da = da_ref[:, lo:lo + MEM_HEAD_DIM].astype(MXU_DT)
            dp = _dot_nt(da, vh)
            ds = p * (dp - jnp.sum(dp * p, axis=-1, keepdims=True))
            dsb = ds.astype(MXU_DT)
            dq_ref[:, lo:lo + MEM_HEAD_DIM] = (_dot(dsb, kh) * scale).astype(dq_ref.dtype)
            dkv_ref[:, lo:lo + MEM_HEAD_DIM] += _dot_tn(dsb, qh) * scale
            dkv_ref[:, d + lo:d + lo + MEM_HEAD_DIM] += _dot_tn(p.astype(MXU_DT), da)

    row = pl.BlockSpec((tm, d), lambda i: (i, 0))
    full = pl.BlockSpec(kv.shape, lambda i: (0, 0))
    return pl.pallas_call(
        body, name=name, grid=(t // tm,),
        in_specs=[row, full, row],
        out_specs=[row, full],
        out_shape=[jax.ShapeDtypeStruct((t, d), MXU_DT), jax.ShapeDtypeStruct(kv.shape, F32)],
        compiler_params=_cparams(("arbitrary",)),
    )(q, kv, datt)


def _halo(k):
    return 8 * ((k - 1 + 7) // 8)


def dwconv_fwd(u, w, bias, *, col0=0, width=None, name):
    t = u.shape[0]
    kk, c = w.shape
    width = c if width is None else width
    assert width == c and col0 % c == 0
    cb = col0 // c
    hb = _halo(kk)
    tm = _tile(t, 512, hb)
    r = tm // hb
    has_bias = bias is not None

    def body(*refs):
        if has_bias:
            prev_ref, cur_ref, w_ref, b_ref, o_ref, scr = refs
        else:
            prev_ref, cur_ref, w_ref, o_ref, scr = refs
        i = pl.program_id(0)
        scr[0:hb, :] = jnp.where(i == 0, 0.0, prev_ref[...])
        scr[hb:hb + tm, :] = cur_ref[...]
        acc = jnp.zeros((tm, c), F32)
        for k in range(kk):
            acc = acc + w_ref[k:k + 1, :] * scr[pl.ds(hb - (kk - 1) + k, tm), :]
        if has_bias:
            acc = acc + b_ref[...]
        o_ref[...] = acc

    in_specs = [pl.BlockSpec((hb, c), lambda i: (jnp.maximum(i * r - 1, 0), cb)),
                pl.BlockSpec((tm, c), lambda i: (i, cb)),
                pl.BlockSpec((kk, c), lambda i: (0, 0))]
    args = [u, u, w]
    if has_bias:
        in_specs.append(pl.BlockSpec((1, c), lambda i: (0, 0)))
        args.append(bias)
    return pl.pallas_call(
        body, name=name, grid=(t // tm,),
        in_specs=in_specs,
        out_specs=pl.BlockSpec((tm, c), lambda i: (i, 0)),
        out_shape=jax.ShapeDtypeStruct((t, c), F32),
        scratch_shapes=[pltpu.VMEM((hb + tm, c), F32)],
        compiler_params=_cparams(("parallel",)),
    )(*args)


def dwconv_bwd(dc, u, w, *, col0=0, name):
    t, c = dc.shape
    kk = w.shape[0]
    assert col0 % c == 0
    cb = col0 // c
    hb = _halo(kk)
    tm = _tile(t, 512, hb)
    r = tm // hb
    n = t // tm

    def body(dcur_ref, dnext_ref, uprev_ref, ucur_ref, w_ref, du_ref, dw_ref, db_ref, sd, su):
        i = pl.program_id(0)

        @pl.when(i == 0)
        def _():
            dw_ref[...] = jnp.zeros_like(dw_ref)
            db_ref[...] = jnp.zeros_like(db_ref)

        dcur = dcur_ref[...]
        sd[0:tm, :] = dcur
        sd[tm:tm + hb, :] = jnp.where(i == n - 1, 0.0, dnext_ref[...])
        su[0:hb, :] = jnp.where(i == 0, 0.0, uprev_ref[...])
        su[hb:hb + tm, :] = ucur_ref[...]
        acc = jnp.zeros((tm, c), F32)
        for k in range(kk):
            acc = acc + w_ref[k:k + 1, :] * sd[pl.ds(kk - 1 - k, tm), :]
            dw_ref[k:k + 1, :] += jnp.sum(dcur * su[pl.ds(hb - (kk - 1) + k, tm), :], axis=0, keepdims=True)
        du_ref[...] = acc
        db_ref[...] += jnp.sum(dcur, axis=0, keepdims=True)

    return pl.pallas_call(
        body, name=name, grid=(n,),
        in_specs=[pl.BlockSpec((tm, c), lambda i: (i, 0)),
                  pl.BlockSpec((hb, c), lambda i: (jnp.minimum((i + 1) * r, n * r - 1), 0)),
                  pl.BlockSpec((hb, c), lambda i: (jnp.maximum(i * r - 1, 0), cb)),
                  pl.BlockSpec((tm, c), lambda i: (i, cb)),
                  pl.BlockSpec((kk, c), lambda i: (0, 0))],
        out_specs=[pl.BlockSpec((tm, c), lambda i: (i, 0)),
                   pl.BlockSpec((kk, c), lambda i: (0, 0)),
                   pl.BlockSpec((1, c), lambda i: (0, 0))],
        out_shape=[jax.ShapeDtypeStruct((t, c), F32), jax.ShapeDtypeStruct((kk, c), F32),
                   jax.ShapeDtypeStruct((1, c), F32)],
        scratch_shapes=[pltpu.VMEM((tm + hb, c), F32), pltpu.VMEM((hb + tm, c), F32)],
        compiler_params=_cparams(("arbitrary",)),
    )(dc, dc, u, u, w)


def glu_fwd(proj, *, name):
    t = proj.shape[0]
    c = GROUP_WIDTH
    tm = _tile(t, 1024, 8)
    vb, gb = P_CONF // c, P_CONF // c + 1

    def body(v_ref, g_ref, o_ref):
        o_ref[...] = v_ref[...] * _sigmoid(g_ref[...])

    return pl.pallas_call(
        body, name=name, grid=(t // tm,),
        in_specs=[pl.BlockSpec((tm, c), lambda i: (i, vb)), pl.BlockSpec((tm, c), lambda i: (i, gb))],
        out_specs=pl.BlockSpec((tm, c), lambda i: (i, 0)),
        out_shape=jax.ShapeDtypeStruct((t, c), F32),
        compiler_params=_cparams(("parallel",)),
    )(proj, proj)


def glu_bwd(proj, du, *, name):
    t = proj.shape[0]
    c = GROUP_WIDTH
    tm = _tile(t, 1024, 8)
    vb, gb = P_CONF // c, P_CONF // c + 1

    def body(v_ref, g_ref, du_ref, o_ref):
        sg = _sigmoid(g_ref[...])
        du = du_ref[...]
        o_ref[:, :c] = du * sg
        o_ref[:, c:] = du * v_ref[...] * sg * (1.0 - sg)

    return pl.pallas_call(
        body, name=name, grid=(t // tm,),
        in_specs=[pl.BlockSpec((tm, c), lambda i: (i, vb)), pl.BlockSpec((tm, c), lambda i: (i, gb)),
                  pl.BlockSpec((tm, c), lambda i: (i, 0))],
        out_specs=pl.BlockSpec((tm, 2 * c), lambda i: (i, 0)),
        out_shape=jax.ShapeDtypeStruct((t, 2 * c), F32),
        compiler_params=_cparams(("parallel",)),
    )(proj, proj, du)


def _group_mean_matrix(c, groups):
    gsz = c // groups
    ri = lax.broadcasted_iota(jnp.int32, (c, c), 0) // gsz
    ci = lax.broadcasted_iota(jnp.int32, (c, c), 1) // gsz
    return jnp.where(ri == ci, 1.0 / gsz, 0.0).astype(F32)


def gn_silu_fwd(cx, gamma, beta, *, name):
    t, c = cx.shape
    tm = _tile(t, 1024, 8)

    def body(c_ref, g_ref, b_ref, o_ref):
        gm = _group_mean_matrix(c, CONF_GROUPS)
        x = c_ref[...]
        mu = _dot_hi(x, gm)
        xc = x - mu
        var = _dot_hi(xc * xc, gm)
        a = xc * lax.rsqrt(var + LN_EPS) * g_ref[...] + b_ref[...]
        o_ref[...] = a * _sigmoid(a)

    row = pl.BlockSpec((tm, c), lambda i: (i, 0))
    vec = pl.BlockSpec((1, c), lambda i: (0, 0))
    return pl.pallas_call(
        body, name=name, grid=(t // tm,),
        in_specs=[row, vec, vec], out_specs=row,
        out_shape=jax.ShapeDtypeStruct((t, c), F32),
        compiler_params=_cparams(("parallel",)),
    )(cx, gamma, beta)


def gn_silu_bwd(cx, gamma, beta, dy, *, name):
    t, c = cx.shape
    tm = _tile(t, 1024, 8)

    def body(c_ref, g_ref, b_ref, dy_ref, dc_ref, dg_ref, db_ref):
        @pl.when(pl.program_id(0) == 0)
        def _():
            dg_ref[...] = jnp.zeros_like(dg_ref)
            db_ref[...] = jnp.zeros_like(db_ref)

        gm = _group_mean_matrix(c, CONF_GROUPS)
        x = c_ref[...]
        mu = _dot_hi(x, gm)
        xc = x - mu
        var = _dot_hi(xc * xc, gm)
        rstd = lax.rsqrt(var + LN_EPS)
        nrm = xc * rstd
        a = nrm * g_ref[...] + b_ref[...]
        sa = _sigmoid(a)
        da = dy_ref[...] * sa * (1.0 + a * (1.0 - sa))
        dg_ref[...] += jnp.sum(da * nrm, axis=0, keepdims=True)
        db_ref[...] += jnp.sum(da, axis=0, keepdims=True)
        dn = da * g_ref[...]
        dc_ref[...] = rstd * (dn - _dot_hi(dn, gm) - nrm * _dot_hi(dn * nrm, gm))

    row = pl.BlockSpec((tm, c), lambda i: (i, 0))
    vec = pl.BlockSpec((1, c), lambda i: (0, 0))
    return pl.pallas_call(
        body, name=name, grid=(t // tm,),
        in_specs=[row, vec, vec, row], out_specs=[row, vec, vec],
        out_shape=[jax.ShapeDtypeStruct((t, c), F32), jax.ShapeDtypeStruct((1, c), F32),
                   jax.ShapeDtypeStruct((1, c), F32)],
        compiler_params=_cparams(("arbitrary",)),
    )(cx, gamma, beta, dy)


FOX_COL = 8
SMALL_BLK = P_SMALL // LANES


def _log_sigmoid(x):
    return jnp.minimum(x, 0.0) - jnp.log(1.0 + jnp.exp(-jnp.abs(x)))


def _fox_cols(shape):
    col = lax.broadcasted_iota(jnp.int32, shape, 1)
    return (col >= FOX_COL) & (col < FOX_COL + GROUP_HEADS)


def fox_gate_fwd(proj, bvec, *, name):
    t = proj.shape[0]
    tm = _tile(t, 256, 8)

    def body(s_ref, b_ref, o_ref, carry):
        @pl.when(pl.program_id(0) == 0)
        def _():
            carry[...] = jnp.zeros_like(carry)

        lf = jnp.where(_fox_cols((tm, LANES)), _log_sigmoid(s_ref[...] + b_ref[...]), 0.0)
        ri = lax.broadcasted_iota(jnp.int32, (tm, tm), 0)
        ci = lax.broadcasted_iota(jnp.int32, (tm, tm), 1)
        cum = _dot_hi(jnp.where(ri >= ci, 1.0, 0.0).astype(F32), lf) + carry[...]
        o_ref[...] = cum
        carry[...] = cum[tm - 1:tm, :]

    return pl.pallas_call(
        body, name=name, grid=(t // tm,),
        in_specs=[pl.BlockSpec((tm, LANES), lambda i: (i, SMALL_BLK)), pl.BlockSpec((1, LANES), lambda i: (0, 0))],
        out_specs=pl.BlockSpec((tm, LANES), lambda i: (i, 0)),
        out_shape=jax.ShapeDtypeStruct((t, LANES), F32),
        scratch_shapes=[pltpu.VMEM((1, LANES), F32)],
        compiler_params=_cparams(("arbitrary",)),
    )(proj, bvec)


def fox_gate_bwd(dcum, proj, bvec, *, name):
    t = proj.shape[0]
    tm = _tile(t, 256, 8)
    n = t // tm

    def body(d_ref, s_ref, b_ref, o_ref, db_ref, carry):
        @pl.when(pl.program_id(0) == 0)
        def _():
            carry[...] = jnp.zeros_like(carry)
            db_ref[...] = jnp.zeros_like(db_ref)

        ri = lax.broadcasted_iota(jnp.int32, (tm, tm), 0)
        ci = lax.broadcasted_iota(jnp.int32, (tm, tm), 1)
        dlf = _dot_hi(jnp.where(ri <= ci, 1.0, 0.0).astype(F32), d_ref[...]) + carry[...]
        carry[...] = dlf[0:1, :]
        x = s_ref[...] + b_ref[...]
        dx = jnp.where(_fox_cols((tm, LANES)), dlf * (1.0 - _sigmoid(x)), 0.0)
        o_ref[...] = dx
        db_ref[...] += jnp.sum(dx, axis=0, keepdims=True)

    return pl.pallas_call(
        body, name=name, grid=(n,),
        in_specs=[pl.BlockSpec((tm, LANES), lambda i: (n - 1 - i, 0)),
                  pl.BlockSpec((tm, LANES), lambda i: (n - 1 - i, SMALL_BLK)),
                  pl.BlockSpec((1, LANES), lambda i: (0, 0))],
        out_specs=[pl.BlockSpec((tm, LANES), lambda i: (n - 1 - i, 0)), pl.BlockSpec((1, LANES), lambda i: (0, 0))],
        out_shape=[jax.ShapeDtypeStruct((t, LANES), F32), jax.ShapeDtypeStruct((1, LANES), F32)],
        scratch_shapes=[pltpu.VMEM((1, LANES), F32)],
        compiler_params=_cparams(("arbitrary",)),
    )(dcum, proj, bvec)


def _head_masks(c):
    lane_head = lax.broadcasted_iota(jnp.int32, (1, c), 1) // HEAD_DIM
    return [lane_head == h for h in range(GROUP_HEADS)]


def _attn_tiles(t, tq, tk):
    tq = _tile(t, tq, 8)
    tk = _tile(t, tk, LANES)
    return tq, tk, t // tq, t // tk


def _grid_ends(*sizes):
    first = lambda: functools.reduce(lambda a, b: a & b, [pl.program_id(d) == 0 for d in range(len(sizes))])
    last = lambda: functools.reduce(lambda a, b: a & b, [pl.program_id(d) == s - 1 for d, s in enumerate(sizes)])
    return first, last


EXP_DEAD = -110.0


def _key_norm_max(k_ref, nk, tk, masks):
    lane = lax.broadcasted_iota(jnp.int32, (1, LANES), 1)

    def one(jt, km):
        kb = k_ref[pl.ds(pl.multiple_of(jt * tk, tk), tk), :].astype(MXU_DT).astype(F32)
        sq = kb * kb
        for h in range(GROUP_HEADS):
            top = jnp.max(jnp.sum(jnp.where(masks[h], sq, 0.0), axis=-1, keepdims=True))
            km = jnp.where(lane == h, jnp.maximum(km, top), km)
        return km

    return lax.fori_loop(0, nk, one, jnp.zeros((1, LANES), F32))


def _fox_reach(qh, km, cc_ref):
    out = []
    for h in range(GROUP_HEADS):
        qf = qh[h].astype(F32)
        qn = jnp.sqrt(jnp.sum(qf * qf, axis=-1, keepdims=True))
        out.append(1.001 * qn * jnp.sqrt(km[:, h:h + 1]) + cc_ref[:, FOX_COL + h:FOX_COL + h + 1])
    return out


def _fox_alive(reach, top, cr_ref, j, tk):
    ends = cr_ref[jnp.maximum(j, 0)][:, tk - 1:tk]
    worst = jnp.float32(NEG_BIG)
    for h in range(GROUP_HEADS):
        worst = jnp.maximum(worst, jnp.max(reach[h] - top[h]) - jnp.max(ends[h:h + 1, :]))
    return (worst > EXP_DEAD).astype(jnp.int32)


def fox_fwd(proj, cum, cum_t, *, name, tq=512, tk=512, comm=None):
    t = proj.shape[0]
    c = GROUP_WIDTH
    tq, tk, nq, nk = _attn_tiles(t, tq, tk)
    assert tq == tk
    qb = P_FOX // c
    scale = HEAD_DIM ** -0.5
    cr3 = cum_t.reshape(8, nk, tk).transpose(1, 0, 2)

    def body(q_ref, k_ref, v_ref, cc_ref, cr_ref, o_ref, lse_ref, m_scr, l_scr, acc_scr, km_scr):
        i = pl.program_id(0)
        masks = _head_masks(c)

        @pl.when(i == 0)
        def _():
            km_scr[...] = _key_norm_max(k_ref, nk, tk, masks)

        q = q_ref[...] * scale
        qh = [jnp.where(masks[h], q, 0.0).astype(MXU_DT) for h in range(GROUP_HEADS)]
        cc = [cc_ref[:, FOX_COL + h:FOX_COL + h + 1] for h in range(GROUP_HEADS)]
        reach = _fox_reach(qh, km_scr[...], cc_ref)
        m_scr[...] = jnp.full_like(m_scr, NEG_BIG)
        l_scr[...] = jnp.zeros_like(l_scr)
        acc_scr[...] = jnp.zeros_like(acc_scr)

        def tile(j, diagonal):
            rows = pl.ds(pl.multiple_of(j * tk, tk), tk)
            kb = k_ref[rows, :].astype(MXU_DT)
            vb = v_ref[rows, :].astype(MXU_DT)
            crj = cr_ref[j]
            if diagonal:
                causal = (lax.broadcasted_iota(jnp.int32, (tq, tk), 1) <= lax.broadcasted_iota(jnp.int32, (tq, tk), 0))
            acc = acc_scr[...]
            for h in range(GROUP_HEADS):
                u = _dot_nt(qh[h], kb) - crj[h:h + 1, :]
                if diagonal:
                    u = jnp.where(causal, u, NEG_BIG)
                m_old = m_scr[h]
                m_new = jnp.maximum(m_old, jnp.max(u, axis=-1, keepdims=True) + cc[h])
                p = jnp.exp(u - (m_new - cc[h]))
                alpha = jnp.exp(m_old - m_new)
                l_scr[h] = alpha * l_scr[h] + jnp.sum(p, axis=-1, keepdims=True)
                m_scr[h] = m_new
                acc = jnp.where(masks[h], alpha * acc + _dot(p.astype(MXU_DT), vb), acc)
            acc_scr[...] = acc

        def alive(j):
            return _fox_alive(reach, [m_scr[h] for h in range(GROUP_HEADS)], cr_ref, j, tk)

        def step(state):
            j = i - state[0]
            tile(j, False)
            return state[0] + 1, alive(j - 1)

        tile(i, True)
        lax.while_loop(lambda s: (s[0] <= i) & (s[1] > 0), step, (jnp.int32(1), alive(i - 1)))
        acc = acc_scr[...]
        o = jnp.zeros_like(acc)
        lse = jnp.zeros((tq, LANES), F32)
        lane = lax.broadcasted_iota(jnp.int32, (1, LANES), 1)
        for h in range(GROUP_HEADS):
            o = jnp.where(masks[h], acc / l_scr[h], o)
            lse = jnp.where(lane == h, m_scr[h] + jnp.log(l_scr[h]), lse)
        o_ref[...] = o
        lse_ref[...] = lse

    resident = lambda blk: pl.BlockSpec((t, c), lambda i: (0, blk), pipeline_mode=pl.Buffered(1))
    call = dict(
        name=name, grid=(nq,),
        in_specs=[pl.BlockSpec((tq, c), lambda i: (i, qb)), resident(qb + 1), resident(qb + 2),
                  pl.BlockSpec((tq, LANES), lambda i: (i, 0)),
                  pl.BlockSpec((nk, 8, tk), lambda i: (0, 0, 0), pipeline_mode=pl.Buffered(1))],
        out_specs=[pl.BlockSpec((tq, c), lambda i: (i, 0)), pl.BlockSpec((tq, LANES), lambda i: (i, 0))],
        out_shape=[jax.ShapeDtypeStruct((t, c), F32), jax.ShapeDtypeStruct((t, LANES), F32)],
        scratch_shapes=[pltpu.VMEM((GROUP_HEADS, tq, 1), F32), pltpu.VMEM((GROUP_HEADS, tq, 1), F32),
                        pltpu.VMEM((tq, c), F32), pltpu.VMEM((1, LANES), F32)],
        compiler_params=_cparams(("arbitrary",)),
    )
    outs, got = carry_comm(call, body, (proj, proj, proj, cum, cr3), comm, 2, *_grid_ends(nq))
    return (*outs, got)


def fox_bwd(proj, cum, cum_t, o, lse, do, *, name, tq=512, tk=512, comm=None):
    t = proj.shape[0]
    c = GROUP_WIDTH
    tq, tk, nq, nk = _attn_tiles(t, tq, tk)
    assert tq == tk
    qb = P_FOX // c
    scale = HEAD_DIM ** -0.5
    cr3 = cum_t.reshape(8, nk, tk).transpose(1, 0, 2)

    def body(q_ref, k_ref, v_ref, cc_ref, cr_ref, o_ref, lse_ref, do_ref,
             dq_ref, dk_hbm, dv_hbm, dcc_ref, dcr_ref, dq_scr, rs_scr, dk_scr, dv_scr, km_scr):
        i = pl.program_id(0)
        masks = _head_masks(c)

        @pl.when(i == 0)
        def _():
            dk_scr[...] = jnp.zeros_like(dk_scr)
            dv_scr[...] = jnp.zeros_like(dv_scr)
            dcr_ref[...] = jnp.zeros_like(dcr_ref)
            km_scr[...] = _key_norm_max(k_ref, nk, tk, masks)

        q = q_ref[...]
        qf = q.astype(MXU_DT)
        qh = [jnp.where(masks[h], q * scale, 0.0).astype(MXU_DT) for h in range(GROUP_HEADS)]
        do = do_ref[...]
        dob = do.astype(MXU_DT)
        doh = [jnp.where(masks[h], do, 0.0).astype(MXU_DT) for h in range(GROUP_HEADS)]
        doo = do * o_ref[...]
        delta = [jnp.sum(jnp.where(masks[h], doo, 0.0), axis=-1, keepdims=True) for h in range(GROUP_HEADS)]
        lse_h = [lse_ref[:, h:h + 1] for h in range(GROUP_HEADS)]
        off = [lse_h[h] - cc_ref[:, FOX_COL + h:FOX_COL + h + 1] for h in range(GROUP_HEADS)]
        reach = _fox_reach(qh, km_scr[...], cc_ref)
        dq_scr[...] = jnp.zeros_like(dq_scr)
        rs_scr[...] = jnp.zeros_like(rs_scr)

        def tile(j, diagonal):
            rows = pl.ds(pl.multiple_of(j * tk, tk), tk)
            kb = k_ref[rows, :].astype(MXU_DT)
            vb = v_ref[rows, :].astype(MXU_DT)
            crj = cr_ref[j]
            if diagonal:
                causal = (lax.broadcasted_iota(jnp.int32, (tq, tk), 1) <= lax.broadcasted_iota(jnp.int32, (tq, tk), 0))
            dq = dq_scr[...]
            dk_upd = jnp.zeros((tk, c), F32)
            dv_upd = jnp.zeros((tk, c), F32)
            for h in range(GROUP_HEADS):
                p = jnp.exp(_dot_nt(qh[h], kb) - crj[h:h + 1, :] - off[h])
                if diagonal:
                    p = jnp.where(causal, p, 0.0)
                ds = p * (_dot_nt(doh[h], vb) - delta[h])
                dsb = ds.astype(MXU_DT)
                dq = jnp.where(masks[h], dq + _dot(dsb, kb) * scale, dq)
                dk_upd = jnp.where(masks[h], _dot_tn(dsb, qf) * scale, dk_upd)
                dv_upd = jnp.where(masks[h], _dot_tn(p.astype(MXU_DT), dob), dv_upd)
                dcr_ref[j, h:h + 1, :] += -jnp.sum(ds, axis=0, keepdims=True)
                rs_scr[h] += jnp.sum(ds, axis=-1, keepdims=True)
            dq_scr[...] = dq
            dk_scr[rows, :] += dk_upd
            dv_scr[rows, :] += dv_upd

        def alive(j):
            return _fox_alive(reach, lse_h, cr_ref, j, tk)

        def step(state):
            j = i - state[0]
            tile(j, False)
            return state[0] + 1, alive(j - 1)

        tile(i, True)
        lax.while_loop(lambda s: (s[0] <= i) & (s[1] > 0), step, (jnp.int32(1), alive(i - 1)))
        dq_ref[...] = dq_scr[...]
        lane = lax.broadcasted_iota(jnp.int32, (1, LANES), 1)
        dcc = jnp.zeros((tq, LANES), F32)
        for h in range(GROUP_HEADS):
            dcc = jnp.where(lane == FOX_COL + h, rs_scr[h], dcc)
        dcc_ref[...] = dcc

        @pl.when(i == nq - 1)
        def _():
            pltpu.sync_copy(dk_scr, dk_hbm)
            pltpu.sync_copy(dv_scr, dv_hbm)

    qrow = lambda i: (i, 0)
    resident = lambda blk: pl.BlockSpec((t, c), lambda i: (0, blk), pipeline_mode=pl.Buffered(1))
    hbm = pl.BlockSpec(memory_space=pl.ANY)
    call = dict(
        name=name, grid=(nq,),
        in_specs=[pl.BlockSpec((tq, c), lambda i: (i, qb)), resident(qb + 1), resident(qb + 2),
                  pl.BlockSpec((tq, LANES), qrow),
                  pl.BlockSpec((nk, 8, tk), lambda i: (0, 0, 0), pipeline_mode=pl.Buffered(1)),
                  pl.BlockSpec((tq, c), qrow), pl.BlockSpec((tq, LANES), qrow), pl.BlockSpec((tq, c), qrow)],
        out_specs=[pl.BlockSpec((tq, c), qrow), hbm, hbm, pl.BlockSpec((tq, LANES), qrow),
                   pl.BlockSpec((nk, 8, tk), lambda i: (0, 0, 0))],
        out_shape=[jax.ShapeDtypeStruct((t, c), F32), jax.ShapeDtypeStruct((t, c), F32),
                   jax.ShapeDtypeStruct((t, c), F32), jax.ShapeDtypeStruct((t, LANES), F32),
                   jax.ShapeDtypeStruct((nk, 8, tk), F32)],
        scratch_shapes=[pltpu.VMEM((tq, c), F32), pltpu.VMEM((GROUP_HEADS, tq, 1), F32),
                        pltpu.VMEM((t, c), F32), pltpu.VMEM((t, c), F32), pltpu.VMEM((1, LANES), F32)],
        compiler_params=_cparams(("arbitrary",)),
    )
    outs, got = carry_comm(call, body, (proj, proj, proj, cum, cr3, o, lse, do), comm, 5, *_grid_ends(nq))
    return (*outs, got)


SB_DEAD = -110.0


def _sb_logs(z, strict):
    tt = jnp.log(1.0 + jnp.exp(-jnp.abs(z)))
    log_keep = jnp.where(strict, -(jnp.maximum(z, 0.0) + tt), 0.0)
    log_beta = jnp.minimum(z, 0.0) - tt
    return log_keep, log_beta


def _tri(n, upper):
    a = lax.broadcasted_iota(jnp.int32, (n, n), 0)
    b = lax.broadcasted_iota(jnp.int32, (n, n), 1)
    return jnp.where((a < b) if upper else (a > b), 1.0, 0.0).astype(MXU_DT)


def _sb_carry_lane(jj, h):
    return GROUP_HEADS * jj + h


def sb_fwd(proj, *, name, tq=256, tk=256, comm=None):
    t = proj.shape[0]
    c = GROUP_WIDTH
    tq, tk, nq, nk = _attn_tiles(t, tq, tk)
    assert nk * GROUP_HEADS <= LANES
    qb = P_SB // c
    scale = HEAD_DIM ** -0.5

    def body(q_ref, k_ref, v_ref, o_ref, rs_ref, r_scr, acc_scr):
        i = pl.program_id(0)
        last = ((i + 1) * tq - 1) // tk
        masks = _head_masks(c)
        q = q_ref[...]
        qh = [jnp.where(masks[h], q, 0.0).astype(MXU_DT) for h in range(GROUP_HEADS)]
        lane = lax.broadcasted_iota(jnp.int32, (1, LANES), 1)
        later = _tri(tk, upper=False)
        r_scr[...] = jnp.zeros_like(r_scr)
        acc_scr[...] = jnp.zeros_like(acc_scr)
        rs_ref[...] = jnp.full((tq, LANES), 2.0 * SB_DEAD, F32)

        def step(state):
            jj, _ = state
            j = last - jj
            rows = pl.ds(pl.multiple_of(j * tk, tk), tk)
            kb = k_ref[rows, :].astype(MXU_DT)
            vb = v_ref[rows, :].astype(MXU_DT)
            row = i * tq + lax.broadcasted_iota(jnp.int32, (tq, tk), 0)
            col = j * tk + lax.broadcasted_iota(jnp.int32, (tq, tk), 1)
            strict = col < row
            acc = acc_scr[...]
            rs = rs_ref[...]
            for h in range(GROUP_HEADS):
                z = _dot_nt(qh[h], kb) * scale
                log_keep, log_beta = _sb_logs(z, strict)
                r_old = r_scr[h]
                rs = jnp.where(lane == _sb_carry_lane(jj, h), r_old, rs)
                rest = r_old + _split_dot(log_keep, later)
                w = jnp.where(strict, jnp.exp(log_beta + rest), 0.0)
                acc = jnp.where(masks[h], acc + _dot(w.astype(MXU_DT), vb), acc)
                r_scr[h] = r_old + jnp.sum(log_keep, axis=-1, keepdims=True)
            acc_scr[...] = acc
            rs_ref[...] = rs
            return jj + 1, jnp.max(r_scr[...])

        lax.while_loop(lambda s: (s[0] <= last) & (s[1] > SB_DEAD), step, (jnp.int32(0), jnp.float32(0.0)))
        o_ref[...] = acc_scr[...]

    resident = lambda blk: pl.BlockSpec((t, c), lambda i: (0, blk), pipeline_mode=pl.Buffered(1))
    call = dict(
        name=name, grid=(nq,),
        in_specs=[pl.BlockSpec((tq, c), lambda i: (i, qb)), resident(qb + 1), resident(qb + 2)],
        out_specs=[pl.BlockSpec((tq, c), lambda i: (i, 0)), pl.BlockSpec((tq, LANES), lambda i: (i, 0))],
        out_shape=[jax.ShapeDtypeStruct((t, c), F32), jax.ShapeDtypeStruct((t, LANES), F32)],
        scratch_shapes=[pltpu.VMEM((GROUP_HEADS, tq, 1), F32), pltpu.VMEM((tq, c), F32)],
        compiler_params=_cparams(("arbitrary",)),
    )
    outs, got = carry_comm(call, body, (proj, proj, proj), comm, 2, *_grid_ends(nq))
    return (*outs, got)


def sb_bwd(proj, rsave, do, *, name, tq=256, tk=256, comm=None):
    t = proj.shape[0]
    c = GROUP_WIDTH
    tq, tk, nq, nk = _attn_tiles(t, tq, tk)
    qb = P_SB // c
    scale = HEAD_DIM ** -0.5

    def body(q_ref, k_ref, v_ref, rs_ref, do_ref, dq_ref, dk_hbm, dv_hbm, e_scr, dq_scr, dk_scr, dv_scr):
        i = pl.program_id(0)
        last = ((i + 1) * tq - 1) // tk
        masks = _head_masks(c)

        @pl.when(i == 0)
        def _():
            dk_scr[...] = jnp.zeros_like(dk_scr)
            dv_scr[...] = jnp.zeros_like(dv_scr)

        e_scr[...] = jnp.zeros_like(e_scr)
        dq_scr[...] = jnp.zeros_like(dq_scr)
        q = q_ref[...]
        qf = q.astype(MXU_DT)
        qh = [jnp.where(masks[h], q, 0.0).astype(MXU_DT) for h in range(GROUP_HEADS)]
        do = do_ref[...]
        dob = do.astype(MXU_DT)
        doh = [jnp.where(masks[h], do, 0.0).astype(MXU_DT) for h in range(GROUP_HEADS)]
        later = _tri(tk, upper=False)
        earlier = _tri(tk, upper=True)
        rs = rs_ref[...]
        lane = lax.broadcasted_iota(jnp.int32, (1, LANES), 1)
        visited = jnp.where(jnp.max(rs, axis=0, keepdims=True) > SB_DEAD, (lane // GROUP_HEADS + 1).astype(F32), 0.0)
        n_visited = jnp.minimum(jnp.max(visited).astype(jnp.int32), last + 1)

        def step(it, carry):
            jj = n_visited - 1 - it
            j = last - jj
            rows = pl.ds(pl.multiple_of(j * tk, tk), tk)
            kb = k_ref[rows, :].astype(MXU_DT)
            vb = v_ref[rows, :].astype(MXU_DT)
            row = i * tq + lax.broadcasted_iota(jnp.int32, (tq, tk), 0)
            col = j * tk + lax.broadcasted_iota(jnp.int32, (tq, tk), 1)
            strict = col < row
            dq = dq_scr[...]
            dk_upd = jnp.zeros((tk, c), F32)
            dv_upd = jnp.zeros((tk, c), F32)
            for h in range(GROUP_HEADS):
                z = _dot_nt(qh[h], kb) * scale
                log_keep, log_beta = _sb_logs(z, strict)
                r_h = jnp.sum(jnp.where(lane == _sb_carry_lane(jj, h), rs, 0.0), axis=-1, keepdims=True)
                rest = r_h + _split_dot(log_keep, later)
                w = jnp.where(strict, jnp.exp(log_beta + rest), 0.0)
                e = w * _dot_nt(doh[h], vb)
                e_old = e_scr[h]
                dkeep = e_old + _split_dot(e, earlier)
                dz = jnp.where(strict, e * jnp.exp(log_keep) - dkeep * jnp.exp(log_beta), 0.0)
                dzb = dz.astype(MXU_DT)
                dq = jnp.where(masks[h], dq + _dot(dzb, kb) * scale, dq)
                dk_upd = jnp.where(masks[h], _dot_tn(dzb, qf) * scale, dk_upd)
                dv_upd = jnp.where(masks[h], _dot_tn(w.astype(MXU_DT), dob), dv_upd)
                e_scr[h] = e_old + jnp.sum(e, axis=-1, keepdims=True)
            dq_scr[...] = dq
            dk_scr[rows, :] += dk_upd
            dv_scr[rows, :] += dv_upd
            return carry

        lax.fori_loop(0, n_visited, step, 0)
        dq_ref[...] = dq_scr[...]

        @pl.when(i == nq - 1)
        def _():
            pltpu.sync_copy(dk_scr, dk_hbm)
            pltpu.sync_copy(dv_scr, dv_hbm)

    qrow = lambda i: (i, 0)
    resident = lambda blk: pl.BlockSpec((t, c), lambda i: (0, blk), pipeline_mode=pl.Buffered(1))
    hbm = pl.BlockSpec(memory_space=pl.ANY)
    call = dict(
        name=name, grid=(nq,),
        in_specs=[pl.BlockSpec((tq, c), lambda i: (i, qb)), resident(qb + 1), resident(qb + 2),
                  pl.BlockSpec((tq, LANES), qrow), pl.BlockSpec((tq, c), qrow)],
        out_specs=[pl.BlockSpec((tq, c), qrow), hbm, hbm],
        out_shape=[jax.ShapeDtypeStruct((t, c), F32)] * 3,
        scratch_shapes=[pltpu.VMEM((GROUP_HEADS, tq, 1), F32), pltpu.VMEM((tq, c), F32),
                        pltpu.VMEM((t, c), F32), pltpu.VMEM((t, c), F32)],
        compiler_params=_cparams(("arbitrary",)),
    )
    outs, got = carry_comm(call, body, (proj, proj, proj, rsave, do), comm, 3, *_grid_ends(nq))
    return (*outs, got)


A_COL, B_COL = 0, 4
Z_BLK = P_Z // GROUP_WIDTH
GDN_CHUNKS_PER_STEP = 8


NN = (((1,), (0,)), ((), ()))
NT = (((1,), (1,)), ((), ()))
TN = (((0,), (0,)), ((), ()))


def _terms(x, n):
    out, rem = [], x
    for _ in range(n):
        t = rem.astype(MXU_DT)
        out.append(t)
        rem = rem - t.astype(F32)
    return out


def _dotp(a, b, dims, a_terms=2, b_terms=2):
    at, bt = _terms(a, a_terms), _terms(b, b_terms)
    out = None
    for i, x in enumerate(at):
        for j, y in enumerate(bt):
            if i + j < max(a_terms, b_terms):
                r = lax.dot_general(x, y, dims, preferred_element_type=F32)
                out = r if out is None else out + r
    return out


def _silu(x):
    return x * _sigmoid(x)


def _dsilu(x):
    s = _sigmoid(x)
    return s * (1.0 + x * (1.0 - s))


def _head_sum(x, masks):
    out = jnp.zeros_like(x)
    for m in masks:
        out = jnp.where(m, jnp.sum(jnp.where(m, x, 0.0), axis=-1, keepdims=True), out)
    return out


def _expand(cols, col0, masks):
    out = jnp.zeros((cols.shape[0], GROUP_WIDTH), F32)
    for h, m in enumerate(masks):
        out = jnp.where(m, cols[:, col0 + h:col0 + h + 1], out)
    return out


def _reduce(x, col0, masks):
    lane = lax.broadcasted_iota(jnp.int32, (1, LANES), 1)
    out = jnp.zeros((x.shape[0], LANES), F32)
    for h, m in enumerate(masks):
        out = jnp.where(lane == col0 + h, jnp.sum(jnp.where(m, x, 0.0), axis=-1, keepdims=True), out)
    return out


def _block_ones():
    ri = lax.broadcasted_iota(jnp.int32, (GROUP_WIDTH, GROUP_WIDTH), 0) // HEAD_DIM
    ci = lax.broadcasted_iota(jnp.int32, (GROUP_WIDTH, GROUP_WIDTH), 1) // HEAD_DIM
    return jnp.where(ri == ci, 1.0, 0.0).astype(F32)


def _blk(x, hs):
    return jnp.concatenate([x] * GROUP_HEADS, axis=0) * hs


def _unblk(m, hs):
    mm = m * hs
    c = GDN_CHUNK
    return mm[0:c] + mm[c:2 * c] + mm[2 * c:3 * c] + mm[3 * c:4 * c]


def _row_mask4():
    ri = lax.broadcasted_iota(jnp.int32, (GROUP_WIDTH, LANES), 0) // HEAD_DIM
    ci = lax.broadcasted_iota(jnp.int32, (GROUP_WIDTH, LANES), 1)
    return jnp.where(ri + A_COL == ci, 1.0, 0.0).astype(F32)


def _lockstep(gens):
    results = [None] * len(gens)
    live = list(range(len(gens)))
    while live:
        for i in list(live):
            try:
                next(gens[i])
            except StopIteration as stop:
                results[i] = stop.value
                live.remove(i)
    return results


def _gdn_chunk(xc, small, avec, dtvec, state, masks, hs):
    (f,) = _lockstep([_gdn_local(xc, small, avec, dtvec, masks, hs)])
    return _gdn_recur(f, state, hs)


def _gdn_local(xc, small, avec, dtvec, masks, hs):
    c = GDN_CHUNK
    w = GROUP_WIDTH
    b16 = lambda v: v.astype(MXU_DT)
    f = {}
    xq, xk, xv = xc[:, :w], xc[:, w:2 * w], xc[:, 2 * w:]
    qs, ks, v = _silu(xq), _silu(xk), _silu(xv)
    rq = lax.rsqrt(_head_sum(qs * qs, masks) + L2_EPS)
    rk = lax.rsqrt(_head_sum(ks * ks, masks) + L2_EPS)
    qn = qs * rq
    k = ks * rk
    q = qn * (HEAD_DIM ** -0.5)
    xg = small + dtvec
    sp = jnp.maximum(xg, 0.0) + jnp.log(1.0 + jnp.exp(-jnp.abs(xg)))
    g128 = -avec * sp
    beta128 = _sigmoid(small)
    ri = lax.broadcasted_iota(jnp.int32, (c, c), 0)
    ci = lax.broadcasted_iota(jnp.int32, (c, c), 1)
    tril = jnp.where(ri >= ci, 1.0, 0.0).astype(F32)
    gam128 = _dotp(tril, g128, NN, 1, 3)
    yield
    gam = _expand(gam128, A_COL, masks)
    bfull = _expand(beta128, B_COL, masks)
    mask4 = _row_mask4()
    ones = jnp.ones((c, LANES), F32)
    gam_row = _dotp(ones, jnp.concatenate([gam128] * GROUP_HEADS, axis=0) * mask4, NT, 1, 3)
    yield
    li = lax.broadcasted_iota(jnp.int32, (c, w), 0)
    lj = lax.broadcasted_iota(jnp.int32, (c, w), 1) % HEAD_DIM
    incl = li >= lj
    strict = li > lj
    dmat = jnp.exp(jnp.where(incl, gam - gam_row, NEG_BIG))
    egam = jnp.exp(gam)
    glast = gam[c - 1:c, :]
    ekd = jnp.exp(glast - gam)
    kb = k * bfull
    vb = v * bfull
    kbg = kb * egam
    qd = q * egam
    kd = k * ekd
    kblk = b16(_blk(k, hs))
    araw = _dot_nt(b16(kb), kblk)
    qk = _dot_nt(b16(q), kblk)
    yield
    a = jnp.where(strict, araw * dmat, 0.0)
    tm = jnp.where(li == lj, 1.0, 0.0) - a
    p = a
    for _ in range(5):
        p = _dotp(p, _blk(p, hs), NN)
        yield
        tm = tm + _dotp(tm, _blk(p, hs), NN)
        yield
    tm16 = b16(tm)
    u = _dot(tm16, b16(_blk(vb, hs)))
    wm = _dot(tm16, b16(_blk(kbg, hs)))
    aqk = jnp.where(incl, qk * dmat, 0.0)
    f.update(xq=xq, xk=xk, xv=xv, v=v, rq=rq, rk=rk, qn=qn, k=k, q=q, xg=xg, g128=g128, beta128=beta128,
             tril=tril, gam=gam, bfull=bfull, mask4=mask4, ones=ones, incl=incl, strict=strict, li=li,
             dmat=dmat, egam=egam, glast=glast, ekd=ekd, kb=kb, vb=vb, kbg=kbg, qd=qd, kd=kd, kblk=kblk,
             araw=araw, tm=tm, tm16=tm16, wm=wm, qk=qk, aqk=aqk, u=u)
    return f


def _gdn_recur(f, state, hs):
    b16 = lambda v: v.astype(MXU_DT)
    s16 = b16(state)
    vn = f["u"] - _dot(b16(f["wm"]), s16)
    o = _dot(b16(f["qd"]), s16) + _dot(b16(f["aqk"]), b16(_blk(vn, hs)))
    s_new = state * jnp.exp(f["glast"]) + hs * _dot_tn(b16(f["kd"]), b16(vn))
    f.update(s16=s16, vn=vn, o=o, s_new=s_new)
    return f


def _decay_rate(a_log):
    lane = lax.broadcasted_iota(jnp.int32, a_log.shape, 1)
    return jnp.where((lane >= A_COL) & (lane < A_COL + GROUP_HEADS), jnp.exp(a_log), 0.0)


def _gdn_post(o, z, ng, masks):
    r = lax.rsqrt(_head_sum(o * o, masks) * (1.0 / HEAD_DIM) + RMS_EPS)
    on = o * r
    return on, r, on * ng * _silu(z)


def gdn_fwd(cqkv, proj, avec, dtvec, ng, *, name, comm=None):
    t = cqkv.shape[0]
    c = GDN_CHUNK
    w = GROUP_WIDTH
    n = t // c

    def body(x_ref, z_ref, sm_ref, a_ref, dt_ref, ng_ref, y_ref, st_ref, s_scr):
        @pl.when(pl.program_id(0) == 0)
        def _():
            s_scr[...] = jnp.zeros_like(s_scr)

        masks = _head_masks(w)
        hs = _block_ones()
        avec_v = _decay_rate(a_ref[...])
        rows = [pl.ds(k * c, c) for k in range(sub)]
        fs = _lockstep([_gdn_local(x_ref[r, :], sm_ref[r, :], avec_v, dt_ref[...], masks, hs) for r in rows])
        state = s_scr[...]
        for k, r in enumerate(rows):
            st_ref[k] = state
            f = _gdn_recur(fs[k], state, hs)
            _, _, y = _gdn_post(f["o"], z_ref[r, :], ng_ref[...], masks)
            y_ref[r, :] = y
            state = f["s_new"]
        s_scr[...] = state

    sub = GDN_CHUNKS_PER_STEP if n % GDN_CHUNKS_PER_STEP == 0 else 1
    rows, steps = c * sub, n // sub
    vec = pl.BlockSpec((1, LANES), lambda i: (0, 0))
    call = dict(
        name=name, grid=(steps,),
        in_specs=[pl.BlockSpec((rows, 3 * w), lambda i: (i, 0)),
                  pl.BlockSpec((rows, w), lambda i: (i, Z_BLK)),
                  pl.BlockSpec((rows, LANES), lambda i: (i, SMALL_BLK)),
                  vec, vec, pl.BlockSpec((1, w), lambda i: (0, 0))],
        out_specs=[pl.BlockSpec((rows, w), lambda i: (i, 0)), pl.BlockSpec((sub, w, w), lambda i: (i, 0, 0))],
        out_shape=[jax.ShapeDtypeStruct((t, w), F32), jax.ShapeDtypeStruct((n, w, w), F32)],
        scratch_shapes=[pltpu.VMEM((w, w), F32)],
        compiler_params=_cparams(("arbitrary",)),
    )
    outs, got = carry_comm(call, body, (cqkv, proj, proj, avec, dtvec, ng), comm, 2, *_grid_ends(steps))
    return (*outs, got)


def gdn_bwd(cqkv, proj, avec, dtvec, ng, states, dy, *, name, comm=None):
    t = cqkv.shape[0]
    c = GDN_CHUNK
    w = GROUP_WIDTH
    n = t // c
    b16 = lambda v: v.astype(MXU_DT)

    def body(x_ref, z_ref, sm_ref, a_ref, dt_ref, ng_ref, st_ref, dy_ref,
             dx_ref, dz_ref, dsm_ref, dng_ref, dal_ref, ddt_ref, ds_scr):
        @pl.when(pl.program_id(0) == 0)
        def _():
            ds_scr[...] = jnp.zeros_like(ds_scr)
            dng_ref[...] = jnp.zeros_like(dng_ref)
            dal_ref[...] = jnp.zeros_like(dal_ref)
            ddt_ref[...] = jnp.zeros_like(ddt_ref)

        masks = _head_masks(w)
        hs = _block_ones()
        avec_v = _decay_rate(a_ref[...])
        rows = [pl.ds(k * c, c) for k in range(sub)]
        fs = _lockstep([_gdn_local(x_ref[r, :], sm_ref[r, :], avec_v, dt_ref[...], masks, hs) for r in rows])
        fs = [_gdn_recur(f, st_ref[k], hs) for k, f in enumerate(fs)]
        _lockstep([chunk(fs[k], st_ref[k], avec_v, masks, hs, z_ref.at[r, :], ng_ref, dy_ref.at[r, :], dx_ref.at[r, :],
                         dz_ref.at[r, :], dsm_ref.at[r, :], dng_ref, dal_ref, ddt_ref, ds_scr)
                   for k, r in reversed(list(enumerate(rows)))])

    def chunk(f, state, avec_v, masks, hs, z_ref, ng_ref, dy_ref, dx_ref, dz_ref, dsm_ref, dng_ref, dal_ref, ddt_ref,
              ds_scr):
        z = z_ref[...]
        ng_v = ng_ref[...]
        dy_v = dy_ref[...]
        on, r, _ = _gdn_post(f["o"], z, ng_v, masks)
        sz = _silu(z)
        dz_ref[...] = dy_v * on * ng_v * _dsilu(z)
        d_on = dy_v * ng_v * sz
        dng_ref[...] += jnp.sum(dy_v * on * sz, axis=0, keepdims=True)
        do = r * (d_on - on * _head_sum(d_on * on, masks) * (1.0 / HEAD_DIM))
        do16 = b16(do)
        dsn = ds_scr[...]
        dsn16 = b16(dsn)
        s16, vn, kd, qd, wm = f["s16"], f["vn"], f["kd"], f["qd"], f["wm"]
        k, q, kblk, tm, tm16 = f["k"], f["q"], f["kblk"], f["tm"], f["tm16"]
        dmat, egam, glast, gam = f["dmat"], f["egam"], f["glast"], f["gam"]
        incl, strict, li = f["incl"], f["strict"], f["li"]
        vn16 = b16(vn)
        dvn = _unblk(_dot_tn(b16(f["aqk"]), do16), hs) + _dot(b16(kd), dsn16)
        daqk = jnp.where(incl, _dot_nt(do16, b16(_blk(vn, hs))), 0.0)
        dqd = _dot_nt(do16, s16)
        dvn16 = b16(dvn)
        ds_scr[...] = hs * (_dot_tn(b16(qd), do16) - _dot_tn(b16(wm), dvn16)) + dsn * jnp.exp(glast)
        yield
        dkd = _dot_nt(vn16, dsn16)
        dglast = jnp.sum(dsn * state, axis=0, keepdims=True) * jnp.exp(glast)
        du16 = dvn16
        dw16 = b16(-_dot_nt(dvn16, s16))
        yield
        dqk16 = b16(daqk * dmat)
        ddm = daqk * f["qk"]
        dq = _dot(dqk16, kblk)
        dk = _unblk(_dot_tn(dqk16, b16(q)), hs)
        dtm = _dot_nt(du16, b16(_blk(f["vb"], hs))) + _dot_nt(dw16, b16(_blk(f["kbg"], hs)))
        dvb = _unblk(_dot_tn(tm16, du16), hs)
        dkbg = _unblk(_dot_tn(tm16, dw16), hs)
        yield
        xx = _unblk(_dotp(tm, dtm, TN), hs)
        yield
        da = jnp.where(strict, -_dotp(xx, _blk(tm, hs), NT), 0.0)
        yield
        daraw16 = b16(da * dmat)
        ddm = ddm + da * f["araw"]
        dkb = _dot(daraw16, kblk)
        dk = dk + _unblk(_dot_tn(daraw16, b16(f["kb"])), hs)
        yield
        tcol = ddm * dmat
        dgam = tcol
        dgam128_row = _dotp(-tcol, f["ones"], TN, 2, 1) * f["mask4"]
        yield
        dgam128_row = (dgam128_row[0:c] + dgam128_row[c:2 * c] + dgam128_row[2 * c:3 * c] + dgam128_row[3 * c:4 * c])
        dk = dk + dkd * f["ekd"]
        tt = dkd * kd
        dgam = dgam - tt
        dglast = dglast + jnp.sum(tt, axis=0, keepdims=True)
        dq = dq + dqd * egam
        dgam = dgam + dqd * qd
        dkb = dkb + dkbg * egam
        dgam = dgam + dkbg * f["kbg"]
        dk = dk + dkb * f["bfull"]
        dbf = dkb * k + dvb * f["v"]
        dv = dvb * f["bfull"]
        dgam = dgam + jnp.where(li == c - 1, dglast, 0.0)
        beta128 = f["beta128"]
        db128 = _reduce(dbf, B_COL, masks) * beta128 * (1.0 - beta128)
        dgam128 = _reduce(dgam, A_COL, masks) + dgam128_row
        dg128 = _dotp(f["tril"], dgam128, TN, 1, 2)
        yield
        dxg = dg128 * (-avec_v * _sigmoid(f["xg"]))
        lane = lax.broadcasted_iota(jnp.int32, (1, LANES), 1)
        dsm_ref[...] = jnp.where(lane < B_COL, dxg, db128)
        ddt_ref[...] += jnp.sum(dxg, axis=0, keepdims=True)
        dal_ref[...] += jnp.sum(dg128 * f["g128"], axis=0, keepdims=True)
        dqn = dq * (HEAD_DIM ** -0.5)
        dqs = f["rq"] * (dqn - f["qn"] * _head_sum(dqn * f["qn"], masks))
        dks = f["rk"] * (dk - k * _head_sum(dk * k, masks))
        dx_ref[:, :w] = dqs * _dsilu(f["xq"])
        dx_ref[:, w:2 * w] = dks * _dsilu(f["xk"])
        dx_ref[:, 2 * w:] = dv * _dsilu(f["xv"])

    sub = GDN_CHUNKS_PER_STEP if n % GDN_CHUNKS_PER_STEP == 0 else 1
    rows, steps = c * sub, n // sub
    vec = pl.BlockSpec((1, LANES), lambda i: (0, 0))
    rev = lambda blk: (lambda i: (steps - 1 - i, blk))
    call = dict(
        name=name, grid=(steps,),
        in_specs=[pl.BlockSpec((rows, 3 * w), rev(0)),
                  pl.BlockSpec((rows, w), rev(Z_BLK)),
                  pl.BlockSpec((rows, LANES), rev(SMALL_BLK)),
                  vec, vec, pl.BlockSpec((1, w), lambda i: (0, 0)),
                  pl.BlockSpec((sub, w, w), lambda i: (steps - 1 - i, 0, 0)),
                  pl.BlockSpec((rows, w), rev(0))],
        out_specs=[pl.BlockSpec((rows, 3 * w), rev(0)), pl.BlockSpec((rows, w), rev(0)),
                   pl.BlockSpec((rows, LANES), rev(0)),
                   pl.BlockSpec((1, w), lambda i: (0, 0)), vec, vec],
        out_shape=[jax.ShapeDtypeStruct((t, 3 * w), F32), jax.ShapeDtypeStruct((t, w), F32),
                   jax.ShapeDtypeStruct((t, LANES), F32), jax.ShapeDtypeStruct((1, w), F32),
                   jax.ShapeDtypeStruct((1, LANES), F32), jax.ShapeDtypeStruct((1, LANES), F32)],
        scratch_shapes=[pltpu.VMEM((w, w), F32)],
        compiler_params=_cparams(("arbitrary",)),
    )
    outs, got = carry_comm(call, body, (cqkv, proj, proj, avec, dtvec, ng, states, dy), comm, 6, *_grid_ends(steps))
    return (*outs, got)


def adamw(w, m, v, gslots, *, row0=0, name):
    r, c = w.shape
    s = gslots.shape[0]
    tr = _tile(r, 64, 8)
    assert row0 % tr == 0 and gslots.shape[2] == c
    rb = row0 // tr
    c1 = 1.0 - ADAM_B1 ** ADAM_STEP
    c2 = 1.0 - ADAM_B2 ** ADAM_STEP

    def body(w_ref, m_ref, v_ref, gs_ref, g_ref, d_ref, mo_ref, vo_ref):
        g = gs_ref[0].astype(F32)
        for k in range(1, s):
            g = g + gs_ref[k].astype(F32)
        m_new = ADAM_B1 * m_ref[...] + (1.0 - ADAM_B1) * g
        v_new = ADAM_B2 * v_ref[...] + (1.0 - ADAM_B2) * (g * g)
        m_hat = m_new / c1
        v_hat = v_new / c2
        g_ref[...] = g
        mo_ref[...] = m_new
        vo_ref[...] = v_new
        d_ref[...] = -ADAM_LR * (m_hat / (jnp.sqrt(v_hat) + ADAM_EPS) + ADAM_WD * w_ref[...])

    row = pl.BlockSpec((tr, c), lambda i: (i, 0))
    return pl.pallas_call(
        body, name=name, grid=(r // tr,),
        in_specs=[row, row, row, pl.BlockSpec((s, tr, c), lambda i: (0, rb + i, 0))],
        out_specs=[row] * 4,
        out_shape=[jax.ShapeDtypeStruct((r, c), F32)] * 4,
        compiler_params=_cparams(("parallel",)),
    )(w, m, v, gslots)


def slot_sum(slots, *, name):
    s, r, c = slots.shape

    def body(s_ref, o_ref):
        acc = s_ref[0]
        for k in range(1, s):
            acc = acc + s_ref[k]
        o_ref[...] = acc

    return pl.pallas_call(
        body, name=name, grid=(1,),
        in_specs=[pl.BlockSpec((s, r, c), lambda i: (0, 0, 0))],
        out_specs=pl.BlockSpec((r, c), lambda i: (0, 0)),
        out_shape=jax.ShapeDtypeStruct((r, c), F32),
        compiler_params=_cparams(("arbitrary",)),
    )(slots)


class Comm:
    def __init__(self, srcs, broadcast):
        self.srcs = list(srcs)
        self.broadcast = [broadcast] * len(self.srcs) if isinstance(broadcast, bool) else list(broadcast)
        self.n = len(self.srcs)
        self.out_shapes = [jax.ShapeDtypeStruct(((N_DEV,) + s.shape) if b else s.shape, s.dtype)
                           for s, b in zip(self.srcs, self.broadcast)]
        self.sems = [pltpu.SemaphoreType.DMA((self.n,))] * 3

    def _local(self, src_refs, out_refs, loc_sem, a, me):
        src = src_refs[a] if self.broadcast[a] else src_refs[a].at[me]
        return pltpu.make_async_copy(src, out_refs[a].at[me], loc_sem.at[a])

    def start(self, src_refs, out_refs, send_sem, recv_sem, loc_sem):
        x, y, c = lax.axis_index("x"), lax.axis_index("y"), lax.axis_index("c")
        me = 4 * x + 2 * y + c
        for a in range(self.n):
            self._local(src_refs, out_refs, loc_sem, a, me).start()
        for d in range(1, N_DEV):
            px, py, pc = x ^ ((d >> 2) & 1), y ^ ((d >> 1) & 1), c ^ (d & 1)
            peer = 4 * px + 2 * py + pc
            for a in range(self.n):
                src = src_refs[a] if self.broadcast[a] else src_refs[a].at[peer]
                pltpu.make_async_remote_copy(
                    src_ref=src, dst_ref=out_refs[a].at[me],
                    send_sem=send_sem.at[a], recv_sem=recv_sem.at[a],
                    device_id=(px, py, pc), device_id_type=pl.DeviceIdType.MESH).start()

    def wait(self, src_refs, out_refs, send_sem, recv_sem, loc_sem):
        x, y, c = lax.axis_index("x"), lax.axis_index("y"), lax.axis_index("c")
        me = 4 * x + 2 * y + c
        for a in range(self.n):
            seven = out_refs[a].at[pl.ds(0, N_DEV - 1)]
            pltpu.make_async_remote_copy(
                src_ref=seven, dst_ref=seven, send_sem=send_sem.at[a], recv_sem=recv_sem.at[a],
                device_id=(x, y, c), device_id_type=pl.DeviceIdType.MESH).wait()
            self._local(src_refs, out_refs, loc_sem, a, me).wait()


def exchange(srcs, *, broadcast, name):
    comm = Comm(srcs, broadcast)
    n = comm.n

    def body(*refs):
        src_refs, out_refs, sems = refs[:n], refs[n:2 * n], refs[2 * n:]
        comm.start(src_refs, out_refs, *sems)
        comm.wait(src_refs, out_refs, *sems)

    anyspec = pl.BlockSpec(memory_space=pl.ANY)
    return pl.pallas_call(
        body, name=name,
        in_specs=[anyspec] * n, out_specs=[anyspec] * n, out_shape=comm.out_shapes,
        scratch_shapes=comm.sems,
        compiler_params=pltpu.CompilerParams(has_side_effects=True),
    )(*srcs)


def carry_comm(call_kwargs, body, args, comm, n_out, is_first, is_last):
    if comm is None:
        return pl.pallas_call(body, **call_kwargs)(*args), []
    n_in, nc = len(args), comm.n
    n_scr = len(call_kwargs["scratch_shapes"])
    anyspec = pl.BlockSpec(memory_space=pl.ANY)

    def wrapped(*refs):
        ins, csrc = refs[:n_in], refs[n_in:n_in + nc]
        outs = refs[n_in + nc:n_in + nc + n_out]
        cout = refs[n_in + nc + n_out:n_in + 2 * nc + n_out]
        rest = refs[n_in + 2 * nc + n_out:]
        scr, sems = rest[:n_scr], rest[n_scr:]

        @pl.when(is_first())
        def _():
            comm.start(csrc, cout, *sems)

        body(*ins, *outs, *scr)

        @pl.when(is_last())
        def _():
            comm.wait(csrc, cout, *sems)

    kw = dict(call_kwargs)
    kw["in_specs"] = list(kw["in_specs"]) + [anyspec] * nc
    kw["out_specs"] = list(kw["out_specs"]) + [anyspec] * nc
    kw["out_shape"] = list(kw["out_shape"]) + comm.out_shapes
    kw["scratch_shapes"] = list(kw["scratch_shapes"]) + comm.sems
    cp = kw["compiler_params"]
    kw["compiler_params"] = pltpu.CompilerParams(dimension_semantics=cp.dimension_semantics,
                                                 vmem_limit_bytes=cp.vmem_limit_bytes, has_side_effects=True)
    res = pl.pallas_call(wrapped, **kw)(*args, *comm.srcs)
    return res[:n_out], res[n_out:]


def _pack(arrs):
    flat = []
    for a in arrs:
        f = a.reshape(-1).astype(F32)
        flat.append(jnp.pad(f, (0, (-f.shape[0]) % LANES)))
    buf = jnp.concatenate(flat)
    buf = jnp.pad(buf, (0, (-buf.shape[0]) % (8 * LANES)))
    return buf.reshape(-1, LANES)


def _unpack(buf, shapes):
    flat = buf.reshape(-1)
    out, off = [], 0
    for s in shapes:
        sz = int(np.prod(s))
        out.append(flat[off:off + sz].reshape(s))
        off += sz + (-sz) % LANES
    return out


def _win_to_aligned(w):
    o = np.cumsum((0,) + IN_SPLITS)
    seg = lambda i: w[..., o[i]:o[i + 1]]
    pad = jnp.zeros(w.shape[:-1] + (P_WIDTH - IN_WIDTH,), w.dtype)
    return jnp.concatenate([seg(0), seg(1), seg(4), seg(6), seg(7), seg(2), seg(3), seg(5), pad], axis=-1)


def _win_from_aligned(w):
    o = np.cumsum((0,) + IN_SPLITS)
    s = P_SMALL
    return jnp.concatenate([w[..., P_GDN:P_GDN + 768], w[..., P_Z:P_Z + 256], w[..., s:s + 4], w[..., s + 4:s + 8],
                            w[..., P_FOX:P_FOX + 768], w[..., s + 8:s + 12], w[..., P_CONF:P_CONF + 512],
                            w[..., P_SB:P_SB + 768]], axis=-1)


def _row128(vals, col0):
    return jnp.pad(vals.astype(F32)[None, :], ((0, 0), (col0, LANES - col0 - GROUP_HEADS)))


def _ffn_fwd(x, x16, w, n, tag, comm=None, on_comm=None):
    gu = mm(x16, w[f"gu{n}"], name=f"{tag}_gu", tm=1024, tn=512, tk=1024, out_dtype=MXU_DT, comm=comm)
    if comm is not None:
        gu, got = gu
        on_comm(got)
    h = act_fwd(gu, name=f"{tag}_act")
    y = mm(h, w[f"d{n}"], name=f"{tag}_down", tm=1024, tn=512, tk=D_FF)
    out, out16, xh, rs = ln_res_fwd(x, y, w[f"ln_ffn{n}_g"], w[f"ln_ffn{n}_b"], 0.5, name=f"{tag}_ln")
    return out, out16, (x16, gu, h, xh, rs)


def _ffn_bwd(dout, saved, w, n, tag, comm_dh=None, comm_dwgu=None, comm_dx=None):
    x, gu, h, xh, rs = saved
    wgu, wd = w[f"gu{n}"], w[f"d{n}"]
    got = [[], [], []]
    dz, dg, db = ln_res_bwd(dout, xh, rs, w[f"ln_ffn{n}_g"], name=f"{tag}_ln_bwd")
    dh = mm(dz, wd, mode="nt", alpha=0.5, name=f"{tag}_dh", tm=1024, tn=D_FF // 2, tk=1024, comm=comm_dh)
    if comm_dh is not None:
        dh, got[0] = dh
    dgu = act_bwd(gu, dh, name=f"{tag}_act_bwd")
    dwd = mm(h, dz, mode="tn", alpha=0.5, name=f"{tag}_dwd", tm=D_FF // 2, tn=1024, tk=512)
    c = comm_dwgu(dwd) if comm_dwgu is not None else None
    dwgu = mm(x, dgu, mode="tn", name=f"{tag}_dwgu", tm=1024, tn=D_FF // 2, tk=512, comm=c)
    if c is not None:
        dwgu, got[1] = dwgu
    c = comm_dx(dwgu) if comm_dx is not None else None
    dx = mm(dgu, wgu, mode="nt", add=dz, beta=DN_ALPHA, name=f"{tag}_dx", tm=1024, tn=1024, tk=D_FF // 2, comm=c)
    if c is not None:
        dx, got[2] = dx
    return dx, dwgu, dwd, dg, db, got


def _layer_fwd(x, x16, mem, w, tag, comm_ffn1=None, on_ffn1=None, comm_gdn=None, on_gdn=None, comm_fox=None,
               on_fox=None, comm_sb=None, on_sb=None):
    sv = {}
    x1, x1h, sv["ffn1"] = _ffn_fwd(x, x16, w, 1, f"{tag}_ffn1", comm=comm_ffn1, on_comm=on_ffn1)
    proj = mm(x1h, w["win"], name=f"{tag}_inproj", tm=1024, tn=640, tk=1024)
    cqkv = dwconv_fwd(proj, w["gdn_conv_w"], None, col0=P_GDN, name=f"{tag}_gdn_conv")
    ya, states, got = gdn_fwd(cqkv, proj, w["alog"], w["dtb"], w["ng"], name=f"{tag}_gdn", comm=comm_gdn)
    if on_gdn is not None:
        on_gdn(got)
    cum = fox_gate_fwd(proj, w["bf"], name=f"{tag}_fox_gate")
    cum_t = jnp.pad(cum[:, FOX_COL:FOX_COL + GROUP_HEADS].T, ((0, 8 - GROUP_HEADS), (0, 0)))
    yb, lse, got = fox_fwd(proj, cum, cum_t, name=f"{tag}_fox", comm=comm_fox)
    if on_fox is not None:
        on_fox(got)
    u = glu_fwd(proj, name=f"{tag}_glu")
    cc = dwconv_fwd(u, w["conf_dw_w"], w["conf_dw_b"], name=f"{tag}_conf_conv")
    yc = gn_silu_fwd(cc, w["conf_norm_g"], w["conf_norm_b"], name=f"{tag}_conf_norm")
    yd, rsave, got = sb_fwd(proj, name=f"{tag}_sb", comm=comm_sb)
    if on_sb is not None:
        on_sb(got)
    ycat = jnp.concatenate([ya, yb, yc, yd], axis=1).astype(MXU_DT)
    mix = mm(ycat, w["wout"], name=f"{tag}_outproj")
    x2, x2h, xh2, rs2 = ln_res_fwd(x1, mix, w["ln_mix_g"], w["ln_mix_b"], 1.0, name=f"{tag}_ln_mix")
    sv["mix"] = (x1h, proj, cqkv, states, cum, cum_t, yb, lse, u, cc, rsave, ycat, xh2, rs2)
    q = mm(x2h, w["wq"], name=f"{tag}_memq", out_dtype=MXU_DT)
    kv = mm(mem, w["wkv"], name=f"{tag}_memkv", tm=N_MEM, out_dtype=MXU_DT)
    att = memattn_fwd(q, kv, name=f"{tag}_memattn")
    mo = mm(att, w["wo"], name=f"{tag}_memo")
    x3, x3h, xh3, rs3 = ln_res_fwd(x2, mo, w["ln_mem_g"], w["ln_mem_b"], 1.0, name=f"{tag}_ln_mem")
    sv["mem"] = (x2h, q, kv, att, xh3, rs3)
    x4, x4h, sv["ffn2"] = _ffn_fwd(x3, x3h, w, 2, f"{tag}_ffn2")
    return x4, x4h, sv


def _layer_bwd(dx4, mem, sv, w, tag, plan, tail=None):
    t = dx4.shape[0]
    gr = {}
    dx3, gr["gu2"], gr["d2"], gr["ln_ffn2_g"], gr["ln_ffn2_b"], _ = _ffn_bwd(dx4, sv["ffn2"], w, 2, f"{tag}_ffn2")
    x2, q, kv, att, xh3, rs3 = sv["mem"]
    dz, gr["ln_mem_g"], gr["ln_mem_b"] = ln_res_bwd(dx3, xh3, rs3, w["ln_mem_g"], name=f"{tag}_ln_mem_bwd")
    datt = mm(dz, w["wo"], mode="nt", name=f"{tag}_datt", out_dtype=MXU_DT)
    gr["wo"] = mm(att, dz, mode="tn", name=f"{tag}_dwo", tk=512)
    dq, dkv = memattn_bwd(q, kv, datt, name=f"{tag}_memattn_bwd")
    gr["wq"] = mm(x2, dq, mode="tn", name=f"{tag}_dwq", tk=512)
    gr["wkv"] = mm(mem, dkv, mode="tn", name=f"{tag}_dwkv", tk=N_MEM)
    dx2 = mm(dq, w["wq"], mode="nt", add=dz, beta=DN_ALPHA, name=f"{tag}_dx2")
    x1, proj, cqkv, states, cum, cum_t, yb, lse, u, cc, rsave, ycat, xh2, rs2 = sv["mix"]
    dz, gr["ln_mix_g"], gr["ln_mix_b"] = ln_res_bwd(dx2, xh2, rs2, w["ln_mix_g"], name=f"{tag}_ln_mix_bwd")
    dycat = mm(dz, w["wout"], mode="nt", name=f"{tag}_dycat")
    gr["wout"] = mm(ycat, dz, mode="tn", name=f"{tag}_dwout", tk=512)
    comm_sb, comm_fox, comm_gdn = plan(gr)
    gw = GROUP_WIDTH
    dya, dyb, dyc, dyd = (dycat[:, i * gw:(i + 1) * gw] for i in range(4))
    dq_d, dk_d, dv_d, got_sb = sb_bwd(proj, rsave, dyd, name=f"{tag}_sb_bwd", comm=comm_sb)
    dcc, gr["conf_norm_g"], gr["conf_norm_b"] = gn_silu_bwd(cc, w["conf_norm_g"], w["conf_norm_b"], dyc,
                                                            name=f"{tag}_conf_norm_bwd")
    du, gr["conf_dw_w"], gr["conf_dw_b"] = dwconv_bwd(dcc, u, w["conf_dw_w"], name=f"{tag}_conf_conv_bwd")
    dglu = glu_bwd(proj, du, name=f"{tag}_glu_bwd")
    dq_b, dk_b, dv_b, dcc, dcr, got_fox = fox_bwd(proj, cum, cum_t, yb, lse, dyb, name=f"{tag}_fox_bwd", comm=comm_fox)
    dcum = dcc + jnp.pad(dcr[:, :GROUP_HEADS, :].transpose(0, 2, 1).reshape(t, GROUP_HEADS),
                   ((0, 0), (FOX_COL, LANES - FOX_COL - GROUP_HEADS)))
    dsm_f, dbf = fox_gate_bwd(dcum, proj, w["bf"], name=f"{tag}_fox_gate_bwd")
    gr["fox_b_f"] = dbf[0, FOX_COL:FOX_COL + GROUP_HEADS]
    dcq, dz_a, dsm_a, dng, dal, ddt, got_gdn = gdn_bwd(cqkv, proj, w["alog"], w["dtb"], w["ng"], states, dya,
                                                       name=f"{tag}_gdn_bwd", comm=comm_gdn)
    gr["gdn_norm_g"] = dng.reshape(GROUP_HEADS, HEAD_DIM).sum(0)
    gr["gdn_a_log"] = dal[0, A_COL:A_COL + GROUP_HEADS]
    gr["gdn_dt_bias"] = ddt[0, A_COL:A_COL + GROUP_HEADS]
    dgq, gr["gdn_conv_w"], _ = dwconv_bwd(dcq, proj, w["gdn_conv_w"], col0=P_GDN, name=f"{tag}_gdn_conv_bwd")
    dproj = jnp.concatenate([dgq, dz_a, dq_b, dk_b, dv_b, dglu, dq_d, dk_d, dv_d, dsm_a + dsm_f],
                            axis=1).astype(MXU_DT)
    gr["win"] = mm(x1, dproj, mode="tn", name=f"{tag}_dwin", tm=1024, tn=640, tk=512)
    dx1 = mm(dproj, w["win"], mode="nt", add=dz, beta=DN_ALPHA, name=f"{tag}_dx1", tm=1024, tn=1024, tk=640)
    tail = {} if tail is None else dict(tail, comm_dh=tail["comm_dh"](gr))
    dx0, gr["gu1"], gr["d1"], gr["ln_ffn1_g"], gr["ln_ffn1_b"], got_tail = _ffn_bwd(
        dx1, sv["ffn1"], w, 1, f"{tag}_ffn1", **tail)
    return dx0, gr, (got_sb, got_fox, got_gdn), got_tail


SMALL_REPLICATED = ("ln_ffn1_g", "ln_ffn1_b", "gdn_a_log", "gdn_dt_bias", "gdn_norm_g", "fox_b_f", "conf_dw_b",
                    "conf_norm_g", "conf_norm_b", "ln_mix_g", "ln_mix_b", "ln_mem_g", "ln_mem_b", "ln_ffn2_g",
                    "ln_ffn2_b")
SMALL_SHARDED = ("gdn_conv_w", "conf_dw_w")
BIG = ("ffn1_w_gate", "ffn1_w_up", "ffn1_w_down", "w_in", "w_out", "mem_w_q", "mem_w_kv", "mem_w_o",
       "ffn2_w_gate", "ffn2_w_up", "ffn2_w_down")
WEIGHT_ORDER = ("ffn1_w_gate", "ffn1_w_up", "ffn1_w_down", "ln_ffn1_g", "ln_ffn1_b", "w_in", "gdn_conv_w", "gdn_a_log",
                "gdn_dt_bias", "gdn_norm_g", "fox_b_f", "conf_dw_w", "conf_dw_b", "conf_norm_g", "conf_norm_b", "w_out",
                "ln_mix_g", "ln_mix_b", "mem_w_q", "mem_w_kv", "mem_w_o", "ln_mem_g", "ln_mem_b", "ffn2_w_gate",
                "ffn2_w_up", "ffn2_w_down", "ln_ffn2_g", "ln_ffn2_b")


def _step(x, mem, loss_target, wts, ms, vs):
    me = 4 * lax.axis_index("x") + 2 * lax.axis_index("y") + lax.axis_index("c")
    x = x[0]
    mem = mem[0]
    target = loss_target[0]
    rows_s = D_MODEL // N_DEV
    first, rest = ("gu1", "d1", "win"), ("sq", "kv", "gu2", "d2")

    def shards(l):
        c = lambda k: wts[k][l].astype(MXU_DT)
        return dict(gu1=jnp.stack([c("ffn1_w_gate"), c("ffn1_w_up")]), d1=c("ffn1_w_down"),
                    win=_win_to_aligned(wts["w_in"][l]).astype(MXU_DT),
                    sq=jnp.stack([c("w_out"), c("mem_w_q"), c("mem_w_o")]), kv=c("mem_w_kv"),
                    gu2=jnp.stack([c("ffn2_w_gate"), c("ffn2_w_up")]), d2=c("ffn2_w_down"))

    def to_compute_layout(w, keys, got):
        for k, g in zip(keys, got):
            if k in ("gu1", "gu2"):
                w[k] = g.transpose(2, 1, 0, 3).reshape(D_MODEL, 2 * D_FF)
            elif k in ("d1", "d2"):
                w[k] = g.reshape(D_FF, D_MODEL)
            elif k == "win":
                w[k] = g.reshape(D_MODEL, P_WIDTH)
            elif k == "sq":
                full = g.transpose(1, 0, 2, 3).reshape(3, D_MODEL, D_MODEL)
                w["wout"], w["wq"], w["wo"] = full[0], full[1], full[2]
            else:
                w["wkv"] = g.transpose(1, 0, 2).reshape(D_MODEL, 2 * D_MODEL)

    def chunks(gr, keys, dtype=MXU_DT):
        out = []
        for k in keys:
            if k in ("gu1", "gu2"):
                out.append(gr[k].reshape(D_MODEL, 2, N_DEV, -1).transpose(2, 1, 0, 3))
            elif k in ("d1", "d2"):
                out.append(gr[k].reshape(N_DEV, -1, D_MODEL))
            elif k == "win":
                out.append(gr[k].reshape(N_DEV, rows_s, P_WIDTH))
            elif k == "sq":
                out.append(jnp.stack([gr[n].reshape(N_DEV, rows_s, D_MODEL) for n in ("wout", "wq", "wo")], axis=1))
            else:
                out.append(gr["wkv"].reshape(D_MODEL, N_DEV, -1).transpose(1, 0, 2))
        return [a.astype(dtype) for a in out]

    sh = [shards(l) for l in range(DEPTH)]
    sm_sh = _pack([wts["gdn_conv_w"], wts["conf_dw_w"]])
    got = exchange([sh[0]["gu1"], sm_sh], broadcast=True, name="gather_first")
    conv_shapes = [wts["gdn_conv_w"].shape, wts["conf_dw_w"].shape]
    parts = [_unpack(got[-1][j], conv_shapes) for j in range(N_DEV)]
    gconv_full = jnp.concatenate([p[0] for p in parts], axis=-1)
    cconv_full = jnp.concatenate([p[1] for p in parts], axis=-1)

    def small_weights(l):
        w = dict(gdn_conv_w=gconv_full[l], conf_dw_w=cconv_full[l],
                 alog=_row128(wts["gdn_a_log"][l], A_COL), dtb=_row128(wts["gdn_dt_bias"][l], A_COL),
                 bf=_row128(wts["fox_b_f"][l], FOX_COL), ng=jnp.tile(wts["gdn_norm_g"][l], GROUP_HEADS)[None, :])
        for k in ("ln_ffn1_g", "ln_ffn1_b", "conf_dw_b", "conf_norm_g", "conf_norm_b", "ln_mix_g", "ln_mix_b",
                  "ln_mem_g", "ln_mem_b", "ln_ffn2_g", "ln_ffn2_b"):
            w[k] = wts[k][l][None, :]
        return w

    lw = [small_weights(l) for l in range(DEPTH)]
    to_compute_layout(lw[0], ("gu1",), got[:-1])

    def take(l, keys):
        return lambda g: to_compute_layout(lw[l], keys, g)

    def take_fox0(g):
        to_compute_layout(lw[0], rest[2:], g[:2])
        to_compute_layout(lw[1], first[:2], g[2:])

    h, h16, sv0 = _layer_fwd(
        x, x.astype(MXU_DT), mem, lw[0], "l0",
        comm_ffn1=Comm([sh[0][k] for k in first[1:]], True), on_ffn1=take(0, first[1:]),
        comm_gdn=Comm([sh[0][k] for k in rest[:2]], True), on_gdn=take(0, rest[:2]),
        comm_fox=Comm([sh[0][k] for k in rest[2:]] + [sh[1][k] for k in first[:2]], True), on_fox=take_fox0,
        comm_sb=Comm([sh[1]["win"]], True), on_sb=take(1, ("win",)))
    h, _, sv1 = _layer_fwd(
        h, h16, mem, lw[1], "l1",
        comm_gdn=Comm([sh[1][k] for k in rest[:2]], True), on_gdn=take(1, rest[:2]),
        comm_fox=Comm([sh[1][k] for k in rest[2:]], True), on_fox=take(1, rest[2:]))
    dh, lpart = loss_head(h, target, name="loss_head")

    recv = [{}, {}]
    e_ffn, e_mem = ("gu2", "d2"), ("sq", "kv")
    dh, g1, got, _ = _layer_bwd(dh, mem, sv1, lw[1], "l1",
                                lambda gr: (None, Comm(chunks(gr, e_ffn), False), Comm(chunks(gr, e_mem), False)))
    recv[1].update(zip(e_ffn, got[1]))
    recv[1].update(zip(e_mem, got[2]))
    tail = dict(comm_dh=lambda gr: Comm(chunks(gr, ("win",)), False),
                comm_dwgu=lambda dwd: Comm(chunks({"d1": dwd}, ("d1",)), False),
                comm_dx=lambda dwgu: Comm(chunks({"gu1": dwgu}, ("gu1",)), False))
    dh, g0, got, got_t = _layer_bwd(
        dh, mem, sv0, lw[0], "l0",
        lambda gr: (Comm(chunks(gr, e_mem), False), Comm(chunks(g1, first[:2]), False),
                    Comm(chunks(g1, first[2:]) + chunks(gr, e_ffn), False)), tail)
    recv[0].update(zip(e_mem, got[0]))
    recv[1].update(zip(first[:2], got[1]))
    recv[1].update(win=got[2][0])
    recv[0].update(zip(e_ffn, got[2][1:]))
    recv[0].update(win=got_t[0][0], d1=got_t[1][0], gu1=got_t[2][0])
    grad_x = dh[None]
    grads = [g0, g1]

    def gl(k):
        return jnp.stack([grads[l][k] for l in range(DEPTH)])

    small_names = SMALL_REPLICATED + SMALL_SHARDED
    small_grads = [gl(k) for k in small_names] + [lpart[0, :1]]
    got = exchange([_pack(small_grads)], broadcast=True, name="gather_small_grads")
    sm_sum = slot_sum(got[0], name="sum_small_grads")
    sm_g = _unpack(sm_sum, [g.shape for g in small_grads])
    loss = sm_g[-1][0]
    small_g = dict(zip(small_names, sm_g[:-1]))
    for k in SMALL_SHARDED:
        width = wts[k].shape[-1]
        small_g[k] = lax.dynamic_slice_in_dim(small_g[k], me * width, width, axis=2)

    out_g, out_d, out_m, out_v = {}, {}, {}, {}

    def update(names, key, fix=lambda a: a):
        res = {k: [] for k in names}
        for l in range(DEPTH):
            slots = fix(recv[l][key])
            slots = slots.reshape(N_DEV, -1, slots.shape[-1])
            for i, k in enumerate(names):
                two = lambda a: a[l].reshape(-1, a.shape[-1])
                res[k].append(adamw(two(wts[k]), two(ms[k]), two(vs[k]), slots, row0=i * two(wts[k]).shape[0],
                                    name=f"adamw_{k}_l{l}"))
        for k in names:
            for dst, per_layer in zip((out_g, out_d, out_m, out_v), zip(*res[k])):
                dst[k] = jnp.stack(per_layer).reshape(wts[k].shape)

    update(("ffn1_w_gate", "ffn1_w_up"), "gu1")
    update(("ffn1_w_down",), "d1")
    update(("w_in",), "win", _win_from_aligned)
    update(("w_out", "mem_w_q", "mem_w_o"), "sq")
    update(("mem_w_kv",), "kv")
    update(("ffn2_w_gate", "ffn2_w_up"), "gu2")
    update(("ffn2_w_down",), "d2")

    sw = _pack([wts[k] for k in small_names])
    smm = _pack([ms[k] for k in small_names])
    smv = _pack([vs[k] for k in small_names])
    sg = _pack([small_g[k] for k in small_names])
    res = adamw(sw, smm, smv, sg[None], name="adamw_small")
    shapes = [wts[k].shape for k in small_names]
    for dst, buf in zip((out_g, out_d, out_m, out_v), res):
        for k, a in zip(small_names, _unpack(buf, shapes)):
            dst[k] = a

    return (loss, grad_x, *[out_g[k] for k in WEIGHT_ORDER], *[out_d[k] for k in WEIGHT_ORDER],
            *[out_m[k] for k in WEIGHT_ORDER], *[out_v[k] for k in WEIGHT_ORDER])


def kernel(x, mem, ffn1_w_gate, ffn1_w_up, ffn1_w_down, ln_ffn1_g, ln_ffn1_b, w_in, gdn_conv_w, gdn_a_log, gdn_dt_bias, gdn_norm_g, fox_b_f, conf_dw_w, conf_dw_b, conf_norm_g, conf_norm_b, w_out, ln_mix_g, ln_mix_b, mem_w_q, mem_w_kv, mem_w_o, ln_mem_g, ln_mem_b, ffn2_w_gate, ffn2_w_up, ffn2_w_down, ln_ffn2_g, ln_ffn2_b, loss_target, m_ffn1_w_gate, m_ffn1_w_up, m_ffn1_w_down, m_ln_ffn1_g, m_ln_ffn1_b, m_w_in, m_gdn_conv_w, m_gdn_a_log, m_gdn_dt_bias, m_gdn_norm_g, m_fox_b_f, m_conf_dw_w, m_conf_dw_b, m_conf_norm_g, m_conf_norm_b, m_w_out, m_ln_mix_g, m_ln_mix_b, m_mem_w_q, m_mem_w_kv, m_mem_w_o, m_ln_mem_g, m_ln_mem_b, m_ffn2_w_gate, m_ffn2_w_up, m_ffn2_w_down, m_ln_ffn2_g, m_ln_ffn2_b, v_ffn1_w_gate, v_ffn1_w_up, v_ffn1_w_down, v_ln_ffn1_g, v_ln_ffn1_b, v_w_in, v_gdn_conv_w, v_gdn_a_log, v_gdn_dt_bias, v_gdn_norm_g, v_fox_b_f, v_conf_dw_w, v_conf_dw_b, v_conf_norm_g, v_conf_norm_b, v_w_out, v_ln_mix_g, v_ln_mix_b, v_mem_w_q, v_mem_w_kv, v_mem_w_o, v_ln_mem_g, v_ln_mem_b, v_ffn2_w_gate, v_ffn2_w_up, v_ffn2_w_down, v_ln_ffn2_g, v_ln_ffn2_b):
    args = locals()
    wts = {k: args[k] for k in WEIGHT_ORDER}
    ms = {k: args["m_" + k] for k in WEIGHT_ORDER}
    vs = {k: args["v_" + k] for k in WEIGHT_ORDER}
    return _step(x, mem, loss_target, wts, ms, vs)
```
